```python
import jax, jax.numpy as jnp
from jax import lax
import numpy as np

D_MODEL = 1024
BATCH = 16
SEQ = 2048
DEPTH = 1

N_MEM = 256
M_HEADS = 4
M_HEAD_DIM = 128
M_WIDTH = M_HEADS * M_HEAD_DIM
M_CONV = 4
M_CHUNK = 64
A_HEADS = 8
A_NOPE = 64
A_ROPE = 32
A_QK = A_NOPE + A_ROPE
A_VDIM = 64
A_WIDTH = A_HEADS * A_VDIM
A_Q_RANK = 256
A_KV_RANK = 128
ROPE_THETA = 10000.0
Q_BLOCK = 128
X_HEADS = 4
X_HEAD_DIM = D_MODEL // X_HEADS
N_EXPERTS = 32
TOP_K = 4
D_EXPERT = D_MODEL
SWIGLU_ALPHA = 1.702
SWIGLU_LIMIT = 7.0
MOE_BLOCK = 128
DN_ALPHA = (2.0 * DEPTH) ** 0.25
DN_BETA = (8.0 * DEPTH) ** -0.25
EPS = 1e-5
IN_SPLITS = (A_Q_RANK, A_KV_RANK, A_ROPE, M_WIDTH, M_WIDTH, M_WIDTH, M_HEADS, M_HEADS, D_MODEL, D_MODEL)
IN_OFFSETS = tuple(int(v) for v in np.cumsum(IN_SPLITS)[:-1])
D_IN = int(sum(IN_SPLITS))

kernel_name = 'hybrid_mlstm_mla_gated_moe_deepnorm'


def layer_norm(x, g, b):
    xf = x.astype(jnp.float32)
    mu = xf.mean(-1, keepdims=True)
    var = jnp.square(xf - mu).mean(-1, keepdims=True)
    return ((xf - mu) * lax.rsqrt(var + EPS) * g + b).astype(x.dtype)


def rms_norm(x, g):
    xf = x.astype(jnp.float32)
    return (xf * lax.rsqrt(jnp.square(xf).mean(-1, keepdims=True) + EPS) * g).astype(x.dtype)


def rope_cos_sin(positions):
    half = A_ROPE // 2
    inv_freq = ROPE_THETA ** (-jnp.arange(half, dtype=jnp.float32) / half)
    ang = positions.astype(jnp.float32)[..., None] * inv_freq
    return jnp.cos(ang), jnp.sin(ang)


def apply_rope(x, cos, sin):
    x1, x2 = jnp.split(x.astype(jnp.float32), 2, axis=-1)
    return jnp.concatenate([x1 * cos - x2 * sin, x2 * cos + x1 * sin], axis=-1).astype(x.dtype)


def mlstm_chunkwise(q, k, v, i_pre, f_pre):
    bsz, s, nh, d = q.shape
    nc = s // M_CHUNK
    f32 = jnp.float32

    def to_chunks(t):
        return t.astype(f32).reshape(bsz, nc, M_CHUNK, nh, -1).transpose(0, 3, 1, 2, 4)

    q, v = to_chunks(q), to_chunks(v)
    k = to_chunks(k) * d ** -0.5
    ig = to_chunks(i_pre[..., None])[..., 0]
    lf = jax.nn.log_sigmoid(to_chunks(f_pre[..., None])[..., 0])
    b = jnp.cumsum(lf, axis=-1)
    b_tot = b[..., -1]
    causal = jnp.tril(jnp.ones((M_CHUNK, M_CHUNK), dtype=bool))
    log_d = jnp.where(causal, b[..., :, None] - b[..., None, :] + ig[..., None, :], -jnp.inf)
    m_intra = log_d.max(-1)

    a = b_tot[..., None] - b + ig
    m_chunk = a.max(-1)
    wa = jnp.exp(a - m_chunk[..., None])
    g_c = jnp.einsum('bhcl,bhclv,bhclk->bhcvk', wa, v, k)
    n_c = jnp.einsum('bhcl,bhclk->bhck', wa, k)

    def step(carry, xs):
        c_st, n_st, m_st = carry
        bt, mc, gc, nvec = xs
        m_new = jnp.maximum(bt + m_st, mc)
        s_old = jnp.exp(bt + m_st - m_new)
        s_new = jnp.exp(mc - m_new)
        c_new = s_old[..., None, None] * c_st + s_new[..., None, None] * gc
        n_new = s_old[..., None] * n_st + s_new[..., None] * nvec
        return (c_new, n_new, m_new), (c_st, n_st, m_st)

    init = (jnp.zeros((bsz, nh, d, d), f32), jnp.zeros((bsz, nh, d), f32), jnp.zeros((bsz, nh), f32))
    xs = (jnp.moveaxis(b_tot, 2, 0), jnp.moveaxis(m_chunk, 2, 0), jnp.moveaxis(g_c, 2, 0), jnp.moveaxis(n_c, 2, 0))
    _, (c_prev, n_prev, m_prev) = lax.scan(step, init, xs)
    c_prev = jnp.moveaxis(c_prev, 0, 2)
    n_prev = jnp.moveaxis(n_prev, 0, 2)
    m_prev = jnp.moveaxis(m_prev, 0, 2)

    g_inter = b + m_prev[..., None]
    m = jnp.maximum(g_inter, m_intra)
    w_inter = jnp.exp(g_inter - m)
    scores = jnp.einsum('bhcjd,bhcsd->bhcjs', q, k) * jnp.exp(log_d - m[..., None])
    num = (w_inter[..., None] * jnp.einsum('bhcvk,bhcjk->bhcjv', c_prev, q)
           + jnp.einsum('bhcjs,bhcsv->bhcjv', scores, v))
    den = w_inter * jnp.einsum('bhck,bhcjk->bhcj', n_prev, q) + scores.sum(-1)
    h = num / jnp.maximum(jnp.abs(den), jnp.exp(-m))[..., None]
    return h.transpose(0, 2, 3, 1, 4).reshape(bsz, s, nh, d)


def mlstm_branch(xm, vm, o_pre, i_pre, f_pre, b_i, b_f, conv_w, conv_b, w_mq, w_mk, g_mhead, w_mskip):
    bsz, s, _ = xm.shape
    xpad = jnp.pad(xm, ((0, 0), (M_CONV - 1, 0), (0, 0)))
    xc = jax.nn.silu(sum(xpad[:, j:j + s] * conv_w[j] for j in range(M_CONV)) + conv_b)
    xch = xc.reshape(bsz, s, M_HEADS, M_HEAD_DIM)
    q = jnp.einsum('bshd,hde->bshe', xch, w_mq)
    k = jnp.einsum('bshd,hde->bshe', xch, w_mk)
    v = vm.reshape(bsz, s, M_HEADS, M_HEAD_DIM)
    h = mlstm_chunkwise(q, k, v, i_pre + b_i, f_pre + b_f)
    mu = h.mean(-1, keepdims=True)
    var = jnp.square(h - mu).mean(-1, keepdims=True)
    hn = (((h - mu) * lax.rsqrt(var + EPS)).reshape(bsz, s, M_WIDTH) * g_mhead).astype(xm.dtype)
    return jax.nn.sigmoid(o_pre) * (hn + w_mskip * xc)


def causal_block_attention(q, k, v, scale):
    bsz, s, nh, dq = q.shape
    nq = s // Q_BLOCK
    qb = q.reshape(bsz, nq, Q_BLOCK, nh, dq).swapaxes(0, 1)
    key_idx = jnp.arange(s)

    def one_block(args):
        qi, blk = args
        sc = jnp.einsum('bqhd,bkhd->bhqk', qi, k).astype(jnp.float32) * scale
        q_idx = blk * Q_BLOCK + jnp.arange(Q_BLOCK)
        sc = jnp.where(key_idx[None, :] <= q_idx[:, None], sc, -jnp.inf)
        p = jax.nn.softmax(sc, axis=-1).astype(v.dtype)
        return jnp.einsum('bhqk,bkhd->bqhd', p, v)

    o = lax.map(one_block, (qb, jnp.arange(nq)))
    return o.swapaxes(0, 1).reshape(bsz, s, nh * v.shape[-1])


def mla_branch(q_lat, kv_lat, k_rope, positions, g_qlat, g_kvlat, w_uq, w_ukv):
    bsz, s, _ = q_lat.shape
    q = (rms_norm(q_lat, g_qlat) @ w_uq).reshape(bsz, s, A_HEADS, A_QK)
    kv = (rms_norm(kv_lat, g_kvlat) @ w_ukv).reshape(bsz, s, A_HEADS, A_NOPE + A_VDIM)
    q_nope, q_pe = q[..., :A_NOPE], q[..., A_NOPE:]
    k_nope, v = kv[..., :A_NOPE], kv[..., A_NOPE:]
    cos, sin = rope_cos_sin(positions)
    q_pe = apply_rope(q_pe, cos[:, :, None, :], sin[:, :, None, :])
    k_pe = apply_rope(k_rope, cos, sin)
    q = jnp.concatenate([q_nope, q_pe], axis=-1)
    k = jnp.concatenate([k_nope, jnp.broadcast_to(k_pe[:, :, None, :], (bsz, s, A_HEADS, A_ROPE))], axis=-1)
    return causal_block_attention(q, k, v, A_QK ** -0.5)


def memory_cross_attention(x, mem, w_cq, w_ckv, w_co):
    bsz, s, _ = x.shape
    q = (x @ w_cq).reshape(bsz, s, X_HEADS, X_HEAD_DIM)
    kv = (mem @ w_ckv).reshape(bsz, mem.shape[1], 2, X_HEADS, X_HEAD_DIM)
    k, v = kv[:, :, 0], kv[:, :, 1]
    sc = jnp.einsum('bshd,bmhd->bhsm', q, k).astype(jnp.float32) * X_HEAD_DIM ** -0.5
    p = jax.nn.softmax(sc, axis=-1).astype(v.dtype)
    o = jnp.einsum('bhsm,bmhd->bshd', p, v).reshape(bsz, s, D_MODEL)
    return o @ w_co


def clamped_swiglu(gu):
    x_glu, x_lin = gu[..., ::2], gu[..., 1::2]
    x_glu = jnp.minimum(x_glu, SWIGLU_LIMIT)
    x_lin = jnp.clip(x_lin, -SWIGLU_LIMIT, SWIGLU_LIMIT)
    return x_glu * jax.nn.sigmoid(SWIGLU_ALPHA * x_glu) * (x_lin + 1.0)


def moe_ffn(x, w_router, b_router, w_gu, b_gu, w_dn, b_dn):
    bsz, s, d = x.shape
    t = bsz * s
    xt = x.reshape(t, d)
    logits = (xt @ w_router + b_router).astype(jnp.float32)
    top_val, top_idx = lax.top_k(logits, TOP_K)
    gate = jax.nn.softmax(top_val, axis=-1)
    n_assign = t * TOP_K
    e_flat = top_idx.reshape(-1).astype(jnp.int32)
    tok_flat = jnp.arange(n_assign, dtype=jnp.int32) // TOP_K
    w_flat = gate.reshape(-1)
    order = jnp.argsort(e_flat)
    e_sorted, tok_sorted, w_sorted = e_flat[order], tok_flat[order], w_flat[order]
    counts = jnp.zeros((N_EXPERTS,), jnp.int32).at[e_flat].add(1)
    padded = (counts + MOE_BLOCK - 1) // MOE_BLOCK * MOE_BLOCK
    starts = jnp.cumsum(counts) - counts
    pends = jnp.cumsum(padded)
    pstarts = pends - padded
    dest = pstarts[e_sorted] + (jnp.arange(n_assign, dtype=jnp.int32) - starts[e_sorted])
    n_rows = n_assign + N_EXPERTS * MOE_BLOCK
    n_blocks = n_rows // MOE_BLOCK
    buf_tok = jnp.full((n_rows,), t, jnp.int32).at[dest].set(tok_sorted)
    buf_w = jnp.zeros((n_rows,), jnp.float32).at[dest].set(w_sorted)
    blk_e = jnp.minimum(jnp.searchsorted(pends, jnp.arange(n_blocks, dtype=jnp.int32) * MOE_BLOCK, side='right'),
                        N_EXPERTS - 1)
    x_pad = jnp.concatenate([xt, jnp.zeros((1, d), xt.dtype)], axis=0)
    xb = x_pad[buf_tok].reshape(n_blocks, MOE_BLOCK, d)

    def expert_block(args):
        xi, e = args
        return clamped_swiglu(xi @ w_gu[e] + b_gu[e]) @ w_dn[e] + b_dn[e]

    yb = lax.map(expert_block, (xb, blk_e)).reshape(n_rows, d)
    y = jnp.zeros((t + 1, d), x.dtype).at[buf_tok].add(yb * buf_w[:, None].astype(yb.dtype))
    return y[:t].reshape(bsz, s, d)


def setup_inputs(seed: int = 0) -> dict:
    key = jax.random.key(seed)
    ks = iter(jax.random.split(key, 48))
    L, D = DEPTH, D_MODEL

    def nrm(shape, scale):
        return scale * jax.random.normal(next(ks), shape, jnp.float32)

    def gain(shape):
        return 1.0 + nrm(shape, 0.02)

    offsets = jax.random.randint(next(ks), (BATCH, 1), 0, 4096, dtype=jnp.int32)
    return {
        'x': nrm((BATCH, SEQ, D), 1.0),
        'mem': nrm((BATCH, N_MEM, D), 1.0),
        'positions': (offsets + jnp.arange(SEQ, dtype=jnp.int32)[None, :]).astype(jnp.int32),
        'w_in': nrm((L, D, D_IN), D ** -0.5),
        'b_igate': nrm((L, M_HEADS), 0.1),
        'b_fgate': jnp.linspace(3.0, 6.0, M_HEADS, dtype=jnp.float32)[None, :] + nrm((L, M_HEADS), 0.1),
        'conv_w': nrm((L, M_CONV, M_WIDTH), M_CONV ** -0.5),
        'conv_b': nrm((L, M_WIDTH), 0.01),
        'w_mq': nrm((L, M_HEADS, M_HEAD_DIM, M_HEAD_DIM), M_HEAD_DIM ** -0.5),
        'w_mk': nrm((L, M_HEADS, M_HEAD_DIM, M_HEAD_DIM), M_HEAD_DIM ** -0.5),
        'g_mhead': gain((L, M_WIDTH)),
        'w_mskip': gain((L, M_WIDTH)),
        'g_qlat': gain((L, A_Q_RANK)),
        'g_kvlat': gain((L, A_KV_RANK)),
        'w_uq': nrm((L, A_Q_RANK, A_HEADS * A_QK), A_Q_RANK ** -0.5),
        'w_ukv': nrm((L, A_KV_RANK, A_HEADS * (A_NOPE + A_VDIM)), A_KV_RANK ** -0.5),
        'w_br_m': nrm((L, M_WIDTH, D), M_WIDTH ** -0.5),
        'w_br_a': nrm((L, A_WIDTH, D), A_WIDTH ** -0.5),
        'w_mix_out': nrm((L, D, D), DN_BETA * D ** -0.5),
        'ln1_g': gain((L, D)),
        'ln1_b': nrm((L, D), 0.01),
        'w_cq': nrm((L, D, D), D ** -0.5),
        'w_ckv': nrm((L, D, 2 * D), D ** -0.5),
        'w_co': nrm((L, D, D), DN_BETA * D ** -0.5),
        'ln2_g': gain((L, D)),
        'ln2_b': nrm((L, D), 0.01),
        'w_router': nrm((L, D, N_EXPERTS), D ** -0.5),
        'b_router': nrm((L, N_EXPERTS), 0.01),
        'w_gu': nrm((L, N_EXPERTS, D, 2 * D_EXPERT), D ** -0.5),
        'b_gu': nrm((L, N_EXPERTS, 2 * D_EXPERT), 0.01),
        'w_dn': nrm((L, N_EXPERTS, D_EXPERT, D), DN_BETA * D_EXPERT ** -0.5),
        'b_dn': nrm((L, N_EXPERTS, D), 0.01),
        'ln3_g': gain((L, D)),
        'ln3_b': nrm((L, D), 0.01),
    }


def reference(x, mem, positions, w_in, b_igate, b_fgate, conv_w, conv_b, w_mq, w_mk, g_mhead, w_mskip,
              g_qlat, g_kvlat, w_uq, w_ukv, w_br_m, w_br_a, w_mix_out, ln1_g, ln1_b,
              w_cq, w_ckv, w_co, ln2_g, ln2_b, w_router, b_router, w_gu, b_gu, w_dn, b_dn, ln3_g, ln3_b):
    h = x
    for l in range(DEPTH):
        proj = h @ w_in[l]
        q_lat, kv_lat, k_rope, xm, vm, o_pre, i_pre, f_pre, g_m, g_a = jnp.split(proj, IN_OFFSETS, axis=-1)
        y_m = mlstm_branch(xm, vm, o_pre, i_pre, f_pre, b_igate[l], b_fgate[l], conv_w[l], conv_b[l],
                           w_mq[l], w_mk[l], g_mhead[l], w_mskip[l]) @ w_br_m[l]
        y_a = mla_branch(q_lat, kv_lat, k_rope, positions, g_qlat[l], g_kvlat[l], w_uq[l], w_ukv[l]) @ w_br_a[l]
        mix = (jax.nn.sigmoid(g_m) * y_m + jax.nn.sigmoid(g_a) * y_a) @ w_mix_out[l]
        h = layer_norm(DN_ALPHA * h + mix, ln1_g[l], ln1_b[l])
        h = layer_norm(DN_ALPHA * h + memory_cross_attention(h, mem, w_cq[l], w_ckv[l], w_co[l]), ln2_g[l], ln2_b[l])
        h = layer_norm(DN_ALPHA * h + moe_ffn(h, w_router[l], b_router[l], w_gu[l], b_gu[l], w_dn[l], b_dn[l]),
                       ln3_g[l], ln3_b[l])
    return h
```

```python
import functools

import numpy as np
import jax
import jax.numpy as jnp
from jax import lax
from jax.experimental import pallas as pl
from jax.experimental.pallas import tpu as pltpu

F32 = jnp.float32
BF16 = jnp.bfloat16
I32 = jnp.int32

D_MODEL = 1024
N_MEM = 256
M_HEADS = 4
M_HEAD_DIM = 128
M_WIDTH = M_HEADS * M_HEAD_DIM
M_CONV = 4
A_HEADS = 8
A_NOPE = 64
A_ROPE = 32
A_QK = A_NOPE + A_ROPE
A_VDIM = 64
A_Q_RANK = 256
A_KV_RANK = 128
ROPE_THETA = 10000.0
X_HEADS = 4
X_HEAD_DIM = D_MODEL // X_HEADS
N_EXPERTS = 32
TOP_K = 4
D_EXPERT = D_MODEL
SWIGLU_ALPHA = 1.702
SWIGLU_LIMIT = 7.0
DEPTH = 1
DN_ALPHA = (2.0 * DEPTH) ** 0.25
EPS = 1e-5
IN_SPLITS = (A_Q_RANK, A_KV_RANK, A_ROPE, M_WIDTH, M_WIDTH, M_WIDTH, M_HEADS, M_HEADS, D_MODEL, D_MODEL)
IN_OFFSETS = tuple(int(v) for v in np.cumsum((0,) + IN_SPLITS))

LANES = 128
VMEM_LIMIT = 56 * 1024 * 1024

PROJ_ROWS = 512
M_CHUNK_ROWS = 128
ATT_Q = 256
ATT_K = 256
MERGE_ROWS = 512
XATT_ROWS = 512
RANK_ROWS = 512
EXPERT_ROWS = 256
DISPATCH_TOKENS = 256
COMBINE_TOKENS = 64
NEG_BIG = -1e30

C_QLAT = 0
C_KVLAT = C_QLAT + A_Q_RANK
C_KR = C_KVLAT + A_KV_RANK
C_KRS = C_KR + LANES
C_GATE = C_KRS + LANES
C_XM = C_GATE + LANES
C_VM = C_XM + M_WIDTH
C_OP = C_VM + M_WIDTH
C_GM = C_OP + M_WIDTH
C_GA = C_GM + D_MODEL
C_END = C_GA + D_MODEL


def _cparams(sem, vmem=VMEM_LIMIT):
    return pltpu.CompilerParams(dimension_semantics=sem, vmem_limit_bytes=vmem)


def _const_spec(shape):
    nd = len(shape)
    return pl.BlockSpec(shape, lambda *a: (0,) * nd)


def _layer_norm_rows(v, g, b):
    mu = jnp.mean(v, axis=-1, keepdims=True)
    d = v - mu
    var = jnp.mean(d * d, axis=-1, keepdims=True)
    return d * lax.rsqrt(var + EPS) * g + b


def _proj_kernel(x_ref, pos_ref, w_ref, gq_ref, gkv_ref, wuq_ref, wuqs_ref, wuk_ref, wuv_ref, invf_ref,
                 q_ref, k_ref, v_ref, xm_ref, vm_ref, op_ref, gm_ref, ga_ref, gate_ref):
    xb = x_ref[...].astype(BF16)

    def mm(lo, n):
        return jnp.dot(xb, w_ref[:, lo:lo + n], preferred_element_type=F32)

    xm_ref[...] = mm(C_XM, M_WIDTH).astype(BF16)
    vm_ref[...] = mm(C_VM, M_WIDTH).astype(BF16)
    op_ref[...] = mm(C_OP, M_WIDTH).astype(BF16)
    gm_ref[...] = mm(C_GM, D_MODEL).astype(BF16)
    ga_ref[...] = mm(C_GA, D_MODEL).astype(BF16)
    gate_ref[...] = mm(C_GATE, LANES)

    ang = pos_ref[...].astype(F32) * invf_ref[...]
    cos = jnp.cos(ang)
    sin = jnp.sin(ang)

    q_lat = mm(C_QLAT, A_Q_RANK)
    qn = (q_lat * lax.rsqrt(jnp.mean(q_lat * q_lat, axis=-1, keepdims=True) + EPS) * gq_ref[...]).astype(BF16)
    q = jnp.dot(qn, wuq_ref[...], preferred_element_type=F32)
    qs = jnp.dot(qn, wuqs_ref[...], preferred_element_type=F32)
    kv_lat = mm(C_KVLAT, A_KV_RANK)
    kvn = (kv_lat * lax.rsqrt(jnp.mean(kv_lat * kv_lat, axis=-1, keepdims=True) + EPS) * gkv_ref[...]).astype(BF16)
    kn = jnp.dot(kvn, wuk_ref[...], preferred_element_type=F32)
    v_ref[...] = jnp.dot(kvn, wuv_ref[...], preferred_element_type=F32).astype(BF16)
    k_pe = mm(C_KR, LANES) * cos + mm(C_KRS, LANES) * sin
    scale = A_QK ** -0.5
    for h in range(A_HEADS):
        sl = slice(h * LANES, (h + 1) * LANES)
        q_ref[:, sl] = ((q[:, sl] * cos + qs[:, sl] * sin) * scale).astype(BF16)
        k_ref[:, sl] = (kn[:, sl] + k_pe).astype(BF16)


def _proj_weights(w_in, w_uq, w_ukv):
    o = IN_OFFSETS
    half = A_ROPE // 2
    w_q, w_kv, w_kr = w_in[:, o[0]:o[1]], w_in[:, o[1]:o[2]], w_in[:, o[2]:o[3]]
    w_xm, w_vm, w_op = w_in[:, o[3]:o[4]], w_in[:, o[4]:o[5]], w_in[:, o[5]:o[6]]
    w_i, w_f, w_gm, w_ga = w_in[:, o[6]:o[7]], w_in[:, o[7]:o[8]], w_in[:, o[8]:o[9]], w_in[:, o[9]:o[10]]
    d = w_in.shape[0]
    z = lambda n: jnp.zeros((d, n), w_in.dtype)
    kr = jnp.concatenate([z(A_NOPE), w_kr, z(LANES - A_QK)], axis=1)
    krs = jnp.concatenate([z(A_NOPE), -w_kr[:, half:], w_kr[:, :half], z(LANES - A_QK)], axis=1)
    gate = jnp.concatenate([w_i, w_f, z(LANES - 2 * M_HEADS)], axis=1)
    w_r = jnp.concatenate([w_q, w_kv, kr, krs, gate, w_xm, w_vm, w_op, w_gm, w_ga], axis=1).astype(BF16)

    uq = w_uq.reshape(A_Q_RANK, A_HEADS, A_QK)
    zq = jnp.zeros((A_Q_RANK, A_HEADS, LANES - A_QK), w_uq.dtype)
    zn = jnp.zeros((A_Q_RANK, A_HEADS, A_NOPE), w_uq.dtype)
    uq_pad = jnp.concatenate([uq, zq], axis=-1).reshape(A_Q_RANK, A_HEADS * LANES).astype(BF16)
    uqs_pad = jnp.concatenate([zn, -uq[..., A_NOPE + half:], uq[..., A_NOPE:A_NOPE + half], zq],
                              axis=-1).reshape(A_Q_RANK, A_HEADS * LANES).astype(BF16)
    ukv = w_ukv.reshape(A_KV_RANK, A_HEADS, A_NOPE + A_VDIM)
    zk = jnp.zeros((A_KV_RANK, A_HEADS, LANES - A_NOPE), w_ukv.dtype)
    uk_pad = jnp.concatenate([ukv[..., :A_NOPE], zk], axis=-1).reshape(A_KV_RANK, A_HEADS * LANES).astype(BF16)
    zv = jnp.zeros((A_KV_RANK, A_HEADS, LANES - A_VDIM), w_ukv.dtype)
    uv_pad = jnp.concatenate([ukv[..., A_NOPE:], zv], axis=-1).reshape(A_KV_RANK, A_HEADS * LANES).astype(BF16)
    return w_r, uq_pad, uqs_pad, uk_pad, uv_pad


def _input_projection(x2, pos2, w_in, g_qlat, g_kvlat, w_uq, w_ukv):
    t = x2.shape[0]
    tm = min(PROJ_ROWS, t)
    w_r, uq_pad, uqs_pad, uk_pad, uv_pad = _proj_weights(w_in, w_uq, w_ukv)
    half = A_ROPE // 2
    inv_freq = ROPE_THETA ** (-jnp.arange(half, dtype=F32) / half)
    invf = jnp.concatenate([jnp.zeros((A_NOPE,), F32), inv_freq, inv_freq,
                            jnp.zeros((LANES - A_QK,), F32)]).reshape(1, LANES)
    hw = A_HEADS * LANES
    row = lambda n: pl.BlockSpec((tm, n), lambda i: (i, 0))
    outs = [
        jax.ShapeDtypeStruct((t, hw), BF16), jax.ShapeDtypeStruct((t, hw), BF16), jax.ShapeDtypeStruct((t, hw), BF16),
        jax.ShapeDtypeStruct((t, M_WIDTH), BF16), jax.ShapeDtypeStruct((t, M_WIDTH), BF16),
        jax.ShapeDtypeStruct((t, M_WIDTH), BF16),
        jax.ShapeDtypeStruct((t, D_MODEL), BF16), jax.ShapeDtypeStruct((t, D_MODEL), BF16),
        jax.ShapeDtypeStruct((t, LANES), F32),
    ]
    return pl.pallas_call(
        _proj_kernel,
        out_shape=outs,
        grid=(t // tm,),
        in_specs=[row(D_MODEL), row(1), _const_spec(w_r.shape), _const_spec((1, A_Q_RANK)),
                  _const_spec((1, A_KV_RANK)), _const_spec(uq_pad.shape), _const_spec(uqs_pad.shape),
                  _const_spec(uk_pad.shape), _const_spec(uv_pad.shape), _const_spec((1, LANES))],
        out_specs=[row(hw), row(hw), row(hw), row(M_WIDTH), row(M_WIDTH), row(M_WIDTH), row(D_MODEL), row(D_MODEL),
                   row(LANES)],
        compiler_params=_cparams(("parallel",)),
        name="input_projection",
    )(x2, pos2, w_r, g_qlat.reshape(1, -1), g_kvlat.reshape(1, -1), uq_pad, uqs_pad, uk_pad, uv_pad, invf)


def _log_sigmoid(v):
    return jnp.minimum(v, 0.0) - jnp.log1p(jnp.exp(-jnp.abs(v)))


def _mlstm_kernel(xm_ref, vm_ref, op_ref, gate_ref, convw_ref, convb_ref, wq_ref, wk_ref, gbias_ref, ghead_ref,
                  skip_ref, tril_ref, out_ref, xpad_s, ct_s, n_s, m_s):
    s = xm_ref.shape[0]
    lc = M_CHUNK_ROWS
    halo = 8
    xpad_s[0:halo, :] = jnp.zeros((halo, M_WIDTH), F32)
    xpad_s[halo:, :] = xm_ref[...].astype(F32)
    ct_s[...] = jnp.zeros_like(ct_s)
    n_s[...] = jnp.zeros_like(n_s)
    m_s[...] = jnp.zeros_like(m_s)
    rows = lax.broadcasted_iota(I32, (lc, lc), 0)
    cols = lax.broadcasted_iota(I32, (lc, lc), 1)
    causal = rows >= cols
    kscale = M_HEAD_DIM ** -0.5

    def chunk(c, carry):
        r0 = pl.multiple_of(c * lc, lc)
        win = xpad_s[pl.ds(r0, lc + halo), :]
        conv = convb_ref[...]
        for j in range(M_CONV):
            lo = halo - (M_CONV - 1) + j
            conv = conv + win[lo:lo + lc, :] * convw_ref[j:j + 1, :]
        xc = conv * jax.nn.sigmoid(conv)
        g = gate_ref[pl.ds(r0, lc), :] + gbias_ref[...]
        ls = _log_sigmoid(g)
        bc = jnp.dot(tril_ref[...], ls, preferred_element_type=F32, precision=lax.Precision.HIGHEST)
        g_t = g.T
        b_t = bc.T
        for h in range(M_HEADS):
            hs = slice(h * M_HEAD_DIM, (h + 1) * M_HEAD_DIM)
            ig_col = g[:, h:h + 1]
            ig_row = g_t[h:h + 1, :]
            b_col = bc[:, M_HEADS + h:M_HEADS + h + 1]
            b_row = b_t[M_HEADS + h:M_HEADS + h + 1, :]
            b_tot = bc[lc - 1:lc, M_HEADS + h:M_HEADS + h + 1]
            m_prev = m_s[h:h + 1, 0:1]
            n_prev = n_s[h:h + 1, :]
            ct_prev = ct_s[h]

            xh = xc[:, hs]
            xhb = xh.astype(BF16)
            q = jnp.dot(xhb, wq_ref[h], preferred_element_type=F32)
            k = jnp.dot(xhb, wk_ref[h], preferred_element_type=F32) * kscale
            qb = q.astype(BF16)
            kb = k.astype(BF16)
            v = vm_ref[pl.ds(r0, lc), hs]

            log_d = jnp.where(causal, b_col - b_row + ig_row, -jnp.inf)
            m_intra = jnp.max(log_d, axis=1, keepdims=True)
            g_inter = b_col + m_prev
            m = jnp.maximum(g_inter, m_intra)
            w_inter = jnp.exp(g_inter - m)
            dmat = jnp.exp(log_d - m)
            sc = lax.dot_general(qb, kb, (((1,), (1,)), ((), ())), preferred_element_type=F32) * dmat
            num = (w_inter * jnp.dot(qb, ct_prev.astype(BF16), preferred_element_type=F32)
                   + jnp.dot(sc.astype(BF16), v, preferred_element_type=F32))
            den = w_inter * jnp.sum(q * n_prev, axis=1, keepdims=True) + jnp.sum(sc, axis=1, keepdims=True)
            hh = num / jnp.maximum(jnp.abs(den), jnp.exp(-m))

            a = b_tot - b_col + ig_col
            m_chunk = jnp.max(a, axis=0, keepdims=True)
            kw = k * jnp.exp(a - m_chunk)
            ct_c = lax.dot_general(kw.astype(BF16), v, (((0,), (0,)), ((), ())), preferred_element_type=F32)
            n_c = jnp.sum(kw, axis=0, keepdims=True)
            m_new = jnp.maximum(b_tot + m_prev, m_chunk)
            s_old = jnp.exp(b_tot + m_prev - m_new)
            s_new = jnp.exp(m_chunk - m_new)
            ct_s[h] = s_old * ct_prev + s_new * ct_c
            n_s[h:h + 1, :] = s_old * n_prev + s_new * n_c
            m_s[h:h + 1, :] = jnp.broadcast_to(m_new, (1, LANES))

            mu = jnp.mean(hh, axis=-1, keepdims=True)
            dv = hh - mu
            var = jnp.mean(dv * dv, axis=-1, keepdims=True)
            hn = dv * lax.rsqrt(var + EPS) * ghead_ref[:, hs]
            y = jax.nn.sigmoid(op_ref[pl.ds(r0, lc), hs].astype(F32)) * (hn + skip_ref[:, hs] * xh)
            out_ref[pl.ds(r0, lc), hs] = y.astype(BF16)
        return carry

    lax.fori_loop(0, s // lc, chunk, 0)


def _mlstm_branch(xm, vm, op, gate, bsz, s, b_i, b_f, conv_w, conv_b, w_mq, w_mk, g_mhead, w_mskip):
    lc = M_CHUNK_ROWS
    gbias = jnp.concatenate([b_i, b_f, jnp.zeros((LANES - 2 * M_HEADS,), F32)]).reshape(1, LANES)
    tril = jnp.tril(jnp.ones((lc, lc), F32))
    seq = lambda n: pl.BlockSpec((s, n), lambda b: (b, 0))
    return pl.pallas_call(
        _mlstm_kernel,
        out_shape=jax.ShapeDtypeStruct((bsz * s, M_WIDTH), BF16),
        grid=(bsz,),
        in_specs=[seq(M_WIDTH), seq(M_WIDTH), seq(M_WIDTH), seq(LANES),
                  _const_spec((M_CONV, M_WIDTH)), _const_spec((1, M_WIDTH)),
                  _const_spec((M_HEADS, M_HEAD_DIM, M_HEAD_DIM)), _const_spec((M_HEADS, M_HEAD_DIM, M_HEAD_DIM)),
                  _const_spec((1, LANES)), _const_spec((1, M_WIDTH)), _const_spec((1, M_WIDTH)),
                  _const_spec((lc, lc))],
        out_specs=seq(M_WIDTH),
        scratch_shapes=[pltpu.VMEM((s + 8, M_WIDTH), F32), pltpu.VMEM((M_HEADS, M_HEAD_DIM, M_HEAD_DIM), F32),
                        pltpu.VMEM((8, LANES), F32), pltpu.VMEM((8, LANES), F32)],
        compiler_params=_cparams(("parallel",)),
        name="mlstm_branch",
    )(xm, vm, op, gate, conv_w, conv_b.reshape(1, -1), w_mq.astype(BF16), w_mk.astype(BF16), gbias,
      g_mhead.reshape(1, -1), w_mskip.reshape(1, -1), tril)


def _attn_kernel(q_ref, k_ref, v_ref, o_ref):
    tq = q_ref.shape[0]
    tk = ATT_K
    i = pl.program_id(2)
    q = q_ref[...]
    rows = lax.broadcasted_iota(I32, (tq, tk), 0) + i * tq
    cols = lax.broadcasted_iota(I32, (tq, tk), 1)

    def body(j, carry):
        m, l, acc = carry
        k0 = pl.multiple_of(j * tk, tk)
        kj = k_ref[pl.ds(k0, tk), :]
        vj = v_ref[pl.ds(k0, tk), :]
        sc = lax.dot_general(q, kj, (((1,), (1,)), ((), ())), preferred_element_type=F32)
        sc = jnp.where(cols + k0 <= rows, sc, NEG_BIG)
        m_new = jnp.maximum(m, jnp.max(sc, axis=1, keepdims=True))
        p = jnp.exp(sc - m_new)
        alpha = jnp.exp(m - m_new)
        l = alpha * l + jnp.sum(p, axis=1, keepdims=True)
        acc = alpha * acc + jnp.dot(p.astype(BF16), vj, preferred_element_type=F32)
        return m_new, l, acc

    n_kv = ((i + 1) * tq + tk - 1) // tk
    init = (jnp.full((tq, 1), NEG_BIG, F32), jnp.zeros((tq, 1), F32), jnp.zeros((tq, LANES), F32))
    m, l, acc = lax.fori_loop(0, n_kv, body, init)
    o_ref[...] = (acc / l).astype(BF16)


def _mla_attention(q, k, v, bsz, s):
    tq = min(ATT_Q, s)
    nq = s // tq
    return pl.pallas_call(
        _attn_kernel,
        out_shape=jax.ShapeDtypeStruct(q.shape, BF16),
        grid=(bsz, A_HEADS, nq),
        in_specs=[pl.BlockSpec((tq, LANES), lambda b, h, i: (b * nq + i, h)),
                  pl.BlockSpec((s, LANES), lambda b, h, i: (b, h)),
                  pl.BlockSpec((s, LANES), lambda b, h, i: (b, h))],
        out_specs=pl.BlockSpec((tq, LANES), lambda b, h, i: (b * nq + i, h)),
        compiler_params=_cparams(("parallel", "parallel", "arbitrary")),
        name="mla_attention",
    )(q, k, v)


def _merge_kernel(ym_ref, oa_ref, gm_ref, ga_ref, x_ref, wbm_ref, wba_ref, wmix_ref, g_ref, b_ref, h_ref):
    y_m = jnp.dot(ym_ref[...], wbm_ref[...], preferred_element_type=F32)
    y_a = jnp.dot(oa_ref[...], wba_ref[...], preferred_element_type=F32)
    mixed = jax.nn.sigmoid(gm_ref[...].astype(F32)) * y_m + jax.nn.sigmoid(ga_ref[...].astype(F32)) * y_a
    mix = jnp.dot(mixed.astype(BF16), wmix_ref[...], preferred_element_type=F32)
    h_ref[...] = _layer_norm_rows(DN_ALPHA * x_ref[...] + mix, g_ref[...], b_ref[...])


def _merge(ym, oa, gm, ga, x2, w_br_m, w_br_a, w_mix_out, ln_g, ln_b):
    t = x2.shape[0]
    tm = min(MERGE_ROWS, t)
    wba = jnp.concatenate([w_br_a.reshape(A_HEADS, A_VDIM, D_MODEL),
                           jnp.zeros((A_HEADS, LANES - A_VDIM, D_MODEL), w_br_a.dtype)], axis=1)
    wba = wba.reshape(A_HEADS * LANES, D_MODEL).astype(BF16)
    row = lambda n: pl.BlockSpec((tm, n), lambda i: (i, 0))
    return pl.pallas_call(
        _merge_kernel,
        out_shape=jax.ShapeDtypeStruct((t, D_MODEL), F32),
        grid=(t // tm,),
        in_specs=[row(M_WIDTH), row(A_HEADS * LANES), row(D_MODEL), row(D_MODEL), row(D_MODEL),
                  _const_spec((M_WIDTH, D_MODEL)), _const_spec((A_HEADS * LANES, D_MODEL)),
                  _const_spec((D_MODEL, D_MODEL)), _const_spec((1, D_MODEL)), _const_spec((1, D_MODEL))],
        out_specs=row(D_MODEL),
        compiler_params=_cparams(("parallel",)),
        name="merge_deepnorm1",
    )(ym, oa, gm, ga, x2, w_br_m.astype(BF16), wba, w_mix_out.astype(BF16), ln_g.reshape(1, -1), ln_b.reshape(1, -1))


def _kv_kernel(m_ref, w_ref, o_ref):
    o_ref[...] = jnp.dot(m_ref[...].astype(BF16), w_ref[...], preferred_element_type=F32).astype(BF16)


def _mem_kv(mem2, w_ckv):
    r = mem2.shape[0]
    tm = min(512, r)
    return pl.pallas_call(
        _kv_kernel,
        out_shape=jax.ShapeDtypeStruct((r, 2 * D_MODEL), BF16),
        grid=(r // tm,),
        in_specs=[pl.BlockSpec((tm, D_MODEL), lambda i: (i, 0)), _const_spec((D_MODEL, 2 * D_MODEL))],
        out_specs=pl.BlockSpec((tm, 2 * D_MODEL), lambda i: (i, 0)),
        compiler_params=_cparams(("parallel",)),
        name="memory_kv",
    )(mem2, w_ckv.astype(BF16))


def _xattn_kernel(h_ref, kv_ref, wq_ref, wo_ref, g_ref, b_ref, wr_ref, br_ref, h2_ref, idx_ref, gate_ref):
    h1 = h_ref[...]
    q = jnp.dot(h1.astype(BF16), wq_ref[...], preferred_element_type=F32).astype(BF16)
    scale = X_HEAD_DIM ** -0.5
    outs = []
    for hd in range(X_HEADS):
        ks = slice(hd * X_HEAD_DIM, (hd + 1) * X_HEAD_DIM)
        vs = slice(D_MODEL + hd * X_HEAD_DIM, D_MODEL + (hd + 1) * X_HEAD_DIM)
        sc = lax.dot_general(q[:, ks], kv_ref[:, ks], (((1,), (1,)), ((), ())), preferred_element_type=F32) * scale
        sc = sc - jnp.max(sc, axis=1, keepdims=True)
        p = jnp.exp(sc)
        p = p / jnp.sum(p, axis=1, keepdims=True)
        outs.append(jnp.dot(p.astype(BF16), kv_ref[:, vs], preferred_element_type=F32).astype(BF16))
    o = jnp.concatenate(outs, axis=1)
    att = jnp.dot(o, wo_ref[...], preferred_element_type=F32)
    h2 = _layer_norm_rows(DN_ALPHA * h1 + att, g_ref[...], b_ref[...])
    h2_ref[...] = h2

    logits = jnp.dot(h2, wr_ref[...], preferred_element_type=F32, precision=lax.Precision.HIGHEST) + br_ref[...]
    tm = logits.shape[0]
    lane = lax.broadcasted_iota(I32, (tm, LANES), 1)
    lane_f = lane.astype(F32)
    work = jnp.where(lane < N_EXPERTS, logits, -jnp.inf)
    vals, idxs = [], []
    for _ in range(TOP_K):
        mx = jnp.max(work, axis=1, keepdims=True)
        ix = jnp.min(jnp.where(work == mx, lane_f, float(LANES)), axis=1, keepdims=True)
        vals.append(mx)
        idxs.append(ix)
        work = jnp.where(lane_f == ix, -jnp.inf, work)
    es = [jnp.exp(vv - vals[0]) for vv in vals]
    tot = es[0] + es[1] + es[2] + es[3]
    idx_slab = jnp.zeros((tm, LANES), F32)
    gate_slab = jnp.zeros((tm, LANES), F32)
    for kk in range(TOP_K):
        idx_slab = jnp.where(lane == kk, idxs[kk], idx_slab)
        gate_slab = jnp.where(lane == kk, es[kk] / tot, gate_slab)
    idx_ref[...] = idx_slab.astype(I32)
    gate_ref[...] = gate_slab


def _cross_attention_router(h1, kv, bsz, s, w_cq, w_co, ln_g, ln_b, w_router, b_router):
    t = h1.shape[0]
    tm = min(XATT_ROWS, s)
    ns = s // tm
    n_mem = kv.shape[0] // bsz
    wr = jnp.concatenate([w_router, jnp.zeros((D_MODEL, LANES - N_EXPERTS), F32)], axis=1)
    br = jnp.concatenate([b_router, jnp.zeros((LANES - N_EXPERTS,), F32)]).reshape(1, LANES)
    row = lambda n: pl.BlockSpec((tm, n), lambda b, i: (b * ns + i, 0))
    return pl.pallas_call(
        _xattn_kernel,
        out_shape=[jax.ShapeDtypeStruct((t, D_MODEL), F32), jax.ShapeDtypeStruct((t, LANES), I32),
                   jax.ShapeDtypeStruct((t, LANES), F32)],
        grid=(bsz, ns),
        in_specs=[row(D_MODEL), pl.BlockSpec((n_mem, 2 * D_MODEL), lambda b, i: (b, 0)),
                  _const_spec((D_MODEL, D_MODEL)), _const_spec((D_MODEL, D_MODEL)),
                  _const_spec((1, D_MODEL)), _const_spec((1, D_MODEL)),
                  _const_spec((D_MODEL, LANES)), _const_spec((1, LANES))],
        out_specs=[row(D_MODEL), row(LANES), row(LANES)],
        compiler_params=_cparams(("parallel", "parallel")),
        name="cross_attention_router",
    )(h1, kv, w_cq.astype(BF16), w_co.astype(BF16), ln_g.reshape(1, -1), ln_b.reshape(1, -1), wr, br)


def _rank_kernel(idx_ref, ltri_ref, rank_ref, cnt_ref, carry_s):
    @pl.when(pl.program_id(0) == 0)
    def _():
        carry_s[...] = jnp.zeros_like(carry_s)

    idx = idx_ref[...]
    tm = idx.shape[0]
    lane = lax.broadcasted_iota(I32, (tm, LANES), 1)
    onehots = [(lane == idx[:, kk:kk + 1]).astype(F32) for kk in range(TOP_K)]
    sel = onehots[0] + onehots[1] + onehots[2] + onehots[3]
    before = jnp.dot(ltri_ref[...], sel.astype(BF16), preferred_element_type=F32) + carry_s[0:1, :]
    rank_slab = jnp.zeros((tm, LANES), F32)
    for kk in range(TOP_K):
        r = jnp.sum(onehots[kk] * before, axis=1, keepdims=True)
        rank_slab = jnp.where(lane == kk, r, rank_slab)
    rank_ref[...] = rank_slab.astype(I32)
    carry_s[0:1, :] = carry_s[0:1, :] + jnp.sum(sel, axis=0, keepdims=True)
    cnt_ref[...] = jnp.broadcast_to(carry_s[0:1, :], cnt_ref.shape).astype(I32)


def _routing_ranks(idx_slab):
    t = idx_slab.shape[0]
    tm = min(RANK_ROWS, t)
    ltri = jnp.tril(jnp.ones((tm, tm), BF16), k=-1)
    return pl.pallas_call(
        _rank_kernel,
        out_shape=[jax.ShapeDtypeStruct((t, LANES), I32), jax.ShapeDtypeStruct((8, LANES), I32)],
        grid=(t // tm,),
        in_specs=[pl.BlockSpec((tm, LANES), lambda i: (i, 0)), _const_spec((tm, tm))],
        out_specs=[pl.BlockSpec((tm, LANES), lambda i: (i, 0)), _const_spec((8, LANES))],
        scratch_shapes=[pltpu.VMEM((8, LANES), F32)],
        compiler_params=_cparams(("arbitrary",)),
        name="routing_ranks",
    )(idx_slab, ltri)


def _dest_kernel(idx_ref, rank_ref, pstart_ref, dest_ref):
    idx = idx_ref[...]
    tm = idx.shape[0]
    lane = lax.broadcasted_iota(I32, (tm, LANES), 1)
    dest = rank_ref[...].astype(F32)
    pstart = pstart_ref[0:1, :].astype(F32)
    for kk in range(TOP_K):
        start = jnp.sum(jnp.where(lane == idx[:, kk:kk + 1], pstart, 0.0), axis=1, keepdims=True)
        dest = dest + jnp.where(lane == kk, start, 0.0)
    dest_ref[...] = dest.astype(I32)


def _dest_rows(idx_slab, rank_slab, pstart_row):
    t = idx_slab.shape[0]
    tm = min(RANK_ROWS, t)
    return pl.pallas_call(
        _dest_kernel,
        out_shape=jax.ShapeDtypeStruct((t, LANES), I32),
        grid=(t // tm,),
        in_specs=[pl.BlockSpec((tm, LANES), lambda i: (i, 0)), pl.BlockSpec((tm, LANES), lambda i: (i, 0)),
                  _const_spec((8, LANES))],
        out_specs=pl.BlockSpec((tm, LANES), lambda i: (i, 0)),
        compiler_params=_cparams(("parallel",)),
        name="routing_dest",
    )(idx_slab, rank_slab, pstart_row)


def _dispatch_kernel(dest_ref, x_hbm, buf_in, buf_hbm, sem):
    del buf_in
    n = dest_ref.shape[-1]
    base = pl.program_id(0) * (n // TOP_K)

    def issue(j, c):
        d = dest_ref[0, j]
        tok = base + j // TOP_K
        pltpu.make_async_copy(x_hbm.at[pl.ds(tok, 1), :], buf_hbm.at[pl.ds(d, 1), :], sem).start()
        return c

    lax.fori_loop(0, n, issue, 0)

    def drain(j, c):
        pltpu.make_async_copy(x_hbm.at[pl.ds(0, 1), :], buf_hbm.at[pl.ds(0, 1), :], sem).wait()
        return c

    lax.fori_loop(0, n, drain, 0)


def _dispatch(h2, dest_flat, n_rows):
    t = h2.shape[0]
    tt = min(DISPATCH_TOKENS, t)
    n = tt * TOP_K
    dest3 = dest_flat.reshape(t // tt, 1, n)
    buf0 = jnp.zeros((n_rows, D_MODEL), h2.dtype)
    return pl.pallas_call(
        _dispatch_kernel,
        out_shape=jax.ShapeDtypeStruct((n_rows, D_MODEL), h2.dtype),
        grid=(t // tt,),
        in_specs=[pl.BlockSpec((None, 1, n), lambda i: (i, 0, 0), memory_space=pltpu.SMEM),
                  pl.BlockSpec(memory_space=pl.ANY), pl.BlockSpec(memory_space=pl.ANY)],
        out_specs=pl.BlockSpec(memory_space=pl.ANY),
        scratch_shapes=[pltpu.SemaphoreType.DMA],
        input_output_aliases={2: 0},
        compiler_params=pltpu.CompilerParams(dimension_semantics=("arbitrary",), has_side_effects=True),
        name="moe_dispatch",
    )(dest3, h2, buf0)


def _expert_kernel(blk_e_ref, nblk_ref, x_ref, wg_ref, wl_ref, bg_ref, bl_ref, wd_ref, bd_ref, y_ref):
    @pl.when(pl.program_id(0) < nblk_ref[0])
    def _():
        xb = x_ref[...].astype(BF16)
        glu = jnp.dot(xb, wg_ref[...], preferred_element_type=F32) + bg_ref[...]
        lin = jnp.dot(xb, wl_ref[...], preferred_element_type=F32) + bl_ref[...]
        glu = jnp.minimum(glu, SWIGLU_LIMIT)
        lin = jnp.clip(lin, -SWIGLU_LIMIT, SWIGLU_LIMIT)
        act = glu * jax.nn.sigmoid(SWIGLU_ALPHA * glu) * (lin + 1.0)
        y_ref[...] = jnp.dot(act.astype(BF16), wd_ref[...], preferred_element_type=F32) + bd_ref[...]

    @pl.when(pl.program_id(0) >= nblk_ref[0])
    def _():
        y_ref[...] = jnp.zeros_like(y_ref)


def _experts(xb, blk_e, nblk, w_glu, w_lin, b_glu, b_lin, w_dn, b_dn):
    n_rows = xb.shape[0]
    bm = EXPERT_ROWS
    grid_spec = pltpu.PrefetchScalarGridSpec(
        num_scalar_prefetch=2,
        grid=(n_rows // bm,),
        in_specs=[pl.BlockSpec((bm, D_MODEL), lambda i, be, nb: (i, 0)),
                  pl.BlockSpec((None, D_MODEL, D_EXPERT), lambda i, be, nb: (be[i], 0, 0)),
                  pl.BlockSpec((None, D_MODEL, D_EXPERT), lambda i, be, nb: (be[i], 0, 0)),
                  pl.BlockSpec((None, 1, D_EXPERT), lambda i, be, nb: (be[i], 0, 0)),
                  pl.BlockSpec((None, 1, D_EXPERT), lambda i, be, nb: (be[i], 0, 0)),
                  pl.BlockSpec((None, D_EXPERT, D_MODEL), lambda i, be, nb: (be[i], 0, 0)),
                  pl.BlockSpec((None, 1, D_MODEL), lambda i, be, nb: (be[i], 0, 0))],
        out_specs=pl.BlockSpec((bm, D_MODEL), lambda i, be, nb: (i, 0)),
    )
    return pl.pallas_call(
        _expert_kernel,
        out_shape=jax.ShapeDtypeStruct((n_rows, D_MODEL), F32),
        grid_spec=grid_spec,
        compiler_params=_cparams(("arbitrary",)),
        name="moe_experts",
    )(blk_e, nblk, xb, w_glu, w_lin, b_glu, b_lin, w_dn, b_dn)


def _combine_kernel(dest_ref, h_ref, gate_ref, y_hbm, g_ref, b_ref, o_ref, rows_s, sem):
    tt = h_ref.shape[0]
    n = tt * TOP_K

    def issue(j, c):
        d = dest_ref[0, j]
        tok = j // TOP_K
        kk = j % TOP_K
        pltpu.make_async_copy(y_hbm.at[pl.ds(d, 1), :], rows_s.at[kk, pl.ds(tok, 1), :], sem).start()
        return c

    lax.fori_loop(0, n, issue, 0)

    def drain(j, c):
        pltpu.make_async_copy(y_hbm.at[pl.ds(0, 1), :], rows_s.at[0, pl.ds(0, 1), :], sem).wait()
        return c

    lax.fori_loop(0, n, drain, 0)
    gates = gate_ref[...]
    acc = DN_ALPHA * h_ref[...]
    for kk in range(TOP_K):
        acc = acc + gates[:, kk:kk + 1] * rows_s[kk]
    o_ref[...] = _layer_norm_rows(acc, g_ref[...], b_ref[...])


def _combine(h2, gate_slab, dest_flat, yb, ln_g, ln_b):
    t = h2.shape[0]
    tt = min(COMBINE_TOKENS, t)
    n = tt * TOP_K
    dest3 = dest_flat.reshape(t // tt, 1, n)
    row = lambda m: pl.BlockSpec((tt, m), lambda i: (i, 0))
    return pl.pallas_call(
        _combine_kernel,
        out_shape=jax.ShapeDtypeStruct((t, D_MODEL), F32),
        grid=(t // tt,),
        in_specs=[pl.BlockSpec((None, 1, n), lambda i: (i, 0, 0), memory_space=pltpu.SMEM),
                  row(D_MODEL), row(LANES), pl.BlockSpec(memory_space=pl.ANY),
                  _const_spec((1, D_MODEL)), _const_spec((1, D_MODEL))],
        out_specs=row(D_MODEL),
        scratch_shapes=[pltpu.VMEM((TOP_K, tt, D_MODEL), F32), pltpu.SemaphoreType.DMA],
        compiler_params=_cparams(("arbitrary",)),
        name="moe_combine",
    )(dest3, h2, gate_slab, yb, ln_g.reshape(1, -1), ln_b.reshape(1, -1))


def _moe(h2, idx_slab, gate_slab, w_gu, b_gu, w_dn, b_dn, ln_g, ln_b):
    t = h2.shape[0]
    bm = EXPERT_ROWS
    n_assign = t * TOP_K
    n_rows = n_assign + N_EXPERTS * bm
    n_blocks = n_rows // bm

    rank_slab, cnt = _routing_ranks(idx_slab)
    counts = cnt[0, :N_EXPERTS]
    padded = (counts + bm - 1) // bm * bm
    pends = jnp.cumsum(padded)
    pstarts = pends - padded
    pstart_row = jnp.zeros((8, LANES), I32).at[:, :N_EXPERTS].set(pstarts[None, :])
    blk_e = jnp.minimum(jnp.searchsorted(pends, jnp.arange(n_blocks, dtype=I32) * bm, side='right'),
                        N_EXPERTS - 1).astype(I32)
    nblk = (pends[-1] // bm).astype(I32).reshape(1)
    dest_slab = _dest_rows(idx_slab, rank_slab, pstart_row)
    dest_flat = dest_slab[:, :TOP_K].reshape(-1)

    xb = _dispatch(h2, dest_flat, n_rows)
    w_glu = w_gu[:, :, 0::2].astype(BF16)
    w_lin = w_gu[:, :, 1::2].astype(BF16)
    b_glu = b_gu[:, None, 0::2]
    b_lin = b_gu[:, None, 1::2]
    yb = _experts(xb, blk_e, nblk, w_glu, w_lin, b_glu, b_lin, w_dn.astype(BF16), b_dn[:, None, :])
    return _combine(h2, gate_slab, dest_flat, yb, ln_g, ln_b)


def kernel(x, mem, positions, w_in, b_igate, b_fgate, conv_w, conv_b, w_mq, w_mk, g_mhead, w_mskip, g_qlat, g_kvlat,
           w_uq, w_ukv, w_br_m, w_br_a, w_mix_out, ln1_g, ln1_b, w_cq, w_ckv, w_co, ln2_g, ln2_b, w_router, b_router,
           w_gu, b_gu, w_dn, b_dn, ln3_g, ln3_b):
    bsz, s, d = x.shape
    t = bsz * s
    h = x.reshape(t, d)
    pos2 = positions.reshape(t, 1)
    for l in range(DEPTH):
        q, k, v, xm, vm, op, gm, ga, gate = _input_projection(h, pos2, w_in[l], g_qlat[l], g_kvlat[l], w_uq[l],
                                                              w_ukv[l])
        ym = _mlstm_branch(xm, vm, op, gate, bsz, s, b_igate[l], b_fgate[l], conv_w[l], conv_b[l], w_mq[l], w_mk[l],
                           g_mhead[l], w_mskip[l])
        oa = _mla_attention(q, k, v, bsz, s)
        h1 = _merge(ym, oa, gm, ga, h, w_br_m[l], w_br_a[l], w_mix_out[l], ln1_g[l], ln1_b[l])
        kv = _mem_kv(mem.reshape(-1, d), w_ckv[l])
        h2, idx_slab, gate_slab = _cross_attention_router(h1, kv, bsz, s, w_cq[l], w_co[l], ln2_g[l], ln2_b[l],
                                                          w_router[l], b_router[l])
        h = _moe(h2, idx_slab, gate_slab, w_gu[l], b_gu[l], w_dn[l], b_dn[l], ln3_g[l], ln3_b[l])
    return h.reshape(bsz, s, d)
```

```python
import functools

import numpy as np
import jax
import jax.numpy as jnp
from jax import lax
from jax.experimental import pallas as pl
from jax.experimental.pallas import tpu as pltpu

F32 = jnp.float32
BF16 = jnp.bfloat16
I32 = jnp.int32

D_MODEL = 1024
N_MEM = 256
M_HEADS = 4
M_HEAD_DIM = 128
M_WIDTH = M_HEADS * M_HEAD_DIM
M_CONV = 4
A_HEADS = 8
A_NOPE = 64
A_ROPE = 32
A_QK = A_NOPE + A_ROPE
A_VDIM = 64
A_Q_RANK = 256
A_KV_RANK = 128
ROPE_THETA = 10000.0
X_HEADS = 4
X_HEAD_DIM = D_MODEL // X_HEADS
N_EXPERTS = 32
TOP_K = 4
D_EXPERT = D_MODEL
SWIGLU_ALPHA = 1.702
SWIGLU_LIMIT = 7.0
DEPTH = 1
DN_ALPHA = (2.0 * DEPTH) ** 0.25
EPS = 1e-5
IN_SPLITS = (A_Q_RANK, A_KV_RANK, A_ROPE, M_WIDTH, M_WIDTH, M_WIDTH, M_HEADS, M_HEADS, D_MODEL, D_MODEL)
IN_OFFSETS = tuple(int(v) for v in np.cumsum((0,) + IN_SPLITS))

LANES = 128
VMEM_LIMIT = 56 * 1024 * 1024

PROJ_ROWS = 512
M_CHUNK_ROWS = 128
ATT_Q = 256
ATT_K = 256
MERGE_ROWS = 512
XATT_ROWS = 512
RANK_ROWS = 512
EXPERT_ROWS = 256
DISPATCH_TOKENS = 256
COMBINE_TOKENS = 64
NEG_BIG = -1e30

C_QLAT = 0
C_KVLAT = C_QLAT + A_Q_RANK
C_KR = C_KVLAT + A_KV_RANK
C_KRS = C_KR + LANES
C_GATE = C_KRS + LANES
C_XM = C_GATE + LANES
C_VM = C_XM + M_WIDTH
C_OP = C_VM + M_WIDTH
C_GM = C_OP + M_WIDTH
C_GA = C_GM + D_MODEL
C_END = C_GA + D_MODEL


def _cparams(sem, vmem=VMEM_LIMIT):
    return pltpu.CompilerParams(dimension_semantics=sem, vmem_limit_bytes=vmem)


def _const_spec(shape):
    nd = len(shape)
    return pl.BlockSpec(shape, lambda *a: (0,) * nd)


def _layer_norm_rows(v, g, b):
    mu = jnp.mean(v, axis=-1, keepdims=True)
    d = v - mu
    var = jnp.mean(d * d, axis=-1, keepdims=True)
    return d * lax.rsqrt(var + EPS) * g + b


def _proj_kernel(x_ref, pos_ref, w_ref, gq_ref, gkv_ref, wuq_ref, wuqs_ref, wuk_ref, wuv_ref, invf_ref,
                 q_ref, k_ref, v_ref, xm_ref, vm_ref, op_ref, gm_ref, ga_ref, gate_ref):
    xb = x_ref[...].astype(BF16)

    def mm(lo, n):
        return jnp.dot(xb, w_ref[:, lo:lo + n], preferred_element_type=F32)

    xm_ref[...] = mm(C_XM, M_WIDTH).astype(BF16)
    vm_ref[...] = mm(C_VM, M_WIDTH).astype(BF16)
    op_ref[...] = mm(C_OP, M_WIDTH).astype(BF16)
    gm_ref[...] = mm(C_GM, D_MODEL).astype(BF16)
    ga_ref[...] = mm(C_GA, D_MODEL).astype(BF16)
    gate_ref[...] = mm(C_GATE, LANES)

    ang = pos_ref[...].astype(F32) * invf_ref[...]
    cos = jnp.cos(ang)
    sin = jnp.sin(ang)

    q_lat = mm(C_QLAT, A_Q_RANK)
    qn = (q_lat * lax.rsqrt(jnp.mean(q_lat * q_lat, axis=-1, keepdims=True) + EPS) * gq_ref[...]).astype(BF16)
    q = jnp.dot(qn, wuq_ref[...], preferred_element_type=F32)
    qs = jnp.dot(qn, wuqs_ref[...], preferred_element_type=F32)
    kv_lat = mm(C_KVLAT, A_KV_RANK)
    kvn = (kv_lat * lax.rsqrt(jnp.mean(kv_lat * kv_lat, axis=-1, keepdims=True) + EPS) * gkv_ref[...]).astype(BF16)
    kn = jnp.dot(kvn, wuk_ref[...], preferred_element_type=F32)
    v_ref[...] = jnp.dot(kvn, wuv_ref[...], preferred_element_type=F32).astype(BF16)
    k_pe = mm(C_KR, LANES) * cos + mm(C_KRS, LANES) * sin
    scale = A_QK ** -0.5
    for h in range(A_HEADS):
        sl = slice(h * LANES, (h + 1) * LANES)
        q_ref[:, sl] = ((q[:, sl] * cos + qs[:, sl] * sin) * scale).astype(BF16)
        k_ref[:, sl] = (kn[:, sl] + k_pe).astype(BF16)


def _proj_weights(w_in, w_uq, w_ukv):
    o = IN_OFFSETS
    half = A_ROPE // 2
    w_q, w_kv, w_kr = w_in[:, o[0]:o[1]], w_in[:, o[1]:o[2]], w_in[:, o[2]:o[3]]
    w_xm, w_vm, w_op = w_in[:, o[3]:o[4]], w_in[:, o[4]:o[5]], w_in[:, o[5]:o[6]]
    w_i, w_f, w_gm, w_ga = w_in[:, o[6]:o[7]], w_in[:, o[7]:o[8]], w_in[:, o[8]:o[9]], w_in[:, o[9]:o[10]]
    d = w_in.shape[0]
    z = lambda n: jnp.zeros((d, n), w_in.dtype)
    kr = jnp.concatenate([z(A_NOPE), w_kr, z(LANES - A_QK)], axis=1)
    krs = jnp.concatenate([z(A_NOPE), -w_kr[:, half:], w_kr[:, :half], z(LANES - A_QK)], axis=1)
    gate = jnp.concatenate([w_i, w_f, z(LANES - 2 * M_HEADS)], axis=1)
    w_r = jnp.concatenate([w_q, w_kv, kr, krs, gate, w_xm, w_vm, w_op, w_gm, w_ga], axis=1).astype(BF16)

    uq = w_uq.reshape(A_Q_RANK, A_HEADS, A_QK)
    zq = jnp.zeros((A_Q_RANK, A_HEADS, LANES - A_QK), w_uq.dtype)
    zn = jnp.zeros((A_Q_RANK, A_HEADS, A_NOPE), w_uq.dtype)
    uq_pad = jnp.concatenate([uq, zq], axis=-1).reshape(A_Q_RANK, A_HEADS * LANES).astype(BF16)
    uqs_pad = jnp.concatenate([zn, -uq[..., A_NOPE + half:], uq[..., A_NOPE:A_NOPE + half], zq],
                              axis=-1).reshape(A_Q_RANK, A_HEADS * LANES).astype(BF16)
    ukv = w_ukv.reshape(A_KV_RANK, A_HEADS, A_NOPE + A_VDIM)
    zk = jnp.zeros((A_KV_RANK, A_HEADS, LANES - A_NOPE), w_ukv.dtype)
    uk_pad = jnp.concatenate([ukv[..., :A_NOPE], zk], axis=-1).reshape(A_KV_RANK, A_HEADS * LANES).astype(BF16)
    zv = jnp.zeros((A_KV_RANK, A_HEADS, LANES - A_VDIM), w_ukv.dtype)
    uv_pad = jnp.concatenate([ukv[..., A_NOPE:], zv], axis=-1).reshape(A_KV_RANK, A_HEADS * LANES).astype(BF16)
    return w_r, uq_pad, uqs_pad, uk_pad, uv_pad


def _input_projection(x2, pos2, w_in, g_qlat, g_kvlat, w_uq, w_ukv):
    t = x2.shape[0]
    tm = min(PROJ_ROWS, t)
    w_r, uq_pad, uqs_pad, uk_pad, uv_pad = _proj_weights(w_in, w_uq, w_ukv)
    half = A_ROPE // 2
    inv_freq = ROPE_THETA ** (-jnp.arange(half, dtype=F32) / half)
    invf = jnp.concatenate([jnp.zeros((A_NOPE,), F32), inv_freq, inv_freq,
                            jnp.zeros((LANES - A_QK,), F32)]).reshape(1, LANES)
    hw = A_HEADS * LANES
    row = lambda n: pl.BlockSpec((tm, n), lambda i: (i, 0))
    outs = [
        jax.ShapeDtypeStruct((t, hw), BF16), jax.ShapeDtypeStruct((t, hw), BF16), jax.ShapeDtypeStruct((t, hw), BF16),
        jax.ShapeDtypeStruct((t, M_WIDTH), BF16), jax.ShapeDtypeStruct((t, M_WIDTH), BF16),
        jax.ShapeDtypeStruct((t, M_WIDTH), BF16),
        jax.ShapeDtypeStruct((t, D_MODEL), BF16), jax.ShapeDtypeStruct((t, D_MODEL), BF16),
        jax.ShapeDtypeStruct((t, LANES), F32),
    ]
    return pl.pallas_call(
        _proj_kernel,
        out_shape=outs,
        grid=(t // tm,),
        in_specs=[row(D_MODEL), row(1), _const_spec(w_r.shape), _const_spec((1, A_Q_RANK)),
                  _const_spec((1, A_KV_RANK)), _const_spec(uq_pad.shape), _const_spec(uqs_pad.shape),
                  _const_spec(uk_pad.shape), _const_spec(uv_pad.shape), _const_spec((1, LANES))],
        out_specs=[row(hw), row(hw), row(hw), row(M_WIDTH), row(M_WIDTH), row(M_WIDTH), row(D_MODEL), row(D_MODEL),
                   row(LANES)],
        compiler_params=_cparams(("parallel",)),
        name="input_projection",
    )(x2, pos2, w_r, g_qlat.reshape(1, -1), g_kvlat.reshape(1, -1), uq_pad, uqs_pad, uk_pad, uv_pad, invf)


def _log_sigmoid(v):
    return jnp.minimum(v, 0.0) - jnp.log1p(jnp.exp(-jnp.abs(v)))


def _mlstm_kernel(xm_ref, vm_ref, op_ref, gate_ref, convw_ref, convb_ref, wq_ref, wk_ref, gbias_ref, ghead_ref,
                  skip_ref, tril_ref, out_ref, xpad_s, ct_s, n_s, m_s):
    s = xm_ref.shape[0]
    lc = M_CHUNK_ROWS
    halo = 8
    xpad_s[0:halo, :] = jnp.zeros((halo, M_WIDTH), F32)
    xpad_s[halo:, :] = xm_ref[...].astype(F32)
    ct_s[...] = jnp.zeros_like(ct_s)
    n_s[...] = jnp.zeros_like(n_s)
    m_s[...] = jnp.zeros_like(m_s)
    rows = lax.broadcasted_iota(I32, (lc, lc), 0)
    cols = lax.broadcasted_iota(I32, (lc, lc), 1)
    causal = rows >= cols
    kscale = M_HEAD_DIM ** -0.5

    def chunk(c, carry):
        r0 = pl.multiple_of(c * lc, lc)
        win = xpad_s[pl.ds(r0, lc + halo), :]
        conv = convb_ref[...]
        for j in range(M_CONV):
            lo = halo - (M_CONV - 1) + j
            conv = conv + win[lo:lo + lc, :] * convw_ref[j:j + 1, :]
        xc = conv * jax.nn.sigmoid(conv)
        g = gate_ref[pl.ds(r0, lc), :] + gbias_ref[...]
        ls = _log_sigmoid(g)
        bc = jnp.dot(tril_ref[...], ls, preferred_element_type=F32, precision=lax.Precision.HIGHEST)
        g_t = g.T
        b_t = bc.T
        for h in range(M_HEADS):
            hs = slice(h * M_HEAD_DIM, (h + 1) * M_HEAD_DIM)
            ig_col = g[:, h:h + 1]
            ig_row = g_t[h:h + 1, :]
            b_col = bc[:, M_HEADS + h:M_HEADS + h + 1]
            b_row = b_t[M_HEADS + h:M_HEADS + h + 1, :]
            b_tot = bc[lc - 1:lc, M_HEADS + h:M_HEADS + h + 1]
            m_prev = m_s[h:h + 1, 0:1]
            n_prev = n_s[h:h + 1, :]
            ct_prev = ct_s[h]

            xh = xc[:, hs]
            xhb = xh.astype(BF16)
            q = jnp.dot(xhb, wq_ref[h], preferred_element_type=F32)
            k = jnp.dot(xhb, wk_ref[h], preferred_element_type=F32) * kscale
            qb = q.astype(BF16)
            kb = k.astype(BF16)
            v = vm_ref[pl.ds(r0, lc), hs]

            log_d = jnp.where(causal, b_col - b_row + ig_row, -jnp.inf)
            m_intra = jnp.max(log_d, axis=1, keepdims=True)
            g_inter = b_col + m_prev
            m = jnp.maximum(g_inter, m_intra)
            w_inter = jnp.exp(g_inter - m)
            dmat = jnp.exp(log_d - m)
            sc = lax.dot_general(qb, kb, (((1,), (1,)), ((), ())), preferred_element_type=F32) * dmat
            num = (w_inter * jnp.dot(qb, ct_prev.astype(BF16), preferred_element_type=F32)
                   + jnp.dot(sc.astype(BF16), v, preferred_element_type=F32))
            den = w_inter * jnp.sum(q * n_prev, axis=1, keepdims=True) + jnp.sum(sc, axis=1, keepdims=True)
            hh = num / jnp.maximum(jnp.abs(den), jnp.exp(-m))

            a = b_tot - b_col + ig_col
            m_chunk = jnp.max(a, axis=0, keepdims=True)
            kw = k * jnp.exp(a - m_chunk)
            ct_c = lax.dot_general(kw.astype(BF16), v, (((0,), (0,)), ((), ())), preferred_element_type=F32)
            n_c = jnp.sum(kw, axis=0, keepdims=True)
            m_new = jnp.maximum(b_tot + m_prev, m_chunk)
            s_old = jnp.exp(b_tot + m_prev - m_new)
            s_new = jnp.exp(m_chunk - m_new)
            ct_s[h] = s_old * ct_prev + s_new * ct_c
            n_s[h:h + 1, :] = s_old * n_prev + s_new * n_c
            m_s[h:h + 1, :] = jnp.broadcast_to(m_new, (1, LANES))

            mu = jnp.mean(hh, axis=-1, keepdims=True)
            dv = hh - mu
            var = jnp.mean(dv * dv, axis=-1, keepdims=True)
            hn = dv * lax.rsqrt(var + EPS) * ghead_ref[:, hs]
            y = jax.nn.sigmoid(op_ref[pl.ds(r0, lc), hs].astype(F32)) * (hn + skip_ref[:, hs] * xh)
            out_ref[pl.ds(r0, lc), hs] = y.astype(BF16)
        return carry

    lax.fori_loop(0, s // lc, chunk, 0)


def _mlstm_branch(xm, vm, op, gate, bsz, s, b_i, b_f, conv_w, conv_b, w_mq, w_mk, g_mhead, w_mskip):
    lc = M_CHUNK_ROWS
    gbias = jnp.concatenate([b_i, b_f, jnp.zeros((LANES - 2 * M_HEADS,), F32)]).reshape(1, LANES)
    tril = jnp.tril(jnp.ones((lc, lc), F32))
    seq = lambda n: pl.BlockSpec((s, n), lambda b: (b, 0))
    return pl.pallas_call(
        _mlstm_kernel,
        out_shape=jax.ShapeDtypeStruct((bsz * s, M_WIDTH), BF16),
        grid=(bsz,),
        in_specs=[seq(M_WIDTH), seq(M_WIDTH), seq(M_WIDTH), seq(LANES),
                  _const_spec((M_CONV, M_WIDTH)), _const_spec((1, M_WIDTH)),
                  _const_spec((M_HEADS, M_HEAD_DIM, M_HEAD_DIM)), _const_spec((M_HEADS, M_HEAD_DIM, M_HEAD_DIM)),
                  _const_spec((1, LANES)), _const_spec((1, M_WIDTH)), _const_spec((1, M_WIDTH)),
                  _const_spec((lc, lc))],
        out_specs=seq(M_WIDTH),
        scratch_shapes=[pltpu.VMEM((s + 8, M_WIDTH), F32), pltpu.VMEM((M_HEADS, M_HEAD_DIM, M_HEAD_DIM), F32),
                        pltpu.VMEM((8, LANES), F32), pltpu.VMEM((8, LANES), F32)],
        compiler_params=_cparams(("parallel",)),
        name="mlstm_branch",
    )(xm, vm, op, gate, conv_w, conv_b.reshape(1, -1), w_mq.astype(BF16), w_mk.astype(BF16), gbias,
      g_mhead.reshape(1, -1), w_mskip.reshape(1, -1), tril)


def _attn_kernel(q_ref, k_ref, v_ref, o_ref, m_s, l_s, acc_s):
    tq = q_ref.shape[0]
    i = pl.program_id(1)
    m_s[...] = jnp.full(m_s.shape, NEG_BIG, F32)
    l_s[...] = jnp.zeros_like(l_s)
    acc_s[...] = jnp.zeros_like(acc_s)
    rows = lax.broadcasted_iota(I32, (tq, tq), 0)
    cols = lax.broadcasted_iota(I32, (tq, tq), 1)

    def kv_block(k0, diagonal):
        for h in range(A_HEADS):
            hs = slice(h * LANES, (h + 1) * LANES)
            sc = lax.dot_general(q_ref[:, hs], k_ref[pl.ds(k0, tq), hs], (((1,), (1,)), ((), ())),
                                 preferred_element_type=F32)
            if diagonal:
                sc = jnp.where(cols <= rows, sc, NEG_BIG)
            m_old = m_s[h]
            m_new = jnp.maximum(m_old, jnp.max(sc, axis=1, keepdims=True))
            p = jnp.exp(sc - m_new)
            alpha = jnp.exp(m_old - m_new)
            m_s[h] = m_new
            l_s[h] = alpha * l_s[h] + jnp.sum(p, axis=1, keepdims=True)
            acc_s[h] = alpha * acc_s[h] + jnp.dot(p.astype(BF16), v_ref[pl.ds(k0, tq), hs],
                                                  preferred_element_type=F32)

    def body(j, c):
        kv_block(pl.multiple_of(j * tq, tq), False)
        return c

    lax.fori_loop(0, i, body, 0)
    kv_block(pl.multiple_of(i * tq, tq), True)
    for h in range(A_HEADS):
        o_ref[:, h * LANES:(h + 1) * LANES] = (acc_s[h] / l_s[h]).astype(BF16)


def _mla_attention(q, k, v, bsz, s):
    tq = min(ATT_Q, s)
    nq = s // tq
    hw = A_HEADS * LANES
    return pl.pallas_call(
        _attn_kernel,
        out_shape=jax.ShapeDtypeStruct(q.shape, BF16),
        grid=(bsz, nq),
        in_specs=[pl.BlockSpec((tq, hw), lambda b, i: (b * nq + i, 0)),
                  pl.BlockSpec((s, hw), lambda b, i: (b, 0)),
                  pl.BlockSpec((s, hw), lambda b, i: (b, 0))],
        out_specs=pl.BlockSpec((tq, hw), lambda b, i: (b * nq + i, 0)),
        scratch_shapes=[pltpu.VMEM((A_HEADS, tq, 1), F32), pltpu.VMEM((A_HEADS, tq, 1), F32),
                        pltpu.VMEM((A_HEADS, tq, LANES), F32)],
        compiler_params=_cparams(("parallel", "arbitrary")),
        name="mla_attention",
    )(q, k, v)


def _merge_kernel(ym_ref, oa_ref, gm_ref, ga_ref, x_ref, wbm_ref, wba_ref, wmix_ref, g_ref, b_ref, h_ref):
    y_m = jnp.dot(ym_ref[...], wbm_ref[...], preferred_element_type=F32)
    y_a = jnp.dot(oa_ref[...], wba_ref[...], preferred_element_type=F32)
    mixed = jax.nn.sigmoid(gm_ref[...].astype(F32)) * y_m + jax.nn.sigmoid(ga_ref[...].astype(F32)) * y_a
    mix = jnp.dot(mixed.astype(BF16), wmix_ref[...], preferred_element_type=F32)
    h_ref[...] = _layer_norm_rows(DN_ALPHA * x_ref[...] + mix, g_ref[...], b_ref[...])


def _merge(ym, oa, gm, ga, x2, w_br_m, w_br_a, w_mix_out, ln_g, ln_b):
    t = x2.shape[0]
    tm = min(MERGE_ROWS, t)
    wba = jnp.concatenate([w_br_a.reshape(A_HEADS, A_VDIM, D_MODEL),
                           jnp.zeros((A_HEADS, LANES - A_VDIM, D_MODEL), w_br_a.dtype)], axis=1)
    wba = wba.reshape(A_HEADS * LANES, D_MODEL).astype(BF16)
    row = lambda n: pl.BlockSpec((tm, n), lambda i: (i, 0))
    return pl.pallas_call(
        _merge_kernel,
        out_shape=jax.ShapeDtypeStruct((t, D_MODEL), F32),
        grid=(t // tm,),
        in_specs=[row(M_WIDTH), row(A_HEADS * LANES), row(D_MODEL), row(D_MODEL), row(D_MODEL),
                  _const_spec((M_WIDTH, D_MODEL)), _const_spec((A_HEADS * LANES, D_MODEL)),
                  _const_spec((D_MODEL, D_MODEL)), _const_spec((1, D_MODEL)), _const_spec((1, D_MODEL))],
        out_specs=row(D_MODEL),
        compiler_params=_cparams(("parallel",)),
        name="merge_deepnorm1",
    )(ym, oa, gm, ga, x2, w_br_m.astype(BF16), wba, w_mix_out.astype(BF16), ln_g.reshape(1, -1), ln_b.reshape(1, -1))


def _kv_kernel(m_ref, w_ref, o_ref):
    o_ref[...] = jnp.dot(m_ref[...].astype(BF16), w_ref[...], preferred_element_type=F32).astype(BF16)


def _mem_kv(mem2, w_ckv):
    r = mem2.shape[0]
    tm = min(512, r)
    return pl.pallas_call(
        _kv_kernel,
        out_shape=jax.ShapeDtypeStruct((r, 2 * D_MODEL), BF16),
        grid=(r // tm,),
        in_specs=[pl.BlockSpec((tm, D_MODEL), lambda i: (i, 0)), _const_spec((D_MODEL, 2 * D_MODEL))],
        out_specs=pl.BlockSpec((tm, 2 * D_MODEL), lambda i: (i, 0)),
        compiler_params=_cparams(("parallel",)),
        name="memory_kv",
    )(mem2, w_ckv.astype(BF16))


def _xattn_kernel(h_ref, kv_ref, wq_ref, wo_ref, g_ref, b_ref, wr_ref, br_ref, h2_ref, idx_ref, gate_ref):
    h1 = h_ref[...]
    q = jnp.dot(h1.astype(BF16), wq_ref[...], preferred_element_type=F32).astype(BF16)
    scale = X_HEAD_DIM ** -0.5
    outs = []
    for hd in range(X_HEADS):
        ks = slice(hd * X_HEAD_DIM, (hd + 1) * X_HEAD_DIM)
        vs = slice(D_MODEL + hd * X_HEAD_DIM, D_MODEL + (hd + 1) * X_HEAD_DIM)
        sc = lax.dot_general(q[:, ks], kv_ref[:, ks], (((1,), (1,)), ((), ())), preferred_element_type=F32) * scale
        sc = sc - jnp.max(sc, axis=1, keepdims=True)
        p = jnp.exp(sc)
        p = p / jnp.sum(p, axis=1, keepdims=True)
        outs.append(jnp.dot(p.astype(BF16), kv_ref[:, vs], preferred_element_type=F32).astype(BF16))
    o = jnp.concatenate(outs, axis=1)
    att = jnp.dot(o, wo_ref[...], preferred_element_type=F32)
    h2 = _layer_norm_rows(DN_ALPHA * h1 + att, g_ref[...], b_ref[...])
    h2_ref[...] = h2

    logits = jnp.dot(h2, wr_ref[...], preferred_element_type=F32, precision=lax.Precision.HIGHEST) + br_ref[...]
    tm = logits.shape[0]
    lane = lax.broadcasted_iota(I32, (tm, LANES), 1)
    lane_f = lane.astype(F32)
    work = jnp.where(lane < N_EXPERTS, logits, -jnp.inf)
    vals, idxs = [], []
    for _ in range(TOP_K):
        mx = jnp.max(work, axis=1, keepdims=True)
        ix = jnp.min(jnp.where(work == mx, lane_f, float(LANES)), axis=1, keepdims=True)
        vals.append(mx)
        idxs.append(ix)
        work = jnp.where(lane_f == ix, -jnp.inf, work)
    es = [jnp.exp(vv - vals[0]) for vv in vals]
    tot = es[0] + es[1] + es[2] + es[3]
    idx_slab = jnp.zeros((tm, LANES), F32)
    gate_slab = jnp.zeros((tm, LANES), F32)
    for kk in range(TOP_K):
        idx_slab = jnp.where(lane == kk, idxs[kk], idx_slab)
        gate_slab = jnp.where(lane == kk, es[kk] / tot, gate_slab)
    idx_ref[...] = idx_slab.astype(I32)
    gate_ref[...] = gate_slab


def _cross_attention_router(h1, kv, bsz, s, w_cq, w_co, ln_g, ln_b, w_router, b_router):
    t = h1.shape[0]
    tm = min(XATT_ROWS, s)
    ns = s // tm
    n_mem = kv.shape[0] // bsz
    wr = jnp.concatenate([w_router, jnp.zeros((D_MODEL, LANES - N_EXPERTS), F32)], axis=1)
    br = jnp.concatenate([b_router, jnp.zeros((LANES - N_EXPERTS,), F32)]).reshape(1, LANES)
    row = lambda n: pl.BlockSpec((tm, n), lambda b, i: (b * ns + i, 0))
    return pl.pallas_call(
        _xattn_kernel,
        out_shape=[jax.ShapeDtypeStruct((t, D_MODEL), F32), jax.ShapeDtypeStruct((t, LANES), I32),
                   jax.ShapeDtypeStruct((t, LANES), F32)],
        grid=(bsz, ns),
        in_specs=[row(D_MODEL), pl.BlockSpec((n_mem, 2 * D_MODEL), lambda b, i: (b, 0)),
                  _const_spec((D_MODEL, D_MODEL)), _const_spec((D_MODEL, D_MODEL)),
                  _const_spec((1, D_MODEL)), _const_spec((1, D_MODEL)),
                  _const_spec((D_MODEL, LANES)), _const_spec((1, LANES))],
        out_specs=[row(D_MODEL), row(LANES), row(LANES)],
        compiler_params=_cparams(("parallel", "parallel")),
        name="cross_attention_router",
    )(h1, kv, w_cq.astype(BF16), w_co.astype(BF16), ln_g.reshape(1, -1), ln_b.reshape(1, -1), wr, br)


def _rank_kernel(idx_ref, ltri_ref, rank_ref, cnt_ref, carry_s):
    @pl.when(pl.program_id(0) == 0)
    def _():
        carry_s[...] = jnp.zeros_like(carry_s)

    idx = idx_ref[...]
    tm = idx.shape[0]
    lane = lax.broadcasted_iota(I32, (tm, LANES), 1)
    onehots = [(lane == idx[:, kk:kk + 1]).astype(F32) for kk in range(TOP_K)]
    sel = onehots[0] + onehots[1] + onehots[2] + onehots[3]
    before = jnp.dot(ltri_ref[...], sel.astype(BF16), preferred_element_type=F32) + carry_s[0:1, :]
    rank_slab = jnp.zeros((tm, LANES), F32)
    for kk in range(TOP_K):
        r = jnp.sum(onehots[kk] * before, axis=1, keepdims=True)
        rank_slab = jnp.where(lane == kk, r, rank_slab)
    rank_ref[...] = rank_slab.astype(I32)
    carry_s[0:1, :] = carry_s[0:1, :] + jnp.sum(sel, axis=0, keepdims=True)
    cnt_ref[...] = jnp.broadcast_to(carry_s[0:1, :], cnt_ref.shape).astype(I32)


def _routing_ranks(idx_slab):
    t = idx_slab.shape[0]
    tm = min(RANK_ROWS, t)
    ltri = jnp.tril(jnp.ones((tm, tm), BF16), k=-1)
    return pl.pallas_call(
        _rank_kernel,
        out_shape=[jax.ShapeDtypeStruct((t, LANES), I32), jax.ShapeDtypeStruct((8, LANES), I32)],
        grid=(t // tm,),
        in_specs=[pl.BlockSpec((tm, LANES), lambda i: (i, 0)), _const_spec((tm, tm))],
        out_specs=[pl.BlockSpec((tm, LANES), lambda i: (i, 0)), _const_spec((8, LANES))],
        scratch_shapes=[pltpu.VMEM((8, LANES), F32)],
        compiler_params=_cparams(("arbitrary",)),
        name="routing_ranks",
    )(idx_slab, ltri)


def _dest_kernel(idx_ref, rank_ref, pstart_ref, dest_ref):
    idx = idx_ref[...]
    tm = idx.shape[0]
    lane = lax.broadcasted_iota(I32, (tm, LANES), 1)
    dest = rank_ref[...].astype(F32)
    pstart = pstart_ref[0:1, :].astype(F32)
    for kk in range(TOP_K):
        start = jnp.sum(jnp.where(lane == idx[:, kk:kk + 1], pstart, 0.0), axis=1, keepdims=True)
        dest = dest + jnp.where(lane == kk, start, 0.0)
    dest_ref[...] = dest.astype(I32)


def _dest_rows(idx_slab, rank_slab, pstart_row):
    t = idx_slab.shape[0]
    tm = min(RANK_ROWS, t)
    return pl.pallas_call(
        _dest_kernel,
        out_shape=jax.ShapeDtypeStruct((t, LANES), I32),
        grid=(t // tm,),
        in_specs=[pl.BlockSpec((tm, LANES), lambda i: (i, 0)), pl.BlockSpec((tm, LANES), lambda i: (i, 0)),
                  _const_spec((8, LANES))],
        out_specs=pl.BlockSpec((tm, LANES), lambda i: (i, 0)),
        compiler_params=_cparams(("parallel",)),
        name="routing_dest",
    )(idx_slab, rank_slab, pstart_row)


def _dispatch_kernel(dest_ref, x_ref, buf_in, buf_hbm, sem):
    del buf_in
    n = dest_ref.shape[-1]

    def issue(j, c):
        d = dest_ref[0, j]
        tok = j // TOP_K
        pltpu.make_async_copy(x_ref.at[pl.ds(tok, 1), :], buf_hbm.at[pl.ds(d, 1), :], sem).start()
        return c

    lax.fori_loop(0, n, issue, 0, unroll=8)

    def drain(j, c):
        pltpu.make_async_copy(x_ref.at[pl.ds(0, 1), :], buf_hbm.at[pl.ds(0, 1), :], sem).wait()
        return c

    lax.fori_loop(0, n, drain, 0, unroll=8)


def _dispatch(h2, dest_flat, n_rows):
    t = h2.shape[0]
    tt = min(DISPATCH_TOKENS, t)
    n = tt * TOP_K
    dest3 = dest_flat.reshape(t // tt, 1, n)
    buf0 = jnp.zeros((n_rows, D_MODEL), h2.dtype)
    return pl.pallas_call(
        _dispatch_kernel,
        out_shape=jax.ShapeDtypeStruct((n_rows, D_MODEL), h2.dtype),
        grid=(t // tt,),
        in_specs=[pl.BlockSpec((None, 1, n), lambda i: (i, 0, 0), memory_space=pltpu.SMEM),
                  pl.BlockSpec((tt, D_MODEL), lambda i: (i, 0)), pl.BlockSpec(memory_space=pl.ANY)],
        out_specs=pl.BlockSpec(memory_space=pl.ANY),
        scratch_shapes=[pltpu.SemaphoreType.DMA],
        input_output_aliases={2: 0},
        compiler_params=_cparams(("arbitrary",)),
        name="moe_dispatch",
    )(dest3, h2, buf0)


def _expert_kernel(blk_e_ref, nblk_ref, x_ref, wg_ref, wl_ref, bg_ref, bl_ref, wd_ref, bd_ref, y_ref):
    @pl.when(pl.program_id(0) < nblk_ref[0])
    def _():
        xb = x_ref[...].astype(BF16)
        glu = jnp.dot(xb, wg_ref[...], preferred_element_type=F32) + bg_ref[...]
        lin = jnp.dot(xb, wl_ref[...], preferred_element_type=F32) + bl_ref[...]
        glu = jnp.minimum(glu, SWIGLU_LIMIT)
        lin = jnp.clip(lin, -SWIGLU_LIMIT, SWIGLU_LIMIT)
        act = glu * jax.nn.sigmoid(SWIGLU_ALPHA * glu) * (lin + 1.0)
        y_ref[...] = jnp.dot(act.astype(BF16), wd_ref[...], preferred_element_type=F32) + bd_ref[...]

    @pl.when(pl.program_id(0) >= nblk_ref[0])
    def _():
        y_ref[...] = jnp.zeros_like(y_ref)


def _split_kernel(w_ref, perm_ref, g_ref, l_ref):
    half = g_ref.shape[-1]
    sorted_cols = jnp.dot(w_ref[...].astype(BF16), perm_ref[...], preferred_element_type=F32)
    g_ref[...] = sorted_cols[:, :half].astype(BF16)
    l_ref[...] = sorted_cols[:, half:].astype(BF16)


def _split_gate_up(w_gu):
    e, d, n2 = w_gu.shape
    cw = 512
    src = np.concatenate([np.arange(0, cw, 2), np.arange(1, cw, 2)])
    perm = jnp.asarray(np.eye(cw, dtype=np.float32)[:, src], BF16)
    return pl.pallas_call(
        _split_kernel,
        out_shape=[jax.ShapeDtypeStruct((e, d, n2 // 2), BF16), jax.ShapeDtypeStruct((e, d, n2 // 2), BF16)],
        grid=(e, n2 // cw),
        in_specs=[pl.BlockSpec((None, d, cw), lambda i, c: (i, 0, c)), _const_spec((cw, cw))],
        out_specs=[pl.BlockSpec((None, d, cw // 2), lambda i, c: (i, 0, c)),
                   pl.BlockSpec((None, d, cw // 2), lambda i, c: (i, 0, c))],
        compiler_params=_cparams(("parallel", "parallel")),
        name="split_gate_up",
    )(w_gu, perm)


def _experts(xb, blk_e, nblk, w_glu, w_lin, b_glu, b_lin, w_dn, b_dn):
    n_rows = xb.shape[0]
    bm = EXPERT_ROWS
    grid_spec = pltpu.PrefetchScalarGridSpec(
        num_scalar_prefetch=2,
        grid=(n_rows // bm,),
        in_specs=[pl.BlockSpec((bm, D_MODEL), lambda i, be, nb: (i, 0)),
                  pl.BlockSpec((None, D_MODEL, D_EXPERT), lambda i, be, nb: (be[i], 0, 0)),
                  pl.BlockSpec((None, D_MODEL, D_EXPERT), lambda i, be, nb: (be[i], 0, 0)),
                  pl.BlockSpec((None, 1, D_EXPERT), lambda i, be, nb: (be[i], 0, 0)),
                  pl.BlockSpec((None, 1, D_EXPERT), lambda i, be, nb: (be[i], 0, 0)),
                  pl.BlockSpec((None, D_EXPERT, D_MODEL), lambda i, be, nb: (be[i], 0, 0)),
                  pl.BlockSpec((None, 1, D_MODEL), lambda i, be, nb: (be[i], 0, 0))],
        out_specs=pl.BlockSpec((bm, D_MODEL), lambda i, be, nb: (i, 0)),
    )
    return pl.pallas_call(
        _expert_kernel,
        out_shape=jax.ShapeDtypeStruct((n_rows, D_MODEL), F32),
        grid_spec=grid_spec,
        compiler_params=_cparams(("arbitrary",)),
        name="moe_experts",
    )(blk_e, nblk, xb, w_glu, w_lin, b_glu, b_lin, w_dn, b_dn)


def _combine_kernel(dest_ref, dest_next_ref, h_ref, gate_ref, y_hbm, g_ref, b_ref, o_ref, rows_s, sem):
    tt = h_ref.shape[0]
    n = tt * TOP_K
    i = pl.program_id(0)
    slot = i % 2

    def row_copy(d, sl, kk, tok):
        return pltpu.make_async_copy(y_hbm.at[pl.ds(d, 1), :], rows_s.at[sl, kk, pl.ds(tok, 1), :], sem.at[sl])

    def issue(idx_ref, sl):
        def one(j, c):
            row_copy(idx_ref[0, j], sl, j % TOP_K, j // TOP_K).start()
            return c
        lax.fori_loop(0, n, one, 0, unroll=8)

    @pl.when(i == 0)
    def _():
        issue(dest_ref, 0)

    @pl.when(i + 1 < pl.num_programs(0))
    def _():
        issue(dest_next_ref, 1 - slot)

    def drain(j, c):
        row_copy(0, slot, 0, 0).wait()
        return c

    lax.fori_loop(0, n, drain, 0, unroll=8)
    gates = gate_ref[...]
    acc = DN_ALPHA * h_ref[...]
    for kk in range(TOP_K):
        acc = acc + gates[:, kk:kk + 1] * rows_s[slot, kk]
    o_ref[...] = _layer_norm_rows(acc, g_ref[...], b_ref[...])


def _combine(h2, gate_slab, dest_flat, yb, ln_g, ln_b):
    t = h2.shape[0]
    tt = min(COMBINE_TOKENS, t)
    n = tt * TOP_K
    steps = t // tt
    dest3 = dest_flat.reshape(steps, 1, n)
    row = lambda m: pl.BlockSpec((tt, m), lambda i: (i, 0))
    return pl.pallas_call(
        _combine_kernel,
        out_shape=jax.ShapeDtypeStruct((t, D_MODEL), F32),
        grid=(steps,),
        in_specs=[pl.BlockSpec((None, 1, n), lambda i: (i, 0, 0), memory_space=pltpu.SMEM),
                  pl.BlockSpec((None, 1, n), lambda i: (jnp.minimum(i + 1, steps - 1), 0, 0),
                               memory_space=pltpu.SMEM),
                  row(D_MODEL), row(LANES), pl.BlockSpec(memory_space=pl.ANY),
                  _const_spec((1, D_MODEL)), _const_spec((1, D_MODEL))],
        out_specs=row(D_MODEL),
        scratch_shapes=[pltpu.VMEM((2, TOP_K, tt, D_MODEL), F32), pltpu.SemaphoreType.DMA((2,))],
        compiler_params=_cparams(("arbitrary",)),
        name="moe_combine",
    )(dest3, dest3, h2, gate_slab, yb, ln_g.reshape(1, -1), ln_b.reshape(1, -1))


def _moe(h2, idx_slab, gate_slab, w_gu, b_gu, w_dn, b_dn, ln_g, ln_b):
    t = h2.shape[0]
    bm = EXPERT_ROWS
    n_assign = t * TOP_K
    n_rows = n_assign + N_EXPERTS * bm
    n_blocks = n_rows // bm

    rank_slab, cnt = _routing_ranks(idx_slab)
    counts = cnt[0, :N_EXPERTS]
    padded = (counts + bm - 1) // bm * bm
    pends = jnp.cumsum(padded)
    pstarts = pends - padded
    pstart_row = jnp.zeros((8, LANES), I32).at[:, :N_EXPERTS].set(pstarts[None, :])
    blk_start = jnp.arange(n_blocks, dtype=I32) * bm
    blk_e = jnp.minimum(jnp.sum((pends[None, :] <= blk_start[:, None]).astype(I32), axis=1), N_EXPERTS - 1)
    nblk = (pends[-1] // bm).astype(I32).reshape(1)
    dest_slab = _dest_rows(idx_slab, rank_slab, pstart_row)
    dest_flat = dest_slab[:, :TOP_K].reshape(-1)

    xb = _dispatch(h2, dest_flat, n_rows)
    w_glu, w_lin = _split_gate_up(w_gu)
    b_glu = b_gu[:, None, 0::2]
    b_lin = b_gu[:, None, 1::2]
    yb = _experts(xb, blk_e, nblk, w_glu, w_lin, b_glu, b_lin, w_dn.astype(BF16), b_dn[:, None, :])
    return _combine(h2, gate_slab, dest_flat, yb, ln_g, ln_b)


def kernel(x, mem, positions, w_in, b_igate, b_fgate, conv_w, conv_b, w_mq, w_mk, g_mhead, w_mskip, g_qlat, g_kvlat,
           w_uq, w_ukv, w_br_m, w_br_a, w_mix_out, ln1_g, ln1_b, w_cq, w_ckv, w_co, ln2_g, ln2_b, w_router, b_router,
           w_gu, b_gu, w_dn, b_dn, ln3_g, ln3_b):
    bsz, s, d = x.shape
    t = bsz * s
    h = x.reshape(t, d)
    pos2 = positions.reshape(t, 1)
    for l in range(DEPTH):
        q, k, v, xm, vm, op, gm, ga, gate = _input_projection(h, pos2, w_in[l], g_qlat[l], g_kvlat[l], w_uq[l],
                                                              w_ukv[l])
        ym = _mlstm_branch(xm, vm, op, gate, bsz, s, b_igate[l], b_fgate[l], conv_w[l], conv_b[l], w_mq[l], w_mk[l],
                           g_mhead[l], w_mskip[l])
        oa = _mla_attention(q, k, v, bsz, s)
        h1 = _merge(ym, oa, gm, ga, h, w_br_m[l], w_br_a[l], w_mix_out[l], ln1_g[l], ln1_b[l])
        kv = _mem_kv(mem.reshape(-1, d), w_ckv[l])
        h2, idx_slab, gate_slab = _cross_attention_router(h1, kv, bsz, s, w_cq[l], w_co[l], ln2_g[l], ln2_b[l],
                                                          w_router[l], b_router[l])
        h = _moe(h2, idx_slab, gate_slab, w_gu[l], b_gu[l], w_dn[l], b_dn[l], ln3_g[l], ln3_b[l])
    return h.reshape(bsz, s, d)
```

```python
import functools

import numpy as np
import jax
import jax.numpy as jnp
from jax import lax
from jax.experimental import pallas as pl
from jax.experimental.pallas import tpu as pltpu

F32 = jnp.float32
BF16 = jnp.bfloat16
I32 = jnp.int32

D_MODEL = 1024
N_MEM = 256
M_HEADS = 4
M_HEAD_DIM = 128
M_WIDTH = M_HEADS * M_HEAD_DIM
M_CONV = 4
A_HEADS = 8
A_NOPE = 64
A_ROPE = 32
A_QK = A_NOPE + A_ROPE
A_VDIM = 64
A_Q_RANK = 256
A_KV_RANK = 128
ROPE_THETA = 10000.0
X_HEADS = 4
X_HEAD_DIM = D_MODEL // X_HEADS
N_EXPERTS = 32
TOP_K = 4
D_EXPERT = D_MODEL
SWIGLU_ALPHA = 1.702
SWIGLU_LIMIT = 7.0
DEPTH = 1
DN_ALPHA = (2.0 * DEPTH) ** 0.25
EPS = 1e-5
IN_SPLITS = (A_Q_RANK, A_KV_RANK, A_ROPE, M_WIDTH, M_WIDTH, M_WIDTH, M_HEADS, M_HEADS, D_MODEL, D_MODEL)
IN_OFFSETS = tuple(int(v) for v in np.cumsum((0,) + IN_SPLITS))

LANES = 128
VMEM_LIMIT = 56 * 1024 * 1024

PROJ_ROWS = 512
M_CHUNK_ROWS = 128
ATT_Q = 256
MERGE_ROWS = 512
XATT_ROWS = 512
RANK_ROWS = 512
EXPERT_ROWS = 256
DISPATCH_TOKENS = 256
COMBINE_TOKENS = 64
NEG_BIG = -1e30
LOG2_E = 1.4426950408889634
ROW_TILES = D_MODEL // LANES

C_QLAT = 0
C_KVLAT = C_QLAT + A_Q_RANK
C_KR = C_KVLAT + A_KV_RANK
C_KRS = C_KR + LANES
C_GATE = C_KRS + LANES
C_XM = C_GATE + LANES
C_VM = C_XM + M_WIDTH
C_OP = C_VM + M_WIDTH
C_GM = C_OP + M_WIDTH
C_GA = C_GM + D_MODEL
C_END = C_GA + D_MODEL


def _cparams(sem, vmem=VMEM_LIMIT):
    return pltpu.CompilerParams(dimension_semantics=sem, vmem_limit_bytes=vmem)


def _const_spec(shape):
    nd = len(shape)
    return pl.BlockSpec(shape, lambda *a: (0,) * nd)


def _layer_norm_rows(v, g, b):
    mu = jnp.mean(v, axis=-1, keepdims=True)
    d = v - mu
    var = jnp.mean(d * d, axis=-1, keepdims=True)
    return d * lax.rsqrt(var + EPS) * g + b


def _proj_kernel(x_ref, pos_ref, w_ref, gq_ref, gkv_ref, wuq_ref, wuqs_ref, wuk_ref, wuv_ref, invf_ref,
                 q_ref, k_ref, v_ref, xm_ref, vm_ref, op_ref, gm_ref, ga_ref, gate_ref):
    xb = x_ref[...].astype(BF16)

    def mm(lo, n):
        return jnp.dot(xb, w_ref[:, lo:lo + n], preferred_element_type=F32)

    xm_ref[...] = mm(C_XM, M_WIDTH).astype(BF16)
    vm_ref[...] = mm(C_VM, M_WIDTH).astype(BF16)
    op_ref[...] = mm(C_OP, M_WIDTH).astype(BF16)
    gm_ref[...] = mm(C_GM, D_MODEL).astype(BF16)
    ga_ref[...] = mm(C_GA, D_MODEL).astype(BF16)
    gate_ref[...] = mm(C_GATE, LANES)

    ang = pos_ref[...].astype(F32) * invf_ref[...]
    cos = jnp.cos(ang)
    sin = jnp.sin(ang)

    q_lat = mm(C_QLAT, A_Q_RANK)
    qn = (q_lat * lax.rsqrt(jnp.mean(q_lat * q_lat, axis=-1, keepdims=True) + EPS) * gq_ref[...]).astype(BF16)
    q = jnp.dot(qn, wuq_ref[...], preferred_element_type=F32)
    qs = jnp.dot(qn, wuqs_ref[...], preferred_element_type=F32)
    kv_lat = mm(C_KVLAT, A_KV_RANK)
    kvn = (kv_lat * lax.rsqrt(jnp.mean(kv_lat * kv_lat, axis=-1, keepdims=True) + EPS) * gkv_ref[...]).astype(BF16)
    kn = jnp.dot(kvn, wuk_ref[...], preferred_element_type=F32)
    lane = lax.broadcasted_iota(I32, (1, A_HEADS * LANES), 1)
    ones_lane = (lane % LANES == A_VDIM).astype(F32)
    v_ref[...] = (jnp.dot(kvn, wuv_ref[...], preferred_element_type=F32) + ones_lane).astype(BF16)
    k_pe = mm(C_KR, LANES) * cos + mm(C_KRS, LANES) * sin
    scale = A_QK ** -0.5 * LOG2_E
    for h in range(A_HEADS):
        sl = slice(h * LANES, (h + 1) * LANES)
        q_ref[:, sl] = ((q[:, sl] * cos + qs[:, sl] * sin) * scale).astype(BF16)
        k_ref[:, sl] = (kn[:, sl] + k_pe).astype(BF16)


def _proj_weights(w_in, w_uq, w_ukv):
    o = IN_OFFSETS
    half = A_ROPE // 2
    w_q, w_kv, w_kr = w_in[:, o[0]:o[1]], w_in[:, o[1]:o[2]], w_in[:, o[2]:o[3]]
    w_xm, w_vm, w_op = w_in[:, o[3]:o[4]], w_in[:, o[4]:o[5]], w_in[:, o[5]:o[6]]
    w_i, w_f, w_gm, w_ga = w_in[:, o[6]:o[7]], w_in[:, o[7]:o[8]], w_in[:, o[8]:o[9]], w_in[:, o[9]:o[10]]
    d = w_in.shape[0]
    z = lambda n: jnp.zeros((d, n), w_in.dtype)
    kr = jnp.concatenate([z(A_NOPE), w_kr, z(LANES - A_QK)], axis=1)
    krs = jnp.concatenate([z(A_NOPE), -w_kr[:, half:], w_kr[:, :half], z(LANES - A_QK)], axis=1)
    gate = jnp.concatenate([w_i, w_f, z(LANES - 2 * M_HEADS)], axis=1)
    w_r = jnp.concatenate([w_q, w_kv, kr, krs, gate, w_xm, w_vm, w_op, w_gm, w_ga], axis=1).astype(BF16)

    uq = w_uq.reshape(A_Q_RANK, A_HEADS, A_QK)
    zq = jnp.zeros((A_Q_RANK, A_HEADS, LANES - A_QK), w_uq.dtype)
    zn = jnp.zeros((A_Q_RANK, A_HEADS, A_NOPE), w_uq.dtype)
    uq_pad = jnp.concatenate([uq, zq], axis=-1).reshape(A_Q_RANK, A_HEADS * LANES).astype(BF16)
    uqs_pad = jnp.concatenate([zn, -uq[..., A_NOPE + half:], uq[..., A_NOPE:A_NOPE + half], zq],
                              axis=-1).reshape(A_Q_RANK, A_HEADS * LANES).astype(BF16)
    ukv = w_ukv.reshape(A_KV_RANK, A_HEADS, A_NOPE + A_VDIM)
    zk = jnp.zeros((A_KV_RANK, A_HEADS, LANES - A_NOPE), w_ukv.dtype)
    uk_pad = jnp.concatenate([ukv[..., :A_NOPE], zk], axis=-1).reshape(A_KV_RANK, A_HEADS * LANES).astype(BF16)
    zv = jnp.zeros((A_KV_RANK, A_HEADS, LANES - A_VDIM), w_ukv.dtype)
    uv_pad = jnp.concatenate([ukv[..., A_NOPE:], zv], axis=-1).reshape(A_KV_RANK, A_HEADS * LANES).astype(BF16)
    return w_r, uq_pad, uqs_pad, uk_pad, uv_pad


def _input_projection(x2, pos2, w_in, g_qlat, g_kvlat, w_uq, w_ukv):
    t = x2.shape[0]
    tm = min(PROJ_ROWS, t)
    w_r, uq_pad, uqs_pad, uk_pad, uv_pad = _proj_weights(w_in, w_uq, w_ukv)
    half = A_ROPE // 2
    inv_freq = ROPE_THETA ** (-jnp.arange(half, dtype=F32) / half)
    invf = jnp.concatenate([jnp.zeros((A_NOPE,), F32), inv_freq, inv_freq,
                            jnp.zeros((LANES - A_QK,), F32)]).reshape(1, LANES)
    hw = A_HEADS * LANES
    row = lambda n: pl.BlockSpec((tm, n), lambda i: (i, 0))
    outs = [
        jax.ShapeDtypeStruct((t, hw), BF16), jax.ShapeDtypeStruct((t, hw), BF16), jax.ShapeDtypeStruct((t, hw), BF16),
        jax.ShapeDtypeStruct((t, M_WIDTH), BF16), jax.ShapeDtypeStruct((t, M_WIDTH), BF16),
        jax.ShapeDtypeStruct((t, M_WIDTH), BF16),
        jax.ShapeDtypeStruct((t, D_MODEL), BF16), jax.ShapeDtypeStruct((t, D_MODEL), BF16),
        jax.ShapeDtypeStruct((t, LANES), F32),
    ]
    return pl.pallas_call(
        _proj_kernel,
        out_shape=outs,
        grid=(t // tm,),
        in_specs=[row(D_MODEL), row(1), _const_spec(w_r.shape), _const_spec((1, A_Q_RANK)),
                  _const_spec((1, A_KV_RANK)), _const_spec(uq_pad.shape), _const_spec(uqs_pad.shape),
                  _const_spec(uk_pad.shape), _const_spec(uv_pad.shape), _const_spec((1, LANES))],
        out_specs=[row(hw), row(hw), row(hw), row(M_WIDTH), row(M_WIDTH), row(M_WIDTH), row(D_MODEL), row(D_MODEL),
                   row(LANES)],
        compiler_params=_cparams(("parallel",)),
        name="input_projection",
    )(x2, pos2, w_r, g_qlat.reshape(1, -1), g_kvlat.reshape(1, -1), uq_pad, uqs_pad, uk_pad, uv_pad, invf)


def _log_sigmoid(v):
    return jnp.minimum(v, 0.0) - jnp.log1p(jnp.exp(-jnp.abs(v)))


def _mlstm_kernel(xm_ref, vm_ref, op_ref, gate_ref, convw_ref, convb_ref, wq_ref, wk_ref, gbias_ref, ghead_ref,
                  skip_ref, tril_ref, out_ref, xpad_s, ct_s, n_s, m_s):
    s = xm_ref.shape[0]
    lc = M_CHUNK_ROWS
    halo = 8
    xpad_s[0:halo, :] = jnp.zeros((halo, M_WIDTH), F32)
    xpad_s[halo:, :] = xm_ref[...].astype(F32)
    ct_s[...] = jnp.zeros_like(ct_s)
    n_s[...] = jnp.zeros_like(n_s)
    m_s[...] = jnp.zeros_like(m_s)
    rows = lax.broadcasted_iota(I32, (lc, lc), 0)
    cols = lax.broadcasted_iota(I32, (lc, lc), 1)
    causal = rows >= cols
    kscale = M_HEAD_DIM ** -0.5

    def chunk(c, carry):
        r0 = pl.multiple_of(c * lc, lc)
        win = xpad_s[pl.ds(r0, lc + halo), :]
        conv = convb_ref[...]
        for j in range(M_CONV):
            lo = halo - (M_CONV - 1) + j
            conv = conv + win[lo:lo + lc, :] * convw_ref[j:j + 1, :]
        xc = conv * jax.nn.sigmoid(conv)
        g = gate_ref[pl.ds(r0, lc), :] + gbias_ref[...]
        ls = _log_sigmoid(g)
        bc = jnp.dot(tril_ref[...], ls, preferred_element_type=F32, precision=lax.Precision.HIGHEST)
        g_t = g.T
        b_t = bc.T
        for h in range(M_HEADS):
            hs = slice(h * M_HEAD_DIM, (h + 1) * M_HEAD_DIM)
            ig_col = g[:, h:h + 1]
            ig_row = g_t[h:h + 1, :]
            b_col = bc[:, M_HEADS + h:M_HEADS + h + 1]
            b_row = b_t[M_HEADS + h:M_HEADS + h + 1, :]
            b_tot = bc[lc - 1:lc, M_HEADS + h:M_HEADS + h + 1]
            m_prev = m_s[h:h + 1, 0:1]
            n_prev = n_s[h:h + 1, :]
            ct_prev = ct_s[h]

            xh = xc[:, hs]
            xhb = xh.astype(BF16)
            q = jnp.dot(xhb, wq_ref[h], preferred_element_type=F32)
            k = jnp.dot(xhb, wk_ref[h], preferred_element_type=F32) * kscale
            qb = q.astype(BF16)
            kb = k.astype(BF16)
            v = vm_ref[pl.ds(r0, lc), hs]

            log_d = jnp.where(causal, b_col - b_row + ig_row, -jnp.inf)
            m_intra = jnp.max(log_d, axis=1, keepdims=True)
            g_inter = b_col + m_prev
            m = jnp.maximum(g_inter, m_intra)
            w_inter = jnp.exp(g_inter - m)
            dmat = jnp.exp(log_d - m)
            sc = lax.dot_general(qb, kb, (((1,), (1,)), ((), ())), preferred_element_type=F32) * dmat
            num = (w_inter * jnp.dot(qb, ct_prev.astype(BF16), preferred_element_type=F32)
                   + jnp.dot(sc.astype(BF16), v, preferred_element_type=F32))
            den = w_inter * jnp.sum(q * n_prev, axis=1, keepdims=True) + jnp.sum(sc, axis=1, keepdims=True)
            hh = num / jnp.maximum(jnp.abs(den), jnp.exp(-m))

            a = b_tot - b_col + ig_col
            m_chunk = jnp.max(a, axis=0, keepdims=True)
            kw = k * jnp.exp(a - m_chunk)
            ct_c = lax.dot_general(kw.astype(BF16), v, (((0,), (0,)), ((), ())), preferred_element_type=F32)
            n_c = jnp.sum(kw, axis=0, keepdims=True)
            m_new = jnp.maximum(b_tot + m_prev, m_chunk)
            s_old = jnp.exp(b_tot + m_prev - m_new)
            s_new = jnp.exp(m_chunk - m_new)
            ct_s[h] = s_old * ct_prev + s_new * ct_c
            n_s[h:h + 1, :] = s_old * n_prev + s_new * n_c
            m_s[h:h + 1, :] = jnp.broadcast_to(m_new, (1, LANES))

            mu = jnp.mean(hh, axis=-1, keepdims=True)
            dv = hh - mu
            var = jnp.mean(dv * dv, axis=-1, keepdims=True)
            hn = dv * lax.rsqrt(var + EPS) * ghead_ref[:, hs]
            y = jax.nn.sigmoid(op_ref[pl.ds(r0, lc), hs].astype(F32)) * (hn + skip_ref[:, hs] * xh)
            out_ref[pl.ds(r0, lc), hs] = y.astype(BF16)
        return carry

    lax.fori_loop(0, s // lc, chunk, 0)


def _mlstm_branch(xm, vm, op, gate, bsz, s, b_i, b_f, conv_w, conv_b, w_mq, w_mk, g_mhead, w_mskip):
    lc = M_CHUNK_ROWS
    gbias = jnp.concatenate([b_i, b_f, jnp.zeros((LANES - 2 * M_HEADS,), F32)]).reshape(1, LANES)
    tril = jnp.tril(jnp.ones((lc, lc), F32))
    seq = lambda n: pl.BlockSpec((s, n), lambda b: (b, 0))
    return pl.pallas_call(
        _mlstm_kernel,
        out_shape=jax.ShapeDtypeStruct((bsz * s, M_WIDTH), BF16),
        grid=(bsz,),
        in_specs=[seq(M_WIDTH), seq(M_WIDTH), seq(M_WIDTH), seq(LANES),
                  _const_spec((M_CONV, M_WIDTH)), _const_spec((1, M_WIDTH)),
                  _const_spec((M_HEADS, M_HEAD_DIM, M_HEAD_DIM)), _const_spec((M_HEADS, M_HEAD_DIM, M_HEAD_DIM)),
                  _const_spec((1, LANES)), _const_spec((1, M_WIDTH)), _const_spec((1, M_WIDTH)),
                  _const_spec((lc, lc))],
        out_specs=seq(M_WIDTH),
        scratch_shapes=[pltpu.VMEM((s + 8, M_WIDTH), F32), pltpu.VMEM((M_HEADS, M_HEAD_DIM, M_HEAD_DIM), F32),
                        pltpu.VMEM((8, LANES), F32), pltpu.VMEM((8, LANES), F32)],
        compiler_params=_cparams(("parallel",)),
        name="mlstm_branch",
    )(xm, vm, op, gate, conv_w, conv_b.reshape(1, -1), w_mq.astype(BF16), w_mk.astype(BF16), gbias,
      g_mhead.reshape(1, -1), w_mskip.reshape(1, -1), tril)


def _attn_kernel(q_ref, k_ref, v_ref, o_ref, s_scr, m_s, acc_s):
    tq = q_ref.shape[0]
    i = pl.program_id(1)
    rows = lax.broadcasted_iota(I32, (tq, tq), 0)
    cols = lax.broadcasted_iota(I32, (tq, tq), 1)

    def lane_tile_max(sc):
        out = sc[:, :LANES]
        for t in range(1, tq // LANES):
            out = jnp.maximum(out, sc[:, t * LANES:(t + 1) * LANES])
        return out

    heads = [slice(h * LANES, (h + 1) * LANES) for h in range(A_HEADS)]
    m_s[...] = jnp.full(m_s.shape, NEG_BIG, F32)
    acc_s[...] = jnp.zeros_like(acc_s)

    def pass_a(j, diagonal):
        k0 = pl.multiple_of(j * tq, tq)
        for h, hs in enumerate(heads):
            sc = lax.dot_general(q_ref[:, hs], k_ref[pl.ds(k0, tq), hs], (((1,), (1,)), ((), ())),
                                 preferred_element_type=F32)
            if diagonal:
                sc = jnp.where(cols <= rows, sc, NEG_BIG)
            s_scr[h, j] = sc
            m_s[h] = jnp.maximum(m_s[h], lane_tile_max(sc))

    def body_a(j, c):
        pass_a(j, False)
        return c

    lax.fori_loop(0, i, body_a, 0)
    pass_a(i, True)
    for h in range(A_HEADS):
        m_s[h] = jnp.broadcast_to(jnp.max(m_s[h], axis=1, keepdims=True), (tq, LANES))

    def body_b(j, c):
        k0 = pl.multiple_of(j * tq, tq)
        for h, hs in enumerate(heads):
            m_row = m_s[h]
            p = jnp.exp2(s_scr[h, j] - jnp.concatenate([m_row] * (tq // LANES), axis=1)).astype(BF16)
            acc_s[h] += jnp.dot(p, v_ref[pl.ds(k0, tq), hs], preferred_element_type=F32)
        return c

    lax.fori_loop(0, i + 1, body_b, 0)
    for h, hs in enumerate(heads):
        acc = acc_s[h]
        o_ref[:, hs] = (acc / acc[:, A_VDIM:A_VDIM + 1]).astype(BF16)


def _mla_attention(q, k, v, bsz, s):
    tq = min(ATT_Q, s)
    nq = s // tq
    hw = A_HEADS * LANES
    return pl.pallas_call(
        _attn_kernel,
        out_shape=jax.ShapeDtypeStruct(q.shape, BF16),
        grid=(bsz, nq),
        in_specs=[pl.BlockSpec((tq, hw), lambda b, i: (b * nq + i, 0)),
                  pl.BlockSpec((s, hw), lambda b, i: (b, 0)),
                  pl.BlockSpec((s, hw), lambda b, i: (b, 0))],
        out_specs=pl.BlockSpec((tq, hw), lambda b, i: (b * nq + i, 0)),
        scratch_shapes=[pltpu.VMEM((A_HEADS, nq, tq, tq), F32), pltpu.VMEM((A_HEADS, tq, LANES), F32),
                        pltpu.VMEM((A_HEADS, tq, LANES), F32)],
        compiler_params=_cparams(("parallel", "arbitrary")),
        name="mla_attention",
    )(q, k, v)


def _merge_kernel(ym_ref, oa_ref, gm_ref, ga_ref, x_ref, wbm_ref, wba_ref, wmix_ref, g_ref, b_ref, h_ref):
    y_m = jnp.dot(ym_ref[...], wbm_ref[...], preferred_element_type=F32)
    y_a = jnp.dot(oa_ref[...], wba_ref[...], preferred_element_type=F32)
    mixed = jax.nn.sigmoid(gm_ref[...].astype(F32)) * y_m + jax.nn.sigmoid(ga_ref[...].astype(F32)) * y_a
    mix = jnp.dot(mixed.astype(BF16), wmix_ref[...], preferred_element_type=F32)
    h_ref[...] = _layer_norm_rows(DN_ALPHA * x_ref[...] + mix, g_ref[...], b_ref[...])


def _merge(ym, oa, gm, ga, x2, w_br_m, w_br_a, w_mix_out, ln_g, ln_b):
    t = x2.shape[0]
    tm = min(MERGE_ROWS, t)
    wba = jnp.concatenate([w_br_a.reshape(A_HEADS, A_VDIM, D_MODEL),
                           jnp.zeros((A_HEADS, LANES - A_VDIM, D_MODEL), w_br_a.dtype)], axis=1)
    wba = wba.reshape(A_HEADS * LANES, D_MODEL).astype(BF16)
    row = lambda n: pl.BlockSpec((tm, n), lambda i: (i, 0))
    return pl.pallas_call(
        _merge_kernel,
        out_shape=jax.ShapeDtypeStruct((t, D_MODEL), F32),
        grid=(t // tm,),
        in_specs=[row(M_WIDTH), row(A_HEADS * LANES), row(D_MODEL), row(D_MODEL), row(D_MODEL),
                  _const_spec((M_WIDTH, D_MODEL)), _const_spec((A_HEADS * LANES, D_MODEL)),
                  _const_spec((D_MODEL, D_MODEL)), _const_spec((1, D_MODEL)), _const_spec((1, D_MODEL))],
        out_specs=row(D_MODEL),
        compiler_params=_cparams(("parallel",)),
        name="merge_deepnorm1",
    )(ym, oa, gm, ga, x2, w_br_m.astype(BF16), wba, w_mix_out.astype(BF16), ln_g.reshape(1, -1), ln_b.reshape(1, -1))


def _kv_kernel(m_ref, w_ref, o_ref):
    o_ref[...] = jnp.dot(m_ref[...].astype(BF16), w_ref[...], preferred_element_type=F32).astype(BF16)


def _mem_kv(mem2, w_ckv):
    r = mem2.shape[0]
    tm = min(512, r)
    return pl.pallas_call(
        _kv_kernel,
        out_shape=jax.ShapeDtypeStruct((r, 2 * D_MODEL), BF16),
        grid=(r // tm,),
        in_specs=[pl.BlockSpec((tm, D_MODEL), lambda i: (i, 0)), _const_spec((D_MODEL, 2 * D_MODEL))],
        out_specs=pl.BlockSpec((tm, 2 * D_MODEL), lambda i: (i, 0)),
        compiler_params=_cparams(("parallel",)),
        name="memory_kv",
    )(mem2, w_ckv.astype(BF16))


def _xattn_kernel(h_ref, kv_ref, wq_ref, wo_ref, g_ref, b_ref, wr_ref, br_ref, h2_ref, idx_ref, gate_ref):
    h1 = h_ref[...]
    q = jnp.dot(h1.astype(BF16), wq_ref[...], preferred_element_type=F32).astype(BF16)
    scale = X_HEAD_DIM ** -0.5
    outs = []
    for hd in range(X_HEADS):
        ks = slice(hd * X_HEAD_DIM, (hd + 1) * X_HEAD_DIM)
        vs = slice(D_MODEL + hd * X_HEAD_DIM, D_MODEL + (hd + 1) * X_HEAD_DIM)
        sc = lax.dot_general(q[:, ks], kv_ref[:, ks], (((1,), (1,)), ((), ())), preferred_element_type=F32) * scale
        sc = sc - jnp.max(sc, axis=1, keepdims=True)
        p = jnp.exp(sc)
        p = p / jnp.sum(p, axis=1, keepdims=True)
        outs.append(jnp.dot(p.astype(BF16), kv_ref[:, vs], preferred_element_type=F32).astype(BF16))
    o = jnp.concatenate(outs, axis=1)
    att = jnp.dot(o, wo_ref[...], preferred_element_type=F32)
    h2 = _layer_norm_rows(DN_ALPHA * h1 + att, g_ref[...], b_ref[...])
    for c in range(ROW_TILES):
        h2_ref[pl.ds(c, h2.shape[0], stride=ROW_TILES), :] = h2[:, c * LANES:(c + 1) * LANES]

    h2_hi = h2.astype(BF16)
    h2_lo = (h2 - h2_hi.astype(F32)).astype(BF16)
    hi_prod = jnp.dot(h2_hi, wr_ref[...], preferred_element_type=F32)
    lo_prod = jnp.dot(h2_lo, wr_ref[:, :LANES], preferred_element_type=F32)
    logits = hi_prod[:, :LANES] + (hi_prod[:, LANES:] + lo_prod) + br_ref[...]
    tm = logits.shape[0]
    lane = lax.broadcasted_iota(I32, (tm, LANES), 1)
    lane_f = lane.astype(F32)
    work = jnp.where(lane < N_EXPERTS, logits, -jnp.inf)
    vals, idxs = [], []
    for _ in range(TOP_K):
        mx = jnp.max(work, axis=1, keepdims=True)
        ix = jnp.min(jnp.where(work == mx, lane_f, float(LANES)), axis=1, keepdims=True)
        vals.append(mx)
        idxs.append(ix)
        work = jnp.where(lane_f == ix, -jnp.inf, work)
    es = [jnp.exp(vv - vals[0]) for vv in vals]
    tot = es[0] + es[1] + es[2] + es[3]
    idx_slab = jnp.zeros((tm, LANES), F32)
    gate_slab = jnp.zeros((tm, LANES), F32)
    for kk in range(TOP_K):
        idx_slab = jnp.where(lane == kk, idxs[kk], idx_slab)
        gate_slab = jnp.where(lane == kk, es[kk] / tot, gate_slab)
    idx_ref[...] = idx_slab.astype(I32)
    gate_ref[...] = gate_slab


def _cross_attention_router(h1, kv, bsz, s, w_cq, w_co, ln_g, ln_b, w_router, b_router):
    t = h1.shape[0]
    tm = min(XATT_ROWS, s)
    ns = s // tm
    n_mem = kv.shape[0] // bsz
    wr = jnp.concatenate([w_router, jnp.zeros((D_MODEL, LANES - N_EXPERTS), F32)], axis=1)
    wr_hi = wr.astype(BF16)
    wr = jnp.concatenate([wr_hi, (wr - wr_hi.astype(F32)).astype(BF16)], axis=1)
    br = jnp.concatenate([b_router, jnp.zeros((LANES - N_EXPERTS,), F32)]).reshape(1, LANES)
    row = lambda n: pl.BlockSpec((tm, n), lambda b, i: (b * ns + i, 0))
    return pl.pallas_call(
        _xattn_kernel,
        out_shape=[jax.ShapeDtypeStruct((t * ROW_TILES, LANES), F32), jax.ShapeDtypeStruct((t, LANES), I32),
                   jax.ShapeDtypeStruct((t, LANES), F32)],
        grid=(bsz, ns),
        in_specs=[row(D_MODEL), pl.BlockSpec((n_mem, 2 * D_MODEL), lambda b, i: (b, 0)),
                  _const_spec((D_MODEL, D_MODEL)), _const_spec((D_MODEL, D_MODEL)),
                  _const_spec((1, D_MODEL)), _const_spec((1, D_MODEL)),
                  _const_spec((D_MODEL, 2 * LANES)), _const_spec((1, LANES))],
        out_specs=[pl.BlockSpec((tm * ROW_TILES, LANES), lambda b, i: (b * ns + i, 0)), row(LANES), row(LANES)],
        compiler_params=_cparams(("parallel", "parallel")),
        name="cross_attention_router",
    )(h1, kv, w_cq.astype(BF16), w_co.astype(BF16), ln_g.reshape(1, -1), ln_b.reshape(1, -1), wr, br)


def _rank_kernel(idx_ref, ltri_ref, rank_ref, cnt_ref, carry_s):
    @pl.when(pl.program_id(0) == 0)
    def _():
        carry_s[...] = jnp.zeros_like(carry_s)

    idx = idx_ref[...]
    tm = idx.shape[0]
    lane = lax.broadcasted_iota(I32, (tm, LANES), 1)
    onehots = [(lane == idx[:, kk:kk + 1]).astype(F32) for kk in range(TOP_K)]
    sel = onehots[0] + onehots[1] + onehots[2] + onehots[3]
    before = jnp.dot(ltri_ref[...], sel.astype(BF16), preferred_element_type=F32) + carry_s[0:1, :]
    rank_slab = jnp.zeros((tm, LANES), F32)
    for kk in range(TOP_K):
        r = jnp.sum(onehots[kk] * before, axis=1, keepdims=True)
        rank_slab = jnp.where(lane == kk, r, rank_slab)
    rank_ref[...] = rank_slab.astype(I32)
    carry_s[0:1, :] = carry_s[0:1, :] + jnp.sum(sel, axis=0, keepdims=True)
    cnt_ref[...] = jnp.broadcast_to(carry_s[0:1, :], cnt_ref.shape).astype(I32)


def _routing_ranks(idx_slab):
    t = idx_slab.shape[0]
    tm = min(RANK_ROWS, t)
    ltri = jnp.tril(jnp.ones((tm, tm), BF16), k=-1)
    return pl.pallas_call(
        _rank_kernel,
        out_shape=[jax.ShapeDtypeStruct((t, LANES), I32), jax.ShapeDtypeStruct((8, LANES), I32)],
        grid=(t // tm,),
        in_specs=[pl.BlockSpec((tm, LANES), lambda i: (i, 0)), _const_spec((tm, tm))],
        out_specs=[pl.BlockSpec((tm, LANES), lambda i: (i, 0)), _const_spec((8, LANES))],
        scratch_shapes=[pltpu.VMEM((8, LANES), F32)],
        compiler_params=_cparams(("arbitrary",)),
        name="routing_ranks",
    )(idx_slab, ltri)


def _dest_kernel(idx_ref, rank_ref, pstart_ref, dest_ref):
    idx = idx_ref[...]
    tm = idx.shape[0]
    lane = lax.broadcasted_iota(I32, (tm, LANES), 1)
    dest = rank_ref[...].astype(F32)
    pstart = pstart_ref[0:1, :].astype(F32)
    for kk in range(TOP_K):
        start = jnp.sum(jnp.where(lane == idx[:, kk:kk + 1], pstart, 0.0), axis=1, keepdims=True)
        dest = dest + jnp.where(lane == kk, start, 0.0)
    dest_ref[...] = dest.astype(I32)


def _dest_rows(idx_slab, rank_slab, pstart_row):
    t = idx_slab.shape[0]
    tm = min(RANK_ROWS, t)
    return pl.pallas_call(
        _dest_kernel,
        out_shape=jax.ShapeDtypeStruct((t, LANES), I32),
        grid=(t // tm,),
        in_specs=[pl.BlockSpec((tm, LANES), lambda i: (i, 0)), pl.BlockSpec((tm, LANES), lambda i: (i, 0)),
                  _const_spec((8, LANES))],
        out_specs=pl.BlockSpec((tm, LANES), lambda i: (i, 0)),
        compiler_params=_cparams(("parallel",)),
        name="routing_dest",
    )(idx_slab, rank_slab, pstart_row)


def _tile_rows(ref, row):
    return ref.at[pl.ds(pl.multiple_of(row * ROW_TILES, ROW_TILES), ROW_TILES), :]


def _dispatch_kernel(dest_ref, x_ref, buf_in, buf_hbm, sem):
    del buf_in
    n = dest_ref.shape[-1]

    def issue(j, c):
        pltpu.make_async_copy(_tile_rows(x_ref, j // TOP_K), _tile_rows(buf_hbm, dest_ref[0, j]), sem).start()
        return c

    lax.fori_loop(0, n, issue, 0, unroll=8)

    def drain(j, c):
        pltpu.make_async_copy(_tile_rows(x_ref, 0), _tile_rows(buf_hbm, 0), sem).wait()
        return c

    lax.fori_loop(0, n, drain, 0, unroll=8)


def _dispatch(h2, dest_flat, n_rows):
    t = h2.shape[0] // ROW_TILES
    tt = min(DISPATCH_TOKENS, t)
    n = tt * TOP_K
    dest3 = dest_flat.reshape(t // tt, 1, n)
    buf0 = jnp.zeros((n_rows * ROW_TILES, LANES), h2.dtype)
    return pl.pallas_call(
        _dispatch_kernel,
        out_shape=jax.ShapeDtypeStruct((n_rows * ROW_TILES, LANES), h2.dtype),
        grid=(t // tt,),
        in_specs=[pl.BlockSpec((None, 1, n), lambda i: (i, 0, 0), memory_space=pltpu.SMEM),
                  pl.BlockSpec((tt * ROW_TILES, LANES), lambda i: (i, 0)), pl.BlockSpec(memory_space=pl.ANY)],
        out_specs=pl.BlockSpec(memory_space=pl.ANY),
        scratch_shapes=[pltpu.SemaphoreType.DMA],
        input_output_aliases={2: 0},
        compiler_params=_cparams(("arbitrary",)),
        name="moe_dispatch",
    )(dest3, h2, buf0)


def _rows_2d(ref, n):
    return jnp.concatenate([ref[pl.ds(c, n, stride=ROW_TILES), :] for c in range(ROW_TILES)], axis=1)


def _store_rows(ref, val):
    for c in range(ROW_TILES):
        ref[pl.ds(c, val.shape[0], stride=ROW_TILES), :] = val[:, c * LANES:(c + 1) * LANES]


def _expert_kernel(blk_e_ref, nblk_ref, x_ref, wg_ref, wl_ref, bg_ref, bl_ref, wd_ref, bd_ref, y_ref):
    @pl.when(pl.program_id(0) < nblk_ref[0])
    def _():
        xb = _rows_2d(x_ref, y_ref.shape[0] // ROW_TILES).astype(BF16)
        glu = jnp.dot(xb, wg_ref[...], preferred_element_type=F32) + bg_ref[...]
        lin = jnp.dot(xb, wl_ref[...], preferred_element_type=F32) + bl_ref[...]
        glu = jnp.minimum(glu, SWIGLU_LIMIT)
        lin = jnp.clip(lin, -SWIGLU_LIMIT, SWIGLU_LIMIT)
        act = glu * jax.nn.sigmoid(SWIGLU_ALPHA * glu) * (lin + 1.0)
        _store_rows(y_ref, jnp.dot(act.astype(BF16), wd_ref[...], preferred_element_type=F32) + bd_ref[...])

    @pl.when(pl.program_id(0) >= nblk_ref[0])
    def _():
        y_ref[...] = jnp.zeros_like(y_ref)


def _split_kernel(w_ref, perm_ref, g_ref, l_ref):
    half = g_ref.shape[-1]
    sorted_cols = jnp.dot(w_ref[...].astype(BF16), perm_ref[...], preferred_element_type=F32)
    g_ref[...] = sorted_cols[:, :half].astype(BF16)
    l_ref[...] = sorted_cols[:, half:].astype(BF16)


def _split_gate_up(w_gu):
    e, d, n2 = w_gu.shape
    cw = 512
    src = np.concatenate([np.arange(0, cw, 2), np.arange(1, cw, 2)])
    perm = jnp.asarray(np.eye(cw, dtype=np.float32)[:, src], BF16)
    return pl.pallas_call(
        _split_kernel,
        out_shape=[jax.ShapeDtypeStruct((e, d, n2 // 2), BF16), jax.ShapeDtypeStruct((e, d, n2 // 2), BF16)],
        grid=(e, n2 // cw),
        in_specs=[pl.BlockSpec((None, d, cw), lambda i, c: (i, 0, c)), _const_spec((cw, cw))],
        out_specs=[pl.BlockSpec((None, d, cw // 2), lambda i, c: (i, 0, c)),
                   pl.BlockSpec((None, d, cw // 2), lambda i, c: (i, 0, c))],
        compiler_params=_cparams(("parallel", "parallel")),
        name="split_gate_up",
    )(w_gu, perm)


def _experts(xb, blk_e, nblk, w_glu, w_lin, b_glu, b_lin, w_dn, b_dn):
    n_rows = xb.shape[0] // ROW_TILES
    bm = EXPERT_ROWS
    grid_spec = pltpu.PrefetchScalarGridSpec(
        num_scalar_prefetch=2,
        grid=(n_rows // bm,),
        in_specs=[pl.BlockSpec((bm * ROW_TILES, LANES), lambda i, be, nb: (i, 0)),
                  pl.BlockSpec((None, D_MODEL, D_EXPERT), lambda i, be, nb: (be[i], 0, 0)),
                  pl.BlockSpec((None, D_MODEL, D_EXPERT), lambda i, be, nb: (be[i], 0, 0)),
                  pl.BlockSpec((None, 1, D_EXPERT), lambda i, be, nb: (be[i], 0, 0)),
                  pl.BlockSpec((None, 1, D_EXPERT), lambda i, be, nb: (be[i], 0, 0)),
                  pl.BlockSpec((None, D_EXPERT, D_MODEL), lambda i, be, nb: (be[i], 0, 0)),
                  pl.BlockSpec((None, 1, D_MODEL), lambda i, be, nb: (be[i], 0, 0))],
        out_specs=pl.BlockSpec((bm * ROW_TILES, LANES), lambda i, be, nb: (i, 0)),
    )
    return pl.pallas_call(
        _expert_kernel,
        out_shape=jax.ShapeDtypeStruct((n_rows * ROW_TILES, LANES), F32),
        grid_spec=grid_spec,
        compiler_params=_cparams(("arbitrary",)),
        name="moe_experts",
    )(blk_e, nblk, xb, w_glu, w_lin, b_glu, b_lin, w_dn, b_dn)


def _combine_kernel(dest_ref, dest_next_ref, h_ref, gate_ref, y_hbm, g_ref, b_ref, o_ref, rows_s, sem):
    tt = o_ref.shape[0]
    n = tt * TOP_K
    i = pl.program_id(0)
    slot = i % 2

    def row_copy(d, sl, kk, tok):
        return pltpu.make_async_copy(_tile_rows(y_hbm, d), _tile_rows(rows_s.at[sl, kk], tok), sem.at[sl])

    def issue(idx_ref, sl):
        def one(j, c):
            row_copy(idx_ref[0, j], sl, j % TOP_K, j // TOP_K).start()
            return c
        lax.fori_loop(0, n, one, 0, unroll=8)

    @pl.when(i == 0)
    def _():
        issue(dest_ref, 0)

    @pl.when(i + 1 < pl.num_programs(0))
    def _():
        issue(dest_next_ref, 1 - slot)

    def drain(j, c):
        row_copy(0, slot, 0, 0).wait()
        return c

    lax.fori_loop(0, n, drain, 0, unroll=8)
    gates = gate_ref[...]
    acc = DN_ALPHA * _rows_2d(h_ref, tt)
    for kk in range(TOP_K):
        acc = acc + gates[:, kk:kk + 1] * _rows_2d(rows_s.at[slot, kk], tt)
    o_ref[...] = _layer_norm_rows(acc, g_ref[...], b_ref[...])


def _combine(h2, gate_slab, dest_flat, yb, ln_g, ln_b):
    t = h2.shape[0] // ROW_TILES
    tt = min(COMBINE_TOKENS, t)
    n = tt * TOP_K
    steps = t // tt
    dest3 = dest_flat.reshape(steps, 1, n)
    row = lambda m: pl.BlockSpec((tt, m), lambda i: (i, 0))
    return pl.pallas_call(
        _combine_kernel,
        out_shape=jax.ShapeDtypeStruct((t, D_MODEL), F32),
        grid=(steps,),
        in_specs=[pl.BlockSpec((None, 1, n), lambda i: (i, 0, 0), memory_space=pltpu.SMEM),
                  pl.BlockSpec((None, 1, n), lambda i: (jnp.minimum(i + 1, steps - 1), 0, 0),
                               memory_space=pltpu.SMEM),
                  pl.BlockSpec((tt * ROW_TILES, LANES), lambda i: (i, 0)), row(LANES),
                  pl.BlockSpec(memory_space=pl.ANY), _const_spec((1, D_MODEL)), _const_spec((1, D_MODEL))],
        out_specs=row(D_MODEL),
        scratch_shapes=[pltpu.VMEM((2, TOP_K, tt * ROW_TILES, LANES), F32), pltpu.SemaphoreType.DMA((2,))],
        compiler_params=_cparams(("arbitrary",)),
        name="moe_combine",
    )(dest3, dest3, h2, gate_slab, yb, ln_g.reshape(1, -1), ln_b.reshape(1, -1))


def _moe(h2, idx_slab, gate_slab, w_gu, b_gu, w_dn, b_dn, ln_g, ln_b):
    t = h2.shape[0] // ROW_TILES
    bm = EXPERT_ROWS
    n_assign = t * TOP_K
    n_rows = n_assign + N_EXPERTS * bm
    n_blocks = n_rows // bm

    rank_slab, cnt = _routing_ranks(idx_slab)
    counts = cnt[0, :N_EXPERTS]
    padded = (counts + bm - 1) // bm * bm
    pends = jnp.cumsum(padded)
    pstarts = pends - padded
    pstart_row = jnp.zeros((8, LANES), I32).at[:, :N_EXPERTS].set(pstarts[None, :])
    blk_start = jnp.arange(n_blocks, dtype=I32) * bm
    blk_e = jnp.minimum(jnp.sum((pends[None, :] <= blk_start[:, None]).astype(I32), axis=1), N_EXPERTS - 1)
    nblk = (pends[-1] // bm).astype(I32).reshape(1)
    dest_slab = _dest_rows(idx_slab, rank_slab, pstart_row)
    dest_flat = dest_slab[:, :TOP_K].reshape(-1)

    xb = _dispatch(h2, dest_flat, n_rows)
    w_glu, w_lin = _split_gate_up(w_gu)
    b_glu = b_gu[:, None, 0::2]
    b_lin = b_gu[:, None, 1::2]
    yb = _experts(xb, blk_e, nblk, w_glu, w_lin, b_glu, b_lin, w_dn.astype(BF16), b_dn[:, None, :])
    return _combine(h2, gate_slab, dest_flat, yb, ln_g, ln_b)


def kernel(x, mem, positions, w_in, b_igate, b_fgate, conv_w, conv_b, w_mq, w_mk, g_mhead, w_mskip, g_qlat, g_kvlat,
           w_uq, w_ukv, w_br_m, w_br_a, w_mix_out, ln1_g, ln1_b, w_cq, w_ckv, w_co, ln2_g, ln2_b, w_router, b_router,
           w_gu, b_gu, w_dn, b_dn, ln3_g, ln3_b):
    bsz, s, d = x.shape
    t = bsz * s
    h = x.reshape(t, d)
    pos2 = positions.reshape(t, 1)
    for l in range(DEPTH):
        q, k, v, xm, vm, op, gm, ga, gate = _input_projection(h, pos2, w_in[l], g_qlat[l], g_kvlat[l], w_uq[l],
                                                              w_ukv[l])
        ym = _mlstm_branch(xm, vm, op, gate, bsz, s, b_igate[l], b_fgate[l], conv_w[l], conv_b[l], w_mq[l], w_mk[l],
                           g_mhead[l], w_mskip[l])
        oa = _mla_attention(q, k, v, bsz, s)
        h1 = _merge(ym, oa, gm, ga, h, w_br_m[l], w_br_a[l], w_mix_out[l], ln1_g[l], ln1_b[l])
        kv = _mem_kv(mem.reshape(-1, d), w_ckv[l])
        h2, idx_slab, gate_slab = _cross_attention_router(h1, kv, bsz, s, w_cq[l], w_co[l], ln2_g[l], ln2_b[l],
                                                          w_router[l], b_router[l])
        h = _moe(h2, idx_slab, gate_slab, w_gu[l], b_gu[l], w_dn[l], b_dn[l], ln3_g[l], ln3_b[l])
    return h.reshape(bsz, s, d)
```

```python
import functools

import numpy as np
import jax
import jax.numpy as jnp
from jax import lax
from jax.experimental import pallas as pl
from jax.experimental.pallas import tpu as pltpu
from jax.experimental.pallas import tpu_sc as plsc

F32 = jnp.float32
BF16 = jnp.bfloat16
I32 = jnp.int32

D_MODEL = 1024
N_MEM = 256
M_HEADS = 4
M_HEAD_DIM = 128
M_WIDTH = M_HEADS * M_HEAD_DIM
M_CONV = 4
A_HEADS = 8
A_NOPE = 64
A_ROPE = 32
A_QK = A_NOPE + A_ROPE
A_VDIM = 64
A_Q_RANK = 256
A_KV_RANK = 128
ROPE_THETA = 10000.0
X_HEADS = 4
X_HEAD_DIM = D_MODEL // X_HEADS
N_EXPERTS = 32
TOP_K = 4
D_EXPERT = D_MODEL
SWIGLU_ALPHA = 1.702
SWIGLU_LIMIT = 7.0
DEPTH = 1
DN_ALPHA = (2.0 * DEPTH) ** 0.25
EPS = 1e-5
IN_SPLITS = (A_Q_RANK, A_KV_RANK, A_ROPE, M_WIDTH, M_WIDTH, M_WIDTH, M_HEADS, M_HEADS, D_MODEL, D_MODEL)
IN_OFFSETS = tuple(int(v) for v in np.cumsum((0,) + IN_SPLITS))

LANES = 128
VMEM_LIMIT = 56 * 1024 * 1024

PROJ_ROWS = 512
M_CHUNK_ROWS = 128
ATT_Q = 256
MERGE_ROWS = 512
XATT_ROWS = 512
RANK_ROWS = 512
EXPERT_ROWS = 256
COMBINE_TOKENS = 256
SC_CORES = 2
SC_SUBCORES = 16
SC_WORKERS = SC_CORES * SC_SUBCORES
SC_ROWS = 32
NEG_BIG = -1e30
LOG2_E = 1.4426950408889634
ROW_TILES = D_MODEL // LANES

C_QLAT = 0
C_KVLAT = C_QLAT + A_Q_RANK
C_KR = C_KVLAT + A_KV_RANK
C_KRS = C_KR + LANES
C_GATE = C_KRS + LANES
C_XM = C_GATE + LANES
C_VM = C_XM + M_WIDTH
C_OP = C_VM + M_WIDTH
C_GM = C_OP + M_WIDTH
C_GA = C_GM + D_MODEL
C_END = C_GA + D_MODEL


def _cparams(sem, vmem=VMEM_LIMIT):
    return pltpu.CompilerParams(dimension_semantics=sem, vmem_limit_bytes=vmem)


def _const_spec(shape):
    nd = len(shape)
    return pl.BlockSpec(shape, lambda *a: (0,) * nd)


def _layer_norm_rows(v, g, b):
    mu = jnp.mean(v, axis=-1, keepdims=True)
    d = v - mu
    var = jnp.mean(d * d, axis=-1, keepdims=True)
    return d * lax.rsqrt(var + EPS) * g + b


def _proj_kernel(x_ref, pos_ref, w_ref, gq_ref, gkv_ref, wuq_ref, wuqs_ref, wuk_ref, wuv_ref, invf_ref,
                 q_ref, k_ref, v_ref, xm_ref, vm_ref, op_ref, gm_ref, ga_ref, gate_ref):
    xb = x_ref[...].astype(BF16)

    def mm(lo, n):
        return jnp.dot(xb, w_ref[:, lo:lo + n], preferred_element_type=F32)

    xm_ref[...] = mm(C_XM, M_WIDTH).astype(BF16)
    vm_ref[...] = mm(C_VM, M_WIDTH).astype(BF16)
    op_ref[...] = mm(C_OP, M_WIDTH).astype(BF16)
    gm_ref[...] = mm(C_GM, D_MODEL).astype(BF16)
    ga_ref[...] = mm(C_GA, D_MODEL).astype(BF16)
    gate_ref[...] = mm(C_GATE, LANES)

    ang = pos_ref[...].astype(F32) * invf_ref[...]
    cos = jnp.cos(ang)
    sin = jnp.sin(ang)

    q_lat = mm(C_QLAT, A_Q_RANK)
    qn = (q_lat * lax.rsqrt(jnp.mean(q_lat * q_lat, axis=-1, keepdims=True) + EPS) * gq_ref[...]).astype(BF16)
    q = jnp.dot(qn, wuq_ref[...], preferred_element_type=F32)
    qs = jnp.dot(qn, wuqs_ref[...], preferred_element_type=F32)
    kv_lat = mm(C_KVLAT, A_KV_RANK)
    kvn = (kv_lat * lax.rsqrt(jnp.mean(kv_lat * kv_lat, axis=-1, keepdims=True) + EPS) * gkv_ref[...]).astype(BF16)
    kn = jnp.dot(kvn, wuk_ref[...], preferred_element_type=F32)
    lane = lax.broadcasted_iota(I32, (1, A_HEADS * LANES), 1)
    ones_lane = (lane % LANES == A_VDIM).astype(F32)
    v_ref[...] = (jnp.dot(kvn, wuv_ref[...], preferred_element_type=F32) + ones_lane).astype(BF16)
    k_pe = mm(C_KR, LANES) * cos + mm(C_KRS, LANES) * sin
    scale = A_QK ** -0.5 * LOG2_E
    for h in range(A_HEADS):
        sl = slice(h * LANES, (h + 1) * LANES)
        q_ref[:, sl] = ((q[:, sl] * cos + qs[:, sl] * sin) * scale).astype(BF16)
        k_ref[:, sl] = (kn[:, sl] + k_pe).astype(BF16)


def _proj_weights(w_in, w_uq, w_ukv):
    o = IN_OFFSETS
    half = A_ROPE // 2
    w_q, w_kv, w_kr = w_in[:, o[0]:o[1]], w_in[:, o[1]:o[2]], w_in[:, o[2]:o[3]]
    w_xm, w_vm, w_op = w_in[:, o[3]:o[4]], w_in[:, o[4]:o[5]], w_in[:, o[5]:o[6]]
    w_i, w_f, w_gm, w_ga = w_in[:, o[6]:o[7]], w_in[:, o[7]:o[8]], w_in[:, o[8]:o[9]], w_in[:, o[9]:o[10]]
    d = w_in.shape[0]
    z = lambda n: jnp.zeros((d, n), w_in.dtype)
    kr = jnp.concatenate([z(A_NOPE), w_kr, z(LANES - A_QK)], axis=1)
    krs = jnp.concatenate([z(A_NOPE), -w_kr[:, half:], w_kr[:, :half], z(LANES - A_QK)], axis=1)
    gate = jnp.concatenate([w_i, w_f, z(LANES - 2 * M_HEADS)], axis=1)
    w_r = jnp.concatenate([w_q, w_kv, kr, krs, gate, w_xm, w_vm, w_op, w_gm, w_ga], axis=1).astype(BF16)

    uq = w_uq.reshape(A_Q_RANK, A_HEADS, A_QK)
    zq = jnp.zeros((A_Q_RANK, A_HEADS, LANES - A_QK), w_uq.dtype)
    zn = jnp.zeros((A_Q_RANK, A_HEADS, A_NOPE), w_uq.dtype)
    uq_pad = jnp.concatenate([uq, zq], axis=-1).reshape(A_Q_RANK, A_HEADS * LANES).astype(BF16)
    uqs_pad = jnp.concatenate([zn, -uq[..., A_NOPE + half:], uq[..., A_NOPE:A_NOPE + half], zq],
                              axis=-1).reshape(A_Q_RANK, A_HEADS * LANES).astype(BF16)
    ukv = w_ukv.reshape(A_KV_RANK, A_HEADS, A_NOPE + A_VDIM)
    zk = jnp.zeros((A_KV_RANK, A_HEADS, LANES - A_NOPE), w_ukv.dtype)
    uk_pad = jnp.concatenate([ukv[..., :A_NOPE], zk], axis=-1).reshape(A_KV_RANK, A_HEADS * LANES).astype(BF16)
    zv = jnp.zeros((A_KV_RANK, A_HEADS, LANES - A_VDIM), w_ukv.dtype)
    uv_pad = jnp.concatenate([ukv[..., A_NOPE:], zv], axis=-1).reshape(A_KV_RANK, A_HEADS * LANES).astype(BF16)
    return w_r, uq_pad, uqs_pad, uk_pad, uv_pad


def _input_projection(x2, pos2, w_in, g_qlat, g_kvlat, w_uq, w_ukv):
    t = x2.shape[0]
    tm = min(PROJ_ROWS, t)
    w_r, uq_pad, uqs_pad, uk_pad, uv_pad = _proj_weights(w_in, w_uq, w_ukv)
    half = A_ROPE // 2
    inv_freq = ROPE_THETA ** (-jnp.arange(half, dtype=F32) / half)
    invf = jnp.concatenate([jnp.zeros((A_NOPE,), F32), inv_freq, inv_freq,
                            jnp.zeros((LANES - A_QK,), F32)]).reshape(1, LANES)
    hw = A_HEADS * LANES
    row = lambda n: pl.BlockSpec((tm, n), lambda i: (i, 0))
    outs = [
        jax.ShapeDtypeStruct((t, hw), BF16), jax.ShapeDtypeStruct((t, hw), BF16), jax.ShapeDtypeStruct((t, hw), BF16),
        jax.ShapeDtypeStruct((t, M_WIDTH), BF16), jax.ShapeDtypeStruct((t, M_WIDTH), BF16),
        jax.ShapeDtypeStruct((t, M_WIDTH), BF16),
        jax.ShapeDtypeStruct((t, D_MODEL), BF16), jax.ShapeDtypeStruct((t, D_MODEL), BF16),
        jax.ShapeDtypeStruct((t, LANES), F32),
    ]
    return pl.pallas_call(
        _proj_kernel,
        out_shape=outs,
        grid=(t // tm,),
        in_specs=[row(D_MODEL), row(1), _const_spec(w_r.shape), _const_spec((1, A_Q_RANK)),
                  _const_spec((1, A_KV_RANK)), _const_spec(uq_pad.shape), _const_spec(uqs_pad.shape),
                  _const_spec(uk_pad.shape), _const_spec(uv_pad.shape), _const_spec((1, LANES))],
        out_specs=[row(hw), row(hw), row(hw), row(M_WIDTH), row(M_WIDTH), row(M_WIDTH), row(D_MODEL), row(D_MODEL),
                   row(LANES)],
        compiler_params=_cparams(("parallel",)),
        name="input_projection",
    )(x2, pos2, w_r, g_qlat.reshape(1, -1), g_kvlat.reshape(1, -1), uq_pad, uqs_pad, uk_pad, uv_pad, invf)


def _log_sigmoid(v):
    return jnp.minimum(v, 0.0) - jnp.log1p(jnp.exp(-jnp.abs(v)))


def _mlstm_kernel(xm_ref, vm_ref, op_ref, gate_ref, convw_ref, convb_ref, wq_ref, wk_ref, gbias_ref, ghead_ref,
                  skip_ref, tril_ref, out_ref, xpad_s, ct_s, n_s, m_s):
    s = xm_ref.shape[0]
    lc = M_CHUNK_ROWS
    halo = 8
    xpad_s[0:halo, :] = jnp.zeros((halo, M_WIDTH), F32)
    xpad_s[halo:, :] = xm_ref[...].astype(F32)
    ct_s[...] = jnp.zeros_like(ct_s)
    n_s[...] = jnp.zeros_like(n_s)
    m_s[...] = jnp.zeros_like(m_s)
    rows = lax.broadcasted_iota(I32, (lc, lc), 0)
    cols = lax.broadcasted_iota(I32, (lc, lc), 1)
    causal = rows >= cols
    kscale = M_HEAD_DIM ** -0.5

    def chunk(c, carry):
        r0 = pl.multiple_of(c * lc, lc)
        win = xpad_s[pl.ds(r0, lc + halo), :]
        conv = convb_ref[...]
        for j in range(M_CONV):
            lo = halo - (M_CONV - 1) + j
            conv = conv + win[lo:lo + lc, :] * convw_ref[j:j + 1, :]
        xc = conv * jax.nn.sigmoid(conv)
        g = gate_ref[pl.ds(r0, lc), :] + gbias_ref[...]
        ls = _log_sigmoid(g)
        bc = jnp.dot(tril_ref[...], ls, preferred_element_type=F32, precision=lax.Precision.HIGHEST)
        g_t = g.T
        b_t = bc.T
        for h in range(M_HEADS):
            hs = slice(h * M_HEAD_DIM, (h + 1) * M_HEAD_DIM)
            ig_col = g[:, h:h + 1]
            ig_row = g_t[h:h + 1, :]
            b_col = bc[:, M_HEADS + h:M_HEADS + h + 1]
            b_row = b_t[M_HEADS + h:M_HEADS + h + 1, :]
            b_tot = bc[lc - 1:lc, M_HEADS + h:M_HEADS + h + 1]
            m_prev = m_s[h:h + 1, 0:1]
            n_prev = n_s[h:h + 1, :]
            ct_prev = ct_s[h]

            xh = xc[:, hs]
            xhb = xh.astype(BF16)
            q = jnp.dot(xhb, wq_ref[h], preferred_element_type=F32)
            k = jnp.dot(xhb, wk_ref[h], preferred_element_type=F32) * kscale
            qb = q.astype(BF16)
            kb = k.astype(BF16)
            v = vm_ref[pl.ds(r0, lc), hs]

            log_d = jnp.where(causal, b_col - b_row + ig_row, -jnp.inf)
            m_intra = jnp.max(log_d, axis=1, keepdims=True)
            g_inter = b_col + m_prev
            m = jnp.maximum(g_inter, m_intra)
            w_inter = jnp.exp(g_inter - m)
            dmat = jnp.exp(log_d - m)
            sc = lax.dot_general(qb, kb, (((1,), (1,)), ((), ())), preferred_element_type=F32) * dmat
            num = (w_inter * jnp.dot(qb, ct_prev.astype(BF16), preferred_element_type=F32)
                   + jnp.dot(sc.astype(BF16), v, preferred_element_type=F32))
            den = w_inter * jnp.sum(q * n_prev, axis=1, keepdims=True) + jnp.sum(sc, axis=1, keepdims=True)
            hh = num / jnp.maximum(jnp.abs(den), jnp.exp(-m))

            a = b_tot - b_col + ig_col
            m_chunk = jnp.max(a, axis=0, keepdims=True)
            kw = k * jnp.exp(a - m_chunk)
            ct_c = lax.dot_general(kw.astype(BF16), v, (((0,), (0,)), ((), ())), preferred_element_type=F32)
            n_c = jnp.sum(kw, axis=0, keepdims=True)
            m_new = jnp.maximum(b_tot + m_prev, m_chunk)
            s_old = jnp.exp(b_tot + m_prev - m_new)
            s_new = jnp.exp(m_chunk - m_new)
            ct_s[h] = s_old * ct_prev + s_new * ct_c
            n_s[h:h + 1, :] = s_old * n_prev + s_new * n_c
            m_s[h:h + 1, :] = jnp.broadcast_to(m_new, (1, LANES))

            mu = jnp.mean(hh, axis=-1, keepdims=True)
            dv = hh - mu
            var = jnp.mean(dv * dv, axis=-1, keepdims=True)
            hn = dv * lax.rsqrt(var + EPS) * ghead_ref[:, hs]
            y = jax.nn.sigmoid(op_ref[pl.ds(r0, lc), hs].astype(F32)) * (hn + skip_ref[:, hs] * xh)
            out_ref[pl.ds(r0, lc), hs] = y.astype(BF16)
        return carry

    lax.fori_loop(0, s // lc, chunk, 0)


def _mlstm_branch(xm, vm, op, gate, bsz, s, b_i, b_f, conv_w, conv_b, w_mq, w_mk, g_mhead, w_mskip):
    lc = M_CHUNK_ROWS
    gbias = jnp.concatenate([b_i, b_f, jnp.zeros((LANES - 2 * M_HEADS,), F32)]).reshape(1, LANES)
    tril = jnp.tril(jnp.ones((lc, lc), F32))
    seq = lambda n: pl.BlockSpec((s, n), lambda b: (b, 0))
    return pl.pallas_call(
        _mlstm_kernel,
        out_shape=jax.ShapeDtypeStruct((bsz * s, M_WIDTH), BF16),
        grid=(bsz,),
        in_specs=[seq(M_WIDTH), seq(M_WIDTH), seq(M_WIDTH), seq(LANES),
                  _const_spec((M_CONV, M_WIDTH)), _const_spec((1, M_WIDTH)),
                  _const_spec((M_HEADS, M_HEAD_DIM, M_HEAD_DIM)), _const_spec((M_HEADS, M_HEAD_DIM, M_HEAD_DIM)),
                  _const_spec((1, LANES)), _const_spec((1, M_WIDTH)), _const_spec((1, M_WIDTH)),
                  _const_spec((lc, lc))],
        out_specs=seq(M_WIDTH),
        scratch_shapes=[pltpu.VMEM((s + 8, M_WIDTH), F32), pltpu.VMEM((M_HEADS, M_HEAD_DIM, M_HEAD_DIM), F32),
                        pltpu.VMEM((8, LANES), F32), pltpu.VMEM((8, LANES), F32)],
        compiler_params=_cparams(("parallel",)),
        name="mlstm_branch",
    )(xm, vm, op, gate, conv_w, conv_b.reshape(1, -1), w_mq.astype(BF16), w_mk.astype(BF16), gbias,
      g_mhead.reshape(1, -1), w_mskip.reshape(1, -1), tril)


def _attn_kernel(q_ref, k_ref, v_ref, o_ref, s_scr, m_s, acc_s):
    tq = q_ref.shape[0]
    i = pl.program_id(1)
    rows = lax.broadcasted_iota(I32, (tq, tq), 0)
    cols = lax.broadcasted_iota(I32, (tq, tq), 1)

    def lane_tile_max(sc):
        out = sc[:, :LANES]
        for t in range(1, tq // LANES):
            out = jnp.maximum(out, sc[:, t * LANES:(t + 1) * LANES])
        return out

    heads = [slice(h * LANES, (h + 1) * LANES) for h in range(A_HEADS)]
    m_s[...] = jnp.full(m_s.shape, NEG_BIG, F32)
    acc_s[...] = jnp.zeros_like(acc_s)

    def pass_a(j, diagonal):
        k0 = pl.multiple_of(j * tq, tq)
        for h, hs in enumerate(heads):
            sc = lax.dot_general(q_ref[:, hs], k_ref[pl.ds(k0, tq), hs], (((1,), (1,)), ((), ())),
                                 preferred_element_type=F32)
            if diagonal:
                sc = jnp.where(cols <= rows, sc, NEG_BIG)
            s_scr[h, j] = sc
            m_s[h] = jnp.maximum(m_s[h], lane_tile_max(sc))

    def body_a(j, c):
        pass_a(j, False)
        return c

    lax.fori_loop(0, i, body_a, 0)
    pass_a(i, True)
    for h in range(A_HEADS):
        m_s[h] = jnp.broadcast_to(jnp.max(m_s[h], axis=1, keepdims=True), (tq, LANES))

    def body_b(j, c):
        k0 = pl.multiple_of(j * tq, tq)
        for h, hs in enumerate(heads):
            m_row = m_s[h]
            p = jnp.exp2(s_scr[h, j] - jnp.concatenate([m_row] * (tq // LANES), axis=1)).astype(BF16)
            acc_s[h] += jnp.dot(p, v_ref[pl.ds(k0, tq), hs], preferred_element_type=F32)
        return c

    lax.fori_loop(0, i + 1, body_b, 0)
    for h, hs in enumerate(heads):
        acc = acc_s[h]
        o_ref[:, hs] = (acc / acc[:, A_VDIM:A_VDIM + 1]).astype(BF16)


def _mla_attention(q, k, v, bsz, s):
    tq = min(ATT_Q, s)
    nq = s // tq
    hw = A_HEADS * LANES
    return pl.pallas_call(
        _attn_kernel,
        out_shape=jax.ShapeDtypeStruct(q.shape, BF16),
        grid=(bsz, nq),
        in_specs=[pl.BlockSpec((tq, hw), lambda b, i: (b * nq + i, 0)),
                  pl.BlockSpec((s, hw), lambda b, i: (b, 0)),
                  pl.BlockSpec((s, hw), lambda b, i: (b, 0))],
        out_specs=pl.BlockSpec((tq, hw), lambda b, i: (b * nq + i, 0)),
        scratch_shapes=[pltpu.VMEM((A_HEADS, nq, tq, tq), F32), pltpu.VMEM((A_HEADS, tq, LANES), F32),
                        pltpu.VMEM((A_HEADS, tq, LANES), F32)],
        compiler_params=_cparams(("parallel", "arbitrary")),
        name="mla_attention",
    )(q, k, v)


def _merge_kernel(ym_ref, oa_ref, gm_ref, ga_ref, x_ref, wbm_ref, wba_ref, wmix_ref, g_ref, b_ref, h_ref):
    y_m = jnp.dot(ym_ref[...], wbm_ref[...], preferred_element_type=F32)
    y_a = jnp.dot(oa_ref[...], wba_ref[...], preferred_element_type=F32)
    mixed = jax.nn.sigmoid(gm_ref[...].astype(F32)) * y_m + jax.nn.sigmoid(ga_ref[...].astype(F32)) * y_a
    mix = jnp.dot(mixed.astype(BF16), wmix_ref[...], preferred_element_type=F32)
    h_ref[...] = _layer_norm_rows(DN_ALPHA * x_ref[...] + mix, g_ref[...], b_ref[...])


def _merge(ym, oa, gm, ga, x2, w_br_m, w_br_a, w_mix_out, ln_g, ln_b):
    t = x2.shape[0]
    tm = min(MERGE_ROWS, t)
    wba = jnp.concatenate([w_br_a.reshape(A_HEADS, A_VDIM, D_MODEL),
                           jnp.zeros((A_HEADS, LANES - A_VDIM, D_MODEL), w_br_a.dtype)], axis=1)
    wba = wba.reshape(A_HEADS * LANES, D_MODEL).astype(BF16)
    row = lambda n: pl.BlockSpec((tm, n), lambda i: (i, 0))
    return pl.pallas_call(
        _merge_kernel,
        out_shape=jax.ShapeDtypeStruct((t, D_MODEL), F32),
        grid=(t // tm,),
        in_specs=[row(M_WIDTH), row(A_HEADS * LANES), row(D_MODEL), row(D_MODEL), row(D_MODEL),
                  _const_spec((M_WIDTH, D_MODEL)), _const_spec((A_HEADS * LANES, D_MODEL)),
                  _const_spec((D_MODEL, D_MODEL)), _const_spec((1, D_MODEL)), _const_spec((1, D_MODEL))],
        out_specs=row(D_MODEL),
        compiler_params=_cparams(("parallel",)),
        name="merge_deepnorm1",
    )(ym, oa, gm, ga, x2, w_br_m.astype(BF16), wba, w_mix_out.astype(BF16), ln_g.reshape(1, -1), ln_b.reshape(1, -1))


def _kv_kernel(m_ref, w_ref, o_ref):
    o_ref[...] = jnp.dot(m_ref[...].astype(BF16), w_ref[...], preferred_element_type=F32).astype(BF16)


def _mem_kv(mem2, w_ckv):
    r = mem2.shape[0]
    tm = min(512, r)
    return pl.pallas_call(
        _kv_kernel,
        out_shape=jax.ShapeDtypeStruct((r, 2 * D_MODEL), BF16),
        grid=(r // tm,),
        in_specs=[pl.BlockSpec((tm, D_MODEL), lambda i: (i, 0)), _const_spec((D_MODEL, 2 * D_MODEL))],
        out_specs=pl.BlockSpec((tm, 2 * D_MODEL), lambda i: (i, 0)),
        compiler_params=_cparams(("parallel",)),
        name="memory_kv",
    )(mem2, w_ckv.astype(BF16))


def _xattn_kernel(h_ref, kv_ref, wq_ref, wo_ref, g_ref, b_ref, wr_ref, br_ref, h2_ref, idx_ref, gate_ref):
    h1 = h_ref[...]
    q = jnp.dot(h1.astype(BF16), wq_ref[...], preferred_element_type=F32).astype(BF16)
    scale = X_HEAD_DIM ** -0.5
    outs = []
    for hd in range(X_HEADS):
        ks = slice(hd * X_HEAD_DIM, (hd + 1) * X_HEAD_DIM)
        vs = slice(D_MODEL + hd * X_HEAD_DIM, D_MODEL + (hd + 1) * X_HEAD_DIM)
        sc = lax.dot_general(q[:, ks], kv_ref[:, ks], (((1,), (1,)), ((), ())), preferred_element_type=F32) * scale
        sc = sc - jnp.max(sc, axis=1, keepdims=True)
        p = jnp.exp(sc)
        p = p / jnp.sum(p, axis=1, keepdims=True)
        outs.append(jnp.dot(p.astype(BF16), kv_ref[:, vs], preferred_element_type=F32).astype(BF16))
    o = jnp.concatenate(outs, axis=1)
    att = jnp.dot(o, wo_ref[...], preferred_element_type=F32)
    h2 = _layer_norm_rows(DN_ALPHA * h1 + att, g_ref[...], b_ref[...])
    for c in range(ROW_TILES):
        h2_ref[pl.ds(c, h2.shape[0], stride=ROW_TILES), :] = h2[:, c * LANES:(c + 1) * LANES]

    h2_hi = h2.astype(BF16)
    h2_lo = (h2 - h2_hi.astype(F32)).astype(BF16)
    hi_prod = jnp.dot(h2_hi, wr_ref[...], preferred_element_type=F32)
    lo_prod = jnp.dot(h2_lo, wr_ref[:, :LANES], preferred_element_type=F32)
    logits = hi_prod[:, :LANES] + (hi_prod[:, LANES:] + lo_prod) + br_ref[...]
    tm = logits.shape[0]
    lane = lax.broadcasted_iota(I32, (tm, LANES), 1)
    lane_f = lane.astype(F32)
    work = jnp.where(lane < N_EXPERTS, logits, -jnp.inf)
    vals, idxs = [], []
    for _ in range(TOP_K):
        mx = jnp.max(work, axis=1, keepdims=True)
        ix = jnp.min(jnp.where(work == mx, lane_f, float(LANES)), axis=1, keepdims=True)
        vals.append(mx)
        idxs.append(ix)
        work = jnp.where(lane_f == ix, -jnp.inf, work)
    es = [jnp.exp(vv - vals[0]) for vv in vals]
    tot = es[0] + es[1] + es[2] + es[3]
    idx_slab = jnp.zeros((tm, LANES), F32)
    gate_slab = jnp.zeros((tm, LANES), F32)
    for kk in range(TOP_K):
        idx_slab = jnp.where(lane == kk, idxs[kk], idx_slab)
        gate_slab = jnp.where(lane == kk, es[kk] / tot, gate_slab)
    idx_ref[...] = idx_slab.astype(I32)
    gate_ref[...] = gate_slab


def _cross_attention_router(h1, kv, bsz, s, w_cq, w_co, ln_g, ln_b, w_router, b_router):
    t = h1.shape[0]
    tm = min(XATT_ROWS, s)
    ns = s // tm
    n_mem = kv.shape[0] // bsz
    wr = jnp.concatenate([w_router, jnp.zeros((D_MODEL, LANES - N_EXPERTS), F32)], axis=1)
    wr_hi = wr.astype(BF16)
    wr = jnp.concatenate([wr_hi, (wr - wr_hi.astype(F32)).astype(BF16)], axis=1)
    br = jnp.concatenate([b_router, jnp.zeros((LANES - N_EXPERTS,), F32)]).reshape(1, LANES)
    row = lambda n: pl.BlockSpec((tm, n), lambda b, i: (b * ns + i, 0))
    return pl.pallas_call(
        _xattn_kernel,
        out_shape=[jax.ShapeDtypeStruct((t * ROW_TILES, LANES), F32), jax.ShapeDtypeStruct((t, LANES), I32),
                   jax.ShapeDtypeStruct((t, LANES), F32)],
        grid=(bsz, ns),
        in_specs=[row(D_MODEL), pl.BlockSpec((n_mem, 2 * D_MODEL), lambda b, i: (b, 0)),
                  _const_spec((D_MODEL, D_MODEL)), _const_spec((D_MODEL, D_MODEL)),
                  _const_spec((1, D_MODEL)), _const_spec((1, D_MODEL)),
                  _const_spec((D_MODEL, 2 * LANES)), _const_spec((1, LANES))],
        out_specs=[pl.BlockSpec((tm * ROW_TILES, LANES), lambda b, i: (b * ns + i, 0)), row(LANES), row(LANES)],
        compiler_params=_cparams(("parallel", "parallel")),
        name="cross_attention_router",
    )(h1, kv, w_cq.astype(BF16), w_co.astype(BF16), ln_g.reshape(1, -1), ln_b.reshape(1, -1), wr, br)


def _rank_kernel(idx_ref, ltri_ref, rank_ref, cnt_ref, carry_s):
    @pl.when(pl.program_id(0) == 0)
    def _():
        carry_s[...] = jnp.zeros_like(carry_s)

    idx = idx_ref[...]
    tm = idx.shape[0]
    lane = lax.broadcasted_iota(I32, (tm, LANES), 1)
    onehots = [(lane == idx[:, kk:kk + 1]).astype(F32) for kk in range(TOP_K)]
    sel = onehots[0] + onehots[1] + onehots[2] + onehots[3]
    before = jnp.dot(ltri_ref[...], sel.astype(BF16), preferred_element_type=F32) + carry_s[0:1, :]
    rank_slab = jnp.zeros((tm, LANES), F32)
    for kk in range(TOP_K):
        r = jnp.sum(onehots[kk] * before, axis=1, keepdims=True)
        rank_slab = jnp.where(lane == kk, r, rank_slab)
    rank_ref[...] = rank_slab.astype(I32)
    carry_s[0:1, :] = carry_s[0:1, :] + jnp.sum(sel, axis=0, keepdims=True)
    cnt_ref[...] = jnp.broadcast_to(carry_s[0:1, :], cnt_ref.shape).astype(I32)


def _routing_ranks(idx_slab):
    t = idx_slab.shape[0]
    tm = min(RANK_ROWS, t)
    ltri = jnp.tril(jnp.ones((tm, tm), BF16), k=-1)
    return pl.pallas_call(
        _rank_kernel,
        out_shape=[jax.ShapeDtypeStruct((t, LANES), I32), jax.ShapeDtypeStruct((8, LANES), I32)],
        grid=(t // tm,),
        in_specs=[pl.BlockSpec((tm, LANES), lambda i: (i, 0)), _const_spec((tm, tm))],
        out_specs=[pl.BlockSpec((tm, LANES), lambda i: (i, 0)), _const_spec((8, LANES))],
        scratch_shapes=[pltpu.VMEM((8, LANES), F32)],
        compiler_params=_cparams(("arbitrary",)),
        name="routing_ranks",
    )(idx_slab, ltri)


def _dest_kernel(idx_ref, rank_ref, pstart_ref, dest_ref):
    idx = idx_ref[...]
    tm = idx.shape[0]
    lane = lax.broadcasted_iota(I32, (tm, LANES), 1)
    dest = rank_ref[...].astype(F32)
    pstart = pstart_ref[0:1, :].astype(F32)
    for kk in range(TOP_K):
        start = jnp.sum(jnp.where(lane == idx[:, kk:kk + 1], pstart, 0.0), axis=1, keepdims=True)
        dest = dest + jnp.where(lane == kk, start, 0.0)
    dest_ref[...] = dest.astype(I32)


def _dest_rows(idx_slab, rank_slab, pstart_row):
    t = idx_slab.shape[0]
    tm = min(RANK_ROWS, t)
    return pl.pallas_call(
        _dest_kernel,
        out_shape=jax.ShapeDtypeStruct((t, LANES), I32),
        grid=(t // tm,),
        in_specs=[pl.BlockSpec((tm, LANES), lambda i: (i, 0)), pl.BlockSpec((tm, LANES), lambda i: (i, 0)),
                  _const_spec((8, LANES))],
        out_specs=pl.BlockSpec((tm, LANES), lambda i: (i, 0)),
        compiler_params=_cparams(("parallel",)),
        name="routing_dest",
    )(idx_slab, rank_slab, pstart_row)


def _sc_mesh():
    return plsc.VectorSubcoreMesh(core_axis_name="c", subcore_axis_name="s", num_cores=SC_CORES,
                                  num_subcores=SC_SUBCORES)


def _sc_worker():
    return lax.axis_index("s") * SC_CORES + lax.axis_index("c")


def _dispatch(h2, dest_tk, pad_rows, n_rows):
    t = h2.shape[0]
    w = SC_ROWS
    per_w = t // SC_WORKERS
    steps = per_w // w
    assert steps % 2 == 0 and steps * w * SC_WORKERS == t
    pad_steps = pad_rows.shape[0] // (SC_WORKERS * w)
    dest_w = dest_tk.reshape(SC_WORKERS, steps, w, TOP_K).transpose(0, 1, 3, 2).reshape(SC_WORKERS, steps * TOP_K, w)
    pad_w = pad_rows.reshape(SC_WORKERS, pad_steps, w)
    zeros = jnp.zeros((w, ROW_TILES, LANES), F32)

    def body(x_hbm, dest_hbm, pad_hbm, zeros_hbm, o_hbm, idx_v, pad_v, rows_v, sem_in, sem_out):
        wid = _sc_worker()
        tok0 = wid * per_w
        pltpu.sync_copy(dest_hbm.at[wid], idx_v)
        pltpu.sync_copy(pad_hbm.at[wid], pad_v)
        pltpu.sync_copy(zeros_hbm, rows_v.at[0])
        for j in range(pad_steps):
            pltpu.sync_copy(rows_v.at[0], o_hbm.at[pad_v.at[j]])

        def load(step, b):
            return pltpu.make_async_copy(x_hbm.at[pl.ds(tok0 + step * w, w)], rows_v.at[b], sem_in.at[b])

        def scatter(step, kk, b):
            return pltpu.make_async_copy(rows_v.at[b], o_hbm.at[idx_v.at[step * TOP_K + kk]], sem_out.at[b])

        load(0, 0).start()

        @pl.loop(0, steps, step=2)
        def _(s0):
            for b in range(2):
                step = s0 + b
                load(step, b).wait()

                @pl.when(step >= 1)
                def _():
                    for kk in range(TOP_K):
                        scatter(step - 1, kk, 1 - b).wait()

                @pl.when(step + 1 < steps)
                def _():
                    load(step + 1, 1 - b).start()

                for kk in range(TOP_K):
                    scatter(step, kk, b).start()

        for kk in range(TOP_K):
            scatter(steps - 1, kk, 1).wait()

    return pl.kernel(
        body,
        out_type=jax.ShapeDtypeStruct((n_rows, ROW_TILES, LANES), F32),
        mesh=_sc_mesh(),
        scratch_types=[pltpu.VMEM((steps * TOP_K, w), I32), pltpu.VMEM((pad_steps, w), I32),
                       pltpu.VMEM((2, w, ROW_TILES, LANES), F32), pltpu.SemaphoreType.DMA((2,)),
                       pltpu.SemaphoreType.DMA((2,))],
        name="moe_dispatch_sc",
    )(h2, dest_w, pad_w, zeros)


def _gather_rows(yb, dest_kt):
    n = dest_kt.shape[0]
    w = SC_ROWS
    per_w = n // SC_WORKERS
    steps = per_w // w
    assert steps % 2 == 0 and steps * w * SC_WORKERS == n
    dest_w = dest_kt.reshape(SC_WORKERS, steps, w)

    def body(y_hbm, dest_hbm, o_hbm, idx_v, rows_v, sem_g, sem_w):
        wid = _sc_worker()
        row0 = wid * per_w
        pltpu.sync_copy(dest_hbm.at[wid], idx_v)

        def gather(step, b):
            return pltpu.make_async_copy(y_hbm.at[idx_v.at[step]], rows_v.at[b], sem_g.at[b])

        def write(step, b):
            return pltpu.make_async_copy(rows_v.at[b], o_hbm.at[pl.ds(row0 + step * w, w)], sem_w.at[b])

        gather(0, 0).start()

        @pl.loop(0, steps, step=2)
        def _(s0):
            for b in range(2):
                step = s0 + b
                gather(step, b).wait()

                @pl.when(step >= 1)
                def _():
                    write(step - 1, 1 - b).wait()

                @pl.when(step + 1 < steps)
                def _():
                    gather(step + 1, 1 - b).start()

                write(step, b).start()

        write(steps - 1, 1).wait()

    return pl.kernel(
        body,
        out_type=jax.ShapeDtypeStruct((n, ROW_TILES, LANES), F32),
        mesh=_sc_mesh(),
        scratch_types=[pltpu.VMEM((steps, w), I32), pltpu.VMEM((2, w, ROW_TILES, LANES), F32),
                       pltpu.SemaphoreType.DMA((2,)), pltpu.SemaphoreType.DMA((2,))],
        name="moe_gather_sc",
    )(yb, dest_w)


def _rows_2d(ref, n):
    return jnp.concatenate([ref[pl.ds(c, n, stride=ROW_TILES), :] for c in range(ROW_TILES)], axis=1)


def _store_rows(ref, val):
    for c in range(ROW_TILES):
        ref[pl.ds(c, val.shape[0], stride=ROW_TILES), :] = val[:, c * LANES:(c + 1) * LANES]


def _expert_kernel(blk_e_ref, nblk_ref, x_ref, wg_ref, wl_ref, bg_ref, bl_ref, wd_ref, bd_ref, y_ref):
    @pl.when(pl.program_id(0) < nblk_ref[0])
    def _():
        xb = _rows_2d(x_ref, y_ref.shape[0] // ROW_TILES).astype(BF16)
        glu = jnp.dot(xb, wg_ref[...], preferred_element_type=F32) + bg_ref[...]
        lin = jnp.dot(xb, wl_ref[...], preferred_element_type=F32) + bl_ref[...]
        glu = jnp.minimum(glu, SWIGLU_LIMIT)
        lin = jnp.clip(lin, -SWIGLU_LIMIT, SWIGLU_LIMIT)
        act = glu * jax.nn.sigmoid(SWIGLU_ALPHA * glu) * (lin + 1.0)
        _store_rows(y_ref, jnp.dot(act.astype(BF16), wd_ref[...], preferred_element_type=F32) + bd_ref[...])

    @pl.when(pl.program_id(0) >= nblk_ref[0])
    def _():
        y_ref[...] = jnp.zeros_like(y_ref)


def _split_kernel(w_ref, perm_ref, g_ref, l_ref):
    half = g_ref.shape[-1]
    sorted_cols = jnp.dot(w_ref[...].astype(BF16), perm_ref[...], preferred_element_type=F32)
    g_ref[...] = sorted_cols[:, :half].astype(BF16)
    l_ref[...] = sorted_cols[:, half:].astype(BF16)


def _split_gate_up(w_gu):
    e, d, n2 = w_gu.shape
    cw = 512
    src = np.concatenate([np.arange(0, cw, 2), np.arange(1, cw, 2)])
    perm = jnp.asarray(np.eye(cw, dtype=np.float32)[:, src], BF16)
    return pl.pallas_call(
        _split_kernel,
        out_shape=[jax.ShapeDtypeStruct((e, d, n2 // 2), BF16), jax.ShapeDtypeStruct((e, d, n2 // 2), BF16)],
        grid=(e, n2 // cw),
        in_specs=[pl.BlockSpec((None, d, cw), lambda i, c: (i, 0, c)), _const_spec((cw, cw))],
        out_specs=[pl.BlockSpec((None, d, cw // 2), lambda i, c: (i, 0, c)),
                   pl.BlockSpec((None, d, cw // 2), lambda i, c: (i, 0, c))],
        compiler_params=_cparams(("parallel", "parallel")),
        name="split_gate_up",
    )(w_gu, perm)


def _experts(xb, blk_e, nblk, w_glu, w_lin, b_glu, b_lin, w_dn, b_dn):
    n_rows = xb.shape[0] // ROW_TILES
    bm = EXPERT_ROWS
    grid_spec = pltpu.PrefetchScalarGridSpec(
        num_scalar_prefetch=2,
        grid=(n_rows // bm,),
        in_specs=[pl.BlockSpec((bm * ROW_TILES, LANES), lambda i, be, nb: (i, 0)),
                  pl.BlockSpec((None, D_MODEL, D_EXPERT), lambda i, be, nb: (be[i], 0, 0)),
                  pl.BlockSpec((None, D_MODEL, D_EXPERT), lambda i, be, nb: (be[i], 0, 0)),
                  pl.BlockSpec((None, 1, D_EXPERT), lambda i, be, nb: (be[i], 0, 0)),
                  pl.BlockSpec((None, 1, D_EXPERT), lambda i, be, nb: (be[i], 0, 0)),
                  pl.BlockSpec((None, D_EXPERT, D_MODEL), lambda i, be, nb: (be[i], 0, 0)),
                  pl.BlockSpec((None, 1, D_MODEL), lambda i, be, nb: (be[i], 0, 0))],
        out_specs=pl.BlockSpec((bm * ROW_TILES, LANES), lambda i, be, nb: (i, 0)),
    )
    return pl.pallas_call(
        _expert_kernel,
        out_shape=jax.ShapeDtypeStruct((n_rows * ROW_TILES, LANES), F32),
        grid_spec=grid_spec,
        compiler_params=_cparams(("arbitrary",)),
        name="moe_experts",
    )(blk_e, nblk, xb, w_glu, w_lin, b_glu, b_lin, w_dn, b_dn)


def _combine_kernel(h_ref, gate_ref, y0_ref, y1_ref, y2_ref, y3_ref, g_ref, b_ref, o_ref):
    tt = o_ref.shape[0]
    gates = gate_ref[...]
    acc = DN_ALPHA * _rows_2d(h_ref, tt)
    for kk, y_ref in enumerate((y0_ref, y1_ref, y2_ref, y3_ref)):
        acc = acc + gates[:, kk:kk + 1] * _rows_2d(y_ref, tt)
    o_ref[...] = _layer_norm_rows(acc, g_ref[...], b_ref[...])


def _combine(h2, gate_slab, yg, ln_g, ln_b):
    t = h2.shape[0] // ROW_TILES
    tt = min(COMBINE_TOKENS, t)
    steps = t // tt
    row = lambda m: pl.BlockSpec((tt, m), lambda i: (i, 0))
    tiles = lambda kk: pl.BlockSpec((tt * ROW_TILES, LANES), lambda i: (kk * steps + i, 0))
    return pl.pallas_call(
        _combine_kernel,
        out_shape=jax.ShapeDtypeStruct((t, D_MODEL), F32),
        grid=(steps,),
        in_specs=[tiles(0), row(LANES), tiles(0), tiles(1), tiles(2), tiles(3),
                  _const_spec((1, D_MODEL)), _const_spec((1, D_MODEL))],
        out_specs=row(D_MODEL),
        compiler_params=_cparams(("parallel",)),
        name="moe_combine",
    )(h2, gate_slab, yg, yg, yg, yg, ln_g.reshape(1, -1), ln_b.reshape(1, -1))


def _moe(h2, idx_slab, gate_slab, w_gu, b_gu, w_dn, b_dn, ln_g, ln_b):
    t = h2.shape[0] // ROW_TILES
    bm = EXPERT_ROWS
    n_assign = t * TOP_K
    n_rows = n_assign + N_EXPERTS * bm
    n_blocks = n_rows // bm

    rank_slab, cnt = _routing_ranks(idx_slab)
    counts = cnt[0, :N_EXPERTS]
    padded = (counts + bm - 1) // bm * bm
    pends = jnp.cumsum(padded)
    pstarts = pends - padded
    pstart_row = jnp.zeros((8, LANES), I32).at[:, :N_EXPERTS].set(pstarts[None, :])
    blk_start = jnp.arange(n_blocks, dtype=I32) * bm
    blk_e = jnp.minimum(jnp.sum((pends[None, :] <= blk_start[:, None]).astype(I32), axis=1), N_EXPERTS - 1)
    nblk = (pends[-1] // bm).astype(I32).reshape(1)
    dest_slab = _dest_rows(idx_slab, rank_slab, pstart_row)
    dest_tk = dest_slab[:, :TOP_K]
    pad_off = jnp.arange(bm, dtype=I32)[None, :]
    pad_rows = jnp.where(pad_off < (padded - counts)[:, None], (pstarts + counts)[:, None] + pad_off,
                         n_rows - 1).reshape(-1)

    xb = _dispatch(h2.reshape(t, ROW_TILES, LANES), dest_tk, pad_rows, n_rows)
    w_glu, w_lin = _split_gate_up(w_gu)
    b_glu = b_gu[:, None, 0::2]
    b_lin = b_gu[:, None, 1::2]
    yb = _experts(xb.reshape(n_rows * ROW_TILES, LANES), blk_e, nblk, w_glu, w_lin, b_glu, b_lin, w_dn.astype(BF16),
                  b_dn[:, None, :])
    yg = _gather_rows(yb.reshape(n_rows, ROW_TILES, LANES), dest_tk.T.reshape(-1))
    return _combine(h2, gate_slab, yg.reshape(n_assign * ROW_TILES, LANES), ln_g, ln_b)


def kernel(x, mem, positions, w_in, b_igate, b_fgate, conv_w, conv_b, w_mq, w_mk, g_mhead, w_mskip, g_qlat, g_kvlat,
           w_uq, w_ukv, w_br_m, w_br_a, w_mix_out, ln1_g, ln1_b, w_cq, w_ckv, w_co, ln2_g, ln2_b, w_router, b_router,
           w_gu, b_gu, w_dn, b_dn, ln3_g, ln3_b):
    bsz, s, d = x.shape
    t = bsz * s
    h = x.reshape(t, d)
    pos2 = positions.reshape(t, 1)
    for l in range(DEPTH):
        q, k, v, xm, vm, op, gm, ga, gate = _input_projection(h, pos2, w_in[l], g_qlat[l], g_kvlat[l], w_uq[l],
                                                              w_ukv[l])
        ym = _mlstm_branch(xm, vm, op, gate, bsz, s, b_igate[l], b_fgate[l], conv_w[l], conv_b[l], w_mq[l], w_mk[l],
                           g_mhead[l], w_mskip[l])
        oa = _mla_attention(q, k, v, bsz, s)
        h1 = _merge(ym, oa, gm, ga, h, w_br_m[l], w_br_a[l], w_mix_out[l], ln1_g[l], ln1_b[l])
        kv = _mem_kv(mem.reshape(-1, d), w_ckv[l])
        h2, idx_slab, gate_slab = _cross_attention_router(h1, kv, bsz, s, w_cq[l], w_co[l], ln2_g[l], ln2_b[l],
                                                          w_router[l], b_router[l])
        h = _moe(h2, idx_slab, gate_slab, w_gu[l], b_gu[l], w_dn[l], b_dn[l], ln3_g[l], ln3_b[l])
    return h.reshape(bsz, s, d)
```

```python
import functools

import numpy as np
import jax
import jax.numpy as jnp
from jax import lax
from jax.experimental import pallas as pl
from jax.experimental.pallas import tpu as pltpu
from jax.experimental.pallas import tpu_sc as plsc

F32 = jnp.float32
BF16 = jnp.bfloat16
I32 = jnp.int32

D_MODEL = 1024
N_MEM = 256
M_HEADS = 4
M_HEAD_DIM = 128
M_WIDTH = M_HEADS * M_HEAD_DIM
M_CONV = 4
A_HEADS = 8
A_NOPE = 64
A_ROPE = 32
A_QK = A_NOPE + A_ROPE
A_VDIM = 64
A_Q_RANK = 256
A_KV_RANK = 128
ROPE_THETA = 10000.0
X_HEADS = 4
X_HEAD_DIM = D_MODEL // X_HEADS
N_EXPERTS = 32
TOP_K = 4
D_EXPERT = D_MODEL
SWIGLU_ALPHA = 1.702
SWIGLU_LIMIT = 7.0
DEPTH = 1
DN_ALPHA = (2.0 * DEPTH) ** 0.25
EPS = 1e-5
IN_SPLITS = (A_Q_RANK, A_KV_RANK, A_ROPE, M_WIDTH, M_WIDTH, M_WIDTH, M_HEADS, M_HEADS, D_MODEL, D_MODEL)
IN_OFFSETS = tuple(int(v) for v in np.cumsum((0,) + IN_SPLITS))

LANES = 128
VMEM_LIMIT = 56 * 1024 * 1024

PROJ_ROWS = 512
M_CHUNK_ROWS = 128
ATT_Q = 256
MERGE_ROWS = 512
XATT_ROWS = 512
RANK_ROWS = 512
EXPERT_ROWS = 256
COMBINE_TOKENS = 256
MOE_PARTS = 2
SC_CORES = 2
SC_SUBCORES = 16
SC_WORKERS = SC_CORES * SC_SUBCORES
SC_ROWS = 32
NEG_BIG = -1e30
LOG2_E = 1.4426950408889634
ROW_TILES = D_MODEL // LANES

C_QLAT = 0
C_KVLAT = C_QLAT + A_Q_RANK
C_KR = C_KVLAT + A_KV_RANK
C_KRS = C_KR + LANES
C_GATE = C_KRS + LANES
C_XM = C_GATE + LANES
C_VM = C_XM + M_WIDTH
C_OP = C_VM + M_WIDTH
C_GM = C_OP + M_WIDTH
C_GA = C_GM + D_MODEL
C_END = C_GA + D_MODEL


def _cparams(sem, vmem=VMEM_LIMIT):
    return pltpu.CompilerParams(dimension_semantics=sem, vmem_limit_bytes=vmem)


def _const_spec(shape):
    nd = len(shape)
    return pl.BlockSpec(shape, lambda *a: (0,) * nd)


def _layer_norm_rows(v, g, b):
    mu = jnp.mean(v, axis=-1, keepdims=True)
    d = v - mu
    var = jnp.mean(d * d, axis=-1, keepdims=True)
    return d * lax.rsqrt(var + EPS) * g + b


def _proj_kernel(x_ref, pos_ref, w_ref, gq_ref, gkv_ref, wuq_ref, wuqs_ref, wuk_ref, wuv_ref, invf_ref,
                 q_ref, k_ref, v_ref, xm_ref, vm_ref, op_ref, gm_ref, ga_ref, gate_ref):
    xb = x_ref[...].astype(BF16)

    def mm(lo, n):
        return jnp.dot(xb, w_ref[:, lo:lo + n], preferred_element_type=F32)

    xm_ref[...] = mm(C_XM, M_WIDTH).astype(BF16)
    vm_ref[...] = mm(C_VM, M_WIDTH).astype(BF16)
    op_ref[...] = mm(C_OP, M_WIDTH).astype(BF16)
    gm_ref[...] = mm(C_GM, D_MODEL).astype(BF16)
    ga_ref[...] = mm(C_GA, D_MODEL).astype(BF16)
    gate_ref[...] = mm(C_GATE, LANES)

    ang = pos_ref[...].astype(F32) * invf_ref[...]
    cos = jnp.cos(ang)
    sin = jnp.sin(ang)

    q_lat = mm(C_QLAT, A_Q_RANK)
    qn = (q_lat * lax.rsqrt(jnp.mean(q_lat * q_lat, axis=-1, keepdims=True) + EPS) * gq_ref[...]).astype(BF16)
    q = jnp.dot(qn, wuq_ref[...], preferred_element_type=F32)
    qs = jnp.dot(qn, wuqs_ref[...], preferred_element_type=F32)
    kv_lat = mm(C_KVLAT, A_KV_RANK)
    kvn = (kv_lat * lax.rsqrt(jnp.mean(kv_lat * kv_lat, axis=-1, keepdims=True) + EPS) * gkv_ref[...]).astype(BF16)
    kn = jnp.dot(kvn, wuk_ref[...], preferred_element_type=F32)
    lane = lax.broadcasted_iota(I32, (1, A_HEADS * LANES), 1)
    ones_lane = (lane % LANES == A_VDIM).astype(F32)
    v_ref[...] = (jnp.dot(kvn, wuv_ref[...], preferred_element_type=F32) + ones_lane).astype(BF16)
    k_pe = mm(C_KR, LANES) * cos + mm(C_KRS, LANES) * sin
    scale = A_QK ** -0.5 * LOG2_E
    for h in range(A_HEADS):
        sl = slice(h * LANES, (h + 1) * LANES)
        q_ref[:, sl] = ((q[:, sl] * cos + qs[:, sl] * sin) * scale).astype(BF16)
        k_ref[:, sl] = (kn[:, sl] + k_pe).astype(BF16)


def _proj_weights(w_in, w_uq, w_ukv):
    o = IN_OFFSETS
    half = A_ROPE // 2
    w_q, w_kv, w_kr = w_in[:, o[0]:o[1]], w_in[:, o[1]:o[2]], w_in[:, o[2]:o[3]]
    w_xm, w_vm, w_op = w_in[:, o[3]:o[4]], w_in[:, o[4]:o[5]], w_in[:, o[5]:o[6]]
    w_i, w_f, w_gm, w_ga = w_in[:, o[6]:o[7]], w_in[:, o[7]:o[8]], w_in[:, o[8]:o[9]], w_in[:, o[9]:o[10]]
    d = w_in.shape[0]
    z = lambda n: jnp.zeros((d, n), w_in.dtype)
    kr = jnp.concatenate([z(A_NOPE), w_kr, z(LANES - A_QK)], axis=1)
    krs = jnp.concatenate([z(A_NOPE), -w_kr[:, half:], w_kr[:, :half], z(LANES - A_QK)], axis=1)
    gate = jnp.concatenate([w_i, w_f, z(LANES - 2 * M_HEADS)], axis=1)
    w_r = jnp.concatenate([w_q, w_kv, kr, krs, gate, w_xm, w_vm, w_op, w_gm, w_ga], axis=1).astype(BF16)

    uq = w_uq.reshape(A_Q_RANK, A_HEADS, A_QK)
    zq = jnp.zeros((A_Q_RANK, A_HEADS, LANES - A_QK), w_uq.dtype)
    zn = jnp.zeros((A_Q_RANK, A_HEADS, A_NOPE), w_uq.dtype)
    uq_pad = jnp.concatenate([uq, zq], axis=-1).reshape(A_Q_RANK, A_HEADS * LANES).astype(BF16)
    uqs_pad = jnp.concatenate([zn, -uq[..., A_NOPE + half:], uq[..., A_NOPE:A_NOPE + half], zq],
                              axis=-1).reshape(A_Q_RANK, A_HEADS * LANES).astype(BF16)
    ukv = w_ukv.reshape(A_KV_RANK, A_HEADS, A_NOPE + A_VDIM)
    zk = jnp.zeros((A_KV_RANK, A_HEADS, LANES - A_NOPE), w_ukv.dtype)
    uk_pad = jnp.concatenate([ukv[..., :A_NOPE], zk], axis=-1).reshape(A_KV_RANK, A_HEADS * LANES).astype(BF16)
    zv = jnp.zeros((A_KV_RANK, A_HEADS, LANES - A_VDIM), w_ukv.dtype)
    uv_pad = jnp.concatenate([ukv[..., A_NOPE:], zv], axis=-1).reshape(A_KV_RANK, A_HEADS * LANES).astype(BF16)
    return w_r, uq_pad, uqs_pad, uk_pad, uv_pad


def _input_projection(x2, pos2, w_in, g_qlat, g_kvlat, w_uq, w_ukv):
    t = x2.shape[0]
    tm = min(PROJ_ROWS, t)
    w_r, uq_pad, uqs_pad, uk_pad, uv_pad = _proj_weights(w_in, w_uq, w_ukv)
    half = A_ROPE // 2
    inv_freq = ROPE_THETA ** (-jnp.arange(half, dtype=F32) / half)
    invf = jnp.concatenate([jnp.zeros((A_NOPE,), F32), inv_freq, inv_freq,
                            jnp.zeros((LANES - A_QK,), F32)]).reshape(1, LANES)
    hw = A_HEADS * LANES
    row = lambda n: pl.BlockSpec((tm, n), lambda i: (i, 0))
    outs = [
        jax.ShapeDtypeStruct((t, hw), BF16), jax.ShapeDtypeStruct((t, hw), BF16), jax.ShapeDtypeStruct((t, hw), BF16),
        jax.ShapeDtypeStruct((t, M_WIDTH), BF16), jax.ShapeDtypeStruct((t, M_WIDTH), BF16),
        jax.ShapeDtypeStruct((t, M_WIDTH), BF16),
        jax.ShapeDtypeStruct((t, D_MODEL), BF16), jax.ShapeDtypeStruct((t, D_MODEL), BF16),
        jax.ShapeDtypeStruct((t, LANES), F32),
    ]
    return pl.pallas_call(
        _proj_kernel,
        out_shape=outs,
        grid=(t // tm,),
        in_specs=[row(D_MODEL), row(1), _const_spec(w_r.shape), _const_spec((1, A_Q_RANK)),
                  _const_spec((1, A_KV_RANK)), _const_spec(uq_pad.shape), _const_spec(uqs_pad.shape),
                  _const_spec(uk_pad.shape), _const_spec(uv_pad.shape), _const_spec((1, LANES))],
        out_specs=[row(hw), row(hw), row(hw), row(M_WIDTH), row(M_WIDTH), row(M_WIDTH), row(D_MODEL), row(D_MODEL),
                   row(LANES)],
        compiler_params=_cparams(("parallel",)),
        name="input_projection",
    )(x2, pos2, w_r, g_qlat.reshape(1, -1), g_kvlat.reshape(1, -1), uq_pad, uqs_pad, uk_pad, uv_pad, invf)


def _log_sigmoid(v):
    return jnp.minimum(v, 0.0) - jnp.log1p(jnp.exp(-jnp.abs(v)))


def _mlstm_kernel(xm_ref, vm_ref, op_ref, gate_ref, convw_ref, convb_ref, wqk_ref, gbias_ref, ghead_ref,
                  skip_ref, tril_ref, out_ref, xpad_s, ctn_s, m_s):
    s = xm_ref.shape[0]
    lc = M_CHUNK_ROWS
    halo = 8
    xpad_s[0:halo, :] = jnp.zeros((halo, M_WIDTH), F32)
    xpad_s[halo:, :] = xm_ref[...].astype(F32)
    ctn_s[...] = jnp.zeros_like(ctn_s)
    m_s[...] = jnp.zeros_like(m_s)
    rows = lax.broadcasted_iota(I32, (lc, lc), 0)
    cols = lax.broadcasted_iota(I32, (lc, lc), 1)
    causal = rows >= cols
    row_id = lax.broadcasted_iota(I32, (lc, LANES), 0)
    ones_blk = (lax.broadcasted_iota(I32, (lc, LANES), 1) == 0).astype(BF16)
    kscale = M_HEAD_DIM ** -0.5

    def chunk(c, carry):
        r0 = pl.multiple_of(c * lc, lc)
        win = xpad_s[pl.ds(r0, lc + halo), :]
        conv = convb_ref[...]
        for j in range(M_CONV):
            lo = halo - (M_CONV - 1) + j
            conv = conv + win[lo:lo + lc, :] * convw_ref[j:j + 1, :]
        xc = conv * jax.nn.sigmoid(conv)
        g = gate_ref[pl.ds(r0, lc), :] + gbias_ref[...]
        ls = _log_sigmoid(g)
        bc = jnp.dot(tril_ref[...], ls, preferred_element_type=F32, precision=lax.Precision.HIGHEST)
        b0 = pltpu.roll(bc, LANES - M_HEADS, axis=1)
        u = g - b0
        cm = u
        shift = 1
        while shift < lc:
            cm = jnp.maximum(cm, jnp.where(row_id >= shift, pltpu.roll(cm, shift, axis=0), -jnp.inf))
            shift *= 2
        m_prev = m_s[0:1, :]
        g_inter = b0 + m_prev
        m = jnp.maximum(g_inter, b0 + cm)
        w_inter = jnp.exp(g_inter - m)
        e_negm = jnp.exp(-m)
        a_mat = b0 - m
        b_tot = b0[lc - 1:lc, :]
        aw = b_tot - b0 + g
        m_chunk = jnp.max(aw, axis=0, keepdims=True)
        wa = jnp.exp(aw - m_chunk)
        m_new = jnp.maximum(b_tot + m_prev, m_chunk)
        s_old = jnp.exp(b_tot + m_prev - m_new)
        s_new = jnp.exp(m_chunk - m_new)
        m_s[0:1, :] = m_new
        u_t = u.T
        heads = range(M_HEADS)
        hsl = [slice(h * M_HEAD_DIM, (h + 1) * M_HEAD_DIM) for h in heads]
        hcl = [slice(h, h + 1) for h in heads]
        qk = jnp.dot(xc.astype(BF16), wqk_ref[...], preferred_element_type=F32)
        q_b = [qk[:, hsl[h]].astype(BF16) for h in heads]
        k_f = [qk[:, M_WIDTH + h * M_HEAD_DIM:M_WIDTH + (h + 1) * M_HEAD_DIM] * kscale for h in heads]
        v_aug = [jnp.concatenate([vm_ref[pl.ds(r0, lc), hsl[h]], ones_blk], axis=1) for h in heads]
        ctn_prev = [ctn_s[h] for h in heads]
        s_raw = [lax.dot_general(q_b[h], k_f[h].astype(BF16), (((1,), (1,)), ((), ())), preferred_element_type=F32)
                 for h in heads]
        inter = [jnp.dot(q_b[h], ctn_prev[h].astype(BF16), preferred_element_type=F32) for h in heads]
        ctn_c = [lax.dot_general((k_f[h] * wa[:, hcl[h]]).astype(BF16), v_aug[h], (((0,), (0,)), ((), ())),
                                 preferred_element_type=F32) for h in heads]
        sc_b = [(s_raw[h] * jnp.exp(jnp.where(causal, a_mat[:, hcl[h]] + u_t[hcl[h], :], -jnp.inf))).astype(BF16)
                for h in heads]
        intra = [jnp.dot(sc_b[h], v_aug[h], preferred_element_type=F32) for h in heads]
        for h in heads:
            ctn_s[h] = s_old[:, hcl[h]] * ctn_prev[h] + s_new[:, hcl[h]] * ctn_c[h]
            wi = w_inter[:, hcl[h]]
            num = wi * inter[h][:, :M_HEAD_DIM] + intra[h][:, :M_HEAD_DIM]
            den = wi * inter[h][:, M_HEAD_DIM:M_HEAD_DIM + 1] + intra[h][:, M_HEAD_DIM:M_HEAD_DIM + 1]
            hh = num / jnp.maximum(jnp.abs(den), e_negm[:, hcl[h]])
            mu = jnp.mean(hh, axis=-1, keepdims=True)
            dv = hh - mu
            var = jnp.mean(dv * dv, axis=-1, keepdims=True)
            hn = dv * lax.rsqrt(var + EPS) * ghead_ref[:, hsl[h]]
            y = jax.nn.sigmoid(op_ref[pl.ds(r0, lc), hsl[h]].astype(F32)) * (hn + skip_ref[:, hsl[h]] * xc[:, hsl[h]])
            out_ref[pl.ds(r0, lc), hsl[h]] = y.astype(BF16)
        return carry

    lax.fori_loop(0, s // lc, chunk, 0)


def _mlstm_branch(xm, vm, op, gate, bsz, s, b_i, b_f, conv_w, conv_b, w_mq, w_mk, g_mhead, w_mskip):
    lc = M_CHUNK_ROWS
    gbias = jnp.concatenate([b_i, b_f, jnp.zeros((LANES - 2 * M_HEADS,), F32)]).reshape(1, LANES)
    tril = jnp.tril(jnp.ones((lc, lc), F32))
    eye = jnp.eye(M_HEADS, dtype=F32)
    block_diag = lambda w: jnp.einsum('hde,hg->hdge', w, eye).reshape(M_WIDTH, M_WIDTH)
    w_qk = jnp.concatenate([block_diag(w_mq), block_diag(w_mk)], axis=1).astype(BF16)
    seq = lambda n: pl.BlockSpec((s, n), lambda b: (b, 0))
    return pl.pallas_call(
        _mlstm_kernel,
        out_shape=jax.ShapeDtypeStruct((bsz * s, M_WIDTH), BF16),
        grid=(bsz,),
        in_specs=[seq(M_WIDTH), seq(M_WIDTH), seq(M_WIDTH), seq(LANES),
                  _const_spec((M_CONV, M_WIDTH)), _const_spec((1, M_WIDTH)),
                  _const_spec((M_WIDTH, 2 * M_WIDTH)),
                  _const_spec((1, LANES)), _const_spec((1, M_WIDTH)), _const_spec((1, M_WIDTH)),
                  _const_spec((lc, lc))],
        out_specs=seq(M_WIDTH),
        scratch_shapes=[pltpu.VMEM((s + 8, M_WIDTH), F32), pltpu.VMEM((M_HEADS, M_HEAD_DIM, 2 * LANES), F32),
                        pltpu.VMEM((8, LANES), F32)],
        compiler_params=_cparams(("parallel",)),
        name="mlstm_branch",
    )(xm, vm, op, gate, conv_w, conv_b.reshape(1, -1), w_qk, gbias, g_mhead.reshape(1, -1), w_mskip.reshape(1, -1),
      tril)


def _attn_kernel(q_ref, k_ref, v_ref, o_ref, s_scr, m_s, acc_s):
    tq = q_ref.shape[0]
    i = pl.program_id(1)
    rows = lax.broadcasted_iota(I32, (tq, tq), 0)
    cols = lax.broadcasted_iota(I32, (tq, tq), 1)

    def lane_tile_max(sc):
        out = sc[:, :LANES]
        for t in range(1, tq // LANES):
            out = jnp.maximum(out, sc[:, t * LANES:(t + 1) * LANES])
        return out

    heads = [slice(h * LANES, (h + 1) * LANES) for h in range(A_HEADS)]
    m_s[...] = jnp.full(m_s.shape, NEG_BIG, F32)
    acc_s[...] = jnp.zeros_like(acc_s)

    def pass_a(j, diagonal):
        k0 = pl.multiple_of(j * tq, tq)
        for h, hs in enumerate(heads):
            sc = lax.dot_general(q_ref[:, hs], k_ref[pl.ds(k0, tq), hs], (((1,), (1,)), ((), ())),
                                 preferred_element_type=F32)
            if diagonal:
                sc = jnp.where(cols <= rows, sc, NEG_BIG)
            s_scr[h, j] = sc
            m_s[h] = jnp.maximum(m_s[h], lane_tile_max(sc))

    def body_a(j, c):
        pass_a(j, False)
        return c

    lax.fori_loop(0, i, body_a, 0)
    pass_a(i, True)
    for h in range(A_HEADS):
        m_s[h] = jnp.broadcast_to(jnp.max(m_s[h], axis=1, keepdims=True), (tq, LANES))

    def body_b(j, c):
        k0 = pl.multiple_of(j * tq, tq)
        for h, hs in enumerate(heads):
            m_row = m_s[h]
            p = jnp.exp2(s_scr[h, j] - jnp.concatenate([m_row] * (tq // LANES), axis=1)).astype(BF16)
            acc_s[h] += jnp.dot(p, v_ref[pl.ds(k0, tq), hs], preferred_element_type=F32)
        return c

    lax.fori_loop(0, i + 1, body_b, 0)
    for h, hs in enumerate(heads):
        acc = acc_s[h]
        o_ref[:, hs] = (acc / acc[:, A_VDIM:A_VDIM + 1]).astype(BF16)


def _mla_attention(q, k, v, bsz, s):
    tq = min(ATT_Q, s)
    nq = s // tq
    hw = A_HEADS * LANES
    return pl.pallas_call(
        _attn_kernel,
        out_shape=jax.ShapeDtypeStruct(q.shape, BF16),
        grid=(bsz, nq),
        in_specs=[pl.BlockSpec((tq, hw), lambda b, i: (b * nq + i, 0)),
                  pl.BlockSpec((s, hw), lambda b, i: (b, 0)),
                  pl.BlockSpec((s, hw), lambda b, i: (b, 0))],
        out_specs=pl.BlockSpec((tq, hw), lambda b, i: (b * nq + i, 0)),
        scratch_shapes=[pltpu.VMEM((A_HEADS, nq, tq, tq), F32), pltpu.VMEM((A_HEADS, tq, LANES), F32),
                        pltpu.VMEM((A_HEADS, tq, LANES), F32)],
        compiler_params=_cparams(("parallel", "arbitrary")),
        name="mla_attention",
    )(q, k, v)


def _merge_kernel(ym_ref, oa_ref, gm_ref, ga_ref, x_ref, wbm_ref, wba_ref, wmix_ref, g_ref, b_ref, h_ref):
    y_m = jnp.dot(ym_ref[...], wbm_ref[...], preferred_element_type=F32)
    y_a = jnp.dot(oa_ref[...], wba_ref[...], preferred_element_type=F32)
    mixed = jax.nn.sigmoid(gm_ref[...].astype(F32)) * y_m + jax.nn.sigmoid(ga_ref[...].astype(F32)) * y_a
    mix = jnp.dot(mixed.astype(BF16), wmix_ref[...], preferred_element_type=F32)
    h_ref[...] = _layer_norm_rows(DN_ALPHA * x_ref[...] + mix, g_ref[...], b_ref[...])


def _merge(ym, oa, gm, ga, x2, w_br_m, w_br_a, w_mix_out, ln_g, ln_b):
    t = x2.shape[0]
    tm = min(MERGE_ROWS, t)
    wba = jnp.concatenate([w_br_a.reshape(A_HEADS, A_VDIM, D_MODEL),
                           jnp.zeros((A_HEADS, LANES - A_VDIM, D_MODEL), w_br_a.dtype)], axis=1)
    wba = wba.reshape(A_HEADS * LANES, D_MODEL).astype(BF16)
    row = lambda n: pl.BlockSpec((tm, n), lambda i: (i, 0))
    return pl.pallas_call(
        _merge_kernel,
        out_shape=jax.ShapeDtypeStruct((t, D_MODEL), F32),
        grid=(t // tm,),
        in_specs=[row(M_WIDTH), row(A_HEADS * LANES), row(D_MODEL), row(D_MODEL), row(D_MODEL),
                  _const_spec((M_WIDTH, D_MODEL)), _const_spec((A_HEADS * LANES, D_MODEL)),
                  _const_spec((D_MODEL, D_MODEL)), _const_spec((1, D_MODEL)), _const_spec((1, D_MODEL))],
        out_specs=row(D_MODEL),
        compiler_params=_cparams(("parallel",)),
        name="merge_deepnorm1",
    )(ym, oa, gm, ga, x2, w_br_m.astype(BF16), wba, w_mix_out.astype(BF16), ln_g.reshape(1, -1), ln_b.reshape(1, -1))


def _kv_kernel(m_ref, w_ref, o_ref):
    o_ref[...] = jnp.dot(m_ref[...].astype(BF16), w_ref[...], preferred_element_type=F32).astype(BF16)


def _mem_kv(mem2, w_ckv):
    r = mem2.shape[0]
    tm = min(512, r)
    return pl.pallas_call(
        _kv_kernel,
        out_shape=jax.ShapeDtypeStruct((r, 2 * D_MODEL), BF16),
        grid=(r // tm,),
        in_specs=[pl.BlockSpec((tm, D_MODEL), lambda i: (i, 0)), _const_spec((D_MODEL, 2 * D_MODEL))],
        out_specs=pl.BlockSpec((tm, 2 * D_MODEL), lambda i: (i, 0)),
        compiler_params=_cparams(("parallel",)),
        name="memory_kv",
    )(mem2, w_ckv.astype(BF16))


def _xattn_kernel(h_ref, kv_ref, wq_ref, wo_ref, g_ref, b_ref, wr_ref, br_ref, h2_ref, idx_ref, gate_ref):
    h1 = h_ref[...]
    q = jnp.dot(h1.astype(BF16), wq_ref[...], preferred_element_type=F32).astype(BF16)
    scale = X_HEAD_DIM ** -0.5
    outs = []
    for hd in range(X_HEADS):
        ks = slice(hd * X_HEAD_DIM, (hd + 1) * X_HEAD_DIM)
        vs = slice(D_MODEL + hd * X_HEAD_DIM, D_MODEL + (hd + 1) * X_HEAD_DIM)
        sc = lax.dot_general(q[:, ks], kv_ref[:, ks], (((1,), (1,)), ((), ())), preferred_element_type=F32) * scale
        sc = sc - jnp.max(sc, axis=1, keepdims=True)
        p = jnp.exp(sc)
        p = p / jnp.sum(p, axis=1, keepdims=True)
        outs.append(jnp.dot(p.astype(BF16), kv_ref[:, vs], preferred_element_type=F32).astype(BF16))
    o = jnp.concatenate(outs, axis=1)
    att = jnp.dot(o, wo_ref[...], preferred_element_type=F32)
    h2 = _layer_norm_rows(DN_ALPHA * h1 + att, g_ref[...], b_ref[...])
    for c in range(ROW_TILES):
        h2_ref[pl.ds(c, h2.shape[0], stride=ROW_TILES), :] = h2[:, c * LANES:(c + 1) * LANES]

    h2_hi = h2.astype(BF16)
    h2_lo = (h2 - h2_hi.astype(F32)).astype(BF16)
    hi_prod = jnp.dot(h2_hi, wr_ref[...], preferred_element_type=F32)
    lo_prod = jnp.dot(h2_lo, wr_ref[:, :LANES], preferred_element_type=F32)
    logits = hi_prod[:, :LANES] + (hi_prod[:, LANES:] + lo_prod) + br_ref[...]
    tm = logits.shape[0]
    lane = lax.broadcasted_iota(I32, (tm, LANES), 1)
    lane_f = lane.astype(F32)
    work = jnp.where(lane < N_EXPERTS, logits, -jnp.inf)
    vals, idxs = [], []
    for _ in range(TOP_K):
        mx = jnp.max(work, axis=1, keepdims=True)
        ix = jnp.min(jnp.where(work == mx, lane_f, float(LANES)), axis=1, keepdims=True)
        vals.append(mx)
        idxs.append(ix)
        work = jnp.where(lane_f == ix, -jnp.inf, work)
    es = [jnp.exp(vv - vals[0]) for vv in vals]
    tot = es[0] + es[1] + es[2] + es[3]
    idx_slab = jnp.zeros((tm, LANES), F32)
    gate_slab = jnp.zeros((tm, LANES), F32)
    for kk in range(TOP_K):
        idx_slab = jnp.where(lane == kk, idxs[kk], idx_slab)
        gate_slab = jnp.where(lane == kk, es[kk] / tot, gate_slab)
    idx_ref[...] = idx_slab.astype(I32)
    gate_ref[...] = gate_slab


def _cross_attention_router(h1, kv, bsz, s, w_cq, w_co, ln_g, ln_b, w_router, b_router):
    t = h1.shape[0]
    tm = min(XATT_ROWS, s)
    ns = s // tm
    n_mem = kv.shape[0] // bsz
    wr = jnp.concatenate([w_router, jnp.zeros((D_MODEL, LANES - N_EXPERTS), F32)], axis=1)
    wr_hi = wr.astype(BF16)
    wr = jnp.concatenate([wr_hi, (wr - wr_hi.astype(F32)).astype(BF16)], axis=1)
    br = jnp.concatenate([b_router, jnp.zeros((LANES - N_EXPERTS,), F32)]).reshape(1, LANES)
    row = lambda n: pl.BlockSpec((tm, n), lambda b, i: (b * ns + i, 0))
    return pl.pallas_call(
        _xattn_kernel,
        out_shape=[jax.ShapeDtypeStruct((t * ROW_TILES, LANES), F32), jax.ShapeDtypeStruct((t, LANES), I32),
                   jax.ShapeDtypeStruct((t, LANES), F32)],
        grid=(bsz, ns),
        in_specs=[row(D_MODEL), pl.BlockSpec((n_mem, 2 * D_MODEL), lambda b, i: (b, 0)),
                  _const_spec((D_MODEL, D_MODEL)), _const_spec((D_MODEL, D_MODEL)),
                  _const_spec((1, D_MODEL)), _const_spec((1, D_MODEL)),
                  _const_spec((D_MODEL, 2 * LANES)), _const_spec((1, LANES))],
        out_specs=[pl.BlockSpec((tm * ROW_TILES, LANES), lambda b, i: (b * ns + i, 0)), row(LANES), row(LANES)],
        compiler_params=_cparams(("parallel", "parallel")),
        name="cross_attention_router",
    )(h1, kv, w_cq.astype(BF16), w_co.astype(BF16), ln_g.reshape(1, -1), ln_b.reshape(1, -1), wr, br)


def _rank_kernel(idx_ref, ltri_ref, rank_ref, cnt_ref, carry_s):
    @pl.when(pl.program_id(0) == 0)
    def _():
        carry_s[...] = jnp.zeros_like(carry_s)

    idx = idx_ref[...]
    tm = idx.shape[0]
    lane = lax.broadcasted_iota(I32, (tm, LANES), 1)
    onehots = [(lane == idx[:, kk:kk + 1]).astype(F32) for kk in range(TOP_K)]
    sel = onehots[0] + onehots[1] + onehots[2] + onehots[3]
    before = jnp.dot(ltri_ref[...], sel.astype(BF16), preferred_element_type=F32) + carry_s[0:1, :]
    rank_slab = jnp.zeros((tm, LANES), F32)
    for kk in range(TOP_K):
        r = jnp.sum(onehots[kk] * before, axis=1, keepdims=True)
        rank_slab = jnp.where(lane == kk, r, rank_slab)
    rank_ref[...] = rank_slab.astype(I32)
    carry_s[0:1, :] = carry_s[0:1, :] + jnp.sum(sel, axis=0, keepdims=True)
    cnt_ref[...] = jnp.broadcast_to(carry_s[0:1, :], cnt_ref.shape).astype(I32)


def _routing_ranks(idx_slab, part, t):
    tm = min(RANK_ROWS, t)
    first = part * (t // tm)
    ltri = jnp.tril(jnp.ones((tm, tm), BF16), k=-1)
    return pl.pallas_call(
        _rank_kernel,
        out_shape=[jax.ShapeDtypeStruct((t, LANES), I32), jax.ShapeDtypeStruct((8, LANES), I32)],
        grid=(t // tm,),
        in_specs=[pl.BlockSpec((tm, LANES), lambda i: (first + i, 0)), _const_spec((tm, tm))],
        out_specs=[pl.BlockSpec((tm, LANES), lambda i: (i, 0)), _const_spec((8, LANES))],
        scratch_shapes=[pltpu.VMEM((8, LANES), F32)],
        compiler_params=_cparams(("arbitrary",)),
        name="routing_ranks",
    )(idx_slab, ltri)


def _dest_kernel(idx_ref, rank_ref, pstart_ref, dest_ref):
    idx = idx_ref[...]
    tm = idx.shape[0]
    lane = lax.broadcasted_iota(I32, (tm, LANES), 1)
    dest = rank_ref[...].astype(F32)
    pstart = pstart_ref[0:1, :].astype(F32)
    for kk in range(TOP_K):
        start = jnp.sum(jnp.where(lane == idx[:, kk:kk + 1], pstart, 0.0), axis=1, keepdims=True)
        dest = dest + jnp.where(lane == kk, start, 0.0)
    dest_ref[...] = dest.astype(I32)


def _dest_rows(idx_slab, rank_slab, pstart_row, part):
    t = rank_slab.shape[0]
    tm = min(RANK_ROWS, t)
    first = part * (t // tm)
    return pl.pallas_call(
        _dest_kernel,
        out_shape=jax.ShapeDtypeStruct((t, LANES), I32),
        grid=(t // tm,),
        in_specs=[pl.BlockSpec((tm, LANES), lambda i: (first + i, 0)), pl.BlockSpec((tm, LANES), lambda i: (i, 0)),
                  _const_spec((8, LANES))],
        out_specs=pl.BlockSpec((tm, LANES), lambda i: (i, 0)),
        compiler_params=_cparams(("parallel",)),
        name="routing_dest",
    )(idx_slab, rank_slab, pstart_row)


def _sc_mesh():
    return plsc.VectorSubcoreMesh(core_axis_name="c", subcore_axis_name="s", num_cores=SC_CORES,
                                  num_subcores=SC_SUBCORES)


def _sc_worker():
    return lax.axis_index("s") * SC_CORES + lax.axis_index("c")


def _dispatch(h2, dest_tk, pad_rows, n_rows, tok_base):
    t = dest_tk.shape[0]
    w = SC_ROWS
    per_w = t // SC_WORKERS
    steps = per_w // w
    assert steps % 2 == 0 and steps * w * SC_WORKERS == t
    pad_steps = pad_rows.shape[0] // (SC_WORKERS * w)
    dest_w = dest_tk.reshape(SC_WORKERS, steps, w, TOP_K).transpose(0, 1, 3, 2).reshape(SC_WORKERS, steps * TOP_K, w)
    pad_w = pad_rows.reshape(SC_WORKERS, pad_steps, w)
    zeros = jnp.zeros((w, ROW_TILES, LANES), F32)

    def body(x_hbm, dest_hbm, pad_hbm, zeros_hbm, o_hbm, idx_v, pad_v, rows_v, sem_in, sem_out):
        wid = _sc_worker()
        tok0 = tok_base + wid * per_w
        pltpu.sync_copy(dest_hbm.at[wid], idx_v)
        pltpu.sync_copy(pad_hbm.at[wid], pad_v)
        pltpu.sync_copy(zeros_hbm, rows_v.at[0])
        for j in range(pad_steps):
            pltpu.sync_copy(rows_v.at[0], o_hbm.at[pad_v.at[j]])

        def load(step, b):
            return pltpu.make_async_copy(x_hbm.at[pl.ds(tok0 + step * w, w)], rows_v.at[b], sem_in.at[b])

        def scatter(step, kk, b):
            return pltpu.make_async_copy(rows_v.at[b], o_hbm.at[idx_v.at[step * TOP_K + kk]], sem_out.at[b])

        load(0, 0).start()

        @pl.loop(0, steps, step=2)
        def _(s0):
            for b in range(2):
                step = s0 + b
                load(step, b).wait()

                @pl.when(step >= 1)
                def _():
                    for kk in range(TOP_K):
                        scatter(step - 1, kk, 1 - b).wait()

                @pl.when(step + 1 < steps)
                def _():
                    load(step + 1, 1 - b).start()

                for kk in range(TOP_K):
                    scatter(step, kk, b).start()

        for kk in range(TOP_K):
            scatter(steps - 1, kk, 1).wait()

    return pl.kernel(
        body,
        out_type=jax.ShapeDtypeStruct((n_rows, ROW_TILES, LANES), F32),
        mesh=_sc_mesh(),
        scratch_types=[pltpu.VMEM((steps * TOP_K, w), I32), pltpu.VMEM((pad_steps, w), I32),
                       pltpu.VMEM((2, w, ROW_TILES, LANES), F32), pltpu.SemaphoreType.DMA((2,)),
                       pltpu.SemaphoreType.DMA((2,))],
        name="moe_dispatch_sc",
    )(h2, dest_w, pad_w, zeros)


def _gather_rows(yb, dest_kt):
    n = dest_kt.shape[0]
    w = SC_ROWS
    per_w = n // SC_WORKERS
    steps = per_w // w
    assert steps % 2 == 0 and steps * w * SC_WORKERS == n
    dest_w = dest_kt.reshape(SC_WORKERS, steps, w)

    def body(y_hbm, dest_hbm, o_hbm, idx_v, rows_v, sem_g, sem_w):
        wid = _sc_worker()
        row0 = wid * per_w
        pltpu.sync_copy(dest_hbm.at[wid], idx_v)

        def gather(step, b):
            return pltpu.make_async_copy(y_hbm.at[idx_v.at[step]], rows_v.at[b], sem_g.at[b])

        def write(step, b):
            return pltpu.make_async_copy(rows_v.at[b], o_hbm.at[pl.ds(row0 + step * w, w)], sem_w.at[b])

        gather(0, 0).start()

        @pl.loop(0, steps, step=2)
        def _(s0):
            for b in range(2):
                step = s0 + b
                gather(step, b).wait()

                @pl.when(step >= 1)
                def _():
                    write(step - 1, 1 - b).wait()

                @pl.when(step + 1 < steps)
                def _():
                    gather(step + 1, 1 - b).start()

                write(step, b).start()

        write(steps - 1, 1).wait()

    return pl.kernel(
        body,
        out_type=jax.ShapeDtypeStruct((n, ROW_TILES, LANES), F32),
        mesh=_sc_mesh(),
        scratch_types=[pltpu.VMEM((steps, w), I32), pltpu.VMEM((2, w, ROW_TILES, LANES), F32),
                       pltpu.SemaphoreType.DMA((2,)), pltpu.SemaphoreType.DMA((2,))],
        name="moe_gather_sc",
    )(yb, dest_w)


def _rows_2d(ref, n):
    return jnp.concatenate([ref[pl.ds(c, n, stride=ROW_TILES), :] for c in range(ROW_TILES)], axis=1)


def _store_rows(ref, val):
    for c in range(ROW_TILES):
        ref[pl.ds(c, val.shape[0], stride=ROW_TILES), :] = val[:, c * LANES:(c + 1) * LANES]


def _expert_kernel(blk_e_ref, first_ref, slot_ref, next_ref, nblk_ref, x_ref, wg_hbm, wl_hbm, bg_ref, bl_ref, wd_hbm,
                   bd_ref, y_ref, wg_buf, wl_buf, wd_buf, sem):
    i = pl.program_id(0)

    def fetch(e, sl):
        return [pltpu.make_async_copy(src.at[e], dst.at[sl], sem.at[sl, j])
                for j, (src, dst) in enumerate(((wg_hbm, wg_buf), (wl_hbm, wl_buf), (wd_hbm, wd_buf)))]

    @pl.when(i < nblk_ref[0])
    def _():
        sl = slot_ref[i]

        @pl.when(first_ref[i] == 1)
        def _():
            @pl.when(i == 0)
            def _():
                for cp in fetch(blk_e_ref[0], sl):
                    cp.start()

            for cp in fetch(blk_e_ref[i], sl):
                cp.wait()

            @pl.when(next_ref[i] >= 0)
            def _():
                for cp in fetch(next_ref[i], 1 - sl):
                    cp.start()

        xb = _rows_2d(x_ref, y_ref.shape[0] // ROW_TILES).astype(BF16)
        glu = jnp.dot(xb, wg_buf[sl], preferred_element_type=F32) + bg_ref[...]
        lin = jnp.dot(xb, wl_buf[sl], preferred_element_type=F32) + bl_ref[...]
        glu = jnp.minimum(glu, SWIGLU_LIMIT)
        lin = jnp.clip(lin, -SWIGLU_LIMIT, SWIGLU_LIMIT)
        act = glu * jax.nn.sigmoid(SWIGLU_ALPHA * glu) * (lin + 1.0)
        _store_rows(y_ref, jnp.dot(act.astype(BF16), wd_buf[sl], preferred_element_type=F32) + bd_ref[...])

    @pl.when(i >= nblk_ref[0])
    def _():
        y_ref[...] = jnp.zeros_like(y_ref)


def _split_kernel(w_ref, perm_ref, g_ref, l_ref):
    half = g_ref.shape[-1]
    sorted_cols = jnp.dot(w_ref[...].astype(BF16), perm_ref[...], preferred_element_type=F32)
    g_ref[...] = sorted_cols[:, :half].astype(BF16)
    l_ref[...] = sorted_cols[:, half:].astype(BF16)


def _split_gate_up(w_gu):
    e, d, n2 = w_gu.shape
    cw = 512
    src = np.concatenate([np.arange(0, cw, 2), np.arange(1, cw, 2)])
    perm = jnp.asarray(np.eye(cw, dtype=np.float32)[:, src], BF16)
    return pl.pallas_call(
        _split_kernel,
        out_shape=[jax.ShapeDtypeStruct((e, d, n2 // 2), BF16), jax.ShapeDtypeStruct((e, d, n2 // 2), BF16)],
        grid=(e, n2 // cw),
        in_specs=[pl.BlockSpec((None, d, cw), lambda i, c: (i, 0, c)), _const_spec((cw, cw))],
        out_specs=[pl.BlockSpec((None, d, cw // 2), lambda i, c: (i, 0, c)),
                   pl.BlockSpec((None, d, cw // 2), lambda i, c: (i, 0, c))],
        compiler_params=_cparams(("parallel", "parallel")),
        name="split_gate_up",
    )(w_gu, perm)


def _experts(xb, blk_meta, w_glu, w_lin, b_glu, b_lin, w_dn, b_dn):
    n_rows = xb.shape[0] // ROW_TILES
    bm = EXPERT_ROWS
    vec = lambda n: pl.BlockSpec((None, 1, n), lambda i, be, *_: (be[i], 0, 0))
    rows = pl.BlockSpec((bm * ROW_TILES, LANES), lambda i, *_: (i, 0))
    hbm = pl.BlockSpec(memory_space=pl.ANY)
    grid_spec = pltpu.PrefetchScalarGridSpec(
        num_scalar_prefetch=5,
        grid=(n_rows // bm,),
        in_specs=[rows, hbm, hbm, vec(D_EXPERT), vec(D_EXPERT), hbm, vec(D_MODEL)],
        out_specs=rows,
        scratch_shapes=[pltpu.VMEM((2, D_MODEL, D_EXPERT), BF16), pltpu.VMEM((2, D_MODEL, D_EXPERT), BF16),
                        pltpu.VMEM((2, D_EXPERT, D_MODEL), BF16), pltpu.SemaphoreType.DMA((2, 3))],
    )
    return pl.pallas_call(
        _expert_kernel,
        out_shape=jax.ShapeDtypeStruct((n_rows * ROW_TILES, LANES), F32),
        grid_spec=grid_spec,
        compiler_params=_cparams(("arbitrary",)),
        name="moe_experts",
    )(*blk_meta, xb, w_glu, w_lin, b_glu, b_lin, w_dn, b_dn)


def _combine_kernel(h_ref, gate_ref, y0_ref, y1_ref, y2_ref, y3_ref, g_ref, b_ref, *rest):
    o_ref = rest[-1]
    tt = o_ref.shape[0]
    gates = gate_ref[...]
    acc = DN_ALPHA * _rows_2d(h_ref, tt)
    for kk, y_ref in enumerate((y0_ref, y1_ref, y2_ref, y3_ref)):
        acc = acc + gates[:, kk:kk + 1] * _rows_2d(y_ref, tt)
    o_ref[...] = _layer_norm_rows(acc, g_ref[...], b_ref[...])


def _combine(h2, gate_slab, yg, ln_g, ln_b, part, prev_out):
    t_all = h2.shape[0] // ROW_TILES
    t = yg.shape[0] // (TOP_K * ROW_TILES)
    tt = min(COMBINE_TOKENS, t)
    steps = t // tt
    first = part * steps
    row = lambda m: pl.BlockSpec((tt, m), lambda i: (first + i, 0))
    tiles = lambda kk: pl.BlockSpec((tt * ROW_TILES, LANES), lambda i: (kk * steps + i, 0))
    in_specs = [pl.BlockSpec((tt * ROW_TILES, LANES), lambda i: (first + i, 0)), row(LANES),
                tiles(0), tiles(1), tiles(2), tiles(3), _const_spec((1, D_MODEL)), _const_spec((1, D_MODEL))]
    args = [h2, gate_slab, yg, yg, yg, yg, ln_g.reshape(1, -1), ln_b.reshape(1, -1)]
    aliases = {}
    if prev_out is not None:
        in_specs.append(pl.BlockSpec(memory_space=pl.ANY))
        args.append(prev_out)
        aliases = {len(args) - 1: 0}
    return pl.pallas_call(
        _combine_kernel,
        out_shape=jax.ShapeDtypeStruct((t_all, D_MODEL), F32),
        grid=(steps,),
        in_specs=in_specs,
        out_specs=row(D_MODEL),
        input_output_aliases=aliases,
        compiler_params=_cparams(("parallel",)),
        name="moe_combine",
    )(*args)


def _moe_part(h2, idx_slab, gate_slab, expert_w, ln_g, ln_b, part, prev_out):
    t = h2.shape[0] // ROW_TILES // MOE_PARTS
    bm = EXPERT_ROWS
    n_assign = t * TOP_K
    n_rows = n_assign + N_EXPERTS * bm
    n_blocks = n_rows // bm

    rank_slab, cnt = _routing_ranks(idx_slab, part, t)
    counts = cnt[0, :N_EXPERTS]
    padded = (counts + bm - 1) // bm * bm
    pends = jnp.cumsum(padded)
    pstarts = pends - padded
    pstart_row = jnp.zeros((8, LANES), I32).at[:, :N_EXPERTS].set(pstarts[None, :])
    blk_start = jnp.arange(n_blocks, dtype=I32) * bm
    blk_e = jnp.minimum(jnp.sum((pends[None, :] <= blk_start[:, None]).astype(I32), axis=1), N_EXPERTS - 1)
    nblk = (pends[-1] // bm).astype(I32).reshape(1)
    active = counts > 0
    ordinal = jnp.cumsum(active.astype(I32)) - 1
    eid = jnp.arange(N_EXPERTS, dtype=I32)
    later = active[None, :] & (eid[None, :] > eid[:, None])
    next_active = jnp.where(later.any(axis=1), jnp.argmax(later, axis=1).astype(I32), -1)
    blk_first = jnp.concatenate([jnp.ones((1,), I32), (blk_e[1:] != blk_e[:-1]).astype(I32)])
    blk_meta = (blk_e, blk_first, ordinal[blk_e] % 2, next_active[blk_e], nblk)
    dest_slab = _dest_rows(idx_slab, rank_slab, pstart_row, part)
    dest_tk = dest_slab[:, :TOP_K]
    pad_off = jnp.arange(bm, dtype=I32)[None, :]
    pad_rows = jnp.where(pad_off < (padded - counts)[:, None], (pstarts + counts)[:, None] + pad_off,
                         n_rows - 1).reshape(-1)

    xb = _dispatch(h2.reshape(-1, ROW_TILES, LANES), dest_tk, pad_rows, n_rows, part * t)
    yb = _experts(xb.reshape(n_rows * ROW_TILES, LANES), blk_meta, *expert_w)
    yg = _gather_rows(yb.reshape(n_rows, ROW_TILES, LANES), dest_tk.T.reshape(-1))
    return _combine(h2, gate_slab, yg.reshape(n_assign * ROW_TILES, LANES), ln_g, ln_b, part, prev_out)


def _moe(h2, idx_slab, gate_slab, w_gu, b_gu, w_dn, b_dn, ln_g, ln_b):
    w_glu, w_lin = _split_gate_up(w_gu)
    expert_w = (w_glu, w_lin, b_gu[:, None, 0::2], b_gu[:, None, 1::2], w_dn.astype(BF16), b_dn[:, None, :])
    out = None
    for part in range(MOE_PARTS):
        out = _moe_part(h2, idx_slab, gate_slab, expert_w, ln_g, ln_b, part, out)
    return out


def kernel(x, mem, positions, w_in, b_igate, b_fgate, conv_w, conv_b, w_mq, w_mk, g_mhead, w_mskip, g_qlat, g_kvlat,
           w_uq, w_ukv, w_br_m, w_br_a, w_mix_out, ln1_g, ln1_b, w_cq, w_ckv, w_co, ln2_g, ln2_b, w_router, b_router,
           w_gu, b_gu, w_dn, b_dn, ln3_g, ln3_b):
    bsz, s, d = x.shape
    t = bsz * s
    h = x.reshape(t, d)
    pos2 = positions.reshape(t, 1)
    for l in range(DEPTH):
        q, k, v, xm, vm, op, gm, ga, gate = _input_projection(h, pos2, w_in[l], g_qlat[l], g_kvlat[l], w_uq[l],
                                                              w_ukv[l])
        ym = _mlstm_branch(xm, vm, op, gate, bsz, s, b_igate[l], b_fgate[l], conv_w[l], conv_b[l], w_mq[l], w_mk[l],
                           g_mhead[l], w_mskip[l])
        oa = _mla_attention(q, k, v, bsz, s)
        h1 = _merge(ym, oa, gm, ga, h, w_br_m[l], w_br_a[l], w_mix_out[l], ln1_g[l], ln1_b[l])
        kv = _mem_kv(mem.reshape(-1, d), w_ckv[l])
        h2, idx_slab, gate_slab = _cross_attention_router(h1, kv, bsz, s, w_cq[l], w_co[l], ln2_g[l], ln2_b[l],
                                                          w_router[l], b_router[l])
        h = _moe(h2, idx_slab, gate_slab, w_gu[l], b_gu[l], w_dn[l], b_dn[l], ln3_g[l], ln3_b[l])
    return h.reshape(bsz, s, d)
```

```python
import functools

import numpy as np
import jax
import jax.numpy as jnp
from jax import lax
from jax.experimental import pallas as pl
from jax.experimental.pallas import tpu as pltpu
from jax.experimental.pallas import tpu_sc as plsc

F32 = jnp.float32
BF16 = jnp.bfloat16
I32 = jnp.int32

D_MODEL = 1024
N_MEM = 256
M_HEADS = 4
M_HEAD_DIM = 128
M_WIDTH = M_HEADS * M_HEAD_DIM
M_CONV = 4
A_HEADS = 8
A_NOPE = 64
A_ROPE = 32
A_QK = A_NOPE + A_ROPE
A_VDIM = 64
A_Q_RANK = 256
A_KV_RANK = 128
ROPE_THETA = 10000.0
X_HEADS = 4
X_HEAD_DIM = D_MODEL // X_HEADS
N_EXPERTS = 32
TOP_K = 4
D_EXPERT = D_MODEL
SWIGLU_ALPHA = 1.702
SWIGLU_LIMIT = 7.0
DEPTH = 1
DN_ALPHA = (2.0 * DEPTH) ** 0.25
EPS = 1e-5
IN_SPLITS = (A_Q_RANK, A_KV_RANK, A_ROPE, M_WIDTH, M_WIDTH, M_WIDTH, M_HEADS, M_HEADS, D_MODEL, D_MODEL)
IN_OFFSETS = tuple(int(v) for v in np.cumsum((0,) + IN_SPLITS))

LANES = 128
VMEM_LIMIT = 56 * 1024 * 1024

PROJ_ROWS = 512
M_CHUNK_ROWS = 128
ATT_Q = 256
MERGE_ROWS = 512
XATT_ROWS = 512
RANK_ROWS = 512
EXPERT_ROWS = 256
COMBINE_TOKENS = 256
MOE_PARTS = 2
SC_CORES = 2
SC_SUBCORES = 16
SC_WORKERS = SC_CORES * SC_SUBCORES
SC_ROWS = 32
NEG_BIG = -1e30
LOG2_E = 1.4426950408889634
ROW_TILES = D_MODEL // LANES

C_QLAT = 0
C_KVLAT = C_QLAT + A_Q_RANK
C_KR = C_KVLAT + A_KV_RANK
C_KRS = C_KR + LANES
C_GATE = C_KRS + LANES
C_XM = C_GATE + LANES
C_VM = C_XM + M_WIDTH
C_OP = C_VM + M_WIDTH
C_GM = C_OP + M_WIDTH
C_GA = C_GM + D_MODEL
C_END = C_GA + D_MODEL


def _cparams(sem, vmem=VMEM_LIMIT):
    return pltpu.CompilerParams(dimension_semantics=sem, vmem_limit_bytes=vmem)


def _const_spec(shape):
    nd = len(shape)
    return pl.BlockSpec(shape, lambda *a: (0,) * nd)


def _layer_norm_rows(v, g, b):
    mu = jnp.mean(v, axis=-1, keepdims=True)
    d = v - mu
    var = jnp.mean(d * d, axis=-1, keepdims=True)
    return d * lax.rsqrt(var + EPS) * g + b


def _proj_kernel(x_ref, pos_ref, w_ref, gq_ref, gkv_ref, wuq_ref, wuqs_ref, wuk_ref, wuv_ref, invf_ref,
                 q_ref, k_ref, v_ref, xm_ref, vm_ref, op_ref, gm_ref, ga_ref, gate_ref):
    xb = x_ref[...].astype(BF16)

    def mm(lo, n):
        return jnp.dot(xb, w_ref[:, lo:lo + n], preferred_element_type=F32)

    xm_ref[...] = mm(C_XM, M_WIDTH).astype(BF16)
    vm_ref[...] = mm(C_VM, M_WIDTH).astype(BF16)
    op_ref[...] = mm(C_OP, M_WIDTH).astype(BF16)
    gm_ref[...] = mm(C_GM, D_MODEL).astype(BF16)
    ga_ref[...] = mm(C_GA, D_MODEL).astype(BF16)
    gate_ref[...] = mm(C_GATE, LANES)

    ang = pos_ref[...].astype(F32) * invf_ref[...]
    cos = jnp.cos(ang)
    sin = jnp.sin(ang)

    q_lat = mm(C_QLAT, A_Q_RANK)
    qn = (q_lat * lax.rsqrt(jnp.mean(q_lat * q_lat, axis=-1, keepdims=True) + EPS) * gq_ref[...]).astype(BF16)
    q = jnp.dot(qn, wuq_ref[...], preferred_element_type=F32)
    qs = jnp.dot(qn, wuqs_ref[...], preferred_element_type=F32)
    kv_lat = mm(C_KVLAT, A_KV_RANK)
    kvn = (kv_lat * lax.rsqrt(jnp.mean(kv_lat * kv_lat, axis=-1, keepdims=True) + EPS) * gkv_ref[...]).astype(BF16)
    kn = jnp.dot(kvn, wuk_ref[...], preferred_element_type=F32)
    lane = lax.broadcasted_iota(I32, (1, A_HEADS * LANES), 1)
    ones_lane = (lane % LANES == A_VDIM).astype(F32)
    v_ref[...] = (jnp.dot(kvn, wuv_ref[...], preferred_element_type=F32) + ones_lane).astype(BF16)
    k_pe = mm(C_KR, LANES) * cos + mm(C_KRS, LANES) * sin
    scale = A_QK ** -0.5 * LOG2_E
    for h in range(A_HEADS):
        sl = slice(h * LANES, (h + 1) * LANES)
        q_ref[:, sl] = ((q[:, sl] * cos + qs[:, sl] * sin) * scale).astype(BF16)
        k_ref[:, sl] = (kn[:, sl] + k_pe).astype(BF16)


def _proj_weights(w_in, w_uq, w_ukv):
    o = IN_OFFSETS
    half = A_ROPE // 2
    w_q, w_kv, w_kr = w_in[:, o[0]:o[1]], w_in[:, o[1]:o[2]], w_in[:, o[2]:o[3]]
    w_xm, w_vm, w_op = w_in[:, o[3]:o[4]], w_in[:, o[4]:o[5]], w_in[:, o[5]:o[6]]
    w_i, w_f, w_gm, w_ga = w_in[:, o[6]:o[7]], w_in[:, o[7]:o[8]], w_in[:, o[8]:o[9]], w_in[:, o[9]:o[10]]
    d = w_in.shape[0]
    z = lambda n: jnp.zeros((d, n), w_in.dtype)
    kr = jnp.concatenate([z(A_NOPE), w_kr, z(LANES - A_QK)], axis=1)
    krs = jnp.concatenate([z(A_NOPE), -w_kr[:, half:], w_kr[:, :half], z(LANES - A_QK)], axis=1)
    gate = jnp.concatenate([w_i, w_f, z(LANES - 2 * M_HEADS)], axis=1)
    w_r = jnp.concatenate([w_q, w_kv, kr, krs, gate, w_xm, w_vm, w_op, w_gm, w_ga], axis=1).astype(BF16)

    uq = w_uq.reshape(A_Q_RANK, A_HEADS, A_QK)
    zq = jnp.zeros((A_Q_RANK, A_HEADS, LANES - A_QK), w_uq.dtype)
    zn = jnp.zeros((A_Q_RANK, A_HEADS, A_NOPE), w_uq.dtype)
    uq_pad = jnp.concatenate([uq, zq], axis=-1).reshape(A_Q_RANK, A_HEADS * LANES).astype(BF16)
    uqs_pad = jnp.concatenate([zn, -uq[..., A_NOPE + half:], uq[..., A_NOPE:A_NOPE + half], zq],
                              axis=-1).reshape(A_Q_RANK, A_HEADS * LANES).astype(BF16)
    ukv = w_ukv.reshape(A_KV_RANK, A_HEADS, A_NOPE + A_VDIM)
    zk = jnp.zeros((A_KV_RANK, A_HEADS, LANES - A_NOPE), w_ukv.dtype)
    uk_pad = jnp.concatenate([ukv[..., :A_NOPE], zk], axis=-1).reshape(A_KV_RANK, A_HEADS * LANES).astype(BF16)
    zv = jnp.zeros((A_KV_RANK, A_HEADS, LANES - A_VDIM), w_ukv.dtype)
    uv_pad = jnp.concatenate([ukv[..., A_NOPE:], zv], axis=-1).reshape(A_KV_RANK, A_HEADS * LANES).astype(BF16)
    return w_r, uq_pad, uqs_pad, uk_pad, uv_pad


def _input_projection(x2, pos2, w_in, g_qlat, g_kvlat, w_uq, w_ukv):
    t = x2.shape[0]
    tm = min(PROJ_ROWS, t)
    w_r, uq_pad, uqs_pad, uk_pad, uv_pad = _proj_weights(w_in, w_uq, w_ukv)
    half = A_ROPE // 2
    inv_freq = ROPE_THETA ** (-jnp.arange(half, dtype=F32) / half)
    invf = jnp.concatenate([jnp.zeros((A_NOPE,), F32), inv_freq, inv_freq,
                            jnp.zeros((LANES - A_QK,), F32)]).reshape(1, LANES)
    hw = A_HEADS * LANES
    row = lambda n: pl.BlockSpec((tm, n), lambda i: (i, 0))
    outs = [
        jax.ShapeDtypeStruct((t, hw), BF16), jax.ShapeDtypeStruct((t, hw), BF16), jax.ShapeDtypeStruct((t, hw), BF16),
        jax.ShapeDtypeStruct((t, M_WIDTH), BF16), jax.ShapeDtypeStruct((t, M_WIDTH), BF16),
        jax.ShapeDtypeStruct((t, M_WIDTH), BF16),
        jax.ShapeDtypeStruct((t, D_MODEL), BF16), jax.ShapeDtypeStruct((t, D_MODEL), BF16),
        jax.ShapeDtypeStruct((t, LANES), F32),
    ]
    return pl.pallas_call(
        _proj_kernel,
        out_shape=outs,
        grid=(t // tm,),
        in_specs=[row(D_MODEL), row(1), _const_spec(w_r.shape), _const_spec((1, A_Q_RANK)),
                  _const_spec((1, A_KV_RANK)), _const_spec(uq_pad.shape), _const_spec(uqs_pad.shape),
                  _const_spec(uk_pad.shape), _const_spec(uv_pad.shape), _const_spec((1, LANES))],
        out_specs=[row(hw), row(hw), row(hw), row(M_WIDTH), row(M_WIDTH), row(M_WIDTH), row(D_MODEL), row(D_MODEL),
                   row(LANES)],
        compiler_params=_cparams(("parallel",)),
        name="input_projection",
    )(x2, pos2, w_r, g_qlat.reshape(1, -1), g_kvlat.reshape(1, -1), uq_pad, uqs_pad, uk_pad, uv_pad, invf)


def _log_sigmoid(v):
    return jnp.minimum(v, 0.0) - jnp.log1p(jnp.exp(-jnp.abs(v)))


def _mlstm_kernel(xm_ref, vm_ref, op_ref, gate_ref, convw_ref, convb_ref, wqk_ref, gbias_ref, ghead_ref,
                  skip_ref, tril_ref, out_ref, xpad_s, ctn_s, m_s):
    s = xm_ref.shape[0]
    lc = M_CHUNK_ROWS
    halo = 8
    xpad_s[0:halo, :] = jnp.zeros((halo, M_WIDTH), F32)
    xpad_s[halo:, :] = xm_ref[...].astype(F32)
    ctn_s[...] = jnp.zeros_like(ctn_s)
    m_s[...] = jnp.zeros_like(m_s)
    rows = lax.broadcasted_iota(I32, (lc, lc), 0)
    cols = lax.broadcasted_iota(I32, (lc, lc), 1)
    causal = rows >= cols
    row_id = lax.broadcasted_iota(I32, (lc, LANES), 0)
    ones_blk = (lax.broadcasted_iota(I32, (lc, LANES), 1) == 0).astype(BF16)
    kscale = M_HEAD_DIM ** -0.5

    def chunk(c, carry):
        r0 = pl.multiple_of(c * lc, lc)
        win = xpad_s[pl.ds(r0, lc + halo), :]
        conv = convb_ref[...]
        for j in range(M_CONV):
            lo = halo - (M_CONV - 1) + j
            conv = conv + win[lo:lo + lc, :] * convw_ref[j:j + 1, :]
        xc = conv * jax.nn.sigmoid(conv)
        g = gate_ref[pl.ds(r0, lc), :] + gbias_ref[...]
        ls = _log_sigmoid(g)
        bc = jnp.dot(tril_ref[...], ls, preferred_element_type=F32, precision=lax.Precision.HIGHEST)
        b0 = pltpu.roll(bc, LANES - M_HEADS, axis=1)
        u = g - b0
        cm = u
        shift = 1
        while shift < lc:
            cm = jnp.maximum(cm, jnp.where(row_id >= shift, pltpu.roll(cm, shift, axis=0), -jnp.inf))
            shift *= 2
        m_prev = m_s[0:1, :]
        g_inter = b0 + m_prev
        m = jnp.maximum(g_inter, b0 + cm)
        w_inter = jnp.exp(g_inter - m)
        e_negm = jnp.exp(-m)
        a_mat = b0 - m
        b_tot = b0[lc - 1:lc, :]
        aw = b_tot - b0 + g
        m_chunk = jnp.max(aw, axis=0, keepdims=True)
        wa = jnp.exp(aw - m_chunk)
        m_new = jnp.maximum(b_tot + m_prev, m_chunk)
        s_old = jnp.exp(b_tot + m_prev - m_new)
        s_new = jnp.exp(m_chunk - m_new)
        m_s[0:1, :] = m_new
        u_t = u.T
        heads = range(M_HEADS)
        hsl = [slice(h * M_HEAD_DIM, (h + 1) * M_HEAD_DIM) for h in heads]
        hcl = [slice(h, h + 1) for h in heads]
        qk = jnp.dot(xc.astype(BF16), wqk_ref[...], preferred_element_type=F32)
        q_b = [qk[:, hsl[h]].astype(BF16) for h in heads]
        k_f = [qk[:, M_WIDTH + h * M_HEAD_DIM:M_WIDTH + (h + 1) * M_HEAD_DIM] * kscale for h in heads]
        v_aug = [jnp.concatenate([vm_ref[pl.ds(r0, lc), hsl[h]], ones_blk], axis=1) for h in heads]
        ctn_prev = [ctn_s[h] for h in heads]
        s_raw = [lax.dot_general(q_b[h], k_f[h].astype(BF16), (((1,), (1,)), ((), ())), preferred_element_type=F32)
                 for h in heads]
        inter = [jnp.dot(q_b[h], ctn_prev[h].astype(BF16), preferred_element_type=F32) for h in heads]
        ctn_c = [lax.dot_general((k_f[h] * wa[:, hcl[h]]).astype(BF16), v_aug[h], (((0,), (0,)), ((), ())),
                                 preferred_element_type=F32) for h in heads]
        sc_b = [(s_raw[h] * jnp.exp(jnp.where(causal, a_mat[:, hcl[h]] + u_t[hcl[h], :], -jnp.inf))).astype(BF16)
                for h in heads]
        intra = [jnp.dot(sc_b[h], v_aug[h], preferred_element_type=F32) for h in heads]
        for h in heads:
            ctn_s[h] = s_old[:, hcl[h]] * ctn_prev[h] + s_new[:, hcl[h]] * ctn_c[h]
            wi = w_inter[:, hcl[h]]
            num = wi * inter[h][:, :M_HEAD_DIM] + intra[h][:, :M_HEAD_DIM]
            den = wi * inter[h][:, M_HEAD_DIM:M_HEAD_DIM + 1] + intra[h][:, M_HEAD_DIM:M_HEAD_DIM + 1]
            hh = num / jnp.maximum(jnp.abs(den), e_negm[:, hcl[h]])
            mu = jnp.mean(hh, axis=-1, keepdims=True)
            dv = hh - mu
            var = jnp.mean(dv * dv, axis=-1, keepdims=True)
            hn = dv * lax.rsqrt(var + EPS) * ghead_ref[:, hsl[h]]
            y = jax.nn.sigmoid(op_ref[pl.ds(r0, lc), hsl[h]].astype(F32)) * (hn + skip_ref[:, hsl[h]] * xc[:, hsl[h]])
            out_ref[pl.ds(r0, lc), hsl[h]] = y.astype(BF16)
        return carry

    lax.fori_loop(0, s // lc, chunk, 0)


def _mlstm_branch(xm, vm, op, gate, bsz, s, b_i, b_f, conv_w, conv_b, w_mq, w_mk, g_mhead, w_mskip):
    lc = M_CHUNK_ROWS
    gbias = jnp.concatenate([b_i, b_f, jnp.zeros((LANES - 2 * M_HEADS,), F32)]).reshape(1, LANES)
    tril = jnp.tril(jnp.ones((lc, lc), F32))
    eye = jnp.eye(M_HEADS, dtype=F32)
    block_diag = lambda w: jnp.einsum('hde,hg->hdge', w, eye).reshape(M_WIDTH, M_WIDTH)
    w_qk = jnp.concatenate([block_diag(w_mq), block_diag(w_mk)], axis=1).astype(BF16)
    seq = lambda n: pl.BlockSpec((s, n), lambda b: (b, 0))
    return pl.pallas_call(
        _mlstm_kernel,
        out_shape=jax.ShapeDtypeStruct((bsz * s, M_WIDTH), BF16),
        grid=(bsz,),
        in_specs=[seq(M_WIDTH), seq(M_WIDTH), seq(M_WIDTH), seq(LANES),
                  _const_spec((M_CONV, M_WIDTH)), _const_spec((1, M_WIDTH)),
                  _const_spec((M_WIDTH, 2 * M_WIDTH)),
                  _const_spec((1, LANES)), _const_spec((1, M_WIDTH)), _const_spec((1, M_WIDTH)),
                  _const_spec((lc, lc))],
        out_specs=seq(M_WIDTH),
        scratch_shapes=[pltpu.VMEM((s + 8, M_WIDTH), F32), pltpu.VMEM((M_HEADS, M_HEAD_DIM, 2 * LANES), F32),
                        pltpu.VMEM((8, LANES), F32)],
        compiler_params=_cparams(("parallel",)),
        name="mlstm_branch",
    )(xm, vm, op, gate, conv_w, conv_b.reshape(1, -1), w_qk, gbias, g_mhead.reshape(1, -1), w_mskip.reshape(1, -1),
      tril)


def _attn_kernel(q_ref, k_ref, v_ref, o_ref, s_scr, m_s, acc_s):
    tq = q_ref.shape[0]
    i = pl.program_id(1)
    rows = lax.broadcasted_iota(I32, (tq, tq), 0)
    cols = lax.broadcasted_iota(I32, (tq, tq), 1)

    def lane_tile_max(sc):
        out = sc[:, :LANES]
        for t in range(1, tq // LANES):
            out = jnp.maximum(out, sc[:, t * LANES:(t + 1) * LANES])
        return out

    heads = [slice(h * LANES, (h + 1) * LANES) for h in range(A_HEADS)]
    m_s[...] = jnp.full(m_s.shape, NEG_BIG, F32)
    acc_s[...] = jnp.zeros_like(acc_s)

    def pass_a(j, diagonal):
        k0 = pl.multiple_of(j * tq, tq)
        for h, hs in enumerate(heads):
            sc = lax.dot_general(q_ref[:, hs], k_ref[pl.ds(k0, tq), hs], (((1,), (1,)), ((), ())),
                                 preferred_element_type=F32)
            if diagonal:
                sc = jnp.where(cols <= rows, sc, NEG_BIG)
            s_scr[h, j] = sc
            m_s[h] = jnp.maximum(m_s[h], lane_tile_max(sc))

    def for_blocks(n, fn):
        def pair(p, c):
            fn(2 * p)
            fn(2 * p + 1)
            return c

        lax.fori_loop(0, n // 2, pair, 0)

        @pl.when(n % 2 == 1)
        def _():
            fn(n - 1)

    for_blocks(i, lambda j: pass_a(j, False))
    pass_a(i, True)
    for h in range(A_HEADS):
        m_s[h] = jnp.broadcast_to(jnp.max(m_s[h], axis=1, keepdims=True), (tq, LANES))

    def pass_b(j):
        k0 = pl.multiple_of(j * tq, tq)
        for h, hs in enumerate(heads):
            m_row = m_s[h]
            p = jnp.exp2(s_scr[h, j] - jnp.concatenate([m_row] * (tq // LANES), axis=1)).astype(BF16)
            acc_s[h] += jnp.dot(p, v_ref[pl.ds(k0, tq), hs], preferred_element_type=F32)

    for_blocks(i + 1, pass_b)
    for h, hs in enumerate(heads):
        acc = acc_s[h]
        o_ref[:, hs] = (acc / acc[:, A_VDIM:A_VDIM + 1]).astype(BF16)


def _mla_attention(q, k, v, bsz, s):
    tq = min(ATT_Q, s)
    nq = s // tq
    hw = A_HEADS * LANES
    return pl.pallas_call(
        _attn_kernel,
        out_shape=jax.ShapeDtypeStruct(q.shape, BF16),
        grid=(bsz, nq),
        in_specs=[pl.BlockSpec((tq, hw), lambda b, i: (b * nq + i, 0)),
                  pl.BlockSpec((s, hw), lambda b, i: (b, 0)),
                  pl.BlockSpec((s, hw), lambda b, i: (b, 0))],
        out_specs=pl.BlockSpec((tq, hw), lambda b, i: (b * nq + i, 0)),
        scratch_shapes=[pltpu.VMEM((A_HEADS, nq, tq, tq), F32), pltpu.VMEM((A_HEADS, tq, LANES), F32),
                        pltpu.VMEM((A_HEADS, tq, LANES), F32)],
        compiler_params=_cparams(("parallel", "arbitrary")),
        name="mla_attention",
    )(q, k, v)


def _merge_kernel(ym_ref, oa_ref, gm_ref, ga_ref, x_ref, wbm_ref, wba_ref, wmix_ref, g_ref, b_ref, h_ref):
    y_m = jnp.dot(ym_ref[...], wbm_ref[...], preferred_element_type=F32)
    y_a = jnp.dot(oa_ref[...], wba_ref[...], preferred_element_type=F32)
    mixed = jax.nn.sigmoid(gm_ref[...].astype(F32)) * y_m + jax.nn.sigmoid(ga_ref[...].astype(F32)) * y_a
    mix = jnp.dot(mixed.astype(BF16), wmix_ref[...], preferred_element_type=F32)
    h_ref[...] = _layer_norm_rows(DN_ALPHA * x_ref[...] + mix, g_ref[...], b_ref[...])


def _merge(ym, oa, gm, ga, x2, w_br_m, w_br_a, w_mix_out, ln_g, ln_b):
    t = x2.shape[0]
    tm = min(MERGE_ROWS, t)
    wba = jnp.concatenate([w_br_a.reshape(A_HEADS, A_VDIM, D_MODEL),
                           jnp.zeros((A_HEADS, LANES - A_VDIM, D_MODEL), w_br_a.dtype)], axis=1)
    wba = wba.reshape(A_HEADS * LANES, D_MODEL).astype(BF16)
    row = lambda n: pl.BlockSpec((tm, n), lambda i: (i, 0))
    return pl.pallas_call(
        _merge_kernel,
        out_shape=jax.ShapeDtypeStruct((t, D_MODEL), F32),
        grid=(t // tm,),
        in_specs=[row(M_WIDTH), row(A_HEADS * LANES), row(D_MODEL), row(D_MODEL), row(D_MODEL),
                  _const_spec((M_WIDTH, D_MODEL)), _const_spec((A_HEADS * LANES, D_MODEL)),
                  _const_spec((D_MODEL, D_MODEL)), _const_spec((1, D_MODEL)), _const_spec((1, D_MODEL))],
        out_specs=row(D_MODEL),
        compiler_params=_cparams(("parallel",)),
        name="merge_deepnorm1",
    )(ym, oa, gm, ga, x2, w_br_m.astype(BF16), wba, w_mix_out.astype(BF16), ln_g.reshape(1, -1), ln_b.reshape(1, -1))


def _kv_kernel(m_ref, w_ref, o_ref):
    o_ref[...] = jnp.dot(m_ref[...].astype(BF16), w_ref[...], preferred_element_type=F32).astype(BF16)


def _mem_kv(mem2, w_ckv):
    r = mem2.shape[0]
    tm = min(512, r)
    return pl.pallas_call(
        _kv_kernel,
        out_shape=jax.ShapeDtypeStruct((r, 2 * D_MODEL), BF16),
        grid=(r // tm,),
        in_specs=[pl.BlockSpec((tm, D_MODEL), lambda i: (i, 0)), _const_spec((D_MODEL, 2 * D_MODEL))],
        out_specs=pl.BlockSpec((tm, 2 * D_MODEL), lambda i: (i, 0)),
        compiler_params=_cparams(("parallel",)),
        name="memory_kv",
    )(mem2, w_ckv.astype(BF16))


def _xattn_kernel(h_ref, kv_ref, wq_ref, wo_ref, g_ref, b_ref, wr_ref, br_ref, h2_ref, idx_ref, gate_ref):
    h1 = h_ref[...]
    q = jnp.dot(h1.astype(BF16), wq_ref[...], preferred_element_type=F32).astype(BF16)
    scale = X_HEAD_DIM ** -0.5
    ksl = [slice(hd * X_HEAD_DIM, (hd + 1) * X_HEAD_DIM) for hd in range(X_HEADS)]
    vsl = [slice(D_MODEL + hd * X_HEAD_DIM, D_MODEL + (hd + 1) * X_HEAD_DIM) for hd in range(X_HEADS)]
    scs = [lax.dot_general(q[:, ks], kv_ref[:, ks], (((1,), (1,)), ((), ())), preferred_element_type=F32) * scale
           for ks in ksl]
    ps = [jnp.exp(sc - jnp.max(sc, axis=1, keepdims=True)) for sc in scs]
    ps = [(p / jnp.sum(p, axis=1, keepdims=True)).astype(BF16) for p in ps]
    o = jnp.concatenate([jnp.dot(p, kv_ref[:, vs], preferred_element_type=F32).astype(BF16)
                         for p, vs in zip(ps, vsl)], axis=1)
    att = jnp.dot(o, wo_ref[...], preferred_element_type=F32)
    h2 = _layer_norm_rows(DN_ALPHA * h1 + att, g_ref[...], b_ref[...])
    for c in range(ROW_TILES):
        h2_ref[pl.ds(c, h2.shape[0], stride=ROW_TILES), :] = h2[:, c * LANES:(c + 1) * LANES]

    h2_hi = h2.astype(BF16)
    h2_lo = (h2 - h2_hi.astype(F32)).astype(BF16)
    hi_prod = jnp.dot(h2_hi, wr_ref[...], preferred_element_type=F32)
    lo_prod = jnp.dot(h2_lo, wr_ref[:, :LANES], preferred_element_type=F32)
    logits = hi_prod[:, :LANES] + (hi_prod[:, LANES:] + lo_prod) + br_ref[...]
    tm = logits.shape[0]
    lane = lax.broadcasted_iota(I32, (tm, LANES), 1)
    lane_f = lane.astype(F32)
    work = jnp.where(lane < N_EXPERTS, logits, -jnp.inf)
    vals, idxs = [], []
    for _ in range(TOP_K):
        mx = jnp.max(work, axis=1, keepdims=True)
        ix = jnp.min(jnp.where(work == mx, lane_f, float(LANES)), axis=1, keepdims=True)
        vals.append(mx)
        idxs.append(ix)
        work = jnp.where(lane_f == ix, -jnp.inf, work)
    es = [jnp.exp(vv - vals[0]) for vv in vals]
    tot = es[0] + es[1] + es[2] + es[3]
    idx_slab = jnp.zeros((tm, LANES), F32)
    gate_slab = jnp.zeros((tm, LANES), F32)
    for kk in range(TOP_K):
        idx_slab = jnp.where(lane == kk, idxs[kk], idx_slab)
        gate_slab = jnp.where(lane == kk, es[kk] / tot, gate_slab)
    idx_ref[...] = idx_slab.astype(I32)
    gate_ref[...] = gate_slab


def _cross_attention_router(h1, kv, bsz, s, w_cq, w_co, ln_g, ln_b, w_router, b_router):
    t = h1.shape[0]
    tm = min(XATT_ROWS, s)
    ns = s // tm
    n_mem = kv.shape[0] // bsz
    wr = jnp.concatenate([w_router, jnp.zeros((D_MODEL, LANES - N_EXPERTS), F32)], axis=1)
    wr_hi = wr.astype(BF16)
    wr = jnp.concatenate([wr_hi, (wr - wr_hi.astype(F32)).astype(BF16)], axis=1)
    br = jnp.concatenate([b_router, jnp.zeros((LANES - N_EXPERTS,), F32)]).reshape(1, LANES)
    row = lambda n: pl.BlockSpec((tm, n), lambda b, i: (b * ns + i, 0))
    return pl.pallas_call(
        _xattn_kernel,
        out_shape=[jax.ShapeDtypeStruct((t * ROW_TILES, LANES), F32), jax.ShapeDtypeStruct((t, LANES), I32),
                   jax.ShapeDtypeStruct((t, LANES), F32)],
        grid=(bsz, ns),
        in_specs=[row(D_MODEL), pl.BlockSpec((n_mem, 2 * D_MODEL), lambda b, i: (b, 0)),
                  _const_spec((D_MODEL, D_MODEL)), _const_spec((D_MODEL, D_MODEL)),
                  _const_spec((1, D_MODEL)), _const_spec((1, D_MODEL)),
                  _const_spec((D_MODEL, 2 * LANES)), _const_spec((1, LANES))],
        out_specs=[pl.BlockSpec((tm * ROW_TILES, LANES), lambda b, i: (b * ns + i, 0)), row(LANES), row(LANES)],
        compiler_params=_cparams(("parallel", "parallel")),
        name="cross_attention_router",
    )(h1, kv, w_cq.astype(BF16), w_co.astype(BF16), ln_g.reshape(1, -1), ln_b.reshape(1, -1), wr, br)


def _rank_kernel(idx_ref, ltri_ref, rank_ref, cnt_ref, carry_s):
    @pl.when(pl.program_id(0) == 0)
    def _():
        carry_s[...] = jnp.zeros_like(carry_s)

    idx = idx_ref[...]
    tm = idx.shape[0]
    lane = lax.broadcasted_iota(I32, (tm, LANES), 1)
    onehots = [(lane == idx[:, kk:kk + 1]).astype(F32) for kk in range(TOP_K)]
    sel = onehots[0] + onehots[1] + onehots[2] + onehots[3]
    before = jnp.dot(ltri_ref[...], sel.astype(BF16), preferred_element_type=F32) + carry_s[0:1, :]
    rank_slab = jnp.zeros((tm, LANES), F32)
    for kk in range(TOP_K):
        r = jnp.sum(onehots[kk] * before, axis=1, keepdims=True)
        rank_slab = jnp.where(lane == kk, r, rank_slab)
    rank_ref[...] = rank_slab.astype(I32)
    carry_s[0:1, :] = carry_s[0:1, :] + jnp.sum(sel, axis=0, keepdims=True)
    cnt_ref[...] = jnp.broadcast_to(carry_s[0:1, :], cnt_ref.shape).astype(I32)


def _routing_ranks(idx_slab, part, t):
    tm = min(RANK_ROWS, t)
    first = part * (t // tm)
    ltri = jnp.tril(jnp.ones((tm, tm), BF16), k=-1)
    return pl.pallas_call(
        _rank_kernel,
        out_shape=[jax.ShapeDtypeStruct((t, LANES), I32), jax.ShapeDtypeStruct((8, LANES), I32)],
        grid=(t // tm,),
        in_specs=[pl.BlockSpec((tm, LANES), lambda i: (first + i, 0)), _const_spec((tm, tm))],
        out_specs=[pl.BlockSpec((tm, LANES), lambda i: (i, 0)), _const_spec((8, LANES))],
        scratch_shapes=[pltpu.VMEM((8, LANES), F32)],
        compiler_params=_cparams(("arbitrary",)),
        name="routing_ranks",
    )(idx_slab, ltri)


def _dest_kernel(idx_ref, rank_ref, pstart_ref, dest_ref):
    idx = idx_ref[...]
    tm = idx.shape[0]
    lane = lax.broadcasted_iota(I32, (tm, LANES), 1)
    dest = rank_ref[...].astype(F32)
    pstart = pstart_ref[0:1, :].astype(F32)
    for kk in range(TOP_K):
        start = jnp.sum(jnp.where(lane == idx[:, kk:kk + 1], pstart, 0.0), axis=1, keepdims=True)
        dest = dest + jnp.where(lane == kk, start, 0.0)
    dest_ref[...] = dest.T[0:8, :].astype(I32)


def _dest_rows(idx_slab, rank_slab, pstart_row, part):
    t = rank_slab.shape[0]
    tm = min(RANK_ROWS, t)
    first = part * (t // tm)
    return pl.pallas_call(
        _dest_kernel,
        out_shape=jax.ShapeDtypeStruct((8, t), I32),
        grid=(t // tm,),
        in_specs=[pl.BlockSpec((tm, LANES), lambda i: (first + i, 0)), pl.BlockSpec((tm, LANES), lambda i: (i, 0)),
                  _const_spec((8, LANES))],
        out_specs=pl.BlockSpec((8, tm), lambda i: (0, i)),
        compiler_params=_cparams(("parallel",)),
        name="routing_dest",
    )(idx_slab, rank_slab, pstart_row)


def _sc_mesh():
    return plsc.VectorSubcoreMesh(core_axis_name="c", subcore_axis_name="s", num_cores=SC_CORES,
                                  num_subcores=SC_SUBCORES)


def _sc_worker():
    return lax.axis_index("s") * SC_CORES + lax.axis_index("c")


def _dispatch(h2, dest_kt, pad_rows, n_rows, tok_base):
    t = dest_kt.shape[1]
    w = SC_ROWS
    per_w = t // SC_WORKERS
    steps = per_w // w
    assert steps % 2 == 0 and steps * w * SC_WORKERS == t
    pad_steps = pad_rows.shape[0] // (SC_WORKERS * w)
    dest_w = dest_kt.reshape(TOP_K, SC_WORKERS, steps, w)
    pad_w = pad_rows.reshape(SC_WORKERS, pad_steps, w)
    zeros = jnp.zeros((w, ROW_TILES, LANES), F32)

    def body(x_hbm, dest_hbm, pad_hbm, zeros_hbm, o_hbm, idx_v, pad_v, rows_v, sem_in, sem_out):
        wid = _sc_worker()
        tok0 = tok_base + wid * per_w
        for kk in range(TOP_K):
            pltpu.sync_copy(dest_hbm.at[kk, wid], idx_v.at[pl.ds(kk * steps, steps)])
        pltpu.sync_copy(pad_hbm.at[wid], pad_v)
        pltpu.sync_copy(zeros_hbm, rows_v.at[0])
        for j in range(pad_steps):
            pltpu.sync_copy(rows_v.at[0], o_hbm.at[pad_v.at[j]])

        def load(step, b):
            return pltpu.make_async_copy(x_hbm.at[pl.ds(tok0 + step * w, w)], rows_v.at[b], sem_in.at[b])

        def scatter(step, kk, b):
            return pltpu.make_async_copy(rows_v.at[b], o_hbm.at[idx_v.at[kk * steps + step]], sem_out.at[b])

        load(0, 0).start()

        @pl.loop(0, steps, step=2)
        def _(s0):
            for b in range(2):
                step = s0 + b
                load(step, b).wait()

                @pl.when(step >= 1)
                def _():
                    for kk in range(TOP_K):
                        scatter(step - 1, kk, 1 - b).wait()

                @pl.when(step + 1 < steps)
                def _():
                    load(step + 1, 1 - b).start()

                for kk in range(TOP_K):
                    scatter(step, kk, b).start()

        for kk in range(TOP_K):
            scatter(steps - 1, kk, 1).wait()

    return pl.kernel(
        body,
        out_type=jax.ShapeDtypeStruct((n_rows, ROW_TILES, LANES), F32),
        mesh=_sc_mesh(),
        scratch_types=[pltpu.VMEM((steps * TOP_K, w), I32), pltpu.VMEM((pad_steps, w), I32),
                       pltpu.VMEM((2, w, ROW_TILES, LANES), F32), pltpu.SemaphoreType.DMA((2,)),
                       pltpu.SemaphoreType.DMA((2,))],
        name="moe_dispatch_sc",
    )(h2, dest_w, pad_w, zeros)


def _gather_rows(yb, dest_kt):
    n = dest_kt.shape[0]
    w = SC_ROWS
    per_w = n // SC_WORKERS
    steps = per_w // w
    assert steps % 2 == 0 and steps * w * SC_WORKERS == n
    dest_w = dest_kt.reshape(SC_WORKERS, steps, w)

    def body(y_hbm, dest_hbm, o_hbm, idx_v, rows_v, sem_g, sem_w):
        wid = _sc_worker()
        row0 = wid * per_w
        pltpu.sync_copy(dest_hbm.at[wid], idx_v)

        def gather(step, b):
            return pltpu.make_async_copy(y_hbm.at[idx_v.at[step]], rows_v.at[b], sem_g.at[b])

        def write(step, b):
            return pltpu.make_async_copy(rows_v.at[b], o_hbm.at[pl.ds(row0 + step * w, w)], sem_w.at[b])

        gather(0, 0).start()

        @pl.loop(0, steps, step=2)
        def _(s0):
            for b in range(2):
                step = s0 + b
                gather(step, b).wait()

                @pl.when(step >= 1)
                def _():
                    write(step - 1, 1 - b).wait()

                @pl.when(step + 1 < steps)
                def _():
                    gather(step + 1, 1 - b).start()

                write(step, b).start()

        write(steps - 1, 1).wait()

    return pl.kernel(
        body,
        out_type=jax.ShapeDtypeStruct((n, ROW_TILES, LANES), F32),
        mesh=_sc_mesh(),
        scratch_types=[pltpu.VMEM((steps, w), I32), pltpu.VMEM((2, w, ROW_TILES, LANES), F32),
                       pltpu.SemaphoreType.DMA((2,)), pltpu.SemaphoreType.DMA((2,))],
        name="moe_gather_sc",
    )(yb, dest_w)


def _rows_2d(ref, n):
    return jnp.concatenate([ref[pl.ds(c, n, stride=ROW_TILES), :] for c in range(ROW_TILES)], axis=1)


def _store_rows(ref, val):
    for c in range(ROW_TILES):
        ref[pl.ds(c, val.shape[0], stride=ROW_TILES), :] = val[:, c * LANES:(c + 1) * LANES]


def _expert_kernel(blk_e_ref, first_ref, slot_ref, next_ref, nblk_ref, x_ref, wg_hbm, wl_hbm, bg_ref, bl_ref, wd_hbm,
                   bd_ref, y_ref, wg_buf, wl_buf, wd_buf, sem):
    i = pl.program_id(0)

    def fetch(e, sl):
        return [pltpu.make_async_copy(src.at[e], dst.at[sl], sem.at[sl, j])
                for j, (src, dst) in enumerate(((wg_hbm, wg_buf), (wl_hbm, wl_buf), (wd_hbm, wd_buf)))]

    @pl.when(i < nblk_ref[0])
    def _():
        sl = slot_ref[i]

        @pl.when(first_ref[i] == 1)
        def _():
            @pl.when(i == 0)
            def _():
                for cp in fetch(blk_e_ref[0], sl):
                    cp.start()

            for cp in fetch(blk_e_ref[i], sl):
                cp.wait()

            @pl.when(next_ref[i] >= 0)
            def _():
                for cp in fetch(next_ref[i], 1 - sl):
                    cp.start()

        xb = _rows_2d(x_ref, y_ref.shape[0] // ROW_TILES).astype(BF16)
        glu = jnp.dot(xb, wg_buf[sl], preferred_element_type=F32) + bg_ref[...]
        lin = jnp.dot(xb, wl_buf[sl], preferred_element_type=F32) + bl_ref[...]
        glu = jnp.minimum(glu, SWIGLU_LIMIT)
        lin = jnp.clip(lin, -SWIGLU_LIMIT, SWIGLU_LIMIT)
        act = glu * jax.nn.sigmoid(SWIGLU_ALPHA * glu) * (lin + 1.0)
        _store_rows(y_ref, jnp.dot(act.astype(BF16), wd_buf[sl], preferred_element_type=F32) + bd_ref[...])

    @pl.when(i >= nblk_ref[0])
    def _():
        y_ref[...] = jnp.zeros_like(y_ref)


def _split_kernel(w_ref, perm_ref, g_ref, l_ref):
    half = g_ref.shape[-1]
    sorted_cols = jnp.dot(w_ref[...].astype(BF16), perm_ref[...], preferred_element_type=F32)
    g_ref[...] = sorted_cols[:, :half].astype(BF16)
    l_ref[...] = sorted_cols[:, half:].astype(BF16)


def _split_gate_up(w_gu):
    e, d, n2 = w_gu.shape
    cw = 512
    src = np.concatenate([np.arange(0, cw, 2), np.arange(1, cw, 2)])
    perm = jnp.asarray(np.eye(cw, dtype=np.float32)[:, src], BF16)
    return pl.pallas_call(
        _split_kernel,
        out_shape=[jax.ShapeDtypeStruct((e, d, n2 // 2), BF16), jax.ShapeDtypeStruct((e, d, n2 // 2), BF16)],
        grid=(e, n2 // cw),
        in_specs=[pl.BlockSpec((None, d, cw), lambda i, c: (i, 0, c)), _const_spec((cw, cw))],
        out_specs=[pl.BlockSpec((None, d, cw // 2), lambda i, c: (i, 0, c)),
                   pl.BlockSpec((None, d, cw // 2), lambda i, c: (i, 0, c))],
        compiler_params=_cparams(("parallel", "parallel")),
        name="split_gate_up",
    )(w_gu, perm)


def _experts(xb, blk_meta, w_glu, w_lin, b_glu, b_lin, w_dn, b_dn):
    n_rows = xb.shape[0] // ROW_TILES
    bm = EXPERT_ROWS
    vec = lambda n: pl.BlockSpec((None, 1, n), lambda i, be, *_: (be[i], 0, 0))
    rows = pl.BlockSpec((bm * ROW_TILES, LANES), lambda i, *_: (i, 0))
    hbm = pl.BlockSpec(memory_space=pl.ANY)
    grid_spec = pltpu.PrefetchScalarGridSpec(
        num_scalar_prefetch=5,
        grid=(n_rows // bm,),
        in_specs=[rows, hbm, hbm, vec(D_EXPERT), vec(D_EXPERT), hbm, vec(D_MODEL)],
        out_specs=rows,
        scratch_shapes=[pltpu.VMEM((2, D_MODEL, D_EXPERT), BF16), pltpu.VMEM((2, D_MODEL, D_EXPERT), BF16),
                        pltpu.VMEM((2, D_EXPERT, D_MODEL), BF16), pltpu.SemaphoreType.DMA((2, 3))],
    )
    return pl.pallas_call(
        _expert_kernel,
        out_shape=jax.ShapeDtypeStruct((n_rows * ROW_TILES, LANES), F32),
        grid_spec=grid_spec,
        compiler_params=_cparams(("arbitrary",)),
        name="moe_experts",
    )(*blk_meta, xb, w_glu, w_lin, b_glu, b_lin, w_dn, b_dn)


def _combine_kernel(h_ref, gate_ref, y0_ref, y1_ref, y2_ref, y3_ref, g_ref, b_ref, *rest):
    o_ref = rest[-1]
    tt = o_ref.shape[0]
    gates = gate_ref[...]
    acc = DN_ALPHA * _rows_2d(h_ref, tt)
    for kk, y_ref in enumerate((y0_ref, y1_ref, y2_ref, y3_ref)):
        acc = acc + gates[:, kk:kk + 1] * _rows_2d(y_ref, tt)
    o_ref[...] = _layer_norm_rows(acc, g_ref[...], b_ref[...])


def _combine(h2, gate_slab, yg, ln_g, ln_b, part, prev_out):
    t_all = h2.shape[0] // ROW_TILES
    t = yg.shape[0] // (TOP_K * ROW_TILES)
    tt = min(COMBINE_TOKENS, t)
    steps = t // tt
    first = part * steps
    row = lambda m: pl.BlockSpec((tt, m), lambda i: (first + i, 0))
    tiles = lambda kk: pl.BlockSpec((tt * ROW_TILES, LANES), lambda i: (kk * steps + i, 0))
    in_specs = [pl.BlockSpec((tt * ROW_TILES, LANES), lambda i: (first + i, 0)), row(LANES),
                tiles(0), tiles(1), tiles(2), tiles(3), _const_spec((1, D_MODEL)), _const_spec((1, D_MODEL))]
    args = [h2, gate_slab, yg, yg, yg, yg, ln_g.reshape(1, -1), ln_b.reshape(1, -1)]
    aliases = {}
    if prev_out is not None:
        in_specs.append(pl.BlockSpec(memory_space=pl.ANY))
        args.append(prev_out)
        aliases = {len(args) - 1: 0}
    return pl.pallas_call(
        _combine_kernel,
        out_shape=jax.ShapeDtypeStruct((t_all, D_MODEL), F32),
        grid=(steps,),
        in_specs=in_specs,
        out_specs=row(D_MODEL),
        input_output_aliases=aliases,
        compiler_params=_cparams(("parallel",)),
        name="moe_combine",
    )(*args)


def _moe_part(h2, idx_slab, gate_slab, expert_w, ln_g, ln_b, part, prev_out):
    t = h2.shape[0] // ROW_TILES // MOE_PARTS
    bm = EXPERT_ROWS
    n_assign = t * TOP_K
    n_rows = n_assign + N_EXPERTS * bm
    n_blocks = n_rows // bm

    rank_slab, cnt = _routing_ranks(idx_slab, part, t)
    counts = cnt[0, :N_EXPERTS]
    padded = (counts + bm - 1) // bm * bm
    pends = jnp.cumsum(padded)
    pstarts = pends - padded
    pstart_row = jnp.zeros((8, LANES), I32).at[:, :N_EXPERTS].set(pstarts[None, :])
    blk_start = jnp.arange(n_blocks, dtype=I32) * bm
    blk_e = jnp.minimum(jnp.sum((pends[None, :] <= blk_start[:, None]).astype(I32), axis=1), N_EXPERTS - 1)
    nblk = (pends[-1] // bm).astype(I32).reshape(1)
    active = counts > 0
    ordinal = jnp.cumsum(active.astype(I32)) - 1
    eid = jnp.arange(N_EXPERTS, dtype=I32)
    later = active[None, :] & (eid[None, :] > eid[:, None])
    next_active = jnp.where(later.any(axis=1), jnp.argmax(later, axis=1).astype(I32), -1)
    blk_first = jnp.concatenate([jnp.ones((1,), I32), (blk_e[1:] != blk_e[:-1]).astype(I32)])
    blk_meta = (blk_e, blk_first, ordinal[blk_e] % 2, next_active[blk_e], nblk)
    dest_slab = _dest_rows(idx_slab, rank_slab, pstart_row, part)
    dest_kt = dest_slab[:TOP_K]
    pad_off = jnp.arange(bm, dtype=I32)[None, :]
    pad_rows = jnp.where(pad_off < (padded - counts)[:, None], (pstarts + counts)[:, None] + pad_off,
                         n_rows - 1).reshape(-1)

    xb = _dispatch(h2.reshape(-1, ROW_TILES, LANES), dest_kt, pad_rows, n_rows, part * t)
    yb = _experts(xb.reshape(n_rows * ROW_TILES, LANES), blk_meta, *expert_w)
    yg = _gather_rows(yb.reshape(n_rows, ROW_TILES, LANES), dest_kt.reshape(-1))
    return _combine(h2, gate_slab, yg.reshape(n_assign * ROW_TILES, LANES), ln_g, ln_b, part, prev_out)


def _moe(h2, idx_slab, gate_slab, w_gu, b_gu, w_dn, b_dn, ln_g, ln_b):
    w_glu, w_lin = _split_gate_up(w_gu)
    expert_w = (w_glu, w_lin, b_gu[:, None, 0::2], b_gu[:, None, 1::2], w_dn.astype(BF16), b_dn[:, None, :])
    out = None
    for part in range(MOE_PARTS):
        out = _moe_part(h2, idx_slab, gate_slab, expert_w, ln_g, ln_b, part, out)
    return out


def kernel(x, mem, positions, w_in, b_igate, b_fgate, conv_w, conv_b, w_mq, w_mk, g_mhead, w_mskip, g_qlat, g_kvlat,
           w_uq, w_ukv, w_br_m, w_br_a, w_mix_out, ln1_g, ln1_b, w_cq, w_ckv, w_co, ln2_g, ln2_b, w_router, b_router,
           w_gu, b_gu, w_dn, b_dn, ln3_g, ln3_b):
    bsz, s, d = x.shape
    t = bsz * s
    h = x.reshape(t, d)
    pos2 = positions.reshape(t, 1)
    for l in range(DEPTH):
        q, k, v, xm, vm, op, gm, ga, gate = _input_projection(h, pos2, w_in[l], g_qlat[l], g_kvlat[l], w_uq[l],
                                                              w_ukv[l])
        ym = _mlstm_branch(xm, vm, op, gate, bsz, s, b_igate[l], b_fgate[l], conv_w[l], conv_b[l], w_mq[l], w_mk[l],
                           g_mhead[l], w_mskip[l])
        oa = _mla_attention(q, k, v, bsz, s)
        h1 = _merge(ym, oa, gm, ga, h, w_br_m[l], w_br_a[l], w_mix_out[l], ln1_g[l], ln1_b[l])
        kv = _mem_kv(mem.reshape(-1, d), w_ckv[l])
        h2, idx_slab, gate_slab = _cross_attention_router(h1, kv, bsz, s, w_cq[l], w_co[l], ln2_g[l], ln2_b[l],
                                                          w_router[l], b_router[l])
        h = _moe(h2, idx_slab, gate_slab, w_gu[l], b_gu[l], w_dn[l], b_dn[l], ln3_g[l], ln3_b[l])
    return h.reshape(bsz, s, d)
```

```python
import functools

import numpy as np
import jax
import jax.numpy as jnp
from jax import lax
from jax.experimental import pallas as pl
from jax.experimental.pallas import tpu as pltpu
from jax.experimental.pallas import tpu_sc as plsc

F32 = jnp.float32
BF16 = jnp.bfloat16
I32 = jnp.int32

D_MODEL = 1024
N_MEM = 256
M_HEADS = 4
M_HEAD_DIM = 128
M_WIDTH = M_HEADS * M_HEAD_DIM
M_CONV = 4
A_HEADS = 8
A_NOPE = 64
A_ROPE = 32
A_QK = A_NOPE + A_ROPE
A_VDIM = 64
A_Q_RANK = 256
A_KV_RANK = 128
ROPE_THETA = 10000.0
X_HEADS = 4
X_HEAD_DIM = D_MODEL // X_HEADS
N_EXPERTS = 32
TOP_K = 4
D_EXPERT = D_MODEL
SWIGLU_ALPHA = 1.702
SWIGLU_LIMIT = 7.0
DEPTH = 1
DN_ALPHA = (2.0 * DEPTH) ** 0.25
EPS = 1e-5
IN_SPLITS = (A_Q_RANK, A_KV_RANK, A_ROPE, M_WIDTH, M_WIDTH, M_WIDTH, M_HEADS, M_HEADS, D_MODEL, D_MODEL)
IN_OFFSETS = tuple(int(v) for v in np.cumsum((0,) + IN_SPLITS))

LANES = 128
VMEM_LIMIT = 56 * 1024 * 1024

PROJ_ROWS = 512
M_CHUNK_ROWS = 128
MLSTM_SEQS = 2
ATT_Q = 256
MERGE_ROWS = 512
XATT_ROWS = 512
RANK_ROWS = 512
EXPERT_ROWS = 256
COMBINE_TOKENS = 256
MOE_PARTS = 2
SC_CORES = 2
SC_SUBCORES = 16
SC_WORKERS = SC_CORES * SC_SUBCORES
SC_ROWS = 32
NEG_BIG = -1e30
LOG2_E = 1.4426950408889634
ROW_TILES = D_MODEL // LANES

C_QLAT = 0
C_KVLAT = C_QLAT + A_Q_RANK
C_KR = C_KVLAT + A_KV_RANK
C_KRS = C_KR + LANES
C_GATE = C_KRS + LANES
C_XM = C_GATE + LANES
C_VM = C_XM + M_WIDTH
C_OP = C_VM + M_WIDTH
C_GM = C_OP + M_WIDTH
C_GA = C_GM + D_MODEL
C_END = C_GA + D_MODEL


def _cparams(sem, vmem=VMEM_LIMIT):
    return pltpu.CompilerParams(dimension_semantics=sem, vmem_limit_bytes=vmem)


def _const_spec(shape):
    nd = len(shape)
    return pl.BlockSpec(shape, lambda *a: (0,) * nd)


def _layer_norm_rows(v, g, b):
    mu = jnp.mean(v, axis=-1, keepdims=True)
    d = v - mu
    var = jnp.mean(d * d, axis=-1, keepdims=True)
    return d * lax.rsqrt(var + EPS) * g + b


def _proj_kernel(x_ref, pos_ref, w_ref, gq_ref, gkv_ref, wuq_ref, wuqs_ref, wuk_ref, wuv_ref, invf_ref,
                 q_ref, k_ref, v_ref, xm_ref, vm_ref, op_ref, gm_ref, ga_ref, gate_ref):
    xb = x_ref[...].astype(BF16)

    def mm(lo, n):
        return jnp.dot(xb, w_ref[:, lo:lo + n], preferred_element_type=F32)

    xm_ref[...] = mm(C_XM, M_WIDTH).astype(BF16)
    vm_ref[...] = mm(C_VM, M_WIDTH).astype(BF16)
    op_ref[...] = mm(C_OP, M_WIDTH).astype(BF16)
    gm_ref[...] = mm(C_GM, D_MODEL).astype(BF16)
    ga_ref[...] = mm(C_GA, D_MODEL).astype(BF16)
    gate_ref[...] = mm(C_GATE, LANES)

    ang = pos_ref[...].astype(F32) * invf_ref[...]
    cos = jnp.cos(ang)
    sin = jnp.sin(ang)

    q_lat = mm(C_QLAT, A_Q_RANK)
    qn = (q_lat * lax.rsqrt(jnp.mean(q_lat * q_lat, axis=-1, keepdims=True) + EPS) * gq_ref[...]).astype(BF16)
    q = jnp.dot(qn, wuq_ref[...], preferred_element_type=F32)
    qs = jnp.dot(qn, wuqs_ref[...], preferred_element_type=F32)
    kv_lat = mm(C_KVLAT, A_KV_RANK)
    kvn = (kv_lat * lax.rsqrt(jnp.mean(kv_lat * kv_lat, axis=-1, keepdims=True) + EPS) * gkv_ref[...]).astype(BF16)
    kn = jnp.dot(kvn, wuk_ref[...], preferred_element_type=F32)
    lane = lax.broadcasted_iota(I32, (1, A_HEADS * LANES), 1)
    ones_lane = (lane % LANES == A_VDIM).astype(F32)
    v_ref[...] = (jnp.dot(kvn, wuv_ref[...], preferred_element_type=F32) + ones_lane).astype(BF16)
    k_pe = mm(C_KR, LANES) * cos + mm(C_KRS, LANES) * sin
    scale = A_QK ** -0.5 * LOG2_E
    for h in range(A_HEADS):
        sl = slice(h * LANES, (h + 1) * LANES)
        q_ref[:, sl] = ((q[:, sl] * cos + qs[:, sl] * sin) * scale).astype(BF16)
        k_ref[:, sl] = (kn[:, sl] + k_pe).astype(BF16)


def _proj_weights(w_in, w_uq, w_ukv):
    o = IN_OFFSETS
    half = A_ROPE // 2
    w_q, w_kv, w_kr = w_in[:, o[0]:o[1]], w_in[:, o[1]:o[2]], w_in[:, o[2]:o[3]]
    w_xm, w_vm, w_op = w_in[:, o[3]:o[4]], w_in[:, o[4]:o[5]], w_in[:, o[5]:o[6]]
    w_i, w_f, w_gm, w_ga = w_in[:, o[6]:o[7]], w_in[:, o[7]:o[8]], w_in[:, o[8]:o[9]], w_in[:, o[9]:o[10]]
    d = w_in.shape[0]
    z = lambda n: jnp.zeros((d, n), w_in.dtype)
    kr = jnp.concatenate([z(A_NOPE), w_kr, z(LANES - A_QK)], axis=1)
    krs = jnp.concatenate([z(A_NOPE), -w_kr[:, half:], w_kr[:, :half], z(LANES - A_QK)], axis=1)
    gate = jnp.concatenate([w_i, w_f, z(LANES - 2 * M_HEADS)], axis=1)
    w_r = jnp.concatenate([w_q, w_kv, kr, krs, gate, w_xm, w_vm, w_op, w_gm, w_ga], axis=1).astype(BF16)

    uq = w_uq.reshape(A_Q_RANK, A_HEADS, A_QK)
    zq = jnp.zeros((A_Q_RANK, A_HEADS, LANES - A_QK), w_uq.dtype)
    zn = jnp.zeros((A_Q_RANK, A_HEADS, A_NOPE), w_uq.dtype)
    uq_pad = jnp.concatenate([uq, zq], axis=-1).reshape(A_Q_RANK, A_HEADS * LANES).astype(BF16)
    uqs_pad = jnp.concatenate([zn, -uq[..., A_NOPE + half:], uq[..., A_NOPE:A_NOPE + half], zq],
                              axis=-1).reshape(A_Q_RANK, A_HEADS * LANES).astype(BF16)
    ukv = w_ukv.reshape(A_KV_RANK, A_HEADS, A_NOPE + A_VDIM)
    zk = jnp.zeros((A_KV_RANK, A_HEADS, LANES - A_NOPE), w_ukv.dtype)
    uk_pad = jnp.concatenate([ukv[..., :A_NOPE], zk], axis=-1).reshape(A_KV_RANK, A_HEADS * LANES).astype(BF16)
    zv = jnp.zeros((A_KV_RANK, A_HEADS, LANES - A_VDIM), w_ukv.dtype)
    uv_pad = jnp.concatenate([ukv[..., A_NOPE:], zv], axis=-1).reshape(A_KV_RANK, A_HEADS * LANES).astype(BF16)
    return w_r, uq_pad, uqs_pad, uk_pad, uv_pad


def _input_projection(x2, pos2, w_in, g_qlat, g_kvlat, w_uq, w_ukv):
    t = x2.shape[0]
    tm = min(PROJ_ROWS, t)
    w_r, uq_pad, uqs_pad, uk_pad, uv_pad = _proj_weights(w_in, w_uq, w_ukv)
    half = A_ROPE // 2
    inv_freq = ROPE_THETA ** (-jnp.arange(half, dtype=F32) / half)
    invf = jnp.concatenate([jnp.zeros((A_NOPE,), F32), inv_freq, inv_freq,
                            jnp.zeros((LANES - A_QK,), F32)]).reshape(1, LANES)
    hw = A_HEADS * LANES
    row = lambda n: pl.BlockSpec((tm, n), lambda i: (i, 0))
    outs = [
        jax.ShapeDtypeStruct((t, hw), BF16), jax.ShapeDtypeStruct((t, hw), BF16), jax.ShapeDtypeStruct((t, hw), BF16),
        jax.ShapeDtypeStruct((t, M_WIDTH), BF16), jax.ShapeDtypeStruct((t, M_WIDTH), BF16),
        jax.ShapeDtypeStruct((t, M_WIDTH), BF16),
        jax.ShapeDtypeStruct((t, D_MODEL), BF16), jax.ShapeDtypeStruct((t, D_MODEL), BF16),
        jax.ShapeDtypeStruct((t, LANES), F32),
    ]
    return pl.pallas_call(
        _proj_kernel,
        out_shape=outs,
        grid=(t // tm,),
        in_specs=[row(D_MODEL), row(1), _const_spec(w_r.shape), _const_spec((1, A_Q_RANK)),
                  _const_spec((1, A_KV_RANK)), _const_spec(uq_pad.shape), _const_spec(uqs_pad.shape),
                  _const_spec(uk_pad.shape), _const_spec(uv_pad.shape), _const_spec((1, LANES))],
        out_specs=[row(hw), row(hw), row(hw), row(M_WIDTH), row(M_WIDTH), row(M_WIDTH), row(D_MODEL), row(D_MODEL),
                   row(LANES)],
        compiler_params=_cparams(("parallel",)),
        name="input_projection",
    )(x2, pos2, w_r, g_qlat.reshape(1, -1), g_kvlat.reshape(1, -1), uq_pad, uqs_pad, uk_pad, uv_pad, invf)


def _log_sigmoid(v):
    return jnp.minimum(v, 0.0) - jnp.log1p(jnp.exp(-jnp.abs(v)))


def _mlstm_kernel(xm_ref, vm_ref, op_ref, gate_ref, convw_ref, convb_ref, wqk_ref, gbias_ref, ghead_ref,
                  skip_ref, tril_ref, out_ref, xpad_s, ctn_s, m_s, *, seq_len):
    s = seq_len
    lc = M_CHUNK_ROWS
    halo = 8
    for bb in range(MLSTM_SEQS):
        xpad_s[bb, 0:halo, :] = jnp.zeros((halo, M_WIDTH), F32)
        xpad_s[bb, halo:, :] = xm_ref[bb * s:(bb + 1) * s, :].astype(F32)
    ctn_s[...] = jnp.zeros_like(ctn_s)
    m_s[...] = jnp.zeros_like(m_s)
    rows = lax.broadcasted_iota(I32, (lc, lc), 0)
    cols = lax.broadcasted_iota(I32, (lc, lc), 1)
    causal = rows >= cols
    row_id = lax.broadcasted_iota(I32, (lc, LANES), 0)
    ones_blk = (lax.broadcasted_iota(I32, (lc, LANES), 1) == 0).astype(BF16)
    kscale = M_HEAD_DIM ** -0.5
    hsl = [slice(h * M_HEAD_DIM, (h + 1) * M_HEAD_DIM) for h in range(M_HEADS)]
    hcl = [slice(h, h + 1) for h in range(M_HEADS)]
    streams = [(bb, h) for bb in range(MLSTM_SEQS) for h in range(M_HEADS)]

    def gates(bb, r0):
        win = xpad_s[bb, pl.ds(r0 - bb * s, lc + halo), :]
        conv = convb_ref[...]
        for j in range(M_CONV):
            lo = halo - (M_CONV - 1) + j
            conv = conv + win[lo:lo + lc, :] * convw_ref[j:j + 1, :]
        xc = conv * jax.nn.sigmoid(conv)
        g = gate_ref[pl.ds(r0, lc), :] + gbias_ref[...]
        ls = _log_sigmoid(g)
        bc = jnp.dot(tril_ref[...], ls, preferred_element_type=F32, precision=lax.Precision.HIGHEST)
        b0 = pltpu.roll(bc, LANES - M_HEADS, axis=1)
        u = g - b0
        cm = u
        shift = 1
        while shift < lc:
            cm = jnp.maximum(cm, jnp.where(row_id >= shift, pltpu.roll(cm, shift, axis=0), -jnp.inf))
            shift *= 2
        m_prev = m_s[bb:bb + 1, :]
        g_inter = b0 + m_prev
        m = jnp.maximum(g_inter, b0 + cm)
        b_tot = b0[lc - 1:lc, :]
        aw = b_tot - b0 + g
        m_chunk = jnp.max(aw, axis=0, keepdims=True)
        m_new = jnp.maximum(b_tot + m_prev, m_chunk)
        m_s[bb:bb + 1, :] = m_new
        return dict(xc=xc, w_inter=jnp.exp(g_inter - m), e_negm=jnp.exp(-m), a_mat=b0 - m, wa=jnp.exp(aw - m_chunk),
                    s_old=jnp.exp(b_tot + m_prev - m_new), s_new=jnp.exp(m_chunk - m_new), u_t=u.T)

    def chunk(c, carry):
        r0s = [pl.multiple_of(bb * s + c * lc, lc) for bb in range(MLSTM_SEQS)]
        gs = [gates(bb, r0s[bb]) for bb in range(MLSTM_SEQS)]
        qk = [jnp.dot(gs[bb]["xc"].astype(BF16), wqk_ref[...], preferred_element_type=F32)
              for bb in range(MLSTM_SEQS)]
        q_b = [qk[bb][:, hsl[h]].astype(BF16) for bb, h in streams]
        k_f = [qk[bb][:, M_WIDTH + h * M_HEAD_DIM:M_WIDTH + (h + 1) * M_HEAD_DIM] * kscale for bb, h in streams]
        v_aug = [jnp.concatenate([vm_ref[pl.ds(r0s[bb], lc), hsl[h]], ones_blk], axis=1) for bb, h in streams]
        ctn_prev = [ctn_s[bb * M_HEADS + h] for bb, h in streams]
        s_raw = [lax.dot_general(q_b[i], k_f[i].astype(BF16), (((1,), (1,)), ((), ())), preferred_element_type=F32)
                 for i in range(len(streams))]
        inter = [jnp.dot(q_b[i], ctn_prev[i].astype(BF16), preferred_element_type=F32)
                 for i in range(len(streams))]
        ctn_c = [lax.dot_general((k_f[i] * gs[bb]["wa"][:, hcl[h]]).astype(BF16), v_aug[i], (((0,), (0,)), ((), ())),
                                 preferred_element_type=F32) for i, (bb, h) in enumerate(streams)]
        sc_b = [(s_raw[i] * jnp.exp(jnp.where(causal, gs[bb]["a_mat"][:, hcl[h]] + gs[bb]["u_t"][hcl[h], :],
                                              -jnp.inf))).astype(BF16) for i, (bb, h) in enumerate(streams)]
        intra = [jnp.dot(sc_b[i], v_aug[i], preferred_element_type=F32) for i in range(len(streams))]
        for i, (bb, h) in enumerate(streams):
            g = gs[bb]
            ctn_s[bb * M_HEADS + h] = g["s_old"][:, hcl[h]] * ctn_prev[i] + g["s_new"][:, hcl[h]] * ctn_c[i]
            wi = g["w_inter"][:, hcl[h]]
            num = wi * inter[i][:, :M_HEAD_DIM] + intra[i][:, :M_HEAD_DIM]
            den = wi * inter[i][:, M_HEAD_DIM:M_HEAD_DIM + 1] + intra[i][:, M_HEAD_DIM:M_HEAD_DIM + 1]
            hh = num / jnp.maximum(jnp.abs(den), g["e_negm"][:, hcl[h]])
            mu = jnp.mean(hh, axis=-1, keepdims=True)
            dv = hh - mu
            var = jnp.mean(dv * dv, axis=-1, keepdims=True)
            hn = dv * lax.rsqrt(var + EPS) * ghead_ref[:, hsl[h]]
            y = (jax.nn.sigmoid(op_ref[pl.ds(r0s[bb], lc), hsl[h]].astype(F32))
                 * (hn + skip_ref[:, hsl[h]] * g["xc"][:, hsl[h]]))
            out_ref[pl.ds(r0s[bb], lc), hsl[h]] = y.astype(BF16)
        return carry

    lax.fori_loop(0, s // lc, chunk, 0)


def _mlstm_branch(xm, vm, op, gate, bsz, s, b_i, b_f, conv_w, conv_b, w_mq, w_mk, g_mhead, w_mskip):
    lc = M_CHUNK_ROWS
    gbias = jnp.concatenate([b_i, b_f, jnp.zeros((LANES - 2 * M_HEADS,), F32)]).reshape(1, LANES)
    tril = jnp.tril(jnp.ones((lc, lc), F32))
    eye = jnp.eye(M_HEADS, dtype=F32)
    block_diag = lambda w: jnp.einsum('hde,hg->hdge', w, eye).reshape(M_WIDTH, M_WIDTH)
    w_qk = jnp.concatenate([block_diag(w_mq), block_diag(w_mk)], axis=1).astype(BF16)
    nseq = MLSTM_SEQS
    assert bsz % nseq == 0
    seq = lambda n: pl.BlockSpec((nseq * s, n), lambda b: (b, 0))
    return pl.pallas_call(
        functools.partial(_mlstm_kernel, seq_len=s),
        out_shape=jax.ShapeDtypeStruct((bsz * s, M_WIDTH), BF16),
        grid=(bsz // nseq,),
        in_specs=[seq(M_WIDTH), seq(M_WIDTH), seq(M_WIDTH), seq(LANES),
                  _const_spec((M_CONV, M_WIDTH)), _const_spec((1, M_WIDTH)),
                  _const_spec((M_WIDTH, 2 * M_WIDTH)),
                  _const_spec((1, LANES)), _const_spec((1, M_WIDTH)), _const_spec((1, M_WIDTH)),
                  _const_spec((lc, lc))],
        out_specs=seq(M_WIDTH),
        scratch_shapes=[pltpu.VMEM((nseq, s + 8, M_WIDTH), F32),
                        pltpu.VMEM((nseq * M_HEADS, M_HEAD_DIM, 2 * LANES), F32), pltpu.VMEM((8, LANES), F32)],
        compiler_params=_cparams(("parallel",)),
        name="mlstm_branch",
    )(xm, vm, op, gate, conv_w, conv_b.reshape(1, -1), w_qk, gbias, g_mhead.reshape(1, -1), w_mskip.reshape(1, -1),
      tril)


def _attn_kernel(q_ref, k_ref, v_ref, o_ref, s_scr, m_s, acc_s):
    tq = q_ref.shape[0]
    i = pl.program_id(1)
    rows = lax.broadcasted_iota(I32, (tq, tq), 0)
    cols = lax.broadcasted_iota(I32, (tq, tq), 1)

    def lane_tile_max(sc):
        out = sc[:, :LANES]
        for t in range(1, tq // LANES):
            out = jnp.maximum(out, sc[:, t * LANES:(t + 1) * LANES])
        return out

    heads = [slice(h * LANES, (h + 1) * LANES) for h in range(A_HEADS)]
    m_s[...] = jnp.full(m_s.shape, NEG_BIG, F32)
    acc_s[...] = jnp.zeros_like(acc_s)

    def pass_a(j, diagonal):
        k0 = pl.multiple_of(j * tq, tq)
        for h, hs in enumerate(heads):
            sc = lax.dot_general(q_ref[:, hs], k_ref[pl.ds(k0, tq), hs], (((1,), (1,)), ((), ())),
                                 preferred_element_type=F32)
            if diagonal:
                sc = jnp.where(cols <= rows, sc, NEG_BIG)
            s_scr[h, j] = sc
            m_s[h] = jnp.maximum(m_s[h], lane_tile_max(sc))

    def for_blocks(n, fn):
        def pair(p, c):
            fn(2 * p)
            fn(2 * p + 1)
            return c

        lax.fori_loop(0, n // 2, pair, 0)

        @pl.when(n % 2 == 1)
        def _():
            fn(n - 1)

    for_blocks(i, lambda j: pass_a(j, False))
    pass_a(i, True)
    for h in range(A_HEADS):
        m_s[h] = jnp.broadcast_to(jnp.max(m_s[h], axis=1, keepdims=True), (tq, LANES))

    def pass_b(j):
        k0 = pl.multiple_of(j * tq, tq)
        for h, hs in enumerate(heads):
            m_row = m_s[h]
            p = jnp.exp2(s_scr[h, j] - jnp.concatenate([m_row] * (tq // LANES), axis=1)).astype(BF16)
            acc_s[h] += jnp.dot(p, v_ref[pl.ds(k0, tq), hs], preferred_element_type=F32)

    for_blocks(i + 1, pass_b)
    for h, hs in enumerate(heads):
        acc = acc_s[h]
        o_ref[:, hs] = (acc / acc[:, A_VDIM:A_VDIM + 1]).astype(BF16)


def _mla_attention(q, k, v, bsz, s):
    tq = min(ATT_Q, s)
    nq = s // tq
    hw = A_HEADS * LANES
    return pl.pallas_call(
        _attn_kernel,
        out_shape=jax.ShapeDtypeStruct(q.shape, BF16),
        grid=(bsz, nq),
        in_specs=[pl.BlockSpec((tq, hw), lambda b, i: (b * nq + i, 0)),
                  pl.BlockSpec((s, hw), lambda b, i: (b, 0)),
                  pl.BlockSpec((s, hw), lambda b, i: (b, 0))],
        out_specs=pl.BlockSpec((tq, hw), lambda b, i: (b * nq + i, 0)),
        scratch_shapes=[pltpu.VMEM((A_HEADS, nq, tq, tq), F32), pltpu.VMEM((A_HEADS, tq, LANES), F32),
                        pltpu.VMEM((A_HEADS, tq, LANES), F32)],
        compiler_params=_cparams(("parallel", "arbitrary")),
        name="mla_attention",
    )(q, k, v)


def _merge_kernel(ym_ref, oa_ref, gm_ref, ga_ref, x_ref, wbm_ref, wba_ref, wmix_ref, g_ref, b_ref, h_ref):
    y_m = jnp.dot(ym_ref[...], wbm_ref[...], preferred_element_type=F32)
    y_a = jnp.dot(oa_ref[...], wba_ref[...], preferred_element_type=F32)
    mixed = jax.nn.sigmoid(gm_ref[...].astype(F32)) * y_m + jax.nn.sigmoid(ga_ref[...].astype(F32)) * y_a
    mix = jnp.dot(mixed.astype(BF16), wmix_ref[...], preferred_element_type=F32)
    h_ref[...] = _layer_norm_rows(DN_ALPHA * x_ref[...] + mix, g_ref[...], b_ref[...])


def _merge(ym, oa, gm, ga, x2, w_br_m, w_br_a, w_mix_out, ln_g, ln_b):
    t = x2.shape[0]
    tm = min(MERGE_ROWS, t)
    wba = jnp.concatenate([w_br_a.reshape(A_HEADS, A_VDIM, D_MODEL),
                           jnp.zeros((A_HEADS, LANES - A_VDIM, D_MODEL), w_br_a.dtype)], axis=1)
    wba = wba.reshape(A_HEADS * LANES, D_MODEL).astype(BF16)
    row = lambda n: pl.BlockSpec((tm, n), lambda i: (i, 0))
    return pl.pallas_call(
        _merge_kernel,
        out_shape=jax.ShapeDtypeStruct((t, D_MODEL), F32),
        grid=(t // tm,),
        in_specs=[row(M_WIDTH), row(A_HEADS * LANES), row(D_MODEL), row(D_MODEL), row(D_MODEL),
                  _const_spec((M_WIDTH, D_MODEL)), _const_spec((A_HEADS * LANES, D_MODEL)),
                  _const_spec((D_MODEL, D_MODEL)), _const_spec((1, D_MODEL)), _const_spec((1, D_MODEL))],
        out_specs=row(D_MODEL),
        compiler_params=_cparams(("parallel",)),
        name="merge_deepnorm1",
    )(ym, oa, gm, ga, x2, w_br_m.astype(BF16), wba, w_mix_out.astype(BF16), ln_g.reshape(1, -1), ln_b.reshape(1, -1))


def _kv_kernel(m_ref, w_ref, o_ref):
    o_ref[...] = jnp.dot(m_ref[...].astype(BF16), w_ref[...], preferred_element_type=F32).astype(BF16)


def _mem_kv(mem2, w_ckv):
    r = mem2.shape[0]
    tm = min(512, r)
    return pl.pallas_call(
        _kv_kernel,
        out_shape=jax.ShapeDtypeStruct((r, 2 * D_MODEL), BF16),
        grid=(r // tm,),
        in_specs=[pl.BlockSpec((tm, D_MODEL), lambda i: (i, 0)), _const_spec((D_MODEL, 2 * D_MODEL))],
        out_specs=pl.BlockSpec((tm, 2 * D_MODEL), lambda i: (i, 0)),
        compiler_params=_cparams(("parallel",)),
        name="memory_kv",
    )(mem2, w_ckv.astype(BF16))


def _xattn_kernel(h_ref, kv_ref, wq_ref, wo_ref, g_ref, b_ref, wr_ref, br_ref, h2_ref, idx_ref, gate_ref):
    h1 = h_ref[...]
    q = jnp.dot(h1.astype(BF16), wq_ref[...], preferred_element_type=F32).astype(BF16)
    scale = X_HEAD_DIM ** -0.5
    ksl = [slice(hd * X_HEAD_DIM, (hd + 1) * X_HEAD_DIM) for hd in range(X_HEADS)]
    vsl = [slice(D_MODEL + hd * X_HEAD_DIM, D_MODEL + (hd + 1) * X_HEAD_DIM) for hd in range(X_HEADS)]
    scs = [lax.dot_general(q[:, ks], kv_ref[:, ks], (((1,), (1,)), ((), ())), preferred_element_type=F32) * scale
           for ks in ksl]
    ps = [jnp.exp(sc - jnp.max(sc, axis=1, keepdims=True)) for sc in scs]
    ps = [(p / jnp.sum(p, axis=1, keepdims=True)).astype(BF16) for p in ps]
    o = jnp.concatenate([jnp.dot(p, kv_ref[:, vs], preferred_element_type=F32).astype(BF16)
                         for p, vs in zip(ps, vsl)], axis=1)
    att = jnp.dot(o, wo_ref[...], preferred_element_type=F32)
    h2 = _layer_norm_rows(DN_ALPHA * h1 + att, g_ref[...], b_ref[...])
    for c in range(ROW_TILES):
        h2_ref[pl.ds(c, h2.shape[0], stride=ROW_TILES), :] = h2[:, c * LANES:(c + 1) * LANES]

    h2_hi = h2.astype(BF16)
    h2_lo = (h2 - h2_hi.astype(F32)).astype(BF16)
    hi_prod = jnp.dot(h2_hi, wr_ref[...], preferred_element_type=F32)
    lo_prod = jnp.dot(h2_lo, wr_ref[:, :LANES], preferred_element_type=F32)
    logits = hi_prod[:, :LANES] + (hi_prod[:, LANES:] + lo_prod) + br_ref[...]
    tm = logits.shape[0]
    lane = lax.broadcasted_iota(I32, (tm, LANES), 1)
    lane_f = lane.astype(F32)
    work = jnp.where(lane < N_EXPERTS, logits, -jnp.inf)
    vals, idxs = [], []
    for _ in range(TOP_K):
        mx = jnp.max(work, axis=1, keepdims=True)
        ix = jnp.min(jnp.where(work == mx, lane_f, float(LANES)), axis=1, keepdims=True)
        vals.append(mx)
        idxs.append(ix)
        work = jnp.where(lane_f == ix, -jnp.inf, work)
    es = [jnp.exp(vv - vals[0]) for vv in vals]
    tot = es[0] + es[1] + es[2] + es[3]
    idx_slab = jnp.zeros((tm, LANES), F32)
    gate_slab = jnp.zeros((tm, LANES), F32)
    for kk in range(TOP_K):
        idx_slab = jnp.where(lane == kk, idxs[kk], idx_slab)
        gate_slab = jnp.where(lane == kk, es[kk] / tot, gate_slab)
    idx_ref[...] = idx_slab.astype(I32)
    gate_ref[...] = gate_slab


def _cross_attention_router(h1, kv, bsz, s, w_cq, w_co, ln_g, ln_b, w_router, b_router):
    t = h1.shape[0]
    tm = min(XATT_ROWS, s)
    ns = s // tm
    n_mem = kv.shape[0] // bsz
    wr = jnp.concatenate([w_router, jnp.zeros((D_MODEL, LANES - N_EXPERTS), F32)], axis=1)
    wr_hi = wr.astype(BF16)
    wr = jnp.concatenate([wr_hi, (wr - wr_hi.astype(F32)).astype(BF16)], axis=1)
    br = jnp.concatenate([b_router, jnp.zeros((LANES - N_EXPERTS,), F32)]).reshape(1, LANES)
    row = lambda n: pl.BlockSpec((tm, n), lambda b, i: (b * ns + i, 0))
    return pl.pallas_call(
        _xattn_kernel,
        out_shape=[jax.ShapeDtypeStruct((t * ROW_TILES, LANES), F32), jax.ShapeDtypeStruct((t, LANES), I32),
                   jax.ShapeDtypeStruct((t, LANES), F32)],
        grid=(bsz, ns),
        in_specs=[row(D_MODEL), pl.BlockSpec((n_mem, 2 * D_MODEL), lambda b, i: (b, 0)),
                  _const_spec((D_MODEL, D_MODEL)), _const_spec((D_MODEL, D_MODEL)),
                  _const_spec((1, D_MODEL)), _const_spec((1, D_MODEL)),
                  _const_spec((D_MODEL, 2 * LANES)), _const_spec((1, LANES))],
        out_specs=[pl.BlockSpec((tm * ROW_TILES, LANES), lambda b, i: (b * ns + i, 0)), row(LANES), row(LANES)],
        compiler_params=_cparams(("parallel", "parallel")),
        name="cross_attention_router",
    )(h1, kv, w_cq.astype(BF16), w_co.astype(BF16), ln_g.reshape(1, -1), ln_b.reshape(1, -1), wr, br)


def _rank_kernel(idx_ref, ltri_ref, rank_ref, cnt_ref, carry_s):
    @pl.when(pl.program_id(0) == 0)
    def _():
        carry_s[...] = jnp.zeros_like(carry_s)

    idx = idx_ref[...]
    tm = idx.shape[0]
    lane = lax.broadcasted_iota(I32, (tm, LANES), 1)
    onehots = [(lane == idx[:, kk:kk + 1]).astype(F32) for kk in range(TOP_K)]
    sel = onehots[0] + onehots[1] + onehots[2] + onehots[3]
    before = jnp.dot(ltri_ref[...], sel.astype(BF16), preferred_element_type=F32) + carry_s[0:1, :]
    rank_slab = jnp.zeros((tm, LANES), F32)
    for kk in range(TOP_K):
        r = jnp.sum(onehots[kk] * before, axis=1, keepdims=True)
        rank_slab = jnp.where(lane == kk, r, rank_slab)
    rank_ref[...] = rank_slab.astype(I32)
    carry_s[0:1, :] = carry_s[0:1, :] + jnp.sum(sel, axis=0, keepdims=True)
    cnt_ref[...] = jnp.broadcast_to(carry_s[0:1, :], cnt_ref.shape).astype(I32)


def _routing_ranks(idx_slab, part, t):
    tm = min(RANK_ROWS, t)
    first = part * (t // tm)
    ltri = jnp.tril(jnp.ones((tm, tm), BF16), k=-1)
    return pl.pallas_call(
        _rank_kernel,
        out_shape=[jax.ShapeDtypeStruct((t, LANES), I32), jax.ShapeDtypeStruct((8, LANES), I32)],
        grid=(t // tm,),
        in_specs=[pl.BlockSpec((tm, LANES), lambda i: (first + i, 0)), _const_spec((tm, tm))],
        out_specs=[pl.BlockSpec((tm, LANES), lambda i: (i, 0)), _const_spec((8, LANES))],
        scratch_shapes=[pltpu.VMEM((8, LANES), F32)],
        compiler_params=_cparams(("arbitrary",)),
        name="routing_ranks",
    )(idx_slab, ltri)


def _dest_kernel(idx_ref, rank_ref, pstart_ref, dest_ref):
    idx = idx_ref[...]
    tm = idx.shape[0]
    lane = lax.broadcasted_iota(I32, (tm, LANES), 1)
    dest = rank_ref[...].astype(F32)
    pstart = pstart_ref[0:1, :].astype(F32)
    for kk in range(TOP_K):
        start = jnp.sum(jnp.where(lane == idx[:, kk:kk + 1], pstart, 0.0), axis=1, keepdims=True)
        dest = dest + jnp.where(lane == kk, start, 0.0)
    dest_ref[...] = dest.T[0:8, :].astype(I32)


def _dest_rows(idx_slab, rank_slab, pstart_row, part):
    t = rank_slab.shape[0]
    tm = min(RANK_ROWS, t)
    first = part * (t // tm)
    return pl.pallas_call(
        _dest_kernel,
        out_shape=jax.ShapeDtypeStruct((8, t), I32),
        grid=(t // tm,),
        in_specs=[pl.BlockSpec((tm, LANES), lambda i: (first + i, 0)), pl.BlockSpec((tm, LANES), lambda i: (i, 0)),
                  _const_spec((8, LANES))],
        out_specs=pl.BlockSpec((8, tm), lambda i: (0, i)),
        compiler_params=_cparams(("parallel",)),
        name="routing_dest",
    )(idx_slab, rank_slab, pstart_row)


def _sc_mesh():
    return plsc.VectorSubcoreMesh(core_axis_name="c", subcore_axis_name="s", num_cores=SC_CORES,
                                  num_subcores=SC_SUBCORES)


def _sc_worker():
    return lax.axis_index("s") * SC_CORES + lax.axis_index("c")


def _dispatch(h2, dest_kt, pad_rows, n_rows, tok_base):
    t = dest_kt.shape[1]
    w = SC_ROWS
    per_w = t // SC_WORKERS
    steps = per_w // w
    assert steps % 2 == 0 and steps * w * SC_WORKERS == t
    pad_steps = pad_rows.shape[0] // (SC_WORKERS * w)
    dest_w = dest_kt.reshape(TOP_K, SC_WORKERS, steps, w)
    pad_w = pad_rows.reshape(SC_WORKERS, pad_steps, w)
    zeros = jnp.zeros((w, ROW_TILES, LANES), F32)

    def body(x_hbm, dest_hbm, pad_hbm, zeros_hbm, o_hbm, idx_v, pad_v, rows_v, sem_in, sem_out):
        wid = _sc_worker()
        tok0 = tok_base + wid * per_w
        for kk in range(TOP_K):
            pltpu.sync_copy(dest_hbm.at[kk, wid], idx_v.at[pl.ds(kk * steps, steps)])
        pltpu.sync_copy(pad_hbm.at[wid], pad_v)
        pltpu.sync_copy(zeros_hbm, rows_v.at[0])
        for j in range(pad_steps):
            pltpu.sync_copy(rows_v.at[0], o_hbm.at[pad_v.at[j]])

        def load(step, b):
            return pltpu.make_async_copy(x_hbm.at[pl.ds(tok0 + step * w, w)], rows_v.at[b], sem_in.at[b])

        def scatter(step, kk, b):
            return pltpu.make_async_copy(rows_v.at[b], o_hbm.at[idx_v.at[kk * steps + step]], sem_out.at[b])

        load(0, 0).start()

        @pl.loop(0, steps, step=2)
        def _(s0):
            for b in range(2):
                step = s0 + b
                load(step, b).wait()

                @pl.when(step >= 1)
                def _():
                    for kk in range(TOP_K):
                        scatter(step - 1, kk, 1 - b).wait()

                @pl.when(step + 1 < steps)
                def _():
                    load(step + 1, 1 - b).start()

                for kk in range(TOP_K):
                    scatter(step, kk, b).start()

        for kk in range(TOP_K):
            scatter(steps - 1, kk, 1).wait()

    return pl.kernel(
        body,
        out_type=jax.ShapeDtypeStruct((n_rows, ROW_TILES, LANES), F32),
        mesh=_sc_mesh(),
        scratch_types=[pltpu.VMEM((steps * TOP_K, w), I32), pltpu.VMEM((pad_steps, w), I32),
                       pltpu.VMEM((2, w, ROW_TILES, LANES), F32), pltpu.SemaphoreType.DMA((2,)),
                       pltpu.SemaphoreType.DMA((2,))],
        name="moe_dispatch_sc",
    )(h2, dest_w, pad_w, zeros)


def _gather_rows(yb, dest_kt):
    n = dest_kt.shape[0]
    w = SC_ROWS
    per_w = n // SC_WORKERS
    steps = per_w // w
    assert steps % 2 == 0 and steps * w * SC_WORKERS == n
    dest_w = dest_kt.reshape(SC_WORKERS, steps, w)

    def body(y_hbm, dest_hbm, o_hbm, idx_v, rows_v, sem_g, sem_w):
        wid = _sc_worker()
        row0 = wid * per_w
        pltpu.sync_copy(dest_hbm.at[wid], idx_v)

        def gather(step, b):
            return pltpu.make_async_copy(y_hbm.at[idx_v.at[step]], rows_v.at[b], sem_g.at[b])

        def write(step, b):
            return pltpu.make_async_copy(rows_v.at[b], o_hbm.at[pl.ds(row0 + step * w, w)], sem_w.at[b])

        gather(0, 0).start()

        @pl.loop(0, steps, step=2)
        def _(s0):
            for b in range(2):
                step = s0 + b
                gather(step, b).wait()

                @pl.when(step >= 1)
                def _():
                    write(step - 1, 1 - b).wait()

                @pl.when(step + 1 < steps)
                def _():
                    gather(step + 1, 1 - b).start()

                write(step, b).start()

        write(steps - 1, 1).wait()

    return pl.kernel(
        body,
        out_type=jax.ShapeDtypeStruct((n, ROW_TILES, LANES), F32),
        mesh=_sc_mesh(),
        scratch_types=[pltpu.VMEM((steps, w), I32), pltpu.VMEM((2, w, ROW_TILES, LANES), F32),
                       pltpu.SemaphoreType.DMA((2,)), pltpu.SemaphoreType.DMA((2,))],
        name="moe_gather_sc",
    )(yb, dest_w)


def _rows_2d(ref, n):
    return jnp.concatenate([ref[pl.ds(c, n, stride=ROW_TILES), :] for c in range(ROW_TILES)], axis=1)


def _store_rows(ref, val):
    for c in range(ROW_TILES):
        ref[pl.ds(c, val.shape[0], stride=ROW_TILES), :] = val[:, c * LANES:(c + 1) * LANES]


def _expert_kernel(blk_e_ref, first_ref, slot_ref, next_ref, nblk_ref, x_ref, wg_hbm, wl_hbm, bg_ref, bl_ref, wd_hbm,
                   bd_ref, y_ref, wg_buf, wl_buf, wd_stage, wd_buf, sem):
    i = pl.program_id(0)

    def fetch(e, sl):
        return [pltpu.make_async_copy(src.at[e], dst.at[sl], sem.at[sl, j])
                for j, (src, dst) in enumerate(((wg_hbm, wg_buf), (wl_hbm, wl_buf), (wd_hbm, wd_stage)))]

    @pl.when(i < nblk_ref[0])
    def _():
        sl = slot_ref[i]

        @pl.when(first_ref[i] == 1)
        def _():
            @pl.when(i == 0)
            def _():
                for cp in fetch(blk_e_ref[0], sl):
                    cp.start()

            for cp in fetch(blk_e_ref[i], sl):
                cp.wait()
            wd_buf[sl] = wd_stage[sl].astype(BF16)

            @pl.when(next_ref[i] >= 0)
            def _():
                for cp in fetch(next_ref[i], 1 - sl):
                    cp.start()

        xb = _rows_2d(x_ref, y_ref.shape[0] // ROW_TILES).astype(BF16)
        glu = jnp.dot(xb, wg_buf[sl], preferred_element_type=F32) + bg_ref[...]
        lin = jnp.dot(xb, wl_buf[sl], preferred_element_type=F32) + bl_ref[...]
        glu = jnp.minimum(glu, SWIGLU_LIMIT)
        lin = jnp.clip(lin, -SWIGLU_LIMIT, SWIGLU_LIMIT)
        act = glu * jax.nn.sigmoid(SWIGLU_ALPHA * glu) * (lin + 1.0)
        _store_rows(y_ref, jnp.dot(act.astype(BF16), wd_buf[sl], preferred_element_type=F32) + bd_ref[...])

    @pl.when(i >= nblk_ref[0])
    def _():
        y_ref[...] = jnp.zeros_like(y_ref)


def _split_kernel(w_ref, perm_ref, g_ref, l_ref):
    half = g_ref.shape[-1]
    sorted_cols = jnp.dot(w_ref[...].astype(BF16), perm_ref[...], preferred_element_type=F32)
    g_ref[...] = sorted_cols[:, :half].astype(BF16)
    l_ref[...] = sorted_cols[:, half:].astype(BF16)


def _split_gate_up(w_gu):
    e, d, n2 = w_gu.shape
    cw = 512
    src = np.concatenate([np.arange(0, cw, 2), np.arange(1, cw, 2)])
    perm = jnp.asarray(np.eye(cw, dtype=np.float32)[:, src], BF16)
    return pl.pallas_call(
        _split_kernel,
        out_shape=[jax.ShapeDtypeStruct((e, d, n2 // 2), BF16), jax.ShapeDtypeStruct((e, d, n2 // 2), BF16)],
        grid=(e, n2 // cw),
        in_specs=[pl.BlockSpec((None, d, cw), lambda i, c: (i, 0, c)), _const_spec((cw, cw))],
        out_specs=[pl.BlockSpec((None, d, cw // 2), lambda i, c: (i, 0, c)),
                   pl.BlockSpec((None, d, cw // 2), lambda i, c: (i, 0, c))],
        compiler_params=_cparams(("parallel", "parallel")),
        name="split_gate_up",
    )(w_gu, perm)


def _experts(xb, blk_meta, w_glu, w_lin, b_glu, b_lin, w_dn, b_dn):
    n_rows = xb.shape[0] // ROW_TILES
    bm = EXPERT_ROWS
    vec = lambda n: pl.BlockSpec((None, 1, n), lambda i, be, *_: (be[i], 0, 0))
    rows = pl.BlockSpec((bm * ROW_TILES, LANES), lambda i, *_: (i, 0))
    hbm = pl.BlockSpec(memory_space=pl.ANY)
    grid_spec = pltpu.PrefetchScalarGridSpec(
        num_scalar_prefetch=5,
        grid=(n_rows // bm,),
        in_specs=[rows, hbm, hbm, vec(D_EXPERT), vec(D_EXPERT), hbm, vec(D_MODEL)],
        out_specs=rows,
        scratch_shapes=[pltpu.VMEM((2, D_MODEL, D_EXPERT), BF16), pltpu.VMEM((2, D_MODEL, D_EXPERT), BF16),
                        pltpu.VMEM((2, D_EXPERT, D_MODEL), F32), pltpu.VMEM((2, D_EXPERT, D_MODEL), BF16),
                        pltpu.SemaphoreType.DMA((2, 3))],
    )
    return pl.pallas_call(
        _expert_kernel,
        out_shape=jax.ShapeDtypeStruct((n_rows * ROW_TILES, LANES), F32),
        grid_spec=grid_spec,
        compiler_params=_cparams(("arbitrary",)),
        name="moe_experts",
    )(*blk_meta, xb, w_glu, w_lin, b_glu, b_lin, w_dn, b_dn)


def _combine_kernel(h_ref, gate_ref, y0_ref, y1_ref, y2_ref, y3_ref, g_ref, b_ref, *rest):
    o_ref = rest[-1]
    tt = o_ref.shape[0]
    gates = gate_ref[...]
    acc = DN_ALPHA * _rows_2d(h_ref, tt)
    for kk, y_ref in enumerate((y0_ref, y1_ref, y2_ref, y3_ref)):
        acc = acc + gates[:, kk:kk + 1] * _rows_2d(y_ref, tt)
    o_ref[...] = _layer_norm_rows(acc, g_ref[...], b_ref[...])


def _combine(h2, gate_slab, yg, ln_g, ln_b, part, prev_out):
    t_all = h2.shape[0] // ROW_TILES
    t = yg.shape[0] // (TOP_K * ROW_TILES)
    tt = min(COMBINE_TOKENS, t)
    steps = t // tt
    first = part * steps
    row = lambda m: pl.BlockSpec((tt, m), lambda i: (first + i, 0))
    tiles = lambda kk: pl.BlockSpec((tt * ROW_TILES, LANES), lambda i: (kk * steps + i, 0))
    in_specs = [pl.BlockSpec((tt * ROW_TILES, LANES), lambda i: (first + i, 0)), row(LANES),
                tiles(0), tiles(1), tiles(2), tiles(3), _const_spec((1, D_MODEL)), _const_spec((1, D_MODEL))]
    args = [h2, gate_slab, yg, yg, yg, yg, ln_g.reshape(1, -1), ln_b.reshape(1, -1)]
    aliases = {}
    if prev_out is not None:
        in_specs.append(pl.BlockSpec(memory_space=pl.ANY))
        args.append(prev_out)
        aliases = {len(args) - 1: 0}
    return pl.pallas_call(
        _combine_kernel,
        out_shape=jax.ShapeDtypeStruct((t_all, D_MODEL), F32),
        grid=(steps,),
        in_specs=in_specs,
        out_specs=row(D_MODEL),
        input_output_aliases=aliases,
        compiler_params=_cparams(("parallel",)),
        name="moe_combine",
    )(*args)


def _moe_part(h2, idx_slab, gate_slab, expert_w, ln_g, ln_b, part, prev_out):
    t = h2.shape[0] // ROW_TILES // MOE_PARTS
    bm = EXPERT_ROWS
    n_assign = t * TOP_K
    n_rows = n_assign + N_EXPERTS * bm
    n_blocks = n_rows // bm

    rank_slab, cnt = _routing_ranks(idx_slab, part, t)
    counts = cnt[0, :N_EXPERTS]
    padded = (counts + bm - 1) // bm * bm
    pends = jnp.cumsum(padded)
    pstarts = pends - padded
    pstart_row = jnp.zeros((8, LANES), I32).at[:, :N_EXPERTS].set(pstarts[None, :])
    blk_start = jnp.arange(n_blocks, dtype=I32) * bm
    blk_e = jnp.minimum(jnp.sum((pends[None, :] <= blk_start[:, None]).astype(I32), axis=1), N_EXPERTS - 1)
    nblk = (pends[-1] // bm).astype(I32).reshape(1)
    active = counts > 0
    ordinal = jnp.cumsum(active.astype(I32)) - 1
    eid = jnp.arange(N_EXPERTS, dtype=I32)
    later = active[None, :] & (eid[None, :] > eid[:, None])
    next_active = jnp.where(later.any(axis=1), jnp.argmax(later, axis=1).astype(I32), -1)
    blk_first = jnp.concatenate([jnp.ones((1,), I32), (blk_e[1:] != blk_e[:-1]).astype(I32)])
    blk_onehot = (blk_e[:, None] == eid[None, :]).astype(I32)
    blk_slot = jnp.sum(blk_onehot * ordinal[None, :], axis=1) % 2
    blk_next = jnp.sum(blk_onehot * next_active[None, :], axis=1)
    blk_meta = (blk_e, blk_first, blk_slot, blk_next, nblk)
    dest_slab = _dest_rows(idx_slab, rank_slab, pstart_row, part)
    dest_kt = dest_slab[:TOP_K]
    pad_off = jnp.arange(bm, dtype=I32)[None, :]
    pad_rows = jnp.where(pad_off < (padded - counts)[:, None], (pstarts + counts)[:, None] + pad_off,
                         n_rows - 1).reshape(-1)

    xb = _dispatch(h2.reshape(-1, ROW_TILES, LANES), dest_kt, pad_rows, n_rows, part * t)
    yb = _experts(xb.reshape(n_rows * ROW_TILES, LANES), blk_meta, *expert_w)
    yg = _gather_rows(yb.reshape(n_rows, ROW_TILES, LANES), dest_kt.reshape(-1))
    return _combine(h2, gate_slab, yg.reshape(n_assign * ROW_TILES, LANES), ln_g, ln_b, part, prev_out)


def _moe(h2, idx_slab, gate_slab, w_gu, b_gu, w_dn, b_dn, ln_g, ln_b):
    w_glu, w_lin = _split_gate_up(w_gu)
    expert_w = (w_glu, w_lin, b_gu[:, None, 0::2], b_gu[:, None, 1::2], w_dn, b_dn[:, None, :])
    out = None
    for part in range(MOE_PARTS):
        out = _moe_part(h2, idx_slab, gate_slab, expert_w, ln_g, ln_b, part, out)
    return out


def kernel(x, mem, positions, w_in, b_igate, b_fgate, conv_w, conv_b, w_mq, w_mk, g_mhead, w_mskip, g_qlat, g_kvlat,
           w_uq, w_ukv, w_br_m, w_br_a, w_mix_out, ln1_g, ln1_b, w_cq, w_ckv, w_co, ln2_g, ln2_b, w_router, b_router,
           w_gu, b_gu, w_dn, b_dn, ln3_g, ln3_b):
    bsz, s, d = x.shape
    t = bsz * s
    h = x.reshape(t, d)
    pos2 = positions.reshape(t, 1)
    for l in range(DEPTH):
        q, k, v, xm, vm, op, gm, ga, gate = _input_projection(h, pos2, w_in[l], g_qlat[l], g_kvlat[l], w_uq[l],
                                                              w_ukv[l])
        ym = _mlstm_branch(xm, vm, op, gate, bsz, s, b_igate[l], b_fgate[l], conv_w[l], conv_b[l], w_mq[l], w_mk[l],
                           g_mhead[l], w_mskip[l])
        oa = _mla_attention(q, k, v, bsz, s)
        h1 = _merge(ym, oa, gm, ga, h, w_br_m[l], w_br_a[l], w_mix_out[l], ln1_g[l], ln1_b[l])
        kv = _mem_kv(mem.reshape(-1, d), w_ckv[l])
        h2, idx_slab, gate_slab = _cross_attention_router(h1, kv, bsz, s, w_cq[l], w_co[l], ln2_g[l], ln2_b[l],
                                                          w_router[l], b_router[l])
        h = _moe(h2, idx_slab, gate_slab, w_gu[l], b_gu[l], w_dn[l], b_dn[l], ln3_g[l], ln3_b[l])
    return h.reshape(bsz, s, d)
```

```python
import functools

import numpy as np
import jax
import jax.numpy as jnp
from jax import lax
from jax.experimental import pallas as pl
from jax.experimental.pallas import tpu as pltpu
from jax.experimental.pallas import tpu_sc as plsc

F32 = jnp.float32
BF16 = jnp.bfloat16
I32 = jnp.int32

D_MODEL = 1024
N_MEM = 256
M_HEADS = 4
M_HEAD_DIM = 128
M_WIDTH = M_HEADS * M_HEAD_DIM
M_CONV = 4
A_HEADS = 8
A_NOPE = 64
A_ROPE = 32
A_QK = A_NOPE + A_ROPE
A_VDIM = 64
A_Q_RANK = 256
A_KV_RANK = 128
ROPE_THETA = 10000.0
X_HEADS = 4
X_HEAD_DIM = D_MODEL // X_HEADS
N_EXPERTS = 32
TOP_K = 4
D_EXPERT = D_MODEL
SWIGLU_ALPHA = 1.702
SWIGLU_LIMIT = 7.0
DEPTH = 1
DN_ALPHA = (2.0 * DEPTH) ** 0.25
EPS = 1e-5
IN_SPLITS = (A_Q_RANK, A_KV_RANK, A_ROPE, M_WIDTH, M_WIDTH, M_WIDTH, M_HEADS, M_HEADS, D_MODEL, D_MODEL)
IN_OFFSETS = tuple(int(v) for v in np.cumsum((0,) + IN_SPLITS))

LANES = 128
VMEM_LIMIT = 56 * 1024 * 1024

PROJ_ROWS = 512
M_CHUNK_ROWS = 128
MLSTM_SEQS = 2
ATT_Q = 256
MERGE_ROWS = 512
XATT_ROWS = 512
RANK_ROWS = 1024
EXPERT_ROWS = 256
COMBINE_TOKENS = 256
MOE_PARTS = 2
SC_CORES = 2
SC_SUBCORES = 16
SC_WORKERS = SC_CORES * SC_SUBCORES
SC_ROWS = 32
NEG_BIG = -1e30
LOG2_E = 1.4426950408889634
ROW_TILES = D_MODEL // LANES

C_QLAT = 0
C_KVLAT = C_QLAT + A_Q_RANK
C_KR = C_KVLAT + A_KV_RANK
C_KRS = C_KR + LANES
C_GATE = C_KRS + LANES
C_XM = C_GATE + LANES
C_VM = C_XM + M_WIDTH
C_OP = C_VM + M_WIDTH
C_GM = C_OP + M_WIDTH
C_GA = C_GM + D_MODEL
C_END = C_GA + D_MODEL


def _cparams(sem, vmem=VMEM_LIMIT):
    return pltpu.CompilerParams(dimension_semantics=sem, vmem_limit_bytes=vmem)


def _const_spec(shape):
    nd = len(shape)
    return pl.BlockSpec(shape, lambda *a: (0,) * nd)


def _layer_norm_rows(v, g, b):
    mu = jnp.mean(v, axis=-1, keepdims=True)
    d = v - mu
    var = jnp.mean(d * d, axis=-1, keepdims=True)
    return d * lax.rsqrt(var + EPS) * g + b


def _proj_kernel(x_ref, pos_ref, w_ref, gq_ref, gkv_ref, wuq_ref, wuqs_ref, wuk_ref, wuv_ref, invf_ref,
                 q_ref, k_ref, v_ref, xm_ref, vm_ref, op_ref, gm_ref, ga_ref, gate_ref):
    xb = x_ref[...].astype(BF16)

    def mm(lo, n):
        return jnp.dot(xb, w_ref[:, lo:lo + n], preferred_element_type=F32)

    xm_ref[...] = mm(C_XM, M_WIDTH).astype(BF16)
    vm_ref[...] = mm(C_VM, M_WIDTH).astype(BF16)
    op_ref[...] = mm(C_OP, M_WIDTH).astype(BF16)
    gm_ref[...] = mm(C_GM, D_MODEL).astype(BF16)
    ga_ref[...] = mm(C_GA, D_MODEL).astype(BF16)
    gate_ref[...] = mm(C_GATE, LANES)

    ang = pos_ref[...].astype(F32) * invf_ref[...]
    cos = jnp.cos(ang)
    sin = jnp.sin(ang)

    q_lat = mm(C_QLAT, A_Q_RANK)
    qn = (q_lat * lax.rsqrt(jnp.mean(q_lat * q_lat, axis=-1, keepdims=True) + EPS) * gq_ref[...]).astype(BF16)
    q = jnp.dot(qn, wuq_ref[...], preferred_element_type=F32)
    qs = jnp.dot(qn, wuqs_ref[...], preferred_element_type=F32)
    kv_lat = mm(C_KVLAT, A_KV_RANK)
    kvn = (kv_lat * lax.rsqrt(jnp.mean(kv_lat * kv_lat, axis=-1, keepdims=True) + EPS) * gkv_ref[...]).astype(BF16)
    kn = jnp.dot(kvn, wuk_ref[...], preferred_element_type=F32)
    lane = lax.broadcasted_iota(I32, (1, A_HEADS * LANES), 1)
    ones_lane = (lane % LANES == A_VDIM).astype(F32)
    v_ref[...] = (jnp.dot(kvn, wuv_ref[...], preferred_element_type=F32) + ones_lane).astype(BF16)
    k_pe = mm(C_KR, LANES) * cos + mm(C_KRS, LANES) * sin
    scale = A_QK ** -0.5 * LOG2_E
    for h in range(A_HEADS):
        sl = slice(h * LANES, (h + 1) * LANES)
        q_ref[:, sl] = ((q[:, sl] * cos + qs[:, sl] * sin) * scale).astype(BF16)
        k_ref[:, sl] = (kn[:, sl] + k_pe).astype(BF16)


def _proj_weights(w_in, w_uq, w_ukv):
    o = IN_OFFSETS
    half = A_ROPE // 2
    w_q, w_kv, w_kr = w_in[:, o[0]:o[1]], w_in[:, o[1]:o[2]], w_in[:, o[2]:o[3]]
    w_xm, w_vm, w_op = w_in[:, o[3]:o[4]], w_in[:, o[4]:o[5]], w_in[:, o[5]:o[6]]
    w_i, w_f, w_gm, w_ga = w_in[:, o[6]:o[7]], w_in[:, o[7]:o[8]], w_in[:, o[8]:o[9]], w_in[:, o[9]:o[10]]
    d = w_in.shape[0]
    z = lambda n: jnp.zeros((d, n), w_in.dtype)
    kr = jnp.concatenate([z(A_NOPE), w_kr, z(LANES - A_QK)], axis=1)
    krs = jnp.concatenate([z(A_NOPE), -w_kr[:, half:], w_kr[:, :half], z(LANES - A_QK)], axis=1)
    gate = jnp.concatenate([w_i, w_f, z(LANES - 2 * M_HEADS)], axis=1)
    w_r = jnp.concatenate([w_q, w_kv, kr, krs, gate, w_xm, w_vm, w_op, w_gm, w_ga], axis=1).astype(BF16)

    uq = w_uq.reshape(A_Q_RANK, A_HEADS, A_QK)
    zq = jnp.zeros((A_Q_RANK, A_HEADS, LANES - A_QK), w_uq.dtype)
    zn = jnp.zeros((A_Q_RANK, A_HEADS, A_NOPE), w_uq.dtype)
    uq_pad = jnp.concatenate([uq, zq], axis=-1).reshape(A_Q_RANK, A_HEADS * LANES).astype(BF16)
    uqs_pad = jnp.concatenate([zn, -uq[..., A_NOPE + half:], uq[..., A_NOPE:A_NOPE + half], zq],
                              axis=-1).reshape(A_Q_RANK, A_HEADS * LANES).astype(BF16)
    ukv = w_ukv.reshape(A_KV_RANK, A_HEADS, A_NOPE + A_VDIM)
    zk = jnp.zeros((A_KV_RANK, A_HEADS, LANES - A_NOPE), w_ukv.dtype)
    uk_pad = jnp.concatenate([ukv[..., :A_NOPE], zk], axis=-1).reshape(A_KV_RANK, A_HEADS * LANES).astype(BF16)
    zv = jnp.zeros((A_KV_RANK, A_HEADS, LANES - A_VDIM), w_ukv.dtype)
    uv_pad = jnp.concatenate([ukv[..., A_NOPE:], zv], axis=-1).reshape(A_KV_RANK, A_HEADS * LANES).astype(BF16)
    return w_r, uq_pad, uqs_pad, uk_pad, uv_pad


def _input_projection(x2, pos2, w_in, g_qlat, g_kvlat, w_uq, w_ukv):
    t = x2.shape[0]
    tm = min(PROJ_ROWS, t)
    w_r, uq_pad, uqs_pad, uk_pad, uv_pad = _proj_weights(w_in, w_uq, w_ukv)
    half = A_ROPE // 2
    inv_freq = ROPE_THETA ** (-jnp.arange(half, dtype=F32) / half)
    invf = jnp.concatenate([jnp.zeros((A_NOPE,), F32), inv_freq, inv_freq,
                            jnp.zeros((LANES - A_QK,), F32)]).reshape(1, LANES)
    hw = A_HEADS * LANES
    row = lambda n: pl.BlockSpec((tm, n), lambda i: (i, 0))
    outs = [
        jax.ShapeDtypeStruct((t, hw), BF16), jax.ShapeDtypeStruct((t, hw), BF16), jax.ShapeDtypeStruct((t, hw), BF16),
        jax.ShapeDtypeStruct((t, M_WIDTH), BF16), jax.ShapeDtypeStruct((t, M_WIDTH), BF16),
        jax.ShapeDtypeStruct((t, M_WIDTH), BF16),
        jax.ShapeDtypeStruct((t, D_MODEL), BF16), jax.ShapeDtypeStruct((t, D_MODEL), BF16),
        jax.ShapeDtypeStruct((t, LANES), F32),
    ]
    return pl.pallas_call(
        _proj_kernel,
        out_shape=outs,
        grid=(t // tm,),
        in_specs=[row(D_MODEL), row(1), _const_spec(w_r.shape), _const_spec((1, A_Q_RANK)),
                  _const_spec((1, A_KV_RANK)), _const_spec(uq_pad.shape), _const_spec(uqs_pad.shape),
                  _const_spec(uk_pad.shape), _const_spec(uv_pad.shape), _const_spec((1, LANES))],
        out_specs=[row(hw), row(hw), row(hw), row(M_WIDTH), row(M_WIDTH), row(M_WIDTH), row(D_MODEL), row(D_MODEL),
                   row(LANES)],
        compiler_params=_cparams(("parallel",)),
        name="input_projection",
    )(x2, pos2, w_r, g_qlat.reshape(1, -1), g_kvlat.reshape(1, -1), uq_pad, uqs_pad, uk_pad, uv_pad, invf)


def _log_sigmoid(v):
    return jnp.minimum(v, 0.0) - jnp.log1p(jnp.exp(-jnp.abs(v)))


def _mlstm_kernel(xm_ref, vm_ref, op_ref, gate_ref, convw_ref, convb_ref, wqk_ref, gbias_ref, ghead_ref,
                  skip_ref, tril_ref, out_ref, xpad_s, ctn_s, m_s, *, seq_len):
    s = seq_len
    lc = M_CHUNK_ROWS
    halo = 8
    for bb in range(MLSTM_SEQS):
        xpad_s[bb, 0:halo, :] = jnp.zeros((halo, M_WIDTH), F32)
        xpad_s[bb, halo:, :] = xm_ref[bb * s:(bb + 1) * s, :].astype(F32)
    ctn_s[...] = jnp.zeros_like(ctn_s)
    m_s[...] = jnp.zeros_like(m_s)
    rows = lax.broadcasted_iota(I32, (lc, lc), 0)
    cols = lax.broadcasted_iota(I32, (lc, lc), 1)
    causal = rows >= cols
    row_id = lax.broadcasted_iota(I32, (lc, LANES), 0)
    ones_blk = (lax.broadcasted_iota(I32, (lc, LANES), 1) == 0).astype(BF16)
    kscale = M_HEAD_DIM ** -0.5
    hsl = [slice(h * M_HEAD_DIM, (h + 1) * M_HEAD_DIM) for h in range(M_HEADS)]
    hcl = [slice(h, h + 1) for h in range(M_HEADS)]
    streams = [(bb, h) for bb in range(MLSTM_SEQS) for h in range(M_HEADS)]

    def gates(bb, r0):
        win = xpad_s[bb, pl.ds(r0 - bb * s, lc + halo), :]
        conv = convb_ref[...]
        for j in range(M_CONV):
            lo = halo - (M_CONV - 1) + j
            conv = conv + win[lo:lo + lc, :] * convw_ref[j:j + 1, :]
        xc = conv * jax.nn.sigmoid(conv)
        g = gate_ref[pl.ds(r0, lc), :] + gbias_ref[...]
        ls = _log_sigmoid(g)
        bc = jnp.dot(tril_ref[...], ls, preferred_element_type=F32, precision=lax.Precision.HIGHEST)
        b0 = pltpu.roll(bc, LANES - M_HEADS, axis=1)
        u = g - b0
        cm = u
        shift = 1
        while shift < lc:
            cm = jnp.maximum(cm, jnp.where(row_id >= shift, pltpu.roll(cm, shift, axis=0), -jnp.inf))
            shift *= 2
        m_prev = m_s[bb:bb + 1, :]
        g_inter = b0 + m_prev
        m = jnp.maximum(g_inter, b0 + cm)
        b_tot = b0[lc - 1:lc, :]
        aw = b_tot - b0 + g
        m_chunk = jnp.max(aw, axis=0, keepdims=True)
        m_new = jnp.maximum(b_tot + m_prev, m_chunk)
        m_s[bb:bb + 1, :] = m_new
        return dict(xc=xc, w_inter=jnp.exp(g_inter - m), e_negm=jnp.exp(-m), a_mat=b0 - m, wa=jnp.exp(aw - m_chunk),
                    s_old=jnp.exp(b_tot + m_prev - m_new), s_new=jnp.exp(m_chunk - m_new), u_t=u.T)

    def chunk(c, carry):
        r0s = [pl.multiple_of(bb * s + c * lc, lc) for bb in range(MLSTM_SEQS)]
        gs = [gates(bb, r0s[bb]) for bb in range(MLSTM_SEQS)]
        qk = [jnp.dot(gs[bb]["xc"].astype(BF16), wqk_ref[...], preferred_element_type=F32)
              for bb in range(MLSTM_SEQS)]
        q_b = [qk[bb][:, hsl[h]].astype(BF16) for bb, h in streams]
        k_f = [qk[bb][:, M_WIDTH + h * M_HEAD_DIM:M_WIDTH + (h + 1) * M_HEAD_DIM] * kscale for bb, h in streams]
        v_aug = [jnp.concatenate([vm_ref[pl.ds(r0s[bb], lc), hsl[h]], ones_blk], axis=1) for bb, h in streams]
        ctn_prev = [ctn_s[bb * M_HEADS + h] for bb, h in streams]
        s_raw = [lax.dot_general(q_b[i], k_f[i].astype(BF16), (((1,), (1,)), ((), ())), preferred_element_type=F32)
                 for i in range(len(streams))]
        inter = [jnp.dot(q_b[i], ctn_prev[i].astype(BF16), preferred_element_type=F32)
                 for i in range(len(streams))]
        ctn_c = [lax.dot_general((k_f[i] * gs[bb]["wa"][:, hcl[h]]).astype(BF16), v_aug[i], (((0,), (0,)), ((), ())),
                                 preferred_element_type=F32) for i, (bb, h) in enumerate(streams)]
        sc_b = [(s_raw[i] * jnp.exp(jnp.where(causal, gs[bb]["a_mat"][:, hcl[h]] + gs[bb]["u_t"][hcl[h], :],
                                              -jnp.inf))).astype(BF16) for i, (bb, h) in enumerate(streams)]
        intra = [jnp.dot(sc_b[i], v_aug[i], preferred_element_type=F32) for i in range(len(streams))]
        for i, (bb, h) in enumerate(streams):
            g = gs[bb]
            ctn_s[bb * M_HEADS + h] = g["s_old"][:, hcl[h]] * ctn_prev[i] + g["s_new"][:, hcl[h]] * ctn_c[i]
            wi = g["w_inter"][:, hcl[h]]
            num = wi * inter[i][:, :M_HEAD_DIM] + intra[i][:, :M_HEAD_DIM]
            den = wi * inter[i][:, M_HEAD_DIM:M_HEAD_DIM + 1] + intra[i][:, M_HEAD_DIM:M_HEAD_DIM + 1]
            hh = num / jnp.maximum(jnp.abs(den), g["e_negm"][:, hcl[h]])
            mu = jnp.mean(hh, axis=-1, keepdims=True)
            dv = hh - mu
            var = jnp.mean(dv * dv, axis=-1, keepdims=True)
            hn = dv * lax.rsqrt(var + EPS) * ghead_ref[:, hsl[h]]
            y = (jax.nn.sigmoid(op_ref[pl.ds(r0s[bb], lc), hsl[h]].astype(F32))
                 * (hn + skip_ref[:, hsl[h]] * g["xc"][:, hsl[h]]))
            out_ref[pl.ds(r0s[bb], lc), hsl[h]] = y.astype(BF16)
        return carry

    lax.fori_loop(0, s // lc, chunk, 0)


def _mlstm_branch(xm, vm, op, gate, bsz, s, b_i, b_f, conv_w, conv_b, w_mq, w_mk, g_mhead, w_mskip):
    lc = M_CHUNK_ROWS
    gbias = jnp.concatenate([b_i, b_f, jnp.zeros((LANES - 2 * M_HEADS,), F32)]).reshape(1, LANES)
    tril = jnp.tril(jnp.ones((lc, lc), F32))
    eye = jnp.eye(M_HEADS, dtype=F32)
    block_diag = lambda w: jnp.einsum('hde,hg->hdge', w, eye).reshape(M_WIDTH, M_WIDTH)
    w_qk = jnp.concatenate([block_diag(w_mq), block_diag(w_mk)], axis=1).astype(BF16)
    nseq = MLSTM_SEQS
    assert bsz % nseq == 0
    seq = lambda n: pl.BlockSpec((nseq * s, n), lambda b: (b, 0))
    return pl.pallas_call(
        functools.partial(_mlstm_kernel, seq_len=s),
        out_shape=jax.ShapeDtypeStruct((bsz * s, M_WIDTH), BF16),
        grid=(bsz // nseq,),
        in_specs=[seq(M_WIDTH), seq(M_WIDTH), seq(M_WIDTH), seq(LANES),
                  _const_spec((M_CONV, M_WIDTH)), _const_spec((1, M_WIDTH)),
                  _const_spec((M_WIDTH, 2 * M_WIDTH)),
                  _const_spec((1, LANES)), _const_spec((1, M_WIDTH)), _const_spec((1, M_WIDTH)),
                  _const_spec((lc, lc))],
        out_specs=seq(M_WIDTH),
        scratch_shapes=[pltpu.VMEM((nseq, s + 8, M_WIDTH), F32),
                        pltpu.VMEM((nseq * M_HEADS, M_HEAD_DIM, 2 * LANES), F32), pltpu.VMEM((8, LANES), F32)],
        compiler_params=_cparams(("parallel",)),
        name="mlstm_branch",
    )(xm, vm, op, gate, conv_w, conv_b.reshape(1, -1), w_qk, gbias, g_mhead.reshape(1, -1), w_mskip.reshape(1, -1),
      tril)


def _attn_kernel(q_ref, k_ref, v_ref, o_ref, s_scr, m_s, acc_s):
    tq = q_ref.shape[0]
    i = pl.program_id(1)
    rows = lax.broadcasted_iota(I32, (tq, tq), 0)
    cols = lax.broadcasted_iota(I32, (tq, tq), 1)

    def lane_tile_max(sc):
        out = sc[:, :LANES]
        for t in range(1, tq // LANES):
            out = jnp.maximum(out, sc[:, t * LANES:(t + 1) * LANES])
        return out

    heads = [slice(h * LANES, (h + 1) * LANES) for h in range(A_HEADS)]
    m_s[...] = jnp.full(m_s.shape, NEG_BIG, F32)
    acc_s[...] = jnp.zeros_like(acc_s)

    def pass_a(j, diagonal):
        k0 = pl.multiple_of(j * tq, tq)
        for h, hs in enumerate(heads):
            sc = lax.dot_general(q_ref[:, hs], k_ref[pl.ds(k0, tq), hs], (((1,), (1,)), ((), ())),
                                 preferred_element_type=F32)
            if diagonal:
                sc = jnp.where(cols <= rows, sc, NEG_BIG)
            s_scr[h, j] = sc
            m_s[h] = jnp.maximum(m_s[h], lane_tile_max(sc))

    def for_blocks(n, fn):
        def pair(p, c):
            fn(2 * p)
            fn(2 * p + 1)
            return c

        lax.fori_loop(0, n // 2, pair, 0)

        @pl.when(n % 2 == 1)
        def _():
            fn(n - 1)

    for_blocks(i, lambda j: pass_a(j, False))
    pass_a(i, True)
    for h in range(A_HEADS):
        m_s[h] = jnp.broadcast_to(jnp.max(m_s[h], axis=1, keepdims=True), (tq, LANES))

    def pass_b(j):
        k0 = pl.multiple_of(j * tq, tq)
        for h, hs in enumerate(heads):
            m_row = m_s[h]
            p = jnp.exp2(s_scr[h, j] - jnp.concatenate([m_row] * (tq // LANES), axis=1)).astype(BF16)
            acc_s[h] += jnp.dot(p, v_ref[pl.ds(k0, tq), hs], preferred_element_type=F32)

    for_blocks(i + 1, pass_b)
    for h, hs in enumerate(heads):
        acc = acc_s[h]
        o_ref[:, hs] = (acc / acc[:, A_VDIM:A_VDIM + 1]).astype(BF16)


def _mla_attention(q, k, v, bsz, s):
    tq = min(ATT_Q, s)
    nq = s // tq
    hw = A_HEADS * LANES
    return pl.pallas_call(
        _attn_kernel,
        out_shape=jax.ShapeDtypeStruct(q.shape, BF16),
        grid=(bsz, nq),
        in_specs=[pl.BlockSpec((tq, hw), lambda b, i: (b * nq + i, 0)),
                  pl.BlockSpec((s, hw), lambda b, i: (b, 0)),
                  pl.BlockSpec((s, hw), lambda b, i: (b, 0))],
        out_specs=pl.BlockSpec((tq, hw), lambda b, i: (b * nq + i, 0)),
        scratch_shapes=[pltpu.VMEM((A_HEADS, nq, tq, tq), F32), pltpu.VMEM((A_HEADS, tq, LANES), F32),
                        pltpu.VMEM((A_HEADS, tq, LANES), F32)],
        compiler_params=_cparams(("parallel", "arbitrary")),
        name="mla_attention",
    )(q, k, v)


def _merge_kernel(ym_ref, oa_ref, gm_ref, ga_ref, x_ref, wbm_ref, wba_ref, wmix_ref, g_ref, b_ref, h_ref):
    y_m = jnp.dot(ym_ref[...], wbm_ref[...], preferred_element_type=F32)
    y_a = jnp.dot(oa_ref[...], wba_ref[...], preferred_element_type=F32)
    mixed = jax.nn.sigmoid(gm_ref[...].astype(F32)) * y_m + jax.nn.sigmoid(ga_ref[...].astype(F32)) * y_a
    mix = jnp.dot(mixed.astype(BF16), wmix_ref[...], preferred_element_type=F32)
    h_ref[...] = _layer_norm_rows(DN_ALPHA * x_ref[...] + mix, g_ref[...], b_ref[...])


def _merge(ym, oa, gm, ga, x2, w_br_m, w_br_a, w_mix_out, ln_g, ln_b):
    t = x2.shape[0]
    tm = min(MERGE_ROWS, t)
    wba = jnp.concatenate([w_br_a.reshape(A_HEADS, A_VDIM, D_MODEL),
                           jnp.zeros((A_HEADS, LANES - A_VDIM, D_MODEL), w_br_a.dtype)], axis=1)
    wba = wba.reshape(A_HEADS * LANES, D_MODEL).astype(BF16)
    row = lambda n: pl.BlockSpec((tm, n), lambda i: (i, 0))
    return pl.pallas_call(
        _merge_kernel,
        out_shape=jax.ShapeDtypeStruct((t, D_MODEL), F32),
        grid=(t // tm,),
        in_specs=[row(M_WIDTH), row(A_HEADS * LANES), row(D_MODEL), row(D_MODEL), row(D_MODEL),
                  _const_spec((M_WIDTH, D_MODEL)), _const_spec((A_HEADS * LANES, D_MODEL)),
                  _const_spec((D_MODEL, D_MODEL)), _const_spec((1, D_MODEL)), _const_spec((1, D_MODEL))],
        out_specs=row(D_MODEL),
        compiler_params=_cparams(("parallel",)),
        name="merge_deepnorm1",
    )(ym, oa, gm, ga, x2, w_br_m.astype(BF16), wba, w_mix_out.astype(BF16), ln_g.reshape(1, -1), ln_b.reshape(1, -1))


def _kv_kernel(m_ref, w_ref, o_ref):
    o_ref[...] = jnp.dot(m_ref[...].astype(BF16), w_ref[...], preferred_element_type=F32).astype(BF16)


def _mem_kv(mem2, w_ckv):
    r = mem2.shape[0]
    tm = min(512, r)
    return pl.pallas_call(
        _kv_kernel,
        out_shape=jax.ShapeDtypeStruct((r, 2 * D_MODEL), BF16),
        grid=(r // tm,),
        in_specs=[pl.BlockSpec((tm, D_MODEL), lambda i: (i, 0)), _const_spec((D_MODEL, 2 * D_MODEL))],
        out_specs=pl.BlockSpec((tm, 2 * D_MODEL), lambda i: (i, 0)),
        compiler_params=_cparams(("parallel",)),
        name="memory_kv",
    )(mem2, w_ckv.astype(BF16))


def _xattn_kernel(h_ref, kv_ref, wq_ref, wo_ref, g_ref, b_ref, wr_ref, br_ref, h2_ref, idx_ref, gate_ref):
    h1 = h_ref[...]
    q = jnp.dot(h1.astype(BF16), wq_ref[...], preferred_element_type=F32).astype(BF16)
    scale = X_HEAD_DIM ** -0.5
    ksl = [slice(hd * X_HEAD_DIM, (hd + 1) * X_HEAD_DIM) for hd in range(X_HEADS)]
    vsl = [slice(D_MODEL + hd * X_HEAD_DIM, D_MODEL + (hd + 1) * X_HEAD_DIM) for hd in range(X_HEADS)]
    scs = [lax.dot_general(q[:, ks], kv_ref[:, ks], (((1,), (1,)), ((), ())), preferred_element_type=F32) * scale
           for ks in ksl]
    ps = [jnp.exp(sc - jnp.max(sc, axis=1, keepdims=True)) for sc in scs]
    ps = [(p / jnp.sum(p, axis=1, keepdims=True)).astype(BF16) for p in ps]
    o = jnp.concatenate([jnp.dot(p, kv_ref[:, vs], preferred_element_type=F32).astype(BF16)
                         for p, vs in zip(ps, vsl)], axis=1)
    att = jnp.dot(o, wo_ref[...], preferred_element_type=F32)
    h2 = _layer_norm_rows(DN_ALPHA * h1 + att, g_ref[...], b_ref[...])
    for c in range(ROW_TILES):
        h2_ref[pl.ds(c, h2.shape[0], stride=ROW_TILES), :] = h2[:, c * LANES:(c + 1) * LANES]

    h2_hi = h2.astype(BF16)
    h2_lo = (h2 - h2_hi.astype(F32)).astype(BF16)
    hi_prod = jnp.dot(h2_hi, wr_ref[...], preferred_element_type=F32)
    lo_prod = jnp.dot(h2_lo, wr_ref[:, :LANES], preferred_element_type=F32)
    logits = hi_prod[:, :LANES] + (hi_prod[:, LANES:] + lo_prod) + br_ref[...]
    tm = logits.shape[0]
    lane = lax.broadcasted_iota(I32, (tm, LANES), 1)
    lane_f = lane.astype(F32)
    work = jnp.where(lane < N_EXPERTS, logits, -jnp.inf)
    vals, idxs = [], []
    for _ in range(TOP_K):
        mx = jnp.max(work, axis=1, keepdims=True)
        ix = jnp.min(jnp.where(work == mx, lane_f, float(LANES)), axis=1, keepdims=True)
        vals.append(mx)
        idxs.append(ix)
        work = jnp.where(lane_f == ix, -jnp.inf, work)
    es = [jnp.exp(vv - vals[0]) for vv in vals]
    tot = es[0] + es[1] + es[2] + es[3]
    idx_slab = jnp.zeros((tm, LANES), F32)
    gate_slab = jnp.zeros((tm, LANES), F32)
    for kk in range(TOP_K):
        idx_slab = jnp.where(lane == kk, idxs[kk], idx_slab)
        gate_slab = jnp.where(lane == kk, es[kk] / tot, gate_slab)
    idx_ref[...] = idx_slab.astype(I32)
    gate_ref[...] = gate_slab


def _cross_attention_router(h1, kv, bsz, s, w_cq, w_co, ln_g, ln_b, w_router, b_router, part):
    n_mem = kv.shape[0] // bsz
    bsz = bsz // MOE_PARTS
    b0 = part * bsz
    t = bsz * s
    tm = min(XATT_ROWS, s)
    ns = s // tm
    wr = jnp.concatenate([w_router, jnp.zeros((D_MODEL, LANES - N_EXPERTS), F32)], axis=1)
    wr_hi = wr.astype(BF16)
    wr = jnp.concatenate([wr_hi, (wr - wr_hi.astype(F32)).astype(BF16)], axis=1)
    br = jnp.concatenate([b_router, jnp.zeros((LANES - N_EXPERTS,), F32)]).reshape(1, LANES)
    row = lambda n: pl.BlockSpec((tm, n), lambda b, i: (b * ns + i, 0))
    h1_rows = pl.BlockSpec((tm, D_MODEL), lambda b, i: ((b0 + b) * ns + i, 0))
    return pl.pallas_call(
        _xattn_kernel,
        out_shape=[jax.ShapeDtypeStruct((t * ROW_TILES, LANES), F32), jax.ShapeDtypeStruct((t, LANES), I32),
                   jax.ShapeDtypeStruct((t, LANES), F32)],
        grid=(bsz, ns),
        in_specs=[h1_rows, pl.BlockSpec((n_mem, 2 * D_MODEL), lambda b, i: (b0 + b, 0)),
                  _const_spec((D_MODEL, D_MODEL)), _const_spec((D_MODEL, D_MODEL)),
                  _const_spec((1, D_MODEL)), _const_spec((1, D_MODEL)),
                  _const_spec((D_MODEL, 2 * LANES)), _const_spec((1, LANES))],
        out_specs=[pl.BlockSpec((tm * ROW_TILES, LANES), lambda b, i: (b * ns + i, 0)), row(LANES), row(LANES)],
        compiler_params=_cparams(("parallel", "parallel")),
        name="cross_attention_router",
    )(h1, kv, w_cq.astype(BF16), w_co.astype(BF16), ln_g.reshape(1, -1), ln_b.reshape(1, -1), wr, br)


def _rank_kernel(idx_ref, ltri_ref, rank_ref, cnt_ref, carry_s):
    @pl.when(pl.program_id(0) == 0)
    def _():
        carry_s[...] = jnp.zeros_like(carry_s)

    idx = idx_ref[...]
    tm = idx.shape[0]
    lane = lax.broadcasted_iota(I32, (tm, LANES), 1)
    onehots = [(lane == idx[:, kk:kk + 1]).astype(F32) for kk in range(TOP_K)]
    sel = onehots[0] + onehots[1] + onehots[2] + onehots[3]
    before = jnp.dot(ltri_ref[...], sel.astype(BF16), preferred_element_type=F32) + carry_s[0:1, :]
    rank_slab = jnp.zeros((tm, LANES), F32)
    for kk in range(TOP_K):
        r = jnp.sum(onehots[kk] * before, axis=1, keepdims=True)
        rank_slab = jnp.where(lane == kk, r, rank_slab)
    rank_ref[...] = rank_slab.astype(I32)
    carry_s[0:1, :] = carry_s[0:1, :] + jnp.sum(sel, axis=0, keepdims=True)
    cnt_ref[...] = jnp.broadcast_to(carry_s[0:1, :], cnt_ref.shape).astype(I32)


def _routing_ranks(idx_slab):
    t = idx_slab.shape[0]
    tm = min(RANK_ROWS, t)
    ltri = jnp.tril(jnp.ones((tm, tm), BF16), k=-1)
    return pl.pallas_call(
        _rank_kernel,
        out_shape=[jax.ShapeDtypeStruct((t, LANES), I32), jax.ShapeDtypeStruct((8, LANES), I32)],
        grid=(t // tm,),
        in_specs=[pl.BlockSpec((tm, LANES), lambda i: (i, 0)), _const_spec((tm, tm))],
        out_specs=[pl.BlockSpec((tm, LANES), lambda i: (i, 0)), _const_spec((8, LANES))],
        scratch_shapes=[pltpu.VMEM((8, LANES), F32)],
        compiler_params=_cparams(("arbitrary",)),
        name="routing_ranks",
    )(idx_slab, ltri)


def _dest_kernel(idx_ref, rank_ref, pstart_ref, dest_ref):
    idx = idx_ref[...]
    tm = idx.shape[0]
    lane = lax.broadcasted_iota(I32, (tm, LANES), 1)
    dest = rank_ref[...].astype(F32)
    pstart = pstart_ref[0:1, :].astype(F32)
    for kk in range(TOP_K):
        start = jnp.sum(jnp.where(lane == idx[:, kk:kk + 1], pstart, 0.0), axis=1, keepdims=True)
        dest = dest + jnp.where(lane == kk, start, 0.0)
    dest_ref[...] = dest.T[0:8, :].astype(I32)


def _dest_rows(idx_slab, rank_slab, pstart_row):
    t = rank_slab.shape[0]
    tm = min(RANK_ROWS, t)
    return pl.pallas_call(
        _dest_kernel,
        out_shape=jax.ShapeDtypeStruct((8, t), I32),
        grid=(t // tm,),
        in_specs=[pl.BlockSpec((tm, LANES), lambda i: (i, 0)), pl.BlockSpec((tm, LANES), lambda i: (i, 0)),
                  _const_spec((8, LANES))],
        out_specs=pl.BlockSpec((8, tm), lambda i: (0, i)),
        compiler_params=_cparams(("parallel",)),
        name="routing_dest",
    )(idx_slab, rank_slab, pstart_row)


def _sc_mesh():
    return plsc.VectorSubcoreMesh(core_axis_name="c", subcore_axis_name="s", num_cores=SC_CORES,
                                  num_subcores=SC_SUBCORES)


def _sc_worker():
    return lax.axis_index("s") * SC_CORES + lax.axis_index("c")


def _dispatch(h2, dest_kt, pad_rows, n_rows):
    t = dest_kt.shape[1]
    w = SC_ROWS
    per_w = t // SC_WORKERS
    steps = per_w // w
    assert steps % 2 == 0 and steps * w * SC_WORKERS == t
    pad_steps = pad_rows.shape[0] // (SC_WORKERS * w)
    dest_w = dest_kt.reshape(TOP_K, SC_WORKERS, steps, w)
    pad_w = pad_rows.reshape(SC_WORKERS, pad_steps, w)
    zeros = jnp.zeros((w, ROW_TILES, LANES), F32)

    def body(x_hbm, dest_hbm, pad_hbm, zeros_hbm, o_hbm, idx_v, pad_v, rows_v, sem_in, sem_out):
        wid = _sc_worker()
        tok0 = wid * per_w
        for kk in range(TOP_K):
            pltpu.sync_copy(dest_hbm.at[kk, wid], idx_v.at[pl.ds(kk * steps, steps)])
        pltpu.sync_copy(pad_hbm.at[wid], pad_v)
        pltpu.sync_copy(zeros_hbm, rows_v.at[0])
        for j in range(pad_steps):
            pltpu.sync_copy(rows_v.at[0], o_hbm.at[pad_v.at[j]])

        def load(step, b):
            return pltpu.make_async_copy(x_hbm.at[pl.ds(tok0 + step * w, w)], rows_v.at[b], sem_in.at[b])

        def scatter(step, kk, b):
            return pltpu.make_async_copy(rows_v.at[b], o_hbm.at[idx_v.at[kk * steps + step]], sem_out.at[b])

        load(0, 0).start()

        @pl.loop(0, steps, step=2)
        def _(s0):
            for b in range(2):
                step = s0 + b
                load(step, b).wait()

                @pl.when(step >= 1)
                def _():
                    for kk in range(TOP_K):
                        scatter(step - 1, kk, 1 - b).wait()

                @pl.when(step + 1 < steps)
                def _():
                    load(step + 1, 1 - b).start()

                for kk in range(TOP_K):
                    scatter(step, kk, b).start()

        for kk in range(TOP_K):
            scatter(steps - 1, kk, 1).wait()

    return pl.kernel(
        body,
        out_type=jax.ShapeDtypeStruct((n_rows, ROW_TILES, LANES), F32),
        mesh=_sc_mesh(),
        scratch_types=[pltpu.VMEM((steps * TOP_K, w), I32), pltpu.VMEM((pad_steps, w), I32),
                       pltpu.VMEM((2, w, ROW_TILES, LANES), F32), pltpu.SemaphoreType.DMA((2,)),
                       pltpu.SemaphoreType.DMA((2,))],
        name="moe_dispatch_sc",
    )(h2, dest_w, pad_w, zeros)


def _gather_rows(yb, dest_kt):
    n = dest_kt.shape[0]
    w = SC_ROWS
    per_w = n // SC_WORKERS
    steps = per_w // w
    assert steps % 2 == 0 and steps * w * SC_WORKERS == n
    dest_w = dest_kt.reshape(SC_WORKERS, steps, w)

    def body(y_hbm, dest_hbm, o_hbm, idx_v, rows_v, sem_g, sem_w):
        wid = _sc_worker()
        row0 = wid * per_w
        pltpu.sync_copy(dest_hbm.at[wid], idx_v)

        def gather(step, b):
            return pltpu.make_async_copy(y_hbm.at[idx_v.at[step]], rows_v.at[b], sem_g.at[b])

        def write(step, b):
            return pltpu.make_async_copy(rows_v.at[b], o_hbm.at[pl.ds(row0 + step * w, w)], sem_w.at[b])

        gather(0, 0).start()

        @pl.loop(0, steps, step=2)
        def _(s0):
            for b in range(2):
                step = s0 + b
                gather(step, b).wait()

                @pl.when(step >= 1)
                def _():
                    write(step - 1, 1 - b).wait()

                @pl.when(step + 1 < steps)
                def _():
                    gather(step + 1, 1 - b).start()

                write(step, b).start()

        write(steps - 1, 1).wait()

    return pl.kernel(
        body,
        out_type=jax.ShapeDtypeStruct((n, ROW_TILES, LANES), F32),
        mesh=_sc_mesh(),
        scratch_types=[pltpu.VMEM((steps, w), I32), pltpu.VMEM((2, w, ROW_TILES, LANES), F32),
                       pltpu.SemaphoreType.DMA((2,)), pltpu.SemaphoreType.DMA((2,))],
        name="moe_gather_sc",
    )(yb, dest_w)


def _rows_2d(ref, n):
    return jnp.concatenate([ref[pl.ds(c, n, stride=ROW_TILES), :] for c in range(ROW_TILES)], axis=1)


def _store_rows(ref, val):
    for c in range(ROW_TILES):
        ref[pl.ds(c, val.shape[0], stride=ROW_TILES), :] = val[:, c * LANES:(c + 1) * LANES]


def _expert_kernel(blk_e_ref, first_ref, slot_ref, next_ref, nblk_ref, x_ref, wg_hbm, wl_hbm, bg_ref, bl_ref, wd_hbm,
                   bd_ref, y_ref, wg_buf, wl_buf, wd_stage, wd_buf, sem):
    i = pl.program_id(0)

    def fetch(e, sl):
        return [pltpu.make_async_copy(src.at[e], dst.at[sl], sem.at[sl, j])
                for j, (src, dst) in enumerate(((wg_hbm, wg_buf), (wl_hbm, wl_buf), (wd_hbm, wd_stage)))]

    @pl.when(i < nblk_ref[0])
    def _():
        sl = slot_ref[i]

        @pl.when(first_ref[i] == 1)
        def _():
            @pl.when(i == 0)
            def _():
                for cp in fetch(blk_e_ref[0], sl):
                    cp.start()

            for cp in fetch(blk_e_ref[i], sl):
                cp.wait()
            wd_buf[sl] = wd_stage[sl].astype(BF16)

            @pl.when(next_ref[i] >= 0)
            def _():
                for cp in fetch(next_ref[i], 1 - sl):
                    cp.start()

        xb = _rows_2d(x_ref, y_ref.shape[0] // ROW_TILES).astype(BF16)
        glu = jnp.dot(xb, wg_buf[sl], preferred_element_type=F32) + bg_ref[...]
        lin = jnp.dot(xb, wl_buf[sl], preferred_element_type=F32) + bl_ref[...]
        glu = jnp.minimum(glu, SWIGLU_LIMIT)
        lin = jnp.clip(lin, -SWIGLU_LIMIT, SWIGLU_LIMIT)
        act = glu * jax.nn.sigmoid(SWIGLU_ALPHA * glu) * (lin + 1.0)
        _store_rows(y_ref, jnp.dot(act.astype(BF16), wd_buf[sl], preferred_element_type=F32) + bd_ref[...])

    @pl.when(i >= nblk_ref[0])
    def _():
        y_ref[...] = jnp.zeros_like(y_ref)


def _split_kernel(w_ref, perm_ref, g_ref, l_ref):
    half = g_ref.shape[-1]
    sorted_cols = jnp.dot(w_ref[...].astype(BF16), perm_ref[...], preferred_element_type=F32)
    g_ref[...] = sorted_cols[:, :half].astype(BF16)
    l_ref[...] = sorted_cols[:, half:].astype(BF16)


def _split_gate_up(w_gu):
    e, d, n2 = w_gu.shape
    cw = 512
    src = np.concatenate([np.arange(0, cw, 2), np.arange(1, cw, 2)])
    perm = jnp.asarray(np.eye(cw, dtype=np.float32)[:, src], BF16)
    return pl.pallas_call(
        _split_kernel,
        out_shape=[jax.ShapeDtypeStruct((e, d, n2 // 2), BF16), jax.ShapeDtypeStruct((e, d, n2 // 2), BF16)],
        grid=(e, n2 // cw),
        in_specs=[pl.BlockSpec((None, d, cw), lambda i, c: (i, 0, c)), _const_spec((cw, cw))],
        out_specs=[pl.BlockSpec((None, d, cw // 2), lambda i, c: (i, 0, c)),
                   pl.BlockSpec((None, d, cw // 2), lambda i, c: (i, 0, c))],
        compiler_params=_cparams(("parallel", "parallel")),
        name="split_gate_up",
    )(w_gu, perm)


def _experts(xb, blk_meta, w_glu, w_lin, b_glu, b_lin, w_dn, b_dn):
    n_rows = xb.shape[0] // ROW_TILES
    bm = EXPERT_ROWS
    vec = lambda n: pl.BlockSpec((None, 1, n), lambda i, be, *_: (be[i], 0, 0))
    rows = pl.BlockSpec((bm * ROW_TILES, LANES), lambda i, *_: (i, 0))
    hbm = pl.BlockSpec(memory_space=pl.ANY)
    grid_spec = pltpu.PrefetchScalarGridSpec(
        num_scalar_prefetch=5,
        grid=(n_rows // bm,),
        in_specs=[rows, hbm, hbm, vec(D_EXPERT), vec(D_EXPERT), hbm, vec(D_MODEL)],
        out_specs=rows,
        scratch_shapes=[pltpu.VMEM((2, D_MODEL, D_EXPERT), BF16), pltpu.VMEM((2, D_MODEL, D_EXPERT), BF16),
                        pltpu.VMEM((2, D_EXPERT, D_MODEL), F32), pltpu.VMEM((2, D_EXPERT, D_MODEL), BF16),
                        pltpu.SemaphoreType.DMA((2, 3))],
    )
    return pl.pallas_call(
        _expert_kernel,
        out_shape=jax.ShapeDtypeStruct((n_rows * ROW_TILES, LANES), F32),
        grid_spec=grid_spec,
        compiler_params=_cparams(("arbitrary",)),
        name="moe_experts",
    )(*blk_meta, xb, w_glu, w_lin, b_glu, b_lin, w_dn, b_dn)


def _combine_kernel(h_ref, gate_ref, y0_ref, y1_ref, y2_ref, y3_ref, g_ref, b_ref, *rest):
    o_ref = rest[-1]
    tt = o_ref.shape[0]
    gates = gate_ref[...]
    acc = DN_ALPHA * _rows_2d(h_ref, tt)
    for kk, y_ref in enumerate((y0_ref, y1_ref, y2_ref, y3_ref)):
        acc = acc + gates[:, kk:kk + 1] * _rows_2d(y_ref, tt)
    o_ref[...] = _layer_norm_rows(acc, g_ref[...], b_ref[...])


def _combine(h2, gate_slab, yg, ln_g, ln_b, part, prev_out):
    t = h2.shape[0] // ROW_TILES
    t_all = t * MOE_PARTS
    tt = min(COMBINE_TOKENS, t)
    steps = t // tt
    first = part * steps
    row = lambda m: pl.BlockSpec((tt, m), lambda i: (i, 0))
    tiles = lambda kk: pl.BlockSpec((tt * ROW_TILES, LANES), lambda i: (kk * steps + i, 0))
    in_specs = [tiles(0), row(LANES),
                tiles(0), tiles(1), tiles(2), tiles(3), _const_spec((1, D_MODEL)), _const_spec((1, D_MODEL))]
    args = [h2, gate_slab, yg, yg, yg, yg, ln_g.reshape(1, -1), ln_b.reshape(1, -1)]
    aliases = {}
    if prev_out is not None:
        in_specs.append(pl.BlockSpec(memory_space=pl.ANY))
        args.append(prev_out)
        aliases = {len(args) - 1: 0}
    return pl.pallas_call(
        _combine_kernel,
        out_shape=jax.ShapeDtypeStruct((t_all, D_MODEL), F32),
        grid=(steps,),
        in_specs=in_specs,
        out_specs=pl.BlockSpec((tt, D_MODEL), lambda i: (first + i, 0)),
        input_output_aliases=aliases,
        compiler_params=_cparams(("parallel",)),
        name="moe_combine",
    )(*args)


def _route_and_dispatch(h2, idx_slab):
    t = h2.shape[0] // ROW_TILES
    bm = EXPERT_ROWS
    n_rows = t * TOP_K + N_EXPERTS * bm
    n_blocks = n_rows // bm

    rank_slab, cnt = _routing_ranks(idx_slab)
    counts = cnt[0, :N_EXPERTS]
    padded = (counts + bm - 1) // bm * bm
    pends = jnp.cumsum(padded)
    pstarts = pends - padded
    pstart_row = jnp.zeros((8, LANES), I32).at[:, :N_EXPERTS].set(pstarts[None, :])
    blk_start = jnp.arange(n_blocks, dtype=I32) * bm
    blk_e = jnp.minimum(jnp.sum((pends[None, :] <= blk_start[:, None]).astype(I32), axis=1), N_EXPERTS - 1)
    nblk = (pends[-1] // bm).astype(I32).reshape(1)
    active = counts > 0
    ordinal = jnp.cumsum(active.astype(I32)) - 1
    eid = jnp.arange(N_EXPERTS, dtype=I32)
    later = active[None, :] & (eid[None, :] > eid[:, None])
    next_active = jnp.where(later.any(axis=1), jnp.argmax(later, axis=1).astype(I32), -1)
    blk_first = jnp.concatenate([jnp.ones((1,), I32), (blk_e[1:] != blk_e[:-1]).astype(I32)])
    blk_onehot = (blk_e[:, None] == eid[None, :]).astype(I32)
    blk_slot = jnp.sum(blk_onehot * ordinal[None, :], axis=1) % 2
    blk_next = jnp.sum(blk_onehot * next_active[None, :], axis=1)
    blk_meta = (blk_e, blk_first, blk_slot, blk_next, nblk)
    dest_kt = _dest_rows(idx_slab, rank_slab, pstart_row)[:TOP_K]
    pad_off = jnp.arange(bm, dtype=I32)[None, :]
    pad_rows = jnp.where(pad_off < (padded - counts)[:, None], (pstarts + counts)[:, None] + pad_off,
                         n_rows - 1).reshape(-1)
    xb = _dispatch(h2.reshape(t, ROW_TILES, LANES), dest_kt, pad_rows, n_rows)
    return xb.reshape(n_rows * ROW_TILES, LANES), blk_meta, dest_kt


def _xattn_moe(h1, kv, bsz, s, w_cq, w_co, ln2_g, ln2_b, w_router, b_router, w_gu, b_gu, w_dn, b_dn, ln3_g, ln3_b):
    parts = []
    for part in range(MOE_PARTS):
        h2, idx_slab, gate_slab = _cross_attention_router(h1, kv, bsz, s, w_cq, w_co, ln2_g, ln2_b, w_router,
                                                          b_router, part)
        parts.append((h2, gate_slab) + _route_and_dispatch(h2, idx_slab))
    w_glu, w_lin = _split_gate_up(w_gu)
    expert_w = (w_glu, w_lin, b_gu[:, None, 0::2], b_gu[:, None, 1::2], w_dn, b_dn[:, None, :])
    out = None
    for part, (h2, gate_slab, xb, blk_meta, dest_kt) in enumerate(parts):
        n_rows = xb.shape[0] // ROW_TILES
        yb = _experts(xb, blk_meta, *expert_w)
        yg = _gather_rows(yb.reshape(n_rows, ROW_TILES, LANES), dest_kt.reshape(-1))
        out = _combine(h2, gate_slab, yg.reshape(-1, LANES), ln3_g, ln3_b, part, out)
    return out


def kernel(x, mem, positions, w_in, b_igate, b_fgate, conv_w, conv_b, w_mq, w_mk, g_mhead, w_mskip, g_qlat, g_kvlat,
           w_uq, w_ukv, w_br_m, w_br_a, w_mix_out, ln1_g, ln1_b, w_cq, w_ckv, w_co, ln2_g, ln2_b, w_router, b_router,
           w_gu, b_gu, w_dn, b_dn, ln3_g, ln3_b):
    bsz, s, d = x.shape
    t = bsz * s
    h = x.reshape(t, d)
    pos2 = positions.reshape(t, 1)
    for l in range(DEPTH):
        q, k, v, xm, vm, op, gm, ga, gate = _input_projection(h, pos2, w_in[l], g_qlat[l], g_kvlat[l], w_uq[l],
                                                              w_ukv[l])
        ym = _mlstm_branch(xm, vm, op, gate, bsz, s, b_igate[l], b_fgate[l], conv_w[l], conv_b[l], w_mq[l], w_mk[l],
                           g_mhead[l], w_mskip[l])
        oa = _mla_attention(q, k, v, bsz, s)
        h1 = _merge(ym, oa, gm, ga, h, w_br_m[l], w_br_a[l], w_mix_out[l], ln1_g[l], ln1_b[l])
        kv = _mem_kv(mem.reshape(-1, d), w_ckv[l])
        h = _xattn_moe(h1, kv, bsz, s, w_cq[l], w_co[l], ln2_g[l], ln2_b[l], w_router[l], b_router[l], w_gu[l],
                       b_gu[l], w_dn[l], b_dn[l], ln3_g[l], ln3_b[l])
    return h.reshape(bsz, s, d)
```

```python
import functools

import numpy as np
import jax
import jax.numpy as jnp
from jax import lax
from jax.experimental import pallas as pl
from jax.experimental.pallas import tpu as pltpu
from jax.experimental.pallas import tpu_sc as plsc

F32 = jnp.float32
BF16 = jnp.bfloat16
I32 = jnp.int32

D_MODEL = 1024
N_MEM = 256
M_HEADS = 4
M_HEAD_DIM = 128
M_WIDTH = M_HEADS * M_HEAD_DIM
M_CONV = 4
A_HEADS = 8
A_NOPE = 64
A_ROPE = 32
A_QK = A_NOPE + A_ROPE
A_VDIM = 64
A_Q_RANK = 256
A_KV_RANK = 128
ROPE_THETA = 10000.0
X_HEADS = 4
X_HEAD_DIM = D_MODEL // X_HEADS
N_EXPERTS = 32
TOP_K = 4
D_EXPERT = D_MODEL
SWIGLU_ALPHA = 1.702
SWIGLU_LIMIT = 7.0
DEPTH = 1
DN_ALPHA = (2.0 * DEPTH) ** 0.25
EPS = 1e-5
IN_SPLITS = (A_Q_RANK, A_KV_RANK, A_ROPE, M_WIDTH, M_WIDTH, M_WIDTH, M_HEADS, M_HEADS, D_MODEL, D_MODEL)
IN_OFFSETS = tuple(int(v) for v in np.cumsum((0,) + IN_SPLITS))

LANES = 128
VMEM_LIMIT = 56 * 1024 * 1024

PROJ_ROWS = 512
M_CHUNK_ROWS = 128
MLSTM_SEQS = 2
ATT_Q = 256
MERGE_ROWS = 512
XATT_ROWS = 512
RANK_ROWS = 1024
EXPERT_ROWS = 256
COMBINE_TOKENS = 256
MOE_PARTS = 2
SC_CORES = 2
SC_SUBCORES = 16
SC_WORKERS = SC_CORES * SC_SUBCORES
SC_ROWS = 32
NEG_BIG = -1e30
LOG2_E = 1.4426950408889634
ROW_TILES = D_MODEL // LANES

C_QLAT = 0
C_KVLAT = C_QLAT + A_Q_RANK
C_KR = C_KVLAT + A_KV_RANK
C_KRS = C_KR + LANES
C_GATE = C_KRS + LANES
C_XM = C_GATE + LANES
C_VM = C_XM + M_WIDTH
C_OP = C_VM + M_WIDTH
C_GM = C_OP + M_WIDTH
C_GA = C_GM + D_MODEL
C_END = C_GA + D_MODEL


def _cparams(sem, vmem=VMEM_LIMIT):
    return pltpu.CompilerParams(dimension_semantics=sem, vmem_limit_bytes=vmem)


def _const_spec(shape):
    nd = len(shape)
    return pl.BlockSpec(shape, lambda *a: (0,) * nd)


def _layer_norm_rows(v, g, b):
    mu = jnp.mean(v, axis=-1, keepdims=True)
    d = v - mu
    var = jnp.mean(d * d, axis=-1, keepdims=True)
    return d * lax.rsqrt(var + EPS) * g + b


def _proj_kernel(x_ref, pos_ref, w_ref, gq_ref, gkv_ref, wuq_ref, wuqs_ref, wuk_ref, wuv_ref, invf_ref,
                 q_ref, k_ref, v_ref, xm_ref, vm_ref, op_ref, gm_ref, ga_ref, gate_ref):
    xb = x_ref[...].astype(BF16)

    def mm(lo, n):
        return jnp.dot(xb, w_ref[:, lo:lo + n], preferred_element_type=F32)

    xm_ref[...] = mm(C_XM, M_WIDTH).astype(BF16)
    vm_ref[...] = mm(C_VM, M_WIDTH).astype(BF16)
    op_ref[...] = mm(C_OP, M_WIDTH).astype(BF16)
    gm_ref[...] = mm(C_GM, D_MODEL).astype(BF16)
    ga_ref[...] = mm(C_GA, D_MODEL).astype(BF16)
    gate_ref[...] = mm(C_GATE, LANES)

    ang = pos_ref[...].astype(F32) * invf_ref[...]
    cos = jnp.cos(ang)
    sin = jnp.sin(ang)

    q_lat = mm(C_QLAT, A_Q_RANK)
    qn = (q_lat * lax.rsqrt(jnp.mean(q_lat * q_lat, axis=-1, keepdims=True) + EPS) * gq_ref[...]).astype(BF16)
    q = jnp.dot(qn, wuq_ref[...], preferred_element_type=F32)
    qs = jnp.dot(qn, wuqs_ref[...], preferred_element_type=F32)
    kv_lat = mm(C_KVLAT, A_KV_RANK)
    kvn = (kv_lat * lax.rsqrt(jnp.mean(kv_lat * kv_lat, axis=-1, keepdims=True) + EPS) * gkv_ref[...]).astype(BF16)
    kn = jnp.dot(kvn, wuk_ref[...], preferred_element_type=F32)
    lane = lax.broadcasted_iota(I32, (1, A_HEADS * LANES), 1)
    ones_lane = (lane % LANES == A_VDIM).astype(F32)
    v_ref[...] = (jnp.dot(kvn, wuv_ref[...], preferred_element_type=F32) + ones_lane).astype(BF16)
    k_pe = mm(C_KR, LANES) * cos + mm(C_KRS, LANES) * sin
    scale = A_QK ** -0.5 * LOG2_E
    for h in range(A_HEADS):
        sl = slice(h * LANES, (h + 1) * LANES)
        q_ref[:, sl] = ((q[:, sl] * cos + qs[:, sl] * sin) * scale).astype(BF16)
        k_ref[:, sl] = (kn[:, sl] + k_pe).astype(BF16)


def _proj_weights(w_in, w_uq, w_ukv):
    o = IN_OFFSETS
    half = A_ROPE // 2
    w_q, w_kv, w_kr = w_in[:, o[0]:o[1]], w_in[:, o[1]:o[2]], w_in[:, o[2]:o[3]]
    w_xm, w_vm, w_op = w_in[:, o[3]:o[4]], w_in[:, o[4]:o[5]], w_in[:, o[5]:o[6]]
    w_i, w_f, w_gm, w_ga = w_in[:, o[6]:o[7]], w_in[:, o[7]:o[8]], w_in[:, o[8]:o[9]], w_in[:, o[9]:o[10]]
    d = w_in.shape[0]
    z = lambda n: jnp.zeros((d, n), w_in.dtype)
    kr = jnp.concatenate([z(A_NOPE), w_kr, z(LANES - A_QK)], axis=1)
    krs = jnp.concatenate([z(A_NOPE), -w_kr[:, half:], w_kr[:, :half], z(LANES - A_QK)], axis=1)
    gate = jnp.concatenate([w_i, w_f, z(LANES - 2 * M_HEADS)], axis=1)
    w_r = jnp.concatenate([w_q, w_kv, kr, krs, gate, w_xm, w_vm, w_op, w_gm, w_ga], axis=1).astype(BF16)

    uq = w_uq.reshape(A_Q_RANK, A_HEADS, A_QK)
    zq = jnp.zeros((A_Q_RANK, A_HEADS, LANES - A_QK), w_uq.dtype)
    zn = jnp.zeros((A_Q_RANK, A_HEADS, A_NOPE), w_uq.dtype)
    uq_pad = jnp.concatenate([uq, zq], axis=-1).reshape(A_Q_RANK, A_HEADS * LANES).astype(BF16)
    uqs_pad = jnp.concatenate([zn, -uq[..., A_NOPE + half:], uq[..., A_NOPE:A_NOPE + half], zq],
                              axis=-1).reshape(A_Q_RANK, A_HEADS * LANES).astype(BF16)
    ukv = w_ukv.reshape(A_KV_RANK, A_HEADS, A_NOPE + A_VDIM)
    zk = jnp.zeros((A_KV_RANK, A_HEADS, LANES - A_NOPE), w_ukv.dtype)
    uk_pad = jnp.concatenate([ukv[..., :A_NOPE], zk], axis=-1).reshape(A_KV_RANK, A_HEADS * LANES).astype(BF16)
    zv = jnp.zeros((A_KV_RANK, A_HEADS, LANES - A_VDIM), w_ukv.dtype)
    uv_pad = jnp.concatenate([ukv[..., A_NOPE:], zv], axis=-1).reshape(A_KV_RANK, A_HEADS * LANES).astype(BF16)
    return w_r, uq_pad, uqs_pad, uk_pad, uv_pad


def _input_projection(x2, pos2, w_in, g_qlat, g_kvlat, w_uq, w_ukv):
    t = x2.shape[0]
    tm = min(PROJ_ROWS, t)
    w_r, uq_pad, uqs_pad, uk_pad, uv_pad = _proj_weights(w_in, w_uq, w_ukv)
    half = A_ROPE // 2
    inv_freq = ROPE_THETA ** (-jnp.arange(half, dtype=F32) / half)
    invf = jnp.concatenate([jnp.zeros((A_NOPE,), F32), inv_freq, inv_freq,
                            jnp.zeros((LANES - A_QK,), F32)]).reshape(1, LANES)
    hw = A_HEADS * LANES
    row = lambda n: pl.BlockSpec((tm, n), lambda i: (i, 0))
    outs = [
        jax.ShapeDtypeStruct((t, hw), BF16), jax.ShapeDtypeStruct((t, hw), BF16), jax.ShapeDtypeStruct((t, hw), BF16),
        jax.ShapeDtypeStruct((t, M_WIDTH), BF16), jax.ShapeDtypeStruct((t, M_WIDTH), BF16),
        jax.ShapeDtypeStruct((t, M_WIDTH), BF16),
        jax.ShapeDtypeStruct((t, D_MODEL), BF16), jax.ShapeDtypeStruct((t, D_MODEL), BF16),
        jax.ShapeDtypeStruct((t, LANES), F32),
    ]
    return pl.pallas_call(
        _proj_kernel,
        out_shape=outs,
        grid=(t // tm,),
        in_specs=[row(D_MODEL), row(1), _const_spec(w_r.shape), _const_spec((1, A_Q_RANK)),
                  _const_spec((1, A_KV_RANK)), _const_spec(uq_pad.shape), _const_spec(uqs_pad.shape),
                  _const_spec(uk_pad.shape), _const_spec(uv_pad.shape), _const_spec((1, LANES))],
        out_specs=[row(hw), row(hw), row(hw), row(M_WIDTH), row(M_WIDTH), row(M_WIDTH), row(D_MODEL), row(D_MODEL),
                   row(LANES)],
        compiler_params=_cparams(("parallel",)),
        name="input_projection",
    )(x2, pos2, w_r, g_qlat.reshape(1, -1), g_kvlat.reshape(1, -1), uq_pad, uqs_pad, uk_pad, uv_pad, invf)


def _log_sigmoid(v):
    return jnp.minimum(v, 0.0) - jnp.log1p(jnp.exp(-jnp.abs(v)))


def _mlstm_kernel(xm_ref, vm_ref, op_ref, gate_ref, convw_ref, convb_ref, wqk_ref, gbias_ref, ghead_ref,
                  skip_ref, tril_ref, out_ref, xpad_s, ctn_s, m_s, *, seq_len):
    s = seq_len
    lc = M_CHUNK_ROWS
    halo = 8
    for bb in range(MLSTM_SEQS):
        xpad_s[bb, 0:halo, :] = jnp.zeros((halo, M_WIDTH), F32)
        xpad_s[bb, halo:, :] = xm_ref[bb * s:(bb + 1) * s, :].astype(F32)
    ctn_s[...] = jnp.zeros_like(ctn_s)
    m_s[...] = jnp.zeros_like(m_s)
    rows = lax.broadcasted_iota(I32, (lc, lc), 0)
    cols = lax.broadcasted_iota(I32, (lc, lc), 1)
    causal = rows >= cols
    row_id = lax.broadcasted_iota(I32, (lc, LANES), 0)
    ones_blk = (lax.broadcasted_iota(I32, (lc, LANES), 1) == 0).astype(BF16)
    kscale = M_HEAD_DIM ** -0.5
    hsl = [slice(h * M_HEAD_DIM, (h + 1) * M_HEAD_DIM) for h in range(M_HEADS)]
    hcl = [slice(h, h + 1) for h in range(M_HEADS)]
    streams = [(bb, h) for bb in range(MLSTM_SEQS) for h in range(M_HEADS)]

    def gates(bb, r0):
        win = xpad_s[bb, pl.ds(r0 - bb * s, lc + halo), :]
        conv = convb_ref[...]
        for j in range(M_CONV):
            lo = halo - (M_CONV - 1) + j
            conv = conv + win[lo:lo + lc, :] * convw_ref[j:j + 1, :]
        xc = conv * jax.nn.sigmoid(conv)
        g = gate_ref[pl.ds(r0, lc), :] + gbias_ref[...]
        ls = _log_sigmoid(g)
        bc = jnp.dot(tril_ref[...], ls, preferred_element_type=F32, precision=lax.Precision.HIGHEST)
        b0 = pltpu.roll(bc, LANES - M_HEADS, axis=1)
        u = g - b0
        cm = u
        shift = 1
        while shift < lc:
            cm = jnp.maximum(cm, jnp.where(row_id >= shift, pltpu.roll(cm, shift, axis=0), -jnp.inf))
            shift *= 2
        m_prev = m_s[bb:bb + 1, :]
        g_inter = b0 + m_prev
        m = jnp.maximum(g_inter, b0 + cm)
        b_tot = b0[lc - 1:lc, :]
        aw = b_tot - b0 + g
        m_chunk = jnp.max(aw, axis=0, keepdims=True)
        m_new = jnp.maximum(b_tot + m_prev, m_chunk)
        m_s[bb:bb + 1, :] = m_new
        return dict(xc=xc, w_inter=jnp.exp(g_inter - m), e_negm=jnp.exp(-m), a_mat=b0 - m, wa=jnp.exp(aw - m_chunk),
                    s_old=jnp.exp(b_tot + m_prev - m_new), s_new=jnp.exp(m_chunk - m_new), u_t=u.T)

    def chunk(c, carry):
        r0s = [pl.multiple_of(bb * s + c * lc, lc) for bb in range(MLSTM_SEQS)]
        gs = [gates(bb, r0s[bb]) for bb in range(MLSTM_SEQS)]
        qk = [jnp.dot(gs[bb]["xc"].astype(BF16), wqk_ref[...], preferred_element_type=F32)
              for bb in range(MLSTM_SEQS)]
        q_b = [qk[bb][:, hsl[h]].astype(BF16) for bb, h in streams]
        k_f = [qk[bb][:, M_WIDTH + h * M_HEAD_DIM:M_WIDTH + (h + 1) * M_HEAD_DIM] * kscale for bb, h in streams]
        v_aug = [jnp.concatenate([vm_ref[pl.ds(r0s[bb], lc), hsl[h]], ones_blk], axis=1) for bb, h in streams]
        ctn_prev = [ctn_s[bb * M_HEADS + h] for bb, h in streams]
        s_raw = [lax.dot_general(q_b[i], k_f[i].astype(BF16), (((1,), (1,)), ((), ())), preferred_element_type=F32)
                 for i in range(len(streams))]
        inter = [jnp.dot(q_b[i], ctn_prev[i].astype(BF16), preferred_element_type=F32)
                 for i in range(len(streams))]
        ctn_c = [lax.dot_general((k_f[i] * gs[bb]["wa"][:, hcl[h]]).astype(BF16), v_aug[i], (((0,), (0,)), ((), ())),
                                 preferred_element_type=F32) for i, (bb, h) in enumerate(streams)]
        sc_b = [(s_raw[i] * jnp.exp(jnp.where(causal, gs[bb]["a_mat"][:, hcl[h]] + gs[bb]["u_t"][hcl[h], :],
                                              -jnp.inf))).astype(BF16) for i, (bb, h) in enumerate(streams)]
        intra = [jnp.dot(sc_b[i], v_aug[i], preferred_element_type=F32) for i in range(len(streams))]
        for i, (bb, h) in enumerate(streams):
            g = gs[bb]
            ctn_s[bb * M_HEADS + h] = g["s_old"][:, hcl[h]] * ctn_prev[i] + g["s_new"][:, hcl[h]] * ctn_c[i]
            wi = g["w_inter"][:, hcl[h]]
            num = wi * inter[i][:, :M_HEAD_DIM] + intra[i][:, :M_HEAD_DIM]
            den = wi * inter[i][:, M_HEAD_DIM:M_HEAD_DIM + 1] + intra[i][:, M_HEAD_DIM:M_HEAD_DIM + 1]
            hh = num / jnp.maximum(jnp.abs(den), g["e_negm"][:, hcl[h]])
            mu = jnp.mean(hh, axis=-1, keepdims=True)
            dv = hh - mu
            var = jnp.mean(dv * dv, axis=-1, keepdims=True)
            hn = dv * lax.rsqrt(var + EPS) * ghead_ref[:, hsl[h]]
            y = (jax.nn.sigmoid(op_ref[pl.ds(r0s[bb], lc), hsl[h]].astype(F32))
                 * (hn + skip_ref[:, hsl[h]] * g["xc"][:, hsl[h]]))
            out_ref[pl.ds(r0s[bb], lc), hsl[h]] = y.astype(BF16)
        return carry

    lax.fori_loop(0, s // lc, chunk, 0)


def _mlstm_branch(xm, vm, op, gate, bsz, s, b_i, b_f, conv_w, conv_b, w_mq, w_mk, g_mhead, w_mskip):
    lc = M_CHUNK_ROWS
    gbias = jnp.concatenate([b_i, b_f, jnp.zeros((LANES - 2 * M_HEADS,), F32)]).reshape(1, LANES)
    tril = jnp.tril(jnp.ones((lc, lc), F32))
    eye = jnp.eye(M_HEADS, dtype=F32)
    block_diag = lambda w: jnp.einsum('hde,hg->hdge', w, eye).reshape(M_WIDTH, M_WIDTH)
    w_qk = jnp.concatenate([block_diag(w_mq), block_diag(w_mk)], axis=1).astype(BF16)
    nseq = MLSTM_SEQS
    assert bsz % nseq == 0
    seq = lambda n: pl.BlockSpec((nseq * s, n), lambda b: (b, 0))
    return pl.pallas_call(
        functools.partial(_mlstm_kernel, seq_len=s),
        out_shape=jax.ShapeDtypeStruct((bsz * s, M_WIDTH), BF16),
        grid=(bsz // nseq,),
        in_specs=[seq(M_WIDTH), seq(M_WIDTH), seq(M_WIDTH), seq(LANES),
                  _const_spec((M_CONV, M_WIDTH)), _const_spec((1, M_WIDTH)),
                  _const_spec((M_WIDTH, 2 * M_WIDTH)),
                  _const_spec((1, LANES)), _const_spec((1, M_WIDTH)), _const_spec((1, M_WIDTH)),
                  _const_spec((lc, lc))],
        out_specs=seq(M_WIDTH),
        scratch_shapes=[pltpu.VMEM((nseq, s + 8, M_WIDTH), F32),
                        pltpu.VMEM((nseq * M_HEADS, M_HEAD_DIM, 2 * LANES), F32), pltpu.VMEM((8, LANES), F32)],
        compiler_params=_cparams(("parallel",)),
        name="mlstm_branch",
    )(xm, vm, op, gate, conv_w, conv_b.reshape(1, -1), w_qk, gbias, g_mhead.reshape(1, -1), w_mskip.reshape(1, -1),
      tril)


def _attn_kernel(q_ref, k_ref, v_ref, o_ref, s_scr, m_s, acc_s):
    tq = q_ref.shape[0]
    i = pl.program_id(1)
    rows = lax.broadcasted_iota(I32, (tq, tq), 0)
    cols = lax.broadcasted_iota(I32, (tq, tq), 1)

    def lane_tile_max(sc):
        out = sc[:, :LANES]
        for t in range(1, tq // LANES):
            out = jnp.maximum(out, sc[:, t * LANES:(t + 1) * LANES])
        return out

    heads = [slice(h * LANES, (h + 1) * LANES) for h in range(A_HEADS)]
    m_s[...] = jnp.full(m_s.shape, NEG_BIG, F32)
    acc_s[...] = jnp.zeros_like(acc_s)

    def pass_a(j, diagonal):
        k0 = pl.multiple_of(j * tq, tq)
        for h, hs in enumerate(heads):
            sc = lax.dot_general(q_ref[:, hs], k_ref[pl.ds(k0, tq), hs], (((1,), (1,)), ((), ())),
                                 preferred_element_type=F32)
            if diagonal:
                sc = jnp.where(cols <= rows, sc, NEG_BIG)
            s_scr[h, j] = sc
            m_s[h] = jnp.maximum(m_s[h], lane_tile_max(sc))

    def for_blocks(n, fn):
        def pair(p, c):
            fn(2 * p)
            fn(2 * p + 1)
            return c

        lax.fori_loop(0, n // 2, pair, 0)

        @pl.when(n % 2 == 1)
        def _():
            fn(n - 1)

    for_blocks(i, lambda j: pass_a(j, False))
    pass_a(i, True)
    for h in range(A_HEADS):
        m_s[h] = jnp.broadcast_to(jnp.max(m_s[h], axis=1, keepdims=True), (tq, LANES))

    def pass_b(j):
        k0 = pl.multiple_of(j * tq, tq)
        for h, hs in enumerate(heads):
            m_row = m_s[h]
            p = jnp.exp2(s_scr[h, j] - jnp.concatenate([m_row] * (tq // LANES), axis=1)).astype(BF16)
            acc_s[h] += jnp.dot(p, v_ref[pl.ds(k0, tq), hs], preferred_element_type=F32)

    for_blocks(i + 1, pass_b)
    for h, hs in enumerate(heads):
        acc = acc_s[h]
        o_ref[:, hs] = (acc / acc[:, A_VDIM:A_VDIM + 1]).astype(BF16)


def _mla_attention(q, k, v, bsz, s):
    tq = min(ATT_Q, s)
    nq = s // tq
    hw = A_HEADS * LANES
    return pl.pallas_call(
        _attn_kernel,
        out_shape=jax.ShapeDtypeStruct(q.shape, BF16),
        grid=(bsz, nq),
        in_specs=[pl.BlockSpec((tq, hw), lambda b, i: (b * nq + i, 0)),
                  pl.BlockSpec((s, hw), lambda b, i: (b, 0)),
                  pl.BlockSpec((s, hw), lambda b, i: (b, 0))],
        out_specs=pl.BlockSpec((tq, hw), lambda b, i: (b * nq + i, 0)),
        scratch_shapes=[pltpu.VMEM((A_HEADS, nq, tq, tq), F32), pltpu.VMEM((A_HEADS, tq, LANES), F32),
                        pltpu.VMEM((A_HEADS, tq, LANES), F32)],
        compiler_params=_cparams(("parallel", "arbitrary")),
        name="mla_attention",
    )(q, k, v)


def _merge_kernel(ym_ref, oa_ref, gm_ref, ga_ref, x_ref, wbm_ref, wba_ref, wmix_ref, g_ref, b_ref, h_ref):
    y_m = jnp.dot(ym_ref[...], wbm_ref[...], preferred_element_type=F32)
    y_a = jnp.dot(oa_ref[...], wba_ref[...], preferred_element_type=F32)
    mixed = jax.nn.sigmoid(gm_ref[...].astype(F32)) * y_m + jax.nn.sigmoid(ga_ref[...].astype(F32)) * y_a
    mix = jnp.dot(mixed.astype(BF16), wmix_ref[...], preferred_element_type=F32)
    h_ref[...] = _layer_norm_rows(DN_ALPHA * x_ref[...] + mix, g_ref[...], b_ref[...])


def _merge(ym, oa, gm, ga, x2, w_br_m, w_br_a, w_mix_out, ln_g, ln_b):
    t = x2.shape[0]
    tm = min(MERGE_ROWS, t)
    wba = jnp.concatenate([w_br_a.reshape(A_HEADS, A_VDIM, D_MODEL),
                           jnp.zeros((A_HEADS, LANES - A_VDIM, D_MODEL), w_br_a.dtype)], axis=1)
    wba = wba.reshape(A_HEADS * LANES, D_MODEL).astype(BF16)
    row = lambda n: pl.BlockSpec((tm, n), lambda i: (i, 0))
    return pl.pallas_call(
        _merge_kernel,
        out_shape=jax.ShapeDtypeStruct((t, D_MODEL), F32),
        grid=(t // tm,),
        in_specs=[row(M_WIDTH), row(A_HEADS * LANES), row(D_MODEL), row(D_MODEL), row(D_MODEL),
                  _const_spec((M_WIDTH, D_MODEL)), _const_spec((A_HEADS * LANES, D_MODEL)),
                  _const_spec((D_MODEL, D_MODEL)), _const_spec((1, D_MODEL)), _const_spec((1, D_MODEL))],
        out_specs=row(D_MODEL),
        compiler_params=_cparams(("parallel",)),
        name="merge_deepnorm1",
    )(ym, oa, gm, ga, x2, w_br_m.astype(BF16), wba, w_mix_out.astype(BF16), ln_g.reshape(1, -1), ln_b.reshape(1, -1))


def _kv_kernel(m_ref, w_ref, o_ref):
    o_ref[...] = jnp.dot(m_ref[...].astype(BF16), w_ref[...], preferred_element_type=F32).astype(BF16)


def _mem_kv(mem2, w_ckv):
    r = mem2.shape[0]
    tm = min(512, r)
    return pl.pallas_call(
        _kv_kernel,
        out_shape=jax.ShapeDtypeStruct((r, 2 * D_MODEL), BF16),
        grid=(r // tm,),
        in_specs=[pl.BlockSpec((tm, D_MODEL), lambda i: (i, 0)), _const_spec((D_MODEL, 2 * D_MODEL))],
        out_specs=pl.BlockSpec((tm, 2 * D_MODEL), lambda i: (i, 0)),
        compiler_params=_cparams(("parallel",)),
        name="memory_kv",
    )(mem2, w_ckv.astype(BF16))


def _xattn_kernel(h_ref, kv_ref, wq_ref, wo_ref, g_ref, b_ref, wr_ref, br_ref, *rest):
    h2_ref, idx_ref, gate_ref = rest[-3:]
    h1 = h_ref[...]
    q = jnp.dot(h1.astype(BF16), wq_ref[...], preferred_element_type=F32).astype(BF16)
    scale = X_HEAD_DIM ** -0.5
    ksl = [slice(hd * X_HEAD_DIM, (hd + 1) * X_HEAD_DIM) for hd in range(X_HEADS)]
    vsl = [slice(D_MODEL + hd * X_HEAD_DIM, D_MODEL + (hd + 1) * X_HEAD_DIM) for hd in range(X_HEADS)]
    scs = [lax.dot_general(q[:, ks], kv_ref[:, ks], (((1,), (1,)), ((), ())), preferred_element_type=F32) * scale
           for ks in ksl]
    ps = [jnp.exp(sc - jnp.max(sc, axis=1, keepdims=True)) for sc in scs]
    ps = [(p / jnp.sum(p, axis=1, keepdims=True)).astype(BF16) for p in ps]
    o = jnp.concatenate([jnp.dot(p, kv_ref[:, vs], preferred_element_type=F32).astype(BF16)
                         for p, vs in zip(ps, vsl)], axis=1)
    att = jnp.dot(o, wo_ref[...], preferred_element_type=F32)
    h2 = _layer_norm_rows(DN_ALPHA * h1 + att, g_ref[...], b_ref[...])
    for c in range(ROW_TILES):
        h2_ref[pl.ds(c, h2.shape[0], stride=ROW_TILES), :] = h2[:, c * LANES:(c + 1) * LANES]

    h2_hi = h2.astype(BF16)
    h2_lo = (h2 - h2_hi.astype(F32)).astype(BF16)
    hi_prod = jnp.dot(h2_hi, wr_ref[...], preferred_element_type=F32)
    lo_prod = jnp.dot(h2_lo, wr_ref[:, :LANES], preferred_element_type=F32)
    logits = hi_prod[:, :LANES] + (hi_prod[:, LANES:] + lo_prod) + br_ref[...]
    tm = logits.shape[0]
    lane = lax.broadcasted_iota(I32, (tm, LANES), 1)
    lane_f = lane.astype(F32)
    work = jnp.where(lane < N_EXPERTS, logits, -jnp.inf)
    vals, idxs = [], []
    for _ in range(TOP_K):
        mx = jnp.max(work, axis=1, keepdims=True)
        ix = jnp.min(jnp.where(work == mx, lane_f, float(LANES)), axis=1, keepdims=True)
        vals.append(mx)
        idxs.append(ix)
        work = jnp.where(lane_f == ix, -jnp.inf, work)
    es = [jnp.exp(vv - vals[0]) for vv in vals]
    tot = es[0] + es[1] + es[2] + es[3]
    idx_slab = jnp.zeros((tm, LANES), F32)
    gate_slab = jnp.zeros((tm, LANES), F32)
    for kk in range(TOP_K):
        idx_slab = jnp.where(lane == kk, idxs[kk], idx_slab)
        gate_slab = jnp.where(lane == kk, es[kk] / tot, gate_slab)
    idx_ref[...] = idx_slab.astype(I32)
    gate_ref[...] = gate_slab


def _cross_attention_router(h1, kv, bsz, s, w_cq, w_co, ln_g, ln_b, w_router, b_router, part, after):
    n_mem = kv.shape[0] // bsz
    bsz = bsz // MOE_PARTS
    b0 = part * bsz
    t = bsz * s
    tm = min(XATT_ROWS, s)
    ns = s // tm
    wr = jnp.concatenate([w_router, jnp.zeros((D_MODEL, LANES - N_EXPERTS), F32)], axis=1)
    wr_hi = wr.astype(BF16)
    wr = jnp.concatenate([wr_hi, (wr - wr_hi.astype(F32)).astype(BF16)], axis=1)
    br = jnp.concatenate([b_router, jnp.zeros((LANES - N_EXPERTS,), F32)]).reshape(1, LANES)
    row = lambda n: pl.BlockSpec((tm, n), lambda b, i: (b * ns + i, 0))
    h1_rows = pl.BlockSpec((tm, D_MODEL), lambda b, i: ((b0 + b) * ns + i, 0))
    in_specs = [h1_rows, pl.BlockSpec((n_mem, 2 * D_MODEL), lambda b, i: (b0 + b, 0)),
                _const_spec((D_MODEL, D_MODEL)), _const_spec((D_MODEL, D_MODEL)),
                _const_spec((1, D_MODEL)), _const_spec((1, D_MODEL)),
                _const_spec((D_MODEL, 2 * LANES)), _const_spec((1, LANES))]
    args = [h1, kv, w_cq.astype(BF16), w_co.astype(BF16), ln_g.reshape(1, -1), ln_b.reshape(1, -1), wr, br]
    if after is not None:
        in_specs.append(pl.BlockSpec(memory_space=pl.ANY))
        args.append(after)
    return pl.pallas_call(
        _xattn_kernel,
        out_shape=[jax.ShapeDtypeStruct((t * ROW_TILES, LANES), F32), jax.ShapeDtypeStruct((t, LANES), I32),
                   jax.ShapeDtypeStruct((t, LANES), F32)],
        grid=(bsz, ns),
        in_specs=in_specs,
        out_specs=[pl.BlockSpec((tm * ROW_TILES, LANES), lambda b, i: (b * ns + i, 0)), row(LANES), row(LANES)],
        compiler_params=_cparams(("parallel", "parallel")),
        name="cross_attention_router",
    )(*args)


def _rank_kernel(idx_ref, ltri_ref, rank_ref, cnt_ref, carry_s):
    @pl.when(pl.program_id(0) == 0)
    def _():
        carry_s[...] = jnp.zeros_like(carry_s)

    idx = idx_ref[...]
    tm = idx.shape[0]
    lane = lax.broadcasted_iota(I32, (tm, LANES), 1)
    onehots = [(lane == idx[:, kk:kk + 1]).astype(F32) for kk in range(TOP_K)]
    sel = onehots[0] + onehots[1] + onehots[2] + onehots[3]
    before = jnp.dot(ltri_ref[...], sel.astype(BF16), preferred_element_type=F32) + carry_s[0:1, :]
    rank_slab = jnp.zeros((tm, LANES), F32)
    for kk in range(TOP_K):
        r = jnp.sum(onehots[kk] * before, axis=1, keepdims=True)
        rank_slab = jnp.where(lane == kk, r, rank_slab)
    rank_ref[...] = rank_slab.astype(I32)
    carry_s[0:1, :] = carry_s[0:1, :] + jnp.sum(sel, axis=0, keepdims=True)
    cnt_ref[...] = jnp.broadcast_to(carry_s[0:1, :], cnt_ref.shape).astype(I32)


def _routing_ranks(idx_slab):
    t = idx_slab.shape[0]
    tm = min(RANK_ROWS, t)
    ltri = jnp.tril(jnp.ones((tm, tm), BF16), k=-1)
    return pl.pallas_call(
        _rank_kernel,
        out_shape=[jax.ShapeDtypeStruct((t, LANES), I32), jax.ShapeDtypeStruct((8, LANES), I32)],
        grid=(t // tm,),
        in_specs=[pl.BlockSpec((tm, LANES), lambda i: (i, 0)), _const_spec((tm, tm))],
        out_specs=[pl.BlockSpec((tm, LANES), lambda i: (i, 0)), _const_spec((8, LANES))],
        scratch_shapes=[pltpu.VMEM((8, LANES), F32)],
        compiler_params=_cparams(("arbitrary",)),
        name="routing_ranks",
    )(idx_slab, ltri)


def _dest_kernel(idx_ref, rank_ref, pstart_ref, dest_ref):
    idx = idx_ref[...]
    tm = idx.shape[0]
    lane = lax.broadcasted_iota(I32, (tm, LANES), 1)
    dest = rank_ref[...].astype(F32)
    pstart = pstart_ref[0:1, :].astype(F32)
    for kk in range(TOP_K):
        start = jnp.sum(jnp.where(lane == idx[:, kk:kk + 1], pstart, 0.0), axis=1, keepdims=True)
        dest = dest + jnp.where(lane == kk, start, 0.0)
    dest_ref[...] = dest.T[0:8, :].astype(I32)


def _dest_rows(idx_slab, rank_slab, pstart_row):
    t = rank_slab.shape[0]
    tm = min(RANK_ROWS, t)
    return pl.pallas_call(
        _dest_kernel,
        out_shape=jax.ShapeDtypeStruct((8, t), I32),
        grid=(t // tm,),
        in_specs=[pl.BlockSpec((tm, LANES), lambda i: (i, 0)), pl.BlockSpec((tm, LANES), lambda i: (i, 0)),
                  _const_spec((8, LANES))],
        out_specs=pl.BlockSpec((8, tm), lambda i: (0, i)),
        compiler_params=_cparams(("parallel",)),
        name="routing_dest",
    )(idx_slab, rank_slab, pstart_row)


def _sc_mesh():
    return plsc.VectorSubcoreMesh(core_axis_name="c", subcore_axis_name="s", num_cores=SC_CORES,
                                  num_subcores=SC_SUBCORES)


def _sc_worker():
    return lax.axis_index("s") * SC_CORES + lax.axis_index("c")


def _dispatch(h2, dest_kt, pad_rows, n_rows):
    t = dest_kt.shape[1]
    w = SC_ROWS
    per_w = t // SC_WORKERS
    steps = per_w // w
    assert steps % 2 == 0 and steps * w * SC_WORKERS == t
    pad_steps = pad_rows.shape[0] // (SC_WORKERS * w)
    dest_w = dest_kt.reshape(TOP_K, SC_WORKERS, steps, w)
    pad_w = pad_rows.reshape(SC_WORKERS, pad_steps, w)
    zeros = jnp.zeros((w, ROW_TILES, LANES), F32)

    def body(x_hbm, dest_hbm, pad_hbm, zeros_hbm, o_hbm, idx_v, pad_v, rows_v, sem_in, sem_out):
        wid = _sc_worker()
        tok0 = wid * per_w
        for kk in range(TOP_K):
            pltpu.sync_copy(dest_hbm.at[kk, wid], idx_v.at[pl.ds(kk * steps, steps)])
        pltpu.sync_copy(pad_hbm.at[wid], pad_v)
        pltpu.sync_copy(zeros_hbm, rows_v.at[0])
        for j in range(pad_steps):
            pltpu.sync_copy(rows_v.at[0], o_hbm.at[pad_v.at[j]])

        def load(step, b):
            return pltpu.make_async_copy(x_hbm.at[pl.ds(tok0 + step * w, w)], rows_v.at[b], sem_in.at[b])

        def scatter(step, kk, b):
            return pltpu.make_async_copy(rows_v.at[b], o_hbm.at[idx_v.at[kk * steps + step]], sem_out.at[b])

        load(0, 0).start()

        @pl.loop(0, steps, step=2)
        def _(s0):
            for b in range(2):
                step = s0 + b
                load(step, b).wait()

                @pl.when(step >= 1)
                def _():
                    for kk in range(TOP_K):
                        scatter(step - 1, kk, 1 - b).wait()

                @pl.when(step + 1 < steps)
                def _():
                    load(step + 1, 1 - b).start()

                for kk in range(TOP_K):
                    scatter(step, kk, b).start()

        for kk in range(TOP_K):
            scatter(steps - 1, kk, 1).wait()

    return pl.kernel(
        body,
        out_type=jax.ShapeDtypeStruct((n_rows, ROW_TILES, LANES), F32),
        mesh=_sc_mesh(),
        scratch_types=[pltpu.VMEM((steps * TOP_K, w), I32), pltpu.VMEM((pad_steps, w), I32),
                       pltpu.VMEM((2, w, ROW_TILES, LANES), F32), pltpu.SemaphoreType.DMA((2,)),
                       pltpu.SemaphoreType.DMA((2,))],
        name="moe_dispatch_sc",
    )(h2, dest_w, pad_w, zeros)


def _gather_rows(yb, dest_kt):
    n = dest_kt.shape[0]
    w = SC_ROWS
    per_w = n // SC_WORKERS
    steps = per_w // w
    assert steps % 2 == 0 and steps * w * SC_WORKERS == n
    dest_w = dest_kt.reshape(SC_WORKERS, steps, w)

    def body(y_hbm, dest_hbm, o_hbm, idx_v, rows_v, sem_g, sem_w):
        wid = _sc_worker()
        row0 = wid * per_w
        pltpu.sync_copy(dest_hbm.at[wid], idx_v)

        def gather(step, b):
            return pltpu.make_async_copy(y_hbm.at[idx_v.at[step]], rows_v.at[b], sem_g.at[b])

        def write(step, b):
            return pltpu.make_async_copy(rows_v.at[b], o_hbm.at[pl.ds(row0 + step * w, w)], sem_w.at[b])

        gather(0, 0).start()

        @pl.loop(0, steps, step=2)
        def _(s0):
            for b in range(2):
                step = s0 + b
                gather(step, b).wait()

                @pl.when(step >= 1)
                def _():
                    write(step - 1, 1 - b).wait()

                @pl.when(step + 1 < steps)
                def _():
                    gather(step + 1, 1 - b).start()

                write(step, b).start()

        write(steps - 1, 1).wait()

    return pl.kernel(
        body,
        out_type=jax.ShapeDtypeStruct((n, ROW_TILES, LANES), F32),
        mesh=_sc_mesh(),
        scratch_types=[pltpu.VMEM((steps, w), I32), pltpu.VMEM((2, w, ROW_TILES, LANES), F32),
                       pltpu.SemaphoreType.DMA((2,)), pltpu.SemaphoreType.DMA((2,))],
        name="moe_gather_sc",
    )(yb, dest_w)


def _rows_2d(ref, n):
    return jnp.concatenate([ref[pl.ds(c, n, stride=ROW_TILES), :] for c in range(ROW_TILES)], axis=1)


def _store_rows(ref, val):
    for c in range(ROW_TILES):
        ref[pl.ds(c, val.shape[0], stride=ROW_TILES), :] = val[:, c * LANES:(c + 1) * LANES]


def _expert_kernel(blk_e_ref, first_ref, slot_ref, next_ref, nblk_ref, x_ref, wg_hbm, wl_hbm, bg_ref, bl_ref, wd_hbm,
                   bd_ref, y_ref, wg_buf, wl_buf, wd_stage, wd_buf, sem):
    i = pl.program_id(0)

    def fetch(e, sl):
        return [pltpu.make_async_copy(src.at[e], dst.at[sl], sem.at[sl, j])
                for j, (src, dst) in enumerate(((wg_hbm, wg_buf), (wl_hbm, wl_buf), (wd_hbm, wd_stage)))]

    @pl.when(i < nblk_ref[0])
    def _():
        sl = slot_ref[i]

        @pl.when(first_ref[i] == 1)
        def _():
            @pl.when(i == 0)
            def _():
                for cp in fetch(blk_e_ref[0], sl):
                    cp.start()

            for cp in fetch(blk_e_ref[i], sl):
                cp.wait()
            wd_buf[sl] = wd_stage[sl].astype(BF16)

            @pl.when(next_ref[i] >= 0)
            def _():
                for cp in fetch(next_ref[i], 1 - sl):
                    cp.start()

        xb = _rows_2d(x_ref, y_ref.shape[0] // ROW_TILES).astype(BF16)
        glu = jnp.dot(xb, wg_buf[sl], preferred_element_type=F32) + bg_ref[...]
        lin = jnp.dot(xb, wl_buf[sl], preferred_element_type=F32) + bl_ref[...]
        glu = jnp.minimum(glu, SWIGLU_LIMIT)
        lin = jnp.clip(lin, -SWIGLU_LIMIT, SWIGLU_LIMIT)
        act = glu * jax.nn.sigmoid(SWIGLU_ALPHA * glu) * (lin + 1.0)
        _store_rows(y_ref, jnp.dot(act.astype(BF16), wd_buf[sl], preferred_element_type=F32) + bd_ref[...])

    @pl.when(i >= nblk_ref[0])
    def _():
        y_ref[...] = jnp.zeros_like(y_ref)


def _split_kernel(w_ref, perm_ref, after_ref, g_ref, l_ref):
    del after_ref
    half = g_ref.shape[-1]
    sorted_cols = jnp.dot(w_ref[...].astype(BF16), perm_ref[...], preferred_element_type=F32)
    g_ref[...] = sorted_cols[:, :half].astype(BF16)
    l_ref[...] = sorted_cols[:, half:].astype(BF16)


def _split_gate_up(w_gu, after):
    e, d, n2 = w_gu.shape
    cw = 512
    src = np.concatenate([np.arange(0, cw, 2), np.arange(1, cw, 2)])
    perm = jnp.asarray(np.eye(cw, dtype=np.float32)[:, src], BF16)
    return pl.pallas_call(
        _split_kernel,
        out_shape=[jax.ShapeDtypeStruct((e, d, n2 // 2), BF16), jax.ShapeDtypeStruct((e, d, n2 // 2), BF16)],
        grid=(e, n2 // cw),
        in_specs=[pl.BlockSpec((None, d, cw), lambda i, c: (i, 0, c)), _const_spec((cw, cw)),
                  pl.BlockSpec(memory_space=pl.ANY)],
        out_specs=[pl.BlockSpec((None, d, cw // 2), lambda i, c: (i, 0, c)),
                   pl.BlockSpec((None, d, cw // 2), lambda i, c: (i, 0, c))],
        compiler_params=_cparams(("parallel", "parallel")),
        name="split_gate_up",
    )(w_gu, perm, after)


def _experts(xb, blk_meta, w_glu, w_lin, b_glu, b_lin, w_dn, b_dn):
    n_rows = xb.shape[0] // ROW_TILES
    bm = EXPERT_ROWS
    vec = lambda n: pl.BlockSpec((None, 1, n), lambda i, be, *_: (be[i], 0, 0))
    rows = pl.BlockSpec((bm * ROW_TILES, LANES), lambda i, *_: (i, 0))
    hbm = pl.BlockSpec(memory_space=pl.ANY)
    grid_spec = pltpu.PrefetchScalarGridSpec(
        num_scalar_prefetch=5,
        grid=(n_rows // bm,),
        in_specs=[rows, hbm, hbm, vec(D_EXPERT), vec(D_EXPERT), hbm, vec(D_MODEL)],
        out_specs=rows,
        scratch_shapes=[pltpu.VMEM((2, D_MODEL, D_EXPERT), BF16), pltpu.VMEM((2, D_MODEL, D_EXPERT), BF16),
                        pltpu.VMEM((2, D_EXPERT, D_MODEL), F32), pltpu.VMEM((2, D_EXPERT, D_MODEL), BF16),
                        pltpu.SemaphoreType.DMA((2, 3))],
    )
    return pl.pallas_call(
        _expert_kernel,
        out_shape=jax.ShapeDtypeStruct((n_rows * ROW_TILES, LANES), F32),
        grid_spec=grid_spec,
        compiler_params=_cparams(("arbitrary",)),
        name="moe_experts",
    )(*blk_meta, xb, w_glu, w_lin, b_glu, b_lin, w_dn, b_dn)


def _combine_kernel(h_ref, gate_ref, y0_ref, y1_ref, y2_ref, y3_ref, g_ref, b_ref, *rest):
    o_ref = rest[-1]
    tt = o_ref.shape[0]
    gates = gate_ref[...]
    acc = DN_ALPHA * _rows_2d(h_ref, tt)
    for kk, y_ref in enumerate((y0_ref, y1_ref, y2_ref, y3_ref)):
        acc = acc + gates[:, kk:kk + 1] * _rows_2d(y_ref, tt)
    o_ref[...] = _layer_norm_rows(acc, g_ref[...], b_ref[...])


def _combine(h2, gate_slab, yg, ln_g, ln_b, part, prev_out):
    t = h2.shape[0] // ROW_TILES
    t_all = t * MOE_PARTS
    tt = min(COMBINE_TOKENS, t)
    steps = t // tt
    first = part * steps
    row = lambda m: pl.BlockSpec((tt, m), lambda i: (i, 0))
    tiles = lambda kk: pl.BlockSpec((tt * ROW_TILES, LANES), lambda i: (kk * steps + i, 0))
    in_specs = [tiles(0), row(LANES),
                tiles(0), tiles(1), tiles(2), tiles(3), _const_spec((1, D_MODEL)), _const_spec((1, D_MODEL))]
    args = [h2, gate_slab, yg, yg, yg, yg, ln_g.reshape(1, -1), ln_b.reshape(1, -1)]
    aliases = {}
    if prev_out is not None:
        in_specs.append(pl.BlockSpec(memory_space=pl.ANY))
        args.append(prev_out)
        aliases = {len(args) - 1: 0}
    return pl.pallas_call(
        _combine_kernel,
        out_shape=jax.ShapeDtypeStruct((t_all, D_MODEL), F32),
        grid=(steps,),
        in_specs=in_specs,
        out_specs=pl.BlockSpec((tt, D_MODEL), lambda i: (first + i, 0)),
        input_output_aliases=aliases,
        compiler_params=_cparams(("parallel",)),
        name="moe_combine",
    )(*args)


def _route_and_dispatch(h2, idx_slab):
    t = h2.shape[0] // ROW_TILES
    bm = EXPERT_ROWS
    n_rows = t * TOP_K + N_EXPERTS * bm
    n_blocks = n_rows // bm

    rank_slab, cnt = _routing_ranks(idx_slab)
    counts = cnt[0, :N_EXPERTS]
    padded = (counts + bm - 1) // bm * bm
    pends = jnp.cumsum(padded)
    pstarts = pends - padded
    pstart_row = jnp.zeros((8, LANES), I32).at[:, :N_EXPERTS].set(pstarts[None, :])
    blk_start = jnp.arange(n_blocks, dtype=I32) * bm
    blk_e = jnp.minimum(jnp.sum((pends[None, :] <= blk_start[:, None]).astype(I32), axis=1), N_EXPERTS - 1)
    nblk = (pends[-1] // bm).astype(I32).reshape(1)
    active = counts > 0
    ordinal = jnp.cumsum(active.astype(I32)) - 1
    eid = jnp.arange(N_EXPERTS, dtype=I32)
    later = active[None, :] & (eid[None, :] > eid[:, None])
    next_active = jnp.where(later.any(axis=1), jnp.argmax(later, axis=1).astype(I32), -1)
    blk_first = jnp.concatenate([jnp.ones((1,), I32), (blk_e[1:] != blk_e[:-1]).astype(I32)])
    blk_onehot = (blk_e[:, None] == eid[None, :]).astype(I32)
    blk_slot = jnp.sum(blk_onehot * ordinal[None, :], axis=1) % 2
    blk_next = jnp.sum(blk_onehot * next_active[None, :], axis=1)
    blk_meta = (blk_e, blk_first, blk_slot, blk_next, nblk)
    dest_kt = _dest_rows(idx_slab, rank_slab, pstart_row)[:TOP_K]
    pad_off = jnp.arange(bm, dtype=I32)[None, :]
    pad_rows = jnp.where(pad_off < (padded - counts)[:, None], (pstarts + counts)[:, None] + pad_off,
                         n_rows - 1).reshape(-1)
    xb = _dispatch(h2.reshape(t, ROW_TILES, LANES), dest_kt, pad_rows, n_rows)
    return xb.reshape(n_rows * ROW_TILES, LANES), blk_meta, dest_kt


def _xattn_moe(h1, kv, bsz, s, w_cq, w_co, ln2_g, ln2_b, w_router, b_router, w_gu, b_gu, w_dn, b_dn, ln3_g, ln3_b):
    parts = []
    after = None
    for part in range(MOE_PARTS):
        h2, idx_slab, gate_slab = _cross_attention_router(h1, kv, bsz, s, w_cq, w_co, ln2_g, ln2_b, w_router,
                                                          b_router, part, after)
        parts.append((h2, gate_slab) + _route_and_dispatch(h2, idx_slab))
        after = parts[-1][-1]
    w_glu, w_lin = _split_gate_up(w_gu, after)
    expert_w = (w_glu, w_lin, b_gu[:, None, 0::2], b_gu[:, None, 1::2], w_dn, b_dn[:, None, :])
    out = None
    for part, (h2, gate_slab, xb, blk_meta, dest_kt) in enumerate(parts):
        n_rows = xb.shape[0] // ROW_TILES
        yb = _experts(xb, blk_meta, *expert_w)
        yg = _gather_rows(yb.reshape(n_rows, ROW_TILES, LANES), dest_kt.reshape(-1))
        out = _combine(h2, gate_slab, yg.reshape(-1, LANES), ln3_g, ln3_b, part, out)
    return out


def kernel(x, mem, positions, w_in, b_igate, b_fgate, conv_w, conv_b, w_mq, w_mk, g_mhead, w_mskip, g_qlat, g_kvlat,
           w_uq, w_ukv, w_br_m, w_br_a, w_mix_out, ln1_g, ln1_b, w_cq, w_ckv, w_co, ln2_g, ln2_b, w_router, b_router,
           w_gu, b_gu, w_dn, b_dn, ln3_g, ln3_b):
    bsz, s, d = x.shape
    t = bsz * s
    h = x.reshape(t, d)
    pos2 = positions.reshape(t, 1)
    for l in range(DEPTH):
        q, k, v, xm, vm, op, gm, ga, gate = _input_projection(h, pos2, w_in[l], g_qlat[l], g_kvlat[l], w_uq[l],
                                                              w_ukv[l])
        ym = _mlstm_branch(xm, vm, op, gate, bsz, s, b_igate[l], b_fgate[l], conv_w[l], conv_b[l], w_mq[l], w_mk[l],
                           g_mhead[l], w_mskip[l])
        oa = _mla_attention(q, k, v, bsz, s)
        h1 = _merge(ym, oa, gm, ga, h, w_br_m[l], w_br_a[l], w_mix_out[l], ln1_g[l], ln1_b[l])
        kv = _mem_kv(mem.reshape(-1, d), w_ckv[l])
        h = _xattn_moe(h1, kv, bsz, s, w_cq[l], w_co[l], ln2_g[l], ln2_b[l], w_router[l], b_router[l], w_gu[l],
                       b_gu[l], w_dn[l], b_dn[l], ln3_g[l], ln3_b[l])
    return h.reshape(bsz, s, d)
```

```python
import functools

import numpy as np
import jax
import jax.numpy as jnp
from jax import lax
from jax.experimental import pallas as pl
from jax.experimental.pallas import tpu as pltpu
from jax.experimental.pallas import tpu_sc as plsc

F32 = jnp.float32
BF16 = jnp.bfloat16
I32 = jnp.int32

D_MODEL = 1024
N_MEM = 256
M_HEADS = 4
M_HEAD_DIM = 128
M_WIDTH = M_HEADS * M_HEAD_DIM
M_CONV = 4
A_HEADS = 8
A_NOPE = 64
A_ROPE = 32
A_QK = A_NOPE + A_ROPE
A_VDIM = 64
A_Q_RANK = 256
A_KV_RANK = 128
ROPE_THETA = 10000.0
X_HEADS = 4
X_HEAD_DIM = D_MODEL // X_HEADS
N_EXPERTS = 32
TOP_K = 4
D_EXPERT = D_MODEL
SWIGLU_ALPHA = 1.702
SWIGLU_LIMIT = 7.0
DEPTH = 1
DN_ALPHA = (2.0 * DEPTH) ** 0.25
EPS = 1e-5
IN_SPLITS = (A_Q_RANK, A_KV_RANK, A_ROPE, M_WIDTH, M_WIDTH, M_WIDTH, M_HEADS, M_HEADS, D_MODEL, D_MODEL)
IN_OFFSETS = tuple(int(v) for v in np.cumsum((0,) + IN_SPLITS))

LANES = 128
VMEM_LIMIT = 56 * 1024 * 1024

PROJ_ROWS = 512
M_CHUNK_ROWS = 128
MLSTM_SEQS = 2
ATT_Q = 256
MERGE_ROWS = 512
XATT_ROWS = 512
RANK_ROWS = 1024
EXPERT_ROWS = 256
COMBINE_TOKENS = 256
MOE_PARTS = 2
SC_CORES = 2
SC_SUBCORES = 16
SC_WORKERS = SC_CORES * SC_SUBCORES
SC_ROWS = 32
SC_SUB = 8
NEG_BIG = -1e30
LOG2_E = 1.4426950408889634
ROW_TILES = D_MODEL // LANES

C_QLAT = 0
C_KVLAT = C_QLAT + A_Q_RANK
C_KR = C_KVLAT + A_KV_RANK
C_KRS = C_KR + LANES
C_GATE = C_KRS + LANES
C_XM = C_GATE + LANES
C_VM = C_XM + M_WIDTH
C_OP = C_VM + M_WIDTH
C_GM = C_OP + M_WIDTH
C_GA = C_GM + D_MODEL
C_END = C_GA + D_MODEL


def _cparams(sem, vmem=VMEM_LIMIT):
    return pltpu.CompilerParams(dimension_semantics=sem, vmem_limit_bytes=vmem)


def _const_spec(shape):
    nd = len(shape)
    return pl.BlockSpec(shape, lambda *a: (0,) * nd)


def _layer_norm_rows(v, g, b):
    mu = jnp.mean(v, axis=-1, keepdims=True)
    d = v - mu
    var = jnp.mean(d * d, axis=-1, keepdims=True)
    return d * lax.rsqrt(var + EPS) * g + b


def _proj_kernel(x_ref, pos_ref, w_ref, gq_ref, gkv_ref, wuq_ref, wuqs_ref, wuk_ref, wuv_ref, invf_ref,
                 q_ref, k_ref, v_ref, xm_ref, vm_ref, op_ref, gm_ref, ga_ref, gate_ref):
    xb = x_ref[...].astype(BF16)

    def mm(lo, n):
        return jnp.dot(xb, w_ref[:, lo:lo + n], preferred_element_type=F32)

    xm_ref[...] = mm(C_XM, M_WIDTH).astype(BF16)
    vm_ref[...] = mm(C_VM, M_WIDTH).astype(BF16)
    op_ref[...] = mm(C_OP, M_WIDTH).astype(BF16)
    gm_ref[...] = mm(C_GM, D_MODEL).astype(BF16)
    ga_ref[...] = mm(C_GA, D_MODEL).astype(BF16)
    gate_ref[...] = mm(C_GATE, LANES)

    ang = pos_ref[...].astype(F32) * invf_ref[...]
    cos = jnp.cos(ang)
    sin = jnp.sin(ang)

    q_lat = mm(C_QLAT, A_Q_RANK)
    qn = (q_lat * lax.rsqrt(jnp.mean(q_lat * q_lat, axis=-1, keepdims=True) + EPS) * gq_ref[...]).astype(BF16)
    q = jnp.dot(qn, wuq_ref[...], preferred_element_type=F32)
    qs = jnp.dot(qn, wuqs_ref[...], preferred_element_type=F32)
    kv_lat = mm(C_KVLAT, A_KV_RANK)
    kvn = (kv_lat * lax.rsqrt(jnp.mean(kv_lat * kv_lat, axis=-1, keepdims=True) + EPS) * gkv_ref[...]).astype(BF16)
    kn = jnp.dot(kvn, wuk_ref[...], preferred_element_type=F32)
    lane = lax.broadcasted_iota(I32, (1, A_HEADS * LANES), 1)
    ones_lane = (lane % LANES == A_VDIM).astype(F32)
    v_ref[...] = (jnp.dot(kvn, wuv_ref[...], preferred_element_type=F32) + ones_lane).astype(BF16)
    k_pe = mm(C_KR, LANES) * cos + mm(C_KRS, LANES) * sin
    scale = A_QK ** -0.5 * LOG2_E
    for h in range(A_HEADS):
        sl = slice(h * LANES, (h + 1) * LANES)
        q_ref[:, sl] = ((q[:, sl] * cos + qs[:, sl] * sin) * scale).astype(BF16)
        k_ref[:, sl] = (kn[:, sl] + k_pe).astype(BF16)


def _proj_weights(w_in, w_uq, w_ukv):
    o = IN_OFFSETS
    half = A_ROPE // 2
    w_q, w_kv, w_kr = w_in[:, o[0]:o[1]], w_in[:, o[1]:o[2]], w_in[:, o[2]:o[3]]
    w_xm, w_vm, w_op = w_in[:, o[3]:o[4]], w_in[:, o[4]:o[5]], w_in[:, o[5]:o[6]]
    w_i, w_f, w_gm, w_ga = w_in[:, o[6]:o[7]], w_in[:, o[7]:o[8]], w_in[:, o[8]:o[9]], w_in[:, o[9]:o[10]]
    d = w_in.shape[0]
    z = lambda n: jnp.zeros((d, n), w_in.dtype)
    kr = jnp.concatenate([z(A_NOPE), w_kr, z(LANES - A_QK)], axis=1)
    krs = jnp.concatenate([z(A_NOPE), -w_kr[:, half:], w_kr[:, :half], z(LANES - A_QK)], axis=1)
    gate = jnp.concatenate([w_i, w_f, z(LANES - 2 * M_HEADS)], axis=1)
    w_r = jnp.concatenate([w_q, w_kv, kr, krs, gate, w_xm, w_vm, w_op, w_gm, w_ga], axis=1).astype(BF16)

    uq = w_uq.reshape(A_Q_RANK, A_HEADS, A_QK)
    zq = jnp.zeros((A_Q_RANK, A_HEADS, LANES - A_QK), w_uq.dtype)
    zn = jnp.zeros((A_Q_RANK, A_HEADS, A_NOPE), w_uq.dtype)
    uq_pad = jnp.concatenate([uq, zq], axis=-1).reshape(A_Q_RANK, A_HEADS * LANES).astype(BF16)
    uqs_pad = jnp.concatenate([zn, -uq[..., A_NOPE + half:], uq[..., A_NOPE:A_NOPE + half], zq],
                              axis=-1).reshape(A_Q_RANK, A_HEADS * LANES).astype(BF16)
    ukv = w_ukv.reshape(A_KV_RANK, A_HEADS, A_NOPE + A_VDIM)
    zk = jnp.zeros((A_KV_RANK, A_HEADS, LANES - A_NOPE), w_ukv.dtype)
    uk_pad = jnp.concatenate([ukv[..., :A_NOPE], zk], axis=-1).reshape(A_KV_RANK, A_HEADS * LANES).astype(BF16)
    zv = jnp.zeros((A_KV_RANK, A_HEADS, LANES - A_VDIM), w_ukv.dtype)
    uv_pad = jnp.concatenate([ukv[..., A_NOPE:], zv], axis=-1).reshape(A_KV_RANK, A_HEADS * LANES).astype(BF16)
    return w_r, uq_pad, uqs_pad, uk_pad, uv_pad


def _input_projection(x2, pos2, w_in, g_qlat, g_kvlat, w_uq, w_ukv):
    t = x2.shape[0]
    tm = min(PROJ_ROWS, t)
    w_r, uq_pad, uqs_pad, uk_pad, uv_pad = _proj_weights(w_in, w_uq, w_ukv)
    half = A_ROPE // 2
    inv_freq = ROPE_THETA ** (-jnp.arange(half, dtype=F32) / half)
    invf = jnp.concatenate([jnp.zeros((A_NOPE,), F32), inv_freq, inv_freq,
                            jnp.zeros((LANES - A_QK,), F32)]).reshape(1, LANES)
    hw = A_HEADS * LANES
    row = lambda n: pl.BlockSpec((tm, n), lambda i: (i, 0))
    outs = [
        jax.ShapeDtypeStruct((t, hw), BF16), jax.ShapeDtypeStruct((t, hw), BF16), jax.ShapeDtypeStruct((t, hw), BF16),
        jax.ShapeDtypeStruct((t, M_WIDTH), BF16), jax.ShapeDtypeStruct((t, M_WIDTH), BF16),
        jax.ShapeDtypeStruct((t, M_WIDTH), BF16),
        jax.ShapeDtypeStruct((t, D_MODEL), BF16), jax.ShapeDtypeStruct((t, D_MODEL), BF16),
        jax.ShapeDtypeStruct((t, LANES), F32),
    ]
    return pl.pallas_call(
        _proj_kernel,
        out_shape=outs,
        grid=(t // tm,),
        in_specs=[row(D_MODEL), row(1), _const_spec(w_r.shape), _const_spec((1, A_Q_RANK)),
                  _const_spec((1, A_KV_RANK)), _const_spec(uq_pad.shape), _const_spec(uqs_pad.shape),
                  _const_spec(uk_pad.shape), _const_spec(uv_pad.shape), _const_spec((1, LANES))],
        out_specs=[row(hw), row(hw), row(hw), row(M_WIDTH), row(M_WIDTH), row(M_WIDTH), row(D_MODEL), row(D_MODEL),
                   row(LANES)],
        compiler_params=_cparams(("parallel",)),
        name="input_projection",
    )(x2, pos2, w_r, g_qlat.reshape(1, -1), g_kvlat.reshape(1, -1), uq_pad, uqs_pad, uk_pad, uv_pad, invf)


def _log_sigmoid(v):
    return jnp.minimum(v, 0.0) - jnp.log1p(jnp.exp(-jnp.abs(v)))


def _mlstm_kernel(xm_ref, vm_ref, op_ref, gate_ref, convw_ref, convb_ref, wqk_ref, gbias_ref, ghead_ref,
                  skip_ref, tril_ref, out_ref, xpad_s, ctn_s, m_s, *, seq_len):
    s = seq_len
    lc = M_CHUNK_ROWS
    halo = 8
    for bb in range(MLSTM_SEQS):
        xpad_s[bb, 0:halo, :] = jnp.zeros((halo, M_WIDTH), F32)
        xpad_s[bb, halo:, :] = xm_ref[bb * s:(bb + 1) * s, :].astype(F32)
    ctn_s[...] = jnp.zeros_like(ctn_s)
    m_s[...] = jnp.zeros_like(m_s)
    rows = lax.broadcasted_iota(I32, (lc, lc), 0)
    cols = lax.broadcasted_iota(I32, (lc, lc), 1)
    causal = rows >= cols
    row_id = lax.broadcasted_iota(I32, (lc, LANES), 0)
    ones_blk = (lax.broadcasted_iota(I32, (lc, LANES), 1) == 0).astype(BF16)
    kscale = M_HEAD_DIM ** -0.5
    hsl = [slice(h * M_HEAD_DIM, (h + 1) * M_HEAD_DIM) for h in range(M_HEADS)]
    hcl = [slice(h, h + 1) for h in range(M_HEADS)]
    streams = [(bb, h) for bb in range(MLSTM_SEQS) for h in range(M_HEADS)]

    def gates(bb, r0):
        win = xpad_s[bb, pl.ds(r0 - bb * s, lc + halo), :]
        conv = convb_ref[...]
        for j in range(M_CONV):
            lo = halo - (M_CONV - 1) + j
            conv = conv + win[lo:lo + lc, :] * convw_ref[j:j + 1, :]
        xc = conv * jax.nn.sigmoid(conv)
        g = gate_ref[pl.ds(r0, lc), :] + gbias_ref[...]
        ls = _log_sigmoid(g)
        bc = jnp.dot(tril_ref[...], ls, preferred_element_type=F32, precision=lax.Precision.HIGHEST)
        b0 = pltpu.roll(bc, LANES - M_HEADS, axis=1)
        u = g - b0
        cm = u
        shift = 1
        while shift < lc:
            cm = jnp.maximum(cm, jnp.where(row_id >= shift, pltpu.roll(cm, shift, axis=0), -jnp.inf))
            shift *= 2
        m_prev = m_s[bb:bb + 1, :]
        g_inter = b0 + m_prev
        m = jnp.maximum(g_inter, b0 + cm)
        b_tot = b0[lc - 1:lc, :]
        aw = b_tot - b0 + g
        m_chunk = jnp.max(aw, axis=0, keepdims=True)
        m_new = jnp.maximum(b_tot + m_prev, m_chunk)
        m_s[bb:bb + 1, :] = m_new
        return dict(xc=xc, w_inter=jnp.exp(g_inter - m), e_negm=jnp.exp(-m), a_mat=b0 - m, wa=jnp.exp(aw - m_chunk),
                    s_old=jnp.exp(b_tot + m_prev - m_new), s_new=jnp.exp(m_chunk - m_new), u_t=u.T)

    def chunk(c, carry):
        r0s = [pl.multiple_of(bb * s + c * lc, lc) for bb in range(MLSTM_SEQS)]
        gs = [gates(bb, r0s[bb]) for bb in range(MLSTM_SEQS)]
        qk = [jnp.dot(gs[bb]["xc"].astype(BF16), wqk_ref[...], preferred_element_type=F32)
              for bb in range(MLSTM_SEQS)]
        q_b = [qk[bb][:, hsl[h]].astype(BF16) for bb, h in streams]
        k_f = [qk[bb][:, M_WIDTH + h * M_HEAD_DIM:M_WIDTH + (h + 1) * M_HEAD_DIM] * kscale for bb, h in streams]
        v_aug = [jnp.concatenate([vm_ref[pl.ds(r0s[bb], lc), hsl[h]], ones_blk], axis=1) for bb, h in streams]
        ctn_prev = [ctn_s[bb * M_HEADS + h] for bb, h in streams]
        s_raw = [lax.dot_general(q_b[i], k_f[i].astype(BF16), (((1,), (1,)), ((), ())), preferred_element_type=F32)
                 for i in range(len(streams))]
        inter = [jnp.dot(q_b[i], ctn_prev[i].astype(BF16), preferred_element_type=F32)
                 for i in range(len(streams))]
        ctn_c = [lax.dot_general((k_f[i] * gs[bb]["wa"][:, hcl[h]]).astype(BF16), v_aug[i], (((0,), (0,)), ((), ())),
                                 preferred_element_type=F32) for i, (bb, h) in enumerate(streams)]
        sc_b = [(s_raw[i] * jnp.exp(jnp.where(causal, gs[bb]["a_mat"][:, hcl[h]] + gs[bb]["u_t"][hcl[h], :],
                                              -jnp.inf))).astype(BF16) for i, (bb, h) in enumerate(streams)]
        intra = [jnp.dot(sc_b[i], v_aug[i], preferred_element_type=F32) for i in range(len(streams))]
        for i, (bb, h) in enumerate(streams):
            g = gs[bb]
            ctn_s[bb * M_HEADS + h] = g["s_old"][:, hcl[h]] * ctn_prev[i] + g["s_new"][:, hcl[h]] * ctn_c[i]
            wi = g["w_inter"][:, hcl[h]]
            num = wi * inter[i][:, :M_HEAD_DIM] + intra[i][:, :M_HEAD_DIM]
            den = wi * inter[i][:, M_HEAD_DIM:M_HEAD_DIM + 1] + intra[i][:, M_HEAD_DIM:M_HEAD_DIM + 1]
            hh = num / jnp.maximum(jnp.abs(den), g["e_negm"][:, hcl[h]])
            mu = jnp.mean(hh, axis=-1, keepdims=True)
            dv = hh - mu
            var = jnp.mean(dv * dv, axis=-1, keepdims=True)
            hn = dv * lax.rsqrt(var + EPS) * ghead_ref[:, hsl[h]]
            y = (jax.nn.sigmoid(op_ref[pl.ds(r0s[bb], lc), hsl[h]].astype(F32))
                 * (hn + skip_ref[:, hsl[h]] * g["xc"][:, hsl[h]]))
            out_ref[pl.ds(r0s[bb], lc), hsl[h]] = y.astype(BF16)
        return carry

    lax.fori_loop(0, s // lc, chunk, 0)


def _mlstm_branch(xm, vm, op, gate, bsz, s, b_i, b_f, conv_w, conv_b, w_mq, w_mk, g_mhead, w_mskip):
    lc = M_CHUNK_ROWS
    gbias = jnp.concatenate([b_i, b_f, jnp.zeros((LANES - 2 * M_HEADS,), F32)]).reshape(1, LANES)
    tril = jnp.tril(jnp.ones((lc, lc), F32))
    eye = jnp.eye(M_HEADS, dtype=F32)
    block_diag = lambda w: jnp.einsum('hde,hg->hdge', w, eye).reshape(M_WIDTH, M_WIDTH)
    w_qk = jnp.concatenate([block_diag(w_mq), block_diag(w_mk)], axis=1).astype(BF16)
    nseq = MLSTM_SEQS
    assert bsz % nseq == 0
    seq = lambda n: pl.BlockSpec((nseq * s, n), lambda b: (b, 0))
    return pl.pallas_call(
        functools.partial(_mlstm_kernel, seq_len=s),
        out_shape=jax.ShapeDtypeStruct((bsz * s, M_WIDTH), BF16),
        grid=(bsz // nseq,),
        in_specs=[seq(M_WIDTH), seq(M_WIDTH), seq(M_WIDTH), seq(LANES),
                  _const_spec((M_CONV, M_WIDTH)), _const_spec((1, M_WIDTH)),
                  _const_spec((M_WIDTH, 2 * M_WIDTH)),
                  _const_spec((1, LANES)), _const_spec((1, M_WIDTH)), _const_spec((1, M_WIDTH)),
                  _const_spec((lc, lc))],
        out_specs=seq(M_WIDTH),
        scratch_shapes=[pltpu.VMEM((nseq, s + 8, M_WIDTH), F32),
                        pltpu.VMEM((nseq * M_HEADS, M_HEAD_DIM, 2 * LANES), F32), pltpu.VMEM((8, LANES), F32)],
        compiler_params=_cparams(("parallel",)),
        name="mlstm_branch",
    )(xm, vm, op, gate, conv_w, conv_b.reshape(1, -1), w_qk, gbias, g_mhead.reshape(1, -1), w_mskip.reshape(1, -1),
      tril)


def _attn_kernel(q_ref, k_ref, v_ref, o_ref, s_scr, m_s, acc_s):
    tq = q_ref.shape[0]
    i = pl.program_id(1)
    rows = lax.broadcasted_iota(I32, (tq, tq), 0)
    cols = lax.broadcasted_iota(I32, (tq, tq), 1)

    def lane_tile_max(sc):
        out = sc[:, :LANES]
        for t in range(1, tq // LANES):
            out = jnp.maximum(out, sc[:, t * LANES:(t + 1) * LANES])
        return out

    heads = [slice(h * LANES, (h + 1) * LANES) for h in range(A_HEADS)]
    m_s[...] = jnp.full(m_s.shape, NEG_BIG, F32)
    acc_s[...] = jnp.zeros_like(acc_s)

    def pass_a(j, diagonal):
        k0 = pl.multiple_of(j * tq, tq)
        for h, hs in enumerate(heads):
            sc = lax.dot_general(q_ref[:, hs], k_ref[pl.ds(k0, tq), hs], (((1,), (1,)), ((), ())),
                                 preferred_element_type=F32)
            if diagonal:
                sc = jnp.where(cols <= rows, sc, NEG_BIG)
            s_scr[h, j] = sc
            m_s[h] = jnp.maximum(m_s[h], lane_tile_max(sc))

    def for_blocks(n, fn):
        def pair(p, c):
            fn(2 * p)
            fn(2 * p + 1)
            return c

        lax.fori_loop(0, n // 2, pair, 0)

        @pl.when(n % 2 == 1)
        def _():
            fn(n - 1)

    for_blocks(i, lambda j: pass_a(j, False))
    pass_a(i, True)
    for h in range(A_HEADS):
        m_s[h] = jnp.broadcast_to(jnp.max(m_s[h], axis=1, keepdims=True), (tq, LANES))

    def pass_b(j):
        k0 = pl.multiple_of(j * tq, tq)
        for h, hs in enumerate(heads):
            m_row = m_s[h]
            p = jnp.exp2(s_scr[h, j] - jnp.concatenate([m_row] * (tq // LANES), axis=1)).astype(BF16)
            acc_s[h] += jnp.dot(p, v_ref[pl.ds(k0, tq), hs], preferred_element_type=F32)

    for_blocks(i + 1, pass_b)
    for h, hs in enumerate(heads):
        acc = acc_s[h]
        o_ref[:, hs] = (acc / acc[:, A_VDIM:A_VDIM + 1]).astype(BF16)


def _mla_attention(q, k, v, bsz, s):
    tq = min(ATT_Q, s)
    nq = s // tq
    hw = A_HEADS * LANES
    return pl.pallas_call(
        _attn_kernel,
        out_shape=jax.ShapeDtypeStruct(q.shape, BF16),
        grid=(bsz, nq),
        in_specs=[pl.BlockSpec((tq, hw), lambda b, i: (b * nq + i, 0)),
                  pl.BlockSpec((s, hw), lambda b, i: (b, 0)),
                  pl.BlockSpec((s, hw), lambda b, i: (b, 0))],
        out_specs=pl.BlockSpec((tq, hw), lambda b, i: (b * nq + i, 0)),
        scratch_shapes=[pltpu.VMEM((A_HEADS, nq, tq, tq), F32), pltpu.VMEM((A_HEADS, tq, LANES), F32),
                        pltpu.VMEM((A_HEADS, tq, LANES), F32)],
        compiler_params=_cparams(("parallel", "arbitrary")),
        name="mla_attention",
    )(q, k, v)


def _merge_kernel(ym_ref, oa_ref, gm_ref, ga_ref, x_ref, wbm_ref, wba_ref, wmix_ref, g_ref, b_ref, h_ref):
    y_m = jnp.dot(ym_ref[...], wbm_ref[...], preferred_element_type=F32)
    y_a = jnp.dot(oa_ref[...], wba_ref[...], preferred_element_type=F32)
    mixed = jax.nn.sigmoid(gm_ref[...].astype(F32)) * y_m + jax.nn.sigmoid(ga_ref[...].astype(F32)) * y_a
    mix = jnp.dot(mixed.astype(BF16), wmix_ref[...], preferred_element_type=F32)
    h_ref[...] = _layer_norm_rows(DN_ALPHA * x_ref[...] + mix, g_ref[...], b_ref[...])


def _merge(ym, oa, gm, ga, x2, w_br_m, w_br_a, w_mix_out, ln_g, ln_b):
    t = x2.shape[0]
    tm = min(MERGE_ROWS, t)
    wba = jnp.concatenate([w_br_a.reshape(A_HEADS, A_VDIM, D_MODEL),
                           jnp.zeros((A_HEADS, LANES - A_VDIM, D_MODEL), w_br_a.dtype)], axis=1)
    wba = wba.reshape(A_HEADS * LANES, D_MODEL).astype(BF16)
    row = lambda n: pl.BlockSpec((tm, n), lambda i: (i, 0))
    return pl.pallas_call(
        _merge_kernel,
        out_shape=jax.ShapeDtypeStruct((t, D_MODEL), F32),
        grid=(t // tm,),
        in_specs=[row(M_WIDTH), row(A_HEADS * LANES), row(D_MODEL), row(D_MODEL), row(D_MODEL),
                  _const_spec((M_WIDTH, D_MODEL)), _const_spec((A_HEADS * LANES, D_MODEL)),
                  _const_spec((D_MODEL, D_MODEL)), _const_spec((1, D_MODEL)), _const_spec((1, D_MODEL))],
        out_specs=row(D_MODEL),
        compiler_params=_cparams(("parallel",)),
        name="merge_deepnorm1",
    )(ym, oa, gm, ga, x2, w_br_m.astype(BF16), wba, w_mix_out.astype(BF16), ln_g.reshape(1, -1), ln_b.reshape(1, -1))


def _kv_kernel(m_ref, w_ref, o_ref):
    o_ref[...] = jnp.dot(m_ref[...].astype(BF16), w_ref[...], preferred_element_type=F32).astype(BF16)


def _mem_kv(mem2, w_ckv):
    r = mem2.shape[0]
    tm = min(512, r)
    return pl.pallas_call(
        _kv_kernel,
        out_shape=jax.ShapeDtypeStruct((r, 2 * D_MODEL), BF16),
        grid=(r // tm,),
        in_specs=[pl.BlockSpec((tm, D_MODEL), lambda i: (i, 0)), _const_spec((D_MODEL, 2 * D_MODEL))],
        out_specs=pl.BlockSpec((tm, 2 * D_MODEL), lambda i: (i, 0)),
        compiler_params=_cparams(("parallel",)),
        name="memory_kv",
    )(mem2, w_ckv.astype(BF16))


def _xattn_kernel(h_ref, kv_ref, wq_ref, wo_ref, g_ref, b_ref, wr_ref, br_ref, *rest):
    h2_ref, idx_ref, gate_ref = rest[-3:]
    h1 = h_ref[...]
    q = jnp.dot(h1.astype(BF16), wq_ref[...], preferred_element_type=F32).astype(BF16)
    scale = X_HEAD_DIM ** -0.5
    ksl = [slice(hd * X_HEAD_DIM, (hd + 1) * X_HEAD_DIM) for hd in range(X_HEADS)]
    vsl = [slice(D_MODEL + hd * X_HEAD_DIM, D_MODEL + (hd + 1) * X_HEAD_DIM) for hd in range(X_HEADS)]
    scs = [lax.dot_general(q[:, ks], kv_ref[:, ks], (((1,), (1,)), ((), ())), preferred_element_type=F32) * scale
           for ks in ksl]
    ps = [jnp.exp(sc - jnp.max(sc, axis=1, keepdims=True)) for sc in scs]
    ps = [(p / jnp.sum(p, axis=1, keepdims=True)).astype(BF16) for p in ps]
    o = jnp.concatenate([jnp.dot(p, kv_ref[:, vs], preferred_element_type=F32).astype(BF16)
                         for p, vs in zip(ps, vsl)], axis=1)
    att = jnp.dot(o, wo_ref[...], preferred_element_type=F32)
    h2 = _layer_norm_rows(DN_ALPHA * h1 + att, g_ref[...], b_ref[...])
    for c in range(ROW_TILES):
        h2_ref[pl.ds(c, h2.shape[0], stride=ROW_TILES), :] = h2[:, c * LANES:(c + 1) * LANES]

    h2_hi = h2.astype(BF16)
    h2_lo = (h2 - h2_hi.astype(F32)).astype(BF16)
    hi_prod = jnp.dot(h2_hi, wr_ref[...], preferred_element_type=F32)
    lo_prod = jnp.dot(h2_lo, wr_ref[:, :LANES], preferred_element_type=F32)
    logits = hi_prod[:, :LANES] + (hi_prod[:, LANES:] + lo_prod) + br_ref[...]
    tm = logits.shape[0]
    lane = lax.broadcasted_iota(I32, (tm, LANES), 1)
    lane_f = lane.astype(F32)
    work = jnp.where(lane < N_EXPERTS, logits, -jnp.inf)
    vals, idxs = [], []
    for _ in range(TOP_K):
        mx = jnp.max(work, axis=1, keepdims=True)
        ix = jnp.min(jnp.where(work == mx, lane_f, float(LANES)), axis=1, keepdims=True)
        vals.append(mx)
        idxs.append(ix)
        work = jnp.where(lane_f == ix, -jnp.inf, work)
    es = [jnp.exp(vv - vals[0]) for vv in vals]
    tot = es[0] + es[1] + es[2] + es[3]
    idx_slab = jnp.zeros((tm, LANES), F32)
    gate_slab = jnp.zeros((tm, LANES), F32)
    for kk in range(TOP_K):
        idx_slab = jnp.where(lane == kk, idxs[kk], idx_slab)
        gate_slab = jnp.where(lane == kk, es[kk] / tot, gate_slab)
    idx_ref[...] = idx_slab.astype(I32)
    gate_ref[...] = gate_slab


def _cross_attention_router(h1, kv, bsz, s, w_cq, w_co, ln_g, ln_b, w_router, b_router, part, after):
    n_mem = kv.shape[0] // bsz
    bsz = bsz // MOE_PARTS
    b0 = part * bsz
    t = bsz * s
    tm = min(XATT_ROWS, s)
    ns = s // tm
    wr = jnp.concatenate([w_router, jnp.zeros((D_MODEL, LANES - N_EXPERTS), F32)], axis=1)
    wr_hi = wr.astype(BF16)
    wr = jnp.concatenate([wr_hi, (wr - wr_hi.astype(F32)).astype(BF16)], axis=1)
    br = jnp.concatenate([b_router, jnp.zeros((LANES - N_EXPERTS,), F32)]).reshape(1, LANES)
    row = lambda n: pl.BlockSpec((tm, n), lambda b, i: (b * ns + i, 0))
    h1_rows = pl.BlockSpec((tm, D_MODEL), lambda b, i: ((b0 + b) * ns + i, 0))
    in_specs = [h1_rows, pl.BlockSpec((n_mem, 2 * D_MODEL), lambda b, i: (b0 + b, 0)),
                _const_spec((D_MODEL, D_MODEL)), _const_spec((D_MODEL, D_MODEL)),
                _const_spec((1, D_MODEL)), _const_spec((1, D_MODEL)),
                _const_spec((D_MODEL, 2 * LANES)), _const_spec((1, LANES))]
    args = [h1, kv, w_cq.astype(BF16), w_co.astype(BF16), ln_g.reshape(1, -1), ln_b.reshape(1, -1), wr, br]
    if after is not None:
        in_specs.append(pl.BlockSpec(memory_space=pl.ANY))
        args.append(after)
    return pl.pallas_call(
        _xattn_kernel,
        out_shape=[jax.ShapeDtypeStruct((t * ROW_TILES, LANES), F32), jax.ShapeDtypeStruct((t, LANES), I32),
                   jax.ShapeDtypeStruct((t, LANES), F32)],
        grid=(bsz, ns),
        in_specs=in_specs,
        out_specs=[pl.BlockSpec((tm * ROW_TILES, LANES), lambda b, i: (b * ns + i, 0)), row(LANES), row(LANES)],
        compiler_params=_cparams(("parallel", "parallel")),
        name="cross_attention_router",
    )(*args)


def _rank_kernel(idx_ref, ltri_ref, rank_ref, cnt_ref, carry_s):
    @pl.when(pl.program_id(0) == 0)
    def _():
        carry_s[...] = jnp.zeros_like(carry_s)

    idx = idx_ref[...]
    tm = idx.shape[0]
    lane = lax.broadcasted_iota(I32, (tm, LANES), 1)
    onehots = [(lane == idx[:, kk:kk + 1]).astype(F32) for kk in range(TOP_K)]
    sel = onehots[0] + onehots[1] + onehots[2] + onehots[3]
    before = jnp.dot(ltri_ref[...], sel.astype(BF16), preferred_element_type=F32) + carry_s[0:1, :]
    rank_slab = jnp.zeros((tm, LANES), F32)
    for kk in range(TOP_K):
        r = jnp.sum(onehots[kk] * before, axis=1, keepdims=True)
        rank_slab = jnp.where(lane == kk, r, rank_slab)
    rank_ref[...] = rank_slab.astype(I32)
    carry_s[0:1, :] = carry_s[0:1, :] + jnp.sum(sel, axis=0, keepdims=True)
    cnt_ref[...] = jnp.broadcast_to(carry_s[0:1, :], cnt_ref.shape).astype(I32)


def _routing_ranks(idx_slab):
    t = idx_slab.shape[0]
    tm = min(RANK_ROWS, t)
    ltri = jnp.tril(jnp.ones((tm, tm), BF16), k=-1)
    return pl.pallas_call(
        _rank_kernel,
        out_shape=[jax.ShapeDtypeStruct((t, LANES), I32), jax.ShapeDtypeStruct((8, LANES), I32)],
        grid=(t // tm,),
        in_specs=[pl.BlockSpec((tm, LANES), lambda i: (i, 0)), _const_spec((tm, tm))],
        out_specs=[pl.BlockSpec((tm, LANES), lambda i: (i, 0)), _const_spec((8, LANES))],
        scratch_shapes=[pltpu.VMEM((8, LANES), F32)],
        compiler_params=_cparams(("arbitrary",)),
        name="routing_ranks",
    )(idx_slab, ltri)


def _dest_kernel(idx_ref, rank_ref, pstart_ref, dest_ref):
    idx = idx_ref[...]
    tm = idx.shape[0]
    lane = lax.broadcasted_iota(I32, (tm, LANES), 1)
    dest = rank_ref[...].astype(F32)
    pstart = pstart_ref[0:1, :].astype(F32)
    for kk in range(TOP_K):
        start = jnp.sum(jnp.where(lane == idx[:, kk:kk + 1], pstart, 0.0), axis=1, keepdims=True)
        dest = dest + jnp.where(lane == kk, start, 0.0)
    dest_ref[...] = dest.T[0:8, :].astype(I32)


def _dest_rows(idx_slab, rank_slab, pstart_row):
    t = rank_slab.shape[0]
    tm = min(RANK_ROWS, t)
    return pl.pallas_call(
        _dest_kernel,
        out_shape=jax.ShapeDtypeStruct((8, t), I32),
        grid=(t // tm,),
        in_specs=[pl.BlockSpec((tm, LANES), lambda i: (i, 0)), pl.BlockSpec((tm, LANES), lambda i: (i, 0)),
                  _const_spec((8, LANES))],
        out_specs=pl.BlockSpec((8, tm), lambda i: (0, i)),
        compiler_params=_cparams(("parallel",)),
        name="routing_dest",
    )(idx_slab, rank_slab, pstart_row)


def _sc_mesh():
    return plsc.VectorSubcoreMesh(core_axis_name="c", subcore_axis_name="s", num_cores=SC_CORES,
                                  num_subcores=SC_SUBCORES)


def _sc_worker():
    return lax.axis_index("s") * SC_CORES + lax.axis_index("c")


def _dispatch(h2, dest_kt, pad_rows, n_rows):
    t = dest_kt.shape[1]
    w, sub = SC_ROWS, SC_SUB
    nsub = w // sub
    per_w = t // SC_WORKERS
    steps = per_w // w
    assert steps % 2 == 0 and steps * w * SC_WORKERS == t
    pad_n = pad_rows.shape[0] // (SC_WORKERS * sub)
    dest_w = dest_kt.reshape(TOP_K, SC_WORKERS, steps * nsub, sub)
    pad_w = pad_rows.reshape(SC_WORKERS, pad_n, sub)
    zeros = jnp.zeros((sub, ROW_TILES, LANES), F32)

    def body(x_hbm, dest_hbm, pad_hbm, zeros_hbm, o_hbm, idx_v, pad_v, rows_v, zero_v, sem_in, sem_out, sem_pad):
        wid = _sc_worker()
        tok0 = wid * per_w
        for kk in range(TOP_K):
            pltpu.sync_copy(dest_hbm.at[kk, wid], idx_v.at[pl.ds(kk * steps * nsub, steps * nsub)])
        pltpu.sync_copy(pad_hbm.at[wid], pad_v)
        pltpu.sync_copy(zeros_hbm, zero_v)

        def pad_scatter(j):
            return pltpu.make_async_copy(zero_v, o_hbm.at[pad_v.at[j]], sem_pad)

        for j in range(pad_n):
            pad_scatter(j).start()

        def load(step, b):
            return pltpu.make_async_copy(x_hbm.at[pl.ds(tok0 + step * w, w)], rows_v.at[b], sem_in.at[b])

        def scatters(step, b):
            return [pltpu.make_async_copy(rows_v.at[b, pl.ds(c * sub, sub)],
                                          o_hbm.at[idx_v.at[(kk * steps + step) * nsub + c]], sem_out.at[b])
                    for kk in range(TOP_K) for c in range(nsub)]

        load(0, 0).start()

        @pl.loop(0, steps, step=2)
        def _(s0):
            for b in range(2):
                step = s0 + b
                load(step, b).wait()

                @pl.when(step >= 1)
                def _():
                    for cp in scatters(step - 1, 1 - b):
                        cp.wait()

                @pl.when(step + 1 < steps)
                def _():
                    load(step + 1, 1 - b).start()

                for cp in scatters(step, b):
                    cp.start()

        for cp in scatters(steps - 1, 1):
            cp.wait()
        for j in range(pad_n):
            pad_scatter(j).wait()

    return pl.kernel(
        body,
        out_type=jax.ShapeDtypeStruct((n_rows, ROW_TILES, LANES), F32),
        mesh=_sc_mesh(),
        scratch_types=[pltpu.VMEM((TOP_K * steps * nsub, sub), I32), pltpu.VMEM((pad_n, sub), I32),
                       pltpu.VMEM((2, w, ROW_TILES, LANES), F32), pltpu.VMEM((sub, ROW_TILES, LANES), F32),
                       pltpu.SemaphoreType.DMA((2,)), pltpu.SemaphoreType.DMA((2,)), pltpu.SemaphoreType.DMA],
        name="moe_dispatch_sc",
    )(h2, dest_w, pad_w, zeros)


def _gather_rows(yb, dest_kt):
    n = dest_kt.shape[0]
    w, sub = SC_ROWS, SC_SUB
    nsub = w // sub
    per_w = n // SC_WORKERS
    steps = per_w // w
    assert steps % 2 == 0 and steps * w * SC_WORKERS == n
    dest_w = dest_kt.reshape(SC_WORKERS, steps * nsub, sub)

    def body(y_hbm, dest_hbm, o_hbm, idx_v, rows_v, sem_g, sem_w):
        wid = _sc_worker()
        row0 = wid * per_w
        pltpu.sync_copy(dest_hbm.at[wid], idx_v)

        def gathers(step, b):
            return [pltpu.make_async_copy(y_hbm.at[idx_v.at[step * nsub + c]], rows_v.at[b, pl.ds(c * sub, sub)],
                                          sem_g.at[b]) for c in range(nsub)]

        def write(step, b):
            return pltpu.make_async_copy(rows_v.at[b], o_hbm.at[pl.ds(row0 + step * w, w)], sem_w.at[b])

        for cp in gathers(0, 0):
            cp.start()

        @pl.loop(0, steps, step=2)
        def _(s0):
            for b in range(2):
                step = s0 + b
                for cp in gathers(step, b):
                    cp.wait()

                @pl.when(step >= 1)
                def _():
                    write(step - 1, 1 - b).wait()

                @pl.when(step + 1 < steps)
                def _():
                    for cp in gathers(step + 1, 1 - b):
                        cp.start()

                write(step, b).start()

        write(steps - 1, 1).wait()

    return pl.kernel(
        body,
        out_type=jax.ShapeDtypeStruct((n, ROW_TILES, LANES), F32),
        mesh=_sc_mesh(),
        scratch_types=[pltpu.VMEM((steps * nsub, sub), I32), pltpu.VMEM((2, w, ROW_TILES, LANES), F32),
                       pltpu.SemaphoreType.DMA((2,)), pltpu.SemaphoreType.DMA((2,))],
        name="moe_gather_sc",
    )(yb, dest_w)


def _rows_2d(ref, n):
    return jnp.concatenate([ref[pl.ds(c, n, stride=ROW_TILES), :] for c in range(ROW_TILES)], axis=1)


def _store_rows(ref, val):
    for c in range(ROW_TILES):
        ref[pl.ds(c, val.shape[0], stride=ROW_TILES), :] = val[:, c * LANES:(c + 1) * LANES]


def _expert_kernel(blk_e_ref, first_ref, slot_ref, next_ref, nblk_ref, x_ref, wg_hbm, wl_hbm, bg_ref, bl_ref, wd_hbm,
                   bd_ref, y_ref, wg_buf, wl_buf, wd_stage, wd_buf, sem):
    i = pl.program_id(0)

    def fetch(e, sl):
        return [pltpu.make_async_copy(src.at[e], dst.at[sl], sem.at[sl, j])
                for j, (src, dst) in enumerate(((wg_hbm, wg_buf), (wl_hbm, wl_buf), (wd_hbm, wd_stage)))]

    @pl.when(i < nblk_ref[0])
    def _():
        sl = slot_ref[i]

        @pl.when(first_ref[i] == 1)
        def _():
            @pl.when(i == 0)
            def _():
                for cp in fetch(blk_e_ref[0], sl):
                    cp.start()

            for cp in fetch(blk_e_ref[i], sl):
                cp.wait()
            wd_buf[sl] = wd_stage[sl].astype(BF16)

            @pl.when(next_ref[i] >= 0)
            def _():
                for cp in fetch(next_ref[i], 1 - sl):
                    cp.start()

        xb = _rows_2d(x_ref, y_ref.shape[0] // ROW_TILES).astype(BF16)
        glu = jnp.dot(xb, wg_buf[sl], preferred_element_type=F32) + bg_ref[...]
        lin = jnp.dot(xb, wl_buf[sl], preferred_element_type=F32) + bl_ref[...]
        glu = jnp.minimum(glu, SWIGLU_LIMIT)
        lin = jnp.clip(lin, -SWIGLU_LIMIT, SWIGLU_LIMIT)
        act = glu * jax.nn.sigmoid(SWIGLU_ALPHA * glu) * (lin + 1.0)
        _store_rows(y_ref, jnp.dot(act.astype(BF16), wd_buf[sl], preferred_element_type=F32) + bd_ref[...])

    @pl.when(i >= nblk_ref[0])
    def _():
        y_ref[...] = jnp.zeros_like(y_ref)


def _split_kernel(w_ref, perm_ref, after_ref, g_ref, l_ref):
    del after_ref
    half = g_ref.shape[-1]
    sorted_cols = jnp.dot(w_ref[...].astype(BF16), perm_ref[...], preferred_element_type=F32)
    g_ref[...] = sorted_cols[:, :half].astype(BF16)
    l_ref[...] = sorted_cols[:, half:].astype(BF16)


def _split_gate_up(w_gu, after):
    e, d, n2 = w_gu.shape
    cw = 512
    src = np.concatenate([np.arange(0, cw, 2), np.arange(1, cw, 2)])
    perm = jnp.asarray(np.eye(cw, dtype=np.float32)[:, src], BF16)
    return pl.pallas_call(
        _split_kernel,
        out_shape=[jax.ShapeDtypeStruct((e, d, n2 // 2), BF16), jax.ShapeDtypeStruct((e, d, n2 // 2), BF16)],
        grid=(e, n2 // cw),
        in_specs=[pl.BlockSpec((None, d, cw), lambda i, c: (i, 0, c)), _const_spec((cw, cw)),
                  pl.BlockSpec(memory_space=pl.ANY)],
        out_specs=[pl.BlockSpec((None, d, cw // 2), lambda i, c: (i, 0, c)),
                   pl.BlockSpec((None, d, cw // 2), lambda i, c: (i, 0, c))],
        compiler_params=_cparams(("parallel", "parallel")),
        name="split_gate_up",
    )(w_gu, perm, after)


def _experts(xb, blk_meta, w_glu, w_lin, b_glu, b_lin, w_dn, b_dn):
    n_rows = xb.shape[0] // ROW_TILES
    bm = EXPERT_ROWS
    vec = lambda n: pl.BlockSpec((None, 1, n), lambda i, be, *_: (be[i], 0, 0))
    rows = pl.BlockSpec((bm * ROW_TILES, LANES), lambda i, *_: (i, 0))
    hbm = pl.BlockSpec(memory_space=pl.ANY)
    grid_spec = pltpu.PrefetchScalarGridSpec(
        num_scalar_prefetch=5,
        grid=(n_rows // bm,),
        in_specs=[rows, hbm, hbm, vec(D_EXPERT), vec(D_EXPERT), hbm, vec(D_MODEL)],
        out_specs=rows,
        scratch_shapes=[pltpu.VMEM((2, D_MODEL, D_EXPERT), BF16), pltpu.VMEM((2, D_MODEL, D_EXPERT), BF16),
                        pltpu.VMEM((2, D_EXPERT, D_MODEL), F32), pltpu.VMEM((2, D_EXPERT, D_MODEL), BF16),
                        pltpu.SemaphoreType.DMA((2, 3))],
    )
    return pl.pallas_call(
        _expert_kernel,
        out_shape=jax.ShapeDtypeStruct((n_rows * ROW_TILES, LANES), F32),
        grid_spec=grid_spec,
        compiler_params=_cparams(("arbitrary",)),
        name="moe_experts",
    )(*blk_meta, xb, w_glu, w_lin, b_glu, b_lin, w_dn, b_dn)


def _combine_kernel(h_ref, gate_ref, y0_ref, y1_ref, y2_ref, y3_ref, g_ref, b_ref, *rest):
    o_ref = rest[-1]
    tt = o_ref.shape[0]
    gates = gate_ref[...]
    acc = DN_ALPHA * _rows_2d(h_ref, tt)
    for kk, y_ref in enumerate((y0_ref, y1_ref, y2_ref, y3_ref)):
        acc = acc + gates[:, kk:kk + 1] * _rows_2d(y_ref, tt)
    o_ref[...] = _layer_norm_rows(acc, g_ref[...], b_ref[...])


def _combine(h2, gate_slab, yg, ln_g, ln_b, part, prev_out):
    t = h2.shape[0] // ROW_TILES
    t_all = t * MOE_PARTS
    tt = min(COMBINE_TOKENS, t)
    steps = t // tt
    first = part * steps
    row = lambda m: pl.BlockSpec((tt, m), lambda i: (i, 0))
    tiles = lambda kk: pl.BlockSpec((tt * ROW_TILES, LANES), lambda i: (kk * steps + i, 0))
    in_specs = [tiles(0), row(LANES),
                tiles(0), tiles(1), tiles(2), tiles(3), _const_spec((1, D_MODEL)), _const_spec((1, D_MODEL))]
    args = [h2, gate_slab, yg, yg, yg, yg, ln_g.reshape(1, -1), ln_b.reshape(1, -1)]
    aliases = {}
    if prev_out is not None:
        in_specs.append(pl.BlockSpec(memory_space=pl.ANY))
        args.append(prev_out)
        aliases = {len(args) - 1: 0}
    return pl.pallas_call(
        _combine_kernel,
        out_shape=jax.ShapeDtypeStruct((t_all, D_MODEL), F32),
        grid=(steps,),
        in_specs=in_specs,
        out_specs=pl.BlockSpec((tt, D_MODEL), lambda i: (first + i, 0)),
        input_output_aliases=aliases,
        compiler_params=_cparams(("parallel",)),
        name="moe_combine",
    )(*args)


def _route_and_dispatch(h2, idx_slab):
    t = h2.shape[0] // ROW_TILES
    bm = EXPERT_ROWS
    n_rows = t * TOP_K + N_EXPERTS * bm
    n_blocks = n_rows // bm

    rank_slab, cnt = _routing_ranks(idx_slab)
    counts = cnt[0, :N_EXPERTS]
    padded = (counts + bm - 1) // bm * bm
    pends = jnp.cumsum(padded)
    pstarts = pends - padded
    pstart_row = jnp.zeros((8, LANES), I32).at[:, :N_EXPERTS].set(pstarts[None, :])
    blk_start = jnp.arange(n_blocks, dtype=I32) * bm
    blk_e = jnp.minimum(jnp.sum((pends[None, :] <= blk_start[:, None]).astype(I32), axis=1), N_EXPERTS - 1)
    nblk = (pends[-1] // bm).astype(I32).reshape(1)
    active = counts > 0
    ordinal = jnp.cumsum(active.astype(I32)) - 1
    eid = jnp.arange(N_EXPERTS, dtype=I32)
    later = active[None, :] & (eid[None, :] > eid[:, None])
    next_active = jnp.where(later.any(axis=1), jnp.argmax(later, axis=1).astype(I32), -1)
    blk_first = jnp.concatenate([jnp.ones((1,), I32), (blk_e[1:] != blk_e[:-1]).astype(I32)])
    blk_onehot = (blk_e[:, None] == eid[None, :]).astype(I32)
    blk_slot = jnp.sum(blk_onehot * ordinal[None, :], axis=1) % 2
    blk_next = jnp.sum(blk_onehot * next_active[None, :], axis=1)
    blk_meta = (blk_e, blk_first, blk_slot, blk_next, nblk)
    dest_kt = _dest_rows(idx_slab, rank_slab, pstart_row)[:TOP_K]
    pad_off = jnp.arange(bm, dtype=I32)[None, :]
    pad_rows = jnp.where(pad_off < (padded - counts)[:, None], (pstarts + counts)[:, None] + pad_off,
                         n_rows - 1).reshape(-1)
    xb = _dispatch(h2.reshape(t, ROW_TILES, LANES), dest_kt, pad_rows, n_rows)
    return xb.reshape(n_rows * ROW_TILES, LANES), blk_meta, dest_kt


def _xattn_moe(h1, kv, bsz, s, w_cq, w_co, ln2_g, ln2_b, w_router, b_router, w_gu, b_gu, w_dn, b_dn, ln3_g, ln3_b):
    parts = []
    after = None
    for part in range(MOE_PARTS):
        h2, idx_slab, gate_slab = _cross_attention_router(h1, kv, bsz, s, w_cq, w_co, ln2_g, ln2_b, w_router,
                                                          b_router, part, after)
        parts.append((h2, gate_slab) + _route_and_dispatch(h2, idx_slab))
        after = parts[-1][-1]
    w_glu, w_lin = _split_gate_up(w_gu, after)
    expert_w = (w_glu, w_lin, b_gu[:, None, 0::2], b_gu[:, None, 1::2], w_dn, b_dn[:, None, :])
    out = None
    for part, (h2, gate_slab, xb, blk_meta, dest_kt) in enumerate(parts):
        n_rows = xb.shape[0] // ROW_TILES
        yb = _experts(xb, blk_meta, *expert_w)
        yg = _gather_rows(yb.reshape(n_rows, ROW_TILES, LANES), dest_kt.reshape(-1))
        out = _combine(h2, gate_slab, yg.reshape(-1, LANES), ln3_g, ln3_b, part, out)
    return out


def kernel(x, mem, positions, w_in, b_igate, b_fgate, conv_w, conv_b, w_mq, w_mk, g_mhead, w_mskip, g_qlat, g_kvlat,
           w_uq, w_ukv, w_br_m, w_br_a, w_mix_out, ln1_g, ln1_b, w_cq, w_ckv, w_co, ln2_g, ln2_b, w_router, b_router,
           w_gu, b_gu, w_dn, b_dn, ln3_g, ln3_b):
    bsz, s, d = x.shape
    t = bsz * s
    h = x.reshape(t, d)
    pos2 = positions.reshape(t, 1)
    for l in range(DEPTH):
        q, k, v, xm, vm, op, gm, ga, gate = _input_projection(h, pos2, w_in[l], g_qlat[l], g_kvlat[l], w_uq[l],
                                                              w_ukv[l])
        ym = _mlstm_branch(xm, vm, op, gate, bsz, s, b_igate[l], b_fgate[l], conv_w[l], conv_b[l], w_mq[l], w_mk[l],
                           g_mhead[l], w_mskip[l])
        oa = _mla_attention(q, k, v, bsz, s)
        h1 = _merge(ym, oa, gm, ga, h, w_br_m[l], w_br_a[l], w_mix_out[l], ln1_g[l], ln1_b[l])
        kv = _mem_kv(mem.reshape(-1, d), w_ckv[l])
        h = _xattn_moe(h1, kv, bsz, s, w_cq[l], w_co[l], ln2_g[l], ln2_b[l], w_router[l], b_router[l], w_gu[l],
                       b_gu[l], w_dn[l], b_dn[l], ln3_g[l], ln3_b[l])
    return h.reshape(bsz, s, d)
```

```python
import functools

import numpy as np
import jax
import jax.numpy as jnp
from jax import lax
from jax.experimental import pallas as pl
from jax.experimental.pallas import tpu as pltpu
from jax.experimental.pallas import tpu_sc as plsc

F32 = jnp.float32
BF16 = jnp.bfloat16
I32 = jnp.int32

D_MODEL = 1024
N_MEM = 256
M_HEADS = 4
M_HEAD_DIM = 128
M_WIDTH = M_HEADS * M_HEAD_DIM
M_CONV = 4
A_HEADS = 8
A_NOPE = 64
A_ROPE = 32
A_QK = A_NOPE + A_ROPE
A_VDIM = 64
A_Q_RANK = 256
A_KV_RANK = 128
ROPE_THETA = 10000.0
X_HEADS = 4
X_HEAD_DIM = D_MODEL // X_HEADS
N_EXPERTS = 32
TOP_K = 4
D_EXPERT = D_MODEL
SWIGLU_ALPHA = 1.702
SWIGLU_LIMIT = 7.0
DEPTH = 1
DN_ALPHA = (2.0 * DEPTH) ** 0.25
EPS = 1e-5
IN_SPLITS = (A_Q_RANK, A_KV_RANK, A_ROPE, M_WIDTH, M_WIDTH, M_WIDTH, M_HEADS, M_HEADS, D_MODEL, D_MODEL)
IN_OFFSETS = tuple(int(v) for v in np.cumsum((0,) + IN_SPLITS))

LANES = 128
VMEM_LIMIT = 56 * 1024 * 1024

PROJ_ROWS = 512
M_CHUNK_ROWS = 128
MLSTM_SEQS = 2
ATT_Q = 256
MERGE_ROWS = 512
XATT_ROWS = 512
RANK_ROWS = 1024
EXPERT_ROWS = 256
COMBINE_TOKENS = 256
MOE_PARTS = 2
SC_CORES = 2
SC_SUBCORES = 16
SC_WORKERS = SC_CORES * SC_SUBCORES
SC_ROWS = 32
SC_SUB = 8
NEG_BIG = -1e30
LOG2_E = 1.4426950408889634
ROW_TILES = D_MODEL // LANES

C_QLAT = 0
C_KVLAT = C_QLAT + A_Q_RANK
C_KR = C_KVLAT + A_KV_RANK
C_KRS = C_KR + LANES
C_GATE = C_KRS + LANES
C_XM = C_GATE + LANES
C_VM = C_XM + M_WIDTH
C_OP = C_VM + M_WIDTH
C_GM = C_OP + M_WIDTH
C_GA = C_GM + D_MODEL
C_END = C_GA + D_MODEL


def _cparams(sem, vmem=VMEM_LIMIT):
    return pltpu.CompilerParams(dimension_semantics=sem, vmem_limit_bytes=vmem)


def _const_spec(shape):
    nd = len(shape)
    return pl.BlockSpec(shape, lambda *a: (0,) * nd)


def _layer_norm_rows(v, g, b):
    mu = jnp.mean(v, axis=-1, keepdims=True)
    d = v - mu
    var = jnp.mean(d * d, axis=-1, keepdims=True)
    return d * lax.rsqrt(var + EPS) * g + b


def _proj_kernel(x_ref, pos_ref, w_ref, gq_ref, gkv_ref, wuq_ref, wuqs_ref, wuk_ref, wuv_ref, invf_ref,
                 q_ref, k_ref, v_ref, xm_ref, vm_ref, op_ref, gm_ref, ga_ref, gate_ref):
    xb = x_ref[...].astype(BF16)

    def mm(lo, n):
        return jnp.dot(xb, w_ref[:, lo:lo + n], preferred_element_type=F32)

    xm_ref[...] = mm(C_XM, M_WIDTH).astype(BF16)
    vm_ref[...] = mm(C_VM, M_WIDTH).astype(BF16)
    op_ref[...] = mm(C_OP, M_WIDTH).astype(BF16)
    gm_ref[...] = mm(C_GM, D_MODEL).astype(BF16)
    ga_ref[...] = mm(C_GA, D_MODEL).astype(BF16)
    gate_ref[...] = mm(C_GATE, LANES)

    ang = pos_ref[...].astype(F32) * invf_ref[...]
    cos = jnp.cos(ang)
    sin = jnp.sin(ang)

    q_lat = mm(C_QLAT, A_Q_RANK)
    qn = (q_lat * lax.rsqrt(jnp.mean(q_lat * q_lat, axis=-1, keepdims=True) + EPS) * gq_ref[...]).astype(BF16)
    q = jnp.dot(qn, wuq_ref[...], preferred_element_type=F32)
    qs = jnp.dot(qn, wuqs_ref[...], preferred_element_type=F32)
    kv_lat = mm(C_KVLAT, A_KV_RANK)
    kvn = (kv_lat * lax.rsqrt(jnp.mean(kv_lat * kv_lat, axis=-1, keepdims=True) + EPS) * gkv_ref[...]).astype(BF16)
    kn = jnp.dot(kvn, wuk_ref[...], preferred_element_type=F32)
    lane = lax.broadcasted_iota(I32, (1, A_HEADS * LANES), 1)
    ones_lane = (lane % LANES == A_VDIM).astype(F32)
    v_ref[...] = (jnp.dot(kvn, wuv_ref[...], preferred_element_type=F32) + ones_lane).astype(BF16)
    k_pe = mm(C_KR, LANES) * cos + mm(C_KRS, LANES) * sin
    scale = A_QK ** -0.5 * LOG2_E
    for h in range(A_HEADS):
        sl = slice(h * LANES, (h + 1) * LANES)
        q_ref[:, sl] = ((q[:, sl] * cos + qs[:, sl] * sin) * scale).astype(BF16)
        k_ref[:, sl] = (kn[:, sl] + k_pe).astype(BF16)


def _proj_weights(w_in, w_uq, w_ukv):
    o = IN_OFFSETS
    half = A_ROPE // 2
    w_q, w_kv, w_kr = w_in[:, o[0]:o[1]], w_in[:, o[1]:o[2]], w_in[:, o[2]:o[3]]
    w_xm, w_vm, w_op = w_in[:, o[3]:o[4]], w_in[:, o[4]:o[5]], w_in[:, o[5]:o[6]]
    w_i, w_f, w_gm, w_ga = w_in[:, o[6]:o[7]], w_in[:, o[7]:o[8]], w_in[:, o[8]:o[9]], w_in[:, o[9]:o[10]]
    d = w_in.shape[0]
    z = lambda n: jnp.zeros((d, n), w_in.dtype)
    kr = jnp.concatenate([z(A_NOPE), w_kr, z(LANES - A_QK)], axis=1)
    krs = jnp.concatenate([z(A_NOPE), -w_kr[:, half:], w_kr[:, :half], z(LANES - A_QK)], axis=1)
    gate = jnp.concatenate([w_i, w_f, z(LANES - 2 * M_HEADS)], axis=1)
    w_r = jnp.concatenate([w_q, w_kv, kr, krs, gate, w_xm, w_vm, w_op, w_gm, w_ga], axis=1).astype(BF16)

    uq = w_uq.reshape(A_Q_RANK, A_HEADS, A_QK)
    zq = jnp.zeros((A_Q_RANK, A_HEADS, LANES - A_QK), w_uq.dtype)
    zn = jnp.zeros((A_Q_RANK, A_HEADS, A_NOPE), w_uq.dtype)
    uq_pad = jnp.concatenate([uq, zq], axis=-1).reshape(A_Q_RANK, A_HEADS * LANES).astype(BF16)
    uqs_pad = jnp.concatenate([zn, -uq[..., A_NOPE + half:], uq[..., A_NOPE:A_NOPE + half], zq],
                              axis=-1).reshape(A_Q_RANK, A_HEADS * LANES).astype(BF16)
    ukv = w_ukv.reshape(A_KV_RANK, A_HEADS, A_NOPE + A_VDIM)
    zk = jnp.zeros((A_KV_RANK, A_HEADS, LANES - A_NOPE), w_ukv.dtype)
    uk_pad = jnp.concatenate([ukv[..., :A_NOPE], zk], axis=-1).reshape(A_KV_RANK, A_HEADS * LANES).astype(BF16)
    zv = jnp.zeros((A_KV_RANK, A_HEADS, LANES - A_VDIM), w_ukv.dtype)
    uv_pad = jnp.concatenate([ukv[..., A_NOPE:], zv], axis=-1).reshape(A_KV_RANK, A_HEADS * LANES).astype(BF16)
    return w_r, uq_pad, uqs_pad, uk_pad, uv_pad


def _input_projection(x2, pos2, w_in, g_qlat, g_kvlat, w_uq, w_ukv):
    t = x2.shape[0]
    tm = min(PROJ_ROWS, t)
    w_r, uq_pad, uqs_pad, uk_pad, uv_pad = _proj_weights(w_in, w_uq, w_ukv)
    half = A_ROPE // 2
    inv_freq = ROPE_THETA ** (-jnp.arange(half, dtype=F32) / half)
    invf = jnp.concatenate([jnp.zeros((A_NOPE,), F32), inv_freq, inv_freq,
                            jnp.zeros((LANES - A_QK,), F32)]).reshape(1, LANES)
    hw = A_HEADS * LANES
    row = lambda n: pl.BlockSpec((tm, n), lambda i: (i, 0))
    outs = [
        jax.ShapeDtypeStruct((t, hw), BF16), jax.ShapeDtypeStruct((t, hw), BF16), jax.ShapeDtypeStruct((t, hw), BF16),
        jax.ShapeDtypeStruct((t, M_WIDTH), BF16), jax.ShapeDtypeStruct((t, M_WIDTH), BF16),
        jax.ShapeDtypeStruct((t, M_WIDTH), BF16),
        jax.ShapeDtypeStruct((t, D_MODEL), BF16), jax.ShapeDtypeStruct((t, D_MODEL), BF16),
        jax.ShapeDtypeStruct((t, LANES), F32),
    ]
    return pl.pallas_call(
        _proj_kernel,
        out_shape=outs,
        grid=(t // tm,),
        in_specs=[row(D_MODEL), row(1), _const_spec(w_r.shape), _const_spec((1, A_Q_RANK)),
                  _const_spec((1, A_KV_RANK)), _const_spec(uq_pad.shape), _const_spec(uqs_pad.shape),
                  _const_spec(uk_pad.shape), _const_spec(uv_pad.shape), _const_spec((1, LANES))],
        out_specs=[row(hw), row(hw), row(hw), row(M_WIDTH), row(M_WIDTH), row(M_WIDTH), row(D_MODEL), row(D_MODEL),
                   row(LANES)],
        compiler_params=_cparams(("parallel",)),
        name="input_projection",
    )(x2, pos2, w_r, g_qlat.reshape(1, -1), g_kvlat.reshape(1, -1), uq_pad, uqs_pad, uk_pad, uv_pad, invf)


def _log_sigmoid(v):
    return jnp.minimum(v, 0.0) - jnp.log1p(jnp.exp(-jnp.abs(v)))


def _mlstm_kernel(xm_ref, vm_ref, op_ref, gate_ref, convw_ref, convb_ref, wqk_ref, gbias_ref, ghead_ref,
                  skip_ref, tril_ref, out_ref, xpad_s, ctn_s, m_s, *, seq_len):
    s = seq_len
    lc = M_CHUNK_ROWS
    halo = 8
    for bb in range(MLSTM_SEQS):
        xpad_s[bb, 0:halo, :] = jnp.zeros((halo, M_WIDTH), F32)
        xpad_s[bb, halo:, :] = xm_ref[bb * s:(bb + 1) * s, :].astype(F32)
    ctn_s[...] = jnp.zeros_like(ctn_s)
    m_s[...] = jnp.zeros_like(m_s)
    rows = lax.broadcasted_iota(I32, (lc, lc), 0)
    cols = lax.broadcasted_iota(I32, (lc, lc), 1)
    causal = rows >= cols
    row_id = lax.broadcasted_iota(I32, (lc, LANES), 0)
    ones_blk = (lax.broadcasted_iota(I32, (lc, LANES), 1) == 0).astype(BF16)
    kscale = M_HEAD_DIM ** -0.5
    hsl = [slice(h * M_HEAD_DIM, (h + 1) * M_HEAD_DIM) for h in range(M_HEADS)]
    hcl = [slice(h, h + 1) for h in range(M_HEADS)]
    streams = [(bb, h) for bb in range(MLSTM_SEQS) for h in range(M_HEADS)]

    def gates(bb, r0):
        win = xpad_s[bb, pl.ds(r0 - bb * s, lc + halo), :]
        conv = convb_ref[...]
        for j in range(M_CONV):
            lo = halo - (M_CONV - 1) + j
            conv = conv + win[lo:lo + lc, :] * convw_ref[j:j + 1, :]
        xc = conv * jax.nn.sigmoid(conv)
        g = gate_ref[pl.ds(r0, lc), :] + gbias_ref[...]
        ls = _log_sigmoid(g)
        bc = jnp.dot(tril_ref[...], ls, preferred_element_type=F32, precision=lax.Precision.HIGHEST)
        b0 = pltpu.roll(bc, LANES - M_HEADS, axis=1)
        u = g - b0
        cm = u
        shift = 1
        while shift < lc:
            cm = jnp.maximum(cm, jnp.where(row_id >= shift, pltpu.roll(cm, shift, axis=0), -jnp.inf))
            shift *= 2
        m_prev = m_s[bb:bb + 1, :]
        g_inter = b0 + m_prev
        m = jnp.maximum(g_inter, b0 + cm)
        b_tot = b0[lc - 1:lc, :]
        aw = b_tot - b0 + g
        m_chunk = jnp.max(aw, axis=0, keepdims=True)
        m_new = jnp.maximum(b_tot + m_prev, m_chunk)
        m_s[bb:bb + 1, :] = m_new
        return dict(xc=xc, w_inter=jnp.exp(g_inter - m), e_negm=jnp.exp(-m), a_mat=b0 - m, wa=jnp.exp(aw - m_chunk),
                    s_old=jnp.exp(b_tot + m_prev - m_new), s_new=jnp.exp(m_chunk - m_new), u_t=u.T)

    def chunk(c, carry):
        r0s = [pl.multiple_of(bb * s + c * lc, lc) for bb in range(MLSTM_SEQS)]
        gs = [gates(bb, r0s[bb]) for bb in range(MLSTM_SEQS)]
        qk = [jnp.dot(gs[bb]["xc"].astype(BF16), wqk_ref[...], preferred_element_type=F32)
              for bb in range(MLSTM_SEQS)]
        q_b = [qk[bb][:, hsl[h]].astype(BF16) for bb, h in streams]
        k_f = [qk[bb][:, M_WIDTH + h * M_HEAD_DIM:M_WIDTH + (h + 1) * M_HEAD_DIM] * kscale for bb, h in streams]
        v_aug = [jnp.concatenate([vm_ref[pl.ds(r0s[bb], lc), hsl[h]], ones_blk], axis=1) for bb, h in streams]
        ctn_prev = [ctn_s[bb * M_HEADS + h] for bb, h in streams]
        s_raw = [lax.dot_general(q_b[i], k_f[i].astype(BF16), (((1,), (1,)), ((), ())), preferred_element_type=F32)
                 for i in range(len(streams))]
        inter = [jnp.dot(q_b[i], ctn_prev[i].astype(BF16), preferred_element_type=F32)
                 for i in range(len(streams))]
        ctn_c = [lax.dot_general((k_f[i] * gs[bb]["wa"][:, hcl[h]]).astype(BF16), v_aug[i], (((0,), (0,)), ((), ())),
                                 preferred_element_type=F32) for i, (bb, h) in enumerate(streams)]
        sc_b = [(s_raw[i] * jnp.exp(jnp.where(causal, gs[bb]["a_mat"][:, hcl[h]] + gs[bb]["u_t"][hcl[h], :],
                                              -jnp.inf))).astype(BF16) for i, (bb, h) in enumerate(streams)]
        intra = [jnp.dot(sc_b[i], v_aug[i], preferred_element_type=F32) for i in range(len(streams))]
        for i, (bb, h) in enumerate(streams):
            g = gs[bb]
            ctn_s[bb * M_HEADS + h] = g["s_old"][:, hcl[h]] * ctn_prev[i] + g["s_new"][:, hcl[h]] * ctn_c[i]
            wi = g["w_inter"][:, hcl[h]]
            num = wi * inter[i][:, :M_HEAD_DIM] + intra[i][:, :M_HEAD_DIM]
            den = wi * inter[i][:, M_HEAD_DIM:M_HEAD_DIM + 1] + intra[i][:, M_HEAD_DIM:M_HEAD_DIM + 1]
            hh = num / jnp.maximum(jnp.abs(den), g["e_negm"][:, hcl[h]])
            mu = jnp.mean(hh, axis=-1, keepdims=True)
            dv = hh - mu
            var = jnp.mean(dv * dv, axis=-1, keepdims=True)
            hn = dv * lax.rsqrt(var + EPS) * ghead_ref[:, hsl[h]]
            y = (jax.nn.sigmoid(op_ref[pl.ds(r0s[bb], lc), hsl[h]].astype(F32))
                 * (hn + skip_ref[:, hsl[h]] * g["xc"][:, hsl[h]]))
            out_ref[pl.ds(r0s[bb], lc), hsl[h]] = y.astype(BF16)
        return carry

    lax.fori_loop(0, s // lc, chunk, 0)


def _mlstm_branch(xm, vm, op, gate, bsz, s, b_i, b_f, conv_w, conv_b, w_mq, w_mk, g_mhead, w_mskip):
    lc = M_CHUNK_ROWS
    gbias = jnp.concatenate([b_i, b_f, jnp.zeros((LANES - 2 * M_HEADS,), F32)]).reshape(1, LANES)
    tril = jnp.tril(jnp.ones((lc, lc), F32))
    eye = jnp.eye(M_HEADS, dtype=F32)
    block_diag = lambda w: jnp.einsum('hde,hg->hdge', w, eye).reshape(M_WIDTH, M_WIDTH)
    w_qk = jnp.concatenate([block_diag(w_mq), block_diag(w_mk)], axis=1).astype(BF16)
    nseq = MLSTM_SEQS
    assert bsz % nseq == 0
    seq = lambda n: pl.BlockSpec((nseq * s, n), lambda b: (b, 0))
    return pl.pallas_call(
        functools.partial(_mlstm_kernel, seq_len=s),
        out_shape=jax.ShapeDtypeStruct((bsz * s, M_WIDTH), BF16),
        grid=(bsz // nseq,),
        in_specs=[seq(M_WIDTH), seq(M_WIDTH), seq(M_WIDTH), seq(LANES),
                  _const_spec((M_CONV, M_WIDTH)), _const_spec((1, M_WIDTH)),
                  _const_spec((M_WIDTH, 2 * M_WIDTH)),
                  _const_spec((1, LANES)), _const_spec((1, M_WIDTH)), _const_spec((1, M_WIDTH)),
                  _const_spec((lc, lc))],
        out_specs=seq(M_WIDTH),
        scratch_shapes=[pltpu.VMEM((nseq, s + 8, M_WIDTH), F32),
                        pltpu.VMEM((nseq * M_HEADS, M_HEAD_DIM, 2 * LANES), F32), pltpu.VMEM((8, LANES), F32)],
        compiler_params=_cparams(("parallel",)),
        name="mlstm_branch",
    )(xm, vm, op, gate, conv_w, conv_b.reshape(1, -1), w_qk, gbias, g_mhead.reshape(1, -1), w_mskip.reshape(1, -1),
      tril)


def _attn_kernel(q_ref, k_ref, v_ref, o_ref, s_scr, m_s, acc_s):
    tq = q_ref.shape[0]
    i = pl.program_id(1)
    rows = lax.broadcasted_iota(I32, (tq, tq), 0)
    cols = lax.broadcasted_iota(I32, (tq, tq), 1)

    def lane_tile_max(sc):
        out = sc[:, :LANES]
        for t in range(1, tq // LANES):
            out = jnp.maximum(out, sc[:, t * LANES:(t + 1) * LANES])
        return out

    heads = [slice(h * LANES, (h + 1) * LANES) for h in range(A_HEADS)]
    m_s[...] = jnp.full(m_s.shape, NEG_BIG, F32)
    acc_s[...] = jnp.zeros_like(acc_s)

    def pass_a(j, diagonal):
        k0 = pl.multiple_of(j * tq, tq)
        for h, hs in enumerate(heads):
            sc = lax.dot_general(q_ref[:, hs], k_ref[pl.ds(k0, tq), hs], (((1,), (1,)), ((), ())),
                                 preferred_element_type=F32)
            if diagonal:
                sc = jnp.where(cols <= rows, sc, NEG_BIG)
            s_scr[h, j] = sc
            m_s[h] = jnp.maximum(m_s[h], lane_tile_max(sc))

    def for_blocks(n, fn):
        def pair(p, c):
            fn(2 * p)
            fn(2 * p + 1)
            return c

        lax.fori_loop(0, n // 2, pair, 0)

        @pl.when(n % 2 == 1)
        def _():
            fn(n - 1)

    for_blocks(i, lambda j: pass_a(j, False))
    pass_a(i, True)
    for h in range(A_HEADS):
        m_s[h] = jnp.broadcast_to(jnp.max(m_s[h], axis=1, keepdims=True), (tq, LANES))

    def pass_b(j):
        k0 = pl.multiple_of(j * tq, tq)
        for h, hs in enumerate(heads):
            m_row = m_s[h]
            p = jnp.exp2(s_scr[h, j] - jnp.concatenate([m_row] * (tq // LANES), axis=1)).astype(BF16)
            acc_s[h] += jnp.dot(p, v_ref[pl.ds(k0, tq), hs], preferred_element_type=F32)

    for_blocks(i + 1, pass_b)
    for h, hs in enumerate(heads):
        acc = acc_s[h]
        o_ref[:, hs] = (acc / acc[:, A_VDIM:A_VDIM + 1]).astype(BF16)


def _mla_attention(q, k, v, bsz, s):
    tq = min(ATT_Q, s)
    nq = s // tq
    hw = A_HEADS * LANES
    return pl.pallas_call(
        _attn_kernel,
        out_shape=jax.ShapeDtypeStruct(q.shape, BF16),
        grid=(bsz, nq),
        in_specs=[pl.BlockSpec((tq, hw), lambda b, i: (b * nq + i, 0)),
                  pl.BlockSpec((s, hw), lambda b, i: (b, 0)),
                  pl.BlockSpec((s, hw), lambda b, i: (b, 0))],
        out_specs=pl.BlockSpec((tq, hw), lambda b, i: (b * nq + i, 0)),
        scratch_shapes=[pltpu.VMEM((A_HEADS, nq, tq, tq), F32), pltpu.VMEM((A_HEADS, tq, LANES), F32),
                        pltpu.VMEM((A_HEADS, tq, LANES), F32)],
        compiler_params=_cparams(("parallel", "arbitrary")),
        name="mla_attention",
    )(q, k, v)


def _merge_kernel(ym_ref, oa_ref, gm_ref, ga_ref, x_ref, wbm_ref, wba_ref, wmix_ref, g_ref, b_ref, h_ref):
    y_m = jnp.dot(ym_ref[...], wbm_ref[...], preferred_element_type=F32)
    y_a = jnp.dot(oa_ref[...], wba_ref[...], preferred_element_type=F32)
    mixed = jax.nn.sigmoid(gm_ref[...].astype(F32)) * y_m + jax.nn.sigmoid(ga_ref[...].astype(F32)) * y_a
    mix = jnp.dot(mixed.astype(BF16), wmix_ref[...], preferred_element_type=F32)
    h_ref[...] = _layer_norm_rows(DN_ALPHA * x_ref[...] + mix, g_ref[...], b_ref[...])


def _merge(ym, oa, gm, ga, x2, w_br_m, w_br_a, w_mix_out, ln_g, ln_b):
    t = x2.shape[0]
    tm = min(MERGE_ROWS, t)
    wba = jnp.concatenate([w_br_a.reshape(A_HEADS, A_VDIM, D_MODEL),
                           jnp.zeros((A_HEADS, LANES - A_VDIM, D_MODEL), w_br_a.dtype)], axis=1)
    wba = wba.reshape(A_HEADS * LANES, D_MODEL).astype(BF16)
    row = lambda n: pl.BlockSpec((tm, n), lambda i: (i, 0))
    return pl.pallas_call(
        _merge_kernel,
        out_shape=jax.ShapeDtypeStruct((t, D_MODEL), F32),
        grid=(t // tm,),
        in_specs=[row(M_WIDTH), row(A_HEADS * LANES), row(D_MODEL), row(D_MODEL), row(D_MODEL),
                  _const_spec((M_WIDTH, D_MODEL)), _const_spec((A_HEADS * LANES, D_MODEL)),
                  _const_spec((D_MODEL, D_MODEL)), _const_spec((1, D_MODEL)), _const_spec((1, D_MODEL))],
        out_specs=row(D_MODEL),
        compiler_params=_cparams(("parallel",)),
        name="merge_deepnorm1",
    )(ym, oa, gm, ga, x2, w_br_m.astype(BF16), wba, w_mix_out.astype(BF16), ln_g.reshape(1, -1), ln_b.reshape(1, -1))


def _kv_kernel(m_ref, w_ref, o_ref):
    o_ref[...] = jnp.dot(m_ref[...].astype(BF16), w_ref[...], preferred_element_type=F32).astype(BF16)


def _mem_kv(mem2, w_ckv):
    r = mem2.shape[0]
    tm = min(512, r)
    return pl.pallas_call(
        _kv_kernel,
        out_shape=jax.ShapeDtypeStruct((r, 2 * D_MODEL), BF16),
        grid=(r // tm,),
        in_specs=[pl.BlockSpec((tm, D_MODEL), lambda i: (i, 0)), _const_spec((D_MODEL, 2 * D_MODEL))],
        out_specs=pl.BlockSpec((tm, 2 * D_MODEL), lambda i: (i, 0)),
        compiler_params=_cparams(("parallel",)),
        name="memory_kv",
    )(mem2, w_ckv.astype(BF16))


def _xattn_kernel(h_ref, kv_ref, wq_ref, wo_ref, g_ref, b_ref, wr_ref, br_ref, *rest):
    h2_ref, idx_ref, gate_ref = rest[-3:]
    h1 = h_ref[...]
    q = jnp.dot(h1.astype(BF16), wq_ref[...], preferred_element_type=F32).astype(BF16)
    scale = X_HEAD_DIM ** -0.5
    ksl = [slice(hd * X_HEAD_DIM, (hd + 1) * X_HEAD_DIM) for hd in range(X_HEADS)]
    vsl = [slice(D_MODEL + hd * X_HEAD_DIM, D_MODEL + (hd + 1) * X_HEAD_DIM) for hd in range(X_HEADS)]
    scs = [lax.dot_general(q[:, ks], kv_ref[:, ks], (((1,), (1,)), ((), ())), preferred_element_type=F32) * scale
           for ks in ksl]
    ps = [jnp.exp(sc - jnp.max(sc, axis=1, keepdims=True)) for sc in scs]
    ps = [(p / jnp.sum(p, axis=1, keepdims=True)).astype(BF16) for p in ps]
    o = jnp.concatenate([jnp.dot(p, kv_ref[:, vs], preferred_element_type=F32).astype(BF16)
                         for p, vs in zip(ps, vsl)], axis=1)
    att = jnp.dot(o, wo_ref[...], preferred_element_type=F32)
    h2 = _layer_norm_rows(DN_ALPHA * h1 + att, g_ref[...], b_ref[...])
    for c in range(ROW_TILES):
        h2_ref[pl.ds(c, h2.shape[0], stride=ROW_TILES), :] = h2[:, c * LANES:(c + 1) * LANES]

    h2_hi = h2.astype(BF16)
    h2_lo = (h2 - h2_hi.astype(F32)).astype(BF16)
    hi_prod = jnp.dot(h2_hi, wr_ref[...], preferred_element_type=F32)
    lo_prod = jnp.dot(h2_lo, wr_ref[:, :LANES], preferred_element_type=F32)
    logits = hi_prod[:, :LANES] + (hi_prod[:, LANES:] + lo_prod) + br_ref[...]
    tm = logits.shape[0]
    lane = lax.broadcasted_iota(I32, (tm, LANES), 1)
    lane_f = lane.astype(F32)
    work = jnp.where(lane < N_EXPERTS, logits, -jnp.inf)
    vals, idxs = [], []
    for _ in range(TOP_K):
        mx = jnp.max(work, axis=1, keepdims=True)
        ix = jnp.min(jnp.where(work == mx, lane_f, float(LANES)), axis=1, keepdims=True)
        vals.append(mx)
        idxs.append(ix)
        work = jnp.where(lane_f == ix, -jnp.inf, work)
    es = [jnp.exp(vv - vals[0]) for vv in vals]
    tot = es[0] + es[1] + es[2] + es[3]
    idx_slab = jnp.zeros((tm, LANES), F32)
    gate_slab = jnp.zeros((tm, LANES), F32)
    for kk in range(TOP_K):
        idx_slab = jnp.where(lane == kk, idxs[kk], idx_slab)
        gate_slab = jnp.where(lane == kk, es[kk] / tot, gate_slab)
    idx_ref[...] = idx_slab.astype(I32)
    gate_ref[...] = gate_slab


def _cross_attention_router(h1, kv, bsz, s, w_cq, w_co, ln_g, ln_b, w_router, b_router, part, after):
    n_mem = kv.shape[0] // bsz
    bsz = bsz // MOE_PARTS
    b0 = part * bsz
    t = bsz * s
    tm = min(XATT_ROWS, s)
    ns = s // tm
    wr = jnp.concatenate([w_router, jnp.zeros((D_MODEL, LANES - N_EXPERTS), F32)], axis=1)
    wr_hi = wr.astype(BF16)
    wr = jnp.concatenate([wr_hi, (wr - wr_hi.astype(F32)).astype(BF16)], axis=1)
    br = jnp.concatenate([b_router, jnp.zeros((LANES - N_EXPERTS,), F32)]).reshape(1, LANES)
    row = lambda n: pl.BlockSpec((tm, n), lambda b, i: (b * ns + i, 0))
    h1_rows = pl.BlockSpec((tm, D_MODEL), lambda b, i: ((b0 + b) * ns + i, 0))
    in_specs = [h1_rows, pl.BlockSpec((n_mem, 2 * D_MODEL), lambda b, i: (b0 + b, 0)),
                _const_spec((D_MODEL, D_MODEL)), _const_spec((D_MODEL, D_MODEL)),
                _const_spec((1, D_MODEL)), _const_spec((1, D_MODEL)),
                _const_spec((D_MODEL, 2 * LANES)), _const_spec((1, LANES))]
    args = [h1, kv, w_cq.astype(BF16), w_co.astype(BF16), ln_g.reshape(1, -1), ln_b.reshape(1, -1), wr, br]
    if after is not None:
        in_specs.append(pl.BlockSpec(memory_space=pl.ANY))
        args.append(after)
    return pl.pallas_call(
        _xattn_kernel,
        out_shape=[jax.ShapeDtypeStruct((t * ROW_TILES, LANES), F32), jax.ShapeDtypeStruct((t, LANES), I32),
                   jax.ShapeDtypeStruct((t, LANES), F32)],
        grid=(bsz, ns),
        in_specs=in_specs,
        out_specs=[pl.BlockSpec((tm * ROW_TILES, LANES), lambda b, i: (b * ns + i, 0)), row(LANES), row(LANES)],
        compiler_params=_cparams(("parallel", "parallel")),
        name="cross_attention_router",
    )(*args)


def _rank_kernel(idx_ref, ltri_ref, rank_ref, cnt_ref, carry_s):
    @pl.when(pl.program_id(0) == 0)
    def _():
        carry_s[...] = jnp.zeros_like(carry_s)

    idx = idx_ref[...]
    tm = idx.shape[0]
    lane = lax.broadcasted_iota(I32, (tm, LANES), 1)
    onehots = [(lane == idx[:, kk:kk + 1]).astype(F32) for kk in range(TOP_K)]
    sel = onehots[0] + onehots[1] + onehots[2] + onehots[3]
    before = jnp.dot(ltri_ref[...], sel.astype(BF16), preferred_element_type=F32) + carry_s[0:1, :]
    rank_slab = jnp.zeros((tm, LANES), F32)
    for kk in range(TOP_K):
        r = jnp.sum(onehots[kk] * before, axis=1, keepdims=True)
        rank_slab = jnp.where(lane == kk, r, rank_slab)
    rank_ref[...] = rank_slab.astype(I32)
    carry_s[0:1, :] = carry_s[0:1, :] + jnp.sum(sel, axis=0, keepdims=True)
    cnt_ref[...] = jnp.broadcast_to(carry_s[0:1, :], cnt_ref.shape).astype(I32)


def _routing_ranks(idx_slab):
    t = idx_slab.shape[0]
    tm = min(RANK_ROWS, t)
    ltri = jnp.tril(jnp.ones((tm, tm), BF16), k=-1)
    return pl.pallas_call(
        _rank_kernel,
        out_shape=[jax.ShapeDtypeStruct((t, LANES), I32), jax.ShapeDtypeStruct((8, LANES), I32)],
        grid=(t // tm,),
        in_specs=[pl.BlockSpec((tm, LANES), lambda i: (i, 0)), _const_spec((tm, tm))],
        out_specs=[pl.BlockSpec((tm, LANES), lambda i: (i, 0)), _const_spec((8, LANES))],
        scratch_shapes=[pltpu.VMEM((8, LANES), F32)],
        compiler_params=_cparams(("arbitrary",)),
        name="routing_ranks",
    )(idx_slab, ltri)


def _dest_kernel(idx_ref, rank_ref, pstart_ref, dest_ref):
    idx = idx_ref[...]
    tm = idx.shape[0]
    lane = lax.broadcasted_iota(I32, (tm, LANES), 1)
    dest = rank_ref[...].astype(F32)
    pstart = pstart_ref[0:1, :].astype(F32)
    for kk in range(TOP_K):
        start = jnp.sum(jnp.where(lane == idx[:, kk:kk + 1], pstart, 0.0), axis=1, keepdims=True)
        dest = dest + jnp.where(lane == kk, start, 0.0)
    dest_ref[...] = dest.T[0:8, :].astype(I32)


def _dest_rows(idx_slab, rank_slab, pstart_row):
    t = rank_slab.shape[0]
    tm = min(RANK_ROWS, t)
    return pl.pallas_call(
        _dest_kernel,
        out_shape=jax.ShapeDtypeStruct((8, t), I32),
        grid=(t // tm,),
        in_specs=[pl.BlockSpec((tm, LANES), lambda i: (i, 0)), pl.BlockSpec((tm, LANES), lambda i: (i, 0)),
                  _const_spec((8, LANES))],
        out_specs=pl.BlockSpec((8, tm), lambda i: (0, i)),
        compiler_params=_cparams(("parallel",)),
        name="routing_dest",
    )(idx_slab, rank_slab, pstart_row)


def _sc_mesh():
    return plsc.VectorSubcoreMesh(core_axis_name="c", subcore_axis_name="s", num_cores=SC_CORES,
                                  num_subcores=SC_SUBCORES)


def _sc_worker():
    return lax.axis_index("s") * SC_CORES + lax.axis_index("c")


def _dispatch(h2, dest_kt, pad_rows, n_rows):
    t = dest_kt.shape[1]
    w, sub = SC_ROWS, SC_SUB
    nsub = w // sub
    per_w = t // SC_WORKERS
    steps = per_w // w
    assert steps % 2 == 0 and steps * w * SC_WORKERS == t
    pad_n = pad_rows.shape[0] // (SC_WORKERS * sub)
    dest_w = dest_kt.reshape(TOP_K, SC_WORKERS, steps * nsub, sub)
    pad_w = pad_rows.reshape(SC_WORKERS, pad_n, sub)
    zeros = jnp.zeros((sub, ROW_TILES, LANES), F32)

    def body(x_hbm, dest_hbm, pad_hbm, zeros_hbm, o_hbm, idx_v, pad_v, rows_v, zero_v, sem_in, sem_out, sem_pad):
        wid = _sc_worker()
        tok0 = wid * per_w
        for kk in range(TOP_K):
            pltpu.sync_copy(dest_hbm.at[kk, wid], idx_v.at[pl.ds(kk * steps * nsub, steps * nsub)])
        pltpu.sync_copy(pad_hbm.at[wid], pad_v)
        pltpu.sync_copy(zeros_hbm, zero_v)

        def pad_scatter(j):
            return pltpu.make_async_copy(zero_v, o_hbm.at[pad_v.at[j]], sem_pad)

        for j in range(pad_n):
            pad_scatter(j).start()

        def load(step, b):
            return pltpu.make_async_copy(x_hbm.at[pl.ds(tok0 + step * w, w)], rows_v.at[b], sem_in.at[b])

        def scatters(step, b):
            return [pltpu.make_async_copy(rows_v.at[b, pl.ds(c * sub, sub)],
                                          o_hbm.at[idx_v.at[(kk * steps + step) * nsub + c]], sem_out.at[b])
                    for kk in range(TOP_K) for c in range(nsub)]

        load(0, 0).start()

        @pl.loop(0, steps, step=2)
        def _(s0):
            for b in range(2):
                step = s0 + b
                load(step, b).wait()

                @pl.when(step >= 1)
                def _():
                    for cp in scatters(step - 1, 1 - b):
                        cp.wait()

                @pl.when(step + 1 < steps)
                def _():
                    load(step + 1, 1 - b).start()

                for cp in scatters(step, b):
                    cp.start()

        for cp in scatters(steps - 1, 1):
            cp.wait()
        for j in range(pad_n):
            pad_scatter(j).wait()

    return pl.kernel(
        body,
        out_type=jax.ShapeDtypeStruct((n_rows, ROW_TILES, LANES), F32),
        mesh=_sc_mesh(),
        scratch_types=[pltpu.VMEM((TOP_K * steps * nsub, sub), I32), pltpu.VMEM((pad_n, sub), I32),
                       pltpu.VMEM((2, w, ROW_TILES, LANES), F32), pltpu.VMEM((sub, ROW_TILES, LANES), F32),
                       pltpu.SemaphoreType.DMA((2,)), pltpu.SemaphoreType.DMA((2,)), pltpu.SemaphoreType.DMA],
        name="moe_dispatch_sc",
    )(h2, dest_w, pad_w, zeros)


def _gather_rows(yb, dest_kt):
    n = dest_kt.shape[0]
    w, sub = SC_ROWS, SC_SUB
    nsub = w // sub
    per_w = n // SC_WORKERS
    steps = per_w // w
    assert steps % 2 == 0 and steps * w * SC_WORKERS == n
    dest_w = dest_kt.reshape(SC_WORKERS, steps * nsub, sub)

    def body(y_hbm, dest_hbm, o_hbm, idx_v, rows_v, sem_g, sem_w):
        wid = _sc_worker()
        row0 = wid * per_w
        pltpu.sync_copy(dest_hbm.at[wid], idx_v)

        def gathers(step, b):
            return [pltpu.make_async_copy(y_hbm.at[idx_v.at[step * nsub + c]], rows_v.at[b, pl.ds(c * sub, sub)],
                                          sem_g.at[b]) for c in range(nsub)]

        def write(step, b):
            return pltpu.make_async_copy(rows_v.at[b], o_hbm.at[pl.ds(row0 + step * w, w)], sem_w.at[b])

        for cp in gathers(0, 0):
            cp.start()

        @pl.loop(0, steps, step=2)
        def _(s0):
            for b in range(2):
                step = s0 + b
                for cp in gathers(step, b):
                    cp.wait()

                @pl.when(step >= 1)
                def _():
                    write(step - 1, 1 - b).wait()

                @pl.when(step + 1 < steps)
                def _():
                    for cp in gathers(step + 1, 1 - b):
                        cp.start()

                write(step, b).start()

        write(steps - 1, 1).wait()

    return pl.kernel(
        body,
        out_type=jax.ShapeDtypeStruct((n, ROW_TILES, LANES), F32),
        mesh=_sc_mesh(),
        scratch_types=[pltpu.VMEM((steps * nsub, sub), I32), pltpu.VMEM((2, w, ROW_TILES, LANES), F32),
                       pltpu.SemaphoreType.DMA((2,)), pltpu.SemaphoreType.DMA((2,))],
        name="moe_gather_sc",
    )(yb, dest_w)


def _rows_2d(ref, n):
    return jnp.concatenate([ref[pl.ds(c, n, stride=ROW_TILES), :] for c in range(ROW_TILES)], axis=1)


def _store_rows(ref, val):
    for c in range(ROW_TILES):
        ref[pl.ds(c, val.shape[0], stride=ROW_TILES), :] = val[:, c * LANES:(c + 1) * LANES]


def _expert_kernel(blk_e_ref, first_ref, slot_ref, next_ref, nblk_ref, x_ref, wg_hbm, wl_hbm, bg_ref, bl_ref, wd_hbm,
                   bd_ref, y_ref, wg_buf, wl_buf, wd_stage, wd_buf, sem):
    i = pl.program_id(0)

    def fetch(e, sl):
        return [pltpu.make_async_copy(src.at[e], dst.at[sl], sem.at[sl, j])
                for j, (src, dst) in enumerate(((wg_hbm, wg_buf), (wl_hbm, wl_buf), (wd_hbm, wd_stage)))]

    @pl.when(i < nblk_ref[0])
    def _():
        sl = slot_ref[i]

        @pl.when(first_ref[i] == 1)
        def _():
            @pl.when(i == 0)
            def _():
                for cp in fetch(blk_e_ref[0], sl):
                    cp.start()

            for cp in fetch(blk_e_ref[i], sl):
                cp.wait()
            wd_buf[sl] = wd_stage[sl].astype(BF16)

            @pl.when(next_ref[i] >= 0)
            def _():
                for cp in fetch(next_ref[i], 1 - sl):
                    cp.start()

        xb = _rows_2d(x_ref, y_ref.shape[0] // ROW_TILES).astype(BF16)
        glu = jnp.dot(xb, wg_buf[sl], preferred_element_type=F32) + bg_ref[...]
        lin = jnp.dot(xb, wl_buf[sl], preferred_element_type=F32) + bl_ref[...]
        glu = jnp.minimum(glu, SWIGLU_LIMIT)
        lin = jnp.clip(lin, -SWIGLU_LIMIT, SWIGLU_LIMIT)
        act = glu * jax.nn.sigmoid(SWIGLU_ALPHA * glu) * (lin + 1.0)
        _store_rows(y_ref, jnp.dot(act.astype(BF16), wd_buf[sl], preferred_element_type=F32) + bd_ref[...])

    @pl.when(i >= nblk_ref[0])
    def _():
        y_ref[...] = jnp.zeros_like(y_ref)


def _split_kernel(w_ref, perm_ref, after_ref, g_ref, l_ref):
    del after_ref
    half = g_ref.shape[-1]
    sorted_cols = jnp.dot(w_ref[...].astype(BF16), perm_ref[...], preferred_element_type=F32)
    g_ref[...] = sorted_cols[:, :half].astype(BF16)
    l_ref[...] = sorted_cols[:, half:].astype(BF16)


def _split_gate_up(w_gu, after):
    e, d, n2 = w_gu.shape
    cw = 512
    src = np.concatenate([np.arange(0, cw, 2), np.arange(1, cw, 2)])
    perm = jnp.asarray(np.eye(cw, dtype=np.float32)[:, src], BF16)
    return pl.pallas_call(
        _split_kernel,
        out_shape=[jax.ShapeDtypeStruct((e, d, n2 // 2), BF16), jax.ShapeDtypeStruct((e, d, n2 // 2), BF16)],
        grid=(e, n2 // cw),
        in_specs=[pl.BlockSpec((None, d, cw), lambda i, c: (i, 0, c)), _const_spec((cw, cw)),
                  pl.BlockSpec(memory_space=pl.ANY)],
        out_specs=[pl.BlockSpec((None, d, cw // 2), lambda i, c: (i, 0, c)),
                   pl.BlockSpec((None, d, cw // 2), lambda i, c: (i, 0, c))],
        compiler_params=_cparams(("parallel", "parallel")),
        name="split_gate_up",
    )(w_gu, perm, after)


def _experts(xb, blk_meta, w_glu, w_lin, b_glu, b_lin, w_dn, b_dn):
    n_rows = xb.shape[0] // ROW_TILES
    bm = EXPERT_ROWS
    vec = lambda n: pl.BlockSpec((None, 1, n), lambda i, be, *_: (be[i], 0, 0))
    rows = pl.BlockSpec((bm * ROW_TILES, LANES), lambda i, *_: (i, 0))
    hbm = pl.BlockSpec(memory_space=pl.ANY)
    grid_spec = pltpu.PrefetchScalarGridSpec(
        num_scalar_prefetch=5,
        grid=(n_rows // bm,),
        in_specs=[rows, hbm, hbm, vec(D_EXPERT), vec(D_EXPERT), hbm, vec(D_MODEL)],
        out_specs=rows,
        scratch_shapes=[pltpu.VMEM((2, D_MODEL, D_EXPERT), BF16), pltpu.VMEM((2, D_MODEL, D_EXPERT), BF16),
                        pltpu.VMEM((2, D_EXPERT, D_MODEL), F32), pltpu.VMEM((2, D_EXPERT, D_MODEL), BF16),
                        pltpu.SemaphoreType.DMA((2, 3))],
    )
    return pl.pallas_call(
        _expert_kernel,
        out_shape=jax.ShapeDtypeStruct((n_rows * ROW_TILES, LANES), F32),
        grid_spec=grid_spec,
        compiler_params=_cparams(("arbitrary",)),
        name="moe_experts",
    )(*blk_meta, xb, w_glu, w_lin, b_glu, b_lin, w_dn, b_dn)


def _combine_kernel(h_ref, gate_ref, y0_ref, y1_ref, y2_ref, y3_ref, g_ref, b_ref, *rest):
    o_ref = rest[-1]
    tt = o_ref.shape[0]
    gates = gate_ref[...]
    acc = DN_ALPHA * _rows_2d(h_ref, tt)
    for kk, y_ref in enumerate((y0_ref, y1_ref, y2_ref, y3_ref)):
        acc = acc + gates[:, kk:kk + 1] * _rows_2d(y_ref, tt)
    o_ref[...] = _layer_norm_rows(acc, g_ref[...], b_ref[...])


def _combine(h2, gate_slab, yg, ln_g, ln_b, part, prev_out):
    t = h2.shape[0] // ROW_TILES
    t_all = t * MOE_PARTS
    tt = min(COMBINE_TOKENS, t)
    steps = t // tt
    first = part * steps
    row = lambda m: pl.BlockSpec((tt, m), lambda i: (i, 0))
    tiles = lambda kk: pl.BlockSpec((tt * ROW_TILES, LANES), lambda i: (kk * steps + i, 0))
    in_specs = [tiles(0), row(LANES),
                tiles(0), tiles(1), tiles(2), tiles(3), _const_spec((1, D_MODEL)), _const_spec((1, D_MODEL))]
    args = [h2, gate_slab, yg, yg, yg, yg, ln_g.reshape(1, -1), ln_b.reshape(1, -1)]
    aliases = {}
    if prev_out is not None:
        in_specs.append(pl.BlockSpec(memory_space=pl.ANY))
        args.append(prev_out)
        aliases = {len(args) - 1: 0}
    return pl.pallas_call(
        _combine_kernel,
        out_shape=jax.ShapeDtypeStruct((t_all, D_MODEL), F32),
        grid=(steps,),
        in_specs=in_specs,
        out_specs=pl.BlockSpec((tt, D_MODEL), lambda i: (first + i, 0)),
        input_output_aliases=aliases,
        compiler_params=_cparams(("parallel",)),
        name="moe_combine",
    )(*args)


def _route_and_dispatch(h2, idx_slab):
    t = h2.shape[0] // ROW_TILES
    bm = EXPERT_ROWS
    n_rows = t * TOP_K + N_EXPERTS * bm
    n_blocks = n_rows // bm

    rank_slab, cnt = _routing_ranks(idx_slab)
    counts = cnt[0, :N_EXPERTS]
    padded = (counts + bm - 1) // bm * bm
    pends = jnp.cumsum(padded)
    pstarts = pends - padded
    pstart_row = jnp.zeros((8, LANES), I32).at[:, :N_EXPERTS].set(pstarts[None, :])
    blk_start = jnp.arange(n_blocks, dtype=I32) * bm
    blk_e = jnp.minimum(jnp.sum((pends[None, :] <= blk_start[:, None]).astype(I32), axis=1), N_EXPERTS - 1)
    nblk = (pends[-1] // bm).astype(I32).reshape(1)
    active = counts > 0
    ordinal = jnp.cumsum(active.astype(I32)) - 1
    eid = jnp.arange(N_EXPERTS, dtype=I32)
    later = active[None, :] & (eid[None, :] > eid[:, None])
    next_active = jnp.where(later.any(axis=1), jnp.argmax(later, axis=1).astype(I32), -1)
    blk_first = jnp.concatenate([jnp.ones((1,), I32), (blk_e[1:] != blk_e[:-1]).astype(I32)])
    blk_onehot = (blk_e[:, None] == eid[None, :]).astype(I32)
    blk_slot = jnp.sum(blk_onehot * ordinal[None, :], axis=1) % 2
    blk_next = jnp.sum(blk_onehot * next_active[None, :], axis=1)
    blk_meta = (blk_e, blk_first, blk_slot, blk_next, nblk)
    dest_kt = _dest_rows(idx_slab, rank_slab, pstart_row)[:TOP_K]
    pad_off = jnp.arange(bm, dtype=I32)[None, :]
    pad_rows = jnp.where(pad_off < (padded - counts)[:, None], (pstarts + counts)[:, None] + pad_off,
                         n_rows - 1).reshape(-1)
    xb = _dispatch(h2.reshape(t, ROW_TILES, LANES), dest_kt, pad_rows, n_rows)
    ready = dest_kt[0, :8] + pad_rows[:8]
    return xb.reshape(n_rows * ROW_TILES, LANES), blk_meta, dest_kt, ready


def _xattn_moe(h1, kv, bsz, s, w_cq, w_co, ln2_g, ln2_b, w_router, b_router, w_gu, b_gu, w_dn, b_dn, ln3_g, ln3_b):
    parts = []
    after = None
    for part in range(MOE_PARTS):
        h2, idx_slab, gate_slab = _cross_attention_router(h1, kv, bsz, s, w_cq, w_co, ln2_g, ln2_b, w_router,
                                                          b_router, part, after)
        xb, blk_meta, dest_kt, after = _route_and_dispatch(h2, idx_slab)
        parts.append((h2, gate_slab, xb, blk_meta, dest_kt))
    w_glu, w_lin = _split_gate_up(w_gu, after)
    expert_w = (w_glu, w_lin, b_gu[:, None, 0::2], b_gu[:, None, 1::2], w_dn, b_dn[:, None, :])
    out = None
    for part, (h2, gate_slab, xb, blk_meta, dest_kt) in enumerate(parts):
        n_rows = xb.shape[0] // ROW_TILES
        yb = _experts(xb, blk_meta, *expert_w)
        yg = _gather_rows(yb.reshape(n_rows, ROW_TILES, LANES), dest_kt.reshape(-1))
        out = _combine(h2, gate_slab, yg.reshape(-1, LANES), ln3_g, ln3_b, part, out)
    return out


def kernel(x, mem, positions, w_in, b_igate, b_fgate, conv_w, conv_b, w_mq, w_mk, g_mhead, w_mskip, g_qlat, g_kvlat,
           w_uq, w_ukv, w_br_m, w_br_a, w_mix_out, ln1_g, ln1_b, w_cq, w_ckv, w_co, ln2_g, ln2_b, w_router, b_router,
           w_gu, b_gu, w_dn, b_dn, ln3_g, ln3_b):
    bsz, s, d = x.shape
    t = bsz * s
    h = x.reshape(t, d)
    pos2 = positions.reshape(t, 1)
    for l in range(DEPTH):
        q, k, v, xm, vm, op, gm, ga, gate = _input_projection(h, pos2, w_in[l], g_qlat[l], g_kvlat[l], w_uq[l],
                                                              w_ukv[l])
        ym = _mlstm_branch(xm, vm, op, gate, bsz, s, b_igate[l], b_fgate[l], conv_w[l], conv_b[l], w_mq[l], w_mk[l],
                           g_mhead[l], w_mskip[l])
        oa = _mla_attention(q, k, v, bsz, s)
        h1 = _merge(ym, oa, gm, ga, h, w_br_m[l], w_br_a[l], w_mix_out[l], ln1_g[l], ln1_b[l])
        kv = _mem_kv(mem.reshape(-1, d), w_ckv[l])
        h = _xattn_moe(h1, kv, bsz, s, w_cq[l], w_co[l], ln2_g[l], ln2_b[l], w_router[l], b_router[l], w_gu[l],
                       b_gu[l], w_dn[l], b_dn[l], ln3_g[l], ln3_b[l])
    return h.reshape(bsz, s, d)
```

```python
import functools

import numpy as np
import jax
import jax.numpy as jnp
from jax import lax
from jax.experimental import pallas as pl
from jax.experimental.pallas import tpu as pltpu
from jax.experimental.pallas import tpu_sc as plsc

F32 = jnp.float32
BF16 = jnp.bfloat16
I32 = jnp.int32

D_MODEL = 1024
N_MEM = 256
M_HEADS = 4
M_HEAD_DIM = 128
M_WIDTH = M_HEADS * M_HEAD_DIM
M_CONV = 4
A_HEADS = 8
A_NOPE = 64
A_ROPE = 32
A_QK = A_NOPE + A_ROPE
A_VDIM = 64
A_Q_RANK = 256
A_KV_RANK = 128
ROPE_THETA = 10000.0
X_HEADS = 4
X_HEAD_DIM = D_MODEL // X_HEADS
N_EXPERTS = 32
TOP_K = 4
D_EXPERT = D_MODEL
SWIGLU_ALPHA = 1.702
SWIGLU_LIMIT = 7.0
DEPTH = 1
DN_ALPHA = (2.0 * DEPTH) ** 0.25
EPS = 1e-5
IN_SPLITS = (A_Q_RANK, A_KV_RANK, A_ROPE, M_WIDTH, M_WIDTH, M_WIDTH, M_HEADS, M_HEADS, D_MODEL, D_MODEL)
IN_OFFSETS = tuple(int(v) for v in np.cumsum((0,) + IN_SPLITS))

LANES = 128
VMEM_LIMIT = 56 * 1024 * 1024

PROJ_ROWS = 512
M_CHUNK_ROWS = 128
MLSTM_SEQS = 2
ATT_Q = 256
MERGE_ROWS = 512
XATT_ROWS = 512
RANK_ROWS = 1024
EXPERT_ROWS = 256
COMBINE_TOKENS = 256
MOE_PARTS = 2
SC_CORES = 2
SC_SUBCORES = 16
SC_WORKERS = SC_CORES * SC_SUBCORES
SC_ROWS = 32
SC_SUB = 8
SC_LANES = 16
NEG_BIG = -1e30
LOG2_E = 1.4426950408889634
ROW_TILES = D_MODEL // LANES

C_QLAT = 0
C_KVLAT = C_QLAT + A_Q_RANK
C_KR = C_KVLAT + A_KV_RANK
C_KRS = C_KR + LANES
C_GATE = C_KRS + LANES
C_XM = C_GATE + LANES
C_VM = C_XM + M_WIDTH
C_OP = C_VM + M_WIDTH
C_GM = C_OP + M_WIDTH
C_GA = C_GM + D_MODEL
C_END = C_GA + D_MODEL


def _cparams(sem, vmem=VMEM_LIMIT):
    return pltpu.CompilerParams(dimension_semantics=sem, vmem_limit_bytes=vmem)


def _const_spec(shape):
    nd = len(shape)
    return pl.BlockSpec(shape, lambda *a: (0,) * nd)


def _layer_norm_rows(v, g, b):
    mu = jnp.mean(v, axis=-1, keepdims=True)
    d = v - mu
    var = jnp.mean(d * d, axis=-1, keepdims=True)
    return d * lax.rsqrt(var + EPS) * g + b


def _proj_kernel(x_ref, pos_ref, w_ref, gq_ref, gkv_ref, wuq_ref, wuqs_ref, wuk_ref, wuv_ref, invf_ref,
                 q_ref, k_ref, v_ref, xm_ref, vm_ref, op_ref, gm_ref, ga_ref, gate_ref):
    xb = x_ref[...].astype(BF16)

    def mm(lo, n):
        return jnp.dot(xb, w_ref[:, lo:lo + n], preferred_element_type=F32)

    xm_ref[...] = mm(C_XM, M_WIDTH).astype(BF16)
    vm_ref[...] = mm(C_VM, M_WIDTH).astype(BF16)
    op_ref[...] = mm(C_OP, M_WIDTH).astype(BF16)
    gm_ref[...] = mm(C_GM, D_MODEL).astype(BF16)
    ga_ref[...] = mm(C_GA, D_MODEL).astype(BF16)
    gate_ref[...] = mm(C_GATE, LANES)

    ang = pos_ref[...].astype(F32) * invf_ref[...]
    cos = jnp.cos(ang)
    sin = jnp.sin(ang)

    q_lat = mm(C_QLAT, A_Q_RANK)
    qn = (q_lat * lax.rsqrt(jnp.mean(q_lat * q_lat, axis=-1, keepdims=True) + EPS) * gq_ref[...]).astype(BF16)
    q = jnp.dot(qn, wuq_ref[...], preferred_element_type=F32)
    qs = jnp.dot(qn, wuqs_ref[...], preferred_element_type=F32)
    kv_lat = mm(C_KVLAT, A_KV_RANK)
    kvn = (kv_lat * lax.rsqrt(jnp.mean(kv_lat * kv_lat, axis=-1, keepdims=True) + EPS) * gkv_ref[...]).astype(BF16)
    kn = jnp.dot(kvn, wuk_ref[...], preferred_element_type=F32)
    lane = lax.broadcasted_iota(I32, (1, A_HEADS * LANES), 1)
    ones_lane = (lane % LANES == A_VDIM).astype(F32)
    v_ref[...] = (jnp.dot(kvn, wuv_ref[...], preferred_element_type=F32) + ones_lane).astype(BF16)
    k_pe = mm(C_KR, LANES) * cos + mm(C_KRS, LANES) * sin
    scale = A_QK ** -0.5 * LOG2_E
    for h in range(A_HEADS):
        sl = slice(h * LANES, (h + 1) * LANES)
        q_ref[:, sl] = ((q[:, sl] * cos + qs[:, sl] * sin) * scale).astype(BF16)
        k_ref[:, sl] = (kn[:, sl] + k_pe).astype(BF16)


def _proj_weights(w_in, w_uq, w_ukv):
    o = IN_OFFSETS
    half = A_ROPE // 2
    w_q, w_kv, w_kr = w_in[:, o[0]:o[1]], w_in[:, o[1]:o[2]], w_in[:, o[2]:o[3]]
    w_xm, w_vm, w_op = w_in[:, o[3]:o[4]], w_in[:, o[4]:o[5]], w_in[:, o[5]:o[6]]
    w_i, w_f, w_gm, w_ga = w_in[:, o[6]:o[7]], w_in[:, o[7]:o[8]], w_in[:, o[8]:o[9]], w_in[:, o[9]:o[10]]
    d = w_in.shape[0]
    z = lambda n: jnp.zeros((d, n), w_in.dtype)
    kr = jnp.concatenate([z(A_NOPE), w_kr, z(LANES - A_QK)], axis=1)
    krs = jnp.concatenate([z(A_NOPE), -w_kr[:, half:], w_kr[:, :half], z(LANES - A_QK)], axis=1)
    gate = jnp.concatenate([w_i, w_f, z(LANES - 2 * M_HEADS)], axis=1)
    w_r = jnp.concatenate([w_q, w_kv, kr, krs, gate, w_xm, w_vm, w_op, w_gm, w_ga], axis=1).astype(BF16)

    uq = w_uq.reshape(A_Q_RANK, A_HEADS, A_QK)
    zq = jnp.zeros((A_Q_RANK, A_HEADS, LANES - A_QK), w_uq.dtype)
    zn = jnp.zeros((A_Q_RANK, A_HEADS, A_NOPE), w_uq.dtype)
    uq_pad = jnp.concatenate([uq, zq], axis=-1).reshape(A_Q_RANK, A_HEADS * LANES).astype(BF16)
    uqs_pad = jnp.concatenate([zn, -uq[..., A_NOPE + half:], uq[..., A_NOPE:A_NOPE + half], zq],
                              axis=-1).reshape(A_Q_RANK, A_HEADS * LANES).astype(BF16)
    ukv = w_ukv.reshape(A_KV_RANK, A_HEADS, A_NOPE + A_VDIM)
    zk = jnp.zeros((A_KV_RANK, A_HEADS, LANES - A_NOPE), w_ukv.dtype)
    uk_pad = jnp.concatenate([ukv[..., :A_NOPE], zk], axis=-1).reshape(A_KV_RANK, A_HEADS * LANES).astype(BF16)
    zv = jnp.zeros((A_KV_RANK, A_HEADS, LANES - A_VDIM), w_ukv.dtype)
    uv_pad = jnp.concatenate([ukv[..., A_NOPE:], zv], axis=-1).reshape(A_KV_RANK, A_HEADS * LANES).astype(BF16)
    return w_r, uq_pad, uqs_pad, uk_pad, uv_pad


def _input_projection(x2, pos2, w_in, g_qlat, g_kvlat, w_uq, w_ukv):
    t = x2.shape[0]
    tm = min(PROJ_ROWS, t)
    w_r, uq_pad, uqs_pad, uk_pad, uv_pad = _proj_weights(w_in, w_uq, w_ukv)
    half = A_ROPE // 2
    inv_freq = ROPE_THETA ** (-jnp.arange(half, dtype=F32) / half)
    invf = jnp.concatenate([jnp.zeros((A_NOPE,), F32), inv_freq, inv_freq,
                            jnp.zeros((LANES - A_QK,), F32)]).reshape(1, LANES)
    hw = A_HEADS * LANES
    row = lambda n: pl.BlockSpec((tm, n), lambda i: (i, 0))
    outs = [
        jax.ShapeDtypeStruct((t, hw), BF16), jax.ShapeDtypeStruct((t, hw), BF16), jax.ShapeDtypeStruct((t, hw), BF16),
        jax.ShapeDtypeStruct((t, M_WIDTH), BF16), jax.ShapeDtypeStruct((t, M_WIDTH), BF16),
        jax.ShapeDtypeStruct((t, M_WIDTH), BF16),
        jax.ShapeDtypeStruct((t, D_MODEL), BF16), jax.ShapeDtypeStruct((t, D_MODEL), BF16),
        jax.ShapeDtypeStruct((t, LANES), F32),
    ]
    return pl.pallas_call(
        _proj_kernel,
        out_shape=outs,
        grid=(t // tm,),
        in_specs=[row(D_MODEL), row(1), _const_spec(w_r.shape), _const_spec((1, A_Q_RANK)),
                  _const_spec((1, A_KV_RANK)), _const_spec(uq_pad.shape), _const_spec(uqs_pad.shape),
                  _const_spec(uk_pad.shape), _const_spec(uv_pad.shape), _const_spec((1, LANES))],
        out_specs=[row(hw), row(hw), row(hw), row(M_WIDTH), row(M_WIDTH), row(M_WIDTH), row(D_MODEL), row(D_MODEL),
                   row(LANES)],
        compiler_params=_cparams(("parallel",)),
        name="input_projection",
    )(x2, pos2, w_r, g_qlat.reshape(1, -1), g_kvlat.reshape(1, -1), uq_pad, uqs_pad, uk_pad, uv_pad, invf)


def _log_sigmoid(v):
    return jnp.minimum(v, 0.0) - jnp.log1p(jnp.exp(-jnp.abs(v)))


def _mlstm_kernel(xm_ref, vm_ref, op_ref, gate_ref, convw_ref, convb_ref, wqk_ref, gbias_ref, ghead_ref,
                  skip_ref, tril_ref, out_ref, xpad_s, ctn_s, m_s, *, seq_len):
    s = seq_len
    lc = M_CHUNK_ROWS
    halo = 8
    for bb in range(MLSTM_SEQS):
        xpad_s[bb, 0:halo, :] = jnp.zeros((halo, M_WIDTH), F32)
        xpad_s[bb, halo:, :] = xm_ref[bb * s:(bb + 1) * s, :].astype(F32)
    ctn_s[...] = jnp.zeros_like(ctn_s)
    m_s[...] = jnp.zeros_like(m_s)
    rows = lax.broadcasted_iota(I32, (lc, lc), 0)
    cols = lax.broadcasted_iota(I32, (lc, lc), 1)
    causal = rows >= cols
    row_id = lax.broadcasted_iota(I32, (lc, LANES), 0)
    ones_blk = (lax.broadcasted_iota(I32, (lc, LANES), 1) == 0).astype(BF16)
    kscale = M_HEAD_DIM ** -0.5
    hsl = [slice(h * M_HEAD_DIM, (h + 1) * M_HEAD_DIM) for h in range(M_HEADS)]
    hcl = [slice(h, h + 1) for h in range(M_HEADS)]
    streams = [(bb, h) for bb in range(MLSTM_SEQS) for h in range(M_HEADS)]

    def gates(bb, r0):
        win = xpad_s[bb, pl.ds(r0 - bb * s, lc + halo), :]
        conv = convb_ref[...]
        for j in range(M_CONV):
            lo = halo - (M_CONV - 1) + j
            conv = conv + win[lo:lo + lc, :] * convw_ref[j:j + 1, :]
        xc = conv * jax.nn.sigmoid(conv)
        g = gate_ref[pl.ds(r0, lc), :] + gbias_ref[...]
        ls = _log_sigmoid(g)
        bc = jnp.dot(tril_ref[...], ls, preferred_element_type=F32, precision=lax.Precision.HIGHEST)
        b0 = pltpu.roll(bc, LANES - M_HEADS, axis=1)
        u = g - b0
        cm = u
        shift = 1
        while shift < lc:
            cm = jnp.maximum(cm, jnp.where(row_id >= shift, pltpu.roll(cm, shift, axis=0), -jnp.inf))
            shift *= 2
        m_prev = m_s[bb:bb + 1, :]
        g_inter = b0 + m_prev
        m = jnp.maximum(g_inter, b0 + cm)
        b_tot = b0[lc - 1:lc, :]
        aw = b_tot - b0 + g
        m_chunk = jnp.max(aw, axis=0, keepdims=True)
        m_new = jnp.maximum(b_tot + m_prev, m_chunk)
        m_s[bb:bb + 1, :] = m_new
        return dict(xc=xc, w_inter=jnp.exp(g_inter - m), e_negm=jnp.exp(-m), a_mat=b0 - m, wa=jnp.exp(aw - m_chunk),
                    s_old=jnp.exp(b_tot + m_prev - m_new), s_new=jnp.exp(m_chunk - m_new), u_t=u.T)

    def chunk(c, carry):
        r0s = [pl.multiple_of(bb * s + c * lc, lc) for bb in range(MLSTM_SEQS)]
        gs = [gates(bb, r0s[bb]) for bb in range(MLSTM_SEQS)]
        qk = [jnp.dot(gs[bb]["xc"].astype(BF16), wqk_ref[...], preferred_element_type=F32)
              for bb in range(MLSTM_SEQS)]
        q_b = [qk[bb][:, hsl[h]].astype(BF16) for bb, h in streams]
        k_f = [qk[bb][:, M_WIDTH + h * M_HEAD_DIM:M_WIDTH + (h + 1) * M_HEAD_DIM] * kscale for bb, h in streams]
        v_aug = [jnp.concatenate([vm_ref[pl.ds(r0s[bb], lc), hsl[h]], ones_blk], axis=1) for bb, h in streams]
        ctn_prev = [ctn_s[bb * M_HEADS + h] for bb, h in streams]
        s_raw = [lax.dot_general(q_b[i], k_f[i].astype(BF16), (((1,), (1,)), ((), ())), preferred_element_type=F32)
                 for i in range(len(streams))]
        inter = [jnp.dot(q_b[i], ctn_prev[i].astype(BF16), preferred_element_type=F32)
                 for i in range(len(streams))]
        ctn_c = [lax.dot_general((k_f[i] * gs[bb]["wa"][:, hcl[h]]).astype(BF16), v_aug[i], (((0,), (0,)), ((), ())),
                                 preferred_element_type=F32) for i, (bb, h) in enumerate(streams)]
        sc_b = [(s_raw[i] * jnp.exp(jnp.where(causal, gs[bb]["a_mat"][:, hcl[h]] + gs[bb]["u_t"][hcl[h], :],
                                              -jnp.inf))).astype(BF16) for i, (bb, h) in enumerate(streams)]
        intra = [jnp.dot(sc_b[i], v_aug[i], preferred_element_type=F32) for i in range(len(streams))]
        for i, (bb, h) in enumerate(streams):
            g = gs[bb]
            ctn_s[bb * M_HEADS + h] = g["s_old"][:, hcl[h]] * ctn_prev[i] + g["s_new"][:, hcl[h]] * ctn_c[i]
            wi = g["w_inter"][:, hcl[h]]
            num = wi * inter[i][:, :M_HEAD_DIM] + intra[i][:, :M_HEAD_DIM]
            den = wi * inter[i][:, M_HEAD_DIM:M_HEAD_DIM + 1] + intra[i][:, M_HEAD_DIM:M_HEAD_DIM + 1]
            hh = num / jnp.maximum(jnp.abs(den), g["e_negm"][:, hcl[h]])
            mu = jnp.mean(hh, axis=-1, keepdims=True)
            dv = hh - mu
            var = jnp.mean(dv * dv, axis=-1, keepdims=True)
            hn = dv * lax.rsqrt(var + EPS) * ghead_ref[:, hsl[h]]
            y = (jax.nn.sigmoid(op_ref[pl.ds(r0s[bb], lc), hsl[h]].astype(F32))
                 * (hn + skip_ref[:, hsl[h]] * g["xc"][:, hsl[h]]))
            out_ref[pl.ds(r0s[bb], lc), hsl[h]] = y.astype(BF16)
        return carry

    lax.fori_loop(0, s // lc, chunk, 0)


def _mlstm_branch(xm, vm, op, gate, bsz, s, b_i, b_f, conv_w, conv_b, w_mq, w_mk, g_mhead, w_mskip):
    lc = M_CHUNK_ROWS
    gbias = jnp.concatenate([b_i, b_f, jnp.zeros((LANES - 2 * M_HEADS,), F32)]).reshape(1, LANES)
    tril = jnp.tril(jnp.ones((lc, lc), F32))
    eye = jnp.eye(M_HEADS, dtype=F32)
    block_diag = lambda w: jnp.einsum('hde,hg->hdge', w, eye).reshape(M_WIDTH, M_WIDTH)
    w_qk = jnp.concatenate([block_diag(w_mq), block_diag(w_mk)], axis=1).astype(BF16)
    nseq = MLSTM_SEQS
    assert bsz % nseq == 0
    seq = lambda n: pl.BlockSpec((nseq * s, n), lambda b: (b, 0))
    return pl.pallas_call(
        functools.partial(_mlstm_kernel, seq_len=s),
        out_shape=jax.ShapeDtypeStruct((bsz * s, M_WIDTH), BF16),
        grid=(bsz // nseq,),
        in_specs=[seq(M_WIDTH), seq(M_WIDTH), seq(M_WIDTH), seq(LANES),
                  _const_spec((M_CONV, M_WIDTH)), _const_spec((1, M_WIDTH)),
                  _const_spec((M_WIDTH, 2 * M_WIDTH)),
                  _const_spec((1, LANES)), _const_spec((1, M_WIDTH)), _const_spec((1, M_WIDTH)),
                  _const_spec((lc, lc))],
        out_specs=seq(M_WIDTH),
        scratch_shapes=[pltpu.VMEM((nseq, s + 8, M_WIDTH), F32),
                        pltpu.VMEM((nseq * M_HEADS, M_HEAD_DIM, 2 * LANES), F32), pltpu.VMEM((8, LANES), F32)],
        compiler_params=_cparams(("parallel",)),
        name="mlstm_branch",
    )(xm, vm, op, gate, conv_w, conv_b.reshape(1, -1), w_qk, gbias, g_mhead.reshape(1, -1), w_mskip.reshape(1, -1),
      tril)


def _attn_kernel(q_ref, k_ref, v_ref, o_ref, s_scr, m_s, acc_s):
    tq = q_ref.shape[0]
    i = pl.program_id(1)
    rows = lax.broadcasted_iota(I32, (tq, tq), 0)
    cols = lax.broadcasted_iota(I32, (tq, tq), 1)

    def lane_tile_max(sc):
        out = sc[:, :LANES]
        for t in range(1, tq // LANES):
            out = jnp.maximum(out, sc[:, t * LANES:(t + 1) * LANES])
        return out

    heads = [slice(h * LANES, (h + 1) * LANES) for h in range(A_HEADS)]
    m_s[...] = jnp.full(m_s.shape, NEG_BIG, F32)
    acc_s[...] = jnp.zeros_like(acc_s)

    def pass_a(j, diagonal):
        k0 = pl.multiple_of(j * tq, tq)
        for h, hs in enumerate(heads):
            sc = lax.dot_general(q_ref[:, hs], k_ref[pl.ds(k0, tq), hs], (((1,), (1,)), ((), ())),
                                 preferred_element_type=F32)
            if diagonal:
                sc = jnp.where(cols <= rows, sc, NEG_BIG)
            s_scr[h, j] = sc
            m_s[h] = jnp.maximum(m_s[h], lane_tile_max(sc))

    def for_blocks(n, fn):
        def pair(p, c):
            fn(2 * p)
            fn(2 * p + 1)
            return c

        lax.fori_loop(0, n // 2, pair, 0)

        @pl.when(n % 2 == 1)
        def _():
            fn(n - 1)

    for_blocks(i, lambda j: pass_a(j, False))
    pass_a(i, True)
    for h in range(A_HEADS):
        m_s[h] = jnp.broadcast_to(jnp.max(m_s[h], axis=1, keepdims=True), (tq, LANES))

    def pass_b(j):
        k0 = pl.multiple_of(j * tq, tq)
        for h, hs in enumerate(heads):
            m_row = m_s[h]
            p = jnp.exp2(s_scr[h, j] - jnp.concatenate([m_row] * (tq // LANES), axis=1)).astype(BF16)
            acc_s[h] += jnp.dot(p, v_ref[pl.ds(k0, tq), hs], preferred_element_type=F32)

    for_blocks(i + 1, pass_b)
    for h, hs in enumerate(heads):
        acc = acc_s[h]
        o_ref[:, hs] = (acc / acc[:, A_VDIM:A_VDIM + 1]).astype(BF16)


def _mla_attention(q, k, v, bsz, s):
    tq = min(ATT_Q, s)
    nq = s // tq
    hw = A_HEADS * LANES
    return pl.pallas_call(
        _attn_kernel,
        out_shape=jax.ShapeDtypeStruct(q.shape, BF16),
        grid=(bsz, nq),
        in_specs=[pl.BlockSpec((tq, hw), lambda b, i: (b * nq + i, 0)),
                  pl.BlockSpec((s, hw), lambda b, i: (b, 0)),
                  pl.BlockSpec((s, hw), lambda b, i: (b, 0))],
        out_specs=pl.BlockSpec((tq, hw), lambda b, i: (b * nq + i, 0)),
        scratch_shapes=[pltpu.VMEM((A_HEADS, nq, tq, tq), F32), pltpu.VMEM((A_HEADS, tq, LANES), F32),
                        pltpu.VMEM((A_HEADS, tq, LANES), F32)],
        compiler_params=_cparams(("parallel", "arbitrary")),
        name="mla_attention",
    )(q, k, v)


def _merge_kernel(ym_ref, oa_ref, gm_ref, ga_ref, x_ref, wbm_ref, wba_ref, wmix_ref, g_ref, b_ref, h_ref):
    y_m = jnp.dot(ym_ref[...], wbm_ref[...], preferred_element_type=F32)
    y_a = jnp.dot(oa_ref[...], wba_ref[...], preferred_element_type=F32)
    mixed = jax.nn.sigmoid(gm_ref[...].astype(F32)) * y_m + jax.nn.sigmoid(ga_ref[...].astype(F32)) * y_a
    mix = jnp.dot(mixed.astype(BF16), wmix_ref[...], preferred_element_type=F32)
    h_ref[...] = _layer_norm_rows(DN_ALPHA * x_ref[...] + mix, g_ref[...], b_ref[...])


def _merge(ym, oa, gm, ga, x2, w_br_m, w_br_a, w_mix_out, ln_g, ln_b):
    t = x2.shape[0]
    tm = min(MERGE_ROWS, t)
    wba = jnp.concatenate([w_br_a.reshape(A_HEADS, A_VDIM, D_MODEL),
                           jnp.zeros((A_HEADS, LANES - A_VDIM, D_MODEL), w_br_a.dtype)], axis=1)
    wba = wba.reshape(A_HEADS * LANES, D_MODEL).astype(BF16)
    row = lambda n: pl.BlockSpec((tm, n), lambda i: (i, 0))
    return pl.pallas_call(
        _merge_kernel,
        out_shape=jax.ShapeDtypeStruct((t, D_MODEL), F32),
        grid=(t // tm,),
        in_specs=[row(M_WIDTH), row(A_HEADS * LANES), row(D_MODEL), row(D_MODEL), row(D_MODEL),
                  _const_spec((M_WIDTH, D_MODEL)), _const_spec((A_HEADS * LANES, D_MODEL)),
                  _const_spec((D_MODEL, D_MODEL)), _const_spec((1, D_MODEL)), _const_spec((1, D_MODEL))],
        out_specs=row(D_MODEL),
        compiler_params=_cparams(("parallel",)),
        name="merge_deepnorm1",
    )(ym, oa, gm, ga, x2, w_br_m.astype(BF16), wba, w_mix_out.astype(BF16), ln_g.reshape(1, -1), ln_b.reshape(1, -1))


def _kv_kernel(m_ref, w_ref, o_ref):
    o_ref[...] = jnp.dot(m_ref[...].astype(BF16), w_ref[...], preferred_element_type=F32).astype(BF16)


def _mem_kv(mem2, w_ckv):
    r = mem2.shape[0]
    tm = min(512, r)
    return pl.pallas_call(
        _kv_kernel,
        out_shape=jax.ShapeDtypeStruct((r, 2 * D_MODEL), BF16),
        grid=(r // tm,),
        in_specs=[pl.BlockSpec((tm, D_MODEL), lambda i: (i, 0)), _const_spec((D_MODEL, 2 * D_MODEL))],
        out_specs=pl.BlockSpec((tm, 2 * D_MODEL), lambda i: (i, 0)),
        compiler_params=_cparams(("parallel",)),
        name="memory_kv",
    )(mem2, w_ckv.astype(BF16))


def _xattn_kernel(h_ref, kv_ref, wq_ref, wo_ref, g_ref, b_ref, wr_ref, br_ref, *rest):
    h2_ref, idx_ref, gate_ref = rest[-3:]
    h1 = h_ref[...]
    q = jnp.dot(h1.astype(BF16), wq_ref[...], preferred_element_type=F32).astype(BF16)
    scale = X_HEAD_DIM ** -0.5
    ksl = [slice(hd * X_HEAD_DIM, (hd + 1) * X_HEAD_DIM) for hd in range(X_HEADS)]
    vsl = [slice(D_MODEL + hd * X_HEAD_DIM, D_MODEL + (hd + 1) * X_HEAD_DIM) for hd in range(X_HEADS)]
    scs = [lax.dot_general(q[:, ks], kv_ref[:, ks], (((1,), (1,)), ((), ())), preferred_element_type=F32) * scale
           for ks in ksl]
    ps = [jnp.exp(sc - jnp.max(sc, axis=1, keepdims=True)) for sc in scs]
    ps = [(p / jnp.sum(p, axis=1, keepdims=True)).astype(BF16) for p in ps]
    o = jnp.concatenate([jnp.dot(p, kv_ref[:, vs], preferred_element_type=F32).astype(BF16)
                         for p, vs in zip(ps, vsl)], axis=1)
    att = jnp.dot(o, wo_ref[...], preferred_element_type=F32)
    h2 = _layer_norm_rows(DN_ALPHA * h1 + att, g_ref[...], b_ref[...])
    for c in range(ROW_TILES):
        h2_ref[pl.ds(c, h2.shape[0], stride=ROW_TILES), :] = h2[:, c * LANES:(c + 1) * LANES]

    h2_hi = h2.astype(BF16)
    h2_lo = (h2 - h2_hi.astype(F32)).astype(BF16)
    hi_prod = jnp.dot(h2_hi, wr_ref[...], preferred_element_type=F32)
    lo_prod = jnp.dot(h2_lo, wr_ref[:, :LANES], preferred_element_type=F32)
    logits = hi_prod[:, :LANES] + (hi_prod[:, LANES:] + lo_prod) + br_ref[...]
    tm = logits.shape[0]
    lane = lax.broadcasted_iota(I32, (tm, LANES), 1)
    lane_f = lane.astype(F32)
    work = jnp.where(lane < N_EXPERTS, logits, -jnp.inf)
    vals, idxs = [], []
    for _ in range(TOP_K):
        mx = jnp.max(work, axis=1, keepdims=True)
        ix = jnp.min(jnp.where(work == mx, lane_f, float(LANES)), axis=1, keepdims=True)
        vals.append(mx)
        idxs.append(ix)
        work = jnp.where(lane_f == ix, -jnp.inf, work)
    es = [jnp.exp(vv - vals[0]) for vv in vals]
    tot = es[0] + es[1] + es[2] + es[3]
    idx_slab = jnp.zeros((tm, LANES), F32)
    gate_slab = jnp.zeros((tm, LANES), F32)
    for kk in range(TOP_K):
        idx_slab = jnp.where(lane == kk, idxs[kk], idx_slab)
        gate_slab = jnp.where(lane == kk, es[kk] / tot, gate_slab)
    idx_ref[...] = idx_slab.astype(I32)
    gate_ref[...] = gate_slab


def _cross_attention_router(h1, kv, bsz, s, w_cq, w_co, ln_g, ln_b, w_router, b_router, part, after):
    n_mem = kv.shape[0] // bsz
    bsz = bsz // MOE_PARTS
    b0 = part * bsz
    t = bsz * s
    tm = min(XATT_ROWS, s)
    ns = s // tm
    wr = jnp.concatenate([w_router, jnp.zeros((D_MODEL, LANES - N_EXPERTS), F32)], axis=1)
    wr_hi = wr.astype(BF16)
    wr = jnp.concatenate([wr_hi, (wr - wr_hi.astype(F32)).astype(BF16)], axis=1)
    br = jnp.concatenate([b_router, jnp.zeros((LANES - N_EXPERTS,), F32)]).reshape(1, LANES)
    row = lambda n: pl.BlockSpec((tm, n), lambda b, i: (b * ns + i, 0))
    h1_rows = pl.BlockSpec((tm, D_MODEL), lambda b, i: ((b0 + b) * ns + i, 0))
    in_specs = [h1_rows, pl.BlockSpec((n_mem, 2 * D_MODEL), lambda b, i: (b0 + b, 0)),
                _const_spec((D_MODEL, D_MODEL)), _const_spec((D_MODEL, D_MODEL)),
                _const_spec((1, D_MODEL)), _const_spec((1, D_MODEL)),
                _const_spec((D_MODEL, 2 * LANES)), _const_spec((1, LANES))]
    args = [h1, kv, w_cq.astype(BF16), w_co.astype(BF16), ln_g.reshape(1, -1), ln_b.reshape(1, -1), wr, br]
    if after is not None:
        in_specs.append(pl.BlockSpec(memory_space=pl.ANY))
        args.append(after)
    return pl.pallas_call(
        _xattn_kernel,
        out_shape=[jax.ShapeDtypeStruct((t * ROW_TILES, LANES), F32), jax.ShapeDtypeStruct((t, LANES), I32),
                   jax.ShapeDtypeStruct((t, LANES), F32)],
        grid=(bsz, ns),
        in_specs=in_specs,
        out_specs=[pl.BlockSpec((tm * ROW_TILES, LANES), lambda b, i: (b * ns + i, 0)), row(LANES), row(LANES)],
        compiler_params=_cparams(("parallel", "parallel")),
        name="cross_attention_router",
    )(*args)


def _rank_kernel(idx_ref, ltri_ref, rank_ref, cnt_ref, carry_s):
    @pl.when(pl.program_id(0) == 0)
    def _():
        carry_s[...] = jnp.zeros_like(carry_s)

    idx = idx_ref[...]
    tm = idx.shape[0]
    lane = lax.broadcasted_iota(I32, (tm, LANES), 1)
    onehots = [(lane == idx[:, kk:kk + 1]).astype(F32) for kk in range(TOP_K)]
    sel = onehots[0] + onehots[1] + onehots[2] + onehots[3]
    before = jnp.dot(ltri_ref[...], sel.astype(BF16), preferred_element_type=F32) + carry_s[0:1, :]
    rank_slab = jnp.zeros((tm, LANES), F32)
    for kk in range(TOP_K):
        r = jnp.sum(onehots[kk] * before, axis=1, keepdims=True)
        rank_slab = jnp.where(lane == kk, r, rank_slab)
    rank_ref[...] = rank_slab.astype(I32)
    carry_s[0:1, :] = carry_s[0:1, :] + jnp.sum(sel, axis=0, keepdims=True)
    cnt_ref[...] = jnp.broadcast_to(carry_s[0:1, :], cnt_ref.shape).astype(I32)


def _routing_ranks(idx_slab):
    t = idx_slab.shape[0]
    tm = min(RANK_ROWS, t)
    ltri = jnp.tril(jnp.ones((tm, tm), BF16), k=-1)
    return pl.pallas_call(
        _rank_kernel,
        out_shape=[jax.ShapeDtypeStruct((t, LANES), I32), jax.ShapeDtypeStruct((8, LANES), I32)],
        grid=(t // tm,),
        in_specs=[pl.BlockSpec((tm, LANES), lambda i: (i, 0)), _const_spec((tm, tm))],
        out_specs=[pl.BlockSpec((tm, LANES), lambda i: (i, 0)), _const_spec((8, LANES))],
        scratch_shapes=[pltpu.VMEM((8, LANES), F32)],
        compiler_params=_cparams(("arbitrary",)),
        name="routing_ranks",
    )(idx_slab, ltri)


def _dest_kernel(idx_ref, rank_ref, pstart_ref, dest_ref):
    idx = idx_ref[...]
    tm = idx.shape[0]
    lane = lax.broadcasted_iota(I32, (tm, LANES), 1)
    dest = rank_ref[...].astype(F32)
    pstart = pstart_ref[0:1, :].astype(F32)
    for kk in range(TOP_K):
        start = jnp.sum(jnp.where(lane == idx[:, kk:kk + 1], pstart, 0.0), axis=1, keepdims=True)
        dest = dest + jnp.where(lane == kk, start, 0.0)
    dest_ref[...] = dest.T[0:8, :].astype(I32)


def _dest_rows(idx_slab, rank_slab, pstart_row):
    t = rank_slab.shape[0]
    tm = min(RANK_ROWS, t)
    return pl.pallas_call(
        _dest_kernel,
        out_shape=jax.ShapeDtypeStruct((8, t), I32),
        grid=(t // tm,),
        in_specs=[pl.BlockSpec((tm, LANES), lambda i: (i, 0)), pl.BlockSpec((tm, LANES), lambda i: (i, 0)),
                  _const_spec((8, LANES))],
        out_specs=pl.BlockSpec((8, tm), lambda i: (0, i)),
        compiler_params=_cparams(("parallel",)),
        name="routing_dest",
    )(idx_slab, rank_slab, pstart_row)


def _sc_mesh():
    return plsc.VectorSubcoreMesh(core_axis_name="c", subcore_axis_name="s", num_cores=SC_CORES,
                                  num_subcores=SC_SUBCORES)


def _sc_worker():
    return lax.axis_index("s") * SC_CORES + lax.axis_index("c")


def _dispatch(h2, dest_kt, n_rows):
    t = dest_kt.shape[1]
    n = TOP_K * t
    w, lanes = SC_ROWS, SC_LANES
    nsub = w // lanes
    per_w = n_rows // SC_WORKERS
    steps = per_w // w
    chunk = 4096
    assert steps % 2 == 0 and steps * w * SC_WORKERS == n_rows and n % chunk == 0 and t & (t - 1) == 0
    dest_flat = dest_kt.reshape(n)

    def body(x_hbm, dest_hbm, o_hbm, src_v, dst_v, rows_v, sem_g, sem_w):
        wid = _sc_worker()
        row0 = wid * per_w
        lane = lax.iota(I32, lanes)

        @pl.loop(0, per_w // lanes)
        def _(i):
            src_v[i, :] = jnp.zeros((lanes,), I32)

        @pl.loop(0, n // chunk)
        def _(c):
            pltpu.sync_copy(dest_hbm.at[pl.ds(c * chunk, chunk)], dst_v)

            @pl.loop(0, chunk // lanes)
            def _(j):
                local = dst_v[pl.ds(j * lanes, lanes)] - row0
                mine = (local >= 0) & (local < per_w)
                local = jnp.where(mine, local, 0)
                tok = (c * chunk + j * lanes + lane) & (t - 1)
                plsc.store_scatter(src_v, [local >> 4, local & (lanes - 1)], tok, mask=mine)

        def gathers(step, b):
            return [pltpu.make_async_copy(x_hbm.at[src_v.at[step * nsub + c]], rows_v.at[b, pl.ds(c * lanes, lanes)],
                                          sem_g.at[b]) for c in range(nsub)]

        def write(step, b):
            return pltpu.make_async_copy(rows_v.at[b], o_hbm.at[pl.ds(row0 + step * w, w)], sem_w.at[b])

        for cp in gathers(0, 0):
            cp.start()

        @pl.loop(0, steps, step=2)
        def _(s0):
            for b in range(2):
                step = s0 + b
                for cp in gathers(step, b):
                    cp.wait()

                @pl.when(step >= 1)
                def _():
                    write(step - 1, 1 - b).wait()

                @pl.when(step + 1 < steps)
                def _():
                    for cp in gathers(step + 1, 1 - b):
                        cp.start()

                write(step, b).start()

        write(steps - 1, 1).wait()

    return pl.kernel(
        body,
        out_type=jax.ShapeDtypeStruct((n_rows, ROW_TILES, LANES), F32),
        mesh=_sc_mesh(),
        scratch_types=[pltpu.VMEM((per_w // lanes, lanes), I32), pltpu.VMEM((chunk,), I32),
                       pltpu.VMEM((2, w, ROW_TILES, LANES), F32), pltpu.SemaphoreType.DMA((2,)),
                       pltpu.SemaphoreType.DMA((2,))],
        compiler_params=pltpu.CompilerParams(needs_layout_passes=False),
        name="moe_dispatch_sc",
    )(h2, dest_flat)


def _gather_rows(yb, dest_kt):
    n = dest_kt.shape[0]
    w, sub = SC_ROWS, SC_SUB
    nsub = w // sub
    per_w = n // SC_WORKERS
    steps = per_w // w
    assert steps % 2 == 0 and steps * w * SC_WORKERS == n
    dest_w = dest_kt.reshape(SC_WORKERS, steps * nsub, sub)

    def body(y_hbm, dest_hbm, o_hbm, idx_v, rows_v, sem_g, sem_w):
        wid = _sc_worker()
        row0 = wid * per_w
        pltpu.sync_copy(dest_hbm.at[wid], idx_v)

        def gathers(step, b):
            return [pltpu.make_async_copy(y_hbm.at[idx_v.at[step * nsub + c]], rows_v.at[b, pl.ds(c * sub, sub)],
                                          sem_g.at[b]) for c in range(nsub)]

        def write(step, b):
            return pltpu.make_async_copy(rows_v.at[b], o_hbm.at[pl.ds(row0 + step * w, w)], sem_w.at[b])

        for cp in gathers(0, 0):
            cp.start()

        @pl.loop(0, steps, step=2)
        def _(s0):
            for b in range(2):
                step = s0 + b
                for cp in gathers(step, b):
                    cp.wait()

                @pl.when(step >= 1)
                def _():
                    write(step - 1, 1 - b).wait()

                @pl.when(step + 1 < steps)
                def _():
                    for cp in gathers(step + 1, 1 - b):
                        cp.start()

                write(step, b).start()

        write(steps - 1, 1).wait()

    return pl.kernel(
        body,
        out_type=jax.ShapeDtypeStruct((n, ROW_TILES, LANES), F32),
        mesh=_sc_mesh(),
        scratch_types=[pltpu.VMEM((steps * nsub, sub), I32), pltpu.VMEM((2, w, ROW_TILES, LANES), F32),
                       pltpu.SemaphoreType.DMA((2,)), pltpu.SemaphoreType.DMA((2,))],
        name="moe_gather_sc",
    )(yb, dest_w)


def _rows_2d(ref, n):
    return jnp.concatenate([ref[pl.ds(c, n, stride=ROW_TILES), :] for c in range(ROW_TILES)], axis=1)


def _store_rows(ref, val):
    for c in range(ROW_TILES):
        ref[pl.ds(c, val.shape[0], stride=ROW_TILES), :] = val[:, c * LANES:(c + 1) * LANES]


def _expert_kernel(blk_e_ref, first_ref, slot_ref, next_ref, nblk_ref, x_ref, wg_hbm, wl_hbm, bg_ref, bl_ref, wd_hbm,
                   bd_ref, y_ref, wg_buf, wl_buf, wd_stage, wd_buf, sem):
    i = pl.program_id(0)

    def fetch(e, sl):
        return [pltpu.make_async_copy(src.at[e], dst.at[sl], sem.at[sl, j])
                for j, (src, dst) in enumerate(((wg_hbm, wg_buf), (wl_hbm, wl_buf), (wd_hbm, wd_stage)))]

    @pl.when(i < nblk_ref[0])
    def _():
        sl = slot_ref[i]

        @pl.when(first_ref[i] == 1)
        def _():
            @pl.when(i == 0)
            def _():
                for cp in fetch(blk_e_ref[0], sl):
                    cp.start()

            for cp in fetch(blk_e_ref[i], sl):
                cp.wait()
            wd_buf[sl] = wd_stage[sl].astype(BF16)

            @pl.when(next_ref[i] >= 0)
            def _():
                for cp in fetch(next_ref[i], 1 - sl):
                    cp.start()

        xb = _rows_2d(x_ref, y_ref.shape[0] // ROW_TILES).astype(BF16)
        glu = jnp.dot(xb, wg_buf[sl], preferred_element_type=F32) + bg_ref[...]
        lin = jnp.dot(xb, wl_buf[sl], preferred_element_type=F32) + bl_ref[...]
        glu = jnp.minimum(glu, SWIGLU_LIMIT)
        lin = jnp.clip(lin, -SWIGLU_LIMIT, SWIGLU_LIMIT)
        act = glu * jax.nn.sigmoid(SWIGLU_ALPHA * glu) * (lin + 1.0)
        _store_rows(y_ref, jnp.dot(act.astype(BF16), wd_buf[sl], preferred_element_type=F32) + bd_ref[...])

    @pl.when(i >= nblk_ref[0])
    def _():
        y_ref[...] = jnp.zeros_like(y_ref)


def _split_kernel(w_ref, perm_ref, after_ref, g_ref, l_ref):
    del after_ref
    half = g_ref.shape[-1]
    sorted_cols = jnp.dot(w_ref[...].astype(BF16), perm_ref[...], preferred_element_type=F32)
    g_ref[...] = sorted_cols[:, :half].astype(BF16)
    l_ref[...] = sorted_cols[:, half:].astype(BF16)


def _split_gate_up(w_gu, after):
    e, d, n2 = w_gu.shape
    cw = 512
    src = np.concatenate([np.arange(0, cw, 2), np.arange(1, cw, 2)])
    perm = jnp.asarray(np.eye(cw, dtype=np.float32)[:, src], BF16)
    return pl.pallas_call(
        _split_kernel,
        out_shape=[jax.ShapeDtypeStruct((e, d, n2 // 2), BF16), jax.ShapeDtypeStruct((e, d, n2 // 2), BF16)],
        grid=(e, n2 // cw),
        in_specs=[pl.BlockSpec((None, d, cw), lambda i, c: (i, 0, c)), _const_spec((cw, cw)),
                  pl.BlockSpec(memory_space=pl.ANY)],
        out_specs=[pl.BlockSpec((None, d, cw // 2), lambda i, c: (i, 0, c)),
                   pl.BlockSpec((None, d, cw // 2), lambda i, c: (i, 0, c))],
        compiler_params=_cparams(("parallel", "parallel")),
        name="split_gate_up",
    )(w_gu, perm, after)


def _experts(xb, blk_meta, w_glu, w_lin, b_glu, b_lin, w_dn, b_dn):
    n_rows = xb.shape[0] // ROW_TILES
    bm = EXPERT_ROWS
    vec = lambda n: pl.BlockSpec((None, 1, n), lambda i, be, *_: (be[i], 0, 0))
    rows = pl.BlockSpec((bm * ROW_TILES, LANES), lambda i, *_: (i, 0))
    hbm = pl.BlockSpec(memory_space=pl.ANY)
    grid_spec = pltpu.PrefetchScalarGridSpec(
        num_scalar_prefetch=5,
        grid=(n_rows // bm,),
        in_specs=[rows, hbm, hbm, vec(D_EXPERT), vec(D_EXPERT), hbm, vec(D_MODEL)],
        out_specs=rows,
        scratch_shapes=[pltpu.VMEM((2, D_MODEL, D_EXPERT), BF16), pltpu.VMEM((2, D_MODEL, D_EXPERT), BF16),
                        pltpu.VMEM((2, D_EXPERT, D_MODEL), F32), pltpu.VMEM((2, D_EXPERT, D_MODEL), BF16),
                        pltpu.SemaphoreType.DMA((2, 3))],
    )
    return pl.pallas_call(
        _expert_kernel,
        out_shape=jax.ShapeDtypeStruct((n_rows * ROW_TILES, LANES), F32),
        grid_spec=grid_spec,
        compiler_params=_cparams(("arbitrary",)),
        name="moe_experts",
    )(*blk_meta, xb, w_glu, w_lin, b_glu, b_lin, w_dn, b_dn)


def _combine_kernel(h_ref, gate_ref, y0_ref, y1_ref, y2_ref, y3_ref, g_ref, b_ref, *rest):
    o_ref = rest[-1]
    tt = o_ref.shape[0]
    gates = gate_ref[...]
    acc = DN_ALPHA * _rows_2d(h_ref, tt)
    for kk, y_ref in enumerate((y0_ref, y1_ref, y2_ref, y3_ref)):
        acc = acc + gates[:, kk:kk + 1] * _rows_2d(y_ref, tt)
    o_ref[...] = _layer_norm_rows(acc, g_ref[...], b_ref[...])


def _combine(h2, gate_slab, yg, ln_g, ln_b, part, prev_out):
    t = h2.shape[0] // ROW_TILES
    t_all = t * MOE_PARTS
    tt = min(COMBINE_TOKENS, t)
    steps = t // tt
    first = part * steps
    row = lambda m: pl.BlockSpec((tt, m), lambda i: (i, 0))
    tiles = lambda kk: pl.BlockSpec((tt * ROW_TILES, LANES), lambda i: (kk * steps + i, 0))
    in_specs = [tiles(0), row(LANES),
                tiles(0), tiles(1), tiles(2), tiles(3), _const_spec((1, D_MODEL)), _const_spec((1, D_MODEL))]
    args = [h2, gate_slab, yg, yg, yg, yg, ln_g.reshape(1, -1), ln_b.reshape(1, -1)]
    aliases = {}
    if prev_out is not None:
        in_specs.append(pl.BlockSpec(memory_space=pl.ANY))
        args.append(prev_out)
        aliases = {len(args) - 1: 0}
    return pl.pallas_call(
        _combine_kernel,
        out_shape=jax.ShapeDtypeStruct((t_all, D_MODEL), F32),
        grid=(steps,),
        in_specs=in_specs,
        out_specs=pl.BlockSpec((tt, D_MODEL), lambda i: (first + i, 0)),
        input_output_aliases=aliases,
        compiler_params=_cparams(("parallel",)),
        name="moe_combine",
    )(*args)


def _route_and_dispatch(h2, idx_slab):
    t = h2.shape[0] // ROW_TILES
    bm = EXPERT_ROWS
    n_rows = t * TOP_K + N_EXPERTS * bm
    n_blocks = n_rows // bm

    rank_slab, cnt = _routing_ranks(idx_slab)
    counts = cnt[0, :N_EXPERTS]
    padded = (counts + bm - 1) // bm * bm
    pends = jnp.cumsum(padded)
    pstarts = pends - padded
    pstart_row = jnp.zeros((8, LANES), I32).at[:, :N_EXPERTS].set(pstarts[None, :])
    blk_start = jnp.arange(n_blocks, dtype=I32) * bm
    blk_e = jnp.minimum(jnp.sum((pends[None, :] <= blk_start[:, None]).astype(I32), axis=1), N_EXPERTS - 1)
    nblk = (pends[-1] // bm).astype(I32).reshape(1)
    active = counts > 0
    ordinal = jnp.cumsum(active.astype(I32)) - 1
    eid = jnp.arange(N_EXPERTS, dtype=I32)
    later = active[None, :] & (eid[None, :] > eid[:, None])
    next_active = jnp.where(later.any(axis=1), jnp.argmax(later, axis=1).astype(I32), -1)
    blk_first = jnp.concatenate([jnp.ones((1,), I32), (blk_e[1:] != blk_e[:-1]).astype(I32)])
    blk_onehot = (blk_e[:, None] == eid[None, :]).astype(I32)
    blk_slot = jnp.sum(blk_onehot * ordinal[None, :], axis=1) % 2
    blk_next = jnp.sum(blk_onehot * next_active[None, :], axis=1)
    blk_meta = (blk_e, blk_first, blk_slot, blk_next, nblk)
    dest_kt = _dest_rows(idx_slab, rank_slab, pstart_row)[:TOP_K]
    xb = _dispatch(h2.reshape(t, ROW_TILES, LANES), dest_kt, n_rows)
    return xb.reshape(n_rows * ROW_TILES, LANES), blk_meta, dest_kt, dest_kt


def _xattn_moe(h1, kv, bsz, s, w_cq, w_co, ln2_g, ln2_b, w_router, b_router, w_gu, b_gu, w_dn, b_dn, ln3_g, ln3_b):
    parts = []
    after = None
    for part in range(MOE_PARTS):
        h2, idx_slab, gate_slab = _cross_attention_router(h1, kv, bsz, s, w_cq, w_co, ln2_g, ln2_b, w_router,
                                                          b_router, part, after)
        xb, blk_meta, dest_kt, after = _route_and_dispatch(h2, idx_slab)
        parts.append((h2, gate_slab, xb, blk_meta, dest_kt))
    w_glu, w_lin = _split_gate_up(w_gu, after)
    expert_w = (w_glu, w_lin, b_gu[:, None, 0::2], b_gu[:, None, 1::2], w_dn, b_dn[:, None, :])
    out = None
    for part, (h2, gate_slab, xb, blk_meta, dest_kt) in enumerate(parts):
        n_rows = xb.shape[0] // ROW_TILES
        yb = _experts(xb, blk_meta, *expert_w)
        yg = _gather_rows(yb.reshape(n_rows, ROW_TILES, LANES), dest_kt.reshape(-1))
        out = _combine(h2, gate_slab, yg.reshape(-1, LANES), ln3_g, ln3_b, part, out)
    return out


def kernel(x, mem, positions, w_in, b_igate, b_fgate, conv_w, conv_b, w_mq, w_mk, g_mhead, w_mskip, g_qlat, g_kvlat,
           w_uq, w_ukv, w_br_m, w_br_a, w_mix_out, ln1_g, ln1_b, w_cq, w_ckv, w_co, ln2_g, ln2_b, w_router, b_router,
           w_gu, b_gu, w_dn, b_dn, ln3_g, ln3_b):
    bsz, s, d = x.shape
    t = bsz * s
    h = x.reshape(t, d)
    pos2 = positions.reshape(t, 1)
    for l in range(DEPTH):
        q, k, v, xm, vm, op, gm, ga, gate = _input_projection(h, pos2, w_in[l], g_qlat[l], g_kvlat[l], w_uq[l],
                                                              w_ukv[l])
        ym = _mlstm_branch(xm, vm, op, gate, bsz, s, b_igate[l], b_fgate[l], conv_w[l], conv_b[l], w_mq[l], w_mk[l],
                           g_mhead[l], w_mskip[l])
        oa = _mla_attention(q, k, v, bsz, s)
        h1 = _merge(ym, oa, gm, ga, h, w_br_m[l], w_br_a[l], w_mix_out[l], ln1_g[l], ln1_b[l])
        kv = _mem_kv(mem.reshape(-1, d), w_ckv[l])
        h = _xattn_moe(h1, kv, bsz, s, w_cq[l], w_co[l], ln2_g[l], ln2_b[l], w_router[l], b_router[l], w_gu[l],
                       b_gu[l], w_dn[l], b_dn[l], ln3_g[l], ln3_b[l])
    return h.reshape(bsz, s, d)
```

```python
import functools

import numpy as np
import jax
import jax.numpy as jnp
from jax import lax
from jax.experimental import pallas as pl
from jax.experimental.pallas import tpu as pltpu
from jax.experimental.pallas import tpu_sc as plsc

F32 = jnp.float32
BF16 = jnp.bfloat16
I32 = jnp.int32

D_MODEL = 1024
N_MEM = 256
M_HEADS = 4
M_HEAD_DIM = 128
M_WIDTH = M_HEADS * M_HEAD_DIM
M_CONV = 4
A_HEADS = 8
A_NOPE = 64
A_ROPE = 32
A_QK = A_NOPE + A_ROPE
A_VDIM = 64
A_Q_RANK = 256
A_KV_RANK = 128
ROPE_THETA = 10000.0
X_HEADS = 4
X_HEAD_DIM = D_MODEL // X_HEADS
N_EXPERTS = 32
TOP_K = 4
D_EXPERT = D_MODEL
SWIGLU_ALPHA = 1.702
SWIGLU_LIMIT = 7.0
DEPTH = 1
DN_ALPHA = (2.0 * DEPTH) ** 0.25
EPS = 1e-5
IN_SPLITS = (A_Q_RANK, A_KV_RANK, A_ROPE, M_WIDTH, M_WIDTH, M_WIDTH, M_HEADS, M_HEADS, D_MODEL, D_MODEL)
IN_OFFSETS = tuple(int(v) for v in np.cumsum((0,) + IN_SPLITS))

LANES = 128
VMEM_LIMIT = 56 * 1024 * 1024

PROJ_ROWS = 512
M_CHUNK_ROWS = 128
MLSTM_SEQS = 2
ATT_Q = 256
MERGE_ROWS = 512
XATT_ROWS = 512
RANK_ROWS = 1024
EXPERT_ROWS = 256
COMBINE_TOKENS = 256
MOE_PARTS = 2
SC_CORES = 2
SC_SUBCORES = 16
SC_WORKERS = SC_CORES * SC_SUBCORES
SC_ROWS = 32
NEG_BIG = -1e30
LOG2_E = 1.4426950408889634
ROW_TILES = D_MODEL // LANES

C_QLAT = 0
C_KVLAT = C_QLAT + A_Q_RANK
C_KR = C_KVLAT + A_KV_RANK
C_KRS = C_KR + LANES
C_GATE = C_KRS + LANES
C_XM = C_GATE + LANES
C_VM = C_XM + M_WIDTH
C_OP = C_VM + M_WIDTH
C_GM = C_OP + M_WIDTH
C_GA = C_GM + D_MODEL
C_END = C_GA + D_MODEL


def _cparams(sem, vmem=VMEM_LIMIT):
    return pltpu.CompilerParams(dimension_semantics=sem, vmem_limit_bytes=vmem)


def _const_spec(shape):
    nd = len(shape)
    return pl.BlockSpec(shape, lambda *a: (0,) * nd)


def _layer_norm_rows(v, g, b):
    mu = jnp.mean(v, axis=-1, keepdims=True)
    d = v - mu
    var = jnp.mean(d * d, axis=-1, keepdims=True)
    return d * lax.rsqrt(var + EPS) * g + b


def _proj_kernel(x_ref, pos_ref, w_ref, gq_ref, gkv_ref, wuq_ref, wuqs_ref, wuk_ref, wuv_ref, invf_ref,
                 q_ref, k_ref, v_ref, xm_ref, vm_ref, op_ref, gm_ref, ga_ref, gate_ref):
    xb = x_ref[...].astype(BF16)

    def mm(lo, n):
        return jnp.dot(xb, w_ref[:, lo:lo + n], preferred_element_type=F32)

    xm_ref[...] = mm(C_XM, M_WIDTH).astype(BF16)
    vm_ref[...] = mm(C_VM, M_WIDTH).astype(BF16)
    op_ref[...] = mm(C_OP, M_WIDTH).astype(BF16)
    gm_ref[...] = mm(C_GM, D_MODEL).astype(BF16)
    ga_ref[...] = mm(C_GA, D_MODEL).astype(BF16)
    gate_ref[...] = mm(C_GATE, LANES)

    ang = pos_ref[...].astype(F32) * invf_ref[...]
    cos = jnp.cos(ang)
    sin = jnp.sin(ang)

    q_lat = mm(C_QLAT, A_Q_RANK)
    qn = (q_lat * lax.rsqrt(jnp.mean(q_lat * q_lat, axis=-1, keepdims=True) + EPS) * gq_ref[...]).astype(BF16)
    q = jnp.dot(qn, wuq_ref[...], preferred_element_type=F32)
    qs = jnp.dot(qn, wuqs_ref[...], preferred_element_type=F32)
    kv_lat = mm(C_KVLAT, A_KV_RANK)
    kvn = (kv_lat * lax.rsqrt(jnp.mean(kv_lat * kv_lat, axis=-1, keepdims=True) + EPS) * gkv_ref[...]).astype(BF16)
    kn = jnp.dot(kvn, wuk_ref[...], preferred_element_type=F32)
    lane = lax.broadcasted_iota(I32, (1, A_HEADS * LANES), 1)
    ones_lane = (lane % LANES == A_VDIM).astype(F32)
    v_ref[...] = (jnp.dot(kvn, wuv_ref[...], preferred_element_type=F32) + ones_lane).astype(BF16)
    k_pe = mm(C_KR, LANES) * cos + mm(C_KRS, LANES) * sin
    scale = A_QK ** -0.5 * LOG2_E
    for h in range(A_HEADS):
        sl = slice(h * LANES, (h + 1) * LANES)
        q_ref[:, sl] = ((q[:, sl] * cos + qs[:, sl] * sin) * scale).astype(BF16)
        k_ref[:, sl] = (kn[:, sl] + k_pe).astype(BF16)


def _proj_weights(w_in, w_uq, w_ukv):
    o = IN_OFFSETS
    half = A_ROPE // 2
    w_q, w_kv, w_kr = w_in[:, o[0]:o[1]], w_in[:, o[1]:o[2]], w_in[:, o[2]:o[3]]
    w_xm, w_vm, w_op = w_in[:, o[3]:o[4]], w_in[:, o[4]:o[5]], w_in[:, o[5]:o[6]]
    w_i, w_f, w_gm, w_ga = w_in[:, o[6]:o[7]], w_in[:, o[7]:o[8]], w_in[:, o[8]:o[9]], w_in[:, o[9]:o[10]]
    d = w_in.shape[0]
    z = lambda n: jnp.zeros((d, n), w_in.dtype)
    kr = jnp.concatenate([z(A_NOPE), w_kr, z(LANES - A_QK)], axis=1)
    krs = jnp.concatenate([z(A_NOPE), -w_kr[:, half:], w_kr[:, :half], z(LANES - A_QK)], axis=1)
    gate = jnp.concatenate([w_i, w_f, z(LANES - 2 * M_HEADS)], axis=1)
    w_r = jnp.concatenate([w_q, w_kv, kr, krs, gate, w_xm, w_vm, w_op, w_gm, w_ga], axis=1).astype(BF16)

    uq = w_uq.reshape(A_Q_RANK, A_HEADS, A_QK)
    zq = jnp.zeros((A_Q_RANK, A_HEADS, LANES - A_QK), w_uq.dtype)
    zn = jnp.zeros((A_Q_RANK, A_HEADS, A_NOPE), w_uq.dtype)
    uq_pad = jnp.concatenate([uq, zq], axis=-1).reshape(A_Q_RANK, A_HEADS * LANES).astype(BF16)
    uqs_pad = jnp.concatenate([zn, -uq[..., A_NOPE + half:], uq[..., A_NOPE:A_NOPE + half], zq],
                              axis=-1).reshape(A_Q_RANK, A_HEADS * LANES).astype(BF16)
    ukv = w_ukv.reshape(A_KV_RANK, A_HEADS, A_NOPE + A_VDIM)
    zk = jnp.zeros((A_KV_RANK, A_HEADS, LANES - A_NOPE), w_ukv.dtype)
    uk_pad = jnp.concatenate([ukv[..., :A_NOPE], zk], axis=-1).reshape(A_KV_RANK, A_HEADS * LANES).astype(BF16)
    zv = jnp.zeros((A_KV_RANK, A_HEADS, LANES - A_VDIM), w_ukv.dtype)
    uv_pad = jnp.concatenate([ukv[..., A_NOPE:], zv], axis=-1).reshape(A_KV_RANK, A_HEADS * LANES).astype(BF16)
    return w_r, uq_pad, uqs_pad, uk_pad, uv_pad


def _input_projection(x2, pos2, w_in, g_qlat, g_kvlat, w_uq, w_ukv):
    t = x2.shape[0]
    tm = min(PROJ_ROWS, t)
    w_r, uq_pad, uqs_pad, uk_pad, uv_pad = _proj_weights(w_in, w_uq, w_ukv)
    half = A_ROPE // 2
    inv_freq = ROPE_THETA ** (-jnp.arange(half, dtype=F32) / half)
    invf = jnp.concatenate([jnp.zeros((A_NOPE,), F32), inv_freq, inv_freq,
                            jnp.zeros((LANES - A_QK,), F32)]).reshape(1, LANES)
    hw = A_HEADS * LANES
    row = lambda n: pl.BlockSpec((tm, n), lambda i: (i, 0))
    outs = [
        jax.ShapeDtypeStruct((t, hw), BF16), jax.ShapeDtypeStruct((t, hw), BF16), jax.ShapeDtypeStruct((t, hw), BF16),
        jax.ShapeDtypeStruct((t, M_WIDTH), BF16), jax.ShapeDtypeStruct((t, M_WIDTH), BF16),
        jax.ShapeDtypeStruct((t, M_WIDTH), BF16),
        jax.ShapeDtypeStruct((t, D_MODEL), BF16), jax.ShapeDtypeStruct((t, D_MODEL), BF16),
        jax.ShapeDtypeStruct((t, LANES), F32),
    ]
    return pl.pallas_call(
        _proj_kernel,
        out_shape=outs,
        grid=(t // tm,),
        in_specs=[row(D_MODEL), row(1), _const_spec(w_r.shape), _const_spec((1, A_Q_RANK)),
                  _const_spec((1, A_KV_RANK)), _const_spec(uq_pad.shape), _const_spec(uqs_pad.shape),
                  _const_spec(uk_pad.shape), _const_spec(uv_pad.shape), _const_spec((1, LANES))],
        out_specs=[row(hw), row(hw), row(hw), row(M_WIDTH), row(M_WIDTH), row(M_WIDTH), row(D_MODEL), row(D_MODEL),
                   row(LANES)],
        compiler_params=_cparams(("parallel",)),
        name="input_projection",
    )(x2, pos2, w_r, g_qlat.reshape(1, -1), g_kvlat.reshape(1, -1), uq_pad, uqs_pad, uk_pad, uv_pad, invf)


def _log_sigmoid(v):
    return jnp.minimum(v, 0.0) - jnp.log1p(jnp.exp(-jnp.abs(v)))


def _mlstm_kernel(xm_ref, vm_ref, op_ref, gate_ref, convw_ref, convb_ref, wqk_ref, gbias_ref, ghead_ref,
                  skip_ref, tril_ref, out_ref, xpad_s, ctn_s, m_s, *, seq_len):
    s = seq_len
    lc = M_CHUNK_ROWS
    halo = 8
    for bb in range(MLSTM_SEQS):
        xpad_s[bb, 0:halo, :] = jnp.zeros((halo, M_WIDTH), F32)
        xpad_s[bb, halo:, :] = xm_ref[bb * s:(bb + 1) * s, :].astype(F32)
    ctn_s[...] = jnp.zeros_like(ctn_s)
    m_s[...] = jnp.zeros_like(m_s)
    rows = lax.broadcasted_iota(I32, (lc, lc), 0)
    cols = lax.broadcasted_iota(I32, (lc, lc), 1)
    causal = rows >= cols
    row_id = lax.broadcasted_iota(I32, (lc, LANES), 0)
    ones_blk = (lax.broadcasted_iota(I32, (lc, LANES), 1) == 0).astype(BF16)
    kscale = M_HEAD_DIM ** -0.5
    hsl = [slice(h * M_HEAD_DIM, (h + 1) * M_HEAD_DIM) for h in range(M_HEADS)]
    hcl = [slice(h, h + 1) for h in range(M_HEADS)]
    streams = [(bb, h) for bb in range(MLSTM_SEQS) for h in range(M_HEADS)]

    def gates(bb, r0):
        win = xpad_s[bb, pl.ds(r0 - bb * s, lc + halo), :]
        conv = convb_ref[...]
        for j in range(M_CONV):
            lo = halo - (M_CONV - 1) + j
            conv = conv + win[lo:lo + lc, :] * convw_ref[j:j + 1, :]
        xc = conv * jax.nn.sigmoid(conv)
        g = gate_ref[pl.ds(r0, lc), :] + gbias_ref[...]
        ls = _log_sigmoid(g)
        bc = jnp.dot(tril_ref[...], ls, preferred_element_type=F32, precision=lax.Precision.HIGHEST)
        b0 = pltpu.roll(bc, LANES - M_HEADS, axis=1)
        u = g - b0
        cm = u
        shift = 1
        while shift < lc:
            cm = jnp.maximum(cm, jnp.where(row_id >= shift, pltpu.roll(cm, shift, axis=0), -jnp.inf))
            shift *= 2
        m_prev = m_s[bb:bb + 1, :]
        g_inter = b0 + m_prev
        m = jnp.maximum(g_inter, b0 + cm)
        b_tot = b0[lc - 1:lc, :]
        aw = b_tot - b0 + g
        m_chunk = jnp.max(aw, axis=0, keepdims=True)
        m_new = jnp.maximum(b_tot + m_prev, m_chunk)
        m_s[bb:bb + 1, :] = m_new
        return dict(xc=xc, w_inter=jnp.exp(g_inter - m), e_negm=jnp.exp(-m), a_mat=b0 - m, wa=jnp.exp(aw - m_chunk),
                    s_old=jnp.exp(b_tot + m_prev - m_new), s_new=jnp.exp(m_chunk - m_new), u_t=u.T)

    def chunk(c, carry):
        r0s = [pl.multiple_of(bb * s + c * lc, lc) for bb in range(MLSTM_SEQS)]
        gs = [gates(bb, r0s[bb]) for bb in range(MLSTM_SEQS)]
        qk = [jnp.dot(gs[bb]["xc"].astype(BF16), wqk_ref[...], preferred_element_type=F32)
              for bb in range(MLSTM_SEQS)]
        q_b = [qk[bb][:, hsl[h]].astype(BF16) for bb, h in streams]
        k_f = [qk[bb][:, M_WIDTH + h * M_HEAD_DIM:M_WIDTH + (h + 1) * M_HEAD_DIM] * kscale for bb, h in streams]
        v_aug = [jnp.concatenate([vm_ref[pl.ds(r0s[bb], lc), hsl[h]], ones_blk], axis=1) for bb, h in streams]
        ctn_prev = [ctn_s[bb * M_HEADS + h] for bb, h in streams]
        s_raw = [lax.dot_general(q_b[i], k_f[i].astype(BF16), (((1,), (1,)), ((), ())), preferred_element_type=F32)
                 for i in range(len(streams))]
        inter = [jnp.dot(q_b[i], ctn_prev[i].astype(BF16), preferred_element_type=F32)
                 for i in range(len(streams))]
        ctn_c = [lax.dot_general((k_f[i] * gs[bb]["wa"][:, hcl[h]]).astype(BF16), v_aug[i], (((0,), (0,)), ((), ())),
                                 preferred_element_type=F32) for i, (bb, h) in enumerate(streams)]
        sc_b = [(s_raw[i] * jnp.exp(jnp.where(causal, gs[bb]["a_mat"][:, hcl[h]] + gs[bb]["u_t"][hcl[h], :],
                                              -jnp.inf))).astype(BF16) for i, (bb, h) in enumerate(streams)]
        intra = [jnp.dot(sc_b[i], v_aug[i], preferred_element_type=F32) for i in range(len(streams))]
        for i, (bb, h) in enumerate(streams):
            g = gs[bb]
            ctn_s[bb * M_HEADS + h] = g["s_old"][:, hcl[h]] * ctn_prev[i] + g["s_new"][:, hcl[h]] * ctn_c[i]
            wi = g["w_inter"][:, hcl[h]]
            num = wi * inter[i][:, :M_HEAD_DIM] + intra[i][:, :M_HEAD_DIM]
            den = wi * inter[i][:, M_HEAD_DIM:M_HEAD_DIM + 1] + intra[i][:, M_HEAD_DIM:M_HEAD_DIM + 1]
            hh = num / jnp.maximum(jnp.abs(den), g["e_negm"][:, hcl[h]])
            mu = jnp.mean(hh, axis=-1, keepdims=True)
            dv = hh - mu
            var = jnp.mean(dv * dv, axis=-1, keepdims=True)
            hn = dv * lax.rsqrt(var + EPS) * ghead_ref[:, hsl[h]]
            y = (jax.nn.sigmoid(op_ref[pl.ds(r0s[bb], lc), hsl[h]].astype(F32))
                 * (hn + skip_ref[:, hsl[h]] * g["xc"][:, hsl[h]]))
            out_ref[pl.ds(r0s[bb], lc), hsl[h]] = y.astype(BF16)
        return carry

    lax.fori_loop(0, s // lc, chunk, 0)


def _mlstm_branch(xm, vm, op, gate, bsz, s, b_i, b_f, conv_w, conv_b, w_mq, w_mk, g_mhead, w_mskip):
    lc = M_CHUNK_ROWS
    gbias = jnp.concatenate([b_i, b_f, jnp.zeros((LANES - 2 * M_HEADS,), F32)]).reshape(1, LANES)
    tril = jnp.tril(jnp.ones((lc, lc), F32))
    eye = jnp.eye(M_HEADS, dtype=F32)
    block_diag = lambda w: jnp.einsum('hde,hg->hdge', w, eye).reshape(M_WIDTH, M_WIDTH)
    w_qk = jnp.concatenate([block_diag(w_mq), block_diag(w_mk)], axis=1).astype(BF16)
    nseq = MLSTM_SEQS
    assert bsz % nseq == 0
    seq = lambda n: pl.BlockSpec((nseq * s, n), lambda b: (b, 0))
    return pl.pallas_call(
        functools.partial(_mlstm_kernel, seq_len=s),
        out_shape=jax.ShapeDtypeStruct((bsz * s, M_WIDTH), BF16),
        grid=(bsz // nseq,),
        in_specs=[seq(M_WIDTH), seq(M_WIDTH), seq(M_WIDTH), seq(LANES),
                  _const_spec((M_CONV, M_WIDTH)), _const_spec((1, M_WIDTH)),
                  _const_spec((M_WIDTH, 2 * M_WIDTH)),
                  _const_spec((1, LANES)), _const_spec((1, M_WIDTH)), _const_spec((1, M_WIDTH)),
                  _const_spec((lc, lc))],
        out_specs=seq(M_WIDTH),
        scratch_shapes=[pltpu.VMEM((nseq, s + 8, M_WIDTH), F32),
                        pltpu.VMEM((nseq * M_HEADS, M_HEAD_DIM, 2 * LANES), F32), pltpu.VMEM((8, LANES), F32)],
        compiler_params=_cparams(("parallel",)),
        name="mlstm_branch",
    )(xm, vm, op, gate, conv_w, conv_b.reshape(1, -1), w_qk, gbias, g_mhead.reshape(1, -1), w_mskip.reshape(1, -1),
      tril)


def _attn_kernel(q_ref, k_ref, v_ref, o_ref, s_scr, m_s, acc_s):
    tq = q_ref.shape[0]
    i = pl.program_id(1)
    rows = lax.broadcasted_iota(I32, (tq, tq), 0)
    cols = lax.broadcasted_iota(I32, (tq, tq), 1)

    def lane_tile_max(sc):
        out = sc[:, :LANES]
        for t in range(1, tq // LANES):
            out = jnp.maximum(out, sc[:, t * LANES:(t + 1) * LANES])
        return out

    heads = [slice(h * LANES, (h + 1) * LANES) for h in range(A_HEADS)]
    m_s[...] = jnp.full(m_s.shape, NEG_BIG, F32)
    acc_s[...] = jnp.zeros_like(acc_s)

    def pass_a(j, diagonal):
        k0 = pl.multiple_of(j * tq, tq)
        for h, hs in enumerate(heads):
            sc = lax.dot_general(q_ref[:, hs], k_ref[pl.ds(k0, tq), hs], (((1,), (1,)), ((), ())),
                                 preferred_element_type=F32)
            if diagonal:
                sc = jnp.where(cols <= rows, sc, NEG_BIG)
            s_scr[h, j] = sc
            m_s[h] = jnp.maximum(m_s[h], lane_tile_max(sc))

    def for_blocks(n, fn):
        def pair(p, c):
            fn(2 * p)
            fn(2 * p + 1)
            return c

        lax.fori_loop(0, n // 2, pair, 0)

        @pl.when(n % 2 == 1)
        def _():
            fn(n - 1)

    for_blocks(i, lambda j: pass_a(j, False))
    pass_a(i, True)
    for h in range(A_HEADS):
        m_s[h] = jnp.broadcast_to(jnp.max(m_s[h], axis=1, keepdims=True), (tq, LANES))

    def pass_b(j):
        k0 = pl.multiple_of(j * tq, tq)
        for h, hs in enumerate(heads):
            m_row = m_s[h]
            p = jnp.exp2(s_scr[h, j] - jnp.concatenate([m_row] * (tq // LANES), axis=1)).astype(BF16)
            acc_s[h] += jnp.dot(p, v_ref[pl.ds(k0, tq), hs], preferred_element_type=F32)

    for_blocks(i + 1, pass_b)
    for h, hs in enumerate(heads):
        acc = acc_s[h]
        o_ref[:, hs] = (acc / acc[:, A_VDIM:A_VDIM + 1]).astype(BF16)


def _mla_attention(q, k, v, bsz, s):
    tq = min(ATT_Q, s)
    nq = s // tq
    hw = A_HEADS * LANES
    return pl.pallas_call(
        _attn_kernel,
        out_shape=jax.ShapeDtypeStruct(q.shape, BF16),
        grid=(bsz, nq),
        in_specs=[pl.BlockSpec((tq, hw), lambda b, i: (b * nq + i, 0)),
                  pl.BlockSpec((s, hw), lambda b, i: (b, 0)),
                  pl.BlockSpec((s, hw), lambda b, i: (b, 0))],
        out_specs=pl.BlockSpec((tq, hw), lambda b, i: (b * nq + i, 0)),
        scratch_shapes=[pltpu.VMEM((A_HEADS, nq, tq, tq), F32), pltpu.VMEM((A_HEADS, tq, LANES), F32),
                        pltpu.VMEM((A_HEADS, tq, LANES), F32)],
        compiler_params=_cparams(("parallel", "arbitrary")),
        name="mla_attention",
    )(q, k, v)


def _merge_kernel(ym_ref, oa_ref, gm_ref, ga_ref, x_ref, wbm_ref, wba_ref, wmix_ref, g_ref, b_ref, h_ref):
    y_m = jnp.dot(ym_ref[...], wbm_ref[...], preferred_element_type=F32)
    y_a = jnp.dot(oa_ref[...], wba_ref[...], preferred_element_type=F32)
    mixed = jax.nn.sigmoid(gm_ref[...].astype(F32)) * y_m + jax.nn.sigmoid(ga_ref[...].astype(F32)) * y_a
    mix = jnp.dot(mixed.astype(BF16), wmix_ref[...], preferred_element_type=F32)
    h_ref[...] = _layer_norm_rows(DN_ALPHA * x_ref[...] + mix, g_ref[...], b_ref[...])


def _merge(ym, oa, gm, ga, x2, w_br_m, w_br_a, w_mix_out, ln_g, ln_b):
    t = x2.shape[0]
    tm = min(MERGE_ROWS, t)
    wba = jnp.concatenate([w_br_a.reshape(A_HEADS, A_VDIM, D_MODEL),
                           jnp.zeros((A_HEADS, LANES - A_VDIM, D_MODEL), w_br_a.dtype)], axis=1)
    wba = wba.reshape(A_HEADS * LANES, D_MODEL).astype(BF16)
    row = lambda n: pl.BlockSpec((tm, n), lambda i: (i, 0))
    return pl.pallas_call(
        _merge_kernel,
        out_shape=jax.ShapeDtypeStruct((t, D_MODEL), F32),
        grid=(t // tm,),
        in_specs=[row(M_WIDTH), row(A_HEADS * LANES), row(D_MODEL), row(D_MODEL), row(D_MODEL),
                  _const_spec((M_WIDTH, D_MODEL)), _const_spec((A_HEADS * LANES, D_MODEL)),
                  _const_spec((D_MODEL, D_MODEL)), _const_spec((1, D_MODEL)), _const_spec((1, D_MODEL))],
        out_specs=row(D_MODEL),
        compiler_params=_cparams(("parallel",)),
        name="merge_deepnorm1",
    )(ym, oa, gm, ga, x2, w_br_m.astype(BF16), wba, w_mix_out.astype(BF16), ln_g.reshape(1, -1), ln_b.reshape(1, -1))


def _kv_kernel(m_ref, w_ref, o_ref):
    o_ref[...] = jnp.dot(m_ref[...].astype(BF16), w_ref[...], preferred_element_type=F32).astype(BF16)


def _mem_kv(mem2, w_ckv):
    r = mem2.shape[0]
    tm = min(512, r)
    return pl.pallas_call(
        _kv_kernel,
        out_shape=jax.ShapeDtypeStruct((r, 2 * D_MODEL), BF16),
        grid=(r // tm,),
        in_specs=[pl.BlockSpec((tm, D_MODEL), lambda i: (i, 0)), _const_spec((D_MODEL, 2 * D_MODEL))],
        out_specs=pl.BlockSpec((tm, 2 * D_MODEL), lambda i: (i, 0)),
        compiler_params=_cparams(("parallel",)),
        name="memory_kv",
    )(mem2, w_ckv.astype(BF16))


def _xattn_kernel(h_ref, kv_ref, wq_ref, wo_ref, g_ref, b_ref, wr_ref, br_ref, *rest):
    h2_ref, idx_ref, gate_ref = rest[-3:]
    h1 = h_ref[...]
    q = jnp.dot(h1.astype(BF16), wq_ref[...], preferred_element_type=F32).astype(BF16)
    scale = X_HEAD_DIM ** -0.5
    ksl = [slice(hd * X_HEAD_DIM, (hd + 1) * X_HEAD_DIM) for hd in range(X_HEADS)]
    vsl = [slice(D_MODEL + hd * X_HEAD_DIM, D_MODEL + (hd + 1) * X_HEAD_DIM) for hd in range(X_HEADS)]
    scs = [lax.dot_general(q[:, ks], kv_ref[:, ks], (((1,), (1,)), ((), ())), preferred_element_type=F32) * scale
           for ks in ksl]
    ps = [jnp.exp(sc - jnp.max(sc, axis=1, keepdims=True)) for sc in scs]
    ps = [(p / jnp.sum(p, axis=1, keepdims=True)).astype(BF16) for p in ps]
    o = jnp.concatenate([jnp.dot(p, kv_ref[:, vs], preferred_element_type=F32).astype(BF16)
                         for p, vs in zip(ps, vsl)], axis=1)
    att = jnp.dot(o, wo_ref[...], preferred_element_type=F32)
    h2 = _layer_norm_rows(DN_ALPHA * h1 + att, g_ref[...], b_ref[...])
    for c in range(ROW_TILES):
        h2_ref[pl.ds(c, h2.shape[0], stride=ROW_TILES), :] = h2[:, c * LANES:(c + 1) * LANES]

    h2_hi = h2.astype(BF16)
    h2_lo = (h2 - h2_hi.astype(F32)).astype(BF16)
    hi_prod = jnp.dot(h2_hi, wr_ref[...], preferred_element_type=F32)
    lo_prod = jnp.dot(h2_lo, wr_ref[:, :LANES], preferred_element_type=F32)
    logits = hi_prod[:, :LANES] + (hi_prod[:, LANES:] + lo_prod) + br_ref[...]
    tm = logits.shape[0]
    lane = lax.broadcasted_iota(I32, (tm, LANES), 1)
    lane_f = lane.astype(F32)
    work = jnp.where(lane < N_EXPERTS, logits, -jnp.inf)
    vals, idxs = [], []
    for _ in range(TOP_K):
        mx = jnp.max(work, axis=1, keepdims=True)
        ix = jnp.min(jnp.where(work == mx, lane_f, float(LANES)), axis=1, keepdims=True)
        vals.append(mx)
        idxs.append(ix)
        work = jnp.where(lane_f == ix, -jnp.inf, work)
    es = [jnp.exp(vv - vals[0]) for vv in vals]
    tot = es[0] + es[1] + es[2] + es[3]
    idx_slab = jnp.zeros((tm, LANES), F32)
    gate_slab = jnp.zeros((tm, LANES), F32)
    for kk in range(TOP_K):
        idx_slab = jnp.where(lane == kk, idxs[kk], idx_slab)
        gate_slab = jnp.where(lane == kk, es[kk] / tot, gate_slab)
    idx_ref[...] = idx_slab.astype(I32)
    gate_ref[...] = gate_slab


def _cross_attention_router(h1, kv, bsz, s, w_cq, w_co, ln_g, ln_b, w_router, b_router, part, after):
    n_mem = kv.shape[0] // bsz
    bsz = bsz // MOE_PARTS
    b0 = part * bsz
    t = bsz * s
    tm = min(XATT_ROWS, s)
    ns = s // tm
    wr = jnp.concatenate([w_router, jnp.zeros((D_MODEL, LANES - N_EXPERTS), F32)], axis=1)
    wr_hi = wr.astype(BF16)
    wr = jnp.concatenate([wr_hi, (wr - wr_hi.astype(F32)).astype(BF16)], axis=1)
    br = jnp.concatenate([b_router, jnp.zeros((LANES - N_EXPERTS,), F32)]).reshape(1, LANES)
    row = lambda n: pl.BlockSpec((tm, n), lambda b, i: (b * ns + i, 0))
    h1_rows = pl.BlockSpec((tm, D_MODEL), lambda b, i: ((b0 + b) * ns + i, 0))
    in_specs = [h1_rows, pl.BlockSpec((n_mem, 2 * D_MODEL), lambda b, i: (b0 + b, 0)),
                _const_spec((D_MODEL, D_MODEL)), _const_spec((D_MODEL, D_MODEL)),
                _const_spec((1, D_MODEL)), _const_spec((1, D_MODEL)),
                _const_spec((D_MODEL, 2 * LANES)), _const_spec((1, LANES))]
    args = [h1, kv, w_cq.astype(BF16), w_co.astype(BF16), ln_g.reshape(1, -1), ln_b.reshape(1, -1), wr, br]
    if after is not None:
        in_specs.append(pl.BlockSpec(memory_space=pl.ANY))
        args.append(after)
    return pl.pallas_call(
        _xattn_kernel,
        out_shape=[jax.ShapeDtypeStruct((t * ROW_TILES, LANES), F32), jax.ShapeDtypeStruct((t, LANES), I32),
                   jax.ShapeDtypeStruct((t, LANES), F32)],
        grid=(bsz, ns),
        in_specs=in_specs,
        out_specs=[pl.BlockSpec((tm * ROW_TILES, LANES), lambda b, i: (b * ns + i, 0)), row(LANES), row(LANES)],
        compiler_params=_cparams(("parallel", "parallel")),
        name="cross_attention_router",
    )(*args)


def _rank_kernel(idx_ref, ltri_ref, rank_ref, cnt_ref, carry_s):
    @pl.when(pl.program_id(0) == 0)
    def _():
        carry_s[...] = jnp.zeros_like(carry_s)

    idx = idx_ref[...]
    tm = idx.shape[0]
    lane = lax.broadcasted_iota(I32, (tm, LANES), 1)
    onehots = [(lane == idx[:, kk:kk + 1]).astype(F32) for kk in range(TOP_K)]
    sel = onehots[0] + onehots[1] + onehots[2] + onehots[3]
    before = jnp.dot(ltri_ref[...], sel.astype(BF16), preferred_element_type=F32) + carry_s[0:1, :]
    rank_slab = jnp.zeros((tm, LANES), F32)
    for kk in range(TOP_K):
        r = jnp.sum(onehots[kk] * before, axis=1, keepdims=True)
        rank_slab = jnp.where(lane == kk, r, rank_slab)
    rank_ref[...] = rank_slab.astype(I32)
    carry_s[0:1, :] = carry_s[0:1, :] + jnp.sum(sel, axis=0, keepdims=True)
    cnt_ref[...] = jnp.broadcast_to(carry_s[0:1, :], cnt_ref.shape).astype(I32)


def _routing_ranks(idx_slab):
    t = idx_slab.shape[0]
    tm = min(RANK_ROWS, t)
    ltri = jnp.tril(jnp.ones((tm, tm), BF16), k=-1)
    return pl.pallas_call(
        _rank_kernel,
        out_shape=[jax.ShapeDtypeStruct((t, LANES), I32), jax.ShapeDtypeStruct((8, LANES), I32)],
        grid=(t // tm,),
        in_specs=[pl.BlockSpec((tm, LANES), lambda i: (i, 0)), _const_spec((tm, tm))],
        out_specs=[pl.BlockSpec((tm, LANES), lambda i: (i, 0)), _const_spec((8, LANES))],
        scratch_shapes=[pltpu.VMEM((8, LANES), F32)],
        compiler_params=_cparams(("arbitrary",)),
        name="routing_ranks",
    )(idx_slab, ltri)


def _dest_kernel(idx_ref, rank_ref, pstart_ref, dest_ref):
    idx = idx_ref[...]
    tm = idx.shape[0]
    lane = lax.broadcasted_iota(I32, (tm, LANES), 1)
    dest = rank_ref[...].astype(F32)
    pstart = pstart_ref[0:1, :].astype(F32)
    for kk in range(TOP_K):
        start = jnp.sum(jnp.where(lane == idx[:, kk:kk + 1], pstart, 0.0), axis=1, keepdims=True)
        dest = dest + jnp.where(lane == kk, start, 0.0)
    dest_ref[...] = dest.T[0:8, :].astype(I32)


def _dest_rows(idx_slab, rank_slab, pstart_row):
    t = rank_slab.shape[0]
    tm = min(RANK_ROWS, t)
    return pl.pallas_call(
        _dest_kernel,
        out_shape=jax.ShapeDtypeStruct((8, t), I32),
        grid=(t // tm,),
        in_specs=[pl.BlockSpec((tm, LANES), lambda i: (i, 0)), pl.BlockSpec((tm, LANES), lambda i: (i, 0)),
                  _const_spec((8, LANES))],
        out_specs=pl.BlockSpec((8, tm), lambda i: (0, i)),
        compiler_params=_cparams(("parallel",)),
        name="routing_dest",
    )(idx_slab, rank_slab, pstart_row)


def _sc_mesh():
    return plsc.VectorSubcoreMesh(core_axis_name="c", subcore_axis_name="s", num_cores=SC_CORES,
                                  num_subcores=SC_SUBCORES)


def _sc_worker():
    return lax.axis_index("s") * SC_CORES + lax.axis_index("c")


def _dispatch(h2, dest_kt, pad_rows, n_rows):
    t = dest_kt.shape[1]
    w = SC_ROWS
    per_w = t // SC_WORKERS
    steps = per_w // w
    assert steps % 2 == 0 and steps * w * SC_WORKERS == t
    pad_steps = pad_rows.shape[0] // (SC_WORKERS * w)
    dest_w = dest_kt.reshape(TOP_K, SC_WORKERS, steps, w)
    pad_w = pad_rows.reshape(SC_WORKERS, pad_steps, w)
    zeros = jnp.zeros((w, ROW_TILES, LANES), F32)

    def body(x_hbm, dest_hbm, pad_hbm, zeros_hbm, o_hbm, idx_v, pad_v, rows_v, sem_in, sem_out):
        wid = _sc_worker()
        tok0 = wid * per_w
        for kk in range(TOP_K):
            pltpu.sync_copy(dest_hbm.at[kk, wid], idx_v.at[pl.ds(kk * steps, steps)])
        pltpu.sync_copy(pad_hbm.at[wid], pad_v)
        pltpu.sync_copy(zeros_hbm, rows_v.at[0])
        for j in range(pad_steps):
            pltpu.sync_copy(rows_v.at[0], o_hbm.at[pad_v.at[j]])

        def load(step, b):
            return pltpu.make_async_copy(x_hbm.at[pl.ds(tok0 + step * w, w)], rows_v.at[b], sem_in.at[b])

        def scatter(step, kk, b):
            return pltpu.make_async_copy(rows_v.at[b], o_hbm.at[idx_v.at[kk * steps + step]], sem_out.at[b])

        load(0, 0).start()

        @pl.loop(0, steps, step=2)
        def _(s0):
            for b in range(2):
                step = s0 + b
                load(step, b).wait()

                @pl.when(step >= 1)
                def _():
                    for kk in range(TOP_K):
                        scatter(step - 1, kk, 1 - b).wait()

                @pl.when(step + 1 < steps)
                def _():
                    load(step + 1, 1 - b).start()

                for kk in range(TOP_K):
                    scatter(step, kk, b).start()

        for kk in range(TOP_K):
            scatter(steps - 1, kk, 1).wait()

    return pl.kernel(
        body,
        out_type=jax.ShapeDtypeStruct((n_rows, ROW_TILES, LANES), F32),
        mesh=_sc_mesh(),
        scratch_types=[pltpu.VMEM((steps * TOP_K, w), I32), pltpu.VMEM((pad_steps, w), I32),
                       pltpu.VMEM((2, w, ROW_TILES, LANES), F32), pltpu.SemaphoreType.DMA((2,)),
                       pltpu.SemaphoreType.DMA((2,))],
        name="moe_dispatch_sc",
    )(h2, dest_w, pad_w, zeros)


def _gather_rows(yb, dest_kt):
    n = dest_kt.shape[0]
    w = SC_ROWS
    per_w = n // SC_WORKERS
    steps = per_w // w
    assert steps % 2 == 0 and steps * w * SC_WORKERS == n
    dest_w = dest_kt.reshape(SC_WORKERS, steps, w)

    def body(y_hbm, dest_hbm, o_hbm, idx_v, rows_v, sem_g, sem_w):
        wid = _sc_worker()
        row0 = wid * per_w
        pltpu.sync_copy(dest_hbm.at[wid], idx_v)

        def gather(step, b):
            return pltpu.make_async_copy(y_hbm.at[idx_v.at[step]], rows_v.at[b], sem_g.at[b])

        def write(step, b):
            return pltpu.make_async_copy(rows_v.at[b], o_hbm.at[pl.ds(row0 + step * w, w)], sem_w.at[b])

        gather(0, 0).start()

        @pl.loop(0, steps, step=2)
        def _(s0):
            for b in range(2):
                step = s0 + b
                gather(step, b).wait()

                @pl.when(step >= 1)
                def _():
                    write(step - 1, 1 - b).wait()

                @pl.when(step + 1 < steps)
                def _():
                    gather(step + 1, 1 - b).start()

                write(step, b).start()

        write(steps - 1, 1).wait()

    return pl.kernel(
        body,
        out_type=jax.ShapeDtypeStruct((n, ROW_TILES, LANES), F32),
        mesh=_sc_mesh(),
        scratch_types=[pltpu.VMEM((steps, w), I32), pltpu.VMEM((2, w, ROW_TILES, LANES), F32),
                       pltpu.SemaphoreType.DMA((2,)), pltpu.SemaphoreType.DMA((2,))],
        name="moe_gather_sc",
    )(yb, dest_w)


def _rows_2d(ref, n):
    return jnp.concatenate([ref[pl.ds(c, n, stride=ROW_TILES), :] for c in range(ROW_TILES)], axis=1)


def _store_rows(ref, val):
    for c in range(ROW_TILES):
        ref[pl.ds(c, val.shape[0], stride=ROW_TILES), :] = val[:, c * LANES:(c + 1) * LANES]


def _expert_kernel(blk_e_ref, first_ref, slot_ref, next_ref, nblk_ref, x_ref, wg_hbm, wl_hbm, bg_ref, bl_ref, wd_hbm,
                   bd_ref, y_ref, wg_buf, wl_buf, wd_stage, wd_buf, sem):
    i = pl.program_id(0)

    def fetch(e, sl):
        return [pltpu.make_async_copy(src.at[e], dst.at[sl], sem.at[sl, j])
                for j, (src, dst) in enumerate(((wg_hbm, wg_buf), (wl_hbm, wl_buf), (wd_hbm, wd_stage)))]

    @pl.when(i < nblk_ref[0])
    def _():
        sl = slot_ref[i]

        @pl.when(first_ref[i] == 1)
        def _():
            @pl.when(i == 0)
            def _():
                for cp in fetch(blk_e_ref[0], sl):
                    cp.start()

            for cp in fetch(blk_e_ref[i], sl):
                cp.wait()
            wd_buf[sl] = wd_stage[sl].astype(BF16)

            @pl.when(next_ref[i] >= 0)
            def _():
                for cp in fetch(next_ref[i], 1 - sl):
                    cp.start()

        xb = _rows_2d(x_ref, y_ref.shape[0] // ROW_TILES).astype(BF16)
        glu = jnp.dot(xb, wg_buf[sl], preferred_element_type=F32) + bg_ref[...]
        lin = jnp.dot(xb, wl_buf[sl], preferred_element_type=F32) + bl_ref[...]
        glu = jnp.minimum(glu, SWIGLU_LIMIT)
        lin = jnp.clip(lin, -SWIGLU_LIMIT, SWIGLU_LIMIT)
        act = glu * jax.nn.sigmoid(SWIGLU_ALPHA * glu) * (lin + 1.0)
        _store_rows(y_ref, jnp.dot(act.astype(BF16), wd_buf[sl], preferred_element_type=F32) + bd_ref[...])

    @pl.when(i >= nblk_ref[0])
    def _():
        y_ref[...] = jnp.zeros_like(y_ref)


def _split_kernel(w_ref, perm_ref, after_ref, g_ref, l_ref):
    del after_ref
    half = g_ref.shape[-1]
    sorted_cols = jnp.dot(w_ref[...].astype(BF16), perm_ref[...], preferred_element_type=F32)
    g_ref[...] = sorted_cols[:, :half].astype(BF16)
    l_ref[...] = sorted_cols[:, half:].astype(BF16)


def _split_gate_up(w_gu, after):
    e, d, n2 = w_gu.shape
    cw = 512
    src = np.concatenate([np.arange(0, cw, 2), np.arange(1, cw, 2)])
    perm = jnp.asarray(np.eye(cw, dtype=np.float32)[:, src], BF16)
    return pl.pallas_call(
        _split_kernel,
        out_shape=[jax.ShapeDtypeStruct((e, d, n2 // 2), BF16), jax.ShapeDtypeStruct((e, d, n2 // 2), BF16)],
        grid=(e, n2 // cw),
        in_specs=[pl.BlockSpec((None, d, cw), lambda i, c: (i, 0, c)), _const_spec((cw, cw)),
                  pl.BlockSpec(memory_space=pl.ANY)],
        out_specs=[pl.BlockSpec((None, d, cw // 2), lambda i, c: (i, 0, c)),
                   pl.BlockSpec((None, d, cw // 2), lambda i, c: (i, 0, c))],
        compiler_params=_cparams(("parallel", "parallel")),
        name="split_gate_up",
    )(w_gu, perm, after)


def _experts(xb, blk_meta, w_glu, w_lin, b_glu, b_lin, w_dn, b_dn):
    n_rows = xb.shape[0] // ROW_TILES
    bm = EXPERT_ROWS
    vec = lambda n: pl.BlockSpec((None, 1, n), lambda i, be, *_: (be[i], 0, 0))
    rows = pl.BlockSpec((bm * ROW_TILES, LANES), lambda i, *_: (i, 0))
    hbm = pl.BlockSpec(memory_space=pl.ANY)
    grid_spec = pltpu.PrefetchScalarGridSpec(
        num_scalar_prefetch=5,
        grid=(n_rows // bm,),
        in_specs=[rows, hbm, hbm, vec(D_EXPERT), vec(D_EXPERT), hbm, vec(D_MODEL)],
        out_specs=rows,
        scratch_shapes=[pltpu.VMEM((2, D_MODEL, D_EXPERT), BF16), pltpu.VMEM((2, D_MODEL, D_EXPERT), BF16),
                        pltpu.VMEM((2, D_EXPERT, D_MODEL), F32), pltpu.VMEM((2, D_EXPERT, D_MODEL), BF16),
                        pltpu.SemaphoreType.DMA((2, 3))],
    )
    return pl.pallas_call(
        _expert_kernel,
        out_shape=jax.ShapeDtypeStruct((n_rows * ROW_TILES, LANES), F32),
        grid_spec=grid_spec,
        compiler_params=_cparams(("arbitrary",)),
        name="moe_experts",
    )(*blk_meta, xb, w_glu, w_lin, b_glu, b_lin, w_dn, b_dn)


def _combine_kernel(h_ref, gate_ref, y0_ref, y1_ref, y2_ref, y3_ref, g_ref, b_ref, *rest):
    o_ref = rest[-1]
    tt = o_ref.shape[0]
    gates = gate_ref[...]
    acc = DN_ALPHA * _rows_2d(h_ref, tt)
    for kk, y_ref in enumerate((y0_ref, y1_ref, y2_ref, y3_ref)):
        acc = acc + gates[:, kk:kk + 1] * _rows_2d(y_ref, tt)
    o_ref[...] = _layer_norm_rows(acc, g_ref[...], b_ref[...])


def _combine(h2, gate_slab, yg, ln_g, ln_b, part, prev_out):
    t = h2.shape[0] // ROW_TILES
    t_all = t * MOE_PARTS
    tt = min(COMBINE_TOKENS, t)
    steps = t // tt
    first = part * steps
    row = lambda m: pl.BlockSpec((tt, m), lambda i: (i, 0))
    tiles = lambda kk: pl.BlockSpec((tt * ROW_TILES, LANES), lambda i: (kk * steps + i, 0))
    in_specs = [tiles(0), row(LANES),
                tiles(0), tiles(1), tiles(2), tiles(3), _const_spec((1, D_MODEL)), _const_spec((1, D_MODEL))]
    args = [h2, gate_slab, yg, yg, yg, yg, ln_g.reshape(1, -1), ln_b.reshape(1, -1)]
    aliases = {}
    if prev_out is not None:
        in_specs.append(pl.BlockSpec(memory_space=pl.ANY))
        args.append(prev_out)
        aliases = {len(args) - 1: 0}
    return pl.pallas_call(
        _combine_kernel,
        out_shape=jax.ShapeDtypeStruct((t_all, D_MODEL), F32),
        grid=(steps,),
        in_specs=in_specs,
        out_specs=pl.BlockSpec((tt, D_MODEL), lambda i: (first + i, 0)),
        input_output_aliases=aliases,
        compiler_params=_cparams(("parallel",)),
        name="moe_combine",
    )(*args)


def _route_and_dispatch(h2, idx_slab):
    t = h2.shape[0] // ROW_TILES
    bm = EXPERT_ROWS
    n_rows = t * TOP_K + N_EXPERTS * bm
    n_blocks = n_rows // bm

    rank_slab, cnt = _routing_ranks(idx_slab)
    counts = cnt[0, :N_EXPERTS]
    padded = (counts + bm - 1) // bm * bm
    pends = jnp.cumsum(padded)
    pstarts = pends - padded
    pstart_row = jnp.zeros((8, LANES), I32).at[:, :N_EXPERTS].set(pstarts[None, :])
    blk_start = jnp.arange(n_blocks, dtype=I32) * bm
    blk_e = jnp.minimum(jnp.sum((pends[None, :] <= blk_start[:, None]).astype(I32), axis=1), N_EXPERTS - 1)
    nblk = (pends[-1] // bm).astype(I32).reshape(1)
    active = counts > 0
    ordinal = jnp.cumsum(active.astype(I32)) - 1
    eid = jnp.arange(N_EXPERTS, dtype=I32)
    later = active[None, :] & (eid[None, :] > eid[:, None])
    next_active = jnp.where(later.any(axis=1), jnp.argmax(later, axis=1).astype(I32), -1)
    blk_first = jnp.concatenate([jnp.ones((1,), I32), (blk_e[1:] != blk_e[:-1]).astype(I32)])
    blk_onehot = (blk_e[:, None] == eid[None, :]).astype(I32)
    blk_slot = jnp.sum(blk_onehot * ordinal[None, :], axis=1) % 2
    blk_next = jnp.sum(blk_onehot * next_active[None, :], axis=1)
    blk_meta = (blk_e, blk_first, blk_slot, blk_next, nblk)
    dest_kt = _dest_rows(idx_slab, rank_slab, pstart_row)[:TOP_K]
    pad_off = jnp.arange(bm, dtype=I32)[None, :]
    pad_rows = jnp.where(pad_off < (padded - counts)[:, None], (pstarts + counts)[:, None] + pad_off,
                         n_rows - 1).reshape(-1)
    xb = _dispatch(h2.reshape(t, ROW_TILES, LANES), dest_kt, pad_rows, n_rows)
    return xb.reshape(n_rows * ROW_TILES, LANES), blk_meta, dest_kt


def _xattn_moe(h1, kv, bsz, s, w_cq, w_co, ln2_g, ln2_b, w_router, b_router, w_gu, b_gu, w_dn, b_dn, ln3_g, ln3_b):
    parts = []
    after = None
    for part in range(MOE_PARTS):
        h2, idx_slab, gate_slab = _cross_attention_router(h1, kv, bsz, s, w_cq, w_co, ln2_g, ln2_b, w_router,
                                                          b_router, part, after)
        parts.append((h2, gate_slab) + _route_and_dispatch(h2, idx_slab))
        if part == 0:
            w_glu, w_lin = _split_gate_up(w_gu, parts[-1][-1])
            after = w_glu
    expert_w = (w_glu, w_lin, b_gu[:, None, 0::2], b_gu[:, None, 1::2], w_dn, b_dn[:, None, :])
    out = None
    for part, (h2, gate_slab, xb, blk_meta, dest_kt) in enumerate(parts):
        n_rows = xb.shape[0] // ROW_TILES
        yb = _experts(xb, blk_meta, *expert_w)
        yg = _gather_rows(yb.reshape(n_rows, ROW_TILES, LANES), dest_kt.reshape(-1))
        out = _combine(h2, gate_slab, yg.reshape(-1, LANES), ln3_g, ln3_b, part, out)
    return out


def kernel(x, mem, positions, w_in, b_igate, b_fgate, conv_w, conv_b, w_mq, w_mk, g_mhead, w_mskip, g_qlat, g_kvlat,
           w_uq, w_ukv, w_br_m, w_br_a, w_mix_out, ln1_g, ln1_b, w_cq, w_ckv, w_co, ln2_g, ln2_b, w_router, b_router,
           w_gu, b_gu, w_dn, b_dn, ln3_g, ln3_b):
    bsz, s, d = x.shape
    t = bsz * s
    h = x.reshape(t, d)
    pos2 = positions.reshape(t, 1)
    for l in range(DEPTH):
        q, k, v, xm, vm, op, gm, ga, gate = _input_projection(h, pos2, w_in[l], g_qlat[l], g_kvlat[l], w_uq[l],
                                                              w_ukv[l])
        ym = _mlstm_branch(xm, vm, op, gate, bsz, s, b_igate[l], b_fgate[l], conv_w[l], conv_b[l], w_mq[l], w_mk[l],
                           g_mhead[l], w_mskip[l])
        oa = _mla_attention(q, k, v, bsz, s)
        h1 = _merge(ym, oa, gm, ga, h, w_br_m[l], w_br_a[l], w_mix_out[l], ln1_g[l], ln1_b[l])
        kv = _mem_kv(mem.reshape(-1, d), w_ckv[l])
        h = _xattn_moe(h1, kv, bsz, s, w_cq[l], w_co[l], ln2_g[l], ln2_b[l], w_router[l], b_router[l], w_gu[l],
                       b_gu[l], w_dn[l], b_dn[l], ln3_g[l], ln3_b[l])
    return h.reshape(bsz, s, d)
```

```python
import functools

import numpy as np
import jax
import jax.numpy as jnp
from jax import lax
from jax.experimental import pallas as pl
from jax.experimental.pallas import tpu as pltpu
from jax.experimental.pallas import tpu_sc as plsc

F32 = jnp.float32
BF16 = jnp.bfloat16
I32 = jnp.int32

D_MODEL = 1024
N_MEM = 256
M_HEADS = 4
M_HEAD_DIM = 128
M_WIDTH = M_HEADS * M_HEAD_DIM
M_CONV = 4
A_HEADS = 8
A_NOPE = 64
A_ROPE = 32
A_QK = A_NOPE + A_ROPE
A_VDIM = 64
A_Q_RANK = 256
A_KV_RANK = 128
ROPE_THETA = 10000.0
X_HEADS = 4
X_HEAD_DIM = D_MODEL // X_HEADS
N_EXPERTS = 32
TOP_K = 4
D_EXPERT = D_MODEL
SWIGLU_ALPHA = 1.702
SWIGLU_LIMIT = 7.0
DEPTH = 1
DN_ALPHA = (2.0 * DEPTH) ** 0.25
EPS = 1e-5
IN_SPLITS = (A_Q_RANK, A_KV_RANK, A_ROPE, M_WIDTH, M_WIDTH, M_WIDTH, M_HEADS, M_HEADS, D_MODEL, D_MODEL)
IN_OFFSETS = tuple(int(v) for v in np.cumsum((0,) + IN_SPLITS))

LANES = 128
VMEM_LIMIT = 56 * 1024 * 1024

PROJ_ROWS = 512
M_CHUNK_ROWS = 128
MLSTM_SEQS = 2
ATT_Q = 256
MERGE_ROWS = 512
XATT_ROWS = 512
RANK_ROWS = 1024
EXPERT_ROWS = 256
COMBINE_TOKENS = 256
MOE_PART_SHARES = (3, 3, 2)
SC_CORES = 2
SC_SUBCORES = 16
SC_WORKERS = SC_CORES * SC_SUBCORES
SC_ROWS = 32
NEG_BIG = -1e30
LOG2_E = 1.4426950408889634
ROW_TILES = D_MODEL // LANES

C_QLAT = 0
C_KVLAT = C_QLAT + A_Q_RANK
C_KR = C_KVLAT + A_KV_RANK
C_KRS = C_KR + LANES
C_GATE = C_KRS + LANES
C_XM = C_GATE + LANES
C_VM = C_XM + M_WIDTH
C_OP = C_VM + M_WIDTH
C_GM = C_OP + M_WIDTH
C_GA = C_GM + D_MODEL
C_END = C_GA + D_MODEL


def _cparams(sem, vmem=VMEM_LIMIT):
    return pltpu.CompilerParams(dimension_semantics=sem, vmem_limit_bytes=vmem)


def _const_spec(shape):
    nd = len(shape)
    return pl.BlockSpec(shape, lambda *a: (0,) * nd)


def _layer_norm_rows(v, g, b):
    mu = jnp.mean(v, axis=-1, keepdims=True)
    d = v - mu
    var = jnp.mean(d * d, axis=-1, keepdims=True)
    return d * lax.rsqrt(var + EPS) * g + b


def _proj_kernel(x_ref, pos_ref, w_ref, gq_ref, gkv_ref, wuq_ref, wuqs_ref, wuk_ref, wuv_ref, invf_ref,
                 q_ref, k_ref, v_ref, xm_ref, vm_ref, op_ref, gm_ref, ga_ref, gate_ref):
    xb = x_ref[...].astype(BF16)

    def mm(lo, n):
        return jnp.dot(xb, w_ref[:, lo:lo + n], preferred_element_type=F32)

    xm_ref[...] = mm(C_XM, M_WIDTH).astype(BF16)
    vm_ref[...] = mm(C_VM, M_WIDTH).astype(BF16)
    op_ref[...] = mm(C_OP, M_WIDTH).astype(BF16)
    gm_ref[...] = mm(C_GM, D_MODEL).astype(BF16)
    ga_ref[...] = mm(C_GA, D_MODEL).astype(BF16)
    gate_ref[...] = mm(C_GATE, LANES)

    ang = pos_ref[...].astype(F32) * invf_ref[...]
    cos = jnp.cos(ang)
    sin = jnp.sin(ang)

    q_lat = mm(C_QLAT, A_Q_RANK)
    qn = (q_lat * lax.rsqrt(jnp.mean(q_lat * q_lat, axis=-1, keepdims=True) + EPS) * gq_ref[...]).astype(BF16)
    q = jnp.dot(qn, wuq_ref[...], preferred_element_type=F32)
    qs = jnp.dot(qn, wuqs_ref[...], preferred_element_type=F32)
    kv_lat = mm(C_KVLAT, A_KV_RANK)
    kvn = (kv_lat * lax.rsqrt(jnp.mean(kv_lat * kv_lat, axis=-1, keepdims=True) + EPS) * gkv_ref[...]).astype(BF16)
    kn = jnp.dot(kvn, wuk_ref[...], preferred_element_type=F32)
    lane = lax.broadcasted_iota(I32, (1, A_HEADS * LANES), 1)
    ones_lane = (lane % LANES == A_VDIM).astype(F32)
    v_ref[...] = (jnp.dot(kvn, wuv_ref[...], preferred_element_type=F32) + ones_lane).astype(BF16)
    k_pe = mm(C_KR, LANES) * cos + mm(C_KRS, LANES) * sin
    scale = A_QK ** -0.5 * LOG2_E
    for h in range(A_HEADS):
        sl = slice(h * LANES, (h + 1) * LANES)
        q_ref[:, sl] = ((q[:, sl] * cos + qs[:, sl] * sin) * scale).astype(BF16)
        k_ref[:, sl] = (kn[:, sl] + k_pe).astype(BF16)


def _proj_weights(w_in, w_uq, w_ukv):
    o = IN_OFFSETS
    half = A_ROPE // 2
    w_q, w_kv, w_kr = w_in[:, o[0]:o[1]], w_in[:, o[1]:o[2]], w_in[:, o[2]:o[3]]
    w_xm, w_vm, w_op = w_in[:, o[3]:o[4]], w_in[:, o[4]:o[5]], w_in[:, o[5]:o[6]]
    w_i, w_f, w_gm, w_ga = w_in[:, o[6]:o[7]], w_in[:, o[7]:o[8]], w_in[:, o[8]:o[9]], w_in[:, o[9]:o[10]]
    d = w_in.shape[0]
    z = lambda n: jnp.zeros((d, n), w_in.dtype)
    kr = jnp.concatenate([z(A_NOPE), w_kr, z(LANES - A_QK)], axis=1)
    krs = jnp.concatenate([z(A_NOPE), -w_kr[:, half:], w_kr[:, :half], z(LANES - A_QK)], axis=1)
    gate = jnp.concatenate([w_i, w_f, z(LANES - 2 * M_HEADS)], axis=1)
    w_r = jnp.concatenate([w_q, w_kv, kr, krs, gate, w_xm, w_vm, w_op, w_gm, w_ga], axis=1).astype(BF16)

    uq = w_uq.reshape(A_Q_RANK, A_HEADS, A_QK)
    zq = jnp.zeros((A_Q_RANK, A_HEADS, LANES - A_QK), w_uq.dtype)
    zn = jnp.zeros((A_Q_RANK, A_HEADS, A_NOPE), w_uq.dtype)
    uq_pad = jnp.concatenate([uq, zq], axis=-1).reshape(A_Q_RANK, A_HEADS * LANES).astype(BF16)
    uqs_pad = jnp.concatenate([zn, -uq[..., A_NOPE + half:], uq[..., A_NOPE:A_NOPE + half], zq],
                              axis=-1).reshape(A_Q_RANK, A_HEADS * LANES).astype(BF16)
    ukv = w_ukv.reshape(A_KV_RANK, A_HEADS, A_NOPE + A_VDIM)
    zk = jnp.zeros((A_KV_RANK, A_HEADS, LANES - A_NOPE), w_ukv.dtype)
    uk_pad = jnp.concatenate([ukv[..., :A_NOPE], zk], axis=-1).reshape(A_KV_RANK, A_HEADS * LANES).astype(BF16)
    zv = jnp.zeros((A_KV_RANK, A_HEADS, LANES - A_VDIM), w_ukv.dtype)
    uv_pad = jnp.concatenate([ukv[..., A_NOPE:], zv], axis=-1).reshape(A_KV_RANK, A_HEADS * LANES).astype(BF16)
    return w_r, uq_pad, uqs_pad, uk_pad, uv_pad


def _input_projection(x2, pos2, w_in, g_qlat, g_kvlat, w_uq, w_ukv):
    t = x2.shape[0]
    tm = min(PROJ_ROWS, t)
    w_r, uq_pad, uqs_pad, uk_pad, uv_pad = _proj_weights(w_in, w_uq, w_ukv)
    half = A_ROPE // 2
    inv_freq = ROPE_THETA ** (-jnp.arange(half, dtype=F32) / half)
    invf = jnp.concatenate([jnp.zeros((A_NOPE,), F32), inv_freq, inv_freq,
                            jnp.zeros((LANES - A_QK,), F32)]).reshape(1, LANES)
    hw = A_HEADS * LANES
    row = lambda n: pl.BlockSpec((tm, n), lambda i: (i, 0))
    outs = [
        jax.ShapeDtypeStruct((t, hw), BF16), jax.ShapeDtypeStruct((t, hw), BF16), jax.ShapeDtypeStruct((t, hw), BF16),
        jax.ShapeDtypeStruct((t, M_WIDTH), BF16), jax.ShapeDtypeStruct((t, M_WIDTH), BF16),
        jax.ShapeDtypeStruct((t, M_WIDTH), BF16),
        jax.ShapeDtypeStruct((t, D_MODEL), BF16), jax.ShapeDtypeStruct((t, D_MODEL), BF16),
        jax.ShapeDtypeStruct((t, LANES), F32),
    ]
    return pl.pallas_call(
        _proj_kernel,
        out_shape=outs,
        grid=(t // tm,),
        in_specs=[row(D_MODEL), row(1), _const_spec(w_r.shape), _const_spec((1, A_Q_RANK)),
                  _const_spec((1, A_KV_RANK)), _const_spec(uq_pad.shape), _const_spec(uqs_pad.shape),
                  _const_spec(uk_pad.shape), _const_spec(uv_pad.shape), _const_spec((1, LANES))],
        out_specs=[row(hw), row(hw), row(hw), row(M_WIDTH), row(M_WIDTH), row(M_WIDTH), row(D_MODEL), row(D_MODEL),
                   row(LANES)],
        compiler_params=_cparams(("parallel",)),
        name="input_projection",
    )(x2, pos2, w_r, g_qlat.reshape(1, -1), g_kvlat.reshape(1, -1), uq_pad, uqs_pad, uk_pad, uv_pad, invf)


def _log_sigmoid(v):
    return jnp.minimum(v, 0.0) - jnp.log1p(jnp.exp(-jnp.abs(v)))


def _mlstm_kernel(xm_ref, vm_ref, op_ref, gate_ref, convw_ref, convb_ref, wqk_ref, gbias_ref, ghead_ref,
                  skip_ref, tril_ref, out_ref, xpad_s, ctn_s, m_s, *, seq_len):
    s = seq_len
    lc = M_CHUNK_ROWS
    halo = 8
    for bb in range(MLSTM_SEQS):
        xpad_s[bb, 0:halo, :] = jnp.zeros((halo, M_WIDTH), F32)
        xpad_s[bb, halo:, :] = xm_ref[bb * s:(bb + 1) * s, :].astype(F32)
    ctn_s[...] = jnp.zeros_like(ctn_s)
    m_s[...] = jnp.zeros_like(m_s)
    rows = lax.broadcasted_iota(I32, (lc, lc), 0)
    cols = lax.broadcasted_iota(I32, (lc, lc), 1)
    causal = rows >= cols
    row_id = lax.broadcasted_iota(I32, (lc, LANES), 0)
    ones_blk = (lax.broadcasted_iota(I32, (lc, LANES), 1) == 0).astype(BF16)
    kscale = M_HEAD_DIM ** -0.5
    hsl = [slice(h * M_HEAD_DIM, (h + 1) * M_HEAD_DIM) for h in range(M_HEADS)]
    hcl = [slice(h, h + 1) for h in range(M_HEADS)]
    streams = [(bb, h) for bb in range(MLSTM_SEQS) for h in range(M_HEADS)]

    def gates(bb, r0):
        win = xpad_s[bb, pl.ds(r0 - bb * s, lc + halo), :]
        conv = convb_ref[...]
        for j in range(M_CONV):
            lo = halo - (M_CONV - 1) + j
            conv = conv + win[lo:lo + lc, :] * convw_ref[j:j + 1, :]
        xc = conv * jax.nn.sigmoid(conv)
        g = gate_ref[pl.ds(r0, lc), :] + gbias_ref[...]
        ls = _log_sigmoid(g)
        bc = jnp.dot(tril_ref[...], ls, preferred_element_type=F32, precision=lax.Precision.HIGHEST)
        b0 = pltpu.roll(bc, LANES - M_HEADS, axis=1)
        u = g - b0
        cm = u
        shift = 1
        while shift < lc:
            cm = jnp.maximum(cm, jnp.where(row_id >= shift, pltpu.roll(cm, shift, axis=0), -jnp.inf))
            shift *= 2
        m_prev = m_s[bb:bb + 1, :]
        g_inter = b0 + m_prev
        m = jnp.maximum(g_inter, b0 + cm)
        b_tot = b0[lc - 1:lc, :]
        aw = b_tot - b0 + g
        m_chunk = jnp.max(aw, axis=0, keepdims=True)
        m_new = jnp.maximum(b_tot + m_prev, m_chunk)
        m_s[bb:bb + 1, :] = m_new
        return dict(xc=xc, w_inter=jnp.exp(g_inter - m), e_negm=jnp.exp(-m), a_mat=b0 - m, wa=jnp.exp(aw - m_chunk),
                    s_old=jnp.exp(b_tot + m_prev - m_new), s_new=jnp.exp(m_chunk - m_new), u_t=u.T)

    def chunk(c, carry):
        r0s = [pl.multiple_of(bb * s + c * lc, lc) for bb in range(MLSTM_SEQS)]
        gs = [gates(bb, r0s[bb]) for bb in range(MLSTM_SEQS)]
        qk = [jnp.dot(gs[bb]["xc"].astype(BF16), wqk_ref[...], preferred_element_type=F32)
              for bb in range(MLSTM_SEQS)]
        q_b = [qk[bb][:, hsl[h]].astype(BF16) for bb, h in streams]
        k_f = [qk[bb][:, M_WIDTH + h * M_HEAD_DIM:M_WIDTH + (h + 1) * M_HEAD_DIM] * kscale for bb, h in streams]
        v_aug = [jnp.concatenate([vm_ref[pl.ds(r0s[bb], lc), hsl[h]], ones_blk], axis=1) for bb, h in streams]
        ctn_prev = [ctn_s[bb * M_HEADS + h] for bb, h in streams]
        s_raw = [lax.dot_general(q_b[i], k_f[i].astype(BF16), (((1,), (1,)), ((), ())), preferred_element_type=F32)
                 for i in range(len(streams))]
        inter = [jnp.dot(q_b[i], ctn_prev[i].astype(BF16), preferred_element_type=F32)
                 for i in range(len(streams))]
        ctn_c = [lax.dot_general((k_f[i] * gs[bb]["wa"][:, hcl[h]]).astype(BF16), v_aug[i], (((0,), (0,)), ((), ())),
                                 preferred_element_type=F32) for i, (bb, h) in enumerate(streams)]
        sc_b = [(s_raw[i] * jnp.exp(jnp.where(causal, gs[bb]["a_mat"][:, hcl[h]] + gs[bb]["u_t"][hcl[h], :],
                                              -jnp.inf))).astype(BF16) for i, (bb, h) in enumerate(streams)]
        intra = [jnp.dot(sc_b[i], v_aug[i], preferred_element_type=F32) for i in range(len(streams))]
        for i, (bb, h) in enumerate(streams):
            g = gs[bb]
            ctn_s[bb * M_HEADS + h] = g["s_old"][:, hcl[h]] * ctn_prev[i] + g["s_new"][:, hcl[h]] * ctn_c[i]
            wi = g["w_inter"][:, hcl[h]]
            num = wi * inter[i][:, :M_HEAD_DIM] + intra[i][:, :M_HEAD_DIM]
            den = wi * inter[i][:, M_HEAD_DIM:M_HEAD_DIM + 1] + intra[i][:, M_HEAD_DIM:M_HEAD_DIM + 1]
            hh = num / jnp.maximum(jnp.abs(den), g["e_negm"][:, hcl[h]])
            mu = jnp.mean(hh, axis=-1, keepdims=True)
            dv = hh - mu
            var = jnp.mean(dv * dv, axis=-1, keepdims=True)
            hn = dv * lax.rsqrt(var + EPS) * ghead_ref[:, hsl[h]]
            y = (jax.nn.sigmoid(op_ref[pl.ds(r0s[bb], lc), hsl[h]].astype(F32))
                 * (hn + skip_ref[:, hsl[h]] * g["xc"][:, hsl[h]]))
            out_ref[pl.ds(r0s[bb], lc), hsl[h]] = y.astype(BF16)
        return carry

    lax.fori_loop(0, s // lc, chunk, 0)


def _mlstm_branch(xm, vm, op, gate, bsz, s, b_i, b_f, conv_w, conv_b, w_mq, w_mk, g_mhead, w_mskip):
    lc = M_CHUNK_ROWS
    gbias = jnp.concatenate([b_i, b_f, jnp.zeros((LANES - 2 * M_HEADS,), F32)]).reshape(1, LANES)
    tril = jnp.tril(jnp.ones((lc, lc), F32))
    eye = jnp.eye(M_HEADS, dtype=F32)
    block_diag = lambda w: jnp.einsum('hde,hg->hdge', w, eye).reshape(M_WIDTH, M_WIDTH)
    w_qk = jnp.concatenate([block_diag(w_mq), block_diag(w_mk)], axis=1).astype(BF16)
    nseq = MLSTM_SEQS
    assert bsz % nseq == 0
    seq = lambda n: pl.BlockSpec((nseq * s, n), lambda b: (b, 0))
    return pl.pallas_call(
        functools.partial(_mlstm_kernel, seq_len=s),
        out_shape=jax.ShapeDtypeStruct((bsz * s, M_WIDTH), BF16),
        grid=(bsz // nseq,),
        in_specs=[seq(M_WIDTH), seq(M_WIDTH), seq(M_WIDTH), seq(LANES),
                  _const_spec((M_CONV, M_WIDTH)), _const_spec((1, M_WIDTH)),
                  _const_spec((M_WIDTH, 2 * M_WIDTH)),
                  _const_spec((1, LANES)), _const_spec((1, M_WIDTH)), _const_spec((1, M_WIDTH)),
                  _const_spec((lc, lc))],
        out_specs=seq(M_WIDTH),
        scratch_shapes=[pltpu.VMEM((nseq, s + 8, M_WIDTH), F32),
                        pltpu.VMEM((nseq * M_HEADS, M_HEAD_DIM, 2 * LANES), F32), pltpu.VMEM((8, LANES), F32)],
        compiler_params=_cparams(("parallel",)),
        name="mlstm_branch",
    )(xm, vm, op, gate, conv_w, conv_b.reshape(1, -1), w_qk, gbias, g_mhead.reshape(1, -1), w_mskip.reshape(1, -1),
      tril)


def _attn_kernel(q_ref, k_ref, v_ref, o_ref, s_scr, m_s, acc_s):
    tq = q_ref.shape[0]
    i = pl.program_id(1)
    rows = lax.broadcasted_iota(I32, (tq, tq), 0)
    cols = lax.broadcasted_iota(I32, (tq, tq), 1)

    def lane_tile_max(sc):
        out = sc[:, :LANES]
        for t in range(1, tq // LANES):
            out = jnp.maximum(out, sc[:, t * LANES:(t + 1) * LANES])
        return out

    heads = [slice(h * LANES, (h + 1) * LANES) for h in range(A_HEADS)]
    m_s[...] = jnp.full(m_s.shape, NEG_BIG, F32)
    acc_s[...] = jnp.zeros_like(acc_s)

    def pass_a(j, diagonal):
        k0 = pl.multiple_of(j * tq, tq)
        for h, hs in enumerate(heads):
            sc = lax.dot_general(q_ref[:, hs], k_ref[pl.ds(k0, tq), hs], (((1,), (1,)), ((), ())),
                                 preferred_element_type=F32)
            if diagonal:
                sc = jnp.where(cols <= rows, sc, NEG_BIG)
            s_scr[h, j] = sc
            m_s[h] = jnp.maximum(m_s[h], lane_tile_max(sc))

    def for_blocks(n, fn):
        def pair(p, c):
            fn(2 * p)
            fn(2 * p + 1)
            return c

        lax.fori_loop(0, n // 2, pair, 0)

        @pl.when(n % 2 == 1)
        def _():
            fn(n - 1)

    for_blocks(i, lambda j: pass_a(j, False))
    pass_a(i, True)
    for h in range(A_HEADS):
        m_s[h] = jnp.broadcast_to(jnp.max(m_s[h], axis=1, keepdims=True), (tq, LANES))

    def pass_b(j):
        k0 = pl.multiple_of(j * tq, tq)
        for h, hs in enumerate(heads):
            m_row = m_s[h]
            p = jnp.exp2(s_scr[h, j] - jnp.concatenate([m_row] * (tq // LANES), axis=1)).astype(BF16)
            acc_s[h] += jnp.dot(p, v_ref[pl.ds(k0, tq), hs], preferred_element_type=F32)

    for_blocks(i + 1, pass_b)
    for h, hs in enumerate(heads):
        acc = acc_s[h]
        o_ref[:, hs] = (acc / acc[:, A_VDIM:A_VDIM + 1]).astype(BF16)


def _mla_attention(q, k, v, bsz, s):
    tq = min(ATT_Q, s)
    nq = s // tq
    hw = A_HEADS * LANES
    return pl.pallas_call(
        _attn_kernel,
        out_shape=jax.ShapeDtypeStruct(q.shape, BF16),
        grid=(bsz, nq),
        in_specs=[pl.BlockSpec((tq, hw), lambda b, i: (b * nq + i, 0)),
                  pl.BlockSpec((s, hw), lambda b, i: (b, 0)),
                  pl.BlockSpec((s, hw), lambda b, i: (b, 0))],
        out_specs=pl.BlockSpec((tq, hw), lambda b, i: (b * nq + i, 0)),
        scratch_shapes=[pltpu.VMEM((A_HEADS, nq, tq, tq), F32), pltpu.VMEM((A_HEADS, tq, LANES), F32),
                        pltpu.VMEM((A_HEADS, tq, LANES), F32)],
        compiler_params=_cparams(("parallel", "arbitrary")),
        name="mla_attention",
    )(q, k, v)


def _merge_kernel(ym_ref, oa_ref, gm_ref, ga_ref, x_ref, wbm_ref, wba_ref, wmix_ref, g_ref, b_ref, h_ref):
    y_m = jnp.dot(ym_ref[...], wbm_ref[...], preferred_element_type=F32)
    y_a = jnp.dot(oa_ref[...], wba_ref[...], preferred_element_type=F32)
    mixed = jax.nn.sigmoid(gm_ref[...].astype(F32)) * y_m + jax.nn.sigmoid(ga_ref[...].astype(F32)) * y_a
    mix = jnp.dot(mixed.astype(BF16), wmix_ref[...], preferred_element_type=F32)
    h_ref[...] = _layer_norm_rows(DN_ALPHA * x_ref[...] + mix, g_ref[...], b_ref[...])


def _merge(ym, oa, gm, ga, x2, w_br_m, w_br_a, w_mix_out, ln_g, ln_b):
    t = x2.shape[0]
    tm = min(MERGE_ROWS, t)
    wba = jnp.concatenate([w_br_a.reshape(A_HEADS, A_VDIM, D_MODEL),
                           jnp.zeros((A_HEADS, LANES - A_VDIM, D_MODEL), w_br_a.dtype)], axis=1)
    wba = wba.reshape(A_HEADS * LANES, D_MODEL).astype(BF16)
    row = lambda n: pl.BlockSpec((tm, n), lambda i: (i, 0))
    return pl.pallas_call(
        _merge_kernel,
        out_shape=jax.ShapeDtypeStruct((t, D_MODEL), F32),
        grid=(t // tm,),
        in_specs=[row(M_WIDTH), row(A_HEADS * LANES), row(D_MODEL), row(D_MODEL), row(D_MODEL),
                  _const_spec((M_WIDTH, D_MODEL)), _const_spec((A_HEADS * LANES, D_MODEL)),
                  _const_spec((D_MODEL, D_MODEL)), _const_spec((1, D_MODEL)), _const_spec((1, D_MODEL))],
        out_specs=row(D_MODEL),
        compiler_params=_cparams(("parallel",)),
        name="merge_deepnorm1",
    )(ym, oa, gm, ga, x2, w_br_m.astype(BF16), wba, w_mix_out.astype(BF16), ln_g.reshape(1, -1), ln_b.reshape(1, -1))


def _kv_kernel(m_ref, w_ref, o_ref):
    o_ref[...] = jnp.dot(m_ref[...].astype(BF16), w_ref[...], preferred_element_type=F32).astype(BF16)


def _mem_kv(mem2, w_ckv):
    r = mem2.shape[0]
    tm = min(512, r)
    return pl.pallas_call(
        _kv_kernel,
        out_shape=jax.ShapeDtypeStruct((r, 2 * D_MODEL), BF16),
        grid=(r // tm,),
        in_specs=[pl.BlockSpec((tm, D_MODEL), lambda i: (i, 0)), _const_spec((D_MODEL, 2 * D_MODEL))],
        out_specs=pl.BlockSpec((tm, 2 * D_MODEL), lambda i: (i, 0)),
        compiler_params=_cparams(("parallel",)),
        name="memory_kv",
    )(mem2, w_ckv.astype(BF16))


def _xattn_kernel(h_ref, kv_ref, wq_ref, wo_ref, g_ref, b_ref, wr_ref, br_ref, *rest):
    h2_ref, idx_ref, gate_ref = rest[-3:]
    h1 = h_ref[...]
    q = jnp.dot(h1.astype(BF16), wq_ref[...], preferred_element_type=F32).astype(BF16)
    scale = X_HEAD_DIM ** -0.5
    ksl = [slice(hd * X_HEAD_DIM, (hd + 1) * X_HEAD_DIM) for hd in range(X_HEADS)]
    vsl = [slice(D_MODEL + hd * X_HEAD_DIM, D_MODEL + (hd + 1) * X_HEAD_DIM) for hd in range(X_HEADS)]
    scs = [lax.dot_general(q[:, ks], kv_ref[:, ks], (((1,), (1,)), ((), ())), preferred_element_type=F32) * scale
           for ks in ksl]
    ps = [jnp.exp(sc - jnp.max(sc, axis=1, keepdims=True)) for sc in scs]
    ps = [(p / jnp.sum(p, axis=1, keepdims=True)).astype(BF16) for p in ps]
    o = jnp.concatenate([jnp.dot(p, kv_ref[:, vs], preferred_element_type=F32).astype(BF16)
                         for p, vs in zip(ps, vsl)], axis=1)
    att = jnp.dot(o, wo_ref[...], preferred_element_type=F32)
    h2 = _layer_norm_rows(DN_ALPHA * h1 + att, g_ref[...], b_ref[...])
    for c in range(ROW_TILES):
        h2_ref[pl.ds(c, h2.shape[0], stride=ROW_TILES), :] = h2[:, c * LANES:(c + 1) * LANES]

    h2_hi = h2.astype(BF16)
    h2_lo = (h2 - h2_hi.astype(F32)).astype(BF16)
    hi_prod = jnp.dot(h2_hi, wr_ref[...], preferred_element_type=F32)
    lo_prod = jnp.dot(h2_lo, wr_ref[:, :LANES], preferred_element_type=F32)
    logits = hi_prod[:, :LANES] + (hi_prod[:, LANES:] + lo_prod) + br_ref[...]
    tm = logits.shape[0]
    lane = lax.broadcasted_iota(I32, (tm, LANES), 1)
    lane_f = lane.astype(F32)
    work = jnp.where(lane < N_EXPERTS, logits, -jnp.inf)
    vals, idxs = [], []
    for _ in range(TOP_K):
        mx = jnp.max(work, axis=1, keepdims=True)
        ix = jnp.min(jnp.where(work == mx, lane_f, float(LANES)), axis=1, keepdims=True)
        vals.append(mx)
        idxs.append(ix)
        work = jnp.where(lane_f == ix, -jnp.inf, work)
    es = [jnp.exp(vv - vals[0]) for vv in vals]
    tot = es[0] + es[1] + es[2] + es[3]
    idx_slab = jnp.zeros((tm, LANES), F32)
    gate_slab = jnp.zeros((tm, LANES), F32)
    for kk in range(TOP_K):
        idx_slab = jnp.where(lane == kk, idxs[kk], idx_slab)
        gate_slab = jnp.where(lane == kk, es[kk] / tot, gate_slab)
    idx_ref[...] = idx_slab.astype(I32)
    gate_ref[...] = gate_slab


def _cross_attention_router(h1, kv, n_mem, b0, bsz, s, w_cq, w_co, ln_g, ln_b, w_router, b_router, after):
    t = bsz * s
    tm = min(XATT_ROWS, s)
    ns = s // tm
    wr = jnp.concatenate([w_router, jnp.zeros((D_MODEL, LANES - N_EXPERTS), F32)], axis=1)
    wr_hi = wr.astype(BF16)
    wr = jnp.concatenate([wr_hi, (wr - wr_hi.astype(F32)).astype(BF16)], axis=1)
    br = jnp.concatenate([b_router, jnp.zeros((LANES - N_EXPERTS,), F32)]).reshape(1, LANES)
    row = lambda n: pl.BlockSpec((tm, n), lambda b, i: (b * ns + i, 0))
    h1_rows = pl.BlockSpec((tm, D_MODEL), lambda b, i: ((b0 + b) * ns + i, 0))
    in_specs = [h1_rows, pl.BlockSpec((n_mem, 2 * D_MODEL), lambda b, i: (b0 + b, 0)),
                _const_spec((D_MODEL, D_MODEL)), _const_spec((D_MODEL, D_MODEL)),
                _const_spec((1, D_MODEL)), _const_spec((1, D_MODEL)),
                _const_spec((D_MODEL, 2 * LANES)), _const_spec((1, LANES))]
    args = [h1, kv, w_cq.astype(BF16), w_co.astype(BF16), ln_g.reshape(1, -1), ln_b.reshape(1, -1), wr, br]
    if after is not None:
        in_specs.append(pl.BlockSpec(memory_space=pl.ANY))
        args.append(after)
    return pl.pallas_call(
        _xattn_kernel,
        out_shape=[jax.ShapeDtypeStruct((t * ROW_TILES, LANES), F32), jax.ShapeDtypeStruct((t, LANES), I32),
                   jax.ShapeDtypeStruct((t, LANES), F32)],
        grid=(bsz, ns),
        in_specs=in_specs,
        out_specs=[pl.BlockSpec((tm * ROW_TILES, LANES), lambda b, i: (b * ns + i, 0)), row(LANES), row(LANES)],
        compiler_params=_cparams(("parallel", "parallel")),
        name="cross_attention_router",
    )(*args)


def _rank_kernel(idx_ref, ltri_ref, rank_ref, cnt_ref, carry_s):
    @pl.when(pl.program_id(0) == 0)
    def _():
        carry_s[...] = jnp.zeros_like(carry_s)

    idx = idx_ref[...]
    tm = idx.shape[0]
    lane = lax.broadcasted_iota(I32, (tm, LANES), 1)
    onehots = [(lane == idx[:, kk:kk + 1]).astype(F32) for kk in range(TOP_K)]
    sel = onehots[0] + onehots[1] + onehots[2] + onehots[3]
    before = jnp.dot(ltri_ref[...], sel.astype(BF16), preferred_element_type=F32) + carry_s[0:1, :]
    rank_slab = jnp.zeros((tm, LANES), F32)
    for kk in range(TOP_K):
        r = jnp.sum(onehots[kk] * before, axis=1, keepdims=True)
        rank_slab = jnp.where(lane == kk, r, rank_slab)
    rank_ref[...] = rank_slab.astype(I32)
    carry_s[0:1, :] = carry_s[0:1, :] + jnp.sum(sel, axis=0, keepdims=True)
    cnt_ref[...] = jnp.broadcast_to(carry_s[0:1, :], cnt_ref.shape).astype(I32)


def _routing_ranks(idx_slab):
    t = idx_slab.shape[0]
    tm = min(RANK_ROWS, t)
    ltri = jnp.tril(jnp.ones((tm, tm), BF16), k=-1)
    return pl.pallas_call(
        _rank_kernel,
        out_shape=[jax.ShapeDtypeStruct((t, LANES), I32), jax.ShapeDtypeStruct((8, LANES), I32)],
        grid=(t // tm,),
        in_specs=[pl.BlockSpec((tm, LANES), lambda i: (i, 0)), _const_spec((tm, tm))],
        out_specs=[pl.BlockSpec((tm, LANES), lambda i: (i, 0)), _const_spec((8, LANES))],
        scratch_shapes=[pltpu.VMEM((8, LANES), F32)],
        compiler_params=_cparams(("arbitrary",)),
        name="routing_ranks",
    )(idx_slab, ltri)


def _dest_kernel(idx_ref, rank_ref, pstart_ref, dest_ref):
    idx = idx_ref[...]
    tm = idx.shape[0]
    lane = lax.broadcasted_iota(I32, (tm, LANES), 1)
    dest = rank_ref[...].astype(F32)
    pstart = pstart_ref[0:1, :].astype(F32)
    for kk in range(TOP_K):
        start = jnp.sum(jnp.where(lane == idx[:, kk:kk + 1], pstart, 0.0), axis=1, keepdims=True)
        dest = dest + jnp.where(lane == kk, start, 0.0)
    dest_ref[...] = dest.T[0:8, :].astype(I32)


def _dest_rows(idx_slab, rank_slab, pstart_row):
    t = rank_slab.shape[0]
    tm = min(RANK_ROWS, t)
    return pl.pallas_call(
        _dest_kernel,
        out_shape=jax.ShapeDtypeStruct((8, t), I32),
        grid=(t // tm,),
        in_specs=[pl.BlockSpec((tm, LANES), lambda i: (i, 0)), pl.BlockSpec((tm, LANES), lambda i: (i, 0)),
                  _const_spec((8, LANES))],
        out_specs=pl.BlockSpec((8, tm), lambda i: (0, i)),
        compiler_params=_cparams(("parallel",)),
        name="routing_dest",
    )(idx_slab, rank_slab, pstart_row)


def _sc_mesh():
    return plsc.VectorSubcoreMesh(core_axis_name="c", subcore_axis_name="s", num_cores=SC_CORES,
                                  num_subcores=SC_SUBCORES)


def _sc_worker():
    return lax.axis_index("s") * SC_CORES + lax.axis_index("c")


def _dispatch(h2, dest_kt, pad_rows, n_rows):
    t = dest_kt.shape[1]
    w = SC_ROWS
    per_w = t // SC_WORKERS
    steps = per_w // w
    assert steps % 2 == 0 and steps * w * SC_WORKERS == t
    pad_steps = pad_rows.shape[0] // (SC_WORKERS * w)
    dest_w = dest_kt.reshape(TOP_K, SC_WORKERS, steps, w)
    pad_w = pad_rows.reshape(SC_WORKERS, pad_steps, w)
    zeros = jnp.zeros((w, ROW_TILES, LANES), F32)

    def body(x_hbm, dest_hbm, pad_hbm, zeros_hbm, o_hbm, idx_v, pad_v, rows_v, sem_in, sem_out):
        wid = _sc_worker()
        tok0 = wid * per_w
        for kk in range(TOP_K):
            pltpu.sync_copy(dest_hbm.at[kk, wid], idx_v.at[pl.ds(kk * steps, steps)])
        pltpu.sync_copy(pad_hbm.at[wid], pad_v)
        pltpu.sync_copy(zeros_hbm, rows_v.at[0])
        for j in range(pad_steps):
            pltpu.sync_copy(rows_v.at[0], o_hbm.at[pad_v.at[j]])

        def load(step, b):
            return pltpu.make_async_copy(x_hbm.at[pl.ds(tok0 + step * w, w)], rows_v.at[b], sem_in.at[b])

        def scatter(step, kk, b):
            return pltpu.make_async_copy(rows_v.at[b], o_hbm.at[idx_v.at[kk * steps + step]], sem_out.at[b])

        load(0, 0).start()

        @pl.loop(0, steps, step=2)
        def _(s0):
            for b in range(2):
                step = s0 + b
                load(step, b).wait()

                @pl.when(step >= 1)
                def _():
                    for kk in range(TOP_K):
                        scatter(step - 1, kk, 1 - b).wait()

                @pl.when(step + 1 < steps)
                def _():
                    load(step + 1, 1 - b).start()

                for kk in range(TOP_K):
                    scatter(step, kk, b).start()

        for kk in range(TOP_K):
            scatter(steps - 1, kk, 1).wait()

    return pl.kernel(
        body,
        out_type=jax.ShapeDtypeStruct((n_rows, ROW_TILES, LANES), F32),
        mesh=_sc_mesh(),
        scratch_types=[pltpu.VMEM((steps * TOP_K, w), I32), pltpu.VMEM((pad_steps, w), I32),
                       pltpu.VMEM((2, w, ROW_TILES, LANES), F32), pltpu.SemaphoreType.DMA((2,)),
                       pltpu.SemaphoreType.DMA((2,))],
        name="moe_dispatch_sc",
    )(h2, dest_w, pad_w, zeros)


def _gather_rows(yb, dest_kt):
    n = dest_kt.shape[0]
    w = SC_ROWS
    per_w = n // SC_WORKERS
    steps = per_w // w
    assert steps % 2 == 0 and steps * w * SC_WORKERS == n
    dest_w = dest_kt.reshape(SC_WORKERS, steps, w)

    def body(y_hbm, dest_hbm, o_hbm, idx_v, rows_v, sem_g, sem_w):
        wid = _sc_worker()
        row0 = wid * per_w
        pltpu.sync_copy(dest_hbm.at[wid], idx_v)

        def gather(step, b):
            return pltpu.make_async_copy(y_hbm.at[idx_v.at[step]], rows_v.at[b], sem_g.at[b])

        def write(step, b):
            return pltpu.make_async_copy(rows_v.at[b], o_hbm.at[pl.ds(row0 + step * w, w)], sem_w.at[b])

        gather(0, 0).start()

        @pl.loop(0, steps, step=2)
        def _(s0):
            for b in range(2):
                step = s0 + b
                gather(step, b).wait()

                @pl.when(step >= 1)
                def _():
                    write(step - 1, 1 - b).wait()

                @pl.when(step + 1 < steps)
                def _():
                    gather(step + 1, 1 - b).start()

                write(step, b).start()

        write(steps - 1, 1).wait()

    return pl.kernel(
        body,
        out_type=jax.ShapeDtypeStruct((n, ROW_TILES, LANES), F32),
        mesh=_sc_mesh(),
        scratch_types=[pltpu.VMEM((steps, w), I32), pltpu.VMEM((2, w, ROW_TILES, LANES), F32),
                       pltpu.SemaphoreType.DMA((2,)), pltpu.SemaphoreType.DMA((2,))],
        name="moe_gather_sc",
    )(yb, dest_w)


def _rows_2d(ref, n):
    return jnp.concatenate([ref[pl.ds(c, n, stride=ROW_TILES), :] for c in range(ROW_TILES)], axis=1)


def _store_rows(ref, val):
    for c in range(ROW_TILES):
        ref[pl.ds(c, val.shape[0], stride=ROW_TILES), :] = val[:, c * LANES:(c + 1) * LANES]


def _expert_kernel(blk_e_ref, first_ref, slot_ref, next_ref, nblk_ref, x_ref, wg_hbm, wl_hbm, bg_ref, bl_ref, wd_hbm,
                   bd_ref, y_ref, wg_buf, wl_buf, wd_stage, wd_buf, sem):
    i = pl.program_id(0)

    def fetch(e, sl):
        return [pltpu.make_async_copy(src.at[e], dst.at[sl], sem.at[sl, j])
                for j, (src, dst) in enumerate(((wg_hbm, wg_buf), (wl_hbm, wl_buf), (wd_hbm, wd_stage)))]

    @pl.when(i < nblk_ref[0])
    def _():
        sl = slot_ref[i]

        @pl.when(first_ref[i] == 1)
        def _():
            @pl.when(i == 0)
            def _():
                for cp in fetch(blk_e_ref[0], sl):
                    cp.start()

            for cp in fetch(blk_e_ref[i], sl):
                cp.wait()
            wd_buf[sl] = wd_stage[sl].astype(BF16)

            @pl.when(next_ref[i] >= 0)
            def _():
                for cp in fetch(next_ref[i], 1 - sl):
                    cp.start()

        xb = _rows_2d(x_ref, y_ref.shape[0] // ROW_TILES).astype(BF16)
        glu = jnp.dot(xb, wg_buf[sl], preferred_element_type=F32) + bg_ref[...]
        lin = jnp.dot(xb, wl_buf[sl], preferred_element_type=F32) + bl_ref[...]
        glu = jnp.minimum(glu, SWIGLU_LIMIT)
        lin = jnp.clip(lin, -SWIGLU_LIMIT, SWIGLU_LIMIT)
        act = glu * jax.nn.sigmoid(SWIGLU_ALPHA * glu) * (lin + 1.0)
        _store_rows(y_ref, jnp.dot(act.astype(BF16), wd_buf[sl], preferred_element_type=F32) + bd_ref[...])

    @pl.when(i >= nblk_ref[0])
    def _():
        y_ref[...] = jnp.zeros_like(y_ref)


def _split_kernel(w_ref, perm_ref, after_ref, g_ref, l_ref):
    del after_ref
    half = g_ref.shape[-1]
    sorted_cols = jnp.dot(w_ref[...].astype(BF16), perm_ref[...], preferred_element_type=F32)
    g_ref[...] = sorted_cols[:, :half].astype(BF16)
    l_ref[...] = sorted_cols[:, half:].astype(BF16)


def _split_gate_up(w_gu, after):
    e, d, n2 = w_gu.shape
    cw = 512
    src = np.concatenate([np.arange(0, cw, 2), np.arange(1, cw, 2)])
    perm = jnp.asarray(np.eye(cw, dtype=np.float32)[:, src], BF16)
    return pl.pallas_call(
        _split_kernel,
        out_shape=[jax.ShapeDtypeStruct((e, d, n2 // 2), BF16), jax.ShapeDtypeStruct((e, d, n2 // 2), BF16)],
        grid=(e, n2 // cw),
        in_specs=[pl.BlockSpec((None, d, cw), lambda i, c: (i, 0, c)), _const_spec((cw, cw)),
                  pl.BlockSpec(memory_space=pl.ANY)],
        out_specs=[pl.BlockSpec((None, d, cw // 2), lambda i, c: (i, 0, c)),
                   pl.BlockSpec((None, d, cw // 2), lambda i, c: (i, 0, c))],
        compiler_params=_cparams(("parallel", "parallel")),
        name="split_gate_up",
    )(w_gu, perm, after)


def _experts(xb, blk_meta, w_glu, w_lin, b_glu, b_lin, w_dn, b_dn):
    n_rows = xb.shape[0] // ROW_TILES
    bm = EXPERT_ROWS
    vec = lambda n: pl.BlockSpec((None, 1, n), lambda i, be, *_: (be[i], 0, 0))
    rows = pl.BlockSpec((bm * ROW_TILES, LANES), lambda i, *_: (i, 0))
    hbm = pl.BlockSpec(memory_space=pl.ANY)
    grid_spec = pltpu.PrefetchScalarGridSpec(
        num_scalar_prefetch=5,
        grid=(n_rows // bm,),
        in_specs=[rows, hbm, hbm, vec(D_EXPERT), vec(D_EXPERT), hbm, vec(D_MODEL)],
        out_specs=rows,
        scratch_shapes=[pltpu.VMEM((2, D_MODEL, D_EXPERT), BF16), pltpu.VMEM((2, D_MODEL, D_EXPERT), BF16),
                        pltpu.VMEM((2, D_EXPERT, D_MODEL), F32), pltpu.VMEM((2, D_EXPERT, D_MODEL), BF16),
                        pltpu.SemaphoreType.DMA((2, 3))],
    )
    return pl.pallas_call(
        _expert_kernel,
        out_shape=jax.ShapeDtypeStruct((n_rows * ROW_TILES, LANES), F32),
        grid_spec=grid_spec,
        compiler_params=_cparams(("arbitrary",)),
        name="moe_experts",
    )(*blk_meta, xb, w_glu, w_lin, b_glu, b_lin, w_dn, b_dn)


def _combine_kernel(h_ref, gate_ref, y0_ref, y1_ref, y2_ref, y3_ref, g_ref, b_ref, *rest):
    o_ref = rest[-1]
    tt = o_ref.shape[0]
    gates = gate_ref[...]
    acc = DN_ALPHA * _rows_2d(h_ref, tt)
    for kk, y_ref in enumerate((y0_ref, y1_ref, y2_ref, y3_ref)):
        acc = acc + gates[:, kk:kk + 1] * _rows_2d(y_ref, tt)
    o_ref[...] = _layer_norm_rows(acc, g_ref[...], b_ref[...])


def _combine(h2, gate_slab, yg, ln_g, ln_b, tok0, t_all, prev_out):
    t = h2.shape[0] // ROW_TILES
    tt = min(COMBINE_TOKENS, t)
    steps = t // tt
    assert tok0 % tt == 0
    first = tok0 // tt
    row = lambda m: pl.BlockSpec((tt, m), lambda i: (i, 0))
    tiles = lambda kk: pl.BlockSpec((tt * ROW_TILES, LANES), lambda i: (kk * steps + i, 0))
    in_specs = [tiles(0), row(LANES),
                tiles(0), tiles(1), tiles(2), tiles(3), _const_spec((1, D_MODEL)), _const_spec((1, D_MODEL))]
    args = [h2, gate_slab, yg, yg, yg, yg, ln_g.reshape(1, -1), ln_b.reshape(1, -1)]
    aliases = {}
    if prev_out is not None:
        in_specs.append(pl.BlockSpec(memory_space=pl.ANY))
        args.append(prev_out)
        aliases = {len(args) - 1: 0}
    return pl.pallas_call(
        _combine_kernel,
        out_shape=jax.ShapeDtypeStruct((t_all, D_MODEL), F32),
        grid=(steps,),
        in_specs=in_specs,
        out_specs=pl.BlockSpec((tt, D_MODEL), lambda i: (first + i, 0)),
        input_output_aliases=aliases,
        compiler_params=_cparams(("parallel",)),
        name="moe_combine",
    )(*args)


def _route_and_dispatch(h2, idx_slab):
    t = h2.shape[0] // ROW_TILES
    bm = EXPERT_ROWS
    n_rows = t * TOP_K + N_EXPERTS * bm
    n_blocks = n_rows // bm

    rank_slab, cnt = _routing_ranks(idx_slab)
    counts = cnt[0, :N_EXPERTS]
    padded = (counts + bm - 1) // bm * bm
    pends = jnp.cumsum(padded)
    pstarts = pends - padded
    pstart_row = jnp.zeros((8, LANES), I32).at[:, :N_EXPERTS].set(pstarts[None, :])
    blk_start = jnp.arange(n_blocks, dtype=I32) * bm
    blk_e = jnp.minimum(jnp.sum((pends[None, :] <= blk_start[:, None]).astype(I32), axis=1), N_EXPERTS - 1)
    nblk = (pends[-1] // bm).astype(I32).reshape(1)
    active = counts > 0
    ordinal = jnp.cumsum(active.astype(I32)) - 1
    eid = jnp.arange(N_EXPERTS, dtype=I32)
    later = active[None, :] & (eid[None, :] > eid[:, None])
    next_active = jnp.where(later.any(axis=1), jnp.argmax(later, axis=1).astype(I32), -1)
    blk_first = jnp.concatenate([jnp.ones((1,), I32), (blk_e[1:] != blk_e[:-1]).astype(I32)])
    blk_onehot = (blk_e[:, None] == eid[None, :]).astype(I32)
    blk_slot = jnp.sum(blk_onehot * ordinal[None, :], axis=1) % 2
    blk_next = jnp.sum(blk_onehot * next_active[None, :], axis=1)
    blk_meta = (blk_e, blk_first, blk_slot, blk_next, nblk)
    dest_kt = _dest_rows(idx_slab, rank_slab, pstart_row)[:TOP_K]
    pad_off = jnp.arange(bm, dtype=I32)[None, :]
    pad_rows = jnp.where(pad_off < (padded - counts)[:, None], (pstarts + counts)[:, None] + pad_off,
                         n_rows - 1).reshape(-1)
    xb = _dispatch(h2.reshape(t, ROW_TILES, LANES), dest_kt, pad_rows, n_rows)
    return xb.reshape(n_rows * ROW_TILES, LANES), blk_meta, dest_kt


def _part_batches(bsz):
    if bsz % 8 == 0:
        return tuple(share * bsz // 8 for share in MOE_PART_SHARES)
    return (bsz // 2, bsz - bsz // 2)


def _xattn_moe(h1, kv, bsz, s, w_cq, w_co, ln2_g, ln2_b, w_router, b_router, w_gu, b_gu, w_dn, b_dn, ln3_g, ln3_b):
    parts = []
    after = None
    n_mem = kv.shape[0] // bsz
    b0 = 0
    for nb in _part_batches(bsz):
        h2, idx_slab, gate_slab = _cross_attention_router(h1, kv, n_mem, b0, nb, s, w_cq, w_co, ln2_g, ln2_b,
                                                          w_router, b_router, after)
        b0 += nb
        parts.append((h2, gate_slab) + _route_and_dispatch(h2, idx_slab))
        after = parts[-1][-1]
    w_glu, w_lin = _split_gate_up(w_gu, after)
    expert_w = (w_glu, w_lin, b_gu[:, None, 0::2], b_gu[:, None, 1::2], w_dn, b_dn[:, None, :])
    out = None
    tok0 = 0
    for h2, gate_slab, xb, blk_meta, dest_kt in parts:
        n_rows = xb.shape[0] // ROW_TILES
        yb = _experts(xb, blk_meta, *expert_w)
        yg = _gather_rows(yb.reshape(n_rows, ROW_TILES, LANES), dest_kt.reshape(-1))
        out = _combine(h2, gate_slab, yg.reshape(-1, LANES), ln3_g, ln3_b, tok0, bsz * s, out)
        tok0 += h2.shape[0] // ROW_TILES
    return out


def kernel(x, mem, positions, w_in, b_igate, b_fgate, conv_w, conv_b, w_mq, w_mk, g_mhead, w_mskip, g_qlat, g_kvlat,
           w_uq, w_ukv, w_br_m, w_br_a, w_mix_out, ln1_g, ln1_b, w_cq, w_ckv, w_co, ln2_g, ln2_b, w_router, b_router,
           w_gu, b_gu, w_dn, b_dn, ln3_g, ln3_b):
    bsz, s, d = x.shape
    t = bsz * s
    h = x.reshape(t, d)
    pos2 = positions.reshape(t, 1)
    for l in range(DEPTH):
        q, k, v, xm, vm, op, gm, ga, gate = _input_projection(h, pos2, w_in[l], g_qlat[l], g_kvlat[l], w_uq[l],
                                                              w_ukv[l])
        ym = _mlstm_branch(xm, vm, op, gate, bsz, s, b_igate[l], b_fgate[l], conv_w[l], conv_b[l], w_mq[l], w_mk[l],
                           g_mhead[l], w_mskip[l])
        oa = _mla_attention(q, k, v, bsz, s)
        h1 = _merge(ym, oa, gm, ga, h, w_br_m[l], w_br_a[l], w_mix_out[l], ln1_g[l], ln1_b[l])
        kv = _mem_kv(mem.reshape(-1, d), w_ckv[l])
        h = _xattn_moe(h1, kv, bsz, s, w_cq[l], w_co[l], ln2_g[l], ln2_b[l], w_router[l], b_router[l], w_gu[l],
                       b_gu[l], w_dn[l], b_dn[l], ln3_g[l], ln3_b[l])
    return h.reshape(bsz, s, d)
```

```python
import functools

import numpy as np
import jax
import jax.numpy as jnp
from jax import lax
from jax.experimental import pallas as pl
from jax.experimental.pallas import tpu as pltpu
from jax.experimental.pallas import tpu_sc as plsc

F32 = jnp.float32
BF16 = jnp.bfloat16
I32 = jnp.int32

D_MODEL = 1024
N_MEM = 256
M_HEADS = 4
M_HEAD_DIM = 128
M_WIDTH = M_HEADS * M_HEAD_DIM
M_CONV = 4
A_HEADS = 8
A_NOPE = 64
A_ROPE = 32
A_QK = A_NOPE + A_ROPE
A_VDIM = 64
A_Q_RANK = 256
A_KV_RANK = 128
ROPE_THETA = 10000.0
X_HEADS = 4
X_HEAD_DIM = D_MODEL // X_HEADS
N_EXPERTS = 32
TOP_K = 4
D_EXPERT = D_MODEL
SWIGLU_ALPHA = 1.702
SWIGLU_LIMIT = 7.0
DEPTH = 1
DN_ALPHA = (2.0 * DEPTH) ** 0.25
EPS = 1e-5
IN_SPLITS = (A_Q_RANK, A_KV_RANK, A_ROPE, M_WIDTH, M_WIDTH, M_WIDTH, M_HEADS, M_HEADS, D_MODEL, D_MODEL)
IN_OFFSETS = tuple(int(v) for v in np.cumsum((0,) + IN_SPLITS))

LANES = 128
VMEM_LIMIT = 56 * 1024 * 1024

PROJ_ROWS = 512
M_CHUNK_ROWS = 128
MLSTM_SEQS = 2
ATT_Q = 256
MERGE_ROWS = 512
XATT_ROWS = 512
RANK_ROWS = 1024
EXPERT_ROWS = 256
COMBINE_TOKENS = 256
MOE_PARTS = 2
SC_CORES = 2
SC_SUBCORES = 16
SC_WORKERS = SC_CORES * SC_SUBCORES
SC_ROWS = 32
NEG_BIG = -1e30
LOG2_E = 1.4426950408889634
ROW_TILES = D_MODEL // LANES

C_QLAT = 0
C_KVLAT = C_QLAT + A_Q_RANK
C_KR = C_KVLAT + A_KV_RANK
C_KRS = C_KR + LANES
C_GATE = C_KRS + LANES
C_XM = C_GATE + LANES
C_VM = C_XM + M_WIDTH
C_OP = C_VM + M_WIDTH
C_GM = C_OP + M_WIDTH
C_GA = C_GM + D_MODEL
C_END = C_GA + D_MODEL


def _cparams(sem, vmem=VMEM_LIMIT):
    return pltpu.CompilerParams(dimension_semantics=sem, vmem_limit_bytes=vmem)


def _const_spec(shape):
    nd = len(shape)
    return pl.BlockSpec(shape, lambda *a: (0,) * nd)


def _layer_norm_rows(v, g, b):
    mu = jnp.mean(v, axis=-1, keepdims=True)
    d = v - mu
    var = jnp.mean(d * d, axis=-1, keepdims=True)
    return d * lax.rsqrt(var + EPS) * g + b


def _proj_kernel(x_ref, pos_ref, w_ref, gq_ref, gkv_ref, wuq_ref, wuqs_ref, wuk_ref, wuv_ref, invf_ref,
                 q_ref, k_ref, v_ref, xm_ref, vm_ref, op_ref, gm_ref, ga_ref, gate_ref):
    xb = x_ref[...].astype(BF16)

    def mm(lo, n):
        return jnp.dot(xb, w_ref[:, lo:lo + n], preferred_element_type=F32)

    xm_ref[...] = mm(C_XM, M_WIDTH).astype(BF16)
    vm_ref[...] = mm(C_VM, M_WIDTH).astype(BF16)
    op_ref[...] = mm(C_OP, M_WIDTH).astype(BF16)
    gm_ref[...] = mm(C_GM, D_MODEL).astype(BF16)
    ga_ref[...] = mm(C_GA, D_MODEL).astype(BF16)
    gate_ref[...] = mm(C_GATE, LANES)

    ang = pos_ref[...].astype(F32) * invf_ref[...]
    cos = jnp.cos(ang)
    sin = jnp.sin(ang)

    q_lat = mm(C_QLAT, A_Q_RANK)
    qn = (q_lat * lax.rsqrt(jnp.mean(q_lat * q_lat, axis=-1, keepdims=True) + EPS) * gq_ref[...]).astype(BF16)
    q = jnp.dot(qn, wuq_ref[...], preferred_element_type=F32)
    qs = jnp.dot(qn, wuqs_ref[...], preferred_element_type=F32)
    kv_lat = mm(C_KVLAT, A_KV_RANK)
    kvn = (kv_lat * lax.rsqrt(jnp.mean(kv_lat * kv_lat, axis=-1, keepdims=True) + EPS) * gkv_ref[...]).astype(BF16)
    kn = jnp.dot(kvn, wuk_ref[...], preferred_element_type=F32)
    lane = lax.broadcasted_iota(I32, (1, A_HEADS * LANES), 1)
    ones_lane = (lane % LANES == A_VDIM).astype(F32)
    v_ref[...] = (jnp.dot(kvn, wuv_ref[...], preferred_element_type=F32) + ones_lane).astype(BF16)
    k_pe = mm(C_KR, LANES) * cos + mm(C_KRS, LANES) * sin
    scale = A_QK ** -0.5 * LOG2_E
    for h in range(A_HEADS):
        sl = slice(h * LANES, (h + 1) * LANES)
        q_ref[:, sl] = ((q[:, sl] * cos + qs[:, sl] * sin) * scale).astype(BF16)
        k_ref[:, sl] = (kn[:, sl] + k_pe).astype(BF16)


def _proj_weights(w_in, w_uq, w_ukv):
    o = IN_OFFSETS
    half = A_ROPE // 2
    w_q, w_kv, w_kr = w_in[:, o[0]:o[1]], w_in[:, o[1]:o[2]], w_in[:, o[2]:o[3]]
    w_xm, w_vm, w_op = w_in[:, o[3]:o[4]], w_in[:, o[4]:o[5]], w_in[:, o[5]:o[6]]
    w_i, w_f, w_gm, w_ga = w_in[:, o[6]:o[7]], w_in[:, o[7]:o[8]], w_in[:, o[8]:o[9]], w_in[:, o[9]:o[10]]
    d = w_in.shape[0]
    z = lambda n: jnp.zeros((d, n), w_in.dtype)
    kr = jnp.concatenate([z(A_NOPE), w_kr, z(LANES - A_QK)], axis=1)
    krs = jnp.concatenate([z(A_NOPE), -w_kr[:, half:], w_kr[:, :half], z(LANES - A_QK)], axis=1)
    gate = jnp.concatenate([w_i, w_f, z(LANES - 2 * M_HEADS)], axis=1)
    w_r = jnp.concatenate([w_q, w_kv, kr, krs, gate, w_xm, w_vm, w_op, w_gm, w_ga], axis=1).astype(BF16)

    uq = w_uq.reshape(A_Q_RANK, A_HEADS, A_QK)
    zq = jnp.zeros((A_Q_RANK, A_HEADS, LANES - A_QK), w_uq.dtype)
    zn = jnp.zeros((A_Q_RANK, A_HEADS, A_NOPE), w_uq.dtype)
    uq_pad = jnp.concatenate([uq, zq], axis=-1).reshape(A_Q_RANK, A_HEADS * LANES).astype(BF16)
    uqs_pad = jnp.concatenate([zn, -uq[..., A_NOPE + half:], uq[..., A_NOPE:A_NOPE + half], zq],
                              axis=-1).reshape(A_Q_RANK, A_HEADS * LANES).astype(BF16)
    ukv = w_ukv.reshape(A_KV_RANK, A_HEADS, A_NOPE + A_VDIM)
    zk = jnp.zeros((A_KV_RANK, A_HEADS, LANES - A_NOPE), w_ukv.dtype)
    uk_pad = jnp.concatenate([ukv[..., :A_NOPE], zk], axis=-1).reshape(A_KV_RANK, A_HEADS * LANES).astype(BF16)
    zv = jnp.zeros((A_KV_RANK, A_HEADS, LANES - A_VDIM), w_ukv.dtype)
    uv_pad = jnp.concatenate([ukv[..., A_NOPE:], zv], axis=-1).reshape(A_KV_RANK, A_HEADS * LANES).astype(BF16)
    return w_r, uq_pad, uqs_pad, uk_pad, uv_pad


def _input_projection(x2, pos2, w_in, g_qlat, g_kvlat, w_uq, w_ukv):
    t = x2.shape[0]
    tm = min(PROJ_ROWS, t)
    w_r, uq_pad, uqs_pad, uk_pad, uv_pad = _proj_weights(w_in, w_uq, w_ukv)
    half = A_ROPE // 2
    inv_freq = ROPE_THETA ** (-jnp.arange(half, dtype=F32) / half)
    invf = jnp.concatenate([jnp.zeros((A_NOPE,), F32), inv_freq, inv_freq,
                            jnp.zeros((LANES - A_QK,), F32)]).reshape(1, LANES)
    hw = A_HEADS * LANES
    row = lambda n: pl.BlockSpec((tm, n), lambda i: (i, 0))
    outs = [
        jax.ShapeDtypeStruct((t, hw), BF16), jax.ShapeDtypeStruct((t, hw), BF16), jax.ShapeDtypeStruct((t, hw), BF16),
        jax.ShapeDtypeStruct((t, M_WIDTH), BF16), jax.ShapeDtypeStruct((t, M_WIDTH), BF16),
        jax.ShapeDtypeStruct((t, M_WIDTH), BF16),
        jax.ShapeDtypeStruct((t, D_MODEL), BF16), jax.ShapeDtypeStruct((t, D_MODEL), BF16),
        jax.ShapeDtypeStruct((t, LANES), F32),
    ]
    return pl.pallas_call(
        _proj_kernel,
        out_shape=outs,
        grid=(t // tm,),
        in_specs=[row(D_MODEL), row(1), _const_spec(w_r.shape), _const_spec((1, A_Q_RANK)),
                  _const_spec((1, A_KV_RANK)), _const_spec(uq_pad.shape), _const_spec(uqs_pad.shape),
                  _const_spec(uk_pad.shape), _const_spec(uv_pad.shape), _const_spec((1, LANES))],
        out_specs=[row(hw), row(hw), row(hw), row(M_WIDTH), row(M_WIDTH), row(M_WIDTH), row(D_MODEL), row(D_MODEL),
                   row(LANES)],
        compiler_params=_cparams(("parallel",)),
        name="input_projection",
    )(x2, pos2, w_r, g_qlat.reshape(1, -1), g_kvlat.reshape(1, -1), uq_pad, uqs_pad, uk_pad, uv_pad, invf)


def _log_sigmoid(v):
    return jnp.minimum(v, 0.0) - jnp.log1p(jnp.exp(-jnp.abs(v)))


def _mlstm_kernel(xm_ref, vm_ref, op_ref, gate_ref, convw_ref, convb_ref, wqk_ref, gbias_ref, ghead_ref,
                  skip_ref, tril_ref, out_ref, xpad_s, ctn_s, m_s, *, seq_len):
    s = seq_len
    lc = M_CHUNK_ROWS
    halo = 8
    for bb in range(MLSTM_SEQS):
        xpad_s[bb, 0:halo, :] = jnp.zeros((halo, M_WIDTH), F32)
        xpad_s[bb, halo:, :] = xm_ref[bb * s:(bb + 1) * s, :].astype(F32)
    ctn_s[...] = jnp.zeros_like(ctn_s)
    m_s[...] = jnp.zeros_like(m_s)
    rows = lax.broadcasted_iota(I32, (lc, lc), 0)
    cols = lax.broadcasted_iota(I32, (lc, lc), 1)
    causal = rows >= cols
    row_id = lax.broadcasted_iota(I32, (lc, LANES), 0)
    ones_blk = (lax.broadcasted_iota(I32, (lc, LANES), 1) == 0).astype(BF16)
    kscale = M_HEAD_DIM ** -0.5
    hsl = [slice(h * M_HEAD_DIM, (h + 1) * M_HEAD_DIM) for h in range(M_HEADS)]
    hcl = [slice(h, h + 1) for h in range(M_HEADS)]
    streams = [(bb, h) for bb in range(MLSTM_SEQS) for h in range(M_HEADS)]

    def gates(bb, r0):
        win = xpad_s[bb, pl.ds(r0 - bb * s, lc + halo), :]
        conv = convb_ref[...]
        for j in range(M_CONV):
            lo = halo - (M_CONV - 1) + j
            conv = conv + win[lo:lo + lc, :] * convw_ref[j:j + 1, :]
        xc = conv * jax.nn.sigmoid(conv)
        g = gate_ref[pl.ds(r0, lc), :] + gbias_ref[...]
        ls = _log_sigmoid(g)
        bc = jnp.dot(tril_ref[...], ls, preferred_element_type=F32, precision=lax.Precision.HIGHEST)
        b0 = pltpu.roll(bc, LANES - M_HEADS, axis=1)
        u = g - b0
        cm = u
        shift = 1
        while shift < lc:
            cm = jnp.maximum(cm, jnp.where(row_id >= shift, pltpu.roll(cm, shift, axis=0), -jnp.inf))
            shift *= 2
        m_prev = m_s[bb:bb + 1, :]
        g_inter = b0 + m_prev
        m = jnp.maximum(g_inter, b0 + cm)
        b_tot = b0[lc - 1:lc, :]
        aw = b_tot - b0 + g
        m_chunk = jnp.max(aw, axis=0, keepdims=True)
        m_new = jnp.maximum(b_tot + m_prev, m_chunk)
        m_s[bb:bb + 1, :] = m_new
        return dict(xc=xc, w_inter=jnp.exp(g_inter - m), e_negm=jnp.exp(-m), a_mat=b0 - m, wa=jnp.exp(aw - m_chunk),
                    s_old=jnp.exp(b_tot + m_prev - m_new), s_new=jnp.exp(m_chunk - m_new), u_t=u.T)

    def chunk(c, carry):
        r0s = [pl.multiple_of(bb * s + c * lc, lc) for bb in range(MLSTM_SEQS)]
        gs = [gates(bb, r0s[bb]) for bb in range(MLSTM_SEQS)]
        qk = [jnp.dot(gs[bb]["xc"].astype(BF16), wqk_ref[...], preferred_element_type=F32)
              for bb in range(MLSTM_SEQS)]
        q_b = [qk[bb][:, hsl[h]].astype(BF16) for bb, h in streams]
        k_f = [qk[bb][:, M_WIDTH + h * M_HEAD_DIM:M_WIDTH + (h + 1) * M_HEAD_DIM] * kscale for bb, h in streams]
        v_aug = [jnp.concatenate([vm_ref[pl.ds(r0s[bb], lc), hsl[h]], ones_blk], axis=1) for bb, h in streams]
        ctn_prev = [ctn_s[bb * M_HEADS + h] for bb, h in streams]
        s_raw = [lax.dot_general(q_b[i], k_f[i].astype(BF16), (((1,), (1,)), ((), ())), preferred_element_type=F32)
                 for i in range(len(streams))]
        inter = [jnp.dot(q_b[i], ctn_prev[i].astype(BF16), preferred_element_type=F32)
                 for i in range(len(streams))]
        ctn_c = [lax.dot_general((k_f[i] * gs[bb]["wa"][:, hcl[h]]).astype(BF16), v_aug[i], (((0,), (0,)), ((), ())),
                                 preferred_element_type=F32) for i, (bb, h) in enumerate(streams)]
        sc_b = [(s_raw[i] * jnp.exp(jnp.where(causal, gs[bb]["a_mat"][:, hcl[h]] + gs[bb]["u_t"][hcl[h], :],
                                              -jnp.inf))).astype(BF16) for i, (bb, h) in enumerate(streams)]
        intra = [jnp.dot(sc_b[i], v_aug[i], preferred_element_type=F32) for i in range(len(streams))]
        for i, (bb, h) in enumerate(streams):
            g = gs[bb]
            ctn_s[bb * M_HEADS + h] = g["s_old"][:, hcl[h]] * ctn_prev[i] + g["s_new"][:, hcl[h]] * ctn_c[i]
            wi = g["w_inter"][:, hcl[h]]
            num = wi * inter[i][:, :M_HEAD_DIM] + intra[i][:, :M_HEAD_DIM]
            den = wi * inter[i][:, M_HEAD_DIM:M_HEAD_DIM + 1] + intra[i][:, M_HEAD_DIM:M_HEAD_DIM + 1]
            hh = num / jnp.maximum(jnp.abs(den), g["e_negm"][:, hcl[h]])
            mu = jnp.mean(hh, axis=-1, keepdims=True)
            dv = hh - mu
            var = jnp.mean(dv * dv, axis=-1, keepdims=True)
            hn = dv * lax.rsqrt(var + EPS) * ghead_ref[:, hsl[h]]
            y = (jax.nn.sigmoid(op_ref[pl.ds(r0s[bb], lc), hsl[h]].astype(F32))
                 * (hn + skip_ref[:, hsl[h]] * g["xc"][:, hsl[h]]))
            out_ref[pl.ds(r0s[bb], lc), hsl[h]] = y.astype(BF16)
        return carry

    lax.fori_loop(0, s // lc, chunk, 0)


def _mlstm_branch(xm, vm, op, gate, bsz, s, b_i, b_f, conv_w, conv_b, w_mq, w_mk, g_mhead, w_mskip):
    lc = M_CHUNK_ROWS
    gbias = jnp.concatenate([b_i, b_f, jnp.zeros((LANES - 2 * M_HEADS,), F32)]).reshape(1, LANES)
    tril = jnp.tril(jnp.ones((lc, lc), F32))
    eye = jnp.eye(M_HEADS, dtype=F32)
    block_diag = lambda w: jnp.einsum('hde,hg->hdge', w, eye).reshape(M_WIDTH, M_WIDTH)
    w_qk = jnp.concatenate([block_diag(w_mq), block_diag(w_mk)], axis=1).astype(BF16)
    nseq = MLSTM_SEQS
    assert bsz % nseq == 0
    seq = lambda n: pl.BlockSpec((nseq * s, n), lambda b: (b, 0))
    return pl.pallas_call(
        functools.partial(_mlstm_kernel, seq_len=s),
        out_shape=jax.ShapeDtypeStruct((bsz * s, M_WIDTH), BF16),
        grid=(bsz // nseq,),
        in_specs=[seq(M_WIDTH), seq(M_WIDTH), seq(M_WIDTH), seq(LANES),
                  _const_spec((M_CONV, M_WIDTH)), _const_spec((1, M_WIDTH)),
                  _const_spec((M_WIDTH, 2 * M_WIDTH)),
                  _const_spec((1, LANES)), _const_spec((1, M_WIDTH)), _const_spec((1, M_WIDTH)),
                  _const_spec((lc, lc))],
        out_specs=seq(M_WIDTH),
        scratch_shapes=[pltpu.VMEM((nseq, s + 8, M_WIDTH), F32),
                        pltpu.VMEM((nseq * M_HEADS, M_HEAD_DIM, 2 * LANES), F32), pltpu.VMEM((8, LANES), F32)],
        compiler_params=_cparams(("parallel",)),
        name="mlstm_branch",
    )(xm, vm, op, gate, conv_w, conv_b.reshape(1, -1), w_qk, gbias, g_mhead.reshape(1, -1), w_mskip.reshape(1, -1),
      tril)


def _attn_kernel(q_ref, k_ref, v_ref, o_ref, s_scr, m_s, acc_s):
    tq = q_ref.shape[0]
    i = pl.program_id(1)
    rows = lax.broadcasted_iota(I32, (tq, tq), 0)
    cols = lax.broadcasted_iota(I32, (tq, tq), 1)

    def lane_tile_max(sc):
        out = sc[:, :LANES]
        for t in range(1, tq // LANES):
            out = jnp.maximum(out, sc[:, t * LANES:(t + 1) * LANES])
        return out

    heads = [slice(h * LANES, (h + 1) * LANES) for h in range(A_HEADS)]
    m_s[...] = jnp.full(m_s.shape, NEG_BIG, F32)
    acc_s[...] = jnp.zeros_like(acc_s)

    def pass_a(j, diagonal):
        k0 = pl.multiple_of(j * tq, tq)
        for h, hs in enumerate(heads):
            sc = lax.dot_general(q_ref[:, hs], k_ref[pl.ds(k0, tq), hs], (((1,), (1,)), ((), ())),
                                 preferred_element_type=F32)
            if diagonal:
                sc = jnp.where(cols <= rows, sc, NEG_BIG)
            s_scr[h, j] = sc
            m_s[h] = jnp.maximum(m_s[h], lane_tile_max(sc))

    def for_blocks(n, fn):
        def pair(p, c):
            fn(2 * p)
            fn(2 * p + 1)
            return c

        lax.fori_loop(0, n // 2, pair, 0)

        @pl.when(n % 2 == 1)
        def _():
            fn(n - 1)

    for_blocks(i, lambda j: pass_a(j, False))
    pass_a(i, True)
    for h in range(A_HEADS):
        m_s[h] = jnp.broadcast_to(jnp.max(m_s[h], axis=1, keepdims=True), (tq, LANES))

    def pass_b(j):
        k0 = pl.multiple_of(j * tq, tq)
        for h, hs in enumerate(heads):
            m_row = m_s[h]
            p = jnp.exp2(s_scr[h, j] - jnp.concatenate([m_row] * (tq // LANES), axis=1)).astype(BF16)
            acc_s[h] += jnp.dot(p, v_ref[pl.ds(k0, tq), hs], preferred_element_type=F32)

    for_blocks(i + 1, pass_b)
    for h, hs in enumerate(heads):
        acc = acc_s[h]
        o_ref[:, hs] = (acc / acc[:, A_VDIM:A_VDIM + 1]).astype(BF16)


def _mla_attention(q, k, v, bsz, s):
    tq = min(ATT_Q, s)
    nq = s // tq
    hw = A_HEADS * LANES
    return pl.pallas_call(
        _attn_kernel,
        out_shape=jax.ShapeDtypeStruct(q.shape, BF16),
        grid=(bsz, nq),
        in_specs=[pl.BlockSpec((tq, hw), lambda b, i: (b * nq + i, 0)),
                  pl.BlockSpec((s, hw), lambda b, i: (b, 0)),
                  pl.BlockSpec((s, hw), lambda b, i: (b, 0))],
        out_specs=pl.BlockSpec((tq, hw), lambda b, i: (b * nq + i, 0)),
        scratch_shapes=[pltpu.VMEM((A_HEADS, nq, tq, tq), F32), pltpu.VMEM((A_HEADS, tq, LANES), F32),
                        pltpu.VMEM((A_HEADS, tq, LANES), F32)],
        compiler_params=_cparams(("parallel", "arbitrary")),
        name="mla_attention",
    )(q, k, v)


def _merge_kernel(ym_ref, oa_ref, gm_ref, ga_ref, x_ref, wbm_ref, wba_ref, wmix_ref, g_ref, b_ref, h_ref):
    y_m = jnp.dot(ym_ref[...], wbm_ref[...], preferred_element_type=F32)
    y_a = jnp.dot(oa_ref[...], wba_ref[...], preferred_element_type=F32)
    mixed = jax.nn.sigmoid(gm_ref[...].astype(F32)) * y_m + jax.nn.sigmoid(ga_ref[...].astype(F32)) * y_a
    mix = jnp.dot(mixed.astype(BF16), wmix_ref[...], preferred_element_type=F32)
    h_ref[...] = _layer_norm_rows(DN_ALPHA * x_ref[...] + mix, g_ref[...], b_ref[...])


def _merge(ym, oa, gm, ga, x2, w_br_m, w_br_a, w_mix_out, ln_g, ln_b):
    t = x2.shape[0]
    tm = min(MERGE_ROWS, t)
    wba = jnp.concatenate([w_br_a.reshape(A_HEADS, A_VDIM, D_MODEL),
                           jnp.zeros((A_HEADS, LANES - A_VDIM, D_MODEL), w_br_a.dtype)], axis=1)
    wba = wba.reshape(A_HEADS * LANES, D_MODEL).astype(BF16)
    row = lambda n: pl.BlockSpec((tm, n), lambda i: (i, 0))
    return pl.pallas_call(
        _merge_kernel,
        out_shape=jax.ShapeDtypeStruct((t, D_MODEL), F32),
        grid=(t // tm,),
        in_specs=[row(M_WIDTH), row(A_HEADS * LANES), row(D_MODEL), row(D_MODEL), row(D_MODEL),
                  _const_spec((M_WIDTH, D_MODEL)), _const_spec((A_HEADS * LANES, D_MODEL)),
                  _const_spec((D_MODEL, D_MODEL)), _const_spec((1, D_MODEL)), _const_spec((1, D_MODEL))],
        out_specs=row(D_MODEL),
        compiler_params=_cparams(("parallel",)),
        name="merge_deepnorm1",
    )(ym, oa, gm, ga, x2, w_br_m.astype(BF16), wba, w_mix_out.astype(BF16), ln_g.reshape(1, -1), ln_b.reshape(1, -1))


def _kv_kernel(m_ref, w_ref, o_ref):
    o_ref[...] = jnp.dot(m_ref[...].astype(BF16), w_ref[...], preferred_element_type=F32).astype(BF16)


def _mem_kv(mem2, w_ckv):
    r = mem2.shape[0]
    tm = min(512, r)
    return pl.pallas_call(
        _kv_kernel,
        out_shape=jax.ShapeDtypeStruct((r, 2 * D_MODEL), BF16),
        grid=(r // tm,),
        in_specs=[pl.BlockSpec((tm, D_MODEL), lambda i: (i, 0)), _const_spec((D_MODEL, 2 * D_MODEL))],
        out_specs=pl.BlockSpec((tm, 2 * D_MODEL), lambda i: (i, 0)),
        compiler_params=_cparams(("parallel",)),
        name="memory_kv",
    )(mem2, w_ckv.astype(BF16))


def _xattn_kernel(h_ref, kv_ref, wq_ref, wo_ref, g_ref, b_ref, wr_ref, br_ref, *rest):
    h2_ref, idx_ref, gate_ref = rest[-3:]
    h1 = h_ref[...]
    q = jnp.dot(h1.astype(BF16), wq_ref[...], preferred_element_type=F32).astype(BF16)
    scale = X_HEAD_DIM ** -0.5
    ksl = [slice(hd * X_HEAD_DIM, (hd + 1) * X_HEAD_DIM) for hd in range(X_HEADS)]
    vsl = [slice(D_MODEL + hd * X_HEAD_DIM, D_MODEL + (hd + 1) * X_HEAD_DIM) for hd in range(X_HEADS)]
    scs = [lax.dot_general(q[:, ks], kv_ref[:, ks], (((1,), (1,)), ((), ())), preferred_element_type=F32) * scale
           for ks in ksl]
    ps = [jnp.exp(sc - jnp.max(sc, axis=1, keepdims=True)) for sc in scs]
    ps = [(p / jnp.sum(p, axis=1, keepdims=True)).astype(BF16) for p in ps]
    o = jnp.concatenate([jnp.dot(p, kv_ref[:, vs], preferred_element_type=F32).astype(BF16)
                         for p, vs in zip(ps, vsl)], axis=1)
    att = jnp.dot(o, wo_ref[...], preferred_element_type=F32)
    h2 = _layer_norm_rows(DN_ALPHA * h1 + att, g_ref[...], b_ref[...])
    for c in range(ROW_TILES):
        h2_ref[pl.ds(c, h2.shape[0], stride=ROW_TILES), :] = h2[:, c * LANES:(c + 1) * LANES]

    h2_hi = h2.astype(BF16)
    h2_lo = (h2 - h2_hi.astype(F32)).astype(BF16)
    hi_prod = jnp.dot(h2_hi, wr_ref[...], preferred_element_type=F32)
    lo_prod = jnp.dot(h2_lo, wr_ref[:, :LANES], preferred_element_type=F32)
    logits = hi_prod[:, :LANES] + (hi_prod[:, LANES:] + lo_prod) + br_ref[...]
    tm = logits.shape[0]
    lane = lax.broadcasted_iota(I32, (tm, LANES), 1)
    lane_f = lane.astype(F32)
    work = jnp.where(lane < N_EXPERTS, logits, -jnp.inf)
    vals, idxs = [], []
    for _ in range(TOP_K):
        mx = jnp.max(work, axis=1, keepdims=True)
        ix = jnp.min(jnp.where(work == mx, lane_f, float(LANES)), axis=1, keepdims=True)
        vals.append(mx)
        idxs.append(ix)
        work = jnp.where(lane_f == ix, -jnp.inf, work)
    es = [jnp.exp(vv - vals[0]) for vv in vals]
    tot = es[0] + es[1] + es[2] + es[3]
    idx_slab = jnp.zeros((tm, LANES), F32)
    gate_slab = jnp.zeros((tm, LANES), F32)
    for kk in range(TOP_K):
        idx_slab = jnp.where(lane == kk, idxs[kk], idx_slab)
        gate_slab = jnp.where(lane == kk, es[kk] / tot, gate_slab)
    idx_ref[...] = idx_slab.astype(I32)
    gate_ref[...] = gate_slab


def _cross_attention_router(h1, kv, bsz, s, w_cq, w_co, ln_g, ln_b, w_router, b_router, part, after):
    n_mem = kv.shape[0] // bsz
    bsz = bsz // MOE_PARTS
    b0 = part * bsz
    t = bsz * s
    tm = min(XATT_ROWS, s)
    ns = s // tm
    wr = jnp.concatenate([w_router, jnp.zeros((D_MODEL, LANES - N_EXPERTS), F32)], axis=1)
    wr_hi = wr.astype(BF16)
    wr = jnp.concatenate([wr_hi, (wr - wr_hi.astype(F32)).astype(BF16)], axis=1)
    br = jnp.concatenate([b_router, jnp.zeros((LANES - N_EXPERTS,), F32)]).reshape(1, LANES)
    row = lambda n: pl.BlockSpec((tm, n), lambda b, i: (b * ns + i, 0))
    h1_rows = pl.BlockSpec((tm, D_MODEL), lambda b, i: ((b0 + b) * ns + i, 0))
    in_specs = [h1_rows, pl.BlockSpec((n_mem, 2 * D_MODEL), lambda b, i: (b0 + b, 0)),
                _const_spec((D_MODEL, D_MODEL)), _const_spec((D_MODEL, D_MODEL)),
                _const_spec((1, D_MODEL)), _const_spec((1, D_MODEL)),
                _const_spec((D_MODEL, 2 * LANES)), _const_spec((1, LANES))]
    args = [h1, kv, w_cq.astype(BF16), w_co.astype(BF16), ln_g.reshape(1, -1), ln_b.reshape(1, -1), wr, br]
    if after is not None:
        in_specs.append(pl.BlockSpec(memory_space=pl.ANY))
        args.append(after)
    return pl.pallas_call(
        _xattn_kernel,
        out_shape=[jax.ShapeDtypeStruct((t * ROW_TILES, LANES), F32), jax.ShapeDtypeStruct((t, LANES), I32),
                   jax.ShapeDtypeStruct((t, LANES), F32)],
        grid=(bsz, ns),
        in_specs=in_specs,
        out_specs=[pl.BlockSpec((tm * ROW_TILES, LANES), lambda b, i: (b * ns + i, 0)), row(LANES), row(LANES)],
        compiler_params=_cparams(("parallel", "parallel")),
        name="cross_attention_router",
    )(*args)


def _rank_kernel(idx_ref, ltri_ref, rank_ref, cnt_ref, carry_s):
    @pl.when(pl.program_id(0) == 0)
    def _():
        carry_s[...] = jnp.zeros_like(carry_s)

    idx = idx_ref[...]
    tm = idx.shape[0]
    lane = lax.broadcasted_iota(I32, (tm, LANES), 1)
    onehots = [(lane == idx[:, kk:kk + 1]).astype(F32) for kk in range(TOP_K)]
    sel = onehots[0] + onehots[1] + onehots[2] + onehots[3]
    before = jnp.dot(ltri_ref[...], sel.astype(BF16), preferred_element_type=F32) + carry_s[0:1, :]
    rank_slab = jnp.zeros((tm, LANES), F32)
    for kk in range(TOP_K):
        r = jnp.sum(onehots[kk] * before, axis=1, keepdims=True)
        rank_slab = jnp.where(lane == kk, r, rank_slab)
    rank_ref[...] = rank_slab.astype(I32)
    carry_s[0:1, :] = carry_s[0:1, :] + jnp.sum(sel, axis=0, keepdims=True)
    cnt_ref[...] = jnp.broadcast_to(carry_s[0:1, :], cnt_ref.shape).astype(I32)


def _routing_ranks(idx_slab):
    t = idx_slab.shape[0]
    tm = min(RANK_ROWS, t)
    ltri = jnp.tril(jnp.ones((tm, tm), BF16), k=-1)
    return pl.pallas_call(
        _rank_kernel,
        out_shape=[jax.ShapeDtypeStruct((t, LANES), I32), jax.ShapeDtypeStruct((8, LANES), I32)],
        grid=(t // tm,),
        in_specs=[pl.BlockSpec((tm, LANES), lambda i: (i, 0)), _const_spec((tm, tm))],
        out_specs=[pl.BlockSpec((tm, LANES), lambda i: (i, 0)), _const_spec((8, LANES))],
        scratch_shapes=[pltpu.VMEM((8, LANES), F32)],
        compiler_params=_cparams(("arbitrary",)),
        name="routing_ranks",
    )(idx_slab, ltri)


def _dest_kernel(idx_ref, rank_ref, pstart_ref, dest_ref):
    idx = idx_ref[...]
    tm = idx.shape[0]
    lane = lax.broadcasted_iota(I32, (tm, LANES), 1)
    dest = rank_ref[...].astype(F32)
    pstart = pstart_ref[0:1, :].astype(F32)
    for kk in range(TOP_K):
        start = jnp.sum(jnp.where(lane == idx[:, kk:kk + 1], pstart, 0.0), axis=1, keepdims=True)
        dest = dest + jnp.where(lane == kk, start, 0.0)
    dest_ref[...] = dest.T[0:8, :].astype(I32)


def _dest_rows(idx_slab, rank_slab, pstart_row):
    t = rank_slab.shape[0]
    tm = min(RANK_ROWS, t)
    return pl.pallas_call(
        _dest_kernel,
        out_shape=jax.ShapeDtypeStruct((8, t), I32),
        grid=(t // tm,),
        in_specs=[pl.BlockSpec((tm, LANES), lambda i: (i, 0)), pl.BlockSpec((tm, LANES), lambda i: (i, 0)),
                  _const_spec((8, LANES))],
        out_specs=pl.BlockSpec((8, tm), lambda i: (0, i)),
        compiler_params=_cparams(("parallel",)),
        name="routing_dest",
    )(idx_slab, rank_slab, pstart_row)


def _sc_mesh():
    return plsc.VectorSubcoreMesh(core_axis_name="c", subcore_axis_name="s", num_cores=SC_CORES,
                                  num_subcores=SC_SUBCORES)


def _sc_worker():
    return lax.axis_index("s") * SC_CORES + lax.axis_index("c")


def _dispatch(h2, dest_kt, pad_rows, n_rows):
    t = dest_kt.shape[1]
    w = SC_ROWS
    per_w = t // SC_WORKERS
    steps = per_w // w
    assert steps % 2 == 0 and steps * w * SC_WORKERS == t
    pad_steps = pad_rows.shape[0] // (SC_WORKERS * w)
    dest_w = dest_kt.reshape(TOP_K, SC_WORKERS, steps, w)
    pad_w = pad_rows.reshape(SC_WORKERS, pad_steps, w)
    zeros = jnp.zeros((w, ROW_TILES, LANES), F32)

    def body(x_hbm, dest_hbm, pad_hbm, zeros_hbm, o_hbm, idx_v, pad_v, rows_v, sem_in, sem_out):
        wid = _sc_worker()
        tok0 = wid * per_w
        for kk in range(TOP_K):
            pltpu.sync_copy(dest_hbm.at[kk, wid], idx_v.at[pl.ds(kk * steps, steps)])
        pltpu.sync_copy(pad_hbm.at[wid], pad_v)
        pltpu.sync_copy(zeros_hbm, rows_v.at[0])
        for j in range(pad_steps):
            pltpu.sync_copy(rows_v.at[0], o_hbm.at[pad_v.at[j]])

        def load(step, b):
            return pltpu.make_async_copy(x_hbm.at[pl.ds(tok0 + step * w, w)], rows_v.at[b], sem_in.at[b])

        def scatter(step, kk, b):
            return pltpu.make_async_copy(rows_v.at[b], o_hbm.at[idx_v.at[kk * steps + step]], sem_out.at[b])

        load(0, 0).start()

        @pl.loop(0, steps, step=2)
        def _(s0):
            for b in range(2):
                step = s0 + b
                load(step, b).wait()

                @pl.when(step >= 1)
                def _():
                    for kk in range(TOP_K):
                        scatter(step - 1, kk, 1 - b).wait()

                @pl.when(step + 1 < steps)
                def _():
                    load(step + 1, 1 - b).start()

                for kk in range(TOP_K):
                    scatter(step, kk, b).start()

        for kk in range(TOP_K):
            scatter(steps - 1, kk, 1).wait()

    return pl.kernel(
        body,
        out_type=jax.ShapeDtypeStruct((n_rows, ROW_TILES, LANES), F32),
        mesh=_sc_mesh(),
        scratch_types=[pltpu.VMEM((steps * TOP_K, w), I32), pltpu.VMEM((pad_steps, w), I32),
                       pltpu.VMEM((2, w, ROW_TILES, LANES), F32), pltpu.SemaphoreType.DMA((2,)),
                       pltpu.SemaphoreType.DMA((2,))],
        name="moe_dispatch_sc",
    )(h2, dest_w, pad_w, zeros)


def _gather_rows(yb, dest_kt):
    n = dest_kt.shape[0]
    w = SC_ROWS
    per_w = n // SC_WORKERS
    steps = per_w // w
    assert steps % 2 == 0 and steps * w * SC_WORKERS == n
    dest_w = dest_kt.reshape(SC_WORKERS, steps, w)

    def body(y_hbm, dest_hbm, o_hbm, idx_v, rows_v, sem_g, sem_w):
        wid = _sc_worker()
        row0 = wid * per_w
        pltpu.sync_copy(dest_hbm.at[wid], idx_v)

        def gather(step, b):
            return pltpu.make_async_copy(y_hbm.at[idx_v.at[step]], rows_v.at[b], sem_g.at[b])

        def write(step, b):
            return pltpu.make_async_copy(rows_v.at[b], o_hbm.at[pl.ds(row0 + step * w, w)], sem_w.at[b])

        gather(0, 0).start()

        @pl.loop(0, steps, step=2)
        def _(s0):
            for b in range(2):
                step = s0 + b
                gather(step, b).wait()

                @pl.when(step >= 1)
                def _():
                    write(step - 1, 1 - b).wait()

                @pl.when(step + 1 < steps)
                def _():
                    gather(step + 1, 1 - b).start()

                write(step, b).start()

        write(steps - 1, 1).wait()

    return pl.kernel(
        body,
        out_type=jax.ShapeDtypeStruct((n, ROW_TILES, LANES), F32),
        mesh=_sc_mesh(),
        scratch_types=[pltpu.VMEM((steps, w), I32), pltpu.VMEM((2, w, ROW_TILES, LANES), F32),
                       pltpu.SemaphoreType.DMA((2,)), pltpu.SemaphoreType.DMA((2,))],
        name="moe_gather_sc",
    )(yb, dest_w)


def _rows_2d(ref, n):
    return jnp.concatenate([ref[pl.ds(c, n, stride=ROW_TILES), :] for c in range(ROW_TILES)], axis=1)


def _store_rows(ref, val):
    for c in range(ROW_TILES):
        ref[pl.ds(c, val.shape[0], stride=ROW_TILES), :] = val[:, c * LANES:(c + 1) * LANES]


def _expert_kernel(blk_e_ref, first_ref, slot_ref, next_ref, nblk_ref, x_ref, wg_hbm, wl_hbm, bg_ref, bl_ref, wd_hbm,
                   bd_ref, y_ref, wg_buf, wl_buf, wd_stage, wd_buf, sem):
    i = pl.program_id(0)

    def fetch(e, sl):
        return [pltpu.make_async_copy(src.at[e], dst.at[sl], sem.at[sl, j])
                for j, (src, dst) in enumerate(((wg_hbm, wg_buf), (wl_hbm, wl_buf), (wd_hbm, wd_stage)))]

    @pl.when(i < nblk_ref[0])
    def _():
        sl = slot_ref[i]

        @pl.when(first_ref[i] == 1)
        def _():
            @pl.when(i == 0)
            def _():
                for cp in fetch(blk_e_ref[0], sl):
                    cp.start()

            for cp in fetch(blk_e_ref[i], sl):
                cp.wait()
            wd_buf[sl] = wd_stage[sl].astype(BF16)

            @pl.when(next_ref[i] >= 0)
            def _():
                for cp in fetch(next_ref[i], 1 - sl):
                    cp.start()

        xb = _rows_2d(x_ref, y_ref.shape[0] // ROW_TILES).astype(BF16)
        glu = jnp.dot(xb, wg_buf[sl], preferred_element_type=F32) + bg_ref[...]
        lin = jnp.dot(xb, wl_buf[sl], preferred_element_type=F32) + bl_ref[...]
        glu = jnp.minimum(glu, SWIGLU_LIMIT)
        lin = jnp.clip(lin, -SWIGLU_LIMIT, SWIGLU_LIMIT)
        act = glu * jax.nn.sigmoid(SWIGLU_ALPHA * glu) * (lin + 1.0)
        _store_rows(y_ref, jnp.dot(act.astype(BF16), wd_buf[sl], preferred_element_type=F32) + bd_ref[...])

    @pl.when(i >= nblk_ref[0])
    def _():
        y_ref[...] = jnp.zeros_like(y_ref)


def _split_kernel(w_ref, perm_ref, after_ref, g_ref, l_ref):
    del after_ref
    half = g_ref.shape[-1]
    sorted_cols = jnp.dot(w_ref[...].astype(BF16), perm_ref[...], preferred_element_type=F32)
    g_ref[...] = sorted_cols[:, :half].astype(BF16)
    l_ref[...] = sorted_cols[:, half:].astype(BF16)


def _split_gate_up(w_gu, after):
    e, d, n2 = w_gu.shape
    cw = 512
    src = np.concatenate([np.arange(0, cw, 2), np.arange(1, cw, 2)])
    perm = jnp.asarray(np.eye(cw, dtype=np.float32)[:, src], BF16)
    return pl.pallas_call(
        _split_kernel,
        out_shape=[jax.ShapeDtypeStruct((e, d, n2 // 2), BF16), jax.ShapeDtypeStruct((e, d, n2 // 2), BF16)],
        grid=(e, n2 // cw),
        in_specs=[pl.BlockSpec((None, d, cw), lambda i, c: (i, 0, c)), _const_spec((cw, cw)),
                  pl.BlockSpec(memory_space=pl.ANY)],
        out_specs=[pl.BlockSpec((None, d, cw // 2), lambda i, c: (i, 0, c)),
                   pl.BlockSpec((None, d, cw // 2), lambda i, c: (i, 0, c))],
        compiler_params=_cparams(("parallel", "parallel")),
        name="split_gate_up",
    )(w_gu, perm, after)


def _experts(xb, blk_meta, w_glu, w_lin, b_glu, b_lin, w_dn, b_dn):
    n_rows = xb.shape[0] // ROW_TILES
    bm = EXPERT_ROWS
    vec = lambda n: pl.BlockSpec((None, 1, n), lambda i, be, *_: (be[i], 0, 0))
    rows = pl.BlockSpec((bm * ROW_TILES, LANES), lambda i, *_: (i, 0))
    hbm = pl.BlockSpec(memory_space=pl.ANY)
    grid_spec = pltpu.PrefetchScalarGridSpec(
        num_scalar_prefetch=5,
        grid=(n_rows // bm,),
        in_specs=[rows, hbm, hbm, vec(D_EXPERT), vec(D_EXPERT), hbm, vec(D_MODEL)],
        out_specs=rows,
        scratch_shapes=[pltpu.VMEM((2, D_MODEL, D_EXPERT), BF16), pltpu.VMEM((2, D_MODEL, D_EXPERT), BF16),
                        pltpu.VMEM((2, D_EXPERT, D_MODEL), F32), pltpu.VMEM((2, D_EXPERT, D_MODEL), BF16),
                        pltpu.SemaphoreType.DMA((2, 3))],
    )
    return pl.pallas_call(
        _expert_kernel,
        out_shape=jax.ShapeDtypeStruct((n_rows * ROW_TILES, LANES), F32),
        grid_spec=grid_spec,
        compiler_params=_cparams(("arbitrary",)),
        name="moe_experts",
    )(*blk_meta, xb, w_glu, w_lin, b_glu, b_lin, w_dn, b_dn)


def _combine_kernel(h_ref, gate_ref, y0_ref, y1_ref, y2_ref, y3_ref, g_ref, b_ref, *rest):
    o_ref = rest[-1]
    tt = o_ref.shape[0]
    gates = gate_ref[...]
    acc = DN_ALPHA * _rows_2d(h_ref, tt)
    for kk, y_ref in enumerate((y0_ref, y1_ref, y2_ref, y3_ref)):
        acc = acc + gates[:, kk:kk + 1] * _rows_2d(y_ref, tt)
    o_ref[...] = _layer_norm_rows(acc, g_ref[...], b_ref[...])


def _combine(h2, gate_slab, yg, ln_g, ln_b, part, prev_out):
    t = h2.shape[0] // ROW_TILES
    t_all = t * MOE_PARTS
    tt = min(COMBINE_TOKENS, t)
    steps = t // tt
    first = part * steps
    row = lambda m: pl.BlockSpec((tt, m), lambda i: (i, 0))
    tiles = lambda kk: pl.BlockSpec((tt * ROW_TILES, LANES), lambda i: (kk * steps + i, 0))
    in_specs = [tiles(0), row(LANES),
                tiles(0), tiles(1), tiles(2), tiles(3), _const_spec((1, D_MODEL)), _const_spec((1, D_MODEL))]
    args = [h2, gate_slab, yg, yg, yg, yg, ln_g.reshape(1, -1), ln_b.reshape(1, -1)]
    aliases = {}
    if prev_out is not None:
        in_specs.append(pl.BlockSpec(memory_space=pl.ANY))
        args.append(prev_out)
        aliases = {len(args) - 1: 0}
    return pl.pallas_call(
        _combine_kernel,
        out_shape=jax.ShapeDtypeStruct((t_all, D_MODEL), F32),
        grid=(steps,),
        in_specs=in_specs,
        out_specs=pl.BlockSpec((tt, D_MODEL), lambda i: (first + i, 0)),
        input_output_aliases=aliases,
        compiler_params=_cparams(("parallel",)),
        name="moe_combine",
    )(*args)


def _route_and_dispatch(h2, idx_slab):
    t = h2.shape[0] // ROW_TILES
    bm = EXPERT_ROWS
    n_rows = t * TOP_K + N_EXPERTS * bm
    n_blocks = n_rows // bm

    rank_slab, cnt = _routing_ranks(idx_slab)
    counts = cnt[0, :N_EXPERTS]
    padded = (counts + bm - 1) // bm * bm
    pends = jnp.cumsum(padded)
    pstarts = pends - padded
    pstart_row = jnp.zeros((8, LANES), I32).at[:, :N_EXPERTS].set(pstarts[None, :])
    blk_start = jnp.arange(n_blocks, dtype=I32) * bm
    blk_e = jnp.minimum(jnp.sum((pends[None, :] <= blk_start[:, None]).astype(I32), axis=1), N_EXPERTS - 1)
    nblk = (pends[-1] // bm).astype(I32).reshape(1)
    active = counts > 0
    ordinal = jnp.cumsum(active.astype(I32)) - 1
    eid = jnp.arange(N_EXPERTS, dtype=I32)
    later = active[None, :] & (eid[None, :] > eid[:, None])
    next_active = jnp.where(later.any(axis=1), jnp.argmax(later, axis=1).astype(I32), -1)
    blk_first = jnp.concatenate([jnp.ones((1,), I32), (blk_e[1:] != blk_e[:-1]).astype(I32)])
    blk_onehot = (blk_e[:, None] == eid[None, :]).astype(I32)
    blk_slot = jnp.sum(blk_onehot * ordinal[None, :], axis=1) % 2
    blk_next = jnp.sum(blk_onehot * next_active[None, :], axis=1)
    blk_meta = (blk_e, blk_first, blk_slot, blk_next, nblk)
    dest_kt = _dest_rows(idx_slab, rank_slab, pstart_row)[:TOP_K]
    pad = padded - counts
    spare_base = pends[-1] + jnp.cumsum(bm - pad) - (bm - pad)
    pad_off = jnp.arange(bm, dtype=I32)[None, :]
    pad_rows = jnp.where(pad_off < pad[:, None], (pstarts + counts)[:, None] + pad_off,
                         spare_base[:, None] + pad_off - pad[:, None]).reshape(-1)
    xb = _dispatch(h2.reshape(t, ROW_TILES, LANES), dest_kt, pad_rows, n_rows)
    return xb.reshape(n_rows * ROW_TILES, LANES), blk_meta, dest_kt


def _xattn_moe(h1, kv, bsz, s, w_cq, w_co, ln2_g, ln2_b, w_router, b_router, w_gu, b_gu, w_dn, b_dn, ln3_g, ln3_b):
    parts = []
    after = None
    for part in range(MOE_PARTS):
        h2, idx_slab, gate_slab = _cross_attention_router(h1, kv, bsz, s, w_cq, w_co, ln2_g, ln2_b, w_router,
                                                          b_router, part, after)
        parts.append((h2, gate_slab) + _route_and_dispatch(h2, idx_slab))
        after = parts[-1][-1]
    w_glu, w_lin = _split_gate_up(w_gu, after)
    expert_w = (w_glu, w_lin, b_gu[:, None, 0::2], b_gu[:, None, 1::2], w_dn, b_dn[:, None, :])
    out = None
    for part, (h2, gate_slab, xb, blk_meta, dest_kt) in enumerate(parts):
        n_rows = xb.shape[0] // ROW_TILES
        yb = _experts(xb, blk_meta, *expert_w)
        yg = _gather_rows(yb.reshape(n_rows, ROW_TILES, LANES), dest_kt.reshape(-1))
        out = _combine(h2, gate_slab, yg.reshape(-1, LANES), ln3_g, ln3_b, part, out)
    return out


def kernel(x, mem, positions, w_in, b_igate, b_fgate, conv_w, conv_b, w_mq, w_mk, g_mhead, w_mskip, g_qlat, g_kvlat,
           w_uq, w_ukv, w_br_m, w_br_a, w_mix_out, ln1_g, ln1_b, w_cq, w_ckv, w_co, ln2_g, ln2_b, w_router, b_router,
           w_gu, b_gu, w_dn, b_dn, ln3_g, ln3_b):
    bsz, s, d = x.shape
    t = bsz * s
    h = x.reshape(t, d)
    pos2 = positions.reshape(t, 1)
    for l in range(DEPTH):
        q, k, v, xm, vm, op, gm, ga, gate = _input_projection(h, pos2, w_in[l], g_qlat[l], g_kvlat[l], w_uq[l],
                                                              w_ukv[l])
        ym = _mlstm_branch(xm, vm, op, gate, bsz, s, b_igate[l], b_fgate[l], conv_w[l], conv_b[l], w_mq[l], w_mk[l],
                           g_mhead[l], w_mskip[l])
        oa = _mla_attention(q, k, v, bsz, s)
        h1 = _merge(ym, oa, gm, ga, h, w_br_m[l], w_br_a[l], w_mix_out[l], ln1_g[l], ln1_b[l])
        kv = _mem_kv(mem.reshape(-1, d), w_ckv[l])
        h = _xattn_moe(h1, kv, bsz, s, w_cq[l], w_co[l], ln2_g[l], ln2_b[l], w_router[l], b_router[l], w_gu[l],
                       b_gu[l], w_dn[l], b_dn[l], ln3_g[l], ln3_b[l])
    return h.reshape(bsz, s, d)
```

```python
import functools

import numpy as np
import jax
import jax.numpy as jnp
from jax import lax
from jax.experimental import pallas as pl
from jax.experimental.pallas import tpu as pltpu
from jax.experimental.pallas import tpu_sc as plsc

F32 = jnp.float32
BF16 = jnp.bfloat16
I32 = jnp.int32

D_MODEL = 1024
N_MEM = 256
M_HEADS = 4
M_HEAD_DIM = 128
M_WIDTH = M_HEADS * M_HEAD_DIM
M_CONV = 4
A_HEADS = 8
A_NOPE = 64
A_ROPE = 32
A_QK = A_NOPE + A_ROPE
A_VDIM = 64
A_Q_RANK = 256
A_KV_RANK = 128
ROPE_THETA = 10000.0
X_HEADS = 4
X_HEAD_DIM = D_MODEL // X_HEADS
N_EXPERTS = 32
TOP_K = 4
D_EXPERT = D_MODEL
SWIGLU_ALPHA = 1.702
SWIGLU_LIMIT = 7.0
DEPTH = 1
DN_ALPHA = (2.0 * DEPTH) ** 0.25
EPS = 1e-5
IN_SPLITS = (A_Q_RANK, A_KV_RANK, A_ROPE, M_WIDTH, M_WIDTH, M_WIDTH, M_HEADS, M_HEADS, D_MODEL, D_MODEL)
IN_OFFSETS = tuple(int(v) for v in np.cumsum((0,) + IN_SPLITS))

LANES = 128
VMEM_LIMIT = 56 * 1024 * 1024

PROJ_ROWS = 512
M_CHUNK_ROWS = 128
MLSTM_SEQS = 2
ATT_Q = 256
MERGE_ROWS = 512
XATT_ROWS = 512
RANK_ROWS = 1024
EXPERT_ROWS = 256
COMBINE_TOKENS = 256
MOE_PART_SHARES = (5, 3)
SC_CORES = 2
SC_SUBCORES = 16
SC_WORKERS = SC_CORES * SC_SUBCORES
SC_ROWS = 32
NEG_BIG = -1e30
LOG2_E = 1.4426950408889634
ROW_TILES = D_MODEL // LANES

C_QLAT = 0
C_KVLAT = C_QLAT + A_Q_RANK
C_KR = C_KVLAT + A_KV_RANK
C_KRS = C_KR + LANES
C_GATE = C_KRS + LANES
C_XM = C_GATE + LANES
C_VM = C_XM + M_WIDTH
C_OP = C_VM + M_WIDTH
C_GM = C_OP + M_WIDTH
C_GA = C_GM + D_MODEL
C_END = C_GA + D_MODEL


def _cparams(sem, vmem=VMEM_LIMIT):
    return pltpu.CompilerParams(dimension_semantics=sem, vmem_limit_bytes=vmem)


def _const_spec(shape):
    nd = len(shape)
    return pl.BlockSpec(shape, lambda *a: (0,) * nd)


def _layer_norm_rows(v, g, b):
    mu = jnp.mean(v, axis=-1, keepdims=True)
    d = v - mu
    var = jnp.mean(d * d, axis=-1, keepdims=True)
    return d * lax.rsqrt(var + EPS) * g + b


def _proj_kernel(x_ref, pos_ref, w_ref, gq_ref, gkv_ref, wuq_ref, wuqs_ref, wuk_ref, wuv_ref, invf_ref,
                 q_ref, k_ref, v_ref, xm_ref, vm_ref, op_ref, gm_ref, ga_ref, gate_ref):
    xb = x_ref[...].astype(BF16)

    def mm(lo, n):
        return jnp.dot(xb, w_ref[:, lo:lo + n], preferred_element_type=F32)

    xm_ref[...] = mm(C_XM, M_WIDTH).astype(BF16)
    vm_ref[...] = mm(C_VM, M_WIDTH).astype(BF16)
    op_ref[...] = mm(C_OP, M_WIDTH).astype(BF16)
    gm_ref[...] = mm(C_GM, D_MODEL).astype(BF16)
    ga_ref[...] = mm(C_GA, D_MODEL).astype(BF16)
    gate_ref[...] = mm(C_GATE, LANES)

    ang = pos_ref[...].astype(F32) * invf_ref[...]
    cos = jnp.cos(ang)
    sin = jnp.sin(ang)

    q_lat = mm(C_QLAT, A_Q_RANK)
    qn = (q_lat * lax.rsqrt(jnp.mean(q_lat * q_lat, axis=-1, keepdims=True) + EPS) * gq_ref[...]).astype(BF16)
    q = jnp.dot(qn, wuq_ref[...], preferred_element_type=F32)
    qs = jnp.dot(qn, wuqs_ref[...], preferred_element_type=F32)
    kv_lat = mm(C_KVLAT, A_KV_RANK)
    kvn = (kv_lat * lax.rsqrt(jnp.mean(kv_lat * kv_lat, axis=-1, keepdims=True) + EPS) * gkv_ref[...]).astype(BF16)
    kn = jnp.dot(kvn, wuk_ref[...], preferred_element_type=F32)
    lane = lax.broadcasted_iota(I32, (1, A_HEADS * LANES), 1)
    ones_lane = (lane % LANES == A_VDIM).astype(F32)
    v_ref[...] = (jnp.dot(kvn, wuv_ref[...], preferred_element_type=F32) + ones_lane).astype(BF16)
    k_pe = mm(C_KR, LANES) * cos + mm(C_KRS, LANES) * sin
    scale = A_QK ** -0.5 * LOG2_E
    for h in range(A_HEADS):
        sl = slice(h * LANES, (h + 1) * LANES)
        q_ref[:, sl] = ((q[:, sl] * cos + qs[:, sl] * sin) * scale).astype(BF16)
        k_ref[:, sl] = (kn[:, sl] + k_pe).astype(BF16)


def _proj_weights(w_in, w_uq, w_ukv):
    o = IN_OFFSETS
    half = A_ROPE // 2
    w_q, w_kv, w_kr = w_in[:, o[0]:o[1]], w_in[:, o[1]:o[2]], w_in[:, o[2]:o[3]]
    w_xm, w_vm, w_op = w_in[:, o[3]:o[4]], w_in[:, o[4]:o[5]], w_in[:, o[5]:o[6]]
    w_i, w_f, w_gm, w_ga = w_in[:, o[6]:o[7]], w_in[:, o[7]:o[8]], w_in[:, o[8]:o[9]], w_in[:, o[9]:o[10]]
    d = w_in.shape[0]
    z = lambda n: jnp.zeros((d, n), w_in.dtype)
    kr = jnp.concatenate([z(A_NOPE), w_kr, z(LANES - A_QK)], axis=1)
    krs = jnp.concatenate([z(A_NOPE), -w_kr[:, half:], w_kr[:, :half], z(LANES - A_QK)], axis=1)
    gate = jnp.concatenate([w_i, w_f, z(LANES - 2 * M_HEADS)], axis=1)
    w_r = jnp.concatenate([w_q, w_kv, kr, krs, gate, w_xm, w_vm, w_op, w_gm, w_ga], axis=1).astype(BF16)

    uq = w_uq.reshape(A_Q_RANK, A_HEADS, A_QK)
    zq = jnp.zeros((A_Q_RANK, A_HEADS, LANES - A_QK), w_uq.dtype)
    zn = jnp.zeros((A_Q_RANK, A_HEADS, A_NOPE), w_uq.dtype)
    uq_pad = jnp.concatenate([uq, zq], axis=-1).reshape(A_Q_RANK, A_HEADS * LANES).astype(BF16)
    uqs_pad = jnp.concatenate([zn, -uq[..., A_NOPE + half:], uq[..., A_NOPE:A_NOPE + half], zq],
                              axis=-1).reshape(A_Q_RANK, A_HEADS * LANES).astype(BF16)
    ukv = w_ukv.reshape(A_KV_RANK, A_HEADS, A_NOPE + A_VDIM)
    zk = jnp.zeros((A_KV_RANK, A_HEADS, LANES - A_NOPE), w_ukv.dtype)
    uk_pad = jnp.concatenate([ukv[..., :A_NOPE], zk], axis=-1).reshape(A_KV_RANK, A_HEADS * LANES).astype(BF16)
    zv = jnp.zeros((A_KV_RANK, A_HEADS, LANES - A_VDIM), w_ukv.dtype)
    uv_pad = jnp.concatenate([ukv[..., A_NOPE:], zv], axis=-1).reshape(A_KV_RANK, A_HEADS * LANES).astype(BF16)
    return w_r, uq_pad, uqs_pad, uk_pad, uv_pad


def _input_projection(x2, pos2, w_in, g_qlat, g_kvlat, w_uq, w_ukv):
    t = x2.shape[0]
    tm = min(PROJ_ROWS, t)
    w_r, uq_pad, uqs_pad, uk_pad, uv_pad = _proj_weights(w_in, w_uq, w_ukv)
    half = A_ROPE // 2
    inv_freq = ROPE_THETA ** (-jnp.arange(half, dtype=F32) / half)
    invf = jnp.concatenate([jnp.zeros((A_NOPE,), F32), inv_freq, inv_freq,
                            jnp.zeros((LANES - A_QK,), F32)]).reshape(1, LANES)
    hw = A_HEADS * LANES
    row = lambda n: pl.BlockSpec((tm, n), lambda i: (i, 0))
    outs = [
        jax.ShapeDtypeStruct((t, hw), BF16), jax.ShapeDtypeStruct((t, hw), BF16), jax.ShapeDtypeStruct((t, hw), BF16),
        jax.ShapeDtypeStruct((t, M_WIDTH), BF16), jax.ShapeDtypeStruct((t, M_WIDTH), BF16),
        jax.ShapeDtypeStruct((t, M_WIDTH), BF16),
        jax.ShapeDtypeStruct((t, D_MODEL), BF16), jax.ShapeDtypeStruct((t, D_MODEL), BF16),
        jax.ShapeDtypeStruct((t, LANES), F32),
    ]
    return pl.pallas_call(
        _proj_kernel,
        out_shape=outs,
        grid=(t // tm,),
        in_specs=[row(D_MODEL), row(1), _const_spec(w_r.shape), _const_spec((1, A_Q_RANK)),
                  _const_spec((1, A_KV_RANK)), _const_spec(uq_pad.shape), _const_spec(uqs_pad.shape),
                  _const_spec(uk_pad.shape), _const_spec(uv_pad.shape), _const_spec((1, LANES))],
        out_specs=[row(hw), row(hw), row(hw), row(M_WIDTH), row(M_WIDTH), row(M_WIDTH), row(D_MODEL), row(D_MODEL),
                   row(LANES)],
        compiler_params=_cparams(("parallel",)),
        name="input_projection",
    )(x2, pos2, w_r, g_qlat.reshape(1, -1), g_kvlat.reshape(1, -1), uq_pad, uqs_pad, uk_pad, uv_pad, invf)


def _log_sigmoid(v):
    return jnp.minimum(v, 0.0) - jnp.log1p(jnp.exp(-jnp.abs(v)))


def _mlstm_kernel(xm_ref, vm_ref, op_ref, gate_ref, convw_ref, convb_ref, wqk_ref, gbias_ref, ghead_ref,
                  skip_ref, tril_ref, out_ref, xpad_s, ctn_s, m_s, *, seq_len):
    s = seq_len
    lc = M_CHUNK_ROWS
    halo = 8
    for bb in range(MLSTM_SEQS):
        xpad_s[bb, 0:halo, :] = jnp.zeros((halo, M_WIDTH), F32)
        xpad_s[bb, halo:, :] = xm_ref[bb * s:(bb + 1) * s, :].astype(F32)
    ctn_s[...] = jnp.zeros_like(ctn_s)
    m_s[...] = jnp.zeros_like(m_s)
    rows = lax.broadcasted_iota(I32, (lc, lc), 0)
    cols = lax.broadcasted_iota(I32, (lc, lc), 1)
    causal = rows >= cols
    row_id = lax.broadcasted_iota(I32, (lc, LANES), 0)
    ones_blk = (lax.broadcasted_iota(I32, (lc, LANES), 1) == 0).astype(BF16)
    kscale = M_HEAD_DIM ** -0.5
    hsl = [slice(h * M_HEAD_DIM, (h + 1) * M_HEAD_DIM) for h in range(M_HEADS)]
    hcl = [slice(h, h + 1) for h in range(M_HEADS)]
    streams = [(bb, h) for bb in range(MLSTM_SEQS) for h in range(M_HEADS)]

    def gates(bb, r0):
        win = xpad_s[bb, pl.ds(r0 - bb * s, lc + halo), :]
        conv = convb_ref[...]
        for j in range(M_CONV):
            lo = halo - (M_CONV - 1) + j
            conv = conv + win[lo:lo + lc, :] * convw_ref[j:j + 1, :]
        xc = conv * jax.nn.sigmoid(conv)
        g = gate_ref[pl.ds(r0, lc), :] + gbias_ref[...]
        ls = _log_sigmoid(g)
        bc = jnp.dot(tril_ref[...], ls, preferred_element_type=F32, precision=lax.Precision.HIGHEST)
        b0 = pltpu.roll(bc, LANES - M_HEADS, axis=1)
        u = g - b0
        cm = u
        shift = 1
        while shift < lc:
            cm = jnp.maximum(cm, jnp.where(row_id >= shift, pltpu.roll(cm, shift, axis=0), -jnp.inf))
            shift *= 2
        m_prev = m_s[bb:bb + 1, :]
        g_inter = b0 + m_prev
        m = jnp.maximum(g_inter, b0 + cm)
        b_tot = b0[lc - 1:lc, :]
        aw = b_tot - b0 + g
        m_chunk = jnp.max(aw, axis=0, keepdims=True)
        m_new = jnp.maximum(b_tot + m_prev, m_chunk)
        m_s[bb:bb + 1, :] = m_new
        return dict(xc=xc, w_inter=jnp.exp(g_inter - m), e_negm=jnp.exp(-m), a_mat=b0 - m, wa=jnp.exp(aw - m_chunk),
                    s_old=jnp.exp(b_tot + m_prev - m_new), s_new=jnp.exp(m_chunk - m_new), u_t=u.T)

    def chunk(c, carry):
        r0s = [pl.multiple_of(bb * s + c * lc, lc) for bb in range(MLSTM_SEQS)]
        gs = [gates(bb, r0s[bb]) for bb in range(MLSTM_SEQS)]
        qk = [jnp.dot(gs[bb]["xc"].astype(BF16), wqk_ref[...], preferred_element_type=F32)
              for bb in range(MLSTM_SEQS)]
        q_b = [qk[bb][:, hsl[h]].astype(BF16) for bb, h in streams]
        k_f = [qk[bb][:, M_WIDTH + h * M_HEAD_DIM:M_WIDTH + (h + 1) * M_HEAD_DIM] * kscale for bb, h in streams]
        v_aug = [jnp.concatenate([vm_ref[pl.ds(r0s[bb], lc), hsl[h]], ones_blk], axis=1) for bb, h in streams]
        ctn_prev = [ctn_s[bb * M_HEADS + h] for bb, h in streams]
        s_raw = [lax.dot_general(q_b[i], k_f[i].astype(BF16), (((1,), (1,)), ((), ())), preferred_element_type=F32)
                 for i in range(len(streams))]
        inter = [jnp.dot(q_b[i], ctn_prev[i].astype(BF16), preferred_element_type=F32)
                 for i in range(len(streams))]
        ctn_c = [lax.dot_general((k_f[i] * gs[bb]["wa"][:, hcl[h]]).astype(BF16), v_aug[i], (((0,), (0,)), ((), ())),
                                 preferred_element_type=F32) for i, (bb, h) in enumerate(streams)]
        sc_b = [(s_raw[i] * jnp.exp(jnp.where(causal, gs[bb]["a_mat"][:, hcl[h]] + gs[bb]["u_t"][hcl[h], :],
                                              -jnp.inf))).astype(BF16) for i, (bb, h) in enumerate(streams)]
        intra = [jnp.dot(sc_b[i], v_aug[i], preferred_element_type=F32) for i in range(len(streams))]
        for i, (bb, h) in enumerate(streams):
            g = gs[bb]
            ctn_s[bb * M_HEADS + h] = g["s_old"][:, hcl[h]] * ctn_prev[i] + g["s_new"][:, hcl[h]] * ctn_c[i]
            wi = g["w_inter"][:, hcl[h]]
            num = wi * inter[i][:, :M_HEAD_DIM] + intra[i][:, :M_HEAD_DIM]
            den = wi * inter[i][:, M_HEAD_DIM:M_HEAD_DIM + 1] + intra[i][:, M_HEAD_DIM:M_HEAD_DIM + 1]
            hh = num / jnp.maximum(jnp.abs(den), g["e_negm"][:, hcl[h]])
            mu = jnp.mean(hh, axis=-1, keepdims=True)
            dv = hh - mu
            var = jnp.mean(dv * dv, axis=-1, keepdims=True)
            hn = dv * lax.rsqrt(var + EPS) * ghead_ref[:, hsl[h]]
            y = (jax.nn.sigmoid(op_ref[pl.ds(r0s[bb], lc), hsl[h]].astype(F32))
                 * (hn + skip_ref[:, hsl[h]] * g["xc"][:, hsl[h]]))
            out_ref[pl.ds(r0s[bb], lc), hsl[h]] = y.astype(BF16)
        return carry

    lax.fori_loop(0, s // lc, chunk, 0)


def _mlstm_branch(xm, vm, op, gate, bsz, s, b_i, b_f, conv_w, conv_b, w_mq, w_mk, g_mhead, w_mskip):
    lc = M_CHUNK_ROWS
    gbias = jnp.concatenate([b_i, b_f, jnp.zeros((LANES - 2 * M_HEADS,), F32)]).reshape(1, LANES)
    tril = jnp.tril(jnp.ones((lc, lc), F32))
    eye = jnp.eye(M_HEADS, dtype=F32)
    block_diag = lambda w: jnp.einsum('hde,hg->hdge', w, eye).reshape(M_WIDTH, M_WIDTH)
    w_qk = jnp.concatenate([block_diag(w_mq), block_diag(w_mk)], axis=1).astype(BF16)
    nseq = MLSTM_SEQS
    assert bsz % nseq == 0
    seq = lambda n: pl.BlockSpec((nseq * s, n), lambda b: (b, 0))
    return pl.pallas_call(
        functools.partial(_mlstm_kernel, seq_len=s),
        out_shape=jax.ShapeDtypeStruct((bsz * s, M_WIDTH), BF16),
        grid=(bsz // nseq,),
        in_specs=[seq(M_WIDTH), seq(M_WIDTH), seq(M_WIDTH), seq(LANES),
                  _const_spec((M_CONV, M_WIDTH)), _const_spec((1, M_WIDTH)),
                  _const_spec((M_WIDTH, 2 * M_WIDTH)),
                  _const_spec((1, LANES)), _const_spec((1, M_WIDTH)), _const_spec((1, M_WIDTH)),
                  _const_spec((lc, lc))],
        out_specs=seq(M_WIDTH),
        scratch_shapes=[pltpu.VMEM((nseq, s + 8, M_WIDTH), F32),
                        pltpu.VMEM((nseq * M_HEADS, M_HEAD_DIM, 2 * LANES), F32), pltpu.VMEM((8, LANES), F32)],
        compiler_params=_cparams(("parallel",)),
        name="mlstm_branch",
    )(xm, vm, op, gate, conv_w, conv_b.reshape(1, -1), w_qk, gbias, g_mhead.reshape(1, -1), w_mskip.reshape(1, -1),
      tril)


def _attn_kernel(q_ref, k_ref, v_ref, o_ref, s_scr, m_s, acc_s):
    tq = q_ref.shape[0]
    i = pl.program_id(1)
    rows = lax.broadcasted_iota(I32, (tq, tq), 0)
    cols = lax.broadcasted_iota(I32, (tq, tq), 1)

    def lane_tile_max(sc):
        out = sc[:, :LANES]
        for t in range(1, tq // LANES):
            out = jnp.maximum(out, sc[:, t * LANES:(t + 1) * LANES])
        return out

    heads = [slice(h * LANES, (h + 1) * LANES) for h in range(A_HEADS)]
    m_s[...] = jnp.full(m_s.shape, NEG_BIG, F32)
    acc_s[...] = jnp.zeros_like(acc_s)

    def pass_a(j, diagonal):
        k0 = pl.multiple_of(j * tq, tq)
        for h, hs in enumerate(heads):
            sc = lax.dot_general(q_ref[:, hs], k_ref[pl.ds(k0, tq), hs], (((1,), (1,)), ((), ())),
                                 preferred_element_type=F32)
            if diagonal:
                sc = jnp.where(cols <= rows, sc, NEG_BIG)
            s_scr[h, j] = sc
            m_s[h] = jnp.maximum(m_s[h], lane_tile_max(sc))

    def for_blocks(n, fn):
        def pair(p, c):
            fn(2 * p)
            fn(2 * p + 1)
            return c

        lax.fori_loop(0, n // 2, pair, 0)

        @pl.when(n % 2 == 1)
        def _():
            fn(n - 1)

    for_blocks(i, lambda j: pass_a(j, False))
    pass_a(i, True)
    for h in range(A_HEADS):
        m_s[h] = jnp.broadcast_to(jnp.max(m_s[h], axis=1, keepdims=True), (tq, LANES))

    def pass_b(j):
        k0 = pl.multiple_of(j * tq, tq)
        for h, hs in enumerate(heads):
            m_row = m_s[h]
            p = jnp.exp2(s_scr[h, j] - jnp.concatenate([m_row] * (tq // LANES), axis=1)).astype(BF16)
            acc_s[h] += jnp.dot(p, v_ref[pl.ds(k0, tq), hs], preferred_element_type=F32)

    for_blocks(i + 1, pass_b)
    for h, hs in enumerate(heads):
        acc = acc_s[h]
        o_ref[:, hs] = (acc / acc[:, A_VDIM:A_VDIM + 1]).astype(BF16)


def _mla_attention(q, k, v, bsz, s):
    tq = min(ATT_Q, s)
    nq = s // tq
    hw = A_HEADS * LANES
    return pl.pallas_call(
        _attn_kernel,
        out_shape=jax.ShapeDtypeStruct(q.shape, BF16),
        grid=(bsz, nq),
        in_specs=[pl.BlockSpec((tq, hw), lambda b, i: (b * nq + i, 0)),
                  pl.BlockSpec((s, hw), lambda b, i: (b, 0)),
                  pl.BlockSpec((s, hw), lambda b, i: (b, 0))],
        out_specs=pl.BlockSpec((tq, hw), lambda b, i: (b * nq + i, 0)),
        scratch_shapes=[pltpu.VMEM((A_HEADS, nq, tq, tq), F32), pltpu.VMEM((A_HEADS, tq, LANES), F32),
                        pltpu.VMEM((A_HEADS, tq, LANES), F32)],
        compiler_params=_cparams(("parallel", "arbitrary")),
        name="mla_attention",
    )(q, k, v)


def _merge_kernel(ym_ref, oa_ref, gm_ref, ga_ref, x_ref, wbm_ref, wba_ref, wmix_ref, g_ref, b_ref, h_ref):
    y_m = jnp.dot(ym_ref[...], wbm_ref[...], preferred_element_type=F32)
    y_a = jnp.dot(oa_ref[...], wba_ref[...], preferred_element_type=F32)
    mixed = jax.nn.sigmoid(gm_ref[...].astype(F32)) * y_m + jax.nn.sigmoid(ga_ref[...].astype(F32)) * y_a
    mix = jnp.dot(mixed.astype(BF16), wmix_ref[...], preferred_element_type=F32)
    h_ref[...] = _layer_norm_rows(DN_ALPHA * x_ref[...] + mix, g_ref[...], b_ref[...])


def _merge(ym, oa, gm, ga, x2, w_br_m, w_br_a, w_mix_out, ln_g, ln_b):
    t = x2.shape[0]
    tm = min(MERGE_ROWS, t)
    wba = jnp.concatenate([w_br_a.reshape(A_HEADS, A_VDIM, D_MODEL),
                           jnp.zeros((A_HEADS, LANES - A_VDIM, D_MODEL), w_br_a.dtype)], axis=1)
    wba = wba.reshape(A_HEADS * LANES, D_MODEL).astype(BF16)
    row = lambda n: pl.BlockSpec((tm, n), lambda i: (i, 0))
    return pl.pallas_call(
        _merge_kernel,
        out_shape=jax.ShapeDtypeStruct((t, D_MODEL), F32),
        grid=(t // tm,),
        in_specs=[row(M_WIDTH), row(A_HEADS * LANES), row(D_MODEL), row(D_MODEL), row(D_MODEL),
                  _const_spec((M_WIDTH, D_MODEL)), _const_spec((A_HEADS * LANES, D_MODEL)),
                  _const_spec((D_MODEL, D_MODEL)), _const_spec((1, D_MODEL)), _const_spec((1, D_MODEL))],
        out_specs=row(D_MODEL),
        compiler_params=_cparams(("parallel",)),
        name="merge_deepnorm1",
    )(ym, oa, gm, ga, x2, w_br_m.astype(BF16), wba, w_mix_out.astype(BF16), ln_g.reshape(1, -1), ln_b.reshape(1, -1))


def _kv_kernel(m_ref, w_ref, o_ref):
    o_ref[...] = jnp.dot(m_ref[...].astype(BF16), w_ref[...], preferred_element_type=F32).astype(BF16)


def _mem_kv(mem2, w_ckv):
    r = mem2.shape[0]
    tm = min(512, r)
    return pl.pallas_call(
        _kv_kernel,
        out_shape=jax.ShapeDtypeStruct((r, 2 * D_MODEL), BF16),
        grid=(r // tm,),
        in_specs=[pl.BlockSpec((tm, D_MODEL), lambda i: (i, 0)), _const_spec((D_MODEL, 2 * D_MODEL))],
        out_specs=pl.BlockSpec((tm, 2 * D_MODEL), lambda i: (i, 0)),
        compiler_params=_cparams(("parallel",)),
        name="memory_kv",
    )(mem2, w_ckv.astype(BF16))


def _xattn_kernel(h_ref, kv_ref, wq_ref, wo_ref, g_ref, b_ref, wr_ref, br_ref, *rest):
    h2_ref, idx_ref, gate_ref = rest[-3:]
    h1 = h_ref[...]
    q = jnp.dot(h1.astype(BF16), wq_ref[...], preferred_element_type=F32).astype(BF16)
    scale = X_HEAD_DIM ** -0.5
    ksl = [slice(hd * X_HEAD_DIM, (hd + 1) * X_HEAD_DIM) for hd in range(X_HEADS)]
    vsl = [slice(D_MODEL + hd * X_HEAD_DIM, D_MODEL + (hd + 1) * X_HEAD_DIM) for hd in range(X_HEADS)]
    scs = [lax.dot_general(q[:, ks], kv_ref[:, ks], (((1,), (1,)), ((), ())), preferred_element_type=F32) * scale
           for ks in ksl]
    ps = [jnp.exp(sc - jnp.max(sc, axis=1, keepdims=True)) for sc in scs]
    ps = [(p / jnp.sum(p, axis=1, keepdims=True)).astype(BF16) for p in ps]
    o = jnp.concatenate([jnp.dot(p, kv_ref[:, vs], preferred_element_type=F32).astype(BF16)
                         for p, vs in zip(ps, vsl)], axis=1)
    att = jnp.dot(o, wo_ref[...], preferred_element_type=F32)
    h2 = _layer_norm_rows(DN_ALPHA * h1 + att, g_ref[...], b_ref[...])
    for c in range(ROW_TILES):
        h2_ref[pl.ds(c, h2.shape[0], stride=ROW_TILES), :] = h2[:, c * LANES:(c + 1) * LANES]

    h2_hi = h2.astype(BF16)
    h2_lo = (h2 - h2_hi.astype(F32)).astype(BF16)
    hi_prod = jnp.dot(h2_hi, wr_ref[...], preferred_element_type=F32)
    lo_prod = jnp.dot(h2_lo, wr_ref[:, :LANES], preferred_element_type=F32)
    logits = hi_prod[:, :LANES] + (hi_prod[:, LANES:] + lo_prod) + br_ref[...]
    tm = logits.shape[0]
    lane = lax.broadcasted_iota(I32, (tm, LANES), 1)
    lane_f = lane.astype(F32)
    work = jnp.where(lane < N_EXPERTS, logits, -jnp.inf)
    vals, idxs = [], []
    for _ in range(TOP_K):
        mx = jnp.max(work, axis=1, keepdims=True)
        ix = jnp.min(jnp.where(work == mx, lane_f, float(LANES)), axis=1, keepdims=True)
        vals.append(mx)
        idxs.append(ix)
        work = jnp.where(lane_f == ix, -jnp.inf, work)
    es = [jnp.exp(vv - vals[0]) for vv in vals]
    tot = es[0] + es[1] + es[2] + es[3]
    idx_slab = jnp.zeros((tm, LANES), F32)
    gate_slab = jnp.zeros((tm, LANES), F32)
    for kk in range(TOP_K):
        idx_slab = jnp.where(lane == kk, idxs[kk], idx_slab)
        gate_slab = jnp.where(lane == kk, es[kk] / tot, gate_slab)
    idx_ref[...] = idx_slab.astype(I32)
    gate_ref[...] = gate_slab


def _cross_attention_router(h1, kv, n_mem, b0, bsz, s, w_cq, w_co, ln_g, ln_b, w_router, b_router, after):
    t = bsz * s
    tm = min(XATT_ROWS, s)
    ns = s // tm
    wr = jnp.concatenate([w_router, jnp.zeros((D_MODEL, LANES - N_EXPERTS), F32)], axis=1)
    wr_hi = wr.astype(BF16)
    wr = jnp.concatenate([wr_hi, (wr - wr_hi.astype(F32)).astype(BF16)], axis=1)
    br = jnp.concatenate([b_router, jnp.zeros((LANES - N_EXPERTS,), F32)]).reshape(1, LANES)
    row = lambda n: pl.BlockSpec((tm, n), lambda b, i: (b * ns + i, 0))
    h1_rows = pl.BlockSpec((tm, D_MODEL), lambda b, i: ((b0 + b) * ns + i, 0))
    in_specs = [h1_rows, pl.BlockSpec((n_mem, 2 * D_MODEL), lambda b, i: (b0 + b, 0)),
                _const_spec((D_MODEL, D_MODEL)), _const_spec((D_MODEL, D_MODEL)),
                _const_spec((1, D_MODEL)), _const_spec((1, D_MODEL)),
                _const_spec((D_MODEL, 2 * LANES)), _const_spec((1, LANES))]
    args = [h1, kv, w_cq.astype(BF16), w_co.astype(BF16), ln_g.reshape(1, -1), ln_b.reshape(1, -1), wr, br]
    if after is not None:
        in_specs.append(pl.BlockSpec(memory_space=pl.ANY))
        args.append(after)
    return pl.pallas_call(
        _xattn_kernel,
        out_shape=[jax.ShapeDtypeStruct((t * ROW_TILES, LANES), F32), jax.ShapeDtypeStruct((t, LANES), I32),
                   jax.ShapeDtypeStruct((t, LANES), F32)],
        grid=(bsz, ns),
        in_specs=in_specs,
        out_specs=[pl.BlockSpec((tm * ROW_TILES, LANES), lambda b, i: (b * ns + i, 0)), row(LANES), row(LANES)],
        compiler_params=_cparams(("parallel", "parallel")),
        name="cross_attention_router",
    )(*args)


def _rank_kernel(idx_ref, ltri_ref, rank_ref, cnt_ref, carry_s):
    @pl.when(pl.program_id(0) == 0)
    def _():
        carry_s[...] = jnp.zeros_like(carry_s)

    idx = idx_ref[...]
    tm = idx.shape[0]
    lane = lax.broadcasted_iota(I32, (tm, LANES), 1)
    onehots = [(lane == idx[:, kk:kk + 1]).astype(F32) for kk in range(TOP_K)]
    sel = onehots[0] + onehots[1] + onehots[2] + onehots[3]
    before = jnp.dot(ltri_ref[...], sel.astype(BF16), preferred_element_type=F32) + carry_s[0:1, :]
    rank_slab = jnp.zeros((tm, LANES), F32)
    for kk in range(TOP_K):
        r = jnp.sum(onehots[kk] * before, axis=1, keepdims=True)
        rank_slab = jnp.where(lane == kk, r, rank_slab)
    rank_ref[...] = rank_slab.astype(I32)
    carry_s[0:1, :] = carry_s[0:1, :] + jnp.sum(sel, axis=0, keepdims=True)
    cnt_ref[...] = jnp.broadcast_to(carry_s[0:1, :], cnt_ref.shape).astype(I32)


def _routing_ranks(idx_slab):
    t = idx_slab.shape[0]
    tm = min(RANK_ROWS, t)
    assert t % tm == 0
    ltri = jnp.tril(jnp.ones((tm, tm), BF16), k=-1)
    return pl.pallas_call(
        _rank_kernel,
        out_shape=[jax.ShapeDtypeStruct((t, LANES), I32), jax.ShapeDtypeStruct((8, LANES), I32)],
        grid=(t // tm,),
        in_specs=[pl.BlockSpec((tm, LANES), lambda i: (i, 0)), _const_spec((tm, tm))],
        out_specs=[pl.BlockSpec((tm, LANES), lambda i: (i, 0)), _const_spec((8, LANES))],
        scratch_shapes=[pltpu.VMEM((8, LANES), F32)],
        compiler_params=_cparams(("arbitrary",)),
        name="routing_ranks",
    )(idx_slab, ltri)


def _dest_kernel(idx_ref, rank_ref, pstart_ref, dest_ref):
    idx = idx_ref[...]
    tm = idx.shape[0]
    lane = lax.broadcasted_iota(I32, (tm, LANES), 1)
    dest = rank_ref[...].astype(F32)
    pstart = pstart_ref[0:1, :].astype(F32)
    for kk in range(TOP_K):
        start = jnp.sum(jnp.where(lane == idx[:, kk:kk + 1], pstart, 0.0), axis=1, keepdims=True)
        dest = dest + jnp.where(lane == kk, start, 0.0)
    dest_ref[...] = dest.T[0:8, :].astype(I32)


def _dest_rows(idx_slab, rank_slab, pstart_row):
    t = rank_slab.shape[0]
    tm = min(RANK_ROWS, t)
    return pl.pallas_call(
        _dest_kernel,
        out_shape=jax.ShapeDtypeStruct((8, t), I32),
        grid=(t // tm,),
        in_specs=[pl.BlockSpec((tm, LANES), lambda i: (i, 0)), pl.BlockSpec((tm, LANES), lambda i: (i, 0)),
                  _const_spec((8, LANES))],
        out_specs=pl.BlockSpec((8, tm), lambda i: (0, i)),
        compiler_params=_cparams(("parallel",)),
        name="routing_dest",
    )(idx_slab, rank_slab, pstart_row)


def _sc_mesh():
    return plsc.VectorSubcoreMesh(core_axis_name="c", subcore_axis_name="s", num_cores=SC_CORES,
                                  num_subcores=SC_SUBCORES)


def _sc_worker():
    return lax.axis_index("s") * SC_CORES + lax.axis_index("c")


def _dispatch(h2, dest_kt, pad_rows, n_rows):
    t = dest_kt.shape[1]
    w = SC_ROWS
    per_w = t // SC_WORKERS
    steps = per_w // w
    assert steps % 2 == 0 and steps * w * SC_WORKERS == t
    pad_steps = pad_rows.shape[0] // (SC_WORKERS * w)
    dest_w = dest_kt.reshape(TOP_K, SC_WORKERS, steps, w)
    pad_w = pad_rows.reshape(SC_WORKERS, pad_steps, w)
    zeros = jnp.zeros((w, ROW_TILES, LANES), F32)

    def body(x_hbm, dest_hbm, pad_hbm, zeros_hbm, o_hbm, idx_v, pad_v, rows_v, sem_in, sem_out):
        wid = _sc_worker()
        tok0 = wid * per_w
        for kk in range(TOP_K):
            pltpu.sync_copy(dest_hbm.at[kk, wid], idx_v.at[pl.ds(kk * steps, steps)])
        pltpu.sync_copy(pad_hbm.at[wid], pad_v)
        pltpu.sync_copy(zeros_hbm, rows_v.at[0])
        for j in range(pad_steps):
            pltpu.sync_copy(rows_v.at[0], o_hbm.at[pad_v.at[j]])

        def load(step, b):
            return pltpu.make_async_copy(x_hbm.at[pl.ds(tok0 + step * w, w)], rows_v.at[b], sem_in.at[b])

        def scatter(step, kk, b):
            return pltpu.make_async_copy(rows_v.at[b], o_hbm.at[idx_v.at[kk * steps + step]], sem_out.at[b])

        load(0, 0).start()

        @pl.loop(0, steps, step=2)
        def _(s0):
            for b in range(2):
                step = s0 + b
                load(step, b).wait()

                @pl.when(step >= 1)
                def _():
                    for kk in range(TOP_K):
                        scatter(step - 1, kk, 1 - b).wait()

                @pl.when(step + 1 < steps)
                def _():
                    load(step + 1, 1 - b).start()

                for kk in range(TOP_K):
                    scatter(step, kk, b).start()

        for kk in range(TOP_K):
            scatter(steps - 1, kk, 1).wait()

    return pl.kernel(
        body,
        out_type=jax.ShapeDtypeStruct((n_rows, ROW_TILES, LANES), F32),
        mesh=_sc_mesh(),
        scratch_types=[pltpu.VMEM((steps * TOP_K, w), I32), pltpu.VMEM((pad_steps, w), I32),
                       pltpu.VMEM((2, w, ROW_TILES, LANES), F32), pltpu.SemaphoreType.DMA((2,)),
                       pltpu.SemaphoreType.DMA((2,))],
        name="moe_dispatch_sc",
    )(h2, dest_w, pad_w, zeros)


def _gather_rows(yb, dest_kt):
    n = dest_kt.shape[0]
    w = SC_ROWS
    per_w = n // SC_WORKERS
    steps = per_w // w
    assert steps % 2 == 0 and steps * w * SC_WORKERS == n
    dest_w = dest_kt.reshape(SC_WORKERS, steps, w)

    def body(y_hbm, dest_hbm, o_hbm, idx_v, rows_v, sem_g, sem_w):
        wid = _sc_worker()
        row0 = wid * per_w
        pltpu.sync_copy(dest_hbm.at[wid], idx_v)

        def gather(step, b):
            return pltpu.make_async_copy(y_hbm.at[idx_v.at[step]], rows_v.at[b], sem_g.at[b])

        def write(step, b):
            return pltpu.make_async_copy(rows_v.at[b], o_hbm.at[pl.ds(row0 + step * w, w)], sem_w.at[b])

        gather(0, 0).start()

        @pl.loop(0, steps, step=2)
        def _(s0):
            for b in range(2):
                step = s0 + b
                gather(step, b).wait()

                @pl.when(step >= 1)
                def _():
                    write(step - 1, 1 - b).wait()

                @pl.when(step + 1 < steps)
                def _():
                    gather(step + 1, 1 - b).start()

                write(step, b).start()

        write(steps - 1, 1).wait()

    return pl.kernel(
        body,
        out_type=jax.ShapeDtypeStruct((n, ROW_TILES, LANES), F32),
        mesh=_sc_mesh(),
        scratch_types=[pltpu.VMEM((steps, w), I32), pltpu.VMEM((2, w, ROW_TILES, LANES), F32),
                       pltpu.SemaphoreType.DMA((2,)), pltpu.SemaphoreType.DMA((2,))],
        name="moe_gather_sc",
    )(yb, dest_w)


def _rows_2d(ref, n):
    return jnp.concatenate([ref[pl.ds(c, n, stride=ROW_TILES), :] for c in range(ROW_TILES)], axis=1)


def _store_rows(ref, val):
    for c in range(ROW_TILES):
        ref[pl.ds(c, val.shape[0], stride=ROW_TILES), :] = val[:, c * LANES:(c + 1) * LANES]


def _expert_kernel(blk_e_ref, first_ref, slot_ref, next_ref, nblk_ref, x_ref, wg_hbm, wl_hbm, bg_ref, bl_ref, wd_hbm,
                   bd_ref, y_ref, wg_buf, wl_buf, wd_stage, wd_buf, sem):
    i = pl.program_id(0)

    def fetch(e, sl):
        return [pltpu.make_async_copy(src.at[e], dst.at[sl], sem.at[sl, j])
                for j, (src, dst) in enumerate(((wg_hbm, wg_buf), (wl_hbm, wl_buf), (wd_hbm, wd_stage)))]

    @pl.when(i < nblk_ref[0])
    def _():
        sl = slot_ref[i]

        @pl.when(first_ref[i] == 1)
        def _():
            @pl.when(i == 0)
            def _():
                for cp in fetch(blk_e_ref[0], sl):
                    cp.start()

            for cp in fetch(blk_e_ref[i], sl):
                cp.wait()
            wd_buf[sl] = wd_stage[sl].astype(BF16)

            @pl.when(next_ref[i] >= 0)
            def _():
                for cp in fetch(next_ref[i], 1 - sl):
                    cp.start()

        xb = _rows_2d(x_ref, y_ref.shape[0] // ROW_TILES).astype(BF16)
        glu = jnp.dot(xb, wg_buf[sl], preferred_element_type=F32) + bg_ref[...]
        lin = jnp.dot(xb, wl_buf[sl], preferred_element_type=F32) + bl_ref[...]
        glu = jnp.minimum(glu, SWIGLU_LIMIT)
        lin = jnp.clip(lin, -SWIGLU_LIMIT, SWIGLU_LIMIT)
        act = glu * jax.nn.sigmoid(SWIGLU_ALPHA * glu) * (lin + 1.0)
        _store_rows(y_ref, jnp.dot(act.astype(BF16), wd_buf[sl], preferred_element_type=F32) + bd_ref[...])

    @pl.when(i >= nblk_ref[0])
    def _():
        y_ref[...] = jnp.zeros_like(y_ref)


def _split_kernel(w_ref, perm_ref, after_ref, g_ref, l_ref):
    del after_ref
    half = g_ref.shape[-1]
    sorted_cols = jnp.dot(w_ref[...].astype(BF16), perm_ref[...], preferred_element_type=F32)
    g_ref[...] = sorted_cols[:, :half].astype(BF16)
    l_ref[...] = sorted_cols[:, half:].astype(BF16)


def _split_gate_up(w_gu, after):
    e, d, n2 = w_gu.shape
    cw = 512
    src = np.concatenate([np.arange(0, cw, 2), np.arange(1, cw, 2)])
    perm = jnp.asarray(np.eye(cw, dtype=np.float32)[:, src], BF16)
    return pl.pallas_call(
        _split_kernel,
        out_shape=[jax.ShapeDtypeStruct((e, d, n2 // 2), BF16), jax.ShapeDtypeStruct((e, d, n2 // 2), BF16)],
        grid=(e, n2 // cw),
        in_specs=[pl.BlockSpec((None, d, cw), lambda i, c: (i, 0, c)), _const_spec((cw, cw)),
                  pl.BlockSpec(memory_space=pl.ANY)],
        out_specs=[pl.BlockSpec((None, d, cw // 2), lambda i, c: (i, 0, c)),
                   pl.BlockSpec((None, d, cw // 2), lambda i, c: (i, 0, c))],
        compiler_params=_cparams(("parallel", "parallel")),
        name="split_gate_up",
    )(w_gu, perm, after)


def _experts(xb, blk_meta, w_glu, w_lin, b_glu, b_lin, w_dn, b_dn):
    n_rows = xb.shape[0] // ROW_TILES
    bm = EXPERT_ROWS
    vec = lambda n: pl.BlockSpec((None, 1, n), lambda i, be, *_: (be[i], 0, 0))
    rows = pl.BlockSpec((bm * ROW_TILES, LANES), lambda i, *_: (i, 0))
    hbm = pl.BlockSpec(memory_space=pl.ANY)
    grid_spec = pltpu.PrefetchScalarGridSpec(
        num_scalar_prefetch=5,
        grid=(n_rows // bm,),
        in_specs=[rows, hbm, hbm, vec(D_EXPERT), vec(D_EXPERT), hbm, vec(D_MODEL)],
        out_specs=rows,
        scratch_shapes=[pltpu.VMEM((2, D_MODEL, D_EXPERT), BF16), pltpu.VMEM((2, D_MODEL, D_EXPERT), BF16),
                        pltpu.VMEM((2, D_EXPERT, D_MODEL), F32), pltpu.VMEM((2, D_EXPERT, D_MODEL), BF16),
                        pltpu.SemaphoreType.DMA((2, 3))],
    )
    return pl.pallas_call(
        _expert_kernel,
        out_shape=jax.ShapeDtypeStruct((n_rows * ROW_TILES, LANES), F32),
        grid_spec=grid_spec,
        compiler_params=_cparams(("arbitrary",)),
        name="moe_experts",
    )(*blk_meta, xb, w_glu, w_lin, b_glu, b_lin, w_dn, b_dn)


def _combine_kernel(h_ref, gate_ref, y0_ref, y1_ref, y2_ref, y3_ref, g_ref, b_ref, *rest):
    o_ref = rest[-1]
    tt = o_ref.shape[0]
    gates = gate_ref[...]
    acc = DN_ALPHA * _rows_2d(h_ref, tt)
    for kk, y_ref in enumerate((y0_ref, y1_ref, y2_ref, y3_ref)):
        acc = acc + gates[:, kk:kk + 1] * _rows_2d(y_ref, tt)
    o_ref[...] = _layer_norm_rows(acc, g_ref[...], b_ref[...])


def _combine(h2, gate_slab, yg, ln_g, ln_b, tok0, t_all, prev_out):
    t = h2.shape[0] // ROW_TILES
    tt = min(COMBINE_TOKENS, t)
    steps = t // tt
    assert tok0 % tt == 0 and t % tt == 0
    first = tok0 // tt
    row = lambda m: pl.BlockSpec((tt, m), lambda i: (i, 0))
    tiles = lambda kk: pl.BlockSpec((tt * ROW_TILES, LANES), lambda i: (kk * steps + i, 0))
    in_specs = [tiles(0), row(LANES),
                tiles(0), tiles(1), tiles(2), tiles(3), _const_spec((1, D_MODEL)), _const_spec((1, D_MODEL))]
    args = [h2, gate_slab, yg, yg, yg, yg, ln_g.reshape(1, -1), ln_b.reshape(1, -1)]
    aliases = {}
    if prev_out is not None:
        in_specs.append(pl.BlockSpec(memory_space=pl.ANY))
        args.append(prev_out)
        aliases = {len(args) - 1: 0}
    return pl.pallas_call(
        _combine_kernel,
        out_shape=jax.ShapeDtypeStruct((t_all, D_MODEL), F32),
        grid=(steps,),
        in_specs=in_specs,
        out_specs=pl.BlockSpec((tt, D_MODEL), lambda i: (first + i, 0)),
        input_output_aliases=aliases,
        compiler_params=_cparams(("parallel",)),
        name="moe_combine",
    )(*args)


def _route_and_dispatch(h2, idx_slab):
    t = h2.shape[0] // ROW_TILES
    bm = EXPERT_ROWS
    n_rows = t * TOP_K + N_EXPERTS * bm
    n_blocks = n_rows // bm

    rank_slab, cnt = _routing_ranks(idx_slab)
    counts = cnt[0, :N_EXPERTS]
    padded = (counts + bm - 1) // bm * bm
    pends = jnp.cumsum(padded)
    pstarts = pends - padded
    pstart_row = jnp.zeros((8, LANES), I32).at[:, :N_EXPERTS].set(pstarts[None, :])
    blk_start = jnp.arange(n_blocks, dtype=I32) * bm
    blk_e = jnp.minimum(jnp.sum((pends[None, :] <= blk_start[:, None]).astype(I32), axis=1), N_EXPERTS - 1)
    nblk = (pends[-1] // bm).astype(I32).reshape(1)
    active = counts > 0
    ordinal = jnp.cumsum(active.astype(I32)) - 1
    eid = jnp.arange(N_EXPERTS, dtype=I32)
    later = active[None, :] & (eid[None, :] > eid[:, None])
    next_active = jnp.where(later.any(axis=1), jnp.argmax(later, axis=1).astype(I32), -1)
    blk_first = jnp.concatenate([jnp.ones((1,), I32), (blk_e[1:] != blk_e[:-1]).astype(I32)])
    blk_onehot = (blk_e[:, None] == eid[None, :]).astype(I32)
    blk_slot = jnp.sum(blk_onehot * ordinal[None, :], axis=1) % 2
    blk_next = jnp.sum(blk_onehot * next_active[None, :], axis=1)
    blk_meta = (blk_e, blk_first, blk_slot, blk_next, nblk)
    dest_kt = _dest_rows(idx_slab, rank_slab, pstart_row)[:TOP_K]
    pad = padded - counts
    spare_base = pends[-1] + jnp.cumsum(bm - pad) - (bm - pad)
    pad_off = jnp.arange(bm, dtype=I32)[None, :]
    pad_rows = jnp.where(pad_off < pad[:, None], (pstarts + counts)[:, None] + pad_off,
                         spare_base[:, None] + pad_off - pad[:, None]).reshape(-1)
    xb = _dispatch(h2.reshape(t, ROW_TILES, LANES), dest_kt, pad_rows, n_rows)
    return xb.reshape(n_rows * ROW_TILES, LANES), blk_meta, dest_kt


def _part_batches(bsz):
    if bsz % 8 == 0:
        return tuple(share * bsz // 8 for share in MOE_PART_SHARES)
    return (bsz // 2, bsz - bsz // 2)


def _xattn_moe(h1, kv, bsz, s, w_cq, w_co, ln2_g, ln2_b, w_router, b_router, w_gu, b_gu, w_dn, b_dn, ln3_g, ln3_b):
    parts = []
    after = None
    n_mem = kv.shape[0] // bsz
    b0 = 0
    for nb in _part_batches(bsz):
        h2, idx_slab, gate_slab = _cross_attention_router(h1, kv, n_mem, b0, nb, s, w_cq, w_co, ln2_g, ln2_b,
                                                          w_router, b_router, after)
        b0 += nb
        parts.append((h2, gate_slab) + _route_and_dispatch(h2, idx_slab))
        after = parts[-1][-1]
    w_glu, w_lin = _split_gate_up(w_gu, after)
    expert_w = (w_glu, w_lin, b_gu[:, None, 0::2], b_gu[:, None, 1::2], w_dn, b_dn[:, None, :])
    out = None
    tok0 = 0
    for h2, gate_slab, xb, blk_meta, dest_kt in parts:
        n_rows = xb.shape[0] // ROW_TILES
        yb = _experts(xb, blk_meta, *expert_w)
        yg = _gather_rows(yb.reshape(n_rows, ROW_TILES, LANES), dest_kt.reshape(-1))
        out = _combine(h2, gate_slab, yg.reshape(-1, LANES), ln3_g, ln3_b, tok0, bsz * s, out)
        tok0 += h2.shape[0] // ROW_TILES
    return out


def kernel(x, mem, positions, w_in, b_igate, b_fgate, conv_w, conv_b, w_mq, w_mk, g_mhead, w_mskip, g_qlat, g_kvlat,
           w_uq, w_ukv, w_br_m, w_br_a, w_mix_out, ln1_g, ln1_b, w_cq, w_ckv, w_co, ln2_g, ln2_b, w_router, b_router,
           w_gu, b_gu, w_dn, b_dn, ln3_g, ln3_b):
    bsz, s, d = x.shape
    t = bsz * s
    h = x.reshape(t, d)
    pos2 = positions.reshape(t, 1)
    for l in range(DEPTH):
        q, k, v, xm, vm, op, gm, ga, gate = _input_projection(h, pos2, w_in[l], g_qlat[l], g_kvlat[l], w_uq[l],
                                                              w_ukv[l])
        ym = _mlstm_branch(xm, vm, op, gate, bsz, s, b_igate[l], b_fgate[l], conv_w[l], conv_b[l], w_mq[l], w_mk[l],
                           g_mhead[l], w_mskip[l])
        oa = _mla_attention(q, k, v, bsz, s)
        h1 = _merge(ym, oa, gm, ga, h, w_br_m[l], w_br_a[l], w_mix_out[l], ln1_g[l], ln1_b[l])
        kv = _mem_kv(mem.reshape(-1, d), w_ckv[l])
        h = _xattn_moe(h1, kv, bsz, s, w_cq[l], w_co[l], ln2_g[l], ln2_b[l], w_router[l], b_router[l], w_gu[l],
                       b_gu[l], w_dn[l], b_dn[l], ln3_g[l], ln3_b[l])
    return h.reshape(bsz, s, d)
```

```python
import functools

import numpy as np
import jax
import jax.numpy as jnp
from jax import lax
from jax.experimental import pallas as pl
from jax.experimental.pallas import tpu as pltpu
from jax.experimental.pallas import tpu_sc as plsc

F32 = jnp.float32
BF16 = jnp.bfloat16
I32 = jnp.int32

D_MODEL = 1024
N_MEM = 256
M_HEADS = 4
M_HEAD_DIM = 128
M_WIDTH = M_HEADS * M_HEAD_DIM
M_CONV = 4
A_HEADS = 8
A_NOPE = 64
A_ROPE = 32
A_QK = A_NOPE + A_ROPE
A_VDIM = 64
A_Q_RANK = 256
A_KV_RANK = 128
ROPE_THETA = 10000.0
X_HEADS = 4
X_HEAD_DIM = D_MODEL // X_HEADS
N_EXPERTS = 32
TOP_K = 4
D_EXPERT = D_MODEL
SWIGLU_ALPHA = 1.702
SWIGLU_LIMIT = 7.0
DEPTH = 1
DN_ALPHA = (2.0 * DEPTH) ** 0.25
EPS = 1e-5
IN_SPLITS = (A_Q_RANK, A_KV_RANK, A_ROPE, M_WIDTH, M_WIDTH, M_WIDTH, M_HEADS, M_HEADS, D_MODEL, D_MODEL)
IN_OFFSETS = tuple(int(v) for v in np.cumsum((0,) + IN_SPLITS))

LANES = 128
VMEM_LIMIT = 56 * 1024 * 1024

PROJ_ROWS = 512
M_CHUNK_ROWS = 128
MLSTM_SEQS = 2
ATT_Q = 256
MERGE_ROWS = 512
XATT_ROWS = 512
RANK_ROWS = 1024
EXPERT_ROWS = 256
COMBINE_TOKENS = 256
MOE_PART_SHARES = (5, 3)
SC_CORES = 2
SC_SUBCORES = 16
SC_WORKERS = SC_CORES * SC_SUBCORES
SC_ROWS = 32
NEG_BIG = -1e30
LOG2_E = 1.4426950408889634
ROW_TILES = D_MODEL // LANES

C_QLAT = 0
C_KVLAT = C_QLAT + A_Q_RANK
C_KR = C_KVLAT + A_KV_RANK
C_KRS = C_KR + LANES
C_GATE = C_KRS + LANES
C_XM = C_GATE + LANES
C_VM = C_XM + M_WIDTH
C_OP = C_VM + M_WIDTH
C_GM = C_OP + M_WIDTH
C_GA = C_GM + D_MODEL
C_END = C_GA + D_MODEL


def _cparams(sem, vmem=VMEM_LIMIT):
    return pltpu.CompilerParams(dimension_semantics=sem, vmem_limit_bytes=vmem)


def _const_spec(shape):
    nd = len(shape)
    return pl.BlockSpec(shape, lambda *a: (0,) * nd)


def _layer_norm_rows(v, g, b):
    mu = jnp.mean(v, axis=-1, keepdims=True)
    d = v - mu
    var = jnp.mean(d * d, axis=-1, keepdims=True)
    return d * lax.rsqrt(var + EPS) * g + b


def _proj_kernel(x_ref, pos_ref, w_ref, gq_ref, gkv_ref, wuq_ref, wuqs_ref, wuk_ref, wuv_ref, invf_ref,
                 q_ref, k_ref, v_ref, xm_ref, vm_ref, op_ref, gm_ref, ga_ref, gate_ref):
    xb = x_ref[...].astype(BF16)

    def mm(lo, n):
        return jnp.dot(xb, w_ref[:, lo:lo + n], preferred_element_type=F32)

    xm_ref[...] = mm(C_XM, M_WIDTH).astype(BF16)
    vm_ref[...] = mm(C_VM, M_WIDTH).astype(BF16)
    op_ref[...] = mm(C_OP, M_WIDTH).astype(BF16)
    gm_ref[...] = mm(C_GM, D_MODEL).astype(BF16)
    ga_ref[...] = mm(C_GA, D_MODEL).astype(BF16)
    gate_ref[...] = mm(C_GATE, LANES)

    ang = pos_ref[...].astype(F32) * invf_ref[...]
    cos = jnp.cos(ang)
    sin = jnp.sin(ang)

    q_lat = mm(C_QLAT, A_Q_RANK)
    qn = (q_lat * lax.rsqrt(jnp.mean(q_lat * q_lat, axis=-1, keepdims=True) + EPS) * gq_ref[...]).astype(BF16)
    q = jnp.dot(qn, wuq_ref[...], preferred_element_type=F32)
    qs = jnp.dot(qn, wuqs_ref[...], preferred_element_type=F32)
    kv_lat = mm(C_KVLAT, A_KV_RANK)
    kvn = (kv_lat * lax.rsqrt(jnp.mean(kv_lat * kv_lat, axis=-1, keepdims=True) + EPS) * gkv_ref[...]).astype(BF16)
    kn = jnp.dot(kvn, wuk_ref[...], preferred_element_type=F32)
    lane = lax.broadcasted_iota(I32, (1, A_HEADS * LANES), 1)
    ones_lane = (lane % LANES == A_VDIM).astype(F32)
    v_ref[...] = (jnp.dot(kvn, wuv_ref[...], preferred_element_type=F32) + ones_lane).astype(BF16)
    k_pe = mm(C_KR, LANES) * cos + mm(C_KRS, LANES) * sin
    scale = A_QK ** -0.5 * LOG2_E
    for h in range(A_HEADS):
        sl = slice(h * LANES, (h + 1) * LANES)
        q_ref[:, sl] = ((q[:, sl] * cos + qs[:, sl] * sin) * scale).astype(BF16)
        k_ref[:, sl] = (kn[:, sl] + k_pe).astype(BF16)


def _proj_weights(w_in, w_uq, w_ukv):
    o = IN_OFFSETS
    half = A_ROPE // 2
    w_q, w_kv, w_kr = w_in[:, o[0]:o[1]], w_in[:, o[1]:o[2]], w_in[:, o[2]:o[3]]
    w_xm, w_vm, w_op = w_in[:, o[3]:o[4]], w_in[:, o[4]:o[5]], w_in[:, o[5]:o[6]]
    w_i, w_f, w_gm, w_ga = w_in[:, o[6]:o[7]], w_in[:, o[7]:o[8]], w_in[:, o[8]:o[9]], w_in[:, o[9]:o[10]]
    d = w_in.shape[0]
    z = lambda n: jnp.zeros((d, n), w_in.dtype)
    kr = jnp.concatenate([z(A_NOPE), w_kr, z(LANES - A_QK)], axis=1)
    krs = jnp.concatenate([z(A_NOPE), -w_kr[:, half:], w_kr[:, :half], z(LANES - A_QK)], axis=1)
    gate = jnp.concatenate([w_i, w_f, z(LANES - 2 * M_HEADS)], axis=1)
    w_r = jnp.concatenate([w_q, w_kv, kr, krs, gate, w_xm, w_vm, w_op, w_gm, w_ga], axis=1).astype(BF16)

    uq = w_uq.reshape(A_Q_RANK, A_HEADS, A_QK)
    zq = jnp.zeros((A_Q_RANK, A_HEADS, LANES - A_QK), w_uq.dtype)
    zn = jnp.zeros((A_Q_RANK, A_HEADS, A_NOPE), w_uq.dtype)
    uq_pad = jnp.concatenate([uq, zq], axis=-1).reshape(A_Q_RANK, A_HEADS * LANES).astype(BF16)
    uqs_pad = jnp.concatenate([zn, -uq[..., A_NOPE + half:], uq[..., A_NOPE:A_NOPE + half], zq],
                              axis=-1).reshape(A_Q_RANK, A_HEADS * LANES).astype(BF16)
    ukv = w_ukv.reshape(A_KV_RANK, A_HEADS, A_NOPE + A_VDIM)
    zk = jnp.zeros((A_KV_RANK, A_HEADS, LANES - A_NOPE), w_ukv.dtype)
    uk_pad = jnp.concatenate([ukv[..., :A_NOPE], zk], axis=-1).reshape(A_KV_RANK, A_HEADS * LANES).astype(BF16)
    zv = jnp.zeros((A_KV_RANK, A_HEADS, LANES - A_VDIM), w_ukv.dtype)
    uv_pad = jnp.concatenate([ukv[..., A_NOPE:], zv], axis=-1).reshape(A_KV_RANK, A_HEADS * LANES).astype(BF16)
    return w_r, uq_pad, uqs_pad, uk_pad, uv_pad


def _input_projection(x2, pos2, w_in, g_qlat, g_kvlat, w_uq, w_ukv):
    t = x2.shape[0]
    tm = min(PROJ_ROWS, t)
    w_r, uq_pad, uqs_pad, uk_pad, uv_pad = _proj_weights(w_in, w_uq, w_ukv)
    half = A_ROPE // 2
    inv_freq = ROPE_THETA ** (-jnp.arange(half, dtype=F32) / half)
    invf = jnp.concatenate([jnp.zeros((A_NOPE,), F32), inv_freq, inv_freq,
                            jnp.zeros((LANES - A_QK,), F32)]).reshape(1, LANES)
    hw = A_HEADS * LANES
    row = lambda n: pl.BlockSpec((tm, n), lambda i: (i, 0))
    outs = [
        jax.ShapeDtypeStruct((t, hw), BF16), jax.ShapeDtypeStruct((t, hw), BF16), jax.ShapeDtypeStruct((t, hw), BF16),
        jax.ShapeDtypeStruct((t, M_WIDTH), BF16), jax.ShapeDtypeStruct((t, M_WIDTH), BF16),
        jax.ShapeDtypeStruct((t, M_WIDTH), BF16),
        jax.ShapeDtypeStruct((t, D_MODEL), BF16), jax.ShapeDtypeStruct((t, D_MODEL), BF16),
        jax.ShapeDtypeStruct((t, LANES), F32),
    ]
    return pl.pallas_call(
        _proj_kernel,
        out_shape=outs,
        grid=(t // tm,),
        in_specs=[row(D_MODEL), row(1), _const_spec(w_r.shape), _const_spec((1, A_Q_RANK)),
                  _const_spec((1, A_KV_RANK)), _const_spec(uq_pad.shape), _const_spec(uqs_pad.shape),
                  _const_spec(uk_pad.shape), _const_spec(uv_pad.shape), _const_spec((1, LANES))],
        out_specs=[row(hw), row(hw), row(hw), row(M_WIDTH), row(M_WIDTH), row(M_WIDTH), row(D_MODEL), row(D_MODEL),
                   row(LANES)],
        compiler_params=_cparams(("parallel",)),
        name="input_projection",
    )(x2, pos2, w_r, g_qlat.reshape(1, -1), g_kvlat.reshape(1, -1), uq_pad, uqs_pad, uk_pad, uv_pad, invf)


def _log_sigmoid(v):
    return jnp.minimum(v, 0.0) - jnp.log1p(jnp.exp(-jnp.abs(v)))


def _mlstm_kernel(xm_ref, vm_ref, op_ref, gate_ref, convw_ref, convb_ref, wqk_ref, gbias_ref, ghead_ref,
                  skip_ref, tril_ref, out_ref, xpad_s, ctn_s, m_s, *, seq_len):
    s = seq_len
    lc = M_CHUNK_ROWS
    halo = 8
    for bb in range(MLSTM_SEQS):
        xpad_s[bb, 0:halo, :] = jnp.zeros((halo, M_WIDTH), F32)
        xpad_s[bb, halo:, :] = xm_ref[bb * s:(bb + 1) * s, :].astype(F32)
    ctn_s[...] = jnp.zeros_like(ctn_s)
    m_s[...] = jnp.zeros_like(m_s)
    rows = lax.broadcasted_iota(I32, (lc, lc), 0)
    cols = lax.broadcasted_iota(I32, (lc, lc), 1)
    causal = rows >= cols
    row_id = lax.broadcasted_iota(I32, (lc, LANES), 0)
    ones_blk = (lax.broadcasted_iota(I32, (lc, LANES), 1) == 0).astype(BF16)
    kscale = M_HEAD_DIM ** -0.5
    hsl = [slice(h * M_HEAD_DIM, (h + 1) * M_HEAD_DIM) for h in range(M_HEADS)]
    hcl = [slice(h, h + 1) for h in range(M_HEADS)]
    streams = [(bb, h) for bb in range(MLSTM_SEQS) for h in range(M_HEADS)]

    def gates(bb, r0):
        win = xpad_s[bb, pl.ds(r0 - bb * s, lc + halo), :]
        conv = convb_ref[...]
        for j in range(M_CONV):
            lo = halo - (M_CONV - 1) + j
            conv = conv + win[lo:lo + lc, :] * convw_ref[j:j + 1, :]
        xc = conv * jax.nn.sigmoid(conv)
        g = gate_ref[pl.ds(r0, lc), :] + gbias_ref[...]
        ls = _log_sigmoid(g)
        bc = jnp.dot(tril_ref[...], ls, preferred_element_type=F32, precision=lax.Precision.HIGHEST)
        b0 = pltpu.roll(bc, LANES - M_HEADS, axis=1)
        u = g - b0
        cm = u
        shift = 1
        while shift < lc:
            cm = jnp.maximum(cm, jnp.where(row_id >= shift, pltpu.roll(cm, shift, axis=0), -jnp.inf))
            shift *= 2
        m_prev = m_s[bb:bb + 1, :]
        g_inter = b0 + m_prev
        m = jnp.maximum(g_inter, b0 + cm)
        b_tot = b0[lc - 1:lc, :]
        aw = b_tot - b0 + g
        m_chunk = jnp.max(aw, axis=0, keepdims=True)
        m_new = jnp.maximum(b_tot + m_prev, m_chunk)
        m_s[bb:bb + 1, :] = m_new
        return dict(xc=xc, w_inter=jnp.exp(g_inter - m), e_negm=jnp.exp(-m), a_mat=b0 - m, wa=jnp.exp(aw - m_chunk),
                    s_old=jnp.exp(b_tot + m_prev - m_new), s_new=jnp.exp(m_chunk - m_new), u_t=u.T)

    def chunk(c, carry):
        r0s = [pl.multiple_of(bb * s + c * lc, lc) for bb in range(MLSTM_SEQS)]
        gs = [gates(bb, r0s[bb]) for bb in range(MLSTM_SEQS)]
        qk = [jnp.dot(gs[bb]["xc"].astype(BF16), wqk_ref[...], preferred_element_type=F32)
              for bb in range(MLSTM_SEQS)]
        q_b = [qk[bb][:, hsl[h]].astype(BF16) for bb, h in streams]
        k_f = [qk[bb][:, M_WIDTH + h * M_HEAD_DIM:M_WIDTH + (h + 1) * M_HEAD_DIM] * kscale for bb, h in streams]
        v_aug = [jnp.concatenate([vm_ref[pl.ds(r0s[bb], lc), hsl[h]], ones_blk], axis=1) for bb, h in streams]
        ctn_prev = [ctn_s[bb * M_HEADS + h] for bb, h in streams]
        s_raw = [lax.dot_general(q_b[i], k_f[i].astype(BF16), (((1,), (1,)), ((), ())), preferred_element_type=F32)
                 for i in range(len(streams))]
        inter = [jnp.dot(q_b[i], ctn_prev[i].astype(BF16), preferred_element_type=F32)
                 for i in range(len(streams))]
        ctn_c = [lax.dot_general((k_f[i] * gs[bb]["wa"][:, hcl[h]]).astype(BF16), v_aug[i], (((0,), (0,)), ((), ())),
                                 preferred_element_type=F32) for i, (bb, h) in enumerate(streams)]
        sc_b = [(s_raw[i] * jnp.exp(jnp.where(causal, gs[bb]["a_mat"][:, hcl[h]] + gs[bb]["u_t"][hcl[h], :],
                                              -jnp.inf))).astype(BF16) for i, (bb, h) in enumerate(streams)]
        intra = [jnp.dot(sc_b[i], v_aug[i], preferred_element_type=F32) for i in range(len(streams))]
        for i, (bb, h) in enumerate(streams):
            g = gs[bb]
            ctn_s[bb * M_HEADS + h] = g["s_old"][:, hcl[h]] * ctn_prev[i] + g["s_new"][:, hcl[h]] * ctn_c[i]
            wi = g["w_inter"][:, hcl[h]]
            num = wi * inter[i][:, :M_HEAD_DIM] + intra[i][:, :M_HEAD_DIM]
            den = wi * inter[i][:, M_HEAD_DIM:M_HEAD_DIM + 1] + intra[i][:, M_HEAD_DIM:M_HEAD_DIM + 1]
            hh = num / jnp.maximum(jnp.abs(den), g["e_negm"][:, hcl[h]])
            mu = jnp.mean(hh, axis=-1, keepdims=True)
            dv = hh - mu
            var = jnp.mean(dv * dv, axis=-1, keepdims=True)
            hn = dv * lax.rsqrt(var + EPS) * ghead_ref[:, hsl[h]]
            y = (jax.nn.sigmoid(op_ref[pl.ds(r0s[bb], lc), hsl[h]].astype(F32))
                 * (hn + skip_ref[:, hsl[h]] * g["xc"][:, hsl[h]]))
            out_ref[pl.ds(r0s[bb], lc), hsl[h]] = y.astype(BF16)
        return carry

    lax.fori_loop(0, s // lc, chunk, 0)


def _mlstm_branch(xm, vm, op, gate, bsz, s, b_i, b_f, conv_w, conv_b, w_mq, w_mk, g_mhead, w_mskip):
    lc = M_CHUNK_ROWS
    gbias = jnp.concatenate([b_i, b_f, jnp.zeros((LANES - 2 * M_HEADS,), F32)]).reshape(1, LANES)
    tril = jnp.tril(jnp.ones((lc, lc), F32))
    eye = jnp.eye(M_HEADS, dtype=F32)
    block_diag = lambda w: jnp.einsum('hde,hg->hdge', w, eye).reshape(M_WIDTH, M_WIDTH)
    w_qk = jnp.concatenate([block_diag(w_mq), block_diag(w_mk)], axis=1).astype(BF16)
    nseq = MLSTM_SEQS
    assert bsz % nseq == 0
    seq = lambda n: pl.BlockSpec((nseq * s, n), lambda b: (b, 0))
    return pl.pallas_call(
        functools.partial(_mlstm_kernel, seq_len=s),
        out_shape=jax.ShapeDtypeStruct((bsz * s, M_WIDTH), BF16),
        grid=(bsz // nseq,),
        in_specs=[seq(M_WIDTH), seq(M_WIDTH), seq(M_WIDTH), seq(LANES),
                  _const_spec((M_CONV, M_WIDTH)), _const_spec((1, M_WIDTH)),
                  _const_spec((M_WIDTH, 2 * M_WIDTH)),
                  _const_spec((1, LANES)), _const_spec((1, M_WIDTH)), _const_spec((1, M_WIDTH)),
                  _const_spec((lc, lc))],
        out_specs=seq(M_WIDTH),
        scratch_shapes=[pltpu.VMEM((nseq, s + 8, M_WIDTH), F32),
                        pltpu.VMEM((nseq * M_HEADS, M_HEAD_DIM, 2 * LANES), F32), pltpu.VMEM((8, LANES), F32)],
        compiler_params=_cparams(("parallel",)),
        name="mlstm_branch",
    )(xm, vm, op, gate, conv_w, conv_b.reshape(1, -1), w_qk, gbias, g_mhead.reshape(1, -1), w_mskip.reshape(1, -1),
      tril)


def _attn_kernel(q_ref, k_ref, v_ref, o_ref, s_scr, m_s, acc_s):
    tq = q_ref.shape[0]
    i = pl.program_id(1)
    rows = lax.broadcasted_iota(I32, (tq, tq), 0)
    cols = lax.broadcasted_iota(I32, (tq, tq), 1)

    def lane_tile_max(sc):
        out = sc[:, :LANES]
        for t in range(1, tq // LANES):
            out = jnp.maximum(out, sc[:, t * LANES:(t + 1) * LANES])
        return out

    heads = [slice(h * LANES, (h + 1) * LANES) for h in range(A_HEADS)]
    m_s[...] = jnp.full(m_s.shape, NEG_BIG, F32)
    acc_s[...] = jnp.zeros_like(acc_s)

    def pass_a(j, diagonal):
        k0 = pl.multiple_of(j * tq, tq)
        for h, hs in enumerate(heads):
            sc = lax.dot_general(q_ref[:, hs], k_ref[pl.ds(k0, tq), hs], (((1,), (1,)), ((), ())),
                                 preferred_element_type=F32)
            if diagonal:
                sc = jnp.where(cols <= rows, sc, NEG_BIG)
            s_scr[h, j] = sc
            m_s[h] = jnp.maximum(m_s[h], lane_tile_max(sc))

    def for_blocks(n, fn):
        def quad(p, c):
            for u in range(4):
                fn(4 * p + u)
            return c

        lax.fori_loop(0, n // 4, quad, 0)
        rem = n % 4

        @pl.when(rem >= 2)
        def _():
            fn(n - rem)
            fn(n - rem + 1)

        @pl.when(rem % 2 == 1)
        def _():
            fn(n - 1)

    for_blocks(i, lambda j: pass_a(j, False))
    pass_a(i, True)
    for h in range(A_HEADS):
        m_s[h] = jnp.broadcast_to(jnp.max(m_s[h], axis=1, keepdims=True), (tq, LANES))

    def pass_b(j):
        k0 = pl.multiple_of(j * tq, tq)
        for h, hs in enumerate(heads):
            m_row = m_s[h]
            p = jnp.exp2(s_scr[h, j] - jnp.concatenate([m_row] * (tq // LANES), axis=1)).astype(BF16)
            acc_s[h] += jnp.dot(p, v_ref[pl.ds(k0, tq), hs], preferred_element_type=F32)

    for_blocks(i + 1, pass_b)
    for h, hs in enumerate(heads):
        acc = acc_s[h]
        o_ref[:, hs] = (acc / acc[:, A_VDIM:A_VDIM + 1]).astype(BF16)


def _mla_attention(q, k, v, bsz, s):
    tq = min(ATT_Q, s)
    nq = s // tq
    hw = A_HEADS * LANES
    return pl.pallas_call(
        _attn_kernel,
        out_shape=jax.ShapeDtypeStruct(q.shape, BF16),
        grid=(bsz, nq),
        in_specs=[pl.BlockSpec((tq, hw), lambda b, i: (b * nq + i, 0)),
                  pl.BlockSpec((s, hw), lambda b, i: (b, 0)),
                  pl.BlockSpec((s, hw), lambda b, i: (b, 0))],
        out_specs=pl.BlockSpec((tq, hw), lambda b, i: (b * nq + i, 0)),
        scratch_shapes=[pltpu.VMEM((A_HEADS, nq, tq, tq), F32), pltpu.VMEM((A_HEADS, tq, LANES), F32),
                        pltpu.VMEM((A_HEADS, tq, LANES), F32)],
        compiler_params=_cparams(("parallel", "arbitrary")),
        name="mla_attention",
    )(q, k, v)


def _merge_kernel(ym_ref, oa_ref, gm_ref, ga_ref, x_ref, wbm_ref, wba_ref, wmix_ref, g_ref, b_ref, h_ref):
    y_m = jnp.dot(ym_ref[...], wbm_ref[...], preferred_element_type=F32)
    y_a = jnp.dot(oa_ref[...], wba_ref[...], preferred_element_type=F32)
    mixed = jax.nn.sigmoid(gm_ref[...].astype(F32)) * y_m + jax.nn.sigmoid(ga_ref[...].astype(F32)) * y_a
    mix = jnp.dot(mixed.astype(BF16), wmix_ref[...], preferred_element_type=F32)
    h_ref[...] = _layer_norm_rows(DN_ALPHA * x_ref[...] + mix, g_ref[...], b_ref[...])


def _merge(ym, oa, gm, ga, x2, w_br_m, w_br_a, w_mix_out, ln_g, ln_b):
    t = x2.shape[0]
    tm = min(MERGE_ROWS, t)
    wba = jnp.concatenate([w_br_a.reshape(A_HEADS, A_VDIM, D_MODEL),
                           jnp.zeros((A_HEADS, LANES - A_VDIM, D_MODEL), w_br_a.dtype)], axis=1)
    wba = wba.reshape(A_HEADS * LANES, D_MODEL).astype(BF16)
    row = lambda n: pl.BlockSpec((tm, n), lambda i: (i, 0))
    return pl.pallas_call(
        _merge_kernel,
        out_shape=jax.ShapeDtypeStruct((t, D_MODEL), F32),
        grid=(t // tm,),
        in_specs=[row(M_WIDTH), row(A_HEADS * LANES), row(D_MODEL), row(D_MODEL), row(D_MODEL),
                  _const_spec((M_WIDTH, D_MODEL)), _const_spec((A_HEADS * LANES, D_MODEL)),
                  _const_spec((D_MODEL, D_MODEL)), _const_spec((1, D_MODEL)), _const_spec((1, D_MODEL))],
        out_specs=row(D_MODEL),
        compiler_params=_cparams(("parallel",)),
        name="merge_deepnorm1",
    )(ym, oa, gm, ga, x2, w_br_m.astype(BF16), wba, w_mix_out.astype(BF16), ln_g.reshape(1, -1), ln_b.reshape(1, -1))


def _kv_kernel(m_ref, w_ref, o_ref):
    o_ref[...] = jnp.dot(m_ref[...].astype(BF16), w_ref[...], preferred_element_type=F32).astype(BF16)


def _mem_kv(mem2, w_ckv):
    r = mem2.shape[0]
    tm = min(512, r)
    return pl.pallas_call(
        _kv_kernel,
        out_shape=jax.ShapeDtypeStruct((r, 2 * D_MODEL), BF16),
        grid=(r // tm,),
        in_specs=[pl.BlockSpec((tm, D_MODEL), lambda i: (i, 0)), _const_spec((D_MODEL, 2 * D_MODEL))],
        out_specs=pl.BlockSpec((tm, 2 * D_MODEL), lambda i: (i, 0)),
        compiler_params=_cparams(("parallel",)),
        name="memory_kv",
    )(mem2, w_ckv.astype(BF16))


def _xattn_kernel(h_ref, kv_ref, wq_ref, wo_ref, g_ref, b_ref, wr_ref, br_ref, *rest):
    h2_ref, idx_ref, gate_ref = rest[-3:]
    h1 = h_ref[...]
    q = jnp.dot(h1.astype(BF16), wq_ref[...], preferred_element_type=F32).astype(BF16)
    scale = X_HEAD_DIM ** -0.5
    ksl = [slice(hd * X_HEAD_DIM, (hd + 1) * X_HEAD_DIM) for hd in range(X_HEADS)]
    vsl = [slice(D_MODEL + hd * X_HEAD_DIM, D_MODEL + (hd + 1) * X_HEAD_DIM) for hd in range(X_HEADS)]
    scs = [lax.dot_general(q[:, ks], kv_ref[:, ks], (((1,), (1,)), ((), ())), preferred_element_type=F32) * scale
           for ks in ksl]
    ps = [jnp.exp(sc - jnp.max(sc, axis=1, keepdims=True)) for sc in scs]
    ps = [(p / jnp.sum(p, axis=1, keepdims=True)).astype(BF16) for p in ps]
    o = jnp.concatenate([jnp.dot(p, kv_ref[:, vs], preferred_element_type=F32).astype(BF16)
                         for p, vs in zip(ps, vsl)], axis=1)
    att = jnp.dot(o, wo_ref[...], preferred_element_type=F32)
    h2 = _layer_norm_rows(DN_ALPHA * h1 + att, g_ref[...], b_ref[...])
    for c in range(ROW_TILES):
        h2_ref[pl.ds(c, h2.shape[0], stride=ROW_TILES), :] = h2[:, c * LANES:(c + 1) * LANES]

    h2_hi = h2.astype(BF16)
    h2_lo = (h2 - h2_hi.astype(F32)).astype(BF16)
    hi_prod = jnp.dot(h2_hi, wr_ref[...], preferred_element_type=F32)
    lo_prod = jnp.dot(h2_lo, wr_ref[:, :LANES], preferred_element_type=F32)
    logits = hi_prod[:, :LANES] + (hi_prod[:, LANES:] + lo_prod) + br_ref[...]
    tm = logits.shape[0]
    lane = lax.broadcasted_iota(I32, (tm, LANES), 1)
    lane_f = lane.astype(F32)
    work = jnp.where(lane < N_EXPERTS, logits, -jnp.inf)
    vals, idxs = [], []
    for _ in range(TOP_K):
        mx = jnp.max(work, axis=1, keepdims=True)
        ix = jnp.min(jnp.where(work == mx, lane_f, float(LANES)), axis=1, keepdims=True)
        vals.append(mx)
        idxs.append(ix)
        work = jnp.where(lane_f == ix, -jnp.inf, work)
    es = [jnp.exp(vv - vals[0]) for vv in vals]
    tot = es[0] + es[1] + es[2] + es[3]
    idx_slab = jnp.zeros((tm, LANES), F32)
    gate_slab = jnp.zeros((tm, LANES), F32)
    for kk in range(TOP_K):
        idx_slab = jnp.where(lane == kk, idxs[kk], idx_slab)
        gate_slab = jnp.where(lane == kk, es[kk] / tot, gate_slab)
    idx_ref[...] = idx_slab.astype(I32)
    gate_ref[...] = gate_slab


def _cross_attention_router(h1, kv, n_mem, b0, bsz, s, w_cq, w_co, ln_g, ln_b, w_router, b_router, after):
    t = bsz * s
    tm = min(XATT_ROWS, s)
    ns = s // tm
    wr = jnp.concatenate([w_router, jnp.zeros((D_MODEL, LANES - N_EXPERTS), F32)], axis=1)
    wr_hi = wr.astype(BF16)
    wr = jnp.concatenate([wr_hi, (wr - wr_hi.astype(F32)).astype(BF16)], axis=1)
    br = jnp.concatenate([b_router, jnp.zeros((LANES - N_EXPERTS,), F32)]).reshape(1, LANES)
    row = lambda n: pl.BlockSpec((tm, n), lambda b, i: (b * ns + i, 0))
    h1_rows = pl.BlockSpec((tm, D_MODEL), lambda b, i: ((b0 + b) * ns + i, 0))
    in_specs = [h1_rows, pl.BlockSpec((n_mem, 2 * D_MODEL), lambda b, i: (b0 + b, 0)),
                _const_spec((D_MODEL, D_MODEL)), _const_spec((D_MODEL, D_MODEL)),
                _const_spec((1, D_MODEL)), _const_spec((1, D_MODEL)),
                _const_spec((D_MODEL, 2 * LANES)), _const_spec((1, LANES))]
    args = [h1, kv, w_cq.astype(BF16), w_co.astype(BF16), ln_g.reshape(1, -1), ln_b.reshape(1, -1), wr, br]
    if after is not None:
        in_specs.append(pl.BlockSpec(memory_space=pl.ANY))
        args.append(after)
    return pl.pallas_call(
        _xattn_kernel,
        out_shape=[jax.ShapeDtypeStruct((t * ROW_TILES, LANES), F32), jax.ShapeDtypeStruct((t, LANES), I32),
                   jax.ShapeDtypeStruct((t, LANES), F32)],
        grid=(bsz, ns),
        in_specs=in_specs,
        out_specs=[pl.BlockSpec((tm * ROW_TILES, LANES), lambda b, i: (b * ns + i, 0)), row(LANES), row(LANES)],
        compiler_params=_cparams(("parallel", "parallel")),
        name="cross_attention_router",
    )(*args)


def _rank_kernel(idx_ref, ltri_ref, rank_ref, cnt_ref, carry_s):
    @pl.when(pl.program_id(0) == 0)
    def _():
        carry_s[...] = jnp.zeros_like(carry_s)

    idx = idx_ref[...]
    tm = idx.shape[0]
    lane = lax.broadcasted_iota(I32, (tm, LANES), 1)
    onehots = [(lane == idx[:, kk:kk + 1]).astype(F32) for kk in range(TOP_K)]
    sel = onehots[0] + onehots[1] + onehots[2] + onehots[3]
    before = jnp.dot(ltri_ref[...], sel.astype(BF16), preferred_element_type=F32) + carry_s[0:1, :]
    rank_slab = jnp.zeros((tm, LANES), F32)
    for kk in range(TOP_K):
        r = jnp.sum(onehots[kk] * before, axis=1, keepdims=True)
        rank_slab = jnp.where(lane == kk, r, rank_slab)
    rank_ref[...] = rank_slab.astype(I32)
    carry_s[0:1, :] = carry_s[0:1, :] + jnp.sum(sel, axis=0, keepdims=True)
    cnt_ref[...] = jnp.broadcast_to(carry_s[0:1, :], cnt_ref.shape).astype(I32)


def _routing_ranks(idx_slab):
    t = idx_slab.shape[0]
    tm = min(RANK_ROWS, t)
    assert t % tm == 0
    ltri = jnp.tril(jnp.ones((tm, tm), BF16), k=-1)
    return pl.pallas_call(
        _rank_kernel,
        out_shape=[jax.ShapeDtypeStruct((t, LANES), I32), jax.ShapeDtypeStruct((8, LANES), I32)],
        grid=(t // tm,),
        in_specs=[pl.BlockSpec((tm, LANES), lambda i: (i, 0)), _const_spec((tm, tm))],
        out_specs=[pl.BlockSpec((tm, LANES), lambda i: (i, 0)), _const_spec((8, LANES))],
        scratch_shapes=[pltpu.VMEM((8, LANES), F32)],
        compiler_params=_cparams(("arbitrary",)),
        name="routing_ranks",
    )(idx_slab, ltri)


def _dest_kernel(idx_ref, rank_ref, pstart_ref, dest_ref):
    idx = idx_ref[...]
    tm = idx.shape[0]
    lane = lax.broadcasted_iota(I32, (tm, LANES), 1)
    dest = rank_ref[...].astype(F32)
    pstart = pstart_ref[0:1, :].astype(F32)
    for kk in range(TOP_K):
        start = jnp.sum(jnp.where(lane == idx[:, kk:kk + 1], pstart, 0.0), axis=1, keepdims=True)
        dest = dest + jnp.where(lane == kk, start, 0.0)
    dest_ref[...] = dest.T[0:8, :].astype(I32)


def _dest_rows(idx_slab, rank_slab, pstart_row):
    t = rank_slab.shape[0]
    tm = min(RANK_ROWS, t)
    return pl.pallas_call(
        _dest_kernel,
        out_shape=jax.ShapeDtypeStruct((8, t), I32),
        grid=(t // tm,),
        in_specs=[pl.BlockSpec((tm, LANES), lambda i: (i, 0)), pl.BlockSpec((tm, LANES), lambda i: (i, 0)),
                  _const_spec((8, LANES))],
        out_specs=pl.BlockSpec((8, tm), lambda i: (0, i)),
        compiler_params=_cparams(("parallel",)),
        name="routing_dest",
    )(idx_slab, rank_slab, pstart_row)


def _sc_mesh():
    return plsc.VectorSubcoreMesh(core_axis_name="c", subcore_axis_name="s", num_cores=SC_CORES,
                                  num_subcores=SC_SUBCORES)


def _sc_worker():
    return lax.axis_index("s") * SC_CORES + lax.axis_index("c")


def _dispatch(h2, dest_kt, pad_rows, n_rows):
    t = dest_kt.shape[1]
    w = SC_ROWS
    per_w = t // SC_WORKERS
    steps = per_w // w
    assert steps % 2 == 0 and steps * w * SC_WORKERS == t
    pad_steps = pad_rows.shape[0] // (SC_WORKERS * w)
    dest_w = dest_kt.reshape(TOP_K, SC_WORKERS, steps, w)
    pad_w = pad_rows.reshape(SC_WORKERS, pad_steps, w)
    zeros = jnp.zeros((w, ROW_TILES, LANES), F32)

    def body(x_hbm, dest_hbm, pad_hbm, zeros_hbm, o_hbm, idx_v, pad_v, rows_v, sem_in, sem_out):
        wid = _sc_worker()
        tok0 = wid * per_w
        for kk in range(TOP_K):
            pltpu.sync_copy(dest_hbm.at[kk, wid], idx_v.at[pl.ds(kk * steps, steps)])
        pltpu.sync_copy(pad_hbm.at[wid], pad_v)
        pltpu.sync_copy(zeros_hbm, rows_v.at[0])
        for j in range(pad_steps):
            pltpu.sync_copy(rows_v.at[0], o_hbm.at[pad_v.at[j]])

        def load(step, b):
            return pltpu.make_async_copy(x_hbm.at[pl.ds(tok0 + step * w, w)], rows_v.at[b], sem_in.at[b])

        def scatter(step, kk, b):
            return pltpu.make_async_copy(rows_v.at[b], o_hbm.at[idx_v.at[kk * steps + step]], sem_out.at[b])

        load(0, 0).start()

        @pl.loop(0, steps, step=2)
        def _(s0):
            for b in range(2):
                step = s0 + b
                load(step, b).wait()

                @pl.when(step >= 1)
                def _():
                    for kk in range(TOP_K):
                        scatter(step - 1, kk, 1 - b).wait()

                @pl.when(step + 1 < steps)
                def _():
                    load(step + 1, 1 - b).start()

                for kk in range(TOP_K):
                    scatter(step, kk, b).start()

        for kk in range(TOP_K):
            scatter(steps - 1, kk, 1).wait()

    return pl.kernel(
        body,
        out_type=jax.ShapeDtypeStruct((n_rows, ROW_TILES, LANES), F32),
        mesh=_sc_mesh(),
        scratch_types=[pltpu.VMEM((steps * TOP_K, w), I32), pltpu.VMEM((pad_steps, w), I32),
                       pltpu.VMEM((2, w, ROW_TILES, LANES), F32), pltpu.SemaphoreType.DMA((2,)),
                       pltpu.SemaphoreType.DMA((2,))],
        name="moe_dispatch_sc",
    )(h2, dest_w, pad_w, zeros)


def _gather_rows(yb, dest_kt):
    n = dest_kt.shape[0]
    w = SC_ROWS
    per_w = n // SC_WORKERS
    steps = per_w // w
    assert steps % 2 == 0 and steps * w * SC_WORKERS == n
    dest_w = dest_kt.reshape(SC_WORKERS, steps, w)

    def body(y_hbm, dest_hbm, o_hbm, idx_v, rows_v, sem_g, sem_w):
        wid = _sc_worker()
        row0 = wid * per_w
        pltpu.sync_copy(dest_hbm.at[wid], idx_v)

        def gather(step, b):
            return pltpu.make_async_copy(y_hbm.at[idx_v.at[step]], rows_v.at[b], sem_g.at[b])

        def write(step, b):
            return pltpu.make_async_copy(rows_v.at[b], o_hbm.at[pl.ds(row0 + step * w, w)], sem_w.at[b])

        gather(0, 0).start()

        @pl.loop(0, steps, step=2)
        def _(s0):
            for b in range(2):
                step = s0 + b
                gather(step, b).wait()

                @pl.when(step >= 1)
                def _():
                    write(step - 1, 1 - b).wait()

                @pl.when(step + 1 < steps)
                def _():
                    gather(step + 1, 1 - b).start()

                write(step, b).start()

        write(steps - 1, 1).wait()

    return pl.kernel(
        body,
        out_type=jax.ShapeDtypeStruct((n, ROW_TILES, LANES), F32),
        mesh=_sc_mesh(),
        scratch_types=[pltpu.VMEM((steps, w), I32), pltpu.VMEM((2, w, ROW_TILES, LANES), F32),
                       pltpu.SemaphoreType.DMA((2,)), pltpu.SemaphoreType.DMA((2,))],
        name="moe_gather_sc",
    )(yb, dest_w)


def _rows_2d(ref, n):
    return jnp.concatenate([ref[pl.ds(c, n, stride=ROW_TILES), :] for c in range(ROW_TILES)], axis=1)


def _store_rows(ref, val):
    for c in range(ROW_TILES):
        ref[pl.ds(c, val.shape[0], stride=ROW_TILES), :] = val[:, c * LANES:(c + 1) * LANES]


def _expert_kernel(blk_e_ref, first_ref, slot_ref, next_ref, nblk_ref, x_ref, wg_hbm, wl_hbm, bg_ref, bl_ref, wd_hbm,
                   bd_ref, y_ref, wg_buf, wl_buf, wd_stage, wd_buf, sem):
    i = pl.program_id(0)

    def fetch(e, sl):
        return [pltpu.make_async_copy(src.at[e], dst.at[sl], sem.at[sl, j])
                for j, (src, dst) in enumerate(((wg_hbm, wg_buf), (wl_hbm, wl_buf), (wd_hbm, wd_stage)))]

    @pl.when(i < nblk_ref[0])
    def _():
        sl = slot_ref[i]

        @pl.when(first_ref[i] == 1)
        def _():
            @pl.when(i == 0)
            def _():
                for cp in fetch(blk_e_ref[0], sl):
                    cp.start()

            for cp in fetch(blk_e_ref[i], sl):
                cp.wait()
            wd_buf[sl] = wd_stage[sl].astype(BF16)

            @pl.when(next_ref[i] >= 0)
            def _():
                for cp in fetch(next_ref[i], 1 - sl):
                    cp.start()

        xb = _rows_2d(x_ref, y_ref.shape[0] // ROW_TILES).astype(BF16)
        glu = jnp.dot(xb, wg_buf[sl], preferred_element_type=F32) + bg_ref[...]
        lin = jnp.dot(xb, wl_buf[sl], preferred_element_type=F32) + bl_ref[...]
        glu = jnp.minimum(glu, SWIGLU_LIMIT)
        lin = jnp.clip(lin, -SWIGLU_LIMIT, SWIGLU_LIMIT)
        act = glu * jax.nn.sigmoid(SWIGLU_ALPHA * glu) * (lin + 1.0)
        _store_rows(y_ref, jnp.dot(act.astype(BF16), wd_buf[sl], preferred_element_type=F32) + bd_ref[...])

    @pl.when(i >= nblk_ref[0])
    def _():
        y_ref[...] = jnp.zeros_like(y_ref)


def _split_kernel(w_ref, perm_ref, after_ref, g_ref, l_ref):
    del after_ref
    half = g_ref.shape[-1]
    sorted_cols = jnp.dot(w_ref[...].astype(BF16), perm_ref[...], preferred_element_type=F32)
    g_ref[...] = sorted_cols[:, :half].astype(BF16)
    l_ref[...] = sorted_cols[:, half:].astype(BF16)


def _split_gate_up(w_gu, after):
    e, d, n2 = w_gu.shape
    cw = 512
    src = np.concatenate([np.arange(0, cw, 2), np.arange(1, cw, 2)])
    perm = jnp.asarray(np.eye(cw, dtype=np.float32)[:, src], BF16)
    return pl.pallas_call(
        _split_kernel,
        out_shape=[jax.ShapeDtypeStruct((e, d, n2 // 2), BF16), jax.ShapeDtypeStruct((e, d, n2 // 2), BF16)],
        grid=(e, n2 // cw),
        in_specs=[pl.BlockSpec((None, d, cw), lambda i, c: (i, 0, c)), _const_spec((cw, cw)),
                  pl.BlockSpec(memory_space=pl.ANY)],
        out_specs=[pl.BlockSpec((None, d, cw // 2), lambda i, c: (i, 0, c)),
                   pl.BlockSpec((None, d, cw // 2), lambda i, c: (i, 0, c))],
        compiler_params=_cparams(("parallel", "parallel")),
        name="split_gate_up",
    )(w_gu, perm, after)


def _experts(xb, blk_meta, w_glu, w_lin, b_glu, b_lin, w_dn, b_dn):
    n_rows = xb.shape[0] // ROW_TILES
    bm = EXPERT_ROWS
    vec = lambda n: pl.BlockSpec((None, 1, n), lambda i, be, *_: (be[i], 0, 0))
    rows = pl.BlockSpec((bm * ROW_TILES, LANES), lambda i, *_: (i, 0))
    hbm = pl.BlockSpec(memory_space=pl.ANY)
    grid_spec = pltpu.PrefetchScalarGridSpec(
        num_scalar_prefetch=5,
        grid=(n_rows // bm,),
        in_specs=[rows, hbm, hbm, vec(D_EXPERT), vec(D_EXPERT), hbm, vec(D_MODEL)],
        out_specs=rows,
        scratch_shapes=[pltpu.VMEM((2, D_MODEL, D_EXPERT), BF16), pltpu.VMEM((2, D_MODEL, D_EXPERT), BF16),
                        pltpu.VMEM((2, D_EXPERT, D_MODEL), F32), pltpu.VMEM((2, D_EXPERT, D_MODEL), BF16),
                        pltpu.SemaphoreType.DMA((2, 3))],
    )
    return pl.pallas_call(
        _expert_kernel,
        out_shape=jax.ShapeDtypeStruct((n_rows * ROW_TILES, LANES), F32),
        grid_spec=grid_spec,
        compiler_params=_cparams(("arbitrary",)),
        name="moe_experts",
    )(*blk_meta, xb, w_glu, w_lin, b_glu, b_lin, w_dn, b_dn)


def _combine_kernel(h_ref, gate_ref, y0_ref, y1_ref, y2_ref, y3_ref, g_ref, b_ref, *rest):
    o_ref = rest[-1]
    tt = o_ref.shape[0]
    gates = gate_ref[...]
    acc = DN_ALPHA * _rows_2d(h_ref, tt)
    for kk, y_ref in enumerate((y0_ref, y1_ref, y2_ref, y3_ref)):
        acc = acc + gates[:, kk:kk + 1] * _rows_2d(y_ref, tt)
    o_ref[...] = _layer_norm_rows(acc, g_ref[...], b_ref[...])


def _combine(h2, gate_slab, yg, ln_g, ln_b, tok0, t_all, prev_out):
    t = h2.shape[0] // ROW_TILES
    tt = min(COMBINE_TOKENS, t)
    steps = t // tt
    assert tok0 % tt == 0 and t % tt == 0
    first = tok0 // tt
    row = lambda m: pl.BlockSpec((tt, m), lambda i: (i, 0))
    tiles = lambda kk: pl.BlockSpec((tt * ROW_TILES, LANES), lambda i: (kk * steps + i, 0))
    in_specs = [tiles(0), row(LANES),
                tiles(0), tiles(1), tiles(2), tiles(3), _const_spec((1, D_MODEL)), _const_spec((1, D_MODEL))]
    args = [h2, gate_slab, yg, yg, yg, yg, ln_g.reshape(1, -1), ln_b.reshape(1, -1)]
    aliases = {}
    if prev_out is not None:
        in_specs.append(pl.BlockSpec(memory_space=pl.ANY))
        args.append(prev_out)
        aliases = {len(args) - 1: 0}
    return pl.pallas_call(
        _combine_kernel,
        out_shape=jax.ShapeDtypeStruct((t_all, D_MODEL), F32),
        grid=(steps,),
        in_specs=in_specs,
        out_specs=pl.BlockSpec((tt, D_MODEL), lambda i: (first + i, 0)),
        input_output_aliases=aliases,
        compiler_params=_cparams(("parallel",)),
        name="moe_combine",
    )(*args)


def _route_and_dispatch(h2, idx_slab):
    t = h2.shape[0] // ROW_TILES
    bm = EXPERT_ROWS
    n_rows = t * TOP_K + N_EXPERTS * bm
    n_blocks = n_rows // bm

    rank_slab, cnt = _routing_ranks(idx_slab)
    counts = cnt[0, :N_EXPERTS]
    padded = (counts + bm - 1) // bm * bm
    pends = jnp.cumsum(padded)
    pstarts = pends - padded
    pstart_row = jnp.zeros((8, LANES), I32).at[:, :N_EXPERTS].set(pstarts[None, :])
    blk_start = jnp.arange(n_blocks, dtype=I32) * bm
    blk_e = jnp.minimum(jnp.sum((pends[None, :] <= blk_start[:, None]).astype(I32), axis=1), N_EXPERTS - 1)
    nblk = (pends[-1] // bm).astype(I32).reshape(1)
    active = counts > 0
    ordinal = jnp.cumsum(active.astype(I32)) - 1
    eid = jnp.arange(N_EXPERTS, dtype=I32)
    later = active[None, :] & (eid[None, :] > eid[:, None])
    next_active = jnp.where(later.any(axis=1), jnp.argmax(later, axis=1).astype(I32), -1)
    blk_first = jnp.concatenate([jnp.ones((1,), I32), (blk_e[1:] != blk_e[:-1]).astype(I32)])
    blk_onehot = (blk_e[:, None] == eid[None, :]).astype(I32)
    blk_slot = jnp.sum(blk_onehot * ordinal[None, :], axis=1) % 2
    blk_next = jnp.sum(blk_onehot * next_active[None, :], axis=1)
    blk_meta = (blk_e, blk_first, blk_slot, blk_next, nblk)
    dest_kt = _dest_rows(idx_slab, rank_slab, pstart_row)[:TOP_K]
    pad = padded - counts
    spare_base = pends[-1] + jnp.cumsum(bm - pad) - (bm - pad)
    pad_off = jnp.arange(bm, dtype=I32)[None, :]
    pad_rows = jnp.where(pad_off < pad[:, None], (pstarts + counts)[:, None] + pad_off,
                         spare_base[:, None] + pad_off - pad[:, None]).reshape(-1)
    xb = _dispatch(h2.reshape(t, ROW_TILES, LANES), dest_kt, pad_rows, n_rows)
    return xb.reshape(n_rows * ROW_TILES, LANES), blk_meta, dest_kt


def _part_batches(bsz):
    if bsz % 8 == 0:
        return tuple(share * bsz // 8 for share in MOE_PART_SHARES)
    return (bsz // 2, bsz - bsz // 2)


def _xattn_moe(h1, kv, bsz, s, w_cq, w_co, ln2_g, ln2_b, w_router, b_router, w_gu, b_gu, w_dn, b_dn, ln3_g, ln3_b):
    parts = []
    after = None
    n_mem = kv.shape[0] // bsz
    b0 = 0
    for nb in _part_batches(bsz):
        h2, idx_slab, gate_slab = _cross_attention_router(h1, kv, n_mem, b0, nb, s, w_cq, w_co, ln2_g, ln2_b,
                                                          w_router, b_router, after)
        b0 += nb
        parts.append((h2, gate_slab) + _route_and_dispatch(h2, idx_slab))
        after = parts[-1][-1]
    w_glu, w_lin = _split_gate_up(w_gu, after)
    expert_w = (w_glu, w_lin, b_gu[:, None, 0::2], b_gu[:, None, 1::2], w_dn, b_dn[:, None, :])
    out = None
    tok0 = 0
    for h2, gate_slab, xb, blk_meta, dest_kt in parts:
        n_rows = xb.shape[0] // ROW_TILES
        yb = _experts(xb, blk_meta, *expert_w)
        yg = _gather_rows(yb.reshape(n_rows, ROW_TILES, LANES), dest_kt.reshape(-1))
        out = _combine(h2, gate_slab, yg.reshape(-1, LANES), ln3_g, ln3_b, tok0, bsz * s, out)
        tok0 += h2.shape[0] // ROW_TILES
    return out


def kernel(x, mem, positions, w_in, b_igate, b_fgate, conv_w, conv_b, w_mq, w_mk, g_mhead, w_mskip, g_qlat, g_kvlat,
           w_uq, w_ukv, w_br_m, w_br_a, w_mix_out, ln1_g, ln1_b, w_cq, w_ckv, w_co, ln2_g, ln2_b, w_router, b_router,
           w_gu, b_gu, w_dn, b_dn, ln3_g, ln3_b):
    bsz, s, d = x.shape
    t = bsz * s
    h = x.reshape(t, d)
    pos2 = positions.reshape(t, 1)
    for l in range(DEPTH):
        q, k, v, xm, vm, op, gm, ga, gate = _input_projection(h, pos2, w_in[l], g_qlat[l], g_kvlat[l], w_uq[l],
                                                              w_ukv[l])
        ym = _mlstm_branch(xm, vm, op, gate, bsz, s, b_igate[l], b_fgate[l], conv_w[l], conv_b[l], w_mq[l], w_mk[l],
                           g_mhead[l], w_mskip[l])
        oa = _mla_attention(q, k, v, bsz, s)
        h1 = _merge(ym, oa, gm, ga, h, w_br_m[l], w_br_a[l], w_mix_out[l], ln1_g[l], ln1_b[l])
        kv = _mem_kv(mem.reshape(-1, d), w_ckv[l])
        h = _xattn_moe(h1, kv, bsz, s, w_cq[l], w_co[l], ln2_g[l], ln2_b[l], w_router[l], b_router[l], w_gu[l],
                       b_gu[l], w_dn[l], b_dn[l], ln3_g[l], ln3_b[l])
    return h.reshape(bsz, s, d)
```

```python
import functools

import numpy as np
import jax
import jax.numpy as jnp
from jax import lax
from jax.experimental import pallas as pl
from jax.experimental.pallas import tpu as pltpu
from jax.experimental.pallas import tpu_sc as plsc

F32 = jnp.float32
BF16 = jnp.bfloat16
I32 = jnp.int32

D_MODEL = 1024
N_MEM = 256
M_HEADS = 4
M_HEAD_DIM = 128
M_WIDTH = M_HEADS * M_HEAD_DIM
M_CONV = 4
A_HEADS = 8
A_NOPE = 64
A_ROPE = 32
A_QK = A_NOPE + A_ROPE
A_VDIM = 64
A_Q_RANK = 256
A_KV_RANK = 128
ROPE_THETA = 10000.0
X_HEADS = 4
X_HEAD_DIM = D_MODEL // X_HEADS
N_EXPERTS = 32
TOP_K = 4
D_EXPERT = D_MODEL
SWIGLU_ALPHA = 1.702
SWIGLU_LIMIT = 7.0
DEPTH = 1
DN_ALPHA = (2.0 * DEPTH) ** 0.25
EPS = 1e-5
IN_SPLITS = (A_Q_RANK, A_KV_RANK, A_ROPE, M_WIDTH, M_WIDTH, M_WIDTH, M_HEADS, M_HEADS, D_MODEL, D_MODEL)
IN_OFFSETS = tuple(int(v) for v in np.cumsum((0,) + IN_SPLITS))

LANES = 128
VMEM_LIMIT = 56 * 1024 * 1024

PROJ_ROWS = 512
M_CHUNK_ROWS = 128
MLSTM_SEQS = 2
ATT_Q = 256
MERGE_ROWS = 512
XATT_ROWS = 1024
RANK_ROWS = 1024
EXPERT_ROWS = 256
COMBINE_TOKENS = 512
MOE_PART_SHARES = (5, 3)
SC_CORES = 2
SC_SUBCORES = 16
SC_WORKERS = SC_CORES * SC_SUBCORES
SC_ROWS = 32
NEG_BIG = -1e30
LOG2_E = 1.4426950408889634
ROW_TILES = D_MODEL // LANES

C_QLAT = 0
C_KVLAT = C_QLAT + A_Q_RANK
C_KR = C_KVLAT + A_KV_RANK
C_KRS = C_KR + LANES
C_GATE = C_KRS + LANES
C_XM = C_GATE + LANES
C_VM = C_XM + M_WIDTH
C_OP = C_VM + M_WIDTH
C_GM = C_OP + M_WIDTH
C_GA = C_GM + D_MODEL
C_END = C_GA + D_MODEL


def _cparams(sem, vmem=VMEM_LIMIT):
    return pltpu.CompilerParams(dimension_semantics=sem, vmem_limit_bytes=vmem)


def _const_spec(shape):
    nd = len(shape)
    return pl.BlockSpec(shape, lambda *a: (0,) * nd)


def _layer_norm_rows(v, g, b):
    mu = jnp.mean(v, axis=-1, keepdims=True)
    d = v - mu
    var = jnp.mean(d * d, axis=-1, keepdims=True)
    return d * lax.rsqrt(var + EPS) * g + b


def _proj_kernel(x_ref, pos_ref, w_ref, gq_ref, gkv_ref, wuq_ref, wuqs_ref, wuk_ref, wuv_ref, invf_ref,
                 q_ref, k_ref, v_ref, xm_ref, vm_ref, op_ref, gm_ref, ga_ref, gate_ref):
    xb = x_ref[...].astype(BF16)

    def mm(lo, n):
        return jnp.dot(xb, w_ref[:, lo:lo + n], preferred_element_type=F32)

    xm_ref[...] = mm(C_XM, M_WIDTH).astype(BF16)
    vm_ref[...] = mm(C_VM, M_WIDTH).astype(BF16)
    op_ref[...] = mm(C_OP, M_WIDTH).astype(BF16)
    gm_ref[...] = mm(C_GM, D_MODEL).astype(BF16)
    ga_ref[...] = mm(C_GA, D_MODEL).astype(BF16)
    gate_ref[...] = mm(C_GATE, LANES)

    ang = pos_ref[...].astype(F32) * invf_ref[...]
    cos = jnp.cos(ang)
    sin = jnp.sin(ang)

    q_lat = mm(C_QLAT, A_Q_RANK)
    qn = (q_lat * lax.rsqrt(jnp.mean(q_lat * q_lat, axis=-1, keepdims=True) + EPS) * gq_ref[...]).astype(BF16)
    q = jnp.dot(qn, wuq_ref[...], preferred_element_type=F32)
    qs = jnp.dot(qn, wuqs_ref[...], preferred_element_type=F32)
    kv_lat = mm(C_KVLAT, A_KV_RANK)
    kvn = (kv_lat * lax.rsqrt(jnp.mean(kv_lat * kv_lat, axis=-1, keepdims=True) + EPS) * gkv_ref[...]).astype(BF16)
    kn = jnp.dot(kvn, wuk_ref[...], preferred_element_type=F32)
    lane = lax.broadcasted_iota(I32, (1, A_HEADS * LANES), 1)
    ones_lane = (lane % LANES == A_VDIM).astype(F32)
    v_ref[...] = (jnp.dot(kvn, wuv_ref[...], preferred_element_type=F32) + ones_lane).astype(BF16)
    k_pe = mm(C_KR, LANES) * cos + mm(C_KRS, LANES) * sin
    scale = A_QK ** -0.5 * LOG2_E
    for h in range(A_HEADS):
        sl = slice(h * LANES, (h + 1) * LANES)
        q_ref[:, sl] = ((q[:, sl] * cos + qs[:, sl] * sin) * scale).astype(BF16)
        k_ref[:, sl] = (kn[:, sl] + k_pe).astype(BF16)


def _proj_weights(w_in, w_uq, w_ukv):
    o = IN_OFFSETS
    half = A_ROPE // 2
    w_q, w_kv, w_kr = w_in[:, o[0]:o[1]], w_in[:, o[1]:o[2]], w_in[:, o[2]:o[3]]
    w_xm, w_vm, w_op = w_in[:, o[3]:o[4]], w_in[:, o[4]:o[5]], w_in[:, o[5]:o[6]]
    w_i, w_f, w_gm, w_ga = w_in[:, o[6]:o[7]], w_in[:, o[7]:o[8]], w_in[:, o[8]:o[9]], w_in[:, o[9]:o[10]]
    d = w_in.shape[0]
    z = lambda n: jnp.zeros((d, n), w_in.dtype)
    kr = jnp.concatenate([z(A_NOPE), w_kr, z(LANES - A_QK)], axis=1)
    krs = jnp.concatenate([z(A_NOPE), -w_kr[:, half:], w_kr[:, :half], z(LANES - A_QK)], axis=1)
    gate = jnp.concatenate([w_i, w_f, z(LANES - 2 * M_HEADS)], axis=1)
    w_r = jnp.concatenate([w_q, w_kv, kr, krs, gate, w_xm, w_vm, w_op, w_gm, w_ga], axis=1).astype(BF16)

    uq = w_uq.reshape(A_Q_RANK, A_HEADS, A_QK)
    zq = jnp.zeros((A_Q_RANK, A_HEADS, LANES - A_QK), w_uq.dtype)
    zn = jnp.zeros((A_Q_RANK, A_HEADS, A_NOPE), w_uq.dtype)
    uq_pad = jnp.concatenate([uq, zq], axis=-1).reshape(A_Q_RANK, A_HEADS * LANES).astype(BF16)
    uqs_pad = jnp.concatenate([zn, -uq[..., A_NOPE + half:], uq[..., A_NOPE:A_NOPE + half], zq],
                              axis=-1).reshape(A_Q_RANK, A_HEADS * LANES).astype(BF16)
    ukv = w_ukv.reshape(A_KV_RANK, A_HEADS, A_NOPE + A_VDIM)
    zk = jnp.zeros((A_KV_RANK, A_HEADS, LANES - A_NOPE), w_ukv.dtype)
    uk_pad = jnp.concatenate([ukv[..., :A_NOPE], zk], axis=-1).reshape(A_KV_RANK, A_HEADS * LANES).astype(BF16)
    zv = jnp.zeros((A_KV_RANK, A_HEADS, LANES - A_VDIM), w_ukv.dtype)
    uv_pad = jnp.concatenate([ukv[..., A_NOPE:], zv], axis=-1).reshape(A_KV_RANK, A_HEADS * LANES).astype(BF16)
    return w_r, uq_pad, uqs_pad, uk_pad, uv_pad


def _input_projection(x2, pos2, w_in, g_qlat, g_kvlat, w_uq, w_ukv):
    t = x2.shape[0]
    tm = min(PROJ_ROWS, t)
    w_r, uq_pad, uqs_pad, uk_pad, uv_pad = _proj_weights(w_in, w_uq, w_ukv)
    half = A_ROPE // 2
    inv_freq = ROPE_THETA ** (-jnp.arange(half, dtype=F32) / half)
    invf = jnp.concatenate([jnp.zeros((A_NOPE,), F32), inv_freq, inv_freq,
                            jnp.zeros((LANES - A_QK,), F32)]).reshape(1, LANES)
    hw = A_HEADS * LANES
    row = lambda n: pl.BlockSpec((tm, n), lambda i: (i, 0))
    outs = [
        jax.ShapeDtypeStruct((t, hw), BF16), jax.ShapeDtypeStruct((t, hw), BF16), jax.ShapeDtypeStruct((t, hw), BF16),
        jax.ShapeDtypeStruct((t, M_WIDTH), BF16), jax.ShapeDtypeStruct((t, M_WIDTH), BF16),
        jax.ShapeDtypeStruct((t, M_WIDTH), BF16),
        jax.ShapeDtypeStruct((t, D_MODEL), BF16), jax.ShapeDtypeStruct((t, D_MODEL), BF16),
        jax.ShapeDtypeStruct((t, LANES), F32),
    ]
    return pl.pallas_call(
        _proj_kernel,
        out_shape=outs,
        grid=(t // tm,),
        in_specs=[row(D_MODEL), row(1), _const_spec(w_r.shape), _const_spec((1, A_Q_RANK)),
                  _const_spec((1, A_KV_RANK)), _const_spec(uq_pad.shape), _const_spec(uqs_pad.shape),
                  _const_spec(uk_pad.shape), _const_spec(uv_pad.shape), _const_spec((1, LANES))],
        out_specs=[row(hw), row(hw), row(hw), row(M_WIDTH), row(M_WIDTH), row(M_WIDTH), row(D_MODEL), row(D_MODEL),
                   row(LANES)],
        compiler_params=_cparams(("parallel",)),
        name="input_projection",
    )(x2, pos2, w_r, g_qlat.reshape(1, -1), g_kvlat.reshape(1, -1), uq_pad, uqs_pad, uk_pad, uv_pad, invf)


def _log_sigmoid(v):
    return jnp.minimum(v, 0.0) - jnp.log1p(jnp.exp(-jnp.abs(v)))


def _mlstm_kernel(xm_ref, vm_ref, op_ref, gate_ref, convw_ref, convb_ref, wqk_ref, gbias_ref, ghead_ref,
                  skip_ref, tril_ref, out_ref, xpad_s, ctn_s, m_s, *, seq_len):
    s = seq_len
    lc = M_CHUNK_ROWS
    halo = 8
    for bb in range(MLSTM_SEQS):
        xpad_s[bb, 0:halo, :] = jnp.zeros((halo, M_WIDTH), F32)
        xpad_s[bb, halo:, :] = xm_ref[bb * s:(bb + 1) * s, :].astype(F32)
    ctn_s[...] = jnp.zeros_like(ctn_s)
    m_s[...] = jnp.zeros_like(m_s)
    rows = lax.broadcasted_iota(I32, (lc, lc), 0)
    cols = lax.broadcasted_iota(I32, (lc, lc), 1)
    causal = rows >= cols
    row_id = lax.broadcasted_iota(I32, (lc, LANES), 0)
    ones_blk = (lax.broadcasted_iota(I32, (lc, LANES), 1) == 0).astype(BF16)
    kscale = M_HEAD_DIM ** -0.5
    hsl = [slice(h * M_HEAD_DIM, (h + 1) * M_HEAD_DIM) for h in range(M_HEADS)]
    hcl = [slice(h, h + 1) for h in range(M_HEADS)]
    streams = [(bb, h) for bb in range(MLSTM_SEQS) for h in range(M_HEADS)]

    def gates(bb, r0):
        win = xpad_s[bb, pl.ds(r0 - bb * s, lc + halo), :]
        conv = convb_ref[...]
        for j in range(M_CONV):
            lo = halo - (M_CONV - 1) + j
            conv = conv + win[lo:lo + lc, :] * convw_ref[j:j + 1, :]
        xc = conv * jax.nn.sigmoid(conv)
        g = gate_ref[pl.ds(r0, lc), :] + gbias_ref[...]
        ls = _log_sigmoid(g)
        bc = jnp.dot(tril_ref[...], ls, preferred_element_type=F32, precision=lax.Precision.HIGHEST)
        b0 = pltpu.roll(bc, LANES - M_HEADS, axis=1)
        u = g - b0
        cm = u
        shift = 1
        while shift < lc:
            cm = jnp.maximum(cm, jnp.where(row_id >= shift, pltpu.roll(cm, shift, axis=0), -jnp.inf))
            shift *= 2
        m_prev = m_s[bb:bb + 1, :]
        g_inter = b0 + m_prev
        m = jnp.maximum(g_inter, b0 + cm)
        b_tot = b0[lc - 1:lc, :]
        aw = b_tot - b0 + g
        m_chunk = jnp.max(aw, axis=0, keepdims=True)
        m_new = jnp.maximum(b_tot + m_prev, m_chunk)
        m_s[bb:bb + 1, :] = m_new
        return dict(xc=xc, w_inter=jnp.exp(g_inter - m), e_negm=jnp.exp(-m), a_mat=b0 - m, wa=jnp.exp(aw - m_chunk),
                    s_old=jnp.exp(b_tot + m_prev - m_new), s_new=jnp.exp(m_chunk - m_new), u_t=u.T)

    def chunk(c, carry):
        r0s = [pl.multiple_of(bb * s + c * lc, lc) for bb in range(MLSTM_SEQS)]
        gs = [gates(bb, r0s[bb]) for bb in range(MLSTM_SEQS)]
        qk = [jnp.dot(gs[bb]["xc"].astype(BF16), wqk_ref[...], preferred_element_type=F32)
              for bb in range(MLSTM_SEQS)]
        q_b = [qk[bb][:, hsl[h]].astype(BF16) for bb, h in streams]
        k_f = [qk[bb][:, M_WIDTH + h * M_HEAD_DIM:M_WIDTH + (h + 1) * M_HEAD_DIM] * kscale for bb, h in streams]
        v_aug = [jnp.concatenate([vm_ref[pl.ds(r0s[bb], lc), hsl[h]], ones_blk], axis=1) for bb, h in streams]
        ctn_prev = [ctn_s[bb * M_HEADS + h] for bb, h in streams]
        s_raw = [lax.dot_general(q_b[i], k_f[i].astype(BF16), (((1,), (1,)), ((), ())), preferred_element_type=F32)
                 for i in range(len(streams))]
        inter = [jnp.dot(q_b[i], ctn_prev[i].astype(BF16), preferred_element_type=F32)
                 for i in range(len(streams))]
        ctn_c = [lax.dot_general((k_f[i] * gs[bb]["wa"][:, hcl[h]]).astype(BF16), v_aug[i], (((0,), (0,)), ((), ())),
                                 preferred_element_type=F32) for i, (bb, h) in enumerate(streams)]
        sc_b = [(s_raw[i] * jnp.exp(jnp.where(causal, gs[bb]["a_mat"][:, hcl[h]] + gs[bb]["u_t"][hcl[h], :],
                                              -jnp.inf))).astype(BF16) for i, (bb, h) in enumerate(streams)]
        intra = [jnp.dot(sc_b[i], v_aug[i], preferred_element_type=F32) for i in range(len(streams))]
        for i, (bb, h) in enumerate(streams):
            g = gs[bb]
            ctn_s[bb * M_HEADS + h] = g["s_old"][:, hcl[h]] * ctn_prev[i] + g["s_new"][:, hcl[h]] * ctn_c[i]
            wi = g["w_inter"][:, hcl[h]]
            num = wi * inter[i][:, :M_HEAD_DIM] + intra[i][:, :M_HEAD_DIM]
            den = wi * inter[i][:, M_HEAD_DIM:M_HEAD_DIM + 1] + intra[i][:, M_HEAD_DIM:M_HEAD_DIM + 1]
            hh = num / jnp.maximum(jnp.abs(den), g["e_negm"][:, hcl[h]])
            mu = jnp.mean(hh, axis=-1, keepdims=True)
            dv = hh - mu
            var = jnp.mean(dv * dv, axis=-1, keepdims=True)
            hn = dv * lax.rsqrt(var + EPS) * ghead_ref[:, hsl[h]]
            y = (jax.nn.sigmoid(op_ref[pl.ds(r0s[bb], lc), hsl[h]].astype(F32))
                 * (hn + skip_ref[:, hsl[h]] * g["xc"][:, hsl[h]]))
            out_ref[pl.ds(r0s[bb], lc), hsl[h]] = y.astype(BF16)
        return carry

    lax.fori_loop(0, s // lc, chunk, 0)


def _mlstm_branch(xm, vm, op, gate, bsz, s, b_i, b_f, conv_w, conv_b, w_mq, w_mk, g_mhead, w_mskip):
    lc = M_CHUNK_ROWS
    gbias = jnp.concatenate([b_i, b_f, jnp.zeros((LANES - 2 * M_HEADS,), F32)]).reshape(1, LANES)
    tril = jnp.tril(jnp.ones((lc, lc), F32))
    eye = jnp.eye(M_HEADS, dtype=F32)
    block_diag = lambda w: jnp.einsum('hde,hg->hdge', w, eye).reshape(M_WIDTH, M_WIDTH)
    w_qk = jnp.concatenate([block_diag(w_mq), block_diag(w_mk)], axis=1).astype(BF16)
    nseq = MLSTM_SEQS
    assert bsz % nseq == 0
    seq = lambda n: pl.BlockSpec((nseq * s, n), lambda b: (b, 0))
    return pl.pallas_call(
        functools.partial(_mlstm_kernel, seq_len=s),
        out_shape=jax.ShapeDtypeStruct((bsz * s, M_WIDTH), BF16),
        grid=(bsz // nseq,),
        in_specs=[seq(M_WIDTH), seq(M_WIDTH), seq(M_WIDTH), seq(LANES),
                  _const_spec((M_CONV, M_WIDTH)), _const_spec((1, M_WIDTH)),
                  _const_spec((M_WIDTH, 2 * M_WIDTH)),
                  _const_spec((1, LANES)), _const_spec((1, M_WIDTH)), _const_spec((1, M_WIDTH)),
                  _const_spec((lc, lc))],
        out_specs=seq(M_WIDTH),
        scratch_shapes=[pltpu.VMEM((nseq, s + 8, M_WIDTH), F32),
                        pltpu.VMEM((nseq * M_HEADS, M_HEAD_DIM, 2 * LANES), F32), pltpu.VMEM((8, LANES), F32)],
        compiler_params=_cparams(("parallel",)),
        name="mlstm_branch",
    )(xm, vm, op, gate, conv_w, conv_b.reshape(1, -1), w_qk, gbias, g_mhead.reshape(1, -1), w_mskip.reshape(1, -1),
      tril)


def _attn_kernel(q_ref, k_ref, v_ref, o_ref, s_scr, m_s, acc_s):
    tq = q_ref.shape[0]
    i = pl.program_id(1)
    rows = lax.broadcasted_iota(I32, (tq, tq), 0)
    cols = lax.broadcasted_iota(I32, (tq, tq), 1)

    def lane_tile_max(sc):
        out = sc[:, :LANES]
        for t in range(1, tq // LANES):
            out = jnp.maximum(out, sc[:, t * LANES:(t + 1) * LANES])
        return out

    heads = [slice(h * LANES, (h + 1) * LANES) for h in range(A_HEADS)]
    m_s[...] = jnp.full(m_s.shape, NEG_BIG, F32)
    acc_s[...] = jnp.zeros_like(acc_s)

    def pass_a(j, diagonal):
        k0 = pl.multiple_of(j * tq, tq)
        for h, hs in enumerate(heads):
            sc = lax.dot_general(q_ref[:, hs], k_ref[pl.ds(k0, tq), hs], (((1,), (1,)), ((), ())),
                                 preferred_element_type=F32)
            if diagonal:
                sc = jnp.where(cols <= rows, sc, NEG_BIG)
            s_scr[h, j] = sc
            m_s[h] = jnp.maximum(m_s[h], lane_tile_max(sc))

    def for_blocks(n, fn):
        def quad(p, c):
            for u in range(4):
                fn(4 * p + u)
            return c

        lax.fori_loop(0, n // 4, quad, 0)
        rem = n % 4

        @pl.when(rem >= 2)
        def _():
            fn(n - rem)
            fn(n - rem + 1)

        @pl.when(rem % 2 == 1)
        def _():
            fn(n - 1)

    for_blocks(i, lambda j: pass_a(j, False))
    pass_a(i, True)
    for h in range(A_HEADS):
        m_s[h] = jnp.broadcast_to(jnp.max(m_s[h], axis=1, keepdims=True), (tq, LANES))

    def pass_b(j):
        k0 = pl.multiple_of(j * tq, tq)
        for h, hs in enumerate(heads):
            m_row = m_s[h]
            p = jnp.exp2(s_scr[h, j] - jnp.concatenate([m_row] * (tq // LANES), axis=1)).astype(BF16)
            acc_s[h] += jnp.dot(p, v_ref[pl.ds(k0, tq), hs], preferred_element_type=F32)

    for_blocks(i + 1, pass_b)
    for h, hs in enumerate(heads):
        acc = acc_s[h]
        o_ref[:, hs] = (acc / acc[:, A_VDIM:A_VDIM + 1]).astype(BF16)


def _mla_attention(q, k, v, bsz, s):
    tq = min(ATT_Q, s)
    nq = s // tq
    hw = A_HEADS * LANES
    return pl.pallas_call(
        _attn_kernel,
        out_shape=jax.ShapeDtypeStruct(q.shape, BF16),
        grid=(bsz, nq),
        in_specs=[pl.BlockSpec((tq, hw), lambda b, i: (b * nq + i, 0)),
                  pl.BlockSpec((s, hw), lambda b, i: (b, 0)),
                  pl.BlockSpec((s, hw), lambda b, i: (b, 0))],
        out_specs=pl.BlockSpec((tq, hw), lambda b, i: (b * nq + i, 0)),
        scratch_shapes=[pltpu.VMEM((A_HEADS, nq, tq, tq), F32), pltpu.VMEM((A_HEADS, tq, LANES), F32),
                        pltpu.VMEM((A_HEADS, tq, LANES), F32)],
        compiler_params=_cparams(("parallel", "arbitrary")),
        name="mla_attention",
    )(q, k, v)


def _merge_kernel(ym_ref, oa_ref, gm_ref, ga_ref, x_ref, wbm_ref, wba_ref, wmix_ref, g_ref, b_ref, h_ref):
    y_m = jnp.dot(ym_ref[...], wbm_ref[...], preferred_element_type=F32)
    y_a = jnp.dot(oa_ref[...], wba_ref[...], preferred_element_type=F32)
    mixed = jax.nn.sigmoid(gm_ref[...].astype(F32)) * y_m + jax.nn.sigmoid(ga_ref[...].astype(F32)) * y_a
    mix = jnp.dot(mixed.astype(BF16), wmix_ref[...], preferred_element_type=F32)
    h_ref[...] = _layer_norm_rows(DN_ALPHA * x_ref[...] + mix, g_ref[...], b_ref[...])


def _merge(ym, oa, gm, ga, x2, w_br_m, w_br_a, w_mix_out, ln_g, ln_b):
    t = x2.shape[0]
    tm = min(MERGE_ROWS, t)
    wba = jnp.concatenate([w_br_a.reshape(A_HEADS, A_VDIM, D_MODEL),
                           jnp.zeros((A_HEADS, LANES - A_VDIM, D_MODEL), w_br_a.dtype)], axis=1)
    wba = wba.reshape(A_HEADS * LANES, D_MODEL).astype(BF16)
    row = lambda n: pl.BlockSpec((tm, n), lambda i: (i, 0))
    return pl.pallas_call(
        _merge_kernel,
        out_shape=jax.ShapeDtypeStruct((t, D_MODEL), F32),
        grid=(t // tm,),
        in_specs=[row(M_WIDTH), row(A_HEADS * LANES), row(D_MODEL), row(D_MODEL), row(D_MODEL),
                  _const_spec((M_WIDTH, D_MODEL)), _const_spec((A_HEADS * LANES, D_MODEL)),
                  _const_spec((D_MODEL, D_MODEL)), _const_spec((1, D_MODEL)), _const_spec((1, D_MODEL))],
        out_specs=row(D_MODEL),
        compiler_params=_cparams(("parallel",)),
        name="merge_deepnorm1",
    )(ym, oa, gm, ga, x2, w_br_m.astype(BF16), wba, w_mix_out.astype(BF16), ln_g.reshape(1, -1), ln_b.reshape(1, -1))


def _kv_kernel(m_ref, w_ref, o_ref):
    o_ref[...] = jnp.dot(m_ref[...].astype(BF16), w_ref[...], preferred_element_type=F32).astype(BF16)


def _mem_kv(mem2, w_ckv):
    r = mem2.shape[0]
    tm = min(512, r)
    return pl.pallas_call(
        _kv_kernel,
        out_shape=jax.ShapeDtypeStruct((r, 2 * D_MODEL), BF16),
        grid=(r // tm,),
        in_specs=[pl.BlockSpec((tm, D_MODEL), lambda i: (i, 0)), _const_spec((D_MODEL, 2 * D_MODEL))],
        out_specs=pl.BlockSpec((tm, 2 * D_MODEL), lambda i: (i, 0)),
        compiler_params=_cparams(("parallel",)),
        name="memory_kv",
    )(mem2, w_ckv.astype(BF16))


def _xattn_kernel(h_ref, kv_ref, wq_ref, wo_ref, g_ref, b_ref, wr_ref, br_ref, *rest):
    h2_ref, idx_ref, gate_ref = rest[-3:]
    h1 = h_ref[...]
    q = jnp.dot(h1.astype(BF16), wq_ref[...], preferred_element_type=F32).astype(BF16)
    scale = X_HEAD_DIM ** -0.5
    ksl = [slice(hd * X_HEAD_DIM, (hd + 1) * X_HEAD_DIM) for hd in range(X_HEADS)]
    vsl = [slice(D_MODEL + hd * X_HEAD_DIM, D_MODEL + (hd + 1) * X_HEAD_DIM) for hd in range(X_HEADS)]
    scs = [lax.dot_general(q[:, ks], kv_ref[:, ks], (((1,), (1,)), ((), ())), preferred_element_type=F32) * scale
           for ks in ksl]
    ps = [jnp.exp(sc - jnp.max(sc, axis=1, keepdims=True)) for sc in scs]
    ps = [(p / jnp.sum(p, axis=1, keepdims=True)).astype(BF16) for p in ps]
    o = jnp.concatenate([jnp.dot(p, kv_ref[:, vs], preferred_element_type=F32).astype(BF16)
                         for p, vs in zip(ps, vsl)], axis=1)
    att = jnp.dot(o, wo_ref[...], preferred_element_type=F32)
    h2 = _layer_norm_rows(DN_ALPHA * h1 + att, g_ref[...], b_ref[...])
    for c in range(ROW_TILES):
        h2_ref[pl.ds(c, h2.shape[0], stride=ROW_TILES), :] = h2[:, c * LANES:(c + 1) * LANES]

    h2_hi = h2.astype(BF16)
    h2_lo = (h2 - h2_hi.astype(F32)).astype(BF16)
    hi_prod = jnp.dot(h2_hi, wr_ref[...], preferred_element_type=F32)
    lo_prod = jnp.dot(h2_lo, wr_ref[:, :LANES], preferred_element_type=F32)
    logits = hi_prod[:, :LANES] + (hi_prod[:, LANES:] + lo_prod) + br_ref[...]
    tm = logits.shape[0]
    lane = lax.broadcasted_iota(I32, (tm, LANES), 1)
    lane_f = lane.astype(F32)
    work = jnp.where(lane < N_EXPERTS, logits, -jnp.inf)
    vals, idxs = [], []
    for _ in range(TOP_K):
        mx = jnp.max(work, axis=1, keepdims=True)
        ix = jnp.min(jnp.where(work == mx, lane_f, float(LANES)), axis=1, keepdims=True)
        vals.append(mx)
        idxs.append(ix)
        work = jnp.where(lane_f == ix, -jnp.inf, work)
    es = [jnp.exp(vv - vals[0]) for vv in vals]
    tot = es[0] + es[1] + es[2] + es[3]
    idx_slab = jnp.zeros((tm, LANES), F32)
    gate_slab = jnp.zeros((tm, LANES), F32)
    for kk in range(TOP_K):
        idx_slab = jnp.where(lane == kk, idxs[kk], idx_slab)
        gate_slab = jnp.where(lane == kk, es[kk] / tot, gate_slab)
    idx_ref[...] = idx_slab.astype(I32)
    gate_ref[...] = gate_slab


def _cross_attention_router(h1, kv, n_mem, b0, bsz, s, w_cq, w_co, ln_g, ln_b, w_router, b_router, after):
    t = bsz * s
    tm = min(XATT_ROWS, s)
    ns = s // tm
    wr = jnp.concatenate([w_router, jnp.zeros((D_MODEL, LANES - N_EXPERTS), F32)], axis=1)
    wr_hi = wr.astype(BF16)
    wr = jnp.concatenate([wr_hi, (wr - wr_hi.astype(F32)).astype(BF16)], axis=1)
    br = jnp.concatenate([b_router, jnp.zeros((LANES - N_EXPERTS,), F32)]).reshape(1, LANES)
    row = lambda n: pl.BlockSpec((tm, n), lambda b, i: (b * ns + i, 0))
    h1_rows = pl.BlockSpec((tm, D_MODEL), lambda b, i: ((b0 + b) * ns + i, 0))
    in_specs = [h1_rows, pl.BlockSpec((n_mem, 2 * D_MODEL), lambda b, i: (b0 + b, 0)),
                _const_spec((D_MODEL, D_MODEL)), _const_spec((D_MODEL, D_MODEL)),
                _const_spec((1, D_MODEL)), _const_spec((1, D_MODEL)),
                _const_spec((D_MODEL, 2 * LANES)), _const_spec((1, LANES))]
    args = [h1, kv, w_cq.astype(BF16), w_co.astype(BF16), ln_g.reshape(1, -1), ln_b.reshape(1, -1), wr, br]
    if after is not None:
        in_specs.append(pl.BlockSpec(memory_space=pl.ANY))
        args.append(after)
    return pl.pallas_call(
        _xattn_kernel,
        out_shape=[jax.ShapeDtypeStruct((t * ROW_TILES, LANES), F32), jax.ShapeDtypeStruct((t, LANES), I32),
                   jax.ShapeDtypeStruct((t, LANES), F32)],
        grid=(bsz, ns),
        in_specs=in_specs,
        out_specs=[pl.BlockSpec((tm * ROW_TILES, LANES), lambda b, i: (b * ns + i, 0)), row(LANES), row(LANES)],
        compiler_params=_cparams(("parallel", "parallel")),
        name="cross_attention_router",
    )(*args)


def _rank_kernel(idx_ref, ltri_ref, rank_ref, cnt_ref, carry_s):
    @pl.when(pl.program_id(0) == 0)
    def _():
        carry_s[...] = jnp.zeros_like(carry_s)

    idx = idx_ref[...]
    tm = idx.shape[0]
    lane = lax.broadcasted_iota(I32, (tm, LANES), 1)
    onehots = [(lane == idx[:, kk:kk + 1]).astype(F32) for kk in range(TOP_K)]
    sel = onehots[0] + onehots[1] + onehots[2] + onehots[3]
    before = jnp.dot(ltri_ref[...], sel.astype(BF16), preferred_element_type=F32) + carry_s[0:1, :]
    rank_slab = jnp.zeros((tm, LANES), F32)
    for kk in range(TOP_K):
        r = jnp.sum(onehots[kk] * before, axis=1, keepdims=True)
        rank_slab = jnp.where(lane == kk, r, rank_slab)
    rank_ref[...] = rank_slab.astype(I32)
    carry_s[0:1, :] = carry_s[0:1, :] + jnp.sum(sel, axis=0, keepdims=True)
    cnt_ref[...] = jnp.broadcast_to(carry_s[0:1, :], cnt_ref.shape).astype(I32)


def _routing_ranks(idx_slab):
    t = idx_slab.shape[0]
    tm = min(RANK_ROWS, t)
    assert t % tm == 0
    ltri = jnp.tril(jnp.ones((tm, tm), BF16), k=-1)
    return pl.pallas_call(
        _rank_kernel,
        out_shape=[jax.ShapeDtypeStruct((t, LANES), I32), jax.ShapeDtypeStruct((8, LANES), I32)],
        grid=(t // tm,),
        in_specs=[pl.BlockSpec((tm, LANES), lambda i: (i, 0)), _const_spec((tm, tm))],
        out_specs=[pl.BlockSpec((tm, LANES), lambda i: (i, 0)), _const_spec((8, LANES))],
        scratch_shapes=[pltpu.VMEM((8, LANES), F32)],
        compiler_params=_cparams(("arbitrary",)),
        name="routing_ranks",
    )(idx_slab, ltri)


def _dest_kernel(idx_ref, rank_ref, pstart_ref, dest_ref):
    idx = idx_ref[...]
    tm = idx.shape[0]
    lane = lax.broadcasted_iota(I32, (tm, LANES), 1)
    dest = rank_ref[...].astype(F32)
    pstart = pstart_ref[0:1, :].astype(F32)
    for kk in range(TOP_K):
        start = jnp.sum(jnp.where(lane == idx[:, kk:kk + 1], pstart, 0.0), axis=1, keepdims=True)
        dest = dest + jnp.where(lane == kk, start, 0.0)
    dest_ref[...] = dest.T[0:8, :].astype(I32)


def _dest_rows(idx_slab, rank_slab, pstart_row):
    t = rank_slab.shape[0]
    tm = min(RANK_ROWS, t)
    return pl.pallas_call(
        _dest_kernel,
        out_shape=jax.ShapeDtypeStruct((8, t), I32),
        grid=(t // tm,),
        in_specs=[pl.BlockSpec((tm, LANES), lambda i: (i, 0)), pl.BlockSpec((tm, LANES), lambda i: (i, 0)),
                  _const_spec((8, LANES))],
        out_specs=pl.BlockSpec((8, tm), lambda i: (0, i)),
        compiler_params=_cparams(("parallel",)),
        name="routing_dest",
    )(idx_slab, rank_slab, pstart_row)


def _sc_mesh():
    return plsc.VectorSubcoreMesh(core_axis_name="c", subcore_axis_name="s", num_cores=SC_CORES,
                                  num_subcores=SC_SUBCORES)


def _sc_worker():
    return lax.axis_index("s") * SC_CORES + lax.axis_index("c")


def _dispatch(h2, dest_kt, pad_rows, n_rows):
    t = dest_kt.shape[1]
    w = SC_ROWS
    per_w = t // SC_WORKERS
    steps = per_w // w
    assert steps % 2 == 0 and steps * w * SC_WORKERS == t
    pad_steps = pad_rows.shape[0] // (SC_WORKERS * w)
    dest_w = dest_kt.reshape(TOP_K, SC_WORKERS, steps, w)
    pad_w = pad_rows.reshape(SC_WORKERS, pad_steps, w)
    zeros = jnp.zeros((w, ROW_TILES, LANES), F32)

    def body(x_hbm, dest_hbm, pad_hbm, zeros_hbm, o_hbm, idx_v, pad_v, rows_v, sem_in, sem_out):
        wid = _sc_worker()
        tok0 = wid * per_w
        for kk in range(TOP_K):
            pltpu.sync_copy(dest_hbm.at[kk, wid], idx_v.at[pl.ds(kk * steps, steps)])
        pltpu.sync_copy(pad_hbm.at[wid], pad_v)
        pltpu.sync_copy(zeros_hbm, rows_v.at[0])
        for j in range(pad_steps):
            pltpu.sync_copy(rows_v.at[0], o_hbm.at[pad_v.at[j]])

        def load(step, b):
            return pltpu.make_async_copy(x_hbm.at[pl.ds(tok0 + step * w, w)], rows_v.at[b], sem_in.at[b])

        def scatter(step, kk, b):
            return pltpu.make_async_copy(rows_v.at[b], o_hbm.at[idx_v.at[kk * steps + step]], sem_out.at[b])

        load(0, 0).start()

        @pl.loop(0, steps, step=2)
        def _(s0):
            for b in range(2):
                step = s0 + b
                load(step, b).wait()

                @pl.when(step >= 1)
                def _():
                    for kk in range(TOP_K):
                        scatter(step - 1, kk, 1 - b).wait()

                @pl.when(step + 1 < steps)
                def _():
                    load(step + 1, 1 - b).start()

                for kk in range(TOP_K):
                    scatter(step, kk, b).start()

        for kk in range(TOP_K):
            scatter(steps - 1, kk, 1).wait()

    return pl.kernel(
        body,
        out_type=jax.ShapeDtypeStruct((n_rows, ROW_TILES, LANES), F32),
        mesh=_sc_mesh(),
        scratch_types=[pltpu.VMEM((steps * TOP_K, w), I32), pltpu.VMEM((pad_steps, w), I32),
                       pltpu.VMEM((2, w, ROW_TILES, LANES), F32), pltpu.SemaphoreType.DMA((2,)),
                       pltpu.SemaphoreType.DMA((2,))],
        name="moe_dispatch_sc",
    )(h2, dest_w, pad_w, zeros)


def _gather_rows(yb, dest_kt):
    n = dest_kt.shape[0]
    w = SC_ROWS
    per_w = n // SC_WORKERS
    steps = per_w // w
    assert steps % 2 == 0 and steps * w * SC_WORKERS == n
    dest_w = dest_kt.reshape(SC_WORKERS, steps, w)

    def body(y_hbm, dest_hbm, o_hbm, idx_v, rows_v, sem_g, sem_w):
        wid = _sc_worker()
        row0 = wid * per_w
        pltpu.sync_copy(dest_hbm.at[wid], idx_v)

        def gather(step, b):
            return pltpu.make_async_copy(y_hbm.at[idx_v.at[step]], rows_v.at[b], sem_g.at[b])

        def write(step, b):
            return pltpu.make_async_copy(rows_v.at[b], o_hbm.at[pl.ds(row0 + step * w, w)], sem_w.at[b])

        gather(0, 0).start()

        @pl.loop(0, steps, step=2)
        def _(s0):
            for b in range(2):
                step = s0 + b
                gather(step, b).wait()

                @pl.when(step >= 1)
                def _():
                    write(step - 1, 1 - b).wait()

                @pl.when(step + 1 < steps)
                def _():
                    gather(step + 1, 1 - b).start()

                write(step, b).start()

        write(steps - 1, 1).wait()

    return pl.kernel(
        body,
        out_type=jax.ShapeDtypeStruct((n, ROW_TILES, LANES), F32),
        mesh=_sc_mesh(),
        scratch_types=[pltpu.VMEM((steps, w), I32), pltpu.VMEM((2, w, ROW_TILES, LANES), F32),
                       pltpu.SemaphoreType.DMA((2,)), pltpu.SemaphoreType.DMA((2,))],
        name="moe_gather_sc",
    )(yb, dest_w)


def _rows_2d(ref, n):
    return jnp.concatenate([ref[pl.ds(c, n, stride=ROW_TILES), :] for c in range(ROW_TILES)], axis=1)


def _store_rows(ref, val):
    for c in range(ROW_TILES):
        ref[pl.ds(c, val.shape[0], stride=ROW_TILES), :] = val[:, c * LANES:(c + 1) * LANES]


def _expert_kernel(blk_e_ref, first_ref, slot_ref, next_ref, nblk_ref, x_ref, wg_hbm, wl_hbm, bg_ref, bl_ref, wd_hbm,
                   bd_ref, y_ref, wg_buf, wl_buf, wd_stage, wd_buf, sem):
    i = pl.program_id(0)

    def fetch(e, sl):
        return [pltpu.make_async_copy(src.at[e], dst.at[sl], sem.at[sl, j])
                for j, (src, dst) in enumerate(((wg_hbm, wg_buf), (wl_hbm, wl_buf), (wd_hbm, wd_stage)))]

    @pl.when(i < nblk_ref[0])
    def _():
        sl = slot_ref[i]

        @pl.when(first_ref[i] == 1)
        def _():
            @pl.when(i == 0)
            def _():
                for cp in fetch(blk_e_ref[0], sl):
                    cp.start()

            for cp in fetch(blk_e_ref[i], sl):
                cp.wait()
            wd_buf[sl] = wd_stage[sl].astype(BF16)

            @pl.when(next_ref[i] >= 0)
            def _():
                for cp in fetch(next_ref[i], 1 - sl):
                    cp.start()

        xb = _rows_2d(x_ref, y_ref.shape[0] // ROW_TILES).astype(BF16)
        glu = jnp.dot(xb, wg_buf[sl], preferred_element_type=F32) + bg_ref[...]
        lin = jnp.dot(xb, wl_buf[sl], preferred_element_type=F32) + bl_ref[...]
        glu = jnp.minimum(glu, SWIGLU_LIMIT)
        lin = jnp.clip(lin, -SWIGLU_LIMIT, SWIGLU_LIMIT)
        act = glu * jax.nn.sigmoid(SWIGLU_ALPHA * glu) * (lin + 1.0)
        _store_rows(y_ref, jnp.dot(act.astype(BF16), wd_buf[sl], preferred_element_type=F32) + bd_ref[...])

    @pl.when(i >= nblk_ref[0])
    def _():
        y_ref[...] = jnp.zeros_like(y_ref)


def _split_kernel(w_ref, perm_ref, after_ref, g_ref, l_ref):
    del after_ref
    half = g_ref.shape[-1]
    sorted_cols = jnp.dot(w_ref[...].astype(BF16), perm_ref[...], preferred_element_type=F32)
    g_ref[...] = sorted_cols[:, :half].astype(BF16)
    l_ref[...] = sorted_cols[:, half:].astype(BF16)


def _split_gate_up(w_gu, after):
    e, d, n2 = w_gu.shape
    cw = 512
    src = np.concatenate([np.arange(0, cw, 2), np.arange(1, cw, 2)])
    perm = jnp.asarray(np.eye(cw, dtype=np.float32)[:, src], BF16)
    return pl.pallas_call(
        _split_kernel,
        out_shape=[jax.ShapeDtypeStruct((e, d, n2 // 2), BF16), jax.ShapeDtypeStruct((e, d, n2 // 2), BF16)],
        grid=(e, n2 // cw),
        in_specs=[pl.BlockSpec((None, d, cw), lambda i, c: (i, 0, c)), _const_spec((cw, cw)),
                  pl.BlockSpec(memory_space=pl.ANY)],
        out_specs=[pl.BlockSpec((None, d, cw // 2), lambda i, c: (i, 0, c)),
                   pl.BlockSpec((None, d, cw // 2), lambda i, c: (i, 0, c))],
        compiler_params=_cparams(("parallel", "parallel")),
        name="split_gate_up",
    )(w_gu, perm, after)


def _experts(xb, blk_meta, w_glu, w_lin, b_glu, b_lin, w_dn, b_dn):
    n_rows = xb.shape[0] // ROW_TILES
    bm = EXPERT_ROWS
    vec = lambda n: pl.BlockSpec((None, 1, n), lambda i, be, *_: (be[i], 0, 0))
    rows = pl.BlockSpec((bm * ROW_TILES, LANES), lambda i, *_: (i, 0))
    hbm = pl.BlockSpec(memory_space=pl.ANY)
    grid_spec = pltpu.PrefetchScalarGridSpec(
        num_scalar_prefetch=5,
        grid=(n_rows // bm,),
        in_specs=[rows, hbm, hbm, vec(D_EXPERT), vec(D_EXPERT), hbm, vec(D_MODEL)],
        out_specs=rows,
        scratch_shapes=[pltpu.VMEM((2, D_MODEL, D_EXPERT), BF16), pltpu.VMEM((2, D_MODEL, D_EXPERT), BF16),
                        pltpu.VMEM((2, D_EXPERT, D_MODEL), F32), pltpu.VMEM((2, D_EXPERT, D_MODEL), BF16),
                        pltpu.SemaphoreType.DMA((2, 3))],
    )
    return pl.pallas_call(
        _expert_kernel,
        out_shape=jax.ShapeDtypeStruct((n_rows * ROW_TILES, LANES), F32),
        grid_spec=grid_spec,
        compiler_params=_cparams(("arbitrary",)),
        name="moe_experts",
    )(*blk_meta, xb, w_glu, w_lin, b_glu, b_lin, w_dn, b_dn)


def _combine_kernel(h_ref, gate_ref, y0_ref, y1_ref, y2_ref, y3_ref, g_ref, b_ref, *rest):
    o_ref = rest[-1]
    tt = o_ref.shape[0]
    gates = gate_ref[...]
    acc = DN_ALPHA * _rows_2d(h_ref, tt)
    for kk, y_ref in enumerate((y0_ref, y1_ref, y2_ref, y3_ref)):
        acc = acc + gates[:, kk:kk + 1] * _rows_2d(y_ref, tt)
    o_ref[...] = _layer_norm_rows(acc, g_ref[...], b_ref[...])


def _combine(h2, gate_slab, yg, ln_g, ln_b, tok0, t_all, prev_out):
    t = h2.shape[0] // ROW_TILES
    tt = min(COMBINE_TOKENS, t)
    steps = t // tt
    assert tok0 % tt == 0 and t % tt == 0
    first = tok0 // tt
    row = lambda m: pl.BlockSpec((tt, m), lambda i: (i, 0))
    tiles = lambda kk: pl.BlockSpec((tt * ROW_TILES, LANES), lambda i: (kk * steps + i, 0))
    in_specs = [tiles(0), row(LANES),
                tiles(0), tiles(1), tiles(2), tiles(3), _const_spec((1, D_MODEL)), _const_spec((1, D_MODEL))]
    args = [h2, gate_slab, yg, yg, yg, yg, ln_g.reshape(1, -1), ln_b.reshape(1, -1)]
    aliases = {}
    if prev_out is not None:
        in_specs.append(pl.BlockSpec(memory_space=pl.ANY))
        args.append(prev_out)
        aliases = {len(args) - 1: 0}
    return pl.pallas_call(
        _combine_kernel,
        out_shape=jax.ShapeDtypeStruct((t_all, D_MODEL), F32),
        grid=(steps,),
        in_specs=in_specs,
        out_specs=pl.BlockSpec((tt, D_MODEL), lambda i: (first + i, 0)),
        input_output_aliases=aliases,
        compiler_params=_cparams(("parallel",)),
        name="moe_combine",
    )(*args)


def _route_and_dispatch(h2, idx_slab):
    t = h2.shape[0] // ROW_TILES
    bm = EXPERT_ROWS
    n_rows = t * TOP_K + N_EXPERTS * bm
    n_blocks = n_rows // bm

    rank_slab, cnt = _routing_ranks(idx_slab)
    counts = cnt[0, :N_EXPERTS]
    padded = (counts + bm - 1) // bm * bm
    pends = jnp.cumsum(padded)
    pstarts = pends - padded
    pstart_row = jnp.zeros((8, LANES), I32).at[:, :N_EXPERTS].set(pstarts[None, :])
    blk_start = jnp.arange(n_blocks, dtype=I32) * bm
    blk_e = jnp.minimum(jnp.sum((pends[None, :] <= blk_start[:, None]).astype(I32), axis=1), N_EXPERTS - 1)
    nblk = (pends[-1] // bm).astype(I32).reshape(1)
    active = counts > 0
    ordinal = jnp.cumsum(active.astype(I32)) - 1
    eid = jnp.arange(N_EXPERTS, dtype=I32)
    later = active[None, :] & (eid[None, :] > eid[:, None])
    next_active = jnp.where(later.any(axis=1), jnp.argmax(later, axis=1).astype(I32), -1)
    blk_first = jnp.concatenate([jnp.ones((1,), I32), (blk_e[1:] != blk_e[:-1]).astype(I32)])
    blk_onehot = (blk_e[:, None] == eid[None, :]).astype(I32)
    blk_slot = jnp.sum(blk_onehot * ordinal[None, :], axis=1) % 2
    blk_next = jnp.sum(blk_onehot * next_active[None, :], axis=1)
    blk_meta = (blk_e, blk_first, blk_slot, blk_next, nblk)
    dest_kt = _dest_rows(idx_slab, rank_slab, pstart_row)[:TOP_K]
    pad = padded - counts
    spare_base = pends[-1] + jnp.cumsum(bm - pad) - (bm - pad)
    pad_off = jnp.arange(bm, dtype=I32)[None, :]
    pad_rows = jnp.where(pad_off < pad[:, None], (pstarts + counts)[:, None] + pad_off,
                         spare_base[:, None] + pad_off - pad[:, None]).reshape(-1)
    xb = _dispatch(h2.reshape(t, ROW_TILES, LANES), dest_kt, pad_rows, n_rows)
    return xb.reshape(n_rows * ROW_TILES, LANES), blk_meta, dest_kt


def _part_batches(bsz):
    if bsz % 8 == 0:
        return tuple(share * bsz // 8 for share in MOE_PART_SHARES)
    return (bsz // 2, bsz - bsz // 2)


def _xattn_moe(h1, kv, bsz, s, w_cq, w_co, ln2_g, ln2_b, w_router, b_router, w_gu, b_gu, w_dn, b_dn, ln3_g, ln3_b):
    parts = []
    after = None
    n_mem = kv.shape[0] // bsz
    b0 = 0
    for nb in _part_batches(bsz):
        h2, idx_slab, gate_slab = _cross_attention_router(h1, kv, n_mem, b0, nb, s, w_cq, w_co, ln2_g, ln2_b,
                                                          w_router, b_router, after)
        b0 += nb
        parts.append((h2, gate_slab) + _route_and_dispatch(h2, idx_slab))
        after = parts[-1][-1]
    w_glu, w_lin = _split_gate_up(w_gu, after)
    expert_w = (w_glu, w_lin, b_gu[:, None, 0::2], b_gu[:, None, 1::2], w_dn, b_dn[:, None, :])
    out = None
    tok0 = 0
    for h2, gate_slab, xb, blk_meta, dest_kt in parts:
        n_rows = xb.shape[0] // ROW_TILES
        yb = _experts(xb, blk_meta, *expert_w)
        yg = _gather_rows(yb.reshape(n_rows, ROW_TILES, LANES), dest_kt.reshape(-1))
        out = _combine(h2, gate_slab, yg.reshape(-1, LANES), ln3_g, ln3_b, tok0, bsz * s, out)
        tok0 += h2.shape[0] // ROW_TILES
    return out


def kernel(x, mem, positions, w_in, b_igate, b_fgate, conv_w, conv_b, w_mq, w_mk, g_mhead, w_mskip, g_qlat, g_kvlat,
           w_uq, w_ukv, w_br_m, w_br_a, w_mix_out, ln1_g, ln1_b, w_cq, w_ckv, w_co, ln2_g, ln2_b, w_router, b_router,
           w_gu, b_gu, w_dn, b_dn, ln3_g, ln3_b):
    bsz, s, d = x.shape
    t = bsz * s
    h = x.reshape(t, d)
    pos2 = positions.reshape(t, 1)
    for l in range(DEPTH):
        q, k, v, xm, vm, op, gm, ga, gate = _input_projection(h, pos2, w_in[l], g_qlat[l], g_kvlat[l], w_uq[l],
                                                              w_ukv[l])
        ym = _mlstm_branch(xm, vm, op, gate, bsz, s, b_igate[l], b_fgate[l], conv_w[l], conv_b[l], w_mq[l], w_mk[l],
                           g_mhead[l], w_mskip[l])
        oa = _mla_attention(q, k, v, bsz, s)
        h1 = _merge(ym, oa, gm, ga, h, w_br_m[l], w_br_a[l], w_mix_out[l], ln1_g[l], ln1_b[l])
        kv = _mem_kv(mem.reshape(-1, d), w_ckv[l])
        h = _xattn_moe(h1, kv, bsz, s, w_cq[l], w_co[l], ln2_g[l], ln2_b[l], w_router[l], b_router[l], w_gu[l],
                       b_gu[l], w_dn[l], b_dn[l], ln3_g[l], ln3_b[l])
    return h.reshape(bsz, s, d)
```

```python
import functools

import numpy as np
import jax
import jax.numpy as jnp
from jax import lax
from jax.experimental import pallas as pl
from jax.experimental.pallas import tpu as pltpu
from jax.experimental.pallas import tpu_sc as plsc

F32 = jnp.float32
BF16 = jnp.bfloat16
I32 = jnp.int32

D_MODEL = 1024
N_MEM = 256
M_HEADS = 4
M_HEAD_DIM = 128
M_WIDTH = M_HEADS * M_HEAD_DIM
M_CONV = 4
A_HEADS = 8
A_NOPE = 64
A_ROPE = 32
A_QK = A_NOPE + A_ROPE
A_VDIM = 64
A_Q_RANK = 256
A_KV_RANK = 128
ROPE_THETA = 10000.0
X_HEADS = 4
X_HEAD_DIM = D_MODEL // X_HEADS
N_EXPERTS = 32
TOP_K = 4
D_EXPERT = D_MODEL
SWIGLU_ALPHA = 1.702
SWIGLU_LIMIT = 7.0
DEPTH = 1
DN_ALPHA = (2.0 * DEPTH) ** 0.25
EPS = 1e-5
IN_SPLITS = (A_Q_RANK, A_KV_RANK, A_ROPE, M_WIDTH, M_WIDTH, M_WIDTH, M_HEADS, M_HEADS, D_MODEL, D_MODEL)
IN_OFFSETS = tuple(int(v) for v in np.cumsum((0,) + IN_SPLITS))

LANES = 128
VMEM_LIMIT = 56 * 1024 * 1024

PROJ_ROWS = 512
M_CHUNK_ROWS = 128
MLSTM_SEQS = 2
ATT_Q = 256
MERGE_ROWS = 1024
XATT_ROWS = 1024
RANK_ROWS = 1024
EXPERT_ROWS = 256
COMBINE_TOKENS = 512
MOE_PART_SHARES = (5, 3)
SC_CORES = 2
SC_SUBCORES = 16
SC_WORKERS = SC_CORES * SC_SUBCORES
SC_ROWS = 32
NEG_BIG = -1e30
LOG2_E = 1.4426950408889634
ROW_TILES = D_MODEL // LANES

C_QLAT = 0
C_KVLAT = C_QLAT + A_Q_RANK
C_KR = C_KVLAT + A_KV_RANK
C_KRS = C_KR + LANES
C_GATE = C_KRS + LANES
C_XM = C_GATE + LANES
C_VM = C_XM + M_WIDTH
C_OP = C_VM + M_WIDTH
C_GM = C_OP + M_WIDTH
C_GA = C_GM + D_MODEL
C_END = C_GA + D_MODEL


def _cparams(sem, vmem=VMEM_LIMIT):
    return pltpu.CompilerParams(dimension_semantics=sem, vmem_limit_bytes=vmem)


def _const_spec(shape):
    nd = len(shape)
    return pl.BlockSpec(shape, lambda *a: (0,) * nd)


def _layer_norm_rows(v, g, b):
    mu = jnp.mean(v, axis=-1, keepdims=True)
    d = v - mu
    var = jnp.mean(d * d, axis=-1, keepdims=True)
    return d * lax.rsqrt(var + EPS) * g + b


def _proj_kernel(x_ref, pos_ref, w_ref, gq_ref, gkv_ref, wuq_ref, wuqs_ref, wuk_ref, wuv_ref, invf_ref,
                 q_ref, k_ref, v_ref, xm_ref, vm_ref, op_ref, gm_ref, ga_ref, gate_ref):
    xb = x_ref[...].astype(BF16)

    def mm(lo, n):
        return jnp.dot(xb, w_ref[:, lo:lo + n], preferred_element_type=F32)

    xm_ref[...] = mm(C_XM, M_WIDTH).astype(BF16)
    vm_ref[...] = mm(C_VM, M_WIDTH).astype(BF16)
    op_ref[...] = mm(C_OP, M_WIDTH).astype(BF16)
    gm_ref[...] = mm(C_GM, D_MODEL).astype(BF16)
    ga_ref[...] = mm(C_GA, D_MODEL).astype(BF16)
    gate_ref[...] = mm(C_GATE, LANES)

    ang = pos_ref[...].astype(F32) * invf_ref[...]
    cos = jnp.cos(ang)
    sin = jnp.sin(ang)

    q_lat = mm(C_QLAT, A_Q_RANK)
    qn = (q_lat * lax.rsqrt(jnp.mean(q_lat * q_lat, axis=-1, keepdims=True) + EPS) * gq_ref[...]).astype(BF16)
    q = jnp.dot(qn, wuq_ref[...], preferred_element_type=F32)
    qs = jnp.dot(qn, wuqs_ref[...], preferred_element_type=F32)
    kv_lat = mm(C_KVLAT, A_KV_RANK)
    kvn = (kv_lat * lax.rsqrt(jnp.mean(kv_lat * kv_lat, axis=-1, keepdims=True) + EPS) * gkv_ref[...]).astype(BF16)
    kn = jnp.dot(kvn, wuk_ref[...], preferred_element_type=F32)
    lane = lax.broadcasted_iota(I32, (1, A_HEADS * LANES), 1)
    ones_lane = (lane % LANES == A_VDIM).astype(F32)
    v_ref[...] = (jnp.dot(kvn, wuv_ref[...], preferred_element_type=F32) + ones_lane).astype(BF16)
    k_pe = mm(C_KR, LANES) * cos + mm(C_KRS, LANES) * sin
    scale = A_QK ** -0.5 * LOG2_E
    for h in range(A_HEADS):
        sl = slice(h * LANES, (h + 1) * LANES)
        q_ref[:, sl] = ((q[:, sl] * cos + qs[:, sl] * sin) * scale).astype(BF16)
        k_ref[:, sl] = (kn[:, sl] + k_pe).astype(BF16)


def _proj_weights(w_in, w_uq, w_ukv):
    o = IN_OFFSETS
    half = A_ROPE // 2
    w_q, w_kv, w_kr = w_in[:, o[0]:o[1]], w_in[:, o[1]:o[2]], w_in[:, o[2]:o[3]]
    w_xm, w_vm, w_op = w_in[:, o[3]:o[4]], w_in[:, o[4]:o[5]], w_in[:, o[5]:o[6]]
    w_i, w_f, w_gm, w_ga = w_in[:, o[6]:o[7]], w_in[:, o[7]:o[8]], w_in[:, o[8]:o[9]], w_in[:, o[9]:o[10]]
    d = w_in.shape[0]
    z = lambda n: jnp.zeros((d, n), w_in.dtype)
    kr = jnp.concatenate([z(A_NOPE), w_kr, z(LANES - A_QK)], axis=1)
    krs = jnp.concatenate([z(A_NOPE), -w_kr[:, half:], w_kr[:, :half], z(LANES - A_QK)], axis=1)
    gate = jnp.concatenate([w_i, w_f, z(LANES - 2 * M_HEADS)], axis=1)
    w_r = jnp.concatenate([w_q, w_kv, kr, krs, gate, w_xm, w_vm, w_op, w_gm, w_ga], axis=1).astype(BF16)

    uq = w_uq.reshape(A_Q_RANK, A_HEADS, A_QK)
    zq = jnp.zeros((A_Q_RANK, A_HEADS, LANES - A_QK), w_uq.dtype)
    zn = jnp.zeros((A_Q_RANK, A_HEADS, A_NOPE), w_uq.dtype)
    uq_pad = jnp.concatenate([uq, zq], axis=-1).reshape(A_Q_RANK, A_HEADS * LANES).astype(BF16)
    uqs_pad = jnp.concatenate([zn, -uq[..., A_NOPE + half:], uq[..., A_NOPE:A_NOPE + half], zq],
                              axis=-1).reshape(A_Q_RANK, A_HEADS * LANES).astype(BF16)
    ukv = w_ukv.reshape(A_KV_RANK, A_HEADS, A_NOPE + A_VDIM)
    zk = jnp.zeros((A_KV_RANK, A_HEADS, LANES - A_NOPE), w_ukv.dtype)
    uk_pad = jnp.concatenate([ukv[..., :A_NOPE], zk], axis=-1).reshape(A_KV_RANK, A_HEADS * LANES).astype(BF16)
    zv = jnp.zeros((A_KV_RANK, A_HEADS, LANES - A_VDIM), w_ukv.dtype)
    uv_pad = jnp.concatenate([ukv[..., A_NOPE:], zv], axis=-1).reshape(A_KV_RANK, A_HEADS * LANES).astype(BF16)
    return w_r, uq_pad, uqs_pad, uk_pad, uv_pad


def _input_projection(x2, pos2, w_in, g_qlat, g_kvlat, w_uq, w_ukv):
    t = x2.shape[0]
    tm = min(PROJ_ROWS, t)
    w_r, uq_pad, uqs_pad, uk_pad, uv_pad = _proj_weights(w_in, w_uq, w_ukv)
    half = A_ROPE // 2
    inv_freq = ROPE_THETA ** (-jnp.arange(half, dtype=F32) / half)
    invf = jnp.concatenate([jnp.zeros((A_NOPE,), F32), inv_freq, inv_freq,
                            jnp.zeros((LANES - A_QK,), F32)]).reshape(1, LANES)
    hw = A_HEADS * LANES
    row = lambda n: pl.BlockSpec((tm, n), lambda i: (i, 0))
    outs = [
        jax.ShapeDtypeStruct((t, hw), BF16), jax.ShapeDtypeStruct((t, hw), BF16), jax.ShapeDtypeStruct((t, hw), BF16),
        jax.ShapeDtypeStruct((t, M_WIDTH), BF16), jax.ShapeDtypeStruct((t, M_WIDTH), BF16),
        jax.ShapeDtypeStruct((t, M_WIDTH), BF16),
        jax.ShapeDtypeStruct((t, D_MODEL), BF16), jax.ShapeDtypeStruct((t, D_MODEL), BF16),
        jax.ShapeDtypeStruct((t, LANES), F32),
    ]
    return pl.pallas_call(
        _proj_kernel,
        out_shape=outs,
        grid=(t // tm,),
        in_specs=[row(D_MODEL), row(1), _const_spec(w_r.shape), _const_spec((1, A_Q_RANK)),
                  _const_spec((1, A_KV_RANK)), _const_spec(uq_pad.shape), _const_spec(uqs_pad.shape),
                  _const_spec(uk_pad.shape), _const_spec(uv_pad.shape), _const_spec((1, LANES))],
        out_specs=[row(hw), row(hw), row(hw), row(M_WIDTH), row(M_WIDTH), row(M_WIDTH), row(D_MODEL), row(D_MODEL),
                   row(LANES)],
        compiler_params=_cparams(("parallel",)),
        name="input_projection",
    )(x2, pos2, w_r, g_qlat.reshape(1, -1), g_kvlat.reshape(1, -1), uq_pad, uqs_pad, uk_pad, uv_pad, invf)


def _log_sigmoid(v):
    return jnp.minimum(v, 0.0) - jnp.log1p(jnp.exp(-jnp.abs(v)))


def _mlstm_kernel(xm_ref, vm_ref, op_ref, gate_ref, convw_ref, convb_ref, wqk_ref, gbias_ref, ghead_ref,
                  skip_ref, tril_ref, out_ref, xpad_s, ctn_s, m_s, *, seq_len):
    s = seq_len
    lc = M_CHUNK_ROWS
    halo = 8
    for bb in range(MLSTM_SEQS):
        xpad_s[bb, 0:halo, :] = jnp.zeros((halo, M_WIDTH), F32)
        xpad_s[bb, halo:, :] = xm_ref[bb * s:(bb + 1) * s, :].astype(F32)
    ctn_s[...] = jnp.zeros_like(ctn_s)
    m_s[...] = jnp.zeros_like(m_s)
    rows = lax.broadcasted_iota(I32, (lc, lc), 0)
    cols = lax.broadcasted_iota(I32, (lc, lc), 1)
    causal = rows >= cols
    row_id = lax.broadcasted_iota(I32, (lc, LANES), 0)
    ones_blk = (lax.broadcasted_iota(I32, (lc, LANES), 1) == 0).astype(BF16)
    kscale = M_HEAD_DIM ** -0.5
    hsl = [slice(h * M_HEAD_DIM, (h + 1) * M_HEAD_DIM) for h in range(M_HEADS)]
    hcl = [slice(h, h + 1) for h in range(M_HEADS)]
    streams = [(bb, h) for bb in range(MLSTM_SEQS) for h in range(M_HEADS)]

    def gates(bb, r0):
        win = xpad_s[bb, pl.ds(r0 - bb * s, lc + halo), :]
        conv = convb_ref[...]
        for j in range(M_CONV):
            lo = halo - (M_CONV - 1) + j
            conv = conv + win[lo:lo + lc, :] * convw_ref[j:j + 1, :]
        xc = conv * jax.nn.sigmoid(conv)
        g = gate_ref[pl.ds(r0, lc), :] + gbias_ref[...]
        ls = _log_sigmoid(g)
        bc = jnp.dot(tril_ref[...], ls, preferred_element_type=F32, precision=lax.Precision.HIGHEST)
        b0 = pltpu.roll(bc, LANES - M_HEADS, axis=1)
        u = g - b0
        cm = u
        shift = 1
        while shift < lc:
            cm = jnp.maximum(cm, jnp.where(row_id >= shift, pltpu.roll(cm, shift, axis=0), -jnp.inf))
            shift *= 2
        m_prev = m_s[bb:bb + 1, :]
        g_inter = b0 + m_prev
        m = jnp.maximum(g_inter, b0 + cm)
        b_tot = b0[lc - 1:lc, :]
        aw = b_tot - b0 + g
        m_chunk = jnp.max(aw, axis=0, keepdims=True)
        m_new = jnp.maximum(b_tot + m_prev, m_chunk)
        m_s[bb:bb + 1, :] = m_new
        return dict(xc=xc, w_inter=jnp.exp(g_inter - m), e_negm=jnp.exp(-m), a_mat=b0 - m, wa=jnp.exp(aw - m_chunk),
                    s_old=jnp.exp(b_tot + m_prev - m_new), s_new=jnp.exp(m_chunk - m_new), u_t=u.T)

    def chunk(c, carry):
        r0s = [pl.multiple_of(bb * s + c * lc, lc) for bb in range(MLSTM_SEQS)]
        gs = [gates(bb, r0s[bb]) for bb in range(MLSTM_SEQS)]
        qk = [jnp.dot(gs[bb]["xc"].astype(BF16), wqk_ref[...], preferred_element_type=F32)
              for bb in range(MLSTM_SEQS)]
        q_b = [qk[bb][:, hsl[h]].astype(BF16) for bb, h in streams]
        k_f = [qk[bb][:, M_WIDTH + h * M_HEAD_DIM:M_WIDTH + (h + 1) * M_HEAD_DIM] * kscale for bb, h in streams]
        v_aug = [jnp.concatenate([vm_ref[pl.ds(r0s[bb], lc), hsl[h]], ones_blk], axis=1) for bb, h in streams]
        ctn_prev = [ctn_s[bb * M_HEADS + h] for bb, h in streams]
        s_raw = [lax.dot_general(q_b[i], k_f[i].astype(BF16), (((1,), (1,)), ((), ())), preferred_element_type=F32)
                 for i in range(len(streams))]
        inter = [jnp.dot(q_b[i], ctn_prev[i].astype(BF16), preferred_element_type=F32)
                 for i in range(len(streams))]
        ctn_c = [lax.dot_general((k_f[i] * gs[bb]["wa"][:, hcl[h]]).astype(BF16), v_aug[i], (((0,), (0,)), ((), ())),
                                 preferred_element_type=F32) for i, (bb, h) in enumerate(streams)]
        sc_b = [(s_raw[i] * jnp.exp(jnp.where(causal, gs[bb]["a_mat"][:, hcl[h]] + gs[bb]["u_t"][hcl[h], :],
                                              -jnp.inf))).astype(BF16) for i, (bb, h) in enumerate(streams)]
        intra = [jnp.dot(sc_b[i], v_aug[i], preferred_element_type=F32) for i in range(len(streams))]
        for i, (bb, h) in enumerate(streams):
            g = gs[bb]
            ctn_s[bb * M_HEADS + h] = g["s_old"][:, hcl[h]] * ctn_prev[i] + g["s_new"][:, hcl[h]] * ctn_c[i]
            wi = g["w_inter"][:, hcl[h]]
            num = wi * inter[i][:, :M_HEAD_DIM] + intra[i][:, :M_HEAD_DIM]
            den = wi * inter[i][:, M_HEAD_DIM:M_HEAD_DIM + 1] + intra[i][:, M_HEAD_DIM:M_HEAD_DIM + 1]
            hh = num / jnp.maximum(jnp.abs(den), g["e_negm"][:, hcl[h]])
            mu = jnp.mean(hh, axis=-1, keepdims=True)
            dv = hh - mu
            var = jnp.mean(dv * dv, axis=-1, keepdims=True)
            hn = dv * lax.rsqrt(var + EPS) * ghead_ref[:, hsl[h]]
            y = (jax.nn.sigmoid(op_ref[pl.ds(r0s[bb], lc), hsl[h]].astype(F32))
                 * (hn + skip_ref[:, hsl[h]] * g["xc"][:, hsl[h]]))
            out_ref[pl.ds(r0s[bb], lc), hsl[h]] = y.astype(BF16)
        return carry

    lax.fori_loop(0, s // lc, chunk, 0)


def _mlstm_branch(xm, vm, op, gate, bsz, s, b_i, b_f, conv_w, conv_b, w_mq, w_mk, g_mhead, w_mskip):
    lc = M_CHUNK_ROWS
    gbias = jnp.concatenate([b_i, b_f, jnp.zeros((LANES - 2 * M_HEADS,), F32)]).reshape(1, LANES)
    tril = jnp.tril(jnp.ones((lc, lc), F32))
    eye = jnp.eye(M_HEADS, dtype=F32)
    block_diag = lambda w: jnp.einsum('hde,hg->hdge', w, eye).reshape(M_WIDTH, M_WIDTH)
    w_qk = jnp.concatenate([block_diag(w_mq), block_diag(w_mk)], axis=1).astype(BF16)
    nseq = MLSTM_SEQS
    assert bsz % nseq == 0
    seq = lambda n: pl.BlockSpec((nseq * s, n), lambda b: (b, 0))
    return pl.pallas_call(
        functools.partial(_mlstm_kernel, seq_len=s),
        out_shape=jax.ShapeDtypeStruct((bsz * s, M_WIDTH), BF16),
        grid=(bsz // nseq,),
        in_specs=[seq(M_WIDTH), seq(M_WIDTH), seq(M_WIDTH), seq(LANES),
                  _const_spec((M_CONV, M_WIDTH)), _const_spec((1, M_WIDTH)),
                  _const_spec((M_WIDTH, 2 * M_WIDTH)),
                  _const_spec((1, LANES)), _const_spec((1, M_WIDTH)), _const_spec((1, M_WIDTH)),
                  _const_spec((lc, lc))],
        out_specs=seq(M_WIDTH),
        scratch_shapes=[pltpu.VMEM((nseq, s + 8, M_WIDTH), F32),
                        pltpu.VMEM((nseq * M_HEADS, M_HEAD_DIM, 2 * LANES), F32), pltpu.VMEM((8, LANES), F32)],
        compiler_params=_cparams(("parallel",)),
        name="mlstm_branch",
    )(xm, vm, op, gate, conv_w, conv_b.reshape(1, -1), w_qk, gbias, g_mhead.reshape(1, -1), w_mskip.reshape(1, -1),
      tril)


def _attn_kernel(q_ref, k_ref, v_ref, o_ref, s_scr, m_s, acc_s):
    tq = q_ref.shape[0]
    i = pl.program_id(1)
    rows = lax.broadcasted_iota(I32, (tq, tq), 0)
    cols = lax.broadcasted_iota(I32, (tq, tq), 1)

    def lane_tile_max(sc):
        out = sc[:, :LANES]
        for t in range(1, tq // LANES):
            out = jnp.maximum(out, sc[:, t * LANES:(t + 1) * LANES])
        return out

    heads = [slice(h * LANES, (h + 1) * LANES) for h in range(A_HEADS)]
    m_s[...] = jnp.full(m_s.shape, NEG_BIG, F32)
    acc_s[...] = jnp.zeros_like(acc_s)

    def pass_a(j, diagonal):
        k0 = pl.multiple_of(j * tq, tq)
        for h, hs in enumerate(heads):
            sc = lax.dot_general(q_ref[:, hs], k_ref[pl.ds(k0, tq), hs], (((1,), (1,)), ((), ())),
                                 preferred_element_type=F32)
            if diagonal:
                sc = jnp.where(cols <= rows, sc, NEG_BIG)
            s_scr[h, j] = sc
            m_s[h] = jnp.maximum(m_s[h], lane_tile_max(sc))

    def for_blocks(n, fn):
        def quad(p, c):
            for u in range(4):
                fn(4 * p + u)
            return c

        lax.fori_loop(0, n // 4, quad, 0)
        rem = n % 4

        @pl.when(rem >= 2)
        def _():
            fn(n - rem)
            fn(n - rem + 1)

        @pl.when(rem % 2 == 1)
        def _():
            fn(n - 1)

    for_blocks(i, lambda j: pass_a(j, False))
    pass_a(i, True)
    for h in range(A_HEADS):
        m_s[h] = jnp.broadcast_to(jnp.max(m_s[h], axis=1, keepdims=True), (tq, LANES))

    def pass_b(j):
        k0 = pl.multiple_of(j * tq, tq)
        for h, hs in enumerate(heads):
            m_row = m_s[h]
            p = jnp.exp2(s_scr[h, j] - jnp.concatenate([m_row] * (tq // LANES), axis=1)).astype(BF16)
            acc_s[h] += jnp.dot(p, v_ref[pl.ds(k0, tq), hs], preferred_element_type=F32)

    for_blocks(i + 1, pass_b)
    for h, hs in enumerate(heads):
        acc = acc_s[h]
        o_ref[:, hs] = (acc / acc[:, A_VDIM:A_VDIM + 1]).astype(BF16)


def _mla_attention(q, k, v, bsz, s):
    tq = min(ATT_Q, s)
    nq = s // tq
    hw = A_HEADS * LANES
    return pl.pallas_call(
        _attn_kernel,
        out_shape=jax.ShapeDtypeStruct(q.shape, BF16),
        grid=(bsz, nq),
        in_specs=[pl.BlockSpec((tq, hw), lambda b, i: (b * nq + i, 0)),
                  pl.BlockSpec((s, hw), lambda b, i: (b, 0)),
                  pl.BlockSpec((s, hw), lambda b, i: (b, 0))],
        out_specs=pl.BlockSpec((tq, hw), lambda b, i: (b * nq + i, 0)),
        scratch_shapes=[pltpu.VMEM((A_HEADS, nq, tq, tq), F32), pltpu.VMEM((A_HEADS, tq, LANES), F32),
                        pltpu.VMEM((A_HEADS, tq, LANES), F32)],
        compiler_params=_cparams(("parallel", "arbitrary")),
        name="mla_attention",
    )(q, k, v)


def _merge_kernel(ym_ref, oa_ref, gm_ref, ga_ref, x_ref, wbm_ref, wba_ref, wmix_ref, g_ref, b_ref, h_ref):
    y_m = jnp.dot(ym_ref[...], wbm_ref[...], preferred_element_type=F32)
    y_a = jnp.dot(oa_ref[...], wba_ref[...], preferred_element_type=F32)
    mixed = jax.nn.sigmoid(gm_ref[...].astype(F32)) * y_m + jax.nn.sigmoid(ga_ref[...].astype(F32)) * y_a
    mix = jnp.dot(mixed.astype(BF16), wmix_ref[...], preferred_element_type=F32)
    h_ref[...] = _layer_norm_rows(DN_ALPHA * x_ref[...] + mix, g_ref[...], b_ref[...])


def _merge(ym, oa, gm, ga, x2, w_br_m, w_br_a, w_mix_out, ln_g, ln_b):
    t = x2.shape[0]
    tm = min(MERGE_ROWS, t)
    wba = jnp.concatenate([w_br_a.reshape(A_HEADS, A_VDIM, D_MODEL),
                           jnp.zeros((A_HEADS, LANES - A_VDIM, D_MODEL), w_br_a.dtype)], axis=1)
    wba = wba.reshape(A_HEADS * LANES, D_MODEL).astype(BF16)
    row = lambda n: pl.BlockSpec((tm, n), lambda i: (i, 0))
    return pl.pallas_call(
        _merge_kernel,
        out_shape=jax.ShapeDtypeStruct((t, D_MODEL), F32),
        grid=(t // tm,),
        in_specs=[row(M_WIDTH), row(A_HEADS * LANES), row(D_MODEL), row(D_MODEL), row(D_MODEL),
                  _const_spec((M_WIDTH, D_MODEL)), _const_spec((A_HEADS * LANES, D_MODEL)),
                  _const_spec((D_MODEL, D_MODEL)), _const_spec((1, D_MODEL)), _const_spec((1, D_MODEL))],
        out_specs=row(D_MODEL),
        compiler_params=_cparams(("parallel",)),
        name="merge_deepnorm1",
    )(ym, oa, gm, ga, x2, w_br_m.astype(BF16), wba, w_mix_out.astype(BF16), ln_g.reshape(1, -1), ln_b.reshape(1, -1))


def _kv_kernel(m_ref, w_ref, o_ref):
    o_ref[...] = jnp.dot(m_ref[...].astype(BF16), w_ref[...], preferred_element_type=F32).astype(BF16)


def _mem_kv(mem2, w_ckv):
    r = mem2.shape[0]
    tm = min(512, r)
    return pl.pallas_call(
        _kv_kernel,
        out_shape=jax.ShapeDtypeStruct((r, 2 * D_MODEL), BF16),
        grid=(r // tm,),
        in_specs=[pl.BlockSpec((tm, D_MODEL), lambda i: (i, 0)), _const_spec((D_MODEL, 2 * D_MODEL))],
        out_specs=pl.BlockSpec((tm, 2 * D_MODEL), lambda i: (i, 0)),
        compiler_params=_cparams(("parallel",)),
        name="memory_kv",
    )(mem2, w_ckv.astype(BF16))


def _xattn_kernel(h_ref, kv_ref, wq_ref, wo_ref, g_ref, b_ref, wr_ref, br_ref, *rest):
    h2_ref, idx_ref, gate_ref = rest[-3:]
    h1 = h_ref[...]
    q = jnp.dot(h1.astype(BF16), wq_ref[...], preferred_element_type=F32).astype(BF16)
    scale = X_HEAD_DIM ** -0.5
    ksl = [slice(hd * X_HEAD_DIM, (hd + 1) * X_HEAD_DIM) for hd in range(X_HEADS)]
    vsl = [slice(D_MODEL + hd * X_HEAD_DIM, D_MODEL + (hd + 1) * X_HEAD_DIM) for hd in range(X_HEADS)]
    scs = [lax.dot_general(q[:, ks], kv_ref[:, ks], (((1,), (1,)), ((), ())), preferred_element_type=F32) * scale
           for ks in ksl]
    ps = [jnp.exp(sc - jnp.max(sc, axis=1, keepdims=True)) for sc in scs]
    ps = [(p / jnp.sum(p, axis=1, keepdims=True)).astype(BF16) for p in ps]
    o = jnp.concatenate([jnp.dot(p, kv_ref[:, vs], preferred_element_type=F32).astype(BF16)
                         for p, vs in zip(ps, vsl)], axis=1)
    att = jnp.dot(o, wo_ref[...], preferred_element_type=F32)
    h2 = _layer_norm_rows(DN_ALPHA * h1 + att, g_ref[...], b_ref[...])
    for c in range(ROW_TILES):
        h2_ref[pl.ds(c, h2.shape[0], stride=ROW_TILES), :] = h2[:, c * LANES:(c + 1) * LANES]

    h2_hi = h2.astype(BF16)
    h2_lo = (h2 - h2_hi.astype(F32)).astype(BF16)
    hi_prod = jnp.dot(h2_hi, wr_ref[...], preferred_element_type=F32)
    lo_prod = jnp.dot(h2_lo, wr_ref[:, :LANES], preferred_element_type=F32)
    logits = hi_prod[:, :LANES] + (hi_prod[:, LANES:] + lo_prod) + br_ref[...]
    tm = logits.shape[0]
    lane = lax.broadcasted_iota(I32, (tm, LANES), 1)
    lane_f = lane.astype(F32)
    work = jnp.where(lane < N_EXPERTS, logits, -jnp.inf)
    vals, idxs = [], []
    for _ in range(TOP_K):
        mx = jnp.max(work, axis=1, keepdims=True)
        ix = jnp.min(jnp.where(work == mx, lane_f, float(LANES)), axis=1, keepdims=True)
        vals.append(mx)
        idxs.append(ix)
        work = jnp.where(lane_f == ix, -jnp.inf, work)
    es = [jnp.exp(vv - vals[0]) for vv in vals]
    tot = es[0] + es[1] + es[2] + es[3]
    idx_slab = jnp.zeros((tm, LANES), F32)
    gate_slab = jnp.zeros((tm, LANES), F32)
    for kk in range(TOP_K):
        idx_slab = jnp.where(lane == kk, idxs[kk], idx_slab)
        gate_slab = jnp.where(lane == kk, es[kk] / tot, gate_slab)
    idx_ref[...] = idx_slab.astype(I32)
    gate_ref[...] = gate_slab


def _cross_attention_router(h1, kv, n_mem, b0, bsz, s, w_cq, w_co, ln_g, ln_b, w_router, b_router, after):
    t = bsz * s
    tm = min(XATT_ROWS, s)
    ns = s // tm
    wr = jnp.concatenate([w_router, jnp.zeros((D_MODEL, LANES - N_EXPERTS), F32)], axis=1)
    wr_hi = wr.astype(BF16)
    wr = jnp.concatenate([wr_hi, (wr - wr_hi.astype(F32)).astype(BF16)], axis=1)
    br = jnp.concatenate([b_router, jnp.zeros((LANES - N_EXPERTS,), F32)]).reshape(1, LANES)
    row = lambda n: pl.BlockSpec((tm, n), lambda b, i: (b * ns + i, 0))
    h1_rows = pl.BlockSpec((tm, D_MODEL), lambda b, i: ((b0 + b) * ns + i, 0))
    in_specs = [h1_rows, pl.BlockSpec((n_mem, 2 * D_MODEL), lambda b, i: (b0 + b, 0)),
                _const_spec((D_MODEL, D_MODEL)), _const_spec((D_MODEL, D_MODEL)),
                _const_spec((1, D_MODEL)), _const_spec((1, D_MODEL)),
                _const_spec((D_MODEL, 2 * LANES)), _const_spec((1, LANES))]
    args = [h1, kv, w_cq.astype(BF16), w_co.astype(BF16), ln_g.reshape(1, -1), ln_b.reshape(1, -1), wr, br]
    if after is not None:
        in_specs.append(pl.BlockSpec(memory_space=pl.ANY))
        args.append(after)
    return pl.pallas_call(
        _xattn_kernel,
        out_shape=[jax.ShapeDtypeStruct((t * ROW_TILES, LANES), F32), jax.ShapeDtypeStruct((t, LANES), I32),
                   jax.ShapeDtypeStruct((t, LANES), F32)],
        grid=(bsz, ns),
        in_specs=in_specs,
        out_specs=[pl.BlockSpec((tm * ROW_TILES, LANES), lambda b, i: (b * ns + i, 0)), row(LANES), row(LANES)],
        compiler_params=_cparams(("parallel", "parallel")),
        name="cross_attention_router",
    )(*args)


def _rank_kernel(idx_ref, ltri_ref, rank_ref, cnt_ref, carry_s):
    @pl.when(pl.program_id(0) == 0)
    def _():
        carry_s[...] = jnp.zeros_like(carry_s)

    idx = idx_ref[...]
    tm = idx.shape[0]
    lane = lax.broadcasted_iota(I32, (tm, LANES), 1)
    onehots = [(lane == idx[:, kk:kk + 1]).astype(F32) for kk in range(TOP_K)]
    sel = onehots[0] + onehots[1] + onehots[2] + onehots[3]
    before = jnp.dot(ltri_ref[...], sel.astype(BF16), preferred_element_type=F32) + carry_s[0:1, :]
    rank_slab = jnp.zeros((tm, LANES), F32)
    for kk in range(TOP_K):
        r = jnp.sum(onehots[kk] * before, axis=1, keepdims=True)
        rank_slab = jnp.where(lane == kk, r, rank_slab)
    rank_ref[...] = rank_slab.astype(I32)
    carry_s[0:1, :] = carry_s[0:1, :] + jnp.sum(sel, axis=0, keepdims=True)
    cnt_ref[...] = jnp.broadcast_to(carry_s[0:1, :], cnt_ref.shape).astype(I32)


def _routing_ranks(idx_slab):
    t = idx_slab.shape[0]
    tm = min(RANK_ROWS, t)
    assert t % tm == 0
    ltri = jnp.tril(jnp.ones((tm, tm), BF16), k=-1)
    return pl.pallas_call(
        _rank_kernel,
        out_shape=[jax.ShapeDtypeStruct((t, LANES), I32), jax.ShapeDtypeStruct((8, LANES), I32)],
        grid=(t // tm,),
        in_specs=[pl.BlockSpec((tm, LANES), lambda i: (i, 0)), _const_spec((tm, tm))],
        out_specs=[pl.BlockSpec((tm, LANES), lambda i: (i, 0)), _const_spec((8, LANES))],
        scratch_shapes=[pltpu.VMEM((8, LANES), F32)],
        compiler_params=_cparams(("arbitrary",)),
        name="routing_ranks",
    )(idx_slab, ltri)


def _dest_kernel(idx_ref, rank_ref, pstart_ref, dest_ref):
    idx = idx_ref[...]
    tm = idx.shape[0]
    lane = lax.broadcasted_iota(I32, (tm, LANES), 1)
    dest = rank_ref[...].astype(F32)
    pstart = pstart_ref[0:1, :].astype(F32)
    for kk in range(TOP_K):
        start = jnp.sum(jnp.where(lane == idx[:, kk:kk + 1], pstart, 0.0), axis=1, keepdims=True)
        dest = dest + jnp.where(lane == kk, start, 0.0)
    dest_ref[...] = dest.T[0:8, :].astype(I32)


def _dest_rows(idx_slab, rank_slab, pstart_row):
    t = rank_slab.shape[0]
    tm = min(RANK_ROWS, t)
    return pl.pallas_call(
        _dest_kernel,
        out_shape=jax.ShapeDtypeStruct((8, t), I32),
        grid=(t // tm,),
        in_specs=[pl.BlockSpec((tm, LANES), lambda i: (i, 0)), pl.BlockSpec((tm, LANES), lambda i: (i, 0)),
                  _const_spec((8, LANES))],
        out_specs=pl.BlockSpec((8, tm), lambda i: (0, i)),
        compiler_params=_cparams(("parallel",)),
        name="routing_dest",
    )(idx_slab, rank_slab, pstart_row)


def _sc_mesh():
    return plsc.VectorSubcoreMesh(core_axis_name="c", subcore_axis_name="s", num_cores=SC_CORES,
                                  num_subcores=SC_SUBCORES)


def _sc_worker():
    return lax.axis_index("s") * SC_CORES + lax.axis_index("c")


def _dispatch(h2, dest_kt, pad_rows, n_rows):
    t = dest_kt.shape[1]
    w = SC_ROWS
    per_w = t // SC_WORKERS
    steps = per_w // w
    assert steps % 2 == 0 and steps * w * SC_WORKERS == t
    pad_steps = pad_rows.shape[0] // (SC_WORKERS * w)
    dest_w = dest_kt.reshape(TOP_K, SC_WORKERS, steps, w)
    pad_w = pad_rows.reshape(SC_WORKERS, pad_steps, w)
    zeros = jnp.zeros((w, ROW_TILES, LANES), F32)

    def body(x_hbm, dest_hbm, pad_hbm, zeros_hbm, o_hbm, idx_v, pad_v, rows_v, sem_in, sem_out):
        wid = _sc_worker()
        tok0 = wid * per_w
        for kk in range(TOP_K):
            pltpu.sync_copy(dest_hbm.at[kk, wid], idx_v.at[pl.ds(kk * steps, steps)])
        pltpu.sync_copy(pad_hbm.at[wid], pad_v)
        pltpu.sync_copy(zeros_hbm, rows_v.at[0])
        for j in range(pad_steps):
            pltpu.sync_copy(rows_v.at[0], o_hbm.at[pad_v.at[j]])

        def load(step, b):
            return pltpu.make_async_copy(x_hbm.at[pl.ds(tok0 + step * w, w)], rows_v.at[b], sem_in.at[b])

        def scatter(step, kk, b):
            return pltpu.make_async_copy(rows_v.at[b], o_hbm.at[idx_v.at[kk * steps + step]], sem_out.at[b])

        load(0, 0).start()

        @pl.loop(0, steps, step=2)
        def _(s0):
            for b in range(2):
                step = s0 + b
                load(step, b).wait()

                @pl.when(step >= 1)
                def _():
                    for kk in range(TOP_K):
                        scatter(step - 1, kk, 1 - b).wait()

                @pl.when(step + 1 < steps)
                def _():
                    load(step + 1, 1 - b).start()

                for kk in range(TOP_K):
                    scatter(step, kk, b).start()

        for kk in range(TOP_K):
            scatter(steps - 1, kk, 1).wait()

    return pl.kernel(
        body,
        out_type=jax.ShapeDtypeStruct((n_rows, ROW_TILES, LANES), F32),
        mesh=_sc_mesh(),
        scratch_types=[pltpu.VMEM((steps * TOP_K, w), I32), pltpu.VMEM((pad_steps, w), I32),
                       pltpu.VMEM((2, w, ROW_TILES, LANES), F32), pltpu.SemaphoreType.DMA((2,)),
                       pltpu.SemaphoreType.DMA((2,))],
        name="moe_dispatch_sc",
    )(h2, dest_w, pad_w, zeros)


def _gather_rows(yb, dest_kt):
    n = dest_kt.shape[0]
    w = SC_ROWS
    per_w = n // SC_WORKERS
    steps = per_w // w
    assert steps % 2 == 0 and steps * w * SC_WORKERS == n
    dest_w = dest_kt.reshape(SC_WORKERS, steps, w)

    def body(y_hbm, dest_hbm, o_hbm, idx_v, rows_v, sem_g, sem_w):
        wid = _sc_worker()
        row0 = wid * per_w
        pltpu.sync_copy(dest_hbm.at[wid], idx_v)

        def gather(step, b):
            return pltpu.make_async_copy(y_hbm.at[idx_v.at[step]], rows_v.at[b], sem_g.at[b])

        def write(step, b):
            return pltpu.make_async_copy(rows_v.at[b], o_hbm.at[pl.ds(row0 + step * w, w)], sem_w.at[b])

        gather(0, 0).start()

        @pl.loop(0, steps, step=2)
        def _(s0):
            for b in range(2):
                step = s0 + b
                gather(step, b).wait()

                @pl.when(step >= 1)
                def _():
                    write(step - 1, 1 - b).wait()

                @pl.when(step + 1 < steps)
                def _():
                    gather(step + 1, 1 - b).start()

                write(step, b).start()

        write(steps - 1, 1).wait()

    return pl.kernel(
        body,
        out_type=jax.ShapeDtypeStruct((n, ROW_TILES, LANES), F32),
        mesh=_sc_mesh(),
        scratch_types=[pltpu.VMEM((steps, w), I32), pltpu.VMEM((2, w, ROW_TILES, LANES), F32),
                       pltpu.SemaphoreType.DMA((2,)), pltpu.SemaphoreType.DMA((2,))],
        name="moe_gather_sc",
    )(yb, dest_w)


def _rows_2d(ref, n):
    return jnp.concatenate([ref[pl.ds(c, n, stride=ROW_TILES), :] for c in range(ROW_TILES)], axis=1)


def _store_rows(ref, val):
    for c in range(ROW_TILES):
        ref[pl.ds(c, val.shape[0], stride=ROW_TILES), :] = val[:, c * LANES:(c + 1) * LANES]


def _expert_kernel(blk_e_ref, first_ref, slot_ref, next_ref, nblk_ref, x_ref, wg_hbm, wl_hbm, bg_ref, bl_ref, wd_hbm,
                   bd_ref, y_ref, wg_buf, wl_buf, wd_stage, wd_buf, sem):
    i = pl.program_id(0)

    def fetch(e, sl):
        return [pltpu.make_async_copy(src.at[e], dst.at[sl], sem.at[sl, j])
                for j, (src, dst) in enumerate(((wg_hbm, wg_buf), (wl_hbm, wl_buf), (wd_hbm, wd_stage)))]

    @pl.when(i < nblk_ref[0])
    def _():
        sl = slot_ref[i]

        @pl.when(first_ref[i] == 1)
        def _():
            @pl.when(i == 0)
            def _():
                for cp in fetch(blk_e_ref[0], sl):
                    cp.start()

            for cp in fetch(blk_e_ref[i], sl):
                cp.wait()
            wd_buf[sl] = wd_stage[sl].astype(BF16)

            @pl.when(next_ref[i] >= 0)
            def _():
                for cp in fetch(next_ref[i], 1 - sl):
                    cp.start()

        xb = _rows_2d(x_ref, y_ref.shape[0] // ROW_TILES).astype(BF16)
        glu = jnp.dot(xb, wg_buf[sl], preferred_element_type=F32) + bg_ref[...]
        lin = jnp.dot(xb, wl_buf[sl], preferred_element_type=F32) + bl_ref[...]
        glu = jnp.minimum(glu, SWIGLU_LIMIT)
        lin = jnp.clip(lin, -SWIGLU_LIMIT, SWIGLU_LIMIT)
        act = glu * jax.nn.sigmoid(SWIGLU_ALPHA * glu) * (lin + 1.0)
        _store_rows(y_ref, jnp.dot(act.astype(BF16), wd_buf[sl], preferred_element_type=F32) + bd_ref[...])

    @pl.when(i >= nblk_ref[0])
    def _():
        y_ref[...] = jnp.zeros_like(y_ref)


def _split_kernel(w_ref, perm_ref, after_ref, g_ref, l_ref):
    del after_ref
    half = g_ref.shape[-1]
    sorted_cols = jnp.dot(w_ref[...].astype(BF16), perm_ref[...], preferred_element_type=F32)
    g_ref[...] = sorted_cols[:, :half].astype(BF16)
    l_ref[...] = sorted_cols[:, half:].astype(BF16)


def _split_gate_up(w_gu, after):
    e, d, n2 = w_gu.shape
    cw = 512
    src = np.concatenate([np.arange(0, cw, 2), np.arange(1, cw, 2)])
    perm = jnp.asarray(np.eye(cw, dtype=np.float32)[:, src], BF16)
    return pl.pallas_call(
        _split_kernel,
        out_shape=[jax.ShapeDtypeStruct((e, d, n2 // 2), BF16), jax.ShapeDtypeStruct((e, d, n2 // 2), BF16)],
        grid=(e, n2 // cw),
        in_specs=[pl.BlockSpec((None, d, cw), lambda i, c: (i, 0, c)), _const_spec((cw, cw)),
                  pl.BlockSpec(memory_space=pl.ANY)],
        out_specs=[pl.BlockSpec((None, d, cw // 2), lambda i, c: (i, 0, c)),
                   pl.BlockSpec((None, d, cw // 2), lambda i, c: (i, 0, c))],
        compiler_params=_cparams(("parallel", "parallel")),
        name="split_gate_up",
    )(w_gu, perm, after)


def _experts(xb, blk_meta, w_glu, w_lin, b_glu, b_lin, w_dn, b_dn):
    n_rows = xb.shape[0] // ROW_TILES
    bm = EXPERT_ROWS
    vec = lambda n: pl.BlockSpec((None, 1, n), lambda i, be, *_: (be[i], 0, 0))
    rows = pl.BlockSpec((bm * ROW_TILES, LANES), lambda i, *_: (i, 0))
    hbm = pl.BlockSpec(memory_space=pl.ANY)
    grid_spec = pltpu.PrefetchScalarGridSpec(
        num_scalar_prefetch=5,
        grid=(n_rows // bm,),
        in_specs=[rows, hbm, hbm, vec(D_EXPERT), vec(D_EXPERT), hbm, vec(D_MODEL)],
        out_specs=rows,
        scratch_shapes=[pltpu.VMEM((2, D_MODEL, D_EXPERT), BF16), pltpu.VMEM((2, D_MODEL, D_EXPERT), BF16),
                        pltpu.VMEM((2, D_EXPERT, D_MODEL), F32), pltpu.VMEM((2, D_EXPERT, D_MODEL), BF16),
                        pltpu.SemaphoreType.DMA((2, 3))],
    )
    return pl.pallas_call(
        _expert_kernel,
        out_shape=jax.ShapeDtypeStruct((n_rows * ROW_TILES, LANES), F32),
        grid_spec=grid_spec,
        compiler_params=_cparams(("arbitrary",)),
        name="moe_experts",
    )(*blk_meta, xb, w_glu, w_lin, b_glu, b_lin, w_dn, b_dn)


def _combine_kernel(h_ref, gate_ref, y0_ref, y1_ref, y2_ref, y3_ref, g_ref, b_ref, *rest):
    o_ref = rest[-1]
    tt = o_ref.shape[0]
    gates = gate_ref[...]
    acc = DN_ALPHA * _rows_2d(h_ref, tt)
    for kk, y_ref in enumerate((y0_ref, y1_ref, y2_ref, y3_ref)):
        acc = acc + gates[:, kk:kk + 1] * _rows_2d(y_ref, tt)
    o_ref[...] = _layer_norm_rows(acc, g_ref[...], b_ref[...])


def _combine(h2, gate_slab, yg, ln_g, ln_b, tok0, t_all, prev_out):
    t = h2.shape[0] // ROW_TILES
    tt = min(COMBINE_TOKENS, t)
    steps = t // tt
    assert tok0 % tt == 0 and t % tt == 0
    first = tok0 // tt
    row = lambda m: pl.BlockSpec((tt, m), lambda i: (i, 0))
    tiles = lambda kk: pl.BlockSpec((tt * ROW_TILES, LANES), lambda i: (kk * steps + i, 0))
    in_specs = [tiles(0), row(LANES),
                tiles(0), tiles(1), tiles(2), tiles(3), _const_spec((1, D_MODEL)), _const_spec((1, D_MODEL))]
    args = [h2, gate_slab, yg, yg, yg, yg, ln_g.reshape(1, -1), ln_b.reshape(1, -1)]
    aliases = {}
    if prev_out is not None:
        in_specs.append(pl.BlockSpec(memory_space=pl.ANY))
        args.append(prev_out)
        aliases = {len(args) - 1: 0}
    return pl.pallas_call(
        _combine_kernel,
        out_shape=jax.ShapeDtypeStruct((t_all, D_MODEL), F32),
        grid=(steps,),
        in_specs=in_specs,
        out_specs=pl.BlockSpec((tt, D_MODEL), lambda i: (first + i, 0)),
        input_output_aliases=aliases,
        compiler_params=_cparams(("parallel",)),
        name="moe_combine",
    )(*args)


def _route_and_dispatch(h2, idx_slab):
    t = h2.shape[0] // ROW_TILES
    bm = EXPERT_ROWS
    n_rows = t * TOP_K + N_EXPERTS * bm
    n_blocks = n_rows // bm

    rank_slab, cnt = _routing_ranks(idx_slab)
    counts = cnt[0, :N_EXPERTS]
    padded = (counts + bm - 1) // bm * bm
    pends = jnp.cumsum(padded)
    pstarts = pends - padded
    pstart_row = jnp.zeros((8, LANES), I32).at[:, :N_EXPERTS].set(pstarts[None, :])
    blk_start = jnp.arange(n_blocks, dtype=I32) * bm
    blk_e = jnp.minimum(jnp.sum((pends[None, :] <= blk_start[:, None]).astype(I32), axis=1), N_EXPERTS - 1)
    nblk = (pends[-1] // bm).astype(I32).reshape(1)
    active = counts > 0
    ordinal = jnp.cumsum(active.astype(I32)) - 1
    eid = jnp.arange(N_EXPERTS, dtype=I32)
    later = active[None, :] & (eid[None, :] > eid[:, None])
    next_active = jnp.where(later.any(axis=1), jnp.argmax(later, axis=1).astype(I32), -1)
    blk_first = jnp.concatenate([jnp.ones((1,), I32), (blk_e[1:] != blk_e[:-1]).astype(I32)])
    blk_onehot = (blk_e[:, None] == eid[None, :]).astype(I32)
    blk_slot = jnp.sum(blk_onehot * ordinal[None, :], axis=1) % 2
    blk_next = jnp.sum(blk_onehot * next_active[None, :], axis=1)
    blk_meta = (blk_e, blk_first, blk_slot, blk_next, nblk)
    dest_kt = _dest_rows(idx_slab, rank_slab, pstart_row)[:TOP_K]
    pad = padded - counts
    spare_base = pends[-1] + jnp.cumsum(bm - pad) - (bm - pad)
    pad_off = jnp.arange(bm, dtype=I32)[None, :]
    pad_rows = jnp.where(pad_off < pad[:, None], (pstarts + counts)[:, None] + pad_off,
                         spare_base[:, None] + pad_off - pad[:, None]).reshape(-1)
    xb = _dispatch(h2.reshape(t, ROW_TILES, LANES), dest_kt, pad_rows, n_rows)
    return xb.reshape(n_rows * ROW_TILES, LANES), blk_meta, dest_kt


def _part_batches(bsz):
    if bsz % 8 == 0:
        return tuple(share * bsz // 8 for share in MOE_PART_SHARES)
    return (bsz // 2, bsz - bsz // 2)


def _xattn_moe(h1, kv, bsz, s, w_cq, w_co, ln2_g, ln2_b, w_router, b_router, w_gu, b_gu, w_dn, b_dn, ln3_g, ln3_b):
    parts = []
    after = None
    n_mem = kv.shape[0] // bsz
    b0 = 0
    for nb in _part_batches(bsz):
        h2, idx_slab, gate_slab = _cross_attention_router(h1, kv, n_mem, b0, nb, s, w_cq, w_co, ln2_g, ln2_b,
                                                          w_router, b_router, after)
        b0 += nb
        parts.append((h2, gate_slab) + _route_and_dispatch(h2, idx_slab))
        after = parts[-1][-1]
    w_glu, w_lin = _split_gate_up(w_gu, after)
    expert_w = (w_glu, w_lin, b_gu[:, None, 0::2], b_gu[:, None, 1::2], w_dn, b_dn[:, None, :])
    out = None
    tok0 = 0
    for h2, gate_slab, xb, blk_meta, dest_kt in parts:
        n_rows = xb.shape[0] // ROW_TILES
        yb = _experts(xb, blk_meta, *expert_w)
        yg = _gather_rows(yb.reshape(n_rows, ROW_TILES, LANES), dest_kt.reshape(-1))
        out = _combine(h2, gate_slab, yg.reshape(-1, LANES), ln3_g, ln3_b, tok0, bsz * s, out)
        tok0 += h2.shape[0] // ROW_TILES
    return out


def kernel(x, mem, positions, w_in, b_igate, b_fgate, conv_w, conv_b, w_mq, w_mk, g_mhead, w_mskip, g_qlat, g_kvlat,
           w_uq, w_ukv, w_br_m, w_br_a, w_mix_out, ln1_g, ln1_b, w_cq, w_ckv, w_co, ln2_g, ln2_b, w_router, b_router,
           w_gu, b_gu, w_dn, b_dn, ln3_g, ln3_b):
    bsz, s, d = x.shape
    t = bsz * s
    h = x.reshape(t, d)
    pos2 = positions.reshape(t, 1)
    for l in range(DEPTH):
        q, k, v, xm, vm, op, gm, ga, gate = _input_projection(h, pos2, w_in[l], g_qlat[l], g_kvlat[l], w_uq[l],
                                                              w_ukv[l])
        ym = _mlstm_branch(xm, vm, op, gate, bsz, s, b_igate[l], b_fgate[l], conv_w[l], conv_b[l], w_mq[l], w_mk[l],
                           g_mhead[l], w_mskip[l])
        oa = _mla_attention(q, k, v, bsz, s)
        h1 = _merge(ym, oa, gm, ga, h, w_br_m[l], w_br_a[l], w_mix_out[l], ln1_g[l], ln1_b[l])
        kv = _mem_kv(mem.reshape(-1, d), w_ckv[l])
        h = _xattn_moe(h1, kv, bsz, s, w_cq[l], w_co[l], ln2_g[l], ln2_b[l], w_router[l], b_router[l], w_gu[l],
                       b_gu[l], w_dn[l], b_dn[l], ln3_g[l], ln3_b[l])
    return h.reshape(bsz, s, d)
```
